```python
import math
import jax, jax.numpy as jnp
from jax import lax
import numpy as np

D_MODEL = 1024
BATCH = 1
SEQ = 16384
DEPTH = 1

GRID_W = 64
CTX_LEN = 256
S5_GROUP = 16
S5_GROUPS = 32
S5_WIDTH = S5_GROUP * S5_GROUPS
S5_STATE = 64
DT_MIN = 0.001
DT_MAX = 0.1
FFT_GROUPS = 4
FFT_GROUP_DIM = 128
FFT_WIDTH = FFT_GROUPS * FFT_GROUP_DIM
N_BRANCHES = 2
IN_WIDTH = S5_WIDTH + FFT_WIDTH + N_BRANCHES * D_MODEL
N_EXPERTS = 32
TOP_K = 4
D_EXPERT = D_MODEL
SWIGLU_ALPHA = 1.702
SWIGLU_LIMIT = 7.0
MOE_BLOCK = 128
LN_EPS = 1e-5
DEEPNORM_ALPHA = (2.0 * DEPTH) ** 0.25
DEEPNORM_BETA = (8.0 * DEPTH) ** -0.25

kernel_name = "s5_fnet_gated_moe_deepnorm_block"

F32 = jnp.float32


def layer_norm(x, g, b):
    xf = x.astype(F32)
    mu = jnp.mean(xf, axis=-1, keepdims=True)
    var = jnp.mean(jnp.square(xf - mu), axis=-1, keepdims=True)
    return ((xf - mu) * lax.rsqrt(var + LN_EPS) * g.astype(F32) + b.astype(F32)).astype(x.dtype)


def modulate(x, shift, scale):
    return x * (1 + scale) + shift


def sincos_2d(n_tokens, dtype):
    rows = n_tokens // GRID_W
    r, col = jnp.meshgrid(jnp.arange(rows, dtype=F32), jnp.arange(GRID_W, dtype=F32), indexing='ij')
    q = D_MODEL // 4
    omega = 1.0 / (10000.0 ** (jnp.arange(q, dtype=F32) / q))

    def emb(pos):
        ang = pos.reshape(-1)[:, None] * omega[None, :]
        return jnp.concatenate([jnp.sin(ang), jnp.cos(ang)], axis=-1)

    return jnp.concatenate([emb(r), emb(col)], axis=-1).astype(dtype)


def s5_discretise(lam_re, lam_im, log_dt, b_re, b_im):
    lam_re = lam_re.astype(F32)
    lam_im = lam_im.astype(F32)
    dt = jnp.exp(log_dt.astype(F32))[..., None]
    mag = jnp.exp(lam_re * dt)
    a_re = mag * jnp.cos(lam_im * dt)
    a_im = mag * jnp.sin(lam_im * dt)
    den = lam_re * lam_re + lam_im * lam_im
    num_re = a_re - 1.0
    k_re = (num_re * lam_re + a_im * lam_im) / den
    k_im = (a_im * lam_re - num_re * lam_im) / den
    b_re = b_re.astype(F32)
    b_im = b_im.astype(F32)
    bb_re = k_re[..., None] * b_re - k_im[..., None] * b_im
    bb_im = k_re[..., None] * b_im + k_im[..., None] * b_re
    return a_re, a_im, bb_re, bb_im


def complex_affine_combine(e1, e2):
    a1r, a1i, b1r, b1i = e1
    a2r, a2i, b2r, b2i = e2
    return (a2r * a1r - a2i * a1i,
            a2r * a1i + a2i * a1r,
            a2r * b1r - a2i * b1i + b2r,
            a2r * b1i + a2i * b1r + b2i)


def s5_direction(bu_re, bu_im, a_re, a_im, s0_re, s0_im, reverse):
    ar = jnp.broadcast_to(a_re, bu_re.shape)
    ai = jnp.broadcast_to(a_im, bu_re.shape)
    cr, ci, sr, si = lax.associative_scan(complex_affine_combine, (ar, ai, bu_re, bu_im),
                                          reverse=reverse, axis=1)
    s_re = sr + cr * s0_re[:, None] - ci * s0_im[:, None]
    s_im = si + cr * s0_im[:, None] + ci * s0_re[:, None]
    return s_re, s_im


def s5_states(u, s0, a_re, a_im, bb_re, bb_im):
    bsz, n, _ = u.shape
    ug = u.astype(F32).reshape(bsz, n, S5_GROUPS, S5_GROUP)
    states = []
    for d, reverse in ((0, False), (1, True)):
        bu_re = jnp.einsum('gph,blgh->blgp', bb_re[d], ug)
        bu_im = jnp.einsum('gph,blgh->blgp', bb_im[d], ug)
        states.append(s5_direction(bu_re, bu_im, a_re[d], a_im[d], s0[d][0], s0[d][1], reverse))
    return states


def s5_readout(states, u, c_re, c_im, d_skip):
    bsz, n, _ = u.shape
    y = u.astype(F32) * d_skip.astype(F32)
    for d, (s_re, s_im) in enumerate(states):
        yd = (jnp.einsum('ghp,blgp->blgh', c_re[d].astype(F32), s_re)
              - jnp.einsum('ghp,blgp->blgh', c_im[d].astype(F32), s_im))
        y = y + yd.reshape(bsz, n, S5_WIDTH)
    return y.astype(u.dtype)


def s5_branch(y, w_glu, b_glu, w_br):
    z = jax.nn.gelu(y) @ w_glu + b_glu
    z_a, z_b = jnp.split(z, 2, axis=-1)
    return (z_a * jax.nn.sigmoid(z_b)) @ w_br


def fourier_branch(u, w_br, b_br):
    bsz, n, _ = u.shape
    z = jnp.fft.fft2(u.astype(F32).reshape(bsz, n, FFT_GROUPS, FFT_GROUP_DIM), axes=(1, 3), norm='ortho')
    zr = jnp.real(z).reshape(bsz, n, FFT_WIDTH).astype(u.dtype)
    return zr @ w_br + b_br


def merge_branches(y_s5, y_fft, g_logits, w_out, b_out):
    g_s5, g_fft = jnp.split(jax.nn.sigmoid(g_logits), N_BRANCHES, axis=-1)
    return (g_s5 * y_s5 + g_fft * y_fft) @ w_out + b_out


def expert_ffn(xb, w_up, b_up, w_down, b_down):
    h = xb @ w_up + b_up
    h_glu, h_lin = jnp.split(h, 2, axis=-1)
    h_glu = jnp.minimum(h_glu, SWIGLU_LIMIT)
    h_lin = jnp.clip(h_lin, -SWIGLU_LIMIT, SWIGLU_LIMIT)
    return (h_glu * jax.nn.sigmoid(SWIGLU_ALPHA * h_glu) * (h_lin + 1)) @ w_down + b_down


def moe(h, w_router, b_router, w_up, b_up, w_down, b_down):
    n_tok, d = h.shape
    logits = (h @ w_router + b_router).astype(F32)
    top_val, top_idx = lax.top_k(logits, TOP_K)
    gate = jax.nn.softmax(top_val, axis=-1).astype(h.dtype)
    m = n_tok * TOP_K
    e = top_idx.reshape(m)
    order = jnp.argsort(e)
    e_sorted = e[order]
    tok_sorted = order // TOP_K
    w_sorted = gate.reshape(m)[order]
    counts = jnp.zeros((N_EXPERTS,), jnp.int32).at[e].add(1)
    starts = jnp.cumsum(counts) - counts
    padded = (counts + MOE_BLOCK - 1) // MOE_BLOCK * MOE_BLOCK
    pad_ends = jnp.cumsum(padded)
    pad_starts = pad_ends - padded
    dest = pad_starts[e_sorted] + (jnp.arange(m, dtype=jnp.int32) - starts[e_sorted])
    n_blocks = (m + MOE_BLOCK - 1) // MOE_BLOCK + N_EXPERTS
    rows = n_blocks * MOE_BLOCK
    buf = jnp.zeros((rows, d), h.dtype).at[dest].set(h[tok_sorted])
    block_start = jnp.arange(n_blocks, dtype=jnp.int32) * MOE_BLOCK
    block_expert = jnp.minimum(jnp.searchsorted(pad_ends, block_start, side='right'), N_EXPERTS - 1)

    def run_block(args):
        xb, eid = args
        return expert_ffn(xb, w_up[eid], b_up[eid], w_down[eid], b_down[eid])

    y_buf = lax.map(run_block, (buf.reshape(n_blocks, MOE_BLOCK, d), block_expert)).reshape(rows, d)
    y = y_buf[dest] * w_sorted[:, None]
    return jax.ops.segment_sum(y, tok_sorted, num_segments=n_tok)


def setup_inputs(seed: int = 0) -> dict:
    key = jax.random.key(seed)
    ks = iter(jax.random.split(key, 40))

    def nrm(shape, scale):
        return jax.random.normal(next(ks), shape, F32) * scale

    D = D_MODEL
    H2 = 2 * D_EXPERT
    x = nrm((BATCH, SEQ, D), 1.0)
    c = nrm((BATCH, D), 1.0)
    ctx = nrm((BATCH, CTX_LEN, D), 1.0)
    c_ctx = nrm((D,), 1.0)
    ln_in_g = 1.0 + nrm((D,), 0.02)
    ln_in_b = nrm((D,), 0.02)
    w_ada = nrm((DEPTH, D, 6 * D), 0.5 * D ** -0.5)
    b_ada = nrm((DEPTH, 6 * D), 0.02)
    w_in = nrm((DEPTH, D, IN_WIDTH), D ** -0.5)
    b_in = nrm((DEPTH, IN_WIDTH), 0.02)
    s5_lambda_re = -0.5 + nrm((DEPTH, 2, S5_GROUPS, S5_STATE), 0.01)
    s5_lambda_im = jnp.pi * jnp.arange(S5_STATE, dtype=F32) + nrm((DEPTH, 2, S5_GROUPS, S5_STATE), 0.01)
    s5_log_dt = jax.random.uniform(next(ks), (DEPTH, 2, S5_GROUPS), F32,
                                   minval=math.log(DT_MIN), maxval=math.log(DT_MAX))
    s5_b_re = nrm((DEPTH, 2, S5_GROUPS, S5_STATE, S5_GROUP), (2 * S5_GROUP) ** -0.5)
    s5_b_im = nrm((DEPTH, 2, S5_GROUPS, S5_STATE, S5_GROUP), (2 * S5_GROUP) ** -0.5)
    s5_c_re = nrm((DEPTH, 2, S5_GROUPS, S5_GROUP, S5_STATE), (2 * S5_STATE) ** -0.5)
    s5_c_im = nrm((DEPTH, 2, S5_GROUPS, S5_GROUP, S5_STATE), (2 * S5_STATE) ** -0.5)
    s5_d = nrm((DEPTH, S5_WIDTH), 1.0)
    w_glu = nrm((DEPTH, S5_WIDTH, 2 * S5_WIDTH), S5_WIDTH ** -0.5)
    b_glu = nrm((DEPTH, 2 * S5_WIDTH), 0.02)
    w_br_s5 = nrm((DEPTH, S5_WIDTH, D), S5_WIDTH ** -0.5)
    w_br_fft = nrm((DEPTH, FFT_WIDTH, D), FFT_WIDTH ** -0.5)
    b_br_fft = nrm((DEPTH, D), 0.02)
    w_out = nrm((DEPTH, D, D), DEEPNORM_BETA * D ** -0.5)
    b_out = nrm((DEPTH, D), 0.02)
    ln1_g = 1.0 + nrm((DEPTH, D), 0.02)
    ln1_b = nrm((DEPTH, D), 0.02)
    w_router = nrm((DEPTH, D, N_EXPERTS), D ** -0.5)
    b_router = nrm((DEPTH, N_EXPERTS), 0.01)
    w_up = nrm((DEPTH, N_EXPERTS, D, H2), D ** -0.5)
    b_up = nrm((DEPTH, N_EXPERTS, H2), 0.02)
    w_down = nrm((DEPTH, N_EXPERTS, D_EXPERT, D), DEEPNORM_BETA * D_EXPERT ** -0.5)
    b_down = nrm((DEPTH, N_EXPERTS, D), 0.02)
    ln2_g = 1.0 + nrm((DEPTH, D), 0.02)
    ln2_b = nrm((DEPTH, D), 0.02)
    return {"x": x, "c": c, "ctx": ctx, "c_ctx": c_ctx, "ln_in_g": ln_in_g, "ln_in_b": ln_in_b,
            "w_ada": w_ada, "b_ada": b_ada, "w_in": w_in, "b_in": b_in,
            "s5_lambda_re": s5_lambda_re, "s5_lambda_im": s5_lambda_im, "s5_log_dt": s5_log_dt,
            "s5_b_re": s5_b_re, "s5_b_im": s5_b_im, "s5_c_re": s5_c_re, "s5_c_im": s5_c_im, "s5_d": s5_d,
            "w_glu": w_glu, "b_glu": b_glu, "w_br_s5": w_br_s5, "w_br_fft": w_br_fft, "b_br_fft": b_br_fft,
            "w_out": w_out, "b_out": b_out, "ln1_g": ln1_g, "ln1_b": ln1_b,
            "w_router": w_router, "b_router": b_router, "w_up": w_up, "b_up": b_up,
            "w_down": w_down, "b_down": b_down, "ln2_g": ln2_g, "ln2_b": ln2_b}


def reference(x, c, ctx, c_ctx, ln_in_g, ln_in_b, w_ada, b_ada, w_in, b_in,
              s5_lambda_re, s5_lambda_im, s5_log_dt, s5_b_re, s5_b_im, s5_c_re, s5_c_im, s5_d,
              w_glu, b_glu, w_br_s5, w_br_fft, b_br_fft, w_out, b_out, ln1_g, ln1_b,
              w_router, b_router, w_up, b_up, w_down, b_down, ln2_g, ln2_b):
    bsz, n_lat, d = x.shape
    n_ctx = ctx.shape[1]
    h = layer_norm(x + sincos_2d(n_lat, x.dtype)[None], ln_in_g, ln_in_b)
    hc = layer_norm(ctx, ln_in_g, ln_in_b)
    silu_c = jax.nn.silu(c)[:, None, :]
    silu_cc = jnp.broadcast_to(jax.nn.silu(c_ctx), (bsz, 1, d))
    split_cols = [S5_WIDTH, S5_WIDTH + FFT_WIDTH]
    zero_state = jnp.zeros((bsz, S5_GROUPS, S5_STATE), F32)
    for l in range(DEPTH):
        last = l == DEPTH - 1
        sh1, sc1, g1, sh2, sc2, g2 = jnp.split(silu_c @ w_ada[l] + b_ada[l], 6, axis=-1)
        sh1c, sc1c, g1c, sh2c, sc2c, g2c = jnp.split(silu_cc @ w_ada[l] + b_ada[l], 6, axis=-1)
        a_re, a_im, bb_re, bb_im = s5_discretise(s5_lambda_re[l], s5_lambda_im[l], s5_log_dt[l],
                                                 s5_b_re[l], s5_b_im[l])
        pc = modulate(hc, sh1c, sc1c) @ w_in[l] + b_in[l]
        pc_s5, pc_fft, pc_g = jnp.split(pc, split_cols, axis=-1)
        st_c = s5_states(pc_s5, ((zero_state, zero_state), (zero_state, zero_state)), a_re, a_im, bb_re, bb_im)
        s0 = ((st_c[0][0][:, -1], st_c[0][1][:, -1]), (st_c[1][0][:, 0], st_c[1][1][:, 0]))
        p = modulate(h, sh1, sc1) @ w_in[l] + b_in[l]
        p_s5, p_fft, p_g = jnp.split(p, split_cols, axis=-1)
        st = s5_states(p_s5, s0, a_re, a_im, bb_re, bb_im)
        y_s5 = s5_branch(s5_readout(st, p_s5, s5_c_re[l], s5_c_im[l], s5_d[l]), w_glu[l], b_glu[l], w_br_s5[l])
        y_fft = fourier_branch(p_fft, w_br_fft[l], b_br_fft[l])
        y = merge_branches(y_s5, y_fft, p_g, w_out[l], b_out[l])
        h = layer_norm(DEEPNORM_ALPHA * h + g1 * y, ln1_g[l], ln1_b[l])
        moe_args = (w_router[l], b_router[l], w_up[l], b_up[l], w_down[l], b_down[l])
        if last:
            m_lat = moe(modulate(h, sh2, sc2).reshape(-1, d), *moe_args).reshape(bsz, n_lat, d)
            h = layer_norm(DEEPNORM_ALPHA * h + g2 * m_lat, ln2_g[l], ln2_b[l])
        else:
            yc_s5 = s5_branch(s5_readout(st_c, pc_s5, s5_c_re[l], s5_c_im[l], s5_d[l]),
                              w_glu[l], b_glu[l], w_br_s5[l])
            yc_fft = fourier_branch(pc_fft, w_br_fft[l], b_br_fft[l])
            yc = merge_branches(yc_s5, yc_fft, pc_g, w_out[l], b_out[l])
            hc = layer_norm(DEEPNORM_ALPHA * hc + g1c * yc, ln1_g[l], ln1_b[l])
            u2 = jnp.concatenate([modulate(h, sh2, sc2).reshape(-1, d),
                                  modulate(hc, sh2c, sc2c).reshape(-1, d)], axis=0)
            m_all = moe(u2, *moe_args)
            m_lat = m_all[:bsz * n_lat].reshape(bsz, n_lat, d)
            m_ctx = m_all[bsz * n_lat:].reshape(bsz, n_ctx, d)
            h = layer_norm(DEEPNORM_ALPHA * h + g2 * m_lat, ln2_g[l], ln2_b[l])
            hc = layer_norm(DEEPNORM_ALPHA * hc + g2c * m_ctx, ln2_g[l], ln2_b[l])
    return h
```

```python
import functools
import math

import jax
import jax.numpy as jnp
import numpy as np
from jax import lax
from jax.experimental import pallas as pl
from jax.experimental.pallas import tpu as pltpu

F32 = jnp.float32
BF16 = jnp.bfloat16
HI = lax.Precision.HIGHEST

D = 1024
N_TOK = 16384
N_CTX = 256
GRID_W = 64
S5_GROUP = 16
S5_GROUPS = 32
S5_STATE = 64
S5_WIDTH = 512
FFT_GROUPS = 4
FFT_DIM = 128
FFT_WIDTH = 512
N_EXPERTS = 32
TOP_K = 4
LN_EPS = 1e-5
ALPHA = 2.0 ** 0.25
SWIGLU_ALPHA = 1.702
SWIGLU_LIMIT = 7.0

LANES = 128
SUBLANES = 8
VMEM_LIMIT = 56 * 1024 * 1024

CH = 8
N_CHUNK = N_TOK // CH
N_CHUNK_CTX = N_CTX // CH
NSEG = SUBLANES
SEG = N_CHUNK // NSEG
PITCH = SEG + 8
GPT = LANES // S5_GROUP
NJ = S5_WIDTH // LANES
CL = CH * LANES
SW = 4 * GPT * S5_STATE

FN = 128
FB = 4

TM = 512
TM_COMB = 256
BM = 256
N_SLOTS = N_TOK * TOP_K
N_BLOCKS = N_SLOTS // BM + N_EXPERTS
ROWS = N_BLOCKS * BM


def _cparams(sem):
    return pltpu.CompilerParams(dimension_semantics=sem, vmem_limit_bytes=VMEM_LIMIT)


def _layer_norm(x, g, b):
    mu = jnp.mean(x, axis=-1, keepdims=True)
    xc = x - mu
    var = jnp.mean(xc * xc, axis=-1, keepdims=True)
    return xc * lax.rsqrt(var + LN_EPS) * g + b


def _sigmoid(x):
    return 1.0 / (1.0 + jnp.exp(-x))


def _ada_kernel(c_ref, w_ref, b_ref, o_ref):
    c = c_ref[...]
    s = c * _sigmoid(c)
    o_ref[...] = jnp.dot(s, w_ref[...], preferred_element_type=F32, precision=HI) + b_ref[...]


def _ada(cc, w_ada, b_ada):
    nb = 4
    wb = 6 * D // nb
    return pl.pallas_call(
        _ada_kernel,
        grid=(nb,),
        in_specs=[pl.BlockSpec((SUBLANES, D), lambda i: (0, 0)),
                  pl.BlockSpec((D, wb), lambda i: (0, i)),
                  pl.BlockSpec((1, wb), lambda i: (0, i))],
        out_specs=pl.BlockSpec((SUBLANES, wb), lambda i: (0, i)),
        out_shape=jax.ShapeDtypeStruct((SUBLANES, 6 * D), F32),
        compiler_params=_cparams(("parallel",)),
        name="ada",
    )(cc, w_ada, b_ada)


def _fftw_kernel(w_ref, b_ref, f_ref, wo_ref, bo_ref):
    f = f_ref[...]
    wo_ref[...] = jnp.dot(w_ref[...], f, preferred_element_type=F32, precision=HI)
    bo_ref[...] = jnp.dot(b_ref[...], f, preferred_element_type=F32, precision=HI)


def _fft_weights(w_fft, b_fft8, fc):
    return pl.pallas_call(
        _fftw_kernel,
        out_shape=(jax.ShapeDtypeStruct((D, 2 * FFT_WIDTH), F32),
                   jax.ShapeDtypeStruct((SUBLANES, 2 * FFT_WIDTH), F32)),
        compiler_params=pltpu.CompilerParams(vmem_limit_bytes=VMEM_LIMIT),
        name="fftw",
    )(w_fft, b_fft8, fc)


def _pos_code(er_ref, ec_ref, tm):
    nr = tm // GRID_W
    er = er_ref[...]
    row = jnp.broadcast_to(er[:, None, :], (nr, GRID_W, D // 2)).reshape(tm, D // 2)
    col = jnp.concatenate([ec_ref[...]] * nr, axis=0)
    return jnp.concatenate([row, col], axis=-1)


def _to_chunk_major(val, scr_ref, out_ref, tm):
    for j in range(NJ):
        scr_ref[j] = val[:, j * LANES:(j + 1) * LANES]
    for t in range(CH):
        for j in range(NJ):
            piece = scr_ref[j, pl.ds(t, tm // CH, stride=CH), :]
            out_ref[t, :, j * LANES:(j + 1) * LANES] = piece.astype(out_ref.dtype)


def _proj_kernel(x_ref, er_ref, ec_ref, lg_ref, lb_ref, m_ref, s_ref, w_ref, b_ref,
                 p_ref, xr_ref, xi_ref, scr_ref):
    x = x_ref[...] + _pos_code(er_ref, ec_ref, TM)
    h = _layer_norm(x, lg_ref[...], lb_ref[...])
    u = (h * m_ref[...] + s_ref[...]).astype(BF16)
    p = jnp.dot(u, w_ref[...], preferred_element_type=F32) + b_ref[...]
    _to_chunk_major(p[:, :S5_WIDTH], scr_ref, p_ref, TM)
    xr_ref[...] = p[:, S5_WIDTH:S5_WIDTH + FFT_WIDTH].astype(BF16)
    xi_ref[...] = p[:, S5_WIDTH + FFT_WIDTH:].astype(BF16)


def _proj(x, emb_r, emb_c, lg, lb, m1, s1, wcat, bcat):
    nw = wcat.shape[1]
    vec = pl.BlockSpec((1, D), lambda i: (0, 0))
    return pl.pallas_call(
        _proj_kernel,
        grid=(N_TOK // TM,),
        in_specs=[pl.BlockSpec((TM, D), lambda i: (i, 0)),
                  pl.BlockSpec((TM // GRID_W, D // 2), lambda i: (i, 0)),
                  pl.BlockSpec((GRID_W, D // 2), lambda i: (0, 0)),
                  vec, vec, vec, vec,
                  pl.BlockSpec((D, nw), lambda i: (0, 0)),
                  pl.BlockSpec((1, nw), lambda i: (0, 0))],
        out_specs=(pl.BlockSpec((CH, TM // CH, S5_WIDTH), lambda i: (0, i, 0)),
                   pl.BlockSpec((TM, FFT_WIDTH), lambda i: (i, 0)),
                   pl.BlockSpec((TM, FFT_WIDTH), lambda i: (i, 0))),
        out_shape=(jax.ShapeDtypeStruct((CH, N_CHUNK, S5_WIDTH), BF16),
                   jax.ShapeDtypeStruct((N_TOK, FFT_WIDTH), BF16),
                   jax.ShapeDtypeStruct((N_TOK, FFT_WIDTH), BF16)),
        scratch_shapes=[pltpu.VMEM((NJ, TM, LANES), F32)],
        compiler_params=_cparams(("parallel",)),
        name="proj",
    )(x, emb_r, emb_c, lg, lb, m1, s1, wcat, bcat)


def _ctx_proj_kernel(x_ref, lg_ref, lb_ref, m_ref, s_ref, w_ref, b_ref, p_ref, scr_ref):
    h = _layer_norm(x_ref[...], lg_ref[...], lb_ref[...])
    u = (h * m_ref[...] + s_ref[...]).astype(BF16)
    p = jnp.dot(u, w_ref[...], preferred_element_type=F32) + b_ref[...]
    _to_chunk_major(p, scr_ref, p_ref, N_CTX)


def _ctx_proj(ctx, lg, lb, m1, s1, w_s5, b_s5):
    return pl.pallas_call(
        _ctx_proj_kernel,
        out_shape=jax.ShapeDtypeStruct((CH, N_CHUNK_CTX, S5_WIDTH), BF16),
        scratch_shapes=[pltpu.VMEM((NJ, N_CTX, LANES), F32)],
        compiler_params=pltpu.CompilerParams(vmem_limit_bytes=VMEM_LIMIT),
        name="ctxproj",
    )(ctx, lg, lb, m1, s1, w_s5, b_s5)


def _s5_tables(lam_re, lam_im, log_dt, b_re, b_im, c_re, c_im, d_skip):
    dt = jnp.exp(log_dt)[..., None]
    zr = lam_re * dt
    zi = lam_im * dt

    def apow(m):
        m = jnp.asarray(m, F32)
        mag = jnp.exp(zr[..., None] * m)
        return mag * jnp.cos(zi[..., None] * m), mag * jnp.sin(zi[..., None] * m)

    a_re, a_im = apow(jnp.ones((1,), F32))
    a_re, a_im = a_re[..., 0], a_im[..., 0]
    den = lam_re * lam_re + lam_im * lam_im
    num_re = a_re - 1.0
    k_re = (num_re * lam_re + a_im * lam_im) / den
    k_im = (a_im * lam_re - num_re * lam_im) / den
    bb_re = k_re[..., None] * b_re - k_im[..., None] * b_im
    bb_im = k_re[..., None] * b_im + k_im[..., None] * b_re

    ks = jnp.arange(CH + 1, dtype=F32)
    pw_re, pw_im = apow(ks)
    eye = jnp.eye(GPT, dtype=F32)

    cb_rr = jnp.einsum('dghp,dgpk,dgpi->dgkhi', c_re, pw_re[..., :CH], bb_re, precision=HI)
    cb_ii = jnp.einsum('dghp,dgpk,dgpi->dgkhi', c_re, pw_im[..., :CH], bb_im, precision=HI)
    cb_ir = jnp.einsum('dghp,dgpk,dgpi->dgkhi', c_im, pw_re[..., :CH], bb_im, precision=HI)
    cb_ri = jnp.einsum('dghp,dgpk,dgpi->dgkhi', c_im, pw_im[..., :CH], bb_re, precision=HI)
    taps = cb_rr - cb_ii - cb_ir - cb_ri
    t_in = jnp.arange(CH)[:, None]
    t_out = jnp.arange(CH)[None, :]
    lag = jnp.abs(t_out - t_in)
    tf = jnp.where((t_out >= t_in)[None, :, :, None, None], taps[0][:, lag], 0.0)
    tb = jnp.where((t_out <= t_in)[None, :, :, None, None], taps[1][:, lag], 0.0)
    mt = jnp.transpose(tf + tb, (0, 1, 4, 2, 3))
    skip = (jnp.eye(CH, dtype=F32)[None, :, None, :, None]
            * jnp.eye(S5_GROUP, dtype=F32)[None, None, :, None, :]
            * d_skip.reshape(S5_GROUPS, 1, S5_GROUP, 1, 1))
    mt = (mt + skip).reshape(NJ, GPT, CH, S5_GROUP, CH, S5_GROUP)
    mt = jnp.transpose(mt, (0, 2, 1, 3, 4, 5))
    w_m = (mt[:, :, :, :, :, None, :] * eye[None, None, :, None, None, :, None]).reshape(NJ, CL, CL)

    ef = (CH - 1) - jnp.arange(CH)
    eb = jnp.arange(CH)

    def q_part(d, e):
        pr = jnp.transpose(pw_re[d][..., e], (0, 2, 1))[:, :, None, :]
        pi = jnp.transpose(pw_im[d][..., e], (0, 2, 1))[:, :, None, :]
        br = jnp.transpose(bb_re[d], (0, 2, 1))[:, None, :, :]
        bi = jnp.transpose(bb_im[d], (0, 2, 1))[:, None, :, :]
        return pr * br - pi * bi, pr * bi + pi * br
    qfr, qfi = q_part(0, ef)
    qbr, qbi = q_part(1, eb)
    q = jnp.stack([qfr, qfi, qbr, qbi], axis=0).reshape(4, NJ, GPT, CH, S5_GROUP, S5_STATE)
    q = jnp.transpose(q, (1, 3, 2, 4, 0, 5))
    w_q = (q[:, :, :, :, :, None, :] * eye[None, None, :, None, None, :, None]).reshape(NJ, CL, SW)

    of = jnp.arange(CH) + 1
    ob = CH - jnp.arange(CH)

    def p_part(d, e):
        pr = pw_re[d][..., e][:, :, :, None]
        pi = pw_im[d][..., e][:, :, :, None]
        cr = jnp.transpose(c_re[d], (0, 2, 1))[:, :, None, :]
        ci = jnp.transpose(c_im[d], (0, 2, 1))[:, :, None, :]
        return cr * pr - ci * pi, -(cr * pi + ci * pr)
    pfr, pfi = p_part(0, of)
    pbr, pbi = p_part(1, ob)
    p = jnp.stack([pfr, pfi, pbr, pbi], axis=0).reshape(4, NJ, GPT, S5_STATE, CH, S5_GROUP)
    p = jnp.transpose(p, (1, 0, 2, 3, 4, 5))
    w_p = (p[:, :, :, :, :, None, :] * eye[None, None, :, None, None, :, None]).reshape(NJ, SW, CL)

    def lanes(v):
        return jnp.transpose(v.reshape(2, NJ, GPT * S5_STATE), (1, 0, 2))
    c_r, c_i = apow(jnp.full((1,), float(CH), F32))
    s_r, s_i = apow(jnp.full((1,), float(CH * SEG), F32))
    cr, ci, sr, si = (lanes(v[..., 0]) for v in (c_r, c_i, s_r, s_i))
    trans = jnp.stack([cr[:, 0], ci[:, 0], cr[:, 1], ci[:, 1],
                       sr[:, 0], si[:, 0], sr[:, 1], si[:, 1]], axis=1)

    cidx = jnp.arange(N_CHUNK_CTX, dtype=F32)
    wf_r, wf_i = apow(CH * (N_CHUNK_CTX - 1 - cidx))
    wb_r, wb_i = apow(CH * cidx)

    def ctx_lanes(v, d):
        return jnp.transpose(v[d].reshape(NJ, GPT * S5_STATE, N_CHUNK_CTX), (0, 2, 1))
    ctx_w = jnp.stack([ctx_lanes(wf_r, 0), ctx_lanes(wf_i, 0),
                       ctx_lanes(wb_r, 1), ctx_lanes(wb_i, 1)], axis=1)
    return w_m.astype(BF16), w_q.astype(BF16), w_p.astype(BF16), trans, ctx_w


def _s5_kernel(p_ref, pc_ref, wm_ref, wq_ref, wp_ref, tr_ref, cw_ref, y_ref, v_ref):
    nq = NJ
    half = GPT * S5_STATE

    def chunk_rows(ref, r0, nrows):
        return jnp.concatenate([ref[t, pl.ds(r0, nrows), :] for t in range(CH)], axis=-1)

    def fill(k, c):
        r0 = pl.multiple_of(k * SEG, SEG)
        v = jnp.dot(chunk_rows(p_ref, r0, SEG), wq_ref[0], preferred_element_type=F32)
        o0 = pl.multiple_of(k * PITCH, SUBLANES)
        for s in range(4 * nq):
            v_ref[s, pl.ds(o0, SEG), :] = v[:, s * LANES:(s + 1) * LANES]
        return c
    lax.fori_loop(0, NSEG, fill, 0)

    vc = jnp.dot(chunk_rows(pc_ref, 0, N_CHUNK_CTX), wq_ref[0], preferred_element_type=F32)
    vfr, vfi, vbr, vbi = (vc[:, i * half:(i + 1) * half] for i in range(4))
    wfr, wfi, wbr, wbi = (cw_ref[0, i] for i in range(4))
    s0_fr = jnp.sum(wfr * vfr - wfi * vfi, axis=0, keepdims=True)
    s0_fi = jnp.sum(wfr * vfi + wfi * vfr, axis=0, keepdims=True)
    s0_br = jnp.sum(wbr * vbr - wbi * vbi, axis=0, keepdims=True)
    s0_bi = jnp.sum(wbr * vbi + wbi * vbr, axis=0, keepdims=True)

    tr = tr_ref[0]
    afr, afi, abr, abi = (jnp.broadcast_to(tr[i:i + 1], (NSEG, half)) for i in range(4))
    gfr, gfi, gbr, gbi = (tr[i:i + 1] for i in range(4, 8))

    def load_part(part, i):
        return jnp.concatenate(
            [v_ref[part * nq + q, pl.ds(i, NSEG, stride=PITCH), :] for q in range(nq)], axis=-1)

    def store_part(part, i, val):
        for q in range(nq):
            v_ref[part * nq + q, pl.ds(i, NSEG, stride=PITCH), :] = val[:, q * LANES:(q + 1) * LANES]

    def step(i, carry, write):
        fr, fi, br, bi = carry
        ib = SEG - 1 - i
        ufr, ufi = load_part(0, i), load_part(1, i)
        ubr, ubi = load_part(2, ib), load_part(3, ib)
        if write:
            store_part(0, i, fr)
            store_part(1, i, fi)
            store_part(2, ib, br)
            store_part(3, ib, bi)
        return (afr * fr - afi * fi + ufr, afr * fi + afi * fr + ufi,
                abr * br - abi * bi + ubr, abr * bi + abi * br + ubi)

    zero = jnp.zeros((NSEG, half), F32)
    ffr, ffi, fbr, fbi = lax.fori_loop(0, SEG, functools.partial(step, write=False),
                                       (zero, zero, zero, zero))

    rows_fr, rows_fi = [s0_fr], [s0_fi]
    for k in range(1, NSEG):
        pr, pi = rows_fr[-1], rows_fi[-1]
        rows_fr.append(gfr * pr - gfi * pi + ffr[k - 1:k])
        rows_fi.append(gfr * pi + gfi * pr + ffi[k - 1:k])
    rows_br, rows_bi = [s0_br], [s0_bi]
    for k in range(NSEG - 2, -1, -1):
        pr, pi = rows_br[0], rows_bi[0]
        rows_br.insert(0, gbr * pr - gbi * pi + fbr[k + 1:k + 2])
        rows_bi.insert(0, gbr * pi + gbi * pr + fbi[k + 1:k + 2])
    init = tuple(jnp.concatenate(r, axis=0) for r in (rows_fr, rows_fi, rows_br, rows_bi))

    lax.fori_loop(0, SEG, functools.partial(step, write=True), init)

    def emit(k, c):
        r0 = pl.multiple_of(k * SEG, SEG)
        o0 = pl.multiple_of(k * PITCH, SUBLANES)
        b = chunk_rows(p_ref, r0, SEG)
        sin = jnp.concatenate([v_ref[s, pl.ds(o0, SEG), :] for s in range(4 * nq)], axis=-1)
        y = (jnp.dot(b, wm_ref[0], preferred_element_type=F32)
             + jnp.dot(sin.astype(BF16), wp_ref[0], preferred_element_type=F32))
        for t in range(CH):
            y_ref[t, pl.ds(r0, SEG), :] = y[:, t * LANES:(t + 1) * LANES].astype(y_ref.dtype)
        return c
    lax.fori_loop(0, NSEG, emit, 0)


def _s5(p_t, pc_t, w_m, w_q, w_p, trans, ctx_w):
    one = pl.Buffered(1)
    return pl.pallas_call(
        _s5_kernel,
        grid=(NJ,),
        in_specs=[pl.BlockSpec((CH, N_CHUNK, LANES), lambda j: (0, 0, j)),
                  pl.BlockSpec((CH, N_CHUNK_CTX, LANES), lambda j: (0, 0, j)),
                  pl.BlockSpec((1, CL, CL), lambda j: (j, 0, 0), pipeline_mode=one),
                  pl.BlockSpec((1, CL, SW), lambda j: (j, 0, 0), pipeline_mode=one),
                  pl.BlockSpec((1, SW, CL), lambda j: (j, 0, 0), pipeline_mode=one),
                  pl.BlockSpec((1, SUBLANES, GPT * S5_STATE), lambda j: (j, 0, 0)),
                  pl.BlockSpec((1, 4, N_CHUNK_CTX, GPT * S5_STATE), lambda j: (j, 0, 0, 0))],
        out_specs=pl.BlockSpec((CH, N_CHUNK, LANES), lambda j: (0, 0, j)),
        out_shape=jax.ShapeDtypeStruct((CH, N_CHUNK, S5_WIDTH), BF16),
        scratch_shapes=[pltpu.VMEM((4 * NJ, NSEG * PITCH, LANES), F32)],
        compiler_params=_cparams(("parallel",)),
        name="s5",
    )(p_t, pc_t, w_m, w_q, w_p, trans, ctx_w)


def _dft_tables():
    n = np.arange(FN)
    ang = 2.0 * np.pi * np.outer(n, n) / FN
    c, s = np.cos(ang), np.sin(ang)
    st1 = np.block([[c, s], [-s, c]])
    tw = 2.0 * np.pi * np.outer(n, n) / (FN * FN)
    wr, wi = np.cos(tw), -np.sin(tw)
    fr = c[None] * wr[:, None, :] + s[None] * wi[:, None, :]
    fi = c[None] * wi[:, None, :] - s[None] * wr[:, None, :]
    st2 = np.concatenate([fr, -fi], axis=-1)
    scale = 1.0 / math.sqrt(N_TOK * FFT_DIM)
    blk_c = np.kron(np.eye(FFT_GROUPS), c) * scale
    blk_s = np.kron(np.eye(FFT_GROUPS), s) * scale
    fc = np.concatenate([blk_c, -blk_s], axis=1)
    return (jnp.asarray(st1, F32).astype(BF16), jnp.asarray(st2, F32).astype(BF16), jnp.asarray(fc, F32))


def _fft1_kernel(xr_ref, xi_ref, f_ref, yr_ref, yi_ref):
    xs = jnp.concatenate([xr_ref[...], xi_ref[...]], axis=0)
    y = jnp.dot(f_ref[...], xs, preferred_element_type=F32)
    yr_ref[...] = y[:FN].astype(BF16)
    yi_ref[...] = y[FN:].astype(BF16)


def _fft1(xr, xi, st1):
    w = FB * FFT_WIDTH
    spec = pl.BlockSpec((FN, w), lambda i: (0, i))
    xr2 = xr.reshape(FN, FN * FFT_WIDTH)
    xi2 = xi.reshape(FN, FN * FFT_WIDTH)
    return pl.pallas_call(
        _fft1_kernel,
        grid=(FN // FB,),
        in_specs=[spec, spec, pl.BlockSpec((2 * FN, 2 * FN), lambda i: (0, 0))],
        out_specs=(spec, spec),
        out_shape=(jax.ShapeDtypeStruct((FN, FN * FFT_WIDTH), BF16),) * 2,
        compiler_params=_cparams(("parallel",)),
        name="fft1",
    )(xr2, xi2, st1)


def _fft2_kernel(yr_ref, yi_ref, f_ref, z_ref):
    for b in range(FB):
        ys = jnp.concatenate([yr_ref[b * FN:(b + 1) * FN, :], yi_ref[b * FN:(b + 1) * FN, :]], axis=0)
        z = jnp.dot(f_ref[b], ys, preferred_element_type=F32)
        z_ref[:, b * FFT_WIDTH:(b + 1) * FFT_WIDTH] = z.astype(BF16)


def _fft2(yr, yi, st2):
    rows = pl.BlockSpec((FB * FN, FFT_WIDTH), lambda i: (i, 0))
    yr2 = yr.reshape(N_TOK, FFT_WIDTH)
    yi2 = yi.reshape(N_TOK, FFT_WIDTH)
    z = pl.pallas_call(
        _fft2_kernel,
        grid=(FN // FB,),
        in_specs=[rows, rows, pl.BlockSpec((FB, FN, 2 * FN), lambda i: (i, 0, 0))],
        out_specs=pl.BlockSpec((FN, FB * FFT_WIDTH), lambda i: (0, i)),
        out_shape=jax.ShapeDtypeStruct((FN, FN * FFT_WIDTH), BF16),
        compiler_params=_cparams(("parallel",)),
        name="fft2",
    )(yr2, yi2, st2)
    return z.reshape(N_TOK, FFT_WIDTH)


def _gelu_tanh(x):
    return 0.5 * x * (1.0 + jnp.tanh(math.sqrt(2.0 / math.pi) * (x + 0.044715 * (x * x * x))))


def _mix_kernel(x_ref, er_ref, ec_ref, lg_ref, lb_ref, m1_ref, s1_ref, wg_ref, bg_ref,
                yt_ref, zr_ref, wglu_ref, bglu_ref, wbs_ref, wbf_ref, bbf_ref, wo_ref, bo_ref,
                g1_ref, l1g_ref, l1b_ref, m2_ref, s2_ref, wr_ref, br_ref, tri_ref,
                h1_ref, u2_ref, idx_ref, rank_ref, gate_ref, cnt_ref, scr_ref, base_ref):
    i = pl.program_id(0)

    @pl.when(i == 0)
    def _():
        base_ref[...] = jnp.zeros_like(base_ref)

    x = x_ref[...] + _pos_code(er_ref, ec_ref, TM)
    h = _layer_norm(x, lg_ref[...], lb_ref[...])
    u = (h * m1_ref[...] + s1_ref[...]).astype(BF16)
    gates = _sigmoid(jnp.dot(u, wg_ref[...], preferred_element_type=F32) + bg_ref[...])

    for t in range(CH):
        for j in range(NJ):
            scr_ref[j, pl.ds(t, TM // CH, stride=CH), :] = (
                yt_ref[t, :, j * LANES:(j + 1) * LANES].astype(F32))
    ys = jnp.concatenate([scr_ref[j] for j in range(NJ)], axis=-1)
    z = jnp.dot(_gelu_tanh(ys).astype(BF16), wglu_ref[...], preferred_element_type=F32) + bglu_ref[...]
    glu = (z[:, :S5_WIDTH] * _sigmoid(z[:, S5_WIDTH:])).astype(BF16)
    y_s5 = jnp.dot(glu, wbs_ref[...], preferred_element_type=F32)
    y_fft = jnp.dot(zr_ref[...], wbf_ref[...], preferred_element_type=F32) + bbf_ref[...]
    mixed = (gates[:, :D] * y_s5 + gates[:, D:] * y_fft).astype(BF16)
    y = jnp.dot(mixed, wo_ref[...], preferred_element_type=F32) + bo_ref[...]
    h1 = _layer_norm(ALPHA * h + g1_ref[...] * y, l1g_ref[...], l1b_ref[...])
    h1_ref[...] = h1
    u2 = h1 * m2_ref[...] + s2_ref[...]
    u2_ref[...] = u2

    logits = lax.dot_general(wr_ref[...], u2, (((1,), (1,)), ((), ())),
                             preferred_element_type=F32, precision=HI) + br_ref[:, 0:1]
    eidx = lax.broadcasted_iota(jnp.int32, (N_EXPERTS, TM), 0)
    vals, idxs, hots = [], [], []
    cur = logits
    for _k in range(TOP_K):
        m = jnp.max(cur, axis=0, keepdims=True)
        sel = jnp.min(jnp.where(cur == m, eidx, N_EXPERTS), axis=0, keepdims=True)
        hot = eidx == sel
        cur = jnp.where(hot, -jnp.inf, cur)
        vals.append(m)
        idxs.append(sel)
        hots.append(hot)
    exps = [jnp.exp(v - vals[0]) for v in vals]
    den = exps[0] + exps[1] + exps[2] + exps[3]
    gate4 = jnp.concatenate([e / den for e in exps], axis=0)
    idx_ref[...] = jnp.concatenate(idxs, axis=0)

    hot_sum = (hots[0] | hots[1] | hots[2] | hots[3]).astype(F32)
    before = jnp.dot(hot_sum.astype(BF16), tri_ref[...], preferred_element_type=F32)
    tot = base_ref[:, 0:1] + before
    rank_ref[...] = jnp.concatenate(
        [jnp.sum(jnp.where(hk, tot, 0.0), axis=0, keepdims=True) for hk in hots], axis=0
    ).astype(jnp.int32)
    base_ref[...] = base_ref[...] + jnp.sum(hot_sum, axis=1, keepdims=True)
    cnt_ref[...] = base_ref[...]

    gpad = jnp.concatenate([gate4, jnp.zeros((LANES - TOP_K, TM), F32)], axis=0)
    gate_ref[...] = gpad.T


def _mix(x, emb_r, emb_c, lg, lb, m1, s1, wg, bg, y_t, zr, wglu, bglu, wbs, wbf, bbf, wo, bo,
         g1, l1g, l1b, m2, s2, wr_t, br, tri):
    vec = pl.BlockSpec((1, D), lambda i: (0, 0))

    def full(a):
        return pl.BlockSpec(a.shape, lambda i: (0,) * a.ndim)
    return pl.pallas_call(
        _mix_kernel,
        grid=(N_TOK // TM,),
        in_specs=[pl.BlockSpec((TM, D), lambda i: (i, 0)),
                  pl.BlockSpec((TM // GRID_W, D // 2), lambda i: (i, 0)),
                  pl.BlockSpec((GRID_W, D // 2), lambda i: (0, 0)),
                  vec, vec, vec, vec, full(wg), full(bg),
                  pl.BlockSpec((CH, TM // CH, S5_WIDTH), lambda i: (0, i, 0)),
                  pl.BlockSpec((TM, FFT_WIDTH), lambda i: (i, 0)),
                  full(wglu), full(bglu), full(wbs), full(wbf), full(bbf), full(wo), full(bo),
                  vec, vec, vec, vec, vec, full(wr_t), full(br), full(tri)],
        out_specs=(pl.BlockSpec((TM, D), lambda i: (i, 0)),
                   pl.BlockSpec((TM, D), lambda i: (i, 0)),
                   pl.BlockSpec((TOP_K, TM), lambda i: (0, i)),
                   pl.BlockSpec((TOP_K, TM), lambda i: (0, i)),
                   pl.BlockSpec((TM, LANES), lambda i: (i, 0)),
                   pl.BlockSpec((N_EXPERTS, LANES), lambda i: (0, 0))),
        out_shape=(jax.ShapeDtypeStruct((N_TOK, D), F32),
                   jax.ShapeDtypeStruct((N_TOK, D), F32),
                   jax.ShapeDtypeStruct((TOP_K, N_TOK), jnp.int32),
                   jax.ShapeDtypeStruct((TOP_K, N_TOK), jnp.int32),
                   jax.ShapeDtypeStruct((N_TOK, LANES), F32),
                   jax.ShapeDtypeStruct((N_EXPERTS, LANES), F32)),
        scratch_shapes=[pltpu.VMEM((NJ, TM, LANES), F32),
                        pltpu.VMEM((N_EXPERTS, LANES), F32)],
        compiler_params=_cparams(("arbitrary",)),
        name="mix",
    )(x, emb_r, emb_c, lg, lb, m1, s1, wg, bg, y_t, zr, wglu, bglu, wbs, wbf, bbf, wo, bo,
      g1, l1g, l1b, m2, s2, wr_t, br, tri)


def _row_copy(src, dst, sem):
    return pltpu.make_async_copy(src, dst, sem)


def _dispatch_kernel(pend_ref, padded_ref, dest_ref, u_ref, buf_ref, zero_ref, zsem, sem):
    i = pl.program_id(0)

    @pl.when(i == 0)
    def _():
        zero_ref[...] = jnp.zeros_like(zero_ref)
        n_used = pend_ref[N_EXPERTS - 1] // BM

        def clear_copy(start):
            return _row_copy(zero_ref, buf_ref.at[pl.ds(pl.multiple_of(start, BM), BM)], zsem)

        def each(fn):
            def expert(e, c):
                @pl.when(padded_ref[e] > 0)
                def _():
                    fn(clear_copy(pend_ref[e] - BM))
                return c
            lax.fori_loop(0, N_EXPERTS, expert, 0)

            def tail(b, c):
                fn(clear_copy(b * BM))
                return c
            lax.fori_loop(n_used, N_BLOCKS, tail, 0)
        each(lambda cp: cp.start())
        each(lambda cp: cp.wait())

    def issue(t, c):
        for k in range(TOP_K):
            d = dest_ref[0, 0, k * TM + t]
            _row_copy(u_ref.at[pl.ds(t, 1)], buf_ref.at[pl.ds(d, 1)], sem).start()
        return c
    lax.fori_loop(0, TM, issue, 0)
    for _k in range(TOP_K):
        _row_copy(u_ref, buf_ref.at[pl.ds(0, TM)], sem).wait()


def _dispatch(pad_ends, padded, dest_tiles, u2):
    return pl.pallas_call(
        _dispatch_kernel,
        grid_spec=pltpu.PrefetchScalarGridSpec(
            num_scalar_prefetch=2,
            grid=(N_TOK // TM,),
            in_specs=[pl.BlockSpec((1, 1, TOP_K * TM), lambda i, a, b: (i, 0, 0),
                                   memory_space=pltpu.SMEM),
                      pl.BlockSpec((TM, D), lambda i, a, b: (i, 0))],
            out_specs=pl.BlockSpec(memory_space=pl.ANY),
            scratch_shapes=[pltpu.VMEM((BM, D), F32),
                            pltpu.SemaphoreType.DMA(()),
                            pltpu.SemaphoreType.DMA(())]),
        out_shape=jax.ShapeDtypeStruct((ROWS, D), F32),
        compiler_params=_cparams(("arbitrary",)),
        name="dispatch",
    )(pad_ends, padded, dest_tiles, u2)


def _ffn_kernel(be_ref, nu_ref, x_ref, wu_ref, bu_ref, wd_ref, bd_ref, y_ref, wub_ref, wdb_ref):
    i = pl.program_id(0)
    used = i < nu_ref[0]

    @pl.when(used)
    def _():
        prev = be_ref[jnp.maximum(i - 1, 0)]

        @pl.when((i == 0) | (be_ref[i] != prev))
        def _():
            wub_ref[...] = wu_ref[0].astype(BF16)
            wdb_ref[...] = wd_ref[0].astype(BF16)

        h = jnp.dot(x_ref[...].astype(BF16), wub_ref[...], preferred_element_type=F32) + bu_ref[0]
        h_glu = jnp.minimum(h[:, :D], SWIGLU_LIMIT)
        h_lin = jnp.clip(h[:, D:], -SWIGLU_LIMIT, SWIGLU_LIMIT)
        act = (h_glu * _sigmoid(SWIGLU_ALPHA * h_glu) * (h_lin + 1.0)).astype(BF16)
        y_ref[...] = jnp.dot(act, wdb_ref[...], preferred_element_type=F32) + bd_ref[0]

    @pl.when(jnp.logical_not(used))
    def _():
        y_ref[...] = jnp.zeros_like(y_ref)


def _ffn(block_expert, n_used, buf, w_up, b_up, w_down, b_down):
    def blk(i, be, nu):
        return jnp.minimum(i, nu[0] - 1)
    return pl.pallas_call(
        _ffn_kernel,
        grid_spec=pltpu.PrefetchScalarGridSpec(
            num_scalar_prefetch=2,
            grid=(N_BLOCKS,),
            in_specs=[pl.BlockSpec((BM, D), lambda i, be, nu: (blk(i, be, nu), 0)),
                      pl.BlockSpec((1, D, 2 * D), lambda i, be, nu: (be[blk(i, be, nu)], 0, 0)),
                      pl.BlockSpec((1, 1, 2 * D), lambda i, be, nu: (be[blk(i, be, nu)], 0, 0)),
                      pl.BlockSpec((1, D, D), lambda i, be, nu: (be[blk(i, be, nu)], 0, 0)),
                      pl.BlockSpec((1, 1, D), lambda i, be, nu: (be[blk(i, be, nu)], 0, 0))],
            out_specs=pl.BlockSpec((BM, D), lambda i, be, nu: (i, 0)),
            scratch_shapes=[pltpu.VMEM((D, 2 * D), BF16),
                            pltpu.VMEM((D, D), BF16)]),
        out_shape=jax.ShapeDtypeStruct((ROWS, D), F32),
        compiler_params=_cparams(("arbitrary",)),
        name="ffn",
    )(block_expert, n_used, buf, w_up, b_up, w_down, b_down)


def _combine_kernel(dest_ref, y_ref, h1_ref, gate_ref, g2_ref, lg_ref, lb_ref, o_ref, rows_ref, sem):
    def issue(t, c):
        for k in range(TOP_K):
            d = dest_ref[0, 0, k * TM_COMB + t]
            _row_copy(y_ref.at[pl.ds(d, 1)], rows_ref.at[k, pl.ds(t, 1)], sem).start()
        return c
    lax.fori_loop(0, TM_COMB, issue, 0)
    for k in range(TOP_K):
        _row_copy(y_ref.at[pl.ds(0, TM_COMB)], rows_ref.at[k], sem).wait()
    gate = gate_ref[...]
    m = rows_ref[0] * gate[:, 0:1]
    for k in range(1, TOP_K):
        m = m + rows_ref[k] * gate[:, k:k + 1]
    o_ref[...] = _layer_norm(ALPHA * h1_ref[...] + g2_ref[...] * m, lg_ref[...], lb_ref[...])


def _combine(dest_tiles, y_buf, h1, gate_tok, g2, lg, lb):
    vec = pl.BlockSpec((1, D), lambda i: (0, 0))
    return pl.pallas_call(
        _combine_kernel,
        grid=(N_TOK // TM_COMB,),
        in_specs=[pl.BlockSpec((1, 1, TOP_K * TM_COMB), lambda i: (i, 0, 0), memory_space=pltpu.SMEM),
                  pl.BlockSpec(memory_space=pl.ANY),
                  pl.BlockSpec((TM_COMB, D), lambda i: (i, 0)),
                  pl.BlockSpec((TM_COMB, LANES), lambda i: (i, 0)),
                  vec, vec, vec],
        out_specs=pl.BlockSpec((TM_COMB, D), lambda i: (i, 0)),
        out_shape=jax.ShapeDtypeStruct((N_TOK, D), F32),
        scratch_shapes=[pltpu.VMEM((TOP_K, TM_COMB, D), F32),
                        pltpu.SemaphoreType.DMA(())],
        compiler_params=_cparams(("parallel",)),
        name="combine",
    )(dest_tiles, y_buf, h1, gate_tok, g2, lg, lb)


def _sincos_tables():
    q = D // 4
    omega = 1.0 / (10000.0 ** (jnp.arange(q, dtype=F32) / q))

    def emb(n):
        ang = jnp.arange(n, dtype=F32)[:, None] * omega[None, :]
        return jnp.concatenate([jnp.sin(ang), jnp.cos(ang)], axis=-1)
    return emb(N_TOK // GRID_W), emb(GRID_W)


def _tile_slots(a, tile):
    return jnp.transpose(a.reshape(TOP_K, N_TOK // tile, tile), (1, 0, 2)).reshape(N_TOK // tile, 1, TOP_K * tile)


def kernel(x, c, ctx, c_ctx, ln_in_g, ln_in_b, w_ada, b_ada, w_in, b_in, s5_lambda_re, s5_lambda_im, s5_log_dt, s5_b_re, s5_b_im, s5_c_re, s5_c_im, s5_d, w_glu, b_glu, w_br_s5, w_br_fft, b_br_fft, w_out, b_out, ln1_g, ln1_b, w_router, b_router, w_up, b_up, w_down, b_down, ln2_g, ln2_b):
    assert x.shape == (1, N_TOK, D) and ctx.shape == (1, N_CTX, D) and w_ada.shape[0] == 1
    row = lambda v: v.reshape(1, -1).astype(F32)

    cc = jnp.concatenate([c.reshape(1, D), c_ctx.reshape(1, D), jnp.zeros((SUBLANES - 2, D), F32)], axis=0)
    ada = _ada(cc, w_ada[0], row(b_ada[0]))
    sh1, sc1, g1, sh2, sc2, g2 = (ada[0:1, k * D:(k + 1) * D] for k in range(6))
    sh1c, sc1c = ada[1:2, 0:D], ada[1:2, D:2 * D]

    emb_r, emb_c = _sincos_tables()
    st1, st2, fc = _dft_tables()
    lg, lb = row(ln_in_g), row(ln_in_b)

    w_s5 = w_in[0][:, :S5_WIDTH]
    w_fft = w_in[0][:, S5_WIDTH:S5_WIDTH + FFT_WIDTH]
    w_g = w_in[0][:, S5_WIDTH + FFT_WIDTH:]
    b_s5 = row(b_in[0][:S5_WIDTH])
    b_fft8 = jnp.concatenate([row(b_in[0][S5_WIDTH:S5_WIDTH + FFT_WIDTH]),
                              jnp.zeros((SUBLANES - 1, FFT_WIDTH), F32)], axis=0)
    b_g = row(b_in[0][S5_WIDTH + FFT_WIDTH:])
    w_fc, b_fc = _fft_weights(w_fft, b_fft8, fc)
    wcat = jnp.concatenate([w_s5, w_fc], axis=1).astype(BF16)
    bcat = jnp.concatenate([b_s5, b_fc[0:1]], axis=1)

    x2 = x[0]
    p_t, xr, xi = _proj(x2, emb_r, emb_c, lg, lb, 1.0 + sc1, sh1, wcat, bcat)
    pc_t = _ctx_proj(ctx[0], lg, lb, 1.0 + sc1c, sh1c, w_s5.astype(BF16), b_s5)

    w_m, w_q, w_p, trans, ctx_w = _s5_tables(
        s5_lambda_re[0], s5_lambda_im[0], s5_log_dt[0], s5_b_re[0], s5_b_im[0],
        s5_c_re[0], s5_c_im[0], s5_d[0])
    y_t = _s5(p_t, pc_t, w_m, w_q, w_p, trans, ctx_w)

    yr, yi = _fft1(xr, xi, st1)
    zr = _fft2(yr, yi, st2)

    tri = (jnp.arange(TM)[:, None] < jnp.arange(TM)[None, :]).astype(BF16)
    br = jnp.broadcast_to(b_router[0].reshape(N_EXPERTS, 1), (N_EXPERTS, LANES))
    h1, u2, idx_t, rank_t, gate_tok, counts = _mix(
        x2, emb_r, emb_c, lg, lb, 1.0 + sc1, sh1, w_g.astype(BF16), b_g, y_t, zr,
        w_glu[0].astype(BF16), row(b_glu[0]), w_br_s5[0].astype(BF16), w_br_fft[0].astype(BF16),
        row(b_br_fft[0]), w_out[0].astype(BF16), row(b_out[0]), g1, row(ln1_g[0]), row(ln1_b[0]),
        1.0 + sc2, sh2, jnp.transpose(w_router[0]), br, tri)

    cnt = counts[:, 0].astype(jnp.int32)
    padded = (cnt + BM - 1) // BM * BM
    pad_ends = jnp.cumsum(padded)
    pad_starts = pad_ends - padded
    dest_t = jnp.take(pad_starts, idx_t, axis=0) + rank_t
    block_start = jnp.arange(N_BLOCKS, dtype=jnp.int32) * BM
    block_expert = jnp.minimum(jnp.sum(block_start[:, None] >= pad_ends[None, :], axis=1),
                               N_EXPERTS - 1).astype(jnp.int32)
    n_used = (pad_ends[-1:] // BM).astype(jnp.int32)

    buf = _dispatch(pad_ends.astype(jnp.int32), padded.astype(jnp.int32), _tile_slots(dest_t, TM), u2)
    y_buf = _ffn(block_expert, n_used, buf, w_up[0], b_up[0].reshape(N_EXPERTS, 1, 2 * D),
                 w_down[0], b_down[0].reshape(N_EXPERTS, 1, D))
    out = _combine(_tile_slots(dest_t, TM_COMB), y_buf, h1, gate_tok, g2, row(ln2_g[0]), row(ln2_b[0]))
    return out.reshape(1, N_TOK, D)
```

```python
import functools
import math

import jax
import jax.numpy as jnp
import numpy as np
from jax import lax
from jax.experimental import pallas as pl
from jax.experimental.pallas import tpu as pltpu

F32 = jnp.float32
BF16 = jnp.bfloat16
HI = lax.Precision.HIGHEST

D = 1024
N_TOK = 16384
N_CTX = 256
GRID_W = 64
S5_GROUP = 16
S5_GROUPS = 32
S5_STATE = 64
S5_WIDTH = 512
FFT_GROUPS = 4
FFT_DIM = 128
FFT_WIDTH = 512
N_EXPERTS = 32
TOP_K = 4
LN_EPS = 1e-5
ALPHA = 2.0 ** 0.25
SWIGLU_ALPHA = 1.702
SWIGLU_LIMIT = 7.0

LANES = 128
SUBLANES = 8
VMEM_LIMIT = 56 * 1024 * 1024

CH = 8
N_CHUNK = N_TOK // CH
N_CHUNK_CTX = N_CTX // CH
NSEG = SUBLANES
SEG = N_CHUNK // NSEG
PITCH = SEG + 8
GPT = LANES // S5_GROUP
NJ = S5_WIDTH // LANES
CL = CH * LANES
SW = 4 * GPT * S5_STATE

FN = 128
FB = 4

TM = 512
TM_COMB = 256
BM = 256
N_SLOTS = N_TOK * TOP_K
N_BLOCKS = N_SLOTS // BM + N_EXPERTS
ROWS = N_BLOCKS * BM


def _cparams(sem):
    return pltpu.CompilerParams(dimension_semantics=sem, vmem_limit_bytes=VMEM_LIMIT)


def _layer_norm(x, g, b):
    mu = jnp.mean(x, axis=-1, keepdims=True)
    xc = x - mu
    var = jnp.mean(xc * xc, axis=-1, keepdims=True)
    return xc * lax.rsqrt(var + LN_EPS) * g + b


def _sigmoid(x):
    return 1.0 / (1.0 + jnp.exp(-x))


def _ada_kernel(c_ref, w_ref, b_ref, o_ref):
    c = c_ref[...]
    s = c * _sigmoid(c)
    o_ref[...] = jnp.dot(s, w_ref[...], preferred_element_type=F32, precision=HI) + b_ref[...]


def _ada(cc, w_ada, b_ada):
    nb = 4
    wb = 6 * D // nb
    return pl.pallas_call(
        _ada_kernel,
        grid=(nb,),
        in_specs=[pl.BlockSpec((SUBLANES, D), lambda i: (0, 0)),
                  pl.BlockSpec((D, wb), lambda i: (0, i)),
                  pl.BlockSpec((1, wb), lambda i: (0, i))],
        out_specs=pl.BlockSpec((SUBLANES, wb), lambda i: (0, i)),
        out_shape=jax.ShapeDtypeStruct((SUBLANES, 6 * D), F32),
        compiler_params=_cparams(("parallel",)),
        name="ada",
    )(cc, w_ada, b_ada)


def _fftw_kernel(w_ref, b_ref, f_ref, wo_ref, bo_ref):
    f = f_ref[...]
    wo_ref[...] = jnp.dot(w_ref[...], f, preferred_element_type=F32, precision=HI)
    bo_ref[...] = jnp.dot(b_ref[...], f, preferred_element_type=F32, precision=HI)


def _fft_weights(w_fft, b_fft8, fc):
    return pl.pallas_call(
        _fftw_kernel,
        out_shape=(jax.ShapeDtypeStruct((D, 2 * FFT_WIDTH), F32),
                   jax.ShapeDtypeStruct((SUBLANES, 2 * FFT_WIDTH), F32)),
        compiler_params=pltpu.CompilerParams(vmem_limit_bytes=VMEM_LIMIT),
        name="fftw",
    )(w_fft, b_fft8, fc)


def _pos_code(er_ref, ec_ref, tm):
    nr = tm // GRID_W
    er = er_ref[...]
    row = jnp.broadcast_to(er[:, None, :], (nr, GRID_W, D // 2)).reshape(tm, D // 2)
    col = jnp.concatenate([ec_ref[...]] * nr, axis=0)
    return jnp.concatenate([row, col], axis=-1)


def _to_chunk_major(val, scr_ref, out_ref, tm):
    for j in range(NJ):
        scr_ref[j] = val[:, j * LANES:(j + 1) * LANES]
    for t in range(CH):
        for j in range(NJ):
            piece = scr_ref[j, pl.ds(t, tm // CH, stride=CH), :]
            out_ref[t, :, j * LANES:(j + 1) * LANES] = piece.astype(out_ref.dtype)


def _proj_kernel(x_ref, er_ref, ec_ref, lg_ref, lb_ref, m_ref, s_ref, w_ref, b_ref,
                 p_ref, xr_ref, xi_ref, scr_ref):
    x = x_ref[...] + _pos_code(er_ref, ec_ref, TM)
    h = _layer_norm(x, lg_ref[...], lb_ref[...])
    u = (h * m_ref[...] + s_ref[...]).astype(BF16)
    p = jnp.dot(u, w_ref[...], preferred_element_type=F32) + b_ref[...]
    _to_chunk_major(p[:, :S5_WIDTH], scr_ref, p_ref, TM)
    xr_ref[...] = p[:, S5_WIDTH:S5_WIDTH + FFT_WIDTH].astype(BF16)
    xi_ref[...] = p[:, S5_WIDTH + FFT_WIDTH:].astype(BF16)


def _proj(x, emb_r, emb_c, lg, lb, m1, s1, wcat, bcat):
    nw = wcat.shape[1]
    vec = pl.BlockSpec((1, D), lambda i: (0, 0))
    return pl.pallas_call(
        _proj_kernel,
        grid=(N_TOK // TM,),
        in_specs=[pl.BlockSpec((TM, D), lambda i: (i, 0)),
                  pl.BlockSpec((TM // GRID_W, D // 2), lambda i: (i, 0)),
                  pl.BlockSpec((GRID_W, D // 2), lambda i: (0, 0)),
                  vec, vec, vec, vec,
                  pl.BlockSpec((D, nw), lambda i: (0, 0)),
                  pl.BlockSpec((1, nw), lambda i: (0, 0))],
        out_specs=(pl.BlockSpec((CH, TM // CH, S5_WIDTH), lambda i: (0, i, 0)),
                   pl.BlockSpec((TM, FFT_WIDTH), lambda i: (i, 0)),
                   pl.BlockSpec((TM, FFT_WIDTH), lambda i: (i, 0))),
        out_shape=(jax.ShapeDtypeStruct((CH, N_CHUNK, S5_WIDTH), BF16),
                   jax.ShapeDtypeStruct((N_TOK, FFT_WIDTH), BF16),
                   jax.ShapeDtypeStruct((N_TOK, FFT_WIDTH), BF16)),
        scratch_shapes=[pltpu.VMEM((NJ, TM, LANES), F32)],
        compiler_params=_cparams(("parallel",)),
        name="proj",
    )(x, emb_r, emb_c, lg, lb, m1, s1, wcat, bcat)


def _ctx_proj_kernel(x_ref, lg_ref, lb_ref, m_ref, s_ref, w_ref, b_ref, p_ref, scr_ref):
    h = _layer_norm(x_ref[...], lg_ref[...], lb_ref[...])
    u = (h * m_ref[...] + s_ref[...]).astype(BF16)
    p = jnp.dot(u, w_ref[...], preferred_element_type=F32) + b_ref[...]
    _to_chunk_major(p, scr_ref, p_ref, N_CTX)


def _ctx_proj(ctx, lg, lb, m1, s1, w_s5, b_s5):
    return pl.pallas_call(
        _ctx_proj_kernel,
        out_shape=jax.ShapeDtypeStruct((CH, N_CHUNK_CTX, S5_WIDTH), BF16),
        scratch_shapes=[pltpu.VMEM((NJ, N_CTX, LANES), F32)],
        compiler_params=pltpu.CompilerParams(vmem_limit_bytes=VMEM_LIMIT),
        name="ctxproj",
    )(ctx, lg, lb, m1, s1, w_s5, b_s5)


def _s5_tables(lam_re, lam_im, log_dt, b_re, b_im, c_re, c_im, d_skip):
    dt = jnp.exp(log_dt)[..., None]
    zr = lam_re * dt
    zi = lam_im * dt

    def apow(m):
        m = jnp.asarray(m, F32)
        mag = jnp.exp(zr[..., None] * m)
        return mag * jnp.cos(zi[..., None] * m), mag * jnp.sin(zi[..., None] * m)

    a_re, a_im = apow(jnp.ones((1,), F32))
    a_re, a_im = a_re[..., 0], a_im[..., 0]
    den = lam_re * lam_re + lam_im * lam_im
    num_re = a_re - 1.0
    k_re = (num_re * lam_re + a_im * lam_im) / den
    k_im = (a_im * lam_re - num_re * lam_im) / den
    bb_re = k_re[..., None] * b_re - k_im[..., None] * b_im
    bb_im = k_re[..., None] * b_im + k_im[..., None] * b_re

    ks = jnp.arange(CH + 1, dtype=F32)
    pw_re, pw_im = apow(ks)
    eye = jnp.eye(GPT, dtype=F32)

    cb_rr = jnp.einsum('dghp,dgpk,dgpi->dgkhi', c_re, pw_re[..., :CH], bb_re, precision=HI)
    cb_ii = jnp.einsum('dghp,dgpk,dgpi->dgkhi', c_re, pw_im[..., :CH], bb_im, precision=HI)
    cb_ir = jnp.einsum('dghp,dgpk,dgpi->dgkhi', c_im, pw_re[..., :CH], bb_im, precision=HI)
    cb_ri = jnp.einsum('dghp,dgpk,dgpi->dgkhi', c_im, pw_im[..., :CH], bb_re, precision=HI)
    taps = cb_rr - cb_ii - cb_ir - cb_ri
    t_in = jnp.arange(CH)[:, None]
    t_out = jnp.arange(CH)[None, :]
    lag = jnp.abs(t_out - t_in)
    tf = jnp.where((t_out >= t_in)[None, :, :, None, None], taps[0][:, lag], 0.0)
    tb = jnp.where((t_out <= t_in)[None, :, :, None, None], taps[1][:, lag], 0.0)
    mt = jnp.transpose(tf + tb, (0, 1, 4, 2, 3))
    skip = (jnp.eye(CH, dtype=F32)[None, :, None, :, None]
            * jnp.eye(S5_GROUP, dtype=F32)[None, None, :, None, :]
            * d_skip.reshape(S5_GROUPS, 1, S5_GROUP, 1, 1))
    mt = (mt + skip).reshape(NJ, GPT, CH, S5_GROUP, CH * S5_GROUP)
    a_m = jnp.transpose(mt, (0, 2, 1, 3, 4)).reshape(NJ, CL, CH * S5_GROUP)

    ef = (CH - 1) - jnp.arange(CH)
    eb = jnp.arange(CH)

    def q_part(d, e):
        pr = jnp.transpose(pw_re[d][..., e], (0, 2, 1))[:, :, None, :]
        pi = jnp.transpose(pw_im[d][..., e], (0, 2, 1))[:, :, None, :]
        br = jnp.transpose(bb_re[d], (0, 2, 1))[:, None, :, :]
        bi = jnp.transpose(bb_im[d], (0, 2, 1))[:, None, :, :]
        return pr * br - pi * bi, pr * bi + pi * br
    qfr, qfi = q_part(0, ef)
    qbr, qbi = q_part(1, eb)
    q = jnp.stack([qfr, qfi, qbr, qbi], axis=0).reshape(4, NJ, GPT, CH, S5_GROUP, S5_STATE)
    a_q = jnp.transpose(q, (1, 3, 2, 4, 0, 5)).reshape(NJ, CL, 4 * S5_STATE)

    of = jnp.arange(CH) + 1
    ob = CH - jnp.arange(CH)

    def p_part(d, e):
        pr = pw_re[d][..., e][:, :, :, None]
        pi = pw_im[d][..., e][:, :, :, None]
        cr = jnp.transpose(c_re[d], (0, 2, 1))[:, :, None, :]
        ci = jnp.transpose(c_im[d], (0, 2, 1))[:, :, None, :]
        return cr * pr - ci * pi, -(cr * pi + ci * pr)
    pfr, pfi = p_part(0, of)
    pbr, pbi = p_part(1, ob)
    p = jnp.stack([pfr, pfi, pbr, pbi], axis=0).reshape(4, NJ, GPT, S5_STATE, CH, S5_GROUP)
    a_p = jnp.transpose(p, (1, 0, 2, 3, 4, 5)).reshape(NJ, SW, CH * S5_GROUP)

    def lanes(v):
        return jnp.transpose(v.reshape(2, NJ, GPT * S5_STATE), (1, 0, 2))
    c_r, c_i = apow(jnp.full((1,), float(CH), F32))
    s_r, s_i = apow(jnp.full((1,), float(CH * SEG), F32))
    cr, ci, sr, si = (lanes(v[..., 0]) for v in (c_r, c_i, s_r, s_i))
    trans = jnp.stack([cr[:, 0], ci[:, 0], cr[:, 1], ci[:, 1],
                       sr[:, 0], si[:, 0], sr[:, 1], si[:, 1]], axis=1)

    cidx = jnp.arange(N_CHUNK_CTX, dtype=F32)
    wf_r, wf_i = apow(CH * (N_CHUNK_CTX - 1 - cidx))
    wb_r, wb_i = apow(CH * cidx)

    def ctx_lanes(v, d):
        return jnp.transpose(v[d].reshape(NJ, GPT * S5_STATE, N_CHUNK_CTX), (0, 2, 1))
    ctx_w = jnp.stack([ctx_lanes(wf_r, 0), ctx_lanes(wf_i, 0),
                       ctx_lanes(wb_r, 1), ctx_lanes(wb_i, 1)], axis=1)
    return a_m.astype(BF16), a_q.astype(BF16), a_p.astype(BF16), trans, ctx_w


def _s5w_kernel(am_ref, aq_ref, ap_ref, cm_ref, cq_ref, wm_ref, wq_ref, wp_ref):
    def expand(a, c, row_shift, col_shift):
        w = jnp.dot(a, c, preferred_element_type=F32)
        rg = (lax.broadcasted_iota(jnp.int32, (w.shape[0], 1), 0) >> row_shift) & (GPT - 1)
        cg = (lax.broadcasted_iota(jnp.int32, (1, w.shape[1]), 1) >> col_shift) & (GPT - 1)
        return jnp.where(rg == cg, w, 0.0).astype(BF16)
    wm_ref[0] = expand(am_ref[0], cm_ref[...], 4, 4)
    wq_ref[0] = expand(aq_ref[0], cq_ref[...], 4, 6)
    wp_ref[0] = expand(ap_ref[0], cm_ref[...], 6, 4)


def _s5_expand(a_m, a_q, a_p):
    c_m = np.kron(np.eye(CH), np.kron(np.ones((1, GPT)), np.eye(S5_GROUP)))
    c_q = np.kron(np.eye(4), np.kron(np.ones((1, GPT)), np.eye(S5_STATE)))
    c_m = jnp.asarray(c_m, F32).astype(BF16)
    c_q = jnp.asarray(c_q, F32).astype(BF16)
    return pl.pallas_call(
        _s5w_kernel,
        grid=(NJ,),
        in_specs=[pl.BlockSpec((1, CL, CH * S5_GROUP), lambda j: (j, 0, 0)),
                  pl.BlockSpec((1, CL, 4 * S5_STATE), lambda j: (j, 0, 0)),
                  pl.BlockSpec((1, SW, CH * S5_GROUP), lambda j: (j, 0, 0)),
                  pl.BlockSpec(c_m.shape, lambda j: (0, 0)),
                  pl.BlockSpec(c_q.shape, lambda j: (0, 0))],
        out_specs=(pl.BlockSpec((1, CL, CL), lambda j: (j, 0, 0)),
                   pl.BlockSpec((1, CL, SW), lambda j: (j, 0, 0)),
                   pl.BlockSpec((1, SW, CL), lambda j: (j, 0, 0))),
        out_shape=(jax.ShapeDtypeStruct((NJ, CL, CL), BF16),
                   jax.ShapeDtypeStruct((NJ, CL, SW), BF16),
                   jax.ShapeDtypeStruct((NJ, SW, CL), BF16)),
        compiler_params=_cparams(("parallel",)),
        name="s5w",
    )(a_m, a_q, a_p, c_m, c_q)


def _s5_kernel(p_ref, pc_ref, wm_ref, wq_ref, wp_ref, tr_ref, cw_ref, y_ref, v_ref):
    nq = NJ
    half = GPT * S5_STATE

    def chunk_rows(ref, r0, nrows):
        return jnp.concatenate([ref[t, pl.ds(r0, nrows), :] for t in range(CH)], axis=-1)

    def fill(k, c):
        r0 = pl.multiple_of(k * SEG, SEG)
        v = jnp.dot(chunk_rows(p_ref, r0, SEG), wq_ref[0], preferred_element_type=F32)
        o0 = pl.multiple_of(k * PITCH, SUBLANES)
        for s in range(4 * nq):
            v_ref[s, pl.ds(o0, SEG), :] = v[:, s * LANES:(s + 1) * LANES]
        return c
    lax.fori_loop(0, NSEG, fill, 0)

    vc = jnp.dot(chunk_rows(pc_ref, 0, N_CHUNK_CTX), wq_ref[0], preferred_element_type=F32)
    vfr, vfi, vbr, vbi = (vc[:, i * half:(i + 1) * half] for i in range(4))
    wfr, wfi, wbr, wbi = (cw_ref[0, i] for i in range(4))
    s0_fr = jnp.sum(wfr * vfr - wfi * vfi, axis=0, keepdims=True)
    s0_fi = jnp.sum(wfr * vfi + wfi * vfr, axis=0, keepdims=True)
    s0_br = jnp.sum(wbr * vbr - wbi * vbi, axis=0, keepdims=True)
    s0_bi = jnp.sum(wbr * vbi + wbi * vbr, axis=0, keepdims=True)

    tr = tr_ref[0]
    afr, afi, abr, abi = (jnp.broadcast_to(tr[i:i + 1], (NSEG, half)) for i in range(4))
    gfr, gfi, gbr, gbi = (tr[i:i + 1] for i in range(4, 8))

    def load_part(part, i):
        return jnp.concatenate(
            [v_ref[part * nq + q, pl.ds(i, NSEG, stride=PITCH), :] for q in range(nq)], axis=-1)

    def store_part(part, i, val):
        for q in range(nq):
            v_ref[part * nq + q, pl.ds(i, NSEG, stride=PITCH), :] = val[:, q * LANES:(q + 1) * LANES]

    def step(i, carry, write):
        fr, fi, br, bi = carry
        ib = SEG - 1 - i
        ufr, ufi = load_part(0, i), load_part(1, i)
        ubr, ubi = load_part(2, ib), load_part(3, ib)
        if write:
            store_part(0, i, fr)
            store_part(1, i, fi)
            store_part(2, ib, br)
            store_part(3, ib, bi)
        return (afr * fr - afi * fi + ufr, afr * fi + afi * fr + ufi,
                abr * br - abi * bi + ubr, abr * bi + abi * br + ubi)

    zero = jnp.zeros((NSEG, half), F32)
    ffr, ffi, fbr, fbi = lax.fori_loop(0, SEG, functools.partial(step, write=False),
                                       (zero, zero, zero, zero))

    rows_fr, rows_fi = [s0_fr], [s0_fi]
    for k in range(1, NSEG):
        pr, pi = rows_fr[-1], rows_fi[-1]
        rows_fr.append(gfr * pr - gfi * pi + ffr[k - 1:k])
        rows_fi.append(gfr * pi + gfi * pr + ffi[k - 1:k])
    rows_br, rows_bi = [s0_br], [s0_bi]
    for k in range(NSEG - 2, -1, -1):
        pr, pi = rows_br[0], rows_bi[0]
        rows_br.insert(0, gbr * pr - gbi * pi + fbr[k + 1:k + 2])
        rows_bi.insert(0, gbr * pi + gbi * pr + fbi[k + 1:k + 2])
    init = tuple(jnp.concatenate(r, axis=0) for r in (rows_fr, rows_fi, rows_br, rows_bi))

    lax.fori_loop(0, SEG, functools.partial(step, write=True), init)

    def emit(k, c):
        r0 = pl.multiple_of(k * SEG, SEG)
        o0 = pl.multiple_of(k * PITCH, SUBLANES)
        b = chunk_rows(p_ref, r0, SEG)
        sin = jnp.concatenate([v_ref[s, pl.ds(o0, SEG), :] for s in range(4 * nq)], axis=-1)
        y = (jnp.dot(b, wm_ref[0], preferred_element_type=F32)
             + jnp.dot(sin.astype(BF16), wp_ref[0], preferred_element_type=F32))
        for t in range(CH):
            y_ref[t, pl.ds(r0, SEG), :] = y[:, t * LANES:(t + 1) * LANES].astype(y_ref.dtype)
        return c
    lax.fori_loop(0, NSEG, emit, 0)


def _s5(p_t, pc_t, w_m, w_q, w_p, trans, ctx_w):
    one = pl.Buffered(1)
    return pl.pallas_call(
        _s5_kernel,
        grid=(NJ,),
        in_specs=[pl.BlockSpec((CH, N_CHUNK, LANES), lambda j: (0, 0, j)),
                  pl.BlockSpec((CH, N_CHUNK_CTX, LANES), lambda j: (0, 0, j)),
                  pl.BlockSpec((1, CL, CL), lambda j: (j, 0, 0), pipeline_mode=one),
                  pl.BlockSpec((1, CL, SW), lambda j: (j, 0, 0), pipeline_mode=one),
                  pl.BlockSpec((1, SW, CL), lambda j: (j, 0, 0), pipeline_mode=one),
                  pl.BlockSpec((1, SUBLANES, GPT * S5_STATE), lambda j: (j, 0, 0)),
                  pl.BlockSpec((1, 4, N_CHUNK_CTX, GPT * S5_STATE), lambda j: (j, 0, 0, 0))],
        out_specs=pl.BlockSpec((CH, N_CHUNK, LANES), lambda j: (0, 0, j)),
        out_shape=jax.ShapeDtypeStruct((CH, N_CHUNK, S5_WIDTH), BF16),
        scratch_shapes=[pltpu.VMEM((4 * NJ, NSEG * PITCH, LANES), F32)],
        compiler_params=_cparams(("parallel",)),
        name="s5",
    )(p_t, pc_t, w_m, w_q, w_p, trans, ctx_w)


def _dft_tables():
    n = np.arange(FN)
    ang = 2.0 * np.pi * np.outer(n, n) / FN
    c, s = np.cos(ang), np.sin(ang)
    st1 = np.block([[c, s], [-s, c]])
    tw = 2.0 * np.pi * np.outer(n, n) / (FN * FN)
    wr, wi = np.cos(tw), -np.sin(tw)
    fr = c[None] * wr[:, None, :] + s[None] * wi[:, None, :]
    fi = c[None] * wi[:, None, :] - s[None] * wr[:, None, :]
    st2 = np.concatenate([fr, -fi], axis=-1)
    scale = 1.0 / math.sqrt(N_TOK * FFT_DIM)
    blk_c = np.kron(np.eye(FFT_GROUPS), c) * scale
    blk_s = np.kron(np.eye(FFT_GROUPS), s) * scale
    fc = np.concatenate([blk_c, -blk_s], axis=1)
    return (jnp.asarray(st1, F32).astype(BF16), jnp.asarray(st2, F32).astype(BF16), jnp.asarray(fc, F32))


def _fft1_kernel(xr_ref, xi_ref, f_ref, yr_ref, yi_ref):
    xs = jnp.concatenate([xr_ref[...], xi_ref[...]], axis=0)
    y = jnp.dot(f_ref[...], xs, preferred_element_type=F32)
    yr_ref[...] = y[:FN].astype(BF16)
    yi_ref[...] = y[FN:].astype(BF16)


def _fft1(xr, xi, st1):
    w = FB * FFT_WIDTH
    spec = pl.BlockSpec((FN, w), lambda i: (0, i))
    xr2 = xr.reshape(FN, FN * FFT_WIDTH)
    xi2 = xi.reshape(FN, FN * FFT_WIDTH)
    return pl.pallas_call(
        _fft1_kernel,
        grid=(FN // FB,),
        in_specs=[spec, spec, pl.BlockSpec((2 * FN, 2 * FN), lambda i: (0, 0))],
        out_specs=(spec, spec),
        out_shape=(jax.ShapeDtypeStruct((FN, FN * FFT_WIDTH), BF16),) * 2,
        compiler_params=_cparams(("parallel",)),
        name="fft1",
    )(xr2, xi2, st1)


def _fft2_kernel(yr_ref, yi_ref, f_ref, z_ref):
    for b in range(FB):
        ys = jnp.concatenate([yr_ref[b * FN:(b + 1) * FN, :], yi_ref[b * FN:(b + 1) * FN, :]], axis=0)
        z = jnp.dot(f_ref[b], ys, preferred_element_type=F32)
        z_ref[:, b * FFT_WIDTH:(b + 1) * FFT_WIDTH] = z.astype(BF16)


def _fft2(yr, yi, st2):
    rows = pl.BlockSpec((FB * FN, FFT_WIDTH), lambda i: (i, 0))
    yr2 = yr.reshape(N_TOK, FFT_WIDTH)
    yi2 = yi.reshape(N_TOK, FFT_WIDTH)
    z = pl.pallas_call(
        _fft2_kernel,
        grid=(FN // FB,),
        in_specs=[rows, rows, pl.BlockSpec((FB, FN, 2 * FN), lambda i: (i, 0, 0))],
        out_specs=pl.BlockSpec((FN, FB * FFT_WIDTH), lambda i: (0, i)),
        out_shape=jax.ShapeDtypeStruct((FN, FN * FFT_WIDTH), BF16),
        compiler_params=_cparams(("parallel",)),
        name="fft2",
    )(yr2, yi2, st2)
    return z.reshape(N_TOK, FFT_WIDTH)


def _gelu_tanh(x):
    return 0.5 * x * (1.0 + jnp.tanh(math.sqrt(2.0 / math.pi) * (x + 0.044715 * (x * x * x))))


def _mix_kernel(x_ref, er_ref, ec_ref, lg_ref, lb_ref, m1_ref, s1_ref, wg_ref, bg_ref,
                yt_ref, zr_ref, wglu_ref, bglu_ref, wbs_ref, wbf_ref, bbf_ref, wo_ref, bo_ref,
                g1_ref, l1g_ref, l1b_ref, m2_ref, s2_ref, wr_ref, br_ref, tri_ref,
                h1_ref, u2_ref, idx_ref, rank_ref, gate_ref, cnt_ref, scr_ref, base_ref):
    i = pl.program_id(0)

    @pl.when(i == 0)
    def _():
        base_ref[...] = jnp.zeros_like(base_ref)

    x = x_ref[...] + _pos_code(er_ref, ec_ref, TM)
    h = _layer_norm(x, lg_ref[...], lb_ref[...])
    u = (h * m1_ref[...] + s1_ref[...]).astype(BF16)
    gates = _sigmoid(jnp.dot(u, wg_ref[...], preferred_element_type=F32) + bg_ref[...])

    for t in range(CH):
        for j in range(NJ):
            scr_ref[j, pl.ds(t, TM // CH, stride=CH), :] = (
                yt_ref[t, :, j * LANES:(j + 1) * LANES].astype(F32))
    ys = jnp.concatenate([scr_ref[j] for j in range(NJ)], axis=-1)
    z = jnp.dot(_gelu_tanh(ys).astype(BF16), wglu_ref[...], preferred_element_type=F32) + bglu_ref[...]
    glu = (z[:, :S5_WIDTH] * _sigmoid(z[:, S5_WIDTH:])).astype(BF16)
    y_s5 = jnp.dot(glu, wbs_ref[...], preferred_element_type=F32)
    y_fft = jnp.dot(zr_ref[...], wbf_ref[...], preferred_element_type=F32) + bbf_ref[...]
    mixed = (gates[:, :D] * y_s5 + gates[:, D:] * y_fft).astype(BF16)
    y = jnp.dot(mixed, wo_ref[...], preferred_element_type=F32) + bo_ref[...]
    h1 = _layer_norm(ALPHA * h + g1_ref[...] * y, l1g_ref[...], l1b_ref[...])
    h1_ref[...] = h1
    u2 = h1 * m2_ref[...] + s2_ref[...]
    u2_ref[...] = u2

    logits = lax.dot_general(wr_ref[...], u2, (((1,), (1,)), ((), ())),
                             preferred_element_type=F32, precision=HI) + br_ref[:, 0:1]
    eidx = lax.broadcasted_iota(jnp.int32, (N_EXPERTS, TM), 0)
    vals, idxs, hots = [], [], []
    cur = logits
    for _k in range(TOP_K):
        m = jnp.max(cur, axis=0, keepdims=True)
        sel = jnp.min(jnp.where(cur == m, eidx, N_EXPERTS), axis=0, keepdims=True)
        hot = eidx == sel
        cur = jnp.where(hot, -jnp.inf, cur)
        vals.append(m)
        idxs.append(sel)
        hots.append(hot)
    exps = [jnp.exp(v - vals[0]) for v in vals]
    den = exps[0] + exps[1] + exps[2] + exps[3]
    gate4 = jnp.concatenate([e / den for e in exps], axis=0)
    idx_ref[...] = jnp.concatenate(idxs, axis=0)

    hot_sum = (hots[0] | hots[1] | hots[2] | hots[3]).astype(F32)
    before = jnp.dot(hot_sum.astype(BF16), tri_ref[...], preferred_element_type=F32)
    tot = base_ref[:, 0:1] + before
    rank_ref[...] = jnp.concatenate(
        [jnp.sum(jnp.where(hk, tot, 0.0), axis=0, keepdims=True) for hk in hots], axis=0
    ).astype(jnp.int32)
    base_ref[...] = base_ref[...] + jnp.sum(hot_sum, axis=1, keepdims=True)
    cnt_ref[...] = base_ref[...]

    gpad = jnp.concatenate([gate4, jnp.zeros((LANES - TOP_K, TM), F32)], axis=0)
    gate_ref[...] = gpad.T


def _mix(x, emb_r, emb_c, lg, lb, m1, s1, wg, bg, y_t, zr, wglu, bglu, wbs, wbf, bbf, wo, bo,
         g1, l1g, l1b, m2, s2, wr_t, br, tri):
    vec = pl.BlockSpec((1, D), lambda i: (0, 0))

    def full(a):
        return pl.BlockSpec(a.shape, lambda i: (0,) * a.ndim)
    return pl.pallas_call(
        _mix_kernel,
        grid=(N_TOK // TM,),
        in_specs=[pl.BlockSpec((TM, D), lambda i: (i, 0)),
                  pl.BlockSpec((TM // GRID_W, D // 2), lambda i: (i, 0)),
                  pl.BlockSpec((GRID_W, D // 2), lambda i: (0, 0)),
                  vec, vec, vec, vec, full(wg), full(bg),
                  pl.BlockSpec((CH, TM // CH, S5_WIDTH), lambda i: (0, i, 0)),
                  pl.BlockSpec((TM, FFT_WIDTH), lambda i: (i, 0)),
                  full(wglu), full(bglu), full(wbs), full(wbf), full(bbf), full(wo), full(bo),
                  vec, vec, vec, vec, vec, full(wr_t), full(br), full(tri)],
        out_specs=(pl.BlockSpec((TM, D), lambda i: (i, 0)),
                   pl.BlockSpec((TM, D), lambda i: (i, 0)),
                   pl.BlockSpec((TOP_K, TM), lambda i: (0, i)),
                   pl.BlockSpec((TOP_K, TM), lambda i: (0, i)),
                   pl.BlockSpec((TM, LANES), lambda i: (i, 0)),
                   pl.BlockSpec((N_EXPERTS, LANES), lambda i: (0, 0))),
        out_shape=(jax.ShapeDtypeStruct((N_TOK, D), F32),
                   jax.ShapeDtypeStruct((N_TOK, D), F32),
                   jax.ShapeDtypeStruct((TOP_K, N_TOK), jnp.int32),
                   jax.ShapeDtypeStruct((TOP_K, N_TOK), jnp.int32),
                   jax.ShapeDtypeStruct((N_TOK, LANES), F32),
                   jax.ShapeDtypeStruct((N_EXPERTS, LANES), F32)),
        scratch_shapes=[pltpu.VMEM((NJ, TM, LANES), F32),
                        pltpu.VMEM((N_EXPERTS, LANES), F32)],
        compiler_params=_cparams(("arbitrary",)),
        name="mix",
    )(x, emb_r, emb_c, lg, lb, m1, s1, wg, bg, y_t, zr, wglu, bglu, wbs, wbf, bbf, wo, bo,
      g1, l1g, l1b, m2, s2, wr_t, br, tri)


def _row_copy(src, dst, sem):
    return pltpu.make_async_copy(src, dst, sem)


def _dispatch_kernel(pend_ref, padded_ref, dest_ref, u_ref, buf_ref, zero_ref, zsem, sem):
    i = pl.program_id(0)

    @pl.when(i == 0)
    def _():
        zero_ref[...] = jnp.zeros_like(zero_ref)
        n_used = pend_ref[N_EXPERTS - 1] // BM

        def clear_copy(start):
            return _row_copy(zero_ref, buf_ref.at[pl.ds(pl.multiple_of(start, BM), BM)], zsem)

        def each(fn):
            def expert(e, c):
                @pl.when(padded_ref[e] > 0)
                def _():
                    fn(clear_copy(pend_ref[e] - BM))
                return c
            lax.fori_loop(0, N_EXPERTS, expert, 0)

            def tail(b, c):
                fn(clear_copy(b * BM))
                return c
            lax.fori_loop(n_used, N_BLOCKS, tail, 0)
        each(lambda cp: cp.start())
        each(lambda cp: cp.wait())

    def issue(t, c):
        for k in range(TOP_K):
            d = dest_ref[0, 0, k * TM + t]
            _row_copy(u_ref.at[pl.ds(t, 1)], buf_ref.at[pl.ds(d, 1)], sem).start()
        return c
    lax.fori_loop(0, TM, issue, 0)
    for _k in range(TOP_K):
        _row_copy(u_ref, buf_ref.at[pl.ds(0, TM)], sem).wait()


def _dispatch(pad_ends, padded, dest_tiles, u2):
    return pl.pallas_call(
        _dispatch_kernel,
        grid_spec=pltpu.PrefetchScalarGridSpec(
            num_scalar_prefetch=2,
            grid=(N_TOK // TM,),
            in_specs=[pl.BlockSpec((1, 1, TOP_K * TM), lambda i, a, b: (i, 0, 0),
                                   memory_space=pltpu.SMEM),
                      pl.BlockSpec((TM, D), lambda i, a, b: (i, 0))],
            out_specs=pl.BlockSpec(memory_space=pl.ANY),
            scratch_shapes=[pltpu.VMEM((BM, D), F32),
                            pltpu.SemaphoreType.DMA(()),
                            pltpu.SemaphoreType.DMA(())]),
        out_shape=jax.ShapeDtypeStruct((ROWS, D), F32),
        compiler_params=_cparams(("arbitrary",)),
        name="dispatch",
    )(pad_ends, padded, dest_tiles, u2)


def _ffn_kernel(be_ref, nu_ref, x_ref, wu_ref, bu_ref, wd_ref, bd_ref, y_ref, wub_ref, wdb_ref):
    i = pl.program_id(0)
    used = i < nu_ref[0]

    @pl.when(used)
    def _():
        prev = be_ref[jnp.maximum(i - 1, 0)]

        @pl.when((i == 0) | (be_ref[i] != prev))
        def _():
            wub_ref[...] = wu_ref[0].astype(BF16)
            wdb_ref[...] = wd_ref[0].astype(BF16)

        h = jnp.dot(x_ref[...].astype(BF16), wub_ref[...], preferred_element_type=F32) + bu_ref[0]
        h_glu = jnp.minimum(h[:, :D], SWIGLU_LIMIT)
        h_lin = jnp.clip(h[:, D:], -SWIGLU_LIMIT, SWIGLU_LIMIT)
        act = (h_glu * _sigmoid(SWIGLU_ALPHA * h_glu) * (h_lin + 1.0)).astype(BF16)
        y_ref[...] = jnp.dot(act, wdb_ref[...], preferred_element_type=F32) + bd_ref[0]

    @pl.when(jnp.logical_not(used))
    def _():
        y_ref[...] = jnp.zeros_like(y_ref)


def _ffn(block_expert, n_used, buf, w_up, b_up, w_down, b_down):
    def blk(i, be, nu):
        return jnp.minimum(i, nu[0] - 1)
    return pl.pallas_call(
        _ffn_kernel,
        grid_spec=pltpu.PrefetchScalarGridSpec(
            num_scalar_prefetch=2,
            grid=(N_BLOCKS,),
            in_specs=[pl.BlockSpec((BM, D), lambda i, be, nu: (blk(i, be, nu), 0)),
                      pl.BlockSpec((1, D, 2 * D), lambda i, be, nu: (be[blk(i, be, nu)], 0, 0)),
                      pl.BlockSpec((1, 1, 2 * D), lambda i, be, nu: (be[blk(i, be, nu)], 0, 0)),
                      pl.BlockSpec((1, D, D), lambda i, be, nu: (be[blk(i, be, nu)], 0, 0)),
                      pl.BlockSpec((1, 1, D), lambda i, be, nu: (be[blk(i, be, nu)], 0, 0))],
            out_specs=pl.BlockSpec((BM, D), lambda i, be, nu: (i, 0)),
            scratch_shapes=[pltpu.VMEM((D, 2 * D), BF16),
                            pltpu.VMEM((D, D), BF16)]),
        out_shape=jax.ShapeDtypeStruct((ROWS, D), F32),
        compiler_params=_cparams(("arbitrary",)),
        name="ffn",
    )(block_expert, n_used, buf, w_up, b_up, w_down, b_down)


def _combine_kernel(dest_ref, y_ref, h1_ref, gate_ref, g2_ref, lg_ref, lb_ref, o_ref, rows_ref, sem):
    def issue(t, c):
        for k in range(TOP_K):
            d = dest_ref[0, 0, k * TM_COMB + t]
            _row_copy(y_ref.at[pl.ds(d, 1)], rows_ref.at[k, pl.ds(t, 1)], sem).start()
        return c
    lax.fori_loop(0, TM_COMB, issue, 0)
    for k in range(TOP_K):
        _row_copy(y_ref.at[pl.ds(0, TM_COMB)], rows_ref.at[k], sem).wait()
    gate = gate_ref[...]
    m = rows_ref[0] * gate[:, 0:1]
    for k in range(1, TOP_K):
        m = m + rows_ref[k] * gate[:, k:k + 1]
    o_ref[...] = _layer_norm(ALPHA * h1_ref[...] + g2_ref[...] * m, lg_ref[...], lb_ref[...])


def _combine(dest_tiles, y_buf, h1, gate_tok, g2, lg, lb):
    vec = pl.BlockSpec((1, D), lambda i: (0, 0))
    return pl.pallas_call(
        _combine_kernel,
        grid=(N_TOK // TM_COMB,),
        in_specs=[pl.BlockSpec((1, 1, TOP_K * TM_COMB), lambda i: (i, 0, 0), memory_space=pltpu.SMEM),
                  pl.BlockSpec(memory_space=pl.ANY),
                  pl.BlockSpec((TM_COMB, D), lambda i: (i, 0)),
                  pl.BlockSpec((TM_COMB, LANES), lambda i: (i, 0)),
                  vec, vec, vec],
        out_specs=pl.BlockSpec((TM_COMB, D), lambda i: (i, 0)),
        out_shape=jax.ShapeDtypeStruct((N_TOK, D), F32),
        scratch_shapes=[pltpu.VMEM((TOP_K, TM_COMB, D), F32),
                        pltpu.SemaphoreType.DMA(())],
        compiler_params=_cparams(("parallel",)),
        name="combine",
    )(dest_tiles, y_buf, h1, gate_tok, g2, lg, lb)


def _sincos_tables():
    q = D // 4
    omega = 1.0 / (10000.0 ** (jnp.arange(q, dtype=F32) / q))

    def emb(n):
        ang = jnp.arange(n, dtype=F32)[:, None] * omega[None, :]
        return jnp.concatenate([jnp.sin(ang), jnp.cos(ang)], axis=-1)
    return emb(N_TOK // GRID_W), emb(GRID_W)


def _tile_slots(a, tile):
    return jnp.transpose(a.reshape(TOP_K, N_TOK // tile, tile), (1, 0, 2)).reshape(N_TOK // tile, 1, TOP_K * tile)


def kernel(x, c, ctx, c_ctx, ln_in_g, ln_in_b, w_ada, b_ada, w_in, b_in, s5_lambda_re, s5_lambda_im, s5_log_dt, s5_b_re, s5_b_im, s5_c_re, s5_c_im, s5_d, w_glu, b_glu, w_br_s5, w_br_fft, b_br_fft, w_out, b_out, ln1_g, ln1_b, w_router, b_router, w_up, b_up, w_down, b_down, ln2_g, ln2_b):
    assert x.shape == (1, N_TOK, D) and ctx.shape == (1, N_CTX, D) and w_ada.shape[0] == 1
    row = lambda v: v.reshape(1, -1).astype(F32)

    cc = jnp.concatenate([c.reshape(1, D), c_ctx.reshape(1, D), jnp.zeros((SUBLANES - 2, D), F32)], axis=0)
    ada = _ada(cc, w_ada[0], row(b_ada[0]))
    sh1, sc1, g1, sh2, sc2, g2 = (ada[0:1, k * D:(k + 1) * D] for k in range(6))
    sh1c, sc1c = ada[1:2, 0:D], ada[1:2, D:2 * D]

    emb_r, emb_c = _sincos_tables()
    st1, st2, fc = _dft_tables()
    lg, lb = row(ln_in_g), row(ln_in_b)

    w_s5 = w_in[0][:, :S5_WIDTH]
    w_fft = w_in[0][:, S5_WIDTH:S5_WIDTH + FFT_WIDTH]
    w_g = w_in[0][:, S5_WIDTH + FFT_WIDTH:]
    b_s5 = row(b_in[0][:S5_WIDTH])
    b_fft8 = jnp.concatenate([row(b_in[0][S5_WIDTH:S5_WIDTH + FFT_WIDTH]),
                              jnp.zeros((SUBLANES - 1, FFT_WIDTH), F32)], axis=0)
    b_g = row(b_in[0][S5_WIDTH + FFT_WIDTH:])
    w_fc, b_fc = _fft_weights(w_fft, b_fft8, fc)
    wcat = jnp.concatenate([w_s5, w_fc], axis=1).astype(BF16)
    bcat = jnp.concatenate([b_s5, b_fc[0:1]], axis=1)

    x2 = x[0]
    p_t, xr, xi = _proj(x2, emb_r, emb_c, lg, lb, 1.0 + sc1, sh1, wcat, bcat)
    pc_t = _ctx_proj(ctx[0], lg, lb, 1.0 + sc1c, sh1c, w_s5.astype(BF16), b_s5)

    a_m, a_q, a_p, trans, ctx_w = _s5_tables(
        s5_lambda_re[0], s5_lambda_im[0], s5_log_dt[0], s5_b_re[0], s5_b_im[0],
        s5_c_re[0], s5_c_im[0], s5_d[0])
    w_m, w_q, w_p = _s5_expand(a_m, a_q, a_p)
    y_t = _s5(p_t, pc_t, w_m, w_q, w_p, trans, ctx_w)

    yr, yi = _fft1(xr, xi, st1)
    zr = _fft2(yr, yi, st2)

    tri = (jnp.arange(TM)[:, None] < jnp.arange(TM)[None, :]).astype(BF16)
    br = jnp.broadcast_to(b_router[0].reshape(N_EXPERTS, 1), (N_EXPERTS, LANES))
    h1, u2, idx_t, rank_t, gate_tok, counts = _mix(
        x2, emb_r, emb_c, lg, lb, 1.0 + sc1, sh1, w_g.astype(BF16), b_g, y_t, zr,
        w_glu[0].astype(BF16), row(b_glu[0]), w_br_s5[0].astype(BF16), w_br_fft[0].astype(BF16),
        row(b_br_fft[0]), w_out[0].astype(BF16), row(b_out[0]), g1, row(ln1_g[0]), row(ln1_b[0]),
        1.0 + sc2, sh2, jnp.transpose(w_router[0]), br, tri)

    cnt = counts[:, 0].astype(jnp.int32)
    padded = (cnt + BM - 1) // BM * BM
    pad_ends = jnp.cumsum(padded)
    pad_starts = pad_ends - padded
    own = idx_t[None, :, :] == jnp.arange(N_EXPERTS, dtype=jnp.int32)[:, None, None]
    dest_t = jnp.sum(jnp.where(own, pad_starts[:, None, None], 0), axis=0) + rank_t
    block_start = jnp.arange(N_BLOCKS, dtype=jnp.int32) * BM
    block_expert = jnp.minimum(jnp.sum(block_start[:, None] >= pad_ends[None, :], axis=1),
                               N_EXPERTS - 1).astype(jnp.int32)
    n_used = (pad_ends[-1:] // BM).astype(jnp.int32)

    buf = _dispatch(pad_ends.astype(jnp.int32), padded.astype(jnp.int32), _tile_slots(dest_t, TM), u2)
    y_buf = _ffn(block_expert, n_used, buf, w_up[0], b_up[0].reshape(N_EXPERTS, 1, 2 * D),
                 w_down[0], b_down[0].reshape(N_EXPERTS, 1, D))
    out = _combine(_tile_slots(dest_t, TM_COMB), y_buf, h1, gate_tok, g2, row(ln2_g[0]), row(ln2_b[0]))
    return out.reshape(1, N_TOK, D)
```

```python
import functools
import math

import jax
import jax.numpy as jnp
import numpy as np
from jax import lax
from jax.experimental import pallas as pl
from jax.experimental.pallas import tpu as pltpu

F32 = jnp.float32
BF16 = jnp.bfloat16
HI = lax.Precision.HIGHEST

D = 1024
N_TOK = 16384
N_CTX = 256
GRID_W = 64
S5_GROUP = 16
S5_GROUPS = 32
S5_STATE = 64
S5_WIDTH = 512
FFT_GROUPS = 4
FFT_DIM = 128
FFT_WIDTH = 512
N_EXPERTS = 32
TOP_K = 4
LN_EPS = 1e-5
ALPHA = 2.0 ** 0.25
SWIGLU_ALPHA = 1.702
SWIGLU_LIMIT = 7.0

LANES = 128
SUBLANES = 8
VMEM_LIMIT = 56 * 1024 * 1024

CH = 8
N_CHUNK = N_TOK // CH
N_CHUNK_CTX = N_CTX // CH
NSEG = SUBLANES
SEG = N_CHUNK // NSEG
PITCH = SEG + 8
GPT = LANES // S5_GROUP
NJ = S5_WIDTH // LANES
CL = CH * LANES
SW = 4 * GPT * S5_STATE

FN = 128
FB = 4

TM = 512
TM_COMB = 256
BM = 512
N_SLOTS = N_TOK * TOP_K
N_BLOCKS = N_SLOTS // BM + N_EXPERTS
ROWS = N_BLOCKS * BM


def _cparams(sem):
    return pltpu.CompilerParams(dimension_semantics=sem, vmem_limit_bytes=VMEM_LIMIT)


def _layer_norm(x, g, b):
    mu = jnp.mean(x, axis=-1, keepdims=True)
    xc = x - mu
    var = jnp.mean(xc * xc, axis=-1, keepdims=True)
    return xc * lax.rsqrt(var + LN_EPS) * g + b


def _sigmoid(x):
    return 1.0 / (1.0 + jnp.exp(-x))


def _ada_kernel(c_ref, w_ref, b_ref, o_ref):
    c = c_ref[...]
    s = c * _sigmoid(c)
    o_ref[...] = jnp.dot(s, w_ref[...], preferred_element_type=F32, precision=HI) + b_ref[...]


def _ada(cc, w_ada, b_ada):
    nb = 4
    wb = 6 * D // nb
    return pl.pallas_call(
        _ada_kernel,
        grid=(nb,),
        in_specs=[pl.BlockSpec((SUBLANES, D), lambda i: (0, 0)),
                  pl.BlockSpec((D, wb), lambda i: (0, i)),
                  pl.BlockSpec((1, wb), lambda i: (0, i))],
        out_specs=pl.BlockSpec((SUBLANES, wb), lambda i: (0, i)),
        out_shape=jax.ShapeDtypeStruct((SUBLANES, 6 * D), F32),
        compiler_params=_cparams(("parallel",)),
        name="ada",
    )(cc, w_ada, b_ada)


def _fftw_kernel(w_ref, b_ref, f_ref, wo_ref, bo_ref):
    f = f_ref[...]
    wo_ref[...] = jnp.dot(w_ref[...], f, preferred_element_type=F32, precision=HI)
    bo_ref[...] = jnp.dot(b_ref[...], f, preferred_element_type=F32, precision=HI)


def _fft_weights(w_fft, b_fft8, fc):
    return pl.pallas_call(
        _fftw_kernel,
        out_shape=(jax.ShapeDtypeStruct((D, 2 * FFT_WIDTH), F32),
                   jax.ShapeDtypeStruct((SUBLANES, 2 * FFT_WIDTH), F32)),
        compiler_params=pltpu.CompilerParams(vmem_limit_bytes=VMEM_LIMIT),
        name="fftw",
    )(w_fft, b_fft8, fc)


def _pos_code(er_ref, ec_ref, tm):
    nr = tm // GRID_W
    er = er_ref[...]
    row = jnp.broadcast_to(er[:, None, :], (nr, GRID_W, D // 2)).reshape(tm, D // 2)
    col = jnp.concatenate([ec_ref[...]] * nr, axis=0)
    return jnp.concatenate([row, col], axis=-1)


def _to_chunk_major(val, scr_ref, out_ref, tm):
    for j in range(NJ):
        scr_ref[j] = val[:, j * LANES:(j + 1) * LANES]
    for t in range(CH):
        for j in range(NJ):
            piece = scr_ref[j, pl.ds(t, tm // CH, stride=CH), :]
            out_ref[t, :, j * LANES:(j + 1) * LANES] = piece.astype(out_ref.dtype)


def _proj_kernel(x_ref, er_ref, ec_ref, lg_ref, lb_ref, m_ref, s_ref, w_ref, b_ref,
                 p_ref, xr_ref, xi_ref, scr_ref):
    x = x_ref[...] + _pos_code(er_ref, ec_ref, TM)
    h = _layer_norm(x, lg_ref[...], lb_ref[...])
    u = (h * m_ref[...] + s_ref[...]).astype(BF16)
    p = jnp.dot(u, w_ref[...], preferred_element_type=F32) + b_ref[...]
    _to_chunk_major(p[:, :S5_WIDTH], scr_ref, p_ref, TM)
    xr_ref[...] = p[:, S5_WIDTH:S5_WIDTH + FFT_WIDTH].astype(BF16)
    xi_ref[...] = p[:, S5_WIDTH + FFT_WIDTH:].astype(BF16)


def _proj(x, emb_r, emb_c, lg, lb, m1, s1, wcat, bcat):
    nw = wcat.shape[1]
    vec = pl.BlockSpec((1, D), lambda i: (0, 0))
    return pl.pallas_call(
        _proj_kernel,
        grid=(N_TOK // TM,),
        in_specs=[pl.BlockSpec((TM, D), lambda i: (i, 0)),
                  pl.BlockSpec((TM // GRID_W, D // 2), lambda i: (i, 0)),
                  pl.BlockSpec((GRID_W, D // 2), lambda i: (0, 0)),
                  vec, vec, vec, vec,
                  pl.BlockSpec((D, nw), lambda i: (0, 0)),
                  pl.BlockSpec((1, nw), lambda i: (0, 0))],
        out_specs=(pl.BlockSpec((CH, TM // CH, S5_WIDTH), lambda i: (0, i, 0)),
                   pl.BlockSpec((TM, FFT_WIDTH), lambda i: (i, 0)),
                   pl.BlockSpec((TM, FFT_WIDTH), lambda i: (i, 0))),
        out_shape=(jax.ShapeDtypeStruct((CH, N_CHUNK, S5_WIDTH), BF16),
                   jax.ShapeDtypeStruct((N_TOK, FFT_WIDTH), BF16),
                   jax.ShapeDtypeStruct((N_TOK, FFT_WIDTH), BF16)),
        scratch_shapes=[pltpu.VMEM((NJ, TM, LANES), F32)],
        compiler_params=_cparams(("parallel",)),
        name="proj",
    )(x, emb_r, emb_c, lg, lb, m1, s1, wcat, bcat)


def _ctx_proj_kernel(x_ref, lg_ref, lb_ref, m_ref, s_ref, w_ref, b_ref, p_ref, scr_ref):
    h = _layer_norm(x_ref[...], lg_ref[...], lb_ref[...])
    u = (h * m_ref[...] + s_ref[...]).astype(BF16)
    p = jnp.dot(u, w_ref[...], preferred_element_type=F32) + b_ref[...]
    _to_chunk_major(p, scr_ref, p_ref, N_CTX)


def _ctx_proj(ctx, lg, lb, m1, s1, w_s5, b_s5):
    return pl.pallas_call(
        _ctx_proj_kernel,
        out_shape=jax.ShapeDtypeStruct((CH, N_CHUNK_CTX, S5_WIDTH), BF16),
        scratch_shapes=[pltpu.VMEM((NJ, N_CTX, LANES), F32)],
        compiler_params=pltpu.CompilerParams(vmem_limit_bytes=VMEM_LIMIT),
        name="ctxproj",
    )(ctx, lg, lb, m1, s1, w_s5, b_s5)


def _s5_tables(lam_re, lam_im, log_dt, b_re, b_im, c_re, c_im, d_skip):
    dt = jnp.exp(log_dt)[..., None]
    zr = lam_re * dt
    zi = lam_im * dt

    def apow(m):
        m = jnp.asarray(m, F32)
        mag = jnp.exp(zr[..., None] * m)
        return mag * jnp.cos(zi[..., None] * m), mag * jnp.sin(zi[..., None] * m)

    a_re, a_im = apow(jnp.ones((1,), F32))
    a_re, a_im = a_re[..., 0], a_im[..., 0]
    den = lam_re * lam_re + lam_im * lam_im
    num_re = a_re - 1.0
    k_re = (num_re * lam_re + a_im * lam_im) / den
    k_im = (a_im * lam_re - num_re * lam_im) / den
    bb_re = k_re[..., None] * b_re - k_im[..., None] * b_im
    bb_im = k_re[..., None] * b_im + k_im[..., None] * b_re

    ks = jnp.arange(CH + 1, dtype=F32)
    pw_re, pw_im = apow(ks)
    eye = jnp.eye(GPT, dtype=F32)

    cb_rr = jnp.einsum('dghp,dgpk,dgpi->dgkhi', c_re, pw_re[..., :CH], bb_re, precision=HI)
    cb_ii = jnp.einsum('dghp,dgpk,dgpi->dgkhi', c_re, pw_im[..., :CH], bb_im, precision=HI)
    cb_ir = jnp.einsum('dghp,dgpk,dgpi->dgkhi', c_im, pw_re[..., :CH], bb_im, precision=HI)
    cb_ri = jnp.einsum('dghp,dgpk,dgpi->dgkhi', c_im, pw_im[..., :CH], bb_re, precision=HI)
    taps = cb_rr - cb_ii - cb_ir - cb_ri
    t_in = jnp.arange(CH)[:, None]
    t_out = jnp.arange(CH)[None, :]
    lag = jnp.abs(t_out - t_in)
    tf = jnp.where((t_out >= t_in)[None, :, :, None, None], taps[0][:, lag], 0.0)
    tb = jnp.where((t_out <= t_in)[None, :, :, None, None], taps[1][:, lag], 0.0)
    mt = jnp.transpose(tf + tb, (0, 1, 4, 2, 3))
    skip = (jnp.eye(CH, dtype=F32)[None, :, None, :, None]
            * jnp.eye(S5_GROUP, dtype=F32)[None, None, :, None, :]
            * d_skip.reshape(S5_GROUPS, 1, S5_GROUP, 1, 1))
    mt = (mt + skip).reshape(NJ, GPT, CH, S5_GROUP, CH * S5_GROUP)
    a_m = jnp.transpose(mt, (0, 2, 1, 3, 4)).reshape(NJ, CL, CH * S5_GROUP)

    ef = (CH - 1) - jnp.arange(CH)
    eb = jnp.arange(CH)

    def q_part(d, e):
        pr = jnp.transpose(pw_re[d][..., e], (0, 2, 1))[:, :, None, :]
        pi = jnp.transpose(pw_im[d][..., e], (0, 2, 1))[:, :, None, :]
        br = jnp.transpose(bb_re[d], (0, 2, 1))[:, None, :, :]
        bi = jnp.transpose(bb_im[d], (0, 2, 1))[:, None, :, :]
        return pr * br - pi * bi, pr * bi + pi * br
    qfr, qfi = q_part(0, ef)
    qbr, qbi = q_part(1, eb)
    q = jnp.stack([qfr, qfi, qbr, qbi], axis=0).reshape(4, NJ, GPT, CH, S5_GROUP, S5_STATE)
    a_q = jnp.transpose(q, (1, 3, 2, 4, 0, 5)).reshape(NJ, CL, 4 * S5_STATE)

    of = jnp.arange(CH) + 1
    ob = CH - jnp.arange(CH)

    def p_part(d, e):
        pr = pw_re[d][..., e][:, :, :, None]
        pi = pw_im[d][..., e][:, :, :, None]
        cr = jnp.transpose(c_re[d], (0, 2, 1))[:, :, None, :]
        ci = jnp.transpose(c_im[d], (0, 2, 1))[:, :, None, :]
        return cr * pr - ci * pi, -(cr * pi + ci * pr)
    pfr, pfi = p_part(0, of)
    pbr, pbi = p_part(1, ob)
    p = jnp.stack([pfr, pfi, pbr, pbi], axis=0).reshape(4, NJ, GPT, S5_STATE, CH, S5_GROUP)
    a_p = jnp.transpose(p, (1, 0, 2, 3, 4, 5)).reshape(NJ, SW, CH * S5_GROUP)

    def lanes(v):
        return jnp.transpose(v.reshape(2, NJ, GPT * S5_STATE), (1, 0, 2))
    c_r, c_i = apow(jnp.full((1,), float(CH), F32))
    s_r, s_i = apow(jnp.full((1,), float(CH * SEG), F32))
    cr, ci, sr, si = (lanes(v[..., 0]) for v in (c_r, c_i, s_r, s_i))
    trans = jnp.stack([cr[:, 0], ci[:, 0], cr[:, 1], ci[:, 1],
                       sr[:, 0], si[:, 0], sr[:, 1], si[:, 1]], axis=1)

    cidx = jnp.arange(N_CHUNK_CTX, dtype=F32)
    wf_r, wf_i = apow(CH * (N_CHUNK_CTX - 1 - cidx))
    wb_r, wb_i = apow(CH * cidx)

    def ctx_lanes(v, d):
        return jnp.transpose(v[d].reshape(NJ, GPT * S5_STATE, N_CHUNK_CTX), (0, 2, 1))
    ctx_w = jnp.stack([ctx_lanes(wf_r, 0), ctx_lanes(wf_i, 0),
                       ctx_lanes(wb_r, 1), ctx_lanes(wb_i, 1)], axis=1)
    return a_m.astype(BF16), a_q.astype(BF16), a_p.astype(BF16), trans, ctx_w


def _s5w_kernel(am_ref, aq_ref, ap_ref, cm_ref, cq_ref, wm_ref, wq_ref, wp_ref):
    def expand(a, c, row_shift, col_shift):
        w = jnp.dot(a, c, preferred_element_type=F32)
        rg = (lax.broadcasted_iota(jnp.int32, (w.shape[0], 1), 0) >> row_shift) & (GPT - 1)
        cg = (lax.broadcasted_iota(jnp.int32, (1, w.shape[1]), 1) >> col_shift) & (GPT - 1)
        return jnp.where(rg == cg, w, 0.0).astype(BF16)
    wm_ref[0] = expand(am_ref[0], cm_ref[...], 4, 4)
    wq_ref[0] = expand(aq_ref[0], cq_ref[...], 4, 6)
    wp_ref[0] = expand(ap_ref[0], cm_ref[...], 6, 4)


def _s5_expand(a_m, a_q, a_p):
    c_m = np.kron(np.eye(CH), np.kron(np.ones((1, GPT)), np.eye(S5_GROUP)))
    c_q = np.kron(np.eye(4), np.kron(np.ones((1, GPT)), np.eye(S5_STATE)))
    c_m = jnp.asarray(c_m, F32).astype(BF16)
    c_q = jnp.asarray(c_q, F32).astype(BF16)
    return pl.pallas_call(
        _s5w_kernel,
        grid=(NJ,),
        in_specs=[pl.BlockSpec((1, CL, CH * S5_GROUP), lambda j: (j, 0, 0)),
                  pl.BlockSpec((1, CL, 4 * S5_STATE), lambda j: (j, 0, 0)),
                  pl.BlockSpec((1, SW, CH * S5_GROUP), lambda j: (j, 0, 0)),
                  pl.BlockSpec(c_m.shape, lambda j: (0, 0)),
                  pl.BlockSpec(c_q.shape, lambda j: (0, 0))],
        out_specs=(pl.BlockSpec((1, CL, CL), lambda j: (j, 0, 0)),
                   pl.BlockSpec((1, CL, SW), lambda j: (j, 0, 0)),
                   pl.BlockSpec((1, SW, CL), lambda j: (j, 0, 0))),
        out_shape=(jax.ShapeDtypeStruct((NJ, CL, CL), BF16),
                   jax.ShapeDtypeStruct((NJ, CL, SW), BF16),
                   jax.ShapeDtypeStruct((NJ, SW, CL), BF16)),
        compiler_params=_cparams(("parallel",)),
        name="s5w",
    )(a_m, a_q, a_p, c_m, c_q)


def _s5_kernel(p_ref, pc_ref, wm_ref, wq_ref, wp_ref, tr_ref, cw_ref, y_ref, v_ref):
    nq = NJ
    half = GPT * S5_STATE

    def chunk_rows(ref, r0, nrows):
        return jnp.concatenate([ref[t, pl.ds(r0, nrows), :] for t in range(CH)], axis=-1)

    def fill(k, c):
        r0 = pl.multiple_of(k * SEG, SEG)
        v = jnp.dot(chunk_rows(p_ref, r0, SEG), wq_ref[0], preferred_element_type=F32)
        o0 = pl.multiple_of(k * PITCH, SUBLANES)
        for s in range(4 * nq):
            v_ref[s, pl.ds(o0, SEG), :] = v[:, s * LANES:(s + 1) * LANES]
        return c
    lax.fori_loop(0, NSEG, fill, 0)

    vc = jnp.dot(chunk_rows(pc_ref, 0, N_CHUNK_CTX), wq_ref[0], preferred_element_type=F32)
    vfr, vfi, vbr, vbi = (vc[:, i * half:(i + 1) * half] for i in range(4))
    wfr, wfi, wbr, wbi = (cw_ref[0, i] for i in range(4))
    s0_fr = jnp.sum(wfr * vfr - wfi * vfi, axis=0, keepdims=True)
    s0_fi = jnp.sum(wfr * vfi + wfi * vfr, axis=0, keepdims=True)
    s0_br = jnp.sum(wbr * vbr - wbi * vbi, axis=0, keepdims=True)
    s0_bi = jnp.sum(wbr * vbi + wbi * vbr, axis=0, keepdims=True)

    tr = tr_ref[0]
    afr, afi, abr, abi = (jnp.broadcast_to(tr[i:i + 1], (NSEG, half)) for i in range(4))
    gfr, gfi, gbr, gbi = (tr[i:i + 1] for i in range(4, 8))

    def load_part(part, i):
        return jnp.concatenate(
            [v_ref[part * nq + q, pl.ds(i, NSEG, stride=PITCH), :] for q in range(nq)], axis=-1)

    def store_part(part, i, val):
        for q in range(nq):
            v_ref[part * nq + q, pl.ds(i, NSEG, stride=PITCH), :] = val[:, q * LANES:(q + 1) * LANES]

    def step(i, carry, write):
        fr, fi, br, bi = carry
        ib = SEG - 1 - i
        ufr, ufi = load_part(0, i), load_part(1, i)
        ubr, ubi = load_part(2, ib), load_part(3, ib)
        if write:
            store_part(0, i, fr)
            store_part(1, i, fi)
            store_part(2, ib, br)
            store_part(3, ib, bi)
        return (afr * fr - afi * fi + ufr, afr * fi + afi * fr + ufi,
                abr * br - abi * bi + ubr, abr * bi + abi * br + ubi)

    zero = jnp.zeros((NSEG, half), F32)
    ffr, ffi, fbr, fbi = lax.fori_loop(0, SEG, functools.partial(step, write=False),
                                       (zero, zero, zero, zero))

    rows_fr, rows_fi = [s0_fr], [s0_fi]
    for k in range(1, NSEG):
        pr, pi = rows_fr[-1], rows_fi[-1]
        rows_fr.append(gfr * pr - gfi * pi + ffr[k - 1:k])
        rows_fi.append(gfr * pi + gfi * pr + ffi[k - 1:k])
    rows_br, rows_bi = [s0_br], [s0_bi]
    for k in range(NSEG - 2, -1, -1):
        pr, pi = rows_br[0], rows_bi[0]
        rows_br.insert(0, gbr * pr - gbi * pi + fbr[k + 1:k + 2])
        rows_bi.insert(0, gbr * pi + gbi * pr + fbi[k + 1:k + 2])
    init = tuple(jnp.concatenate(r, axis=0) for r in (rows_fr, rows_fi, rows_br, rows_bi))

    lax.fori_loop(0, SEG, functools.partial(step, write=True), init)

    def emit(k, c):
        r0 = pl.multiple_of(k * SEG, SEG)
        o0 = pl.multiple_of(k * PITCH, SUBLANES)
        b = chunk_rows(p_ref, r0, SEG)
        sin = jnp.concatenate([v_ref[s, pl.ds(o0, SEG), :] for s in range(4 * nq)], axis=-1)
        y = (jnp.dot(b, wm_ref[0], preferred_element_type=F32)
             + jnp.dot(sin.astype(BF16), wp_ref[0], preferred_element_type=F32))
        for t in range(CH):
            y_ref[t, pl.ds(r0, SEG), :] = y[:, t * LANES:(t + 1) * LANES].astype(y_ref.dtype)
        return c
    lax.fori_loop(0, NSEG, emit, 0)


def _s5(p_t, pc_t, w_m, w_q, w_p, trans, ctx_w):
    one = pl.Buffered(1)
    return pl.pallas_call(
        _s5_kernel,
        grid=(NJ,),
        in_specs=[pl.BlockSpec((CH, N_CHUNK, LANES), lambda j: (0, 0, j)),
                  pl.BlockSpec((CH, N_CHUNK_CTX, LANES), lambda j: (0, 0, j)),
                  pl.BlockSpec((1, CL, CL), lambda j: (j, 0, 0), pipeline_mode=one),
                  pl.BlockSpec((1, CL, SW), lambda j: (j, 0, 0), pipeline_mode=one),
                  pl.BlockSpec((1, SW, CL), lambda j: (j, 0, 0), pipeline_mode=one),
                  pl.BlockSpec((1, SUBLANES, GPT * S5_STATE), lambda j: (j, 0, 0)),
                  pl.BlockSpec((1, 4, N_CHUNK_CTX, GPT * S5_STATE), lambda j: (j, 0, 0, 0))],
        out_specs=pl.BlockSpec((CH, N_CHUNK, LANES), lambda j: (0, 0, j)),
        out_shape=jax.ShapeDtypeStruct((CH, N_CHUNK, S5_WIDTH), BF16),
        scratch_shapes=[pltpu.VMEM((4 * NJ, NSEG * PITCH, LANES), F32)],
        compiler_params=_cparams(("parallel",)),
        name="s5",
    )(p_t, pc_t, w_m, w_q, w_p, trans, ctx_w)


def _dft_tables():
    n = np.arange(FN)
    ang = 2.0 * np.pi * np.outer(n, n) / FN
    c, s = np.cos(ang), np.sin(ang)
    st1 = np.block([[c, s], [-s, c]])
    tw = 2.0 * np.pi * np.outer(n, n) / (FN * FN)
    wr, wi = np.cos(tw), -np.sin(tw)
    fr = c[None] * wr[:, None, :] + s[None] * wi[:, None, :]
    fi = c[None] * wi[:, None, :] - s[None] * wr[:, None, :]
    st2 = np.concatenate([fr, -fi], axis=-1)
    scale = 1.0 / math.sqrt(N_TOK * FFT_DIM)
    blk_c = np.kron(np.eye(FFT_GROUPS), c) * scale
    blk_s = np.kron(np.eye(FFT_GROUPS), s) * scale
    fc = np.concatenate([blk_c, -blk_s], axis=1)
    return (jnp.asarray(st1, F32).astype(BF16), jnp.asarray(st2, F32).astype(BF16), jnp.asarray(fc, F32))


def _fft1_kernel(xr_ref, xi_ref, f_ref, yr_ref, yi_ref):
    xs = jnp.concatenate([xr_ref[...], xi_ref[...]], axis=0)
    y = jnp.dot(f_ref[...], xs, preferred_element_type=F32)
    yr_ref[...] = y[:FN].astype(BF16)
    yi_ref[...] = y[FN:].astype(BF16)


def _fft1(xr, xi, st1):
    w = FB * FFT_WIDTH
    spec = pl.BlockSpec((FN, w), lambda i: (0, i))
    xr2 = xr.reshape(FN, FN * FFT_WIDTH)
    xi2 = xi.reshape(FN, FN * FFT_WIDTH)
    return pl.pallas_call(
        _fft1_kernel,
        grid=(FN // FB,),
        in_specs=[spec, spec, pl.BlockSpec((2 * FN, 2 * FN), lambda i: (0, 0))],
        out_specs=(spec, spec),
        out_shape=(jax.ShapeDtypeStruct((FN, FN * FFT_WIDTH), BF16),) * 2,
        compiler_params=_cparams(("parallel",)),
        name="fft1",
    )(xr2, xi2, st1)


def _fft2_kernel(yr_ref, yi_ref, f_ref, z_ref):
    for b in range(FB):
        ys = jnp.concatenate([yr_ref[b * FN:(b + 1) * FN, :], yi_ref[b * FN:(b + 1) * FN, :]], axis=0)
        z = jnp.dot(f_ref[b], ys, preferred_element_type=F32)
        z_ref[:, b * FFT_WIDTH:(b + 1) * FFT_WIDTH] = z.astype(BF16)


def _fft2(yr, yi, st2):
    rows = pl.BlockSpec((FB * FN, FFT_WIDTH), lambda i: (i, 0))
    yr2 = yr.reshape(N_TOK, FFT_WIDTH)
    yi2 = yi.reshape(N_TOK, FFT_WIDTH)
    z = pl.pallas_call(
        _fft2_kernel,
        grid=(FN // FB,),
        in_specs=[rows, rows, pl.BlockSpec((FB, FN, 2 * FN), lambda i: (i, 0, 0))],
        out_specs=pl.BlockSpec((FN, FB * FFT_WIDTH), lambda i: (0, i)),
        out_shape=jax.ShapeDtypeStruct((FN, FN * FFT_WIDTH), BF16),
        compiler_params=_cparams(("parallel",)),
        name="fft2",
    )(yr2, yi2, st2)
    return z.reshape(N_TOK, FFT_WIDTH)


def _gelu_tanh(x):
    return 0.5 * x * (1.0 + jnp.tanh(math.sqrt(2.0 / math.pi) * (x + 0.044715 * (x * x * x))))


def _mix_kernel(x_ref, er_ref, ec_ref, lg_ref, lb_ref, m1_ref, s1_ref, wg_ref, bg_ref,
                yt_ref, zr_ref, wglu_ref, bglu_ref, wbs_ref, wbf_ref, bbf_ref, wo_ref, bo_ref,
                g1_ref, l1g_ref, l1b_ref, m2_ref, s2_ref, wr_ref, br_ref, tri_ref,
                h1_ref, u2_ref, idx_ref, rank_ref, gate_ref, cnt_ref, scr_ref, base_ref):
    i = pl.program_id(0)

    @pl.when(i == 0)
    def _():
        base_ref[...] = jnp.zeros_like(base_ref)

    x = x_ref[...] + _pos_code(er_ref, ec_ref, TM)
    h = _layer_norm(x, lg_ref[...], lb_ref[...])
    u = (h * m1_ref[...] + s1_ref[...]).astype(BF16)
    gates = _sigmoid(jnp.dot(u, wg_ref[...], preferred_element_type=F32) + bg_ref[...])

    for t in range(CH):
        for j in range(NJ):
            scr_ref[j, pl.ds(t, TM // CH, stride=CH), :] = (
                yt_ref[t, :, j * LANES:(j + 1) * LANES].astype(F32))
    ys = jnp.concatenate([scr_ref[j] for j in range(NJ)], axis=-1)
    z = jnp.dot(_gelu_tanh(ys).astype(BF16), wglu_ref[...], preferred_element_type=F32) + bglu_ref[...]
    glu = (z[:, :S5_WIDTH] * _sigmoid(z[:, S5_WIDTH:])).astype(BF16)
    y_s5 = jnp.dot(glu, wbs_ref[...], preferred_element_type=F32)
    y_fft = jnp.dot(zr_ref[...], wbf_ref[...], preferred_element_type=F32) + bbf_ref[...]
    mixed = (gates[:, :D] * y_s5 + gates[:, D:] * y_fft).astype(BF16)
    y = jnp.dot(mixed, wo_ref[...], preferred_element_type=F32) + bo_ref[...]
    h1 = _layer_norm(ALPHA * h + g1_ref[...] * y, l1g_ref[...], l1b_ref[...])
    h1_ref[...] = h1
    u2 = h1 * m2_ref[...] + s2_ref[...]
    u2_ref[...] = u2

    logits = lax.dot_general(wr_ref[...], u2, (((1,), (1,)), ((), ())),
                             preferred_element_type=F32, precision=HI) + br_ref[:, 0:1]
    eidx = lax.broadcasted_iota(jnp.int32, (N_EXPERTS, TM), 0)
    vals, idxs, hots = [], [], []
    cur = logits
    for _k in range(TOP_K):
        m = jnp.max(cur, axis=0, keepdims=True)
        sel = jnp.min(jnp.where(cur == m, eidx, N_EXPERTS), axis=0, keepdims=True)
        hot = eidx == sel
        cur = jnp.where(hot, -jnp.inf, cur)
        vals.append(m)
        idxs.append(sel)
        hots.append(hot)
    exps = [jnp.exp(v - vals[0]) for v in vals]
    den = exps[0] + exps[1] + exps[2] + exps[3]
    gate4 = jnp.concatenate([e / den for e in exps], axis=0)
    idx_ref[...] = jnp.concatenate(idxs, axis=0)

    hot_sum = (hots[0] | hots[1] | hots[2] | hots[3]).astype(F32)
    before = jnp.dot(hot_sum.astype(BF16), tri_ref[...], preferred_element_type=F32)
    tot = base_ref[:, 0:1] + before
    rank_ref[...] = jnp.concatenate(
        [jnp.sum(jnp.where(hk, tot, 0.0), axis=0, keepdims=True) for hk in hots], axis=0
    ).astype(jnp.int32)
    base_ref[...] = base_ref[...] + jnp.sum(hot_sum, axis=1, keepdims=True)
    cnt_ref[...] = base_ref[...]

    gpad = jnp.concatenate([gate4, jnp.zeros((LANES - TOP_K, TM), F32)], axis=0)
    gate_ref[...] = gpad.T


def _mix(x, emb_r, emb_c, lg, lb, m1, s1, wg, bg, y_t, zr, wglu, bglu, wbs, wbf, bbf, wo, bo,
         g1, l1g, l1b, m2, s2, wr_t, br, tri):
    vec = pl.BlockSpec((1, D), lambda i: (0, 0))

    def full(a):
        return pl.BlockSpec(a.shape, lambda i: (0,) * a.ndim)
    return pl.pallas_call(
        _mix_kernel,
        grid=(N_TOK // TM,),
        in_specs=[pl.BlockSpec((TM, D), lambda i: (i, 0)),
                  pl.BlockSpec((TM // GRID_W, D // 2), lambda i: (i, 0)),
                  pl.BlockSpec((GRID_W, D // 2), lambda i: (0, 0)),
                  vec, vec, vec, vec, full(wg), full(bg),
                  pl.BlockSpec((CH, TM // CH, S5_WIDTH), lambda i: (0, i, 0)),
                  pl.BlockSpec((TM, FFT_WIDTH), lambda i: (i, 0)),
                  full(wglu), full(bglu), full(wbs), full(wbf), full(bbf), full(wo), full(bo),
                  vec, vec, vec, vec, vec, full(wr_t), full(br), full(tri)],
        out_specs=(pl.BlockSpec((TM, D), lambda i: (i, 0)),
                   pl.BlockSpec((TM, D), lambda i: (i, 0)),
                   pl.BlockSpec((TOP_K, TM), lambda i: (0, i)),
                   pl.BlockSpec((TOP_K, TM), lambda i: (0, i)),
                   pl.BlockSpec((TM, LANES), lambda i: (i, 0)),
                   pl.BlockSpec((N_EXPERTS, LANES), lambda i: (0, 0))),
        out_shape=(jax.ShapeDtypeStruct((N_TOK, D), F32),
                   jax.ShapeDtypeStruct((N_TOK, D), F32),
                   jax.ShapeDtypeStruct((TOP_K, N_TOK), jnp.int32),
                   jax.ShapeDtypeStruct((TOP_K, N_TOK), jnp.int32),
                   jax.ShapeDtypeStruct((N_TOK, LANES), F32),
                   jax.ShapeDtypeStruct((N_EXPERTS, LANES), F32)),
        scratch_shapes=[pltpu.VMEM((NJ, TM, LANES), F32),
                        pltpu.VMEM((N_EXPERTS, LANES), F32)],
        compiler_params=_cparams(("arbitrary",)),
        name="mix",
    )(x, emb_r, emb_c, lg, lb, m1, s1, wg, bg, y_t, zr, wglu, bglu, wbs, wbf, bbf, wo, bo,
      g1, l1g, l1b, m2, s2, wr_t, br, tri)


def _row_copy(src, dst, sem):
    return pltpu.make_async_copy(src, dst, sem)


def _dispatch_kernel(pend_ref, padded_ref, dest_ref, u_ref, buf_ref, zero_ref, zsem, sem):
    i = pl.program_id(0)

    @pl.when(i == 0)
    def _():
        zero_ref[...] = jnp.zeros_like(zero_ref)
        n_used = pend_ref[N_EXPERTS - 1] // BM

        def clear_copy(start):
            return _row_copy(zero_ref, buf_ref.at[pl.ds(pl.multiple_of(start, BM), BM)], zsem)

        def each(fn):
            def expert(e, c):
                @pl.when(padded_ref[e] > 0)
                def _():
                    fn(clear_copy(pend_ref[e] - BM))
                return c
            lax.fori_loop(0, N_EXPERTS, expert, 0)

            def tail(b, c):
                fn(clear_copy(b * BM))
                return c
            lax.fori_loop(n_used, N_BLOCKS, tail, 0)
        each(lambda cp: cp.start())
        each(lambda cp: cp.wait())

    def issue(t, c):
        for k in range(TOP_K):
            d = dest_ref[0, 0, k * TM + t]
            _row_copy(u_ref.at[pl.ds(t, 1)], buf_ref.at[pl.ds(d, 1)], sem).start(priority=k % 2)
        return c
    lax.fori_loop(0, TM, issue, 0)
    for _k in range(TOP_K):
        _row_copy(u_ref, buf_ref.at[pl.ds(0, TM)], sem).wait()


def _dispatch(pad_ends, padded, dest_tiles, u2):
    return pl.pallas_call(
        _dispatch_kernel,
        grid_spec=pltpu.PrefetchScalarGridSpec(
            num_scalar_prefetch=2,
            grid=(N_TOK // TM,),
            in_specs=[pl.BlockSpec((1, 1, TOP_K * TM), lambda i, a, b: (i, 0, 0),
                                   memory_space=pltpu.SMEM),
                      pl.BlockSpec((TM, D), lambda i, a, b: (i, 0))],
            out_specs=pl.BlockSpec(memory_space=pl.ANY),
            scratch_shapes=[pltpu.VMEM((BM, D), F32),
                            pltpu.SemaphoreType.DMA(()),
                            pltpu.SemaphoreType.DMA(())]),
        out_shape=jax.ShapeDtypeStruct((ROWS, D), F32),
        compiler_params=_cparams(("arbitrary",)),
        name="dispatch",
    )(pad_ends, padded, dest_tiles, u2)


def _ffn_kernel(be_ref, nu_ref, x_ref, wu_ref, bu_ref, wd_ref, bd_ref, y_ref, wub_ref, wdb_ref):
    i = pl.program_id(0)
    used = i < nu_ref[0]

    @pl.when(used)
    def _():
        prev = be_ref[jnp.maximum(i - 1, 0)]

        @pl.when((i == 0) | (be_ref[i] != prev))
        def _():
            wub_ref[...] = wu_ref[0].astype(BF16)
            wdb_ref[...] = wd_ref[0].astype(BF16)

        h = jnp.dot(x_ref[...].astype(BF16), wub_ref[...], preferred_element_type=F32) + bu_ref[0]
        h_glu = jnp.minimum(h[:, :D], SWIGLU_LIMIT)
        h_lin = jnp.clip(h[:, D:], -SWIGLU_LIMIT, SWIGLU_LIMIT)
        act = (h_glu * _sigmoid(SWIGLU_ALPHA * h_glu) * (h_lin + 1.0)).astype(BF16)
        y_ref[...] = jnp.dot(act, wdb_ref[...], preferred_element_type=F32) + bd_ref[0]

    @pl.when(jnp.logical_not(used))
    def _():
        y_ref[...] = jnp.zeros_like(y_ref)


def _ffn(block_expert, n_used, buf, w_up, b_up, w_down, b_down):
    def blk(i, be, nu):
        return jnp.minimum(i, nu[0] - 1)
    return pl.pallas_call(
        _ffn_kernel,
        grid_spec=pltpu.PrefetchScalarGridSpec(
            num_scalar_prefetch=2,
            grid=(N_BLOCKS,),
            in_specs=[pl.BlockSpec((BM, D), lambda i, be, nu: (blk(i, be, nu), 0)),
                      pl.BlockSpec((1, D, 2 * D), lambda i, be, nu: (be[blk(i, be, nu)], 0, 0)),
                      pl.BlockSpec((1, 1, 2 * D), lambda i, be, nu: (be[blk(i, be, nu)], 0, 0)),
                      pl.BlockSpec((1, D, D), lambda i, be, nu: (be[blk(i, be, nu)], 0, 0)),
                      pl.BlockSpec((1, 1, D), lambda i, be, nu: (be[blk(i, be, nu)], 0, 0))],
            out_specs=pl.BlockSpec((BM, D), lambda i, be, nu: (i, 0)),
            scratch_shapes=[pltpu.VMEM((D, 2 * D), BF16),
                            pltpu.VMEM((D, D), BF16)]),
        out_shape=jax.ShapeDtypeStruct((ROWS, D), F32),
        compiler_params=_cparams(("arbitrary",)),
        name="ffn",
    )(block_expert, n_used, buf, w_up, b_up, w_down, b_down)


def _combine_kernel(dest_ref, y_ref, h1_ref, gate_ref, g2_ref, lg_ref, lb_ref, o_ref, rows_ref, sem):
    def issue(t, c):
        for k in range(TOP_K):
            d = dest_ref[0, 0, k * TM_COMB + t]
            _row_copy(y_ref.at[pl.ds(d, 1)], rows_ref.at[k, pl.ds(t, 1)], sem).start(priority=k % 2)
        return c
    lax.fori_loop(0, TM_COMB, issue, 0)
    for k in range(TOP_K):
        _row_copy(y_ref.at[pl.ds(0, TM_COMB)], rows_ref.at[k], sem).wait()
    gate = gate_ref[...]
    m = rows_ref[0] * gate[:, 0:1]
    for k in range(1, TOP_K):
        m = m + rows_ref[k] * gate[:, k:k + 1]
    o_ref[...] = _layer_norm(ALPHA * h1_ref[...] + g2_ref[...] * m, lg_ref[...], lb_ref[...])


def _combine(dest_tiles, y_buf, h1, gate_tok, g2, lg, lb):
    vec = pl.BlockSpec((1, D), lambda i: (0, 0))
    return pl.pallas_call(
        _combine_kernel,
        grid=(N_TOK // TM_COMB,),
        in_specs=[pl.BlockSpec((1, 1, TOP_K * TM_COMB), lambda i: (i, 0, 0), memory_space=pltpu.SMEM),
                  pl.BlockSpec(memory_space=pl.ANY),
                  pl.BlockSpec((TM_COMB, D), lambda i: (i, 0)),
                  pl.BlockSpec((TM_COMB, LANES), lambda i: (i, 0)),
                  vec, vec, vec],
        out_specs=pl.BlockSpec((TM_COMB, D), lambda i: (i, 0)),
        out_shape=jax.ShapeDtypeStruct((N_TOK, D), F32),
        scratch_shapes=[pltpu.VMEM((TOP_K, TM_COMB, D), F32),
                        pltpu.SemaphoreType.DMA(())],
        compiler_params=_cparams(("parallel",)),
        name="combine",
    )(dest_tiles, y_buf, h1, gate_tok, g2, lg, lb)


def _sincos_tables():
    q = D // 4
    omega = 1.0 / (10000.0 ** (jnp.arange(q, dtype=F32) / q))

    def emb(n):
        ang = jnp.arange(n, dtype=F32)[:, None] * omega[None, :]
        return jnp.concatenate([jnp.sin(ang), jnp.cos(ang)], axis=-1)
    return emb(N_TOK // GRID_W), emb(GRID_W)


def _tile_slots(a, tile):
    return jnp.transpose(a.reshape(TOP_K, N_TOK // tile, tile), (1, 0, 2)).reshape(N_TOK // tile, 1, TOP_K * tile)


def kernel(x, c, ctx, c_ctx, ln_in_g, ln_in_b, w_ada, b_ada, w_in, b_in, s5_lambda_re, s5_lambda_im, s5_log_dt, s5_b_re, s5_b_im, s5_c_re, s5_c_im, s5_d, w_glu, b_glu, w_br_s5, w_br_fft, b_br_fft, w_out, b_out, ln1_g, ln1_b, w_router, b_router, w_up, b_up, w_down, b_down, ln2_g, ln2_b):
    assert x.shape == (1, N_TOK, D) and ctx.shape == (1, N_CTX, D) and w_ada.shape[0] == 1
    row = lambda v: v.reshape(1, -1).astype(F32)

    cc = jnp.concatenate([c.reshape(1, D), c_ctx.reshape(1, D), jnp.zeros((SUBLANES - 2, D), F32)], axis=0)
    ada = _ada(cc, w_ada[0], row(b_ada[0]))
    sh1, sc1, g1, sh2, sc2, g2 = (ada[0:1, k * D:(k + 1) * D] for k in range(6))
    sh1c, sc1c = ada[1:2, 0:D], ada[1:2, D:2 * D]

    emb_r, emb_c = _sincos_tables()
    st1, st2, fc = _dft_tables()
    lg, lb = row(ln_in_g), row(ln_in_b)

    w_s5 = w_in[0][:, :S5_WIDTH]
    w_fft = w_in[0][:, S5_WIDTH:S5_WIDTH + FFT_WIDTH]
    w_g = w_in[0][:, S5_WIDTH + FFT_WIDTH:]
    b_s5 = row(b_in[0][:S5_WIDTH])
    b_fft8 = jnp.concatenate([row(b_in[0][S5_WIDTH:S5_WIDTH + FFT_WIDTH]),
                              jnp.zeros((SUBLANES - 1, FFT_WIDTH), F32)], axis=0)
    b_g = row(b_in[0][S5_WIDTH + FFT_WIDTH:])
    w_fc, b_fc = _fft_weights(w_fft, b_fft8, fc)
    wcat = jnp.concatenate([w_s5, w_fc], axis=1).astype(BF16)
    bcat = jnp.concatenate([b_s5, b_fc[0:1]], axis=1)

    x2 = x[0]
    p_t, xr, xi = _proj(x2, emb_r, emb_c, lg, lb, 1.0 + sc1, sh1, wcat, bcat)
    pc_t = _ctx_proj(ctx[0], lg, lb, 1.0 + sc1c, sh1c, w_s5.astype(BF16), b_s5)

    a_m, a_q, a_p, trans, ctx_w = _s5_tables(
        s5_lambda_re[0], s5_lambda_im[0], s5_log_dt[0], s5_b_re[0], s5_b_im[0],
        s5_c_re[0], s5_c_im[0], s5_d[0])
    w_m, w_q, w_p = _s5_expand(a_m, a_q, a_p)
    y_t = _s5(p_t, pc_t, w_m, w_q, w_p, trans, ctx_w)

    yr, yi = _fft1(xr, xi, st1)
    zr = _fft2(yr, yi, st2)

    tri = (jnp.arange(TM)[:, None] < jnp.arange(TM)[None, :]).astype(BF16)
    br = jnp.broadcast_to(b_router[0].reshape(N_EXPERTS, 1), (N_EXPERTS, LANES))
    h1, u2, idx_t, rank_t, gate_tok, counts = _mix(
        x2, emb_r, emb_c, lg, lb, 1.0 + sc1, sh1, w_g.astype(BF16), b_g, y_t, zr,
        w_glu[0].astype(BF16), row(b_glu[0]), w_br_s5[0].astype(BF16), w_br_fft[0].astype(BF16),
        row(b_br_fft[0]), w_out[0].astype(BF16), row(b_out[0]), g1, row(ln1_g[0]), row(ln1_b[0]),
        1.0 + sc2, sh2, jnp.transpose(w_router[0]), br, tri)

    cnt = counts[:, 0].astype(jnp.int32)
    padded = (cnt + BM - 1) // BM * BM
    pad_ends = jnp.cumsum(padded)
    pad_starts = pad_ends - padded
    own = idx_t[None, :, :] == jnp.arange(N_EXPERTS, dtype=jnp.int32)[:, None, None]
    dest_t = jnp.sum(jnp.where(own, pad_starts[:, None, None], 0), axis=0) + rank_t
    block_start = jnp.arange(N_BLOCKS, dtype=jnp.int32) * BM
    block_expert = jnp.minimum(jnp.sum(block_start[:, None] >= pad_ends[None, :], axis=1),
                               N_EXPERTS - 1).astype(jnp.int32)
    n_used = (pad_ends[-1:] // BM).astype(jnp.int32)

    buf = _dispatch(pad_ends.astype(jnp.int32), padded.astype(jnp.int32), _tile_slots(dest_t, TM), u2)
    y_buf = _ffn(block_expert, n_used, buf, w_up[0], b_up[0].reshape(N_EXPERTS, 1, 2 * D),
                 w_down[0], b_down[0].reshape(N_EXPERTS, 1, D))
    out = _combine(_tile_slots(dest_t, TM_COMB), y_buf, h1, gate_tok, g2, row(ln2_g[0]), row(ln2_b[0]))
    return out.reshape(1, N_TOK, D)
```

```python
import functools
import math

import jax
import jax.numpy as jnp
import numpy as np
from jax import lax
from jax.experimental import pallas as pl
from jax.experimental.pallas import tpu as pltpu

F32 = jnp.float32
BF16 = jnp.bfloat16
HI = lax.Precision.HIGHEST

D = 1024
N_TOK = 16384
N_CTX = 256
GRID_W = 64
S5_GROUP = 16
S5_GROUPS = 32
S5_STATE = 64
S5_WIDTH = 512
FFT_GROUPS = 4
FFT_DIM = 128
FFT_WIDTH = 512
N_EXPERTS = 32
TOP_K = 4
LN_EPS = 1e-5
ALPHA = 2.0 ** 0.25
SWIGLU_ALPHA = 1.702
SWIGLU_LIMIT = 7.0

LANES = 128
SUBLANES = 8
VMEM_LIMIT = 56 * 1024 * 1024

CH = 8
N_CHUNK = N_TOK // CH
N_CHUNK_CTX = N_CTX // CH
NSEG = SUBLANES
SEG = N_CHUNK // NSEG
PITCH = SEG + 8
GPT = LANES // S5_GROUP
NJ = S5_WIDTH // LANES
CL = CH * LANES
SW = 4 * GPT * S5_STATE

FN = 128
FB = 4

TM = 512
N_TILES = N_TOK // TM
BM = 512
N_SLOTS = N_TOK * TOP_K
SEG_ALIGN = SUBLANES
CAP_BLOCK = 256
CAP = -(-(TOP_K * TM + N_EXPERTS * (SEG_ALIGN - 1)) // CAP_BLOCK) * CAP_BLOCK
NCHK = CAP // SEG_ALIGN
N_BLOCKS = -(-(N_SLOTS + N_TILES * N_EXPERTS * (SEG_ALIGN - 1)) // BM) + N_EXPERTS
ROWS = N_BLOCKS * BM


def _cparams(sem):
    return pltpu.CompilerParams(dimension_semantics=sem, vmem_limit_bytes=VMEM_LIMIT)


def _layer_norm(x, g, b):
    mu = jnp.mean(x, axis=-1, keepdims=True)
    xc = x - mu
    var = jnp.mean(xc * xc, axis=-1, keepdims=True)
    return xc * lax.rsqrt(var + LN_EPS) * g + b


def _sigmoid(x):
    return 1.0 / (1.0 + jnp.exp(-x))


def _ada_kernel(c_ref, w_ref, b_ref, o_ref):
    c = c_ref[...]
    s = c * _sigmoid(c)
    o_ref[...] = jnp.dot(s, w_ref[...], preferred_element_type=F32, precision=HI) + b_ref[...]


def _ada(cc, w_ada, b_ada):
    nb = 4
    wb = 6 * D // nb
    return pl.pallas_call(
        _ada_kernel,
        grid=(nb,),
        in_specs=[pl.BlockSpec((SUBLANES, D), lambda i: (0, 0)),
                  pl.BlockSpec((D, wb), lambda i: (0, i)),
                  pl.BlockSpec((1, wb), lambda i: (0, i))],
        out_specs=pl.BlockSpec((SUBLANES, wb), lambda i: (0, i)),
        out_shape=jax.ShapeDtypeStruct((SUBLANES, 6 * D), F32),
        compiler_params=_cparams(("parallel",)),
        name="ada",
    )(cc, w_ada, b_ada)


def _fftw_kernel(w_ref, b_ref, f_ref, wo_ref, bo_ref):
    f = f_ref[...]
    wo_ref[...] = jnp.dot(w_ref[...], f, preferred_element_type=F32, precision=HI)
    bo_ref[...] = jnp.dot(b_ref[...], f, preferred_element_type=F32, precision=HI)


def _fft_weights(w_fft, b_fft8, fc):
    return pl.pallas_call(
        _fftw_kernel,
        out_shape=(jax.ShapeDtypeStruct((D, 2 * FFT_WIDTH), F32),
                   jax.ShapeDtypeStruct((SUBLANES, 2 * FFT_WIDTH), F32)),
        compiler_params=pltpu.CompilerParams(vmem_limit_bytes=VMEM_LIMIT),
        name="fftw",
    )(w_fft, b_fft8, fc)


def _pos_code(er_ref, ec_ref, tm):
    nr = tm // GRID_W
    er = er_ref[...]
    row = jnp.broadcast_to(er[:, None, :], (nr, GRID_W, D // 2)).reshape(tm, D // 2)
    col = jnp.concatenate([ec_ref[...]] * nr, axis=0)
    return jnp.concatenate([row, col], axis=-1)


def _to_chunk_major(val, scr_ref, out_ref, tm):
    for j in range(NJ):
        scr_ref[j] = val[:, j * LANES:(j + 1) * LANES]
    for t in range(CH):
        for j in range(NJ):
            piece = scr_ref[j, pl.ds(t, tm // CH, stride=CH), :]
            out_ref[t, :, j * LANES:(j + 1) * LANES] = piece.astype(out_ref.dtype)


def _proj_kernel(x_ref, er_ref, ec_ref, lg_ref, lb_ref, m_ref, s_ref, w_ref, b_ref,
                 p_ref, xr_ref, xi_ref, scr_ref):
    x = x_ref[...] + _pos_code(er_ref, ec_ref, TM)
    h = _layer_norm(x, lg_ref[...], lb_ref[...])
    u = (h * m_ref[...] + s_ref[...]).astype(BF16)
    p = jnp.dot(u, w_ref[...], preferred_element_type=F32) + b_ref[...]
    _to_chunk_major(p[:, :S5_WIDTH], scr_ref, p_ref, TM)
    xr_ref[...] = p[:, S5_WIDTH:S5_WIDTH + FFT_WIDTH].astype(BF16)
    xi_ref[...] = p[:, S5_WIDTH + FFT_WIDTH:].astype(BF16)


def _proj(x, emb_r, emb_c, lg, lb, m1, s1, wcat, bcat):
    nw = wcat.shape[1]
    vec = pl.BlockSpec((1, D), lambda i: (0, 0))
    return pl.pallas_call(
        _proj_kernel,
        grid=(N_TOK // TM,),
        in_specs=[pl.BlockSpec((TM, D), lambda i: (i, 0)),
                  pl.BlockSpec((TM // GRID_W, D // 2), lambda i: (i, 0)),
                  pl.BlockSpec((GRID_W, D // 2), lambda i: (0, 0)),
                  vec, vec, vec, vec,
                  pl.BlockSpec((D, nw), lambda i: (0, 0)),
                  pl.BlockSpec((1, nw), lambda i: (0, 0))],
        out_specs=(pl.BlockSpec((CH, TM // CH, S5_WIDTH), lambda i: (0, i, 0)),
                   pl.BlockSpec((TM, FFT_WIDTH), lambda i: (i, 0)),
                   pl.BlockSpec((TM, FFT_WIDTH), lambda i: (i, 0))),
        out_shape=(jax.ShapeDtypeStruct((CH, N_CHUNK, S5_WIDTH), BF16),
                   jax.ShapeDtypeStruct((N_TOK, FFT_WIDTH), BF16),
                   jax.ShapeDtypeStruct((N_TOK, FFT_WIDTH), BF16)),
        scratch_shapes=[pltpu.VMEM((NJ, TM, LANES), F32)],
        compiler_params=_cparams(("parallel",)),
        name="proj",
    )(x, emb_r, emb_c, lg, lb, m1, s1, wcat, bcat)


def _ctx_proj_kernel(x_ref, lg_ref, lb_ref, m_ref, s_ref, w_ref, b_ref, p_ref, scr_ref):
    h = _layer_norm(x_ref[...], lg_ref[...], lb_ref[...])
    u = (h * m_ref[...] + s_ref[...]).astype(BF16)
    p = jnp.dot(u, w_ref[...], preferred_element_type=F32) + b_ref[...]
    _to_chunk_major(p, scr_ref, p_ref, N_CTX)


def _ctx_proj(ctx, lg, lb, m1, s1, w_s5, b_s5):
    return pl.pallas_call(
        _ctx_proj_kernel,
        out_shape=jax.ShapeDtypeStruct((CH, N_CHUNK_CTX, S5_WIDTH), BF16),
        scratch_shapes=[pltpu.VMEM((NJ, N_CTX, LANES), F32)],
        compiler_params=pltpu.CompilerParams(vmem_limit_bytes=VMEM_LIMIT),
        name="ctxproj",
    )(ctx, lg, lb, m1, s1, w_s5, b_s5)


def _s5_tables(lam_re, lam_im, log_dt, b_re, b_im, c_re, c_im, d_skip):
    dt = jnp.exp(log_dt)[..., None]
    zr = lam_re * dt
    zi = lam_im * dt

    def apow(m):
        m = jnp.asarray(m, F32)
        mag = jnp.exp(zr[..., None] * m)
        return mag * jnp.cos(zi[..., None] * m), mag * jnp.sin(zi[..., None] * m)

    a_re, a_im = apow(jnp.ones((1,), F32))
    a_re, a_im = a_re[..., 0], a_im[..., 0]
    den = lam_re * lam_re + lam_im * lam_im
    num_re = a_re - 1.0
    k_re = (num_re * lam_re + a_im * lam_im) / den
    k_im = (a_im * lam_re - num_re * lam_im) / den
    bb_re = k_re[..., None] * b_re - k_im[..., None] * b_im
    bb_im = k_re[..., None] * b_im + k_im[..., None] * b_re

    ks = jnp.arange(CH + 1, dtype=F32)
    pw_re, pw_im = apow(ks)
    eye = jnp.eye(GPT, dtype=F32)

    cb_rr = jnp.einsum('dghp,dgpk,dgpi->dgkhi', c_re, pw_re[..., :CH], bb_re, precision=HI)
    cb_ii = jnp.einsum('dghp,dgpk,dgpi->dgkhi', c_re, pw_im[..., :CH], bb_im, precision=HI)
    cb_ir = jnp.einsum('dghp,dgpk,dgpi->dgkhi', c_im, pw_re[..., :CH], bb_im, precision=HI)
    cb_ri = jnp.einsum('dghp,dgpk,dgpi->dgkhi', c_im, pw_im[..., :CH], bb_re, precision=HI)
    taps = cb_rr - cb_ii - cb_ir - cb_ri
    t_in = jnp.arange(CH)[:, None]
    t_out = jnp.arange(CH)[None, :]
    lag = jnp.abs(t_out - t_in)
    tf = jnp.where((t_out >= t_in)[None, :, :, None, None], taps[0][:, lag], 0.0)
    tb = jnp.where((t_out <= t_in)[None, :, :, None, None], taps[1][:, lag], 0.0)
    mt = jnp.transpose(tf + tb, (0, 1, 4, 2, 3))
    skip = (jnp.eye(CH, dtype=F32)[None, :, None, :, None]
            * jnp.eye(S5_GROUP, dtype=F32)[None, None, :, None, :]
            * d_skip.reshape(S5_GROUPS, 1, S5_GROUP, 1, 1))
    mt = (mt + skip).reshape(NJ, GPT, CH, S5_GROUP, CH * S5_GROUP)
    a_m = jnp.transpose(mt, (0, 2, 1, 3, 4)).reshape(NJ, CL, CH * S5_GROUP)

    ef = (CH - 1) - jnp.arange(CH)
    eb = jnp.arange(CH)

    def q_part(d, e):
        pr = jnp.transpose(pw_re[d][..., e], (0, 2, 1))[:, :, None, :]
        pi = jnp.transpose(pw_im[d][..., e], (0, 2, 1))[:, :, None, :]
        br = jnp.transpose(bb_re[d], (0, 2, 1))[:, None, :, :]
        bi = jnp.transpose(bb_im[d], (0, 2, 1))[:, None, :, :]
        return pr * br - pi * bi, pr * bi + pi * br
    qfr, qfi = q_part(0, ef)
    qbr, qbi = q_part(1, eb)
    q = jnp.stack([qfr, qfi, qbr, qbi], axis=0).reshape(4, NJ, GPT, CH, S5_GROUP, S5_STATE)
    a_q = jnp.transpose(q, (1, 3, 2, 4, 0, 5)).reshape(NJ, CL, 4 * S5_STATE)

    of = jnp.arange(CH) + 1
    ob = CH - jnp.arange(CH)

    def p_part(d, e):
        pr = pw_re[d][..., e][:, :, :, None]
        pi = pw_im[d][..., e][:, :, :, None]
        cr = jnp.transpose(c_re[d], (0, 2, 1))[:, :, None, :]
        ci = jnp.transpose(c_im[d], (0, 2, 1))[:, :, None, :]
        return cr * pr - ci * pi, -(cr * pi + ci * pr)
    pfr, pfi = p_part(0, of)
    pbr, pbi = p_part(1, ob)
    p = jnp.stack([pfr, pfi, pbr, pbi], axis=0).reshape(4, NJ, GPT, S5_STATE, CH, S5_GROUP)
    a_p = jnp.transpose(p, (1, 0, 2, 3, 4, 5)).reshape(NJ, SW, CH * S5_GROUP)

    def lanes(v):
        return jnp.transpose(v.reshape(2, NJ, GPT * S5_STATE), (1, 0, 2))
    c_r, c_i = apow(jnp.full((1,), float(CH), F32))
    s_r, s_i = apow(jnp.full((1,), float(CH * SEG), F32))
    cr, ci, sr, si = (lanes(v[..., 0]) for v in (c_r, c_i, s_r, s_i))
    trans = jnp.stack([cr[:, 0], ci[:, 0], cr[:, 1], ci[:, 1],
                       sr[:, 0], si[:, 0], sr[:, 1], si[:, 1]], axis=1)

    cidx = jnp.arange(N_CHUNK_CTX, dtype=F32)
    wf_r, wf_i = apow(CH * (N_CHUNK_CTX - 1 - cidx))
    wb_r, wb_i = apow(CH * cidx)

    def ctx_lanes(v, d):
        return jnp.transpose(v[d].reshape(NJ, GPT * S5_STATE, N_CHUNK_CTX), (0, 2, 1))
    ctx_w = jnp.stack([ctx_lanes(wf_r, 0), ctx_lanes(wf_i, 0),
                       ctx_lanes(wb_r, 1), ctx_lanes(wb_i, 1)], axis=1)
    return a_m.astype(BF16), a_q.astype(BF16), a_p.astype(BF16), trans, ctx_w


def _s5w_kernel(am_ref, aq_ref, ap_ref, cm_ref, cq_ref, wm_ref, wq_ref, wp_ref):
    def expand(a, c, row_shift, col_shift):
        w = jnp.dot(a, c, preferred_element_type=F32)
        rg = (lax.broadcasted_iota(jnp.int32, (w.shape[0], 1), 0) >> row_shift) & (GPT - 1)
        cg = (lax.broadcasted_iota(jnp.int32, (1, w.shape[1]), 1) >> col_shift) & (GPT - 1)
        return jnp.where(rg == cg, w, 0.0).astype(BF16)
    wm_ref[0] = expand(am_ref[0], cm_ref[...], 4, 4)
    wq_ref[0] = expand(aq_ref[0], cq_ref[...], 4, 6)
    wp_ref[0] = expand(ap_ref[0], cm_ref[...], 6, 4)


def _s5_expand(a_m, a_q, a_p):
    c_m = np.kron(np.eye(CH), np.kron(np.ones((1, GPT)), np.eye(S5_GROUP)))
    c_q = np.kron(np.eye(4), np.kron(np.ones((1, GPT)), np.eye(S5_STATE)))
    c_m = jnp.asarray(c_m, F32).astype(BF16)
    c_q = jnp.asarray(c_q, F32).astype(BF16)
    return pl.pallas_call(
        _s5w_kernel,
        grid=(NJ,),
        in_specs=[pl.BlockSpec((1, CL, CH * S5_GROUP), lambda j: (j, 0, 0)),
                  pl.BlockSpec((1, CL, 4 * S5_STATE), lambda j: (j, 0, 0)),
                  pl.BlockSpec((1, SW, CH * S5_GROUP), lambda j: (j, 0, 0)),
                  pl.BlockSpec(c_m.shape, lambda j: (0, 0)),
                  pl.BlockSpec(c_q.shape, lambda j: (0, 0))],
        out_specs=(pl.BlockSpec((1, CL, CL), lambda j: (j, 0, 0)),
                   pl.BlockSpec((1, CL, SW), lambda j: (j, 0, 0)),
                   pl.BlockSpec((1, SW, CL), lambda j: (j, 0, 0))),
        out_shape=(jax.ShapeDtypeStruct((NJ, CL, CL), BF16),
                   jax.ShapeDtypeStruct((NJ, CL, SW), BF16),
                   jax.ShapeDtypeStruct((NJ, SW, CL), BF16)),
        compiler_params=_cparams(("parallel",)),
        name="s5w",
    )(a_m, a_q, a_p, c_m, c_q)


def _s5_kernel(p_ref, pc_ref, wm_ref, wq_ref, wp_ref, tr_ref, cw_ref, y_ref, v_ref):
    nq = NJ
    half = GPT * S5_STATE

    def chunk_rows(ref, r0, nrows):
        return jnp.concatenate([ref[t, pl.ds(r0, nrows), :] for t in range(CH)], axis=-1)

    def fill(k, c):
        r0 = pl.multiple_of(k * SEG, SEG)
        v = jnp.dot(chunk_rows(p_ref, r0, SEG), wq_ref[0], preferred_element_type=F32)
        o0 = pl.multiple_of(k * PITCH, SUBLANES)
        for s in range(4 * nq):
            v_ref[s, pl.ds(o0, SEG), :] = v[:, s * LANES:(s + 1) * LANES]
        return c
    lax.fori_loop(0, NSEG, fill, 0)

    vc = jnp.dot(chunk_rows(pc_ref, 0, N_CHUNK_CTX), wq_ref[0], preferred_element_type=F32)
    vfr, vfi, vbr, vbi = (vc[:, i * half:(i + 1) * half] for i in range(4))
    wfr, wfi, wbr, wbi = (cw_ref[0, i] for i in range(4))
    s0_fr = jnp.sum(wfr * vfr - wfi * vfi, axis=0, keepdims=True)
    s0_fi = jnp.sum(wfr * vfi + wfi * vfr, axis=0, keepdims=True)
    s0_br = jnp.sum(wbr * vbr - wbi * vbi, axis=0, keepdims=True)
    s0_bi = jnp.sum(wbr * vbi + wbi * vbr, axis=0, keepdims=True)

    tr = tr_ref[0]
    afr, afi, abr, abi = (jnp.broadcast_to(tr[i:i + 1], (NSEG, half)) for i in range(4))
    gfr, gfi, gbr, gbi = (tr[i:i + 1] for i in range(4, 8))

    def load_part(part, i):
        return jnp.concatenate(
            [v_ref[part * nq + q, pl.ds(i, NSEG, stride=PITCH), :] for q in range(nq)], axis=-1)

    def store_part(part, i, val):
        for q in range(nq):
            v_ref[part * nq + q, pl.ds(i, NSEG, stride=PITCH), :] = val[:, q * LANES:(q + 1) * LANES]

    def step(i, carry, write):
        fr, fi, br, bi = carry
        ib = SEG - 1 - i
        ufr, ufi = load_part(0, i), load_part(1, i)
        ubr, ubi = load_part(2, ib), load_part(3, ib)
        if write:
            store_part(0, i, fr)
            store_part(1, i, fi)
            store_part(2, ib, br)
            store_part(3, ib, bi)
        return (afr * fr - afi * fi + ufr, afr * fi + afi * fr + ufi,
                abr * br - abi * bi + ubr, abr * bi + abi * br + ubi)

    zero = jnp.zeros((NSEG, half), F32)
    ffr, ffi, fbr, fbi = lax.fori_loop(0, SEG, functools.partial(step, write=False),
                                       (zero, zero, zero, zero))

    rows_fr, rows_fi = [s0_fr], [s0_fi]
    for k in range(1, NSEG):
        pr, pi = rows_fr[-1], rows_fi[-1]
        rows_fr.append(gfr * pr - gfi * pi + ffr[k - 1:k])
        rows_fi.append(gfr * pi + gfi * pr + ffi[k - 1:k])
    rows_br, rows_bi = [s0_br], [s0_bi]
    for k in range(NSEG - 2, -1, -1):
        pr, pi = rows_br[0], rows_bi[0]
        rows_br.insert(0, gbr * pr - gbi * pi + fbr[k + 1:k + 2])
        rows_bi.insert(0, gbr * pi + gbi * pr + fbi[k + 1:k + 2])
    init = tuple(jnp.concatenate(r, axis=0) for r in (rows_fr, rows_fi, rows_br, rows_bi))

    lax.fori_loop(0, SEG, functools.partial(step, write=True), init)

    def emit(k, c):
        r0 = pl.multiple_of(k * SEG, SEG)
        o0 = pl.multiple_of(k * PITCH, SUBLANES)
        b = chunk_rows(p_ref, r0, SEG)
        sin = jnp.concatenate([v_ref[s, pl.ds(o0, SEG), :] for s in range(4 * nq)], axis=-1)
        y = (jnp.dot(b, wm_ref[0], preferred_element_type=F32)
             + jnp.dot(sin.astype(BF16), wp_ref[0], preferred_element_type=F32))
        for t in range(CH):
            y_ref[t, pl.ds(r0, SEG), :] = y[:, t * LANES:(t + 1) * LANES].astype(y_ref.dtype)
        return c
    lax.fori_loop(0, NSEG, emit, 0)


def _s5(p_t, pc_t, w_m, w_q, w_p, trans, ctx_w):
    one = pl.Buffered(1)
    return pl.pallas_call(
        _s5_kernel,
        grid=(NJ,),
        in_specs=[pl.BlockSpec((CH, N_CHUNK, LANES), lambda j: (0, 0, j)),
                  pl.BlockSpec((CH, N_CHUNK_CTX, LANES), lambda j: (0, 0, j)),
                  pl.BlockSpec((1, CL, CL), lambda j: (j, 0, 0), pipeline_mode=one),
                  pl.BlockSpec((1, CL, SW), lambda j: (j, 0, 0), pipeline_mode=one),
                  pl.BlockSpec((1, SW, CL), lambda j: (j, 0, 0), pipeline_mode=one),
                  pl.BlockSpec((1, SUBLANES, GPT * S5_STATE), lambda j: (j, 0, 0)),
                  pl.BlockSpec((1, 4, N_CHUNK_CTX, GPT * S5_STATE), lambda j: (j, 0, 0, 0))],
        out_specs=pl.BlockSpec((CH, N_CHUNK, LANES), lambda j: (0, 0, j)),
        out_shape=jax.ShapeDtypeStruct((CH, N_CHUNK, S5_WIDTH), BF16),
        scratch_shapes=[pltpu.VMEM((4 * NJ, NSEG * PITCH, LANES), F32)],
        compiler_params=_cparams(("parallel",)),
        name="s5",
    )(p_t, pc_t, w_m, w_q, w_p, trans, ctx_w)


def _dft_tables():
    n = np.arange(FN)
    ang = 2.0 * np.pi * np.outer(n, n) / FN
    c, s = np.cos(ang), np.sin(ang)
    st1 = np.block([[c, s], [-s, c]])
    tw = 2.0 * np.pi * np.outer(n, n) / (FN * FN)
    wr, wi = np.cos(tw), -np.sin(tw)
    fr = c[None] * wr[:, None, :] + s[None] * wi[:, None, :]
    fi = c[None] * wi[:, None, :] - s[None] * wr[:, None, :]
    st2 = np.concatenate([fr, -fi], axis=-1)
    scale = 1.0 / math.sqrt(N_TOK * FFT_DIM)
    blk_c = np.kron(np.eye(FFT_GROUPS), c) * scale
    blk_s = np.kron(np.eye(FFT_GROUPS), s) * scale
    fc = np.concatenate([blk_c, -blk_s], axis=1)
    return (jnp.asarray(st1, F32).astype(BF16), jnp.asarray(st2, F32).astype(BF16), jnp.asarray(fc, F32))


def _fft1_kernel(xr_ref, xi_ref, f_ref, yr_ref, yi_ref):
    xs = jnp.concatenate([xr_ref[...], xi_ref[...]], axis=0)
    y = jnp.dot(f_ref[...], xs, preferred_element_type=F32)
    yr_ref[...] = y[:FN].astype(BF16)
    yi_ref[...] = y[FN:].astype(BF16)


def _fft1(xr, xi, st1):
    w = FB * FFT_WIDTH
    spec = pl.BlockSpec((FN, w), lambda i: (0, i))
    xr2 = xr.reshape(FN, FN * FFT_WIDTH)
    xi2 = xi.reshape(FN, FN * FFT_WIDTH)
    return pl.pallas_call(
        _fft1_kernel,
        grid=(FN // FB,),
        in_specs=[spec, spec, pl.BlockSpec((2 * FN, 2 * FN), lambda i: (0, 0))],
        out_specs=(spec, spec),
        out_shape=(jax.ShapeDtypeStruct((FN, FN * FFT_WIDTH), BF16),) * 2,
        compiler_params=_cparams(("parallel",)),
        name="fft1",
    )(xr2, xi2, st1)


def _fft2_kernel(yr_ref, yi_ref, f_ref, z_ref):
    for b in range(FB):
        ys = jnp.concatenate([yr_ref[b * FN:(b + 1) * FN, :], yi_ref[b * FN:(b + 1) * FN, :]], axis=0)
        z = jnp.dot(f_ref[b], ys, preferred_element_type=F32)
        z_ref[:, b * FFT_WIDTH:(b + 1) * FFT_WIDTH] = z.astype(BF16)


def _fft2(yr, yi, st2):
    rows = pl.BlockSpec((FB * FN, FFT_WIDTH), lambda i: (i, 0))
    yr2 = yr.reshape(N_TOK, FFT_WIDTH)
    yi2 = yi.reshape(N_TOK, FFT_WIDTH)
    z = pl.pallas_call(
        _fft2_kernel,
        grid=(FN // FB,),
        in_specs=[rows, rows, pl.BlockSpec((FB, FN, 2 * FN), lambda i: (i, 0, 0))],
        out_specs=pl.BlockSpec((FN, FB * FFT_WIDTH), lambda i: (0, i)),
        out_shape=jax.ShapeDtypeStruct((FN, FN * FFT_WIDTH), BF16),
        compiler_params=_cparams(("parallel",)),
        name="fft2",
    )(yr2, yi2, st2)
    return z.reshape(N_TOK, FFT_WIDTH)


def _gelu_tanh(x):
    return 0.5 * x * (1.0 + jnp.tanh(math.sqrt(2.0 / math.pi) * (x + 0.044715 * (x * x * x))))


def _mix_kernel(x_ref, er_ref, ec_ref, lg_ref, lb_ref, m1_ref, s1_ref, wg_ref, bg_ref,
                yt_ref, zr_ref, wglu_ref, bglu_ref, wbs_ref, wbf_ref, bbf_ref, wo_ref, bo_ref,
                g1_ref, l1g_ref, l1b_ref, m2_ref, s2_ref, wr_ref, br_ref, tri_ref, etri_ref,
                h1_ref, u2_ref, pos_ref, gate_ref, cnt_ref, scr_ref):
    x = x_ref[...] + _pos_code(er_ref, ec_ref, TM)
    h = _layer_norm(x, lg_ref[...], lb_ref[...])
    u = (h * m1_ref[...] + s1_ref[...]).astype(BF16)
    gates = _sigmoid(jnp.dot(u, wg_ref[...], preferred_element_type=F32) + bg_ref[...])

    for t in range(CH):
        for j in range(NJ):
            scr_ref[j, pl.ds(t, TM // CH, stride=CH), :] = (
                yt_ref[t, :, j * LANES:(j + 1) * LANES].astype(F32))
    ys = jnp.concatenate([scr_ref[j] for j in range(NJ)], axis=-1)
    z = jnp.dot(_gelu_tanh(ys).astype(BF16), wglu_ref[...], preferred_element_type=F32) + bglu_ref[...]
    glu = (z[:, :S5_WIDTH] * _sigmoid(z[:, S5_WIDTH:])).astype(BF16)
    y_s5 = jnp.dot(glu, wbs_ref[...], preferred_element_type=F32)
    y_fft = jnp.dot(zr_ref[...], wbf_ref[...], preferred_element_type=F32) + bbf_ref[...]
    mixed = (gates[:, :D] * y_s5 + gates[:, D:] * y_fft).astype(BF16)
    y = jnp.dot(mixed, wo_ref[...], preferred_element_type=F32) + bo_ref[...]
    h1 = _layer_norm(ALPHA * h + g1_ref[...] * y, l1g_ref[...], l1b_ref[...])
    h1_ref[...] = h1
    u2 = h1 * m2_ref[...] + s2_ref[...]
    u2_ref[...] = u2.astype(BF16)

    logits = lax.dot_general(wr_ref[...], u2, (((1,), (1,)), ((), ())),
                             preferred_element_type=F32, precision=HI) + br_ref[:, 0:1]
    eidx = lax.broadcasted_iota(jnp.int32, (N_EXPERTS, TM), 0)
    vals, hots = [], []
    cur = logits
    for _k in range(TOP_K):
        m = jnp.max(cur, axis=0, keepdims=True)
        sel = jnp.min(jnp.where(cur == m, eidx, N_EXPERTS), axis=0, keepdims=True)
        hot = eidx == sel
        cur = jnp.where(hot, -jnp.inf, cur)
        vals.append(m)
        hots.append(hot)
    exps = [jnp.exp(v - vals[0]) for v in vals]
    den = exps[0] + exps[1] + exps[2] + exps[3]
    gate4 = jnp.concatenate([e / den for e in exps], axis=0)

    hot_sum = (hots[0] | hots[1] | hots[2] | hots[3]).astype(F32)
    before = jnp.dot(hot_sum.astype(BF16), tri_ref[...], preferred_element_type=F32)
    cnt = jnp.broadcast_to(jnp.sum(hot_sum, axis=1, keepdims=True), (N_EXPERTS, LANES))
    cnt8 = jnp.floor((cnt + (SEG_ALIGN - 1)) * (1.0 / SEG_ALIGN)) * SEG_ALIGN
    seg0 = jnp.dot(etri_ref[...], cnt8.astype(BF16), preferred_element_type=F32)
    tot = seg0[:, 0:1] + before
    pos4 = jnp.concatenate(
        [jnp.sum(jnp.where(hk, tot, 0.0), axis=0, keepdims=True) for hk in hots], axis=0)
    pos_ref[...] = pos4.astype(jnp.int32)
    cnt_ref[0] = cnt

    gpad = jnp.concatenate([gate4, pos4, jnp.zeros((LANES - 2 * TOP_K, TM), F32)], axis=0)
    gate_ref[...] = gpad.T


def _mix(x, emb_r, emb_c, lg, lb, m1, s1, wg, bg, y_t, zr, wglu, bglu, wbs, wbf, bbf, wo, bo,
         g1, l1g, l1b, m2, s2, wr_t, br, tri, etri):
    vec = pl.BlockSpec((1, D), lambda i: (0, 0))

    def full(a):
        return pl.BlockSpec(a.shape, lambda i: (0,) * a.ndim)
    return pl.pallas_call(
        _mix_kernel,
        grid=(N_TOK // TM,),
        in_specs=[pl.BlockSpec((TM, D), lambda i: (i, 0)),
                  pl.BlockSpec((TM // GRID_W, D // 2), lambda i: (i, 0)),
                  pl.BlockSpec((GRID_W, D // 2), lambda i: (0, 0)),
                  vec, vec, vec, vec, full(wg), full(bg),
                  pl.BlockSpec((CH, TM // CH, S5_WIDTH), lambda i: (0, i, 0)),
                  pl.BlockSpec((TM, FFT_WIDTH), lambda i: (i, 0)),
                  full(wglu), full(bglu), full(wbs), full(wbf), full(bbf), full(wo), full(bo),
                  vec, vec, vec, vec, vec, full(wr_t), full(br), full(tri), full(etri)],
        out_specs=(pl.BlockSpec((TM, D), lambda i: (i, 0)),
                   pl.BlockSpec((TM, D), lambda i: (i, 0)),
                   pl.BlockSpec((TOP_K, TM), lambda i: (0, i)),
                   pl.BlockSpec((TM, LANES), lambda i: (i, 0)),
                   pl.BlockSpec((1, N_EXPERTS, LANES), lambda i: (i, 0, 0))),
        out_shape=(jax.ShapeDtypeStruct((N_TOK, D), F32),
                   jax.ShapeDtypeStruct((N_TOK, D), BF16),
                   jax.ShapeDtypeStruct((TOP_K, N_TOK), jnp.int32),
                   jax.ShapeDtypeStruct((N_TOK, LANES), F32),
                   jax.ShapeDtypeStruct((N_TILES, N_EXPERTS, LANES), F32)),
        scratch_shapes=[pltpu.VMEM((NJ, TM, LANES), F32)],
        compiler_params=_cparams(("parallel",)),
        name="mix",
    )(x, emb_r, emb_c, lg, lb, m1, s1, wg, bg, y_t, zr, wglu, bglu, wbs, wbf, bbf, wo, bo,
      g1, l1g, l1b, m2, s2, wr_t, br, tri, etri)


def _on_parity(i, fn):
    @pl.when(i % 2 == 0)
    def _():
        fn(0)

    @pl.when(i % 2 == 1)
    def _():
        fn(1)


def _dispatch_kernel(pend_ref, padded_ref, nchk_ref, dest_ref, pos_ref, u_ref, buf_ref,
                     sorted_ref, zero_ref, zsem, sems):
    i = pl.program_id(0)

    @pl.when(i == 0)
    def _():
        zero_ref[...] = jnp.zeros_like(zero_ref)
        n_used = pend_ref[N_EXPERTS - 1] // BM

        def clear_copy(start):
            return pltpu.make_async_copy(
                zero_ref, buf_ref.at[pl.ds(pl.multiple_of(start, BM), BM)], zsem)

        def each(fn):
            def expert(e, c):
                @pl.when(padded_ref[e] > 0)
                def _():
                    fn(clear_copy(pend_ref[e] - BM))
                return c
            lax.fori_loop(0, N_EXPERTS, expert, 0)

            def tail(b, c):
                fn(clear_copy(b * BM))
                return c
            lax.fori_loop(n_used, N_BLOCKS, tail, 0)
        each(lambda cp: cp.start())
        each(lambda cp: cp.wait())

    def drain(slot, tile):
        n = pl.multiple_of(nchk_ref[tile] * SEG_ALIGN, SEG_ALIGN)
        pltpu.make_async_copy(sorted_ref.at[slot, pl.ds(0, n)], buf_ref.at[pl.ds(0, n)],
                              sems.at[slot]).wait()

    def run(slot):
        pos = pos_ref[...]
        u = u_ref[...]
        for rb in range(CAP // CAP_BLOCK):
            rows = lax.broadcasted_iota(jnp.int32, (CAP_BLOCK, TM), 0) + rb * CAP_BLOCK
            hit = rows == pos[0:1]
            for k in range(1, TOP_K):
                hit = hit | (rows == pos[k:k + 1])
            onehot = jnp.where(hit, 1.0, 0.0).astype(BF16)
            sorted_ref[slot, rb * CAP_BLOCK:(rb + 1) * CAP_BLOCK, :] = jnp.dot(
                onehot, u, preferred_element_type=F32)

        def issue(j, c):
            src = pl.multiple_of(j * SEG_ALIGN, SEG_ALIGN)
            dst = pl.multiple_of(dest_ref[0, 0, j], SEG_ALIGN)
            pltpu.make_async_copy(sorted_ref.at[slot, pl.ds(src, SEG_ALIGN)],
                                  buf_ref.at[pl.ds(dst, SEG_ALIGN)], sems.at[slot]).start()
            return c
        lax.fori_loop(0, nchk_ref[i], issue, 0)

        @pl.when(i > 0)
        def _():
            drain(1 - slot, i - 1)

        @pl.when(i == N_TILES - 1)
        def _():
            drain(slot, i)
    _on_parity(i, run)


def _dispatch(pad_ends, padded, nchk, chunk_dest, pos_t, u2):
    return pl.pallas_call(
        _dispatch_kernel,
        grid_spec=pltpu.PrefetchScalarGridSpec(
            num_scalar_prefetch=3,
            grid=(N_TILES,),
            in_specs=[pl.BlockSpec((1, 1, NCHK), lambda i, a, b, c: (i, 0, 0),
                                   memory_space=pltpu.SMEM),
                      pl.BlockSpec((TOP_K, TM), lambda i, a, b, c: (0, i)),
                      pl.BlockSpec((TM, D), lambda i, a, b, c: (i, 0))],
            out_specs=pl.BlockSpec(memory_space=pl.ANY),
            scratch_shapes=[pltpu.VMEM((2, CAP, D), F32),
                            pltpu.VMEM((BM, D), F32),
                            pltpu.SemaphoreType.DMA(()),
                            pltpu.SemaphoreType.DMA((2,))]),
        out_shape=jax.ShapeDtypeStruct((ROWS, D), F32),
        compiler_params=_cparams(("arbitrary",)),
        name="dispatch",
    )(pad_ends, padded, nchk, chunk_dest, pos_t, u2)


def _ffn_kernel(be_ref, nu_ref, x_ref, wu_ref, bu_ref, wd_ref, bd_ref, y_ref, wub_ref, wdb_ref):
    i = pl.program_id(0)
    used = i < nu_ref[0]

    @pl.when(used)
    def _():
        prev = be_ref[jnp.maximum(i - 1, 0)]

        @pl.when((i == 0) | (be_ref[i] != prev))
        def _():
            wub_ref[...] = wu_ref[0].astype(BF16)
            wdb_ref[...] = wd_ref[0].astype(BF16)

        h = jnp.dot(x_ref[...].astype(BF16), wub_ref[...], preferred_element_type=F32) + bu_ref[0]
        h_glu = jnp.minimum(h[:, :D], SWIGLU_LIMIT)
        h_lin = jnp.clip(h[:, D:], -SWIGLU_LIMIT, SWIGLU_LIMIT)
        act = (h_glu * _sigmoid(SWIGLU_ALPHA * h_glu) * (h_lin + 1.0)).astype(BF16)
        y_ref[...] = jnp.dot(act, wdb_ref[...], preferred_element_type=F32) + bd_ref[0]

    @pl.when(jnp.logical_not(used))
    def _():
        y_ref[...] = jnp.zeros_like(y_ref)


def _ffn(block_expert, n_used, buf, w_up, b_up, w_down, b_down):
    def blk(i, be, nu):
        return jnp.minimum(i, nu[0] - 1)
    return pl.pallas_call(
        _ffn_kernel,
        grid_spec=pltpu.PrefetchScalarGridSpec(
            num_scalar_prefetch=2,
            grid=(N_BLOCKS,),
            in_specs=[pl.BlockSpec((BM, D), lambda i, be, nu: (blk(i, be, nu), 0)),
                      pl.BlockSpec((1, D, 2 * D), lambda i, be, nu: (be[blk(i, be, nu)], 0, 0)),
                      pl.BlockSpec((1, 1, 2 * D), lambda i, be, nu: (be[blk(i, be, nu)], 0, 0)),
                      pl.BlockSpec((1, D, D), lambda i, be, nu: (be[blk(i, be, nu)], 0, 0)),
                      pl.BlockSpec((1, 1, D), lambda i, be, nu: (be[blk(i, be, nu)], 0, 0))],
            out_specs=pl.BlockSpec((BM, D), lambda i, be, nu: (i, 0)),
            scratch_shapes=[pltpu.VMEM((D, 2 * D), BF16),
                            pltpu.VMEM((D, D), BF16)]),
        out_shape=jax.ShapeDtypeStruct((ROWS, D), F32),
        compiler_params=_cparams(("arbitrary",)),
        name="ffn",
    )(block_expert, n_used, buf, w_up, b_up, w_down, b_down)


def _combine_kernel(nchk_ref, dest_ref, dnext_ref, y_ref, h1_ref, gate_ref, g2_ref, lg_ref, lb_ref,
                    o_ref, sorted_ref, sems):
    i = pl.program_id(0)

    def fetch(slot, tile, table_ref):
        def issue(j, c):
            src = pl.multiple_of(table_ref[0, 0, j], SEG_ALIGN)
            dst = pl.multiple_of(j * SEG_ALIGN, SEG_ALIGN)
            pltpu.make_async_copy(y_ref.at[pl.ds(src, SEG_ALIGN)],
                                  sorted_ref.at[slot, pl.ds(dst, SEG_ALIGN)], sems.at[slot]).start()
            return c
        lax.fori_loop(0, nchk_ref[tile], issue, 0)

    def drain(slot, tile):
        n = pl.multiple_of(nchk_ref[tile] * SEG_ALIGN, SEG_ALIGN)
        pltpu.make_async_copy(y_ref.at[pl.ds(0, n)], sorted_ref.at[slot, pl.ds(0, n)],
                              sems.at[slot]).wait()

    @pl.when(i == 0)
    def _():
        sorted_ref[...] = jnp.zeros_like(sorted_ref)
        fetch(0, 0, dest_ref)

    def run(slot):
        @pl.when(i + 1 < N_TILES)
        def _():
            fetch(1 - slot, i + 1, dnext_ref)
        drain(slot, i)

        gp = gate_ref[...]
        m = jnp.zeros((TM, D), F32)
        for cb in range(CAP // CAP_BLOCK):
            cols = (lax.broadcasted_iota(jnp.int32, (TM, CAP_BLOCK), 1) + cb * CAP_BLOCK).astype(F32)
            g = jnp.where(cols == gp[:, TOP_K:TOP_K + 1], gp[:, 0:1], 0.0)
            for k in range(1, TOP_K):
                g = g + jnp.where(cols == gp[:, TOP_K + k:TOP_K + k + 1], gp[:, k:k + 1], 0.0)
            rows = sorted_ref[slot, cb * CAP_BLOCK:(cb + 1) * CAP_BLOCK, :].astype(BF16)
            m = m + jnp.dot(g.astype(BF16), rows, preferred_element_type=F32)
        o_ref[...] = _layer_norm(ALPHA * h1_ref[...] + g2_ref[...] * m, lg_ref[...], lb_ref[...])
    _on_parity(i, run)


def _combine(nchk, chunk_dest, y_buf, h1, gate_tok, g2, lg, lb):
    vec = pl.BlockSpec((1, D), lambda i, n: (0, 0))
    return pl.pallas_call(
        _combine_kernel,
        grid_spec=pltpu.PrefetchScalarGridSpec(
            num_scalar_prefetch=1,
            grid=(N_TILES,),
            in_specs=[pl.BlockSpec((1, 1, NCHK), lambda i, n: (i, 0, 0), memory_space=pltpu.SMEM),
                      pl.BlockSpec((1, 1, NCHK), lambda i, n: (jnp.minimum(i + 1, N_TILES - 1), 0, 0),
                                   memory_space=pltpu.SMEM),
                      pl.BlockSpec(memory_space=pl.ANY),
                      pl.BlockSpec((TM, D), lambda i, n: (i, 0)),
                      pl.BlockSpec((TM, LANES), lambda i, n: (i, 0)),
                      vec, vec, vec],
            out_specs=pl.BlockSpec((TM, D), lambda i, n: (i, 0)),
            scratch_shapes=[pltpu.VMEM((2, CAP, D), F32),
                            pltpu.SemaphoreType.DMA((2,))]),
        out_shape=jax.ShapeDtypeStruct((N_TOK, D), F32),
        compiler_params=_cparams(("arbitrary",)),
        name="combine",
    )(nchk, chunk_dest, chunk_dest, y_buf, h1, gate_tok, g2, lg, lb)


def _sincos_tables():
    q = D // 4
    omega = 1.0 / (10000.0 ** (jnp.arange(q, dtype=F32) / q))

    def emb(n):
        ang = jnp.arange(n, dtype=F32)[:, None] * omega[None, :]
        return jnp.concatenate([jnp.sin(ang), jnp.cos(ang)], axis=-1)
    return emb(N_TOK // GRID_W), emb(GRID_W)


def _tile_slots(a, tile):
    return jnp.transpose(a.reshape(TOP_K, N_TOK // tile, tile), (1, 0, 2)).reshape(N_TOK // tile, 1, TOP_K * tile)


def kernel(x, c, ctx, c_ctx, ln_in_g, ln_in_b, w_ada, b_ada, w_in, b_in, s5_lambda_re, s5_lambda_im, s5_log_dt, s5_b_re, s5_b_im, s5_c_re, s5_c_im, s5_d, w_glu, b_glu, w_br_s5, w_br_fft, b_br_fft, w_out, b_out, ln1_g, ln1_b, w_router, b_router, w_up, b_up, w_down, b_down, ln2_g, ln2_b):
    assert x.shape == (1, N_TOK, D) and ctx.shape == (1, N_CTX, D) and w_ada.shape[0] == 1
    row = lambda v: v.reshape(1, -1).astype(F32)

    cc = jnp.concatenate([c.reshape(1, D), c_ctx.reshape(1, D), jnp.zeros((SUBLANES - 2, D), F32)], axis=0)
    ada = _ada(cc, w_ada[0], row(b_ada[0]))
    sh1, sc1, g1, sh2, sc2, g2 = (ada[0:1, k * D:(k + 1) * D] for k in range(6))
    sh1c, sc1c = ada[1:2, 0:D], ada[1:2, D:2 * D]

    emb_r, emb_c = _sincos_tables()
    st1, st2, fc = _dft_tables()
    lg, lb = row(ln_in_g), row(ln_in_b)

    w_s5 = w_in[0][:, :S5_WIDTH]
    w_fft = w_in[0][:, S5_WIDTH:S5_WIDTH + FFT_WIDTH]
    w_g = w_in[0][:, S5_WIDTH + FFT_WIDTH:]
    b_s5 = row(b_in[0][:S5_WIDTH])
    b_fft8 = jnp.concatenate([row(b_in[0][S5_WIDTH:S5_WIDTH + FFT_WIDTH]),
                              jnp.zeros((SUBLANES - 1, FFT_WIDTH), F32)], axis=0)
    b_g = row(b_in[0][S5_WIDTH + FFT_WIDTH:])
    w_fc, b_fc = _fft_weights(w_fft, b_fft8, fc)
    wcat = jnp.concatenate([w_s5, w_fc], axis=1).astype(BF16)
    bcat = jnp.concatenate([b_s5, b_fc[0:1]], axis=1)

    x2 = x[0]
    p_t, xr, xi = _proj(x2, emb_r, emb_c, lg, lb, 1.0 + sc1, sh1, wcat, bcat)
    pc_t = _ctx_proj(ctx[0], lg, lb, 1.0 + sc1c, sh1c, w_s5.astype(BF16), b_s5)

    a_m, a_q, a_p, trans, ctx_w = _s5_tables(
        s5_lambda_re[0], s5_lambda_im[0], s5_log_dt[0], s5_b_re[0], s5_b_im[0],
        s5_c_re[0], s5_c_im[0], s5_d[0])
    w_m, w_q, w_p = _s5_expand(a_m, a_q, a_p)
    y_t = _s5(p_t, pc_t, w_m, w_q, w_p, trans, ctx_w)

    yr, yi = _fft1(xr, xi, st1)
    zr = _fft2(yr, yi, st2)

    tri = (jnp.arange(TM)[:, None] < jnp.arange(TM)[None, :]).astype(BF16)
    br = jnp.broadcast_to(b_router[0].reshape(N_EXPERTS, 1), (N_EXPERTS, LANES))
    etri = (jnp.arange(N_EXPERTS)[:, None] > jnp.arange(N_EXPERTS)[None, :]).astype(BF16)
    h1, u2, pos_t, gate_tok, counts = _mix(
        x2, emb_r, emb_c, lg, lb, 1.0 + sc1, sh1, w_g.astype(BF16), b_g, y_t, zr,
        w_glu[0].astype(BF16), row(b_glu[0]), w_br_s5[0].astype(BF16), w_br_fft[0].astype(BF16),
        row(b_br_fft[0]), w_out[0].astype(BF16), row(b_out[0]), g1, row(ln1_g[0]), row(ln1_b[0]),
        1.0 + sc2, sh2, jnp.transpose(w_router[0]), br, tri, etri)

    cnt = counts[:, :, 0].astype(jnp.int32)
    seg = (cnt + SEG_ALIGN - 1) // SEG_ALIGN * SEG_ALIGN
    seg_end = jnp.cumsum(seg, axis=1)
    seg_start = seg_end - seg
    padded = (jnp.sum(seg, axis=0) + BM - 1) // BM * BM
    pad_ends = jnp.cumsum(padded)
    seg_dest = (pad_ends - padded)[None, :] + jnp.cumsum(seg, axis=0) - seg
    chunk_row = jnp.arange(NCHK, dtype=jnp.int32) * SEG_ALIGN
    chunk_exp = jnp.minimum(jnp.sum(chunk_row[None, :, None] >= seg_end[:, None, :], axis=-1),
                            N_EXPERTS - 1)
    own = chunk_exp[:, :, None] == jnp.arange(N_EXPERTS, dtype=jnp.int32)[None, None, :]
    chunk_dest = (jnp.sum(jnp.where(own, (seg_dest - seg_start)[:, None, :], 0), axis=-1)
                  + chunk_row[None, :]).astype(jnp.int32).reshape(N_TILES, 1, NCHK)
    nchk = (seg_end[:, -1] // SEG_ALIGN).astype(jnp.int32)
    block_start = jnp.arange(N_BLOCKS, dtype=jnp.int32) * BM
    block_expert = jnp.minimum(jnp.sum(block_start[:, None] >= pad_ends[None, :], axis=1),
                               N_EXPERTS - 1).astype(jnp.int32)
    n_used = (pad_ends[-1:] // BM).astype(jnp.int32)

    buf = _dispatch(pad_ends.astype(jnp.int32), padded.astype(jnp.int32), nchk, chunk_dest, pos_t, u2)
    y_buf = _ffn(block_expert, n_used, buf, w_up[0], b_up[0].reshape(N_EXPERTS, 1, 2 * D),
                 w_down[0], b_down[0].reshape(N_EXPERTS, 1, D))
    out = _combine(nchk, chunk_dest, y_buf, h1, gate_tok, g2, row(ln2_g[0]), row(ln2_b[0]))
    return out.reshape(1, N_TOK, D)
```

```python
import functools
import math

import jax
import jax.numpy as jnp
import numpy as np
from jax import lax
from jax.experimental import pallas as pl
from jax.experimental.pallas import tpu as pltpu

F32 = jnp.float32
BF16 = jnp.bfloat16
HI = lax.Precision.HIGHEST

D = 1024
N_TOK = 16384
N_CTX = 256
GRID_W = 64
S5_GROUP = 16
S5_GROUPS = 32
S5_STATE = 64
S5_WIDTH = 512
FFT_GROUPS = 4
FFT_DIM = 128
FFT_WIDTH = 512
N_EXPERTS = 32
TOP_K = 4
LN_EPS = 1e-5
ALPHA = 2.0 ** 0.25
SWIGLU_ALPHA = 1.702
SWIGLU_LIMIT = 7.0

LANES = 128
SUBLANES = 8
VMEM_LIMIT = 56 * 1024 * 1024

CH = 8
N_CHUNK = N_TOK // CH
N_CHUNK_CTX = N_CTX // CH
NSEG = SUBLANES
SEG = N_CHUNK // NSEG
PITCH = SEG + 8
GPT = LANES // S5_GROUP
NJ = S5_WIDTH // LANES
CL = CH * LANES
SW = 4 * GPT * S5_STATE

FN = 128
FB = 16

TM = 512
N_TILES = N_TOK // TM
BM = 512
N_SLOTS = N_TOK * TOP_K
SEG_ALIGN = SUBLANES
CAP_BLOCK = 256
CAP = -(-(TOP_K * TM + N_EXPERTS * (SEG_ALIGN - 1)) // CAP_BLOCK) * CAP_BLOCK
NCHK = CAP // SEG_ALIGN
N_BLOCKS = -(-(N_SLOTS + N_TILES * N_EXPERTS * (SEG_ALIGN - 1)) // BM) + N_EXPERTS
ROWS = N_BLOCKS * BM


def _cparams(sem):
    return pltpu.CompilerParams(dimension_semantics=sem, vmem_limit_bytes=VMEM_LIMIT)


def _layer_norm(x, g, b):
    mu = jnp.mean(x, axis=-1, keepdims=True)
    xc = x - mu
    var = jnp.mean(xc * xc, axis=-1, keepdims=True)
    return xc * lax.rsqrt(var + LN_EPS) * g + b


def _sigmoid(x):
    return 1.0 / (1.0 + jnp.exp(-x))


def _ada_kernel(c_ref, w_ref, b_ref, o_ref):
    c = c_ref[...]
    s = c * _sigmoid(c)
    o_ref[...] = jnp.dot(s, w_ref[...], preferred_element_type=F32, precision=HI) + b_ref[...]


def _ada(cc, w_ada, b_ada):
    nb = 4
    wb = 6 * D // nb
    return pl.pallas_call(
        _ada_kernel,
        grid=(nb,),
        in_specs=[pl.BlockSpec((SUBLANES, D), lambda i: (0, 0)),
                  pl.BlockSpec((D, wb), lambda i: (0, i)),
                  pl.BlockSpec((1, wb), lambda i: (0, i))],
        out_specs=pl.BlockSpec((SUBLANES, wb), lambda i: (0, i)),
        out_shape=jax.ShapeDtypeStruct((SUBLANES, 6 * D), F32),
        compiler_params=_cparams(("parallel",)),
        name="ada",
    )(cc, w_ada, b_ada)


def _fftw_kernel(w_ref, b_ref, f_ref, wo_ref, bo_ref):
    f = f_ref[...]
    wo_ref[...] = jnp.dot(w_ref[...], f, preferred_element_type=F32, precision=HI)
    bo_ref[...] = jnp.dot(b_ref[...], f, preferred_element_type=F32, precision=HI)


def _fft_weights(w_fft, b_fft8, fc):
    return pl.pallas_call(
        _fftw_kernel,
        out_shape=(jax.ShapeDtypeStruct((D, 2 * FFT_WIDTH), F32),
                   jax.ShapeDtypeStruct((SUBLANES, 2 * FFT_WIDTH), F32)),
        compiler_params=pltpu.CompilerParams(vmem_limit_bytes=VMEM_LIMIT),
        name="fftw",
    )(w_fft, b_fft8, fc)


def _pos_code(er_ref, ec_ref, tm):
    nr = tm // GRID_W
    er = er_ref[...]
    row = jnp.broadcast_to(er[:, None, :], (nr, GRID_W, D // 2)).reshape(tm, D // 2)
    col = jnp.concatenate([ec_ref[...]] * nr, axis=0)
    return jnp.concatenate([row, col], axis=-1)


def _to_chunk_major(val, scr_ref, out_ref, tm):
    for j in range(NJ):
        scr_ref[j] = val[:, j * LANES:(j + 1) * LANES]
    for t in range(CH):
        for j in range(NJ):
            piece = scr_ref[j, pl.ds(t, tm // CH, stride=CH), :]
            out_ref[t, :, j * LANES:(j + 1) * LANES] = piece.astype(out_ref.dtype)


def _proj_kernel(x_ref, er_ref, ec_ref, lg_ref, lb_ref, m_ref, s_ref, w_ref, b_ref,
                 p_ref, xr_ref, xi_ref, scr_ref):
    x = x_ref[...] + _pos_code(er_ref, ec_ref, TM)
    h = _layer_norm(x, lg_ref[...], lb_ref[...])
    u = (h * m_ref[...] + s_ref[...]).astype(BF16)
    p = jnp.dot(u, w_ref[...], preferred_element_type=F32) + b_ref[...]
    _to_chunk_major(p[:, :S5_WIDTH], scr_ref, p_ref, TM)
    xr_ref[...] = p[:, S5_WIDTH:S5_WIDTH + FFT_WIDTH].astype(BF16)
    xi_ref[...] = p[:, S5_WIDTH + FFT_WIDTH:].astype(BF16)


def _proj(x, emb_r, emb_c, lg, lb, m1, s1, wcat, bcat):
    nw = wcat.shape[1]
    vec = pl.BlockSpec((1, D), lambda i: (0, 0))
    return pl.pallas_call(
        _proj_kernel,
        grid=(N_TOK // TM,),
        in_specs=[pl.BlockSpec((TM, D), lambda i: (i, 0)),
                  pl.BlockSpec((TM // GRID_W, D // 2), lambda i: (i, 0)),
                  pl.BlockSpec((GRID_W, D // 2), lambda i: (0, 0)),
                  vec, vec, vec, vec,
                  pl.BlockSpec((D, nw), lambda i: (0, 0)),
                  pl.BlockSpec((1, nw), lambda i: (0, 0))],
        out_specs=(pl.BlockSpec((CH, TM // CH, S5_WIDTH), lambda i: (0, i, 0)),
                   pl.BlockSpec((TM, FFT_WIDTH), lambda i: (i, 0)),
                   pl.BlockSpec((TM, FFT_WIDTH), lambda i: (i, 0))),
        out_shape=(jax.ShapeDtypeStruct((CH, N_CHUNK, S5_WIDTH), BF16),
                   jax.ShapeDtypeStruct((N_TOK, FFT_WIDTH), BF16),
                   jax.ShapeDtypeStruct((N_TOK, FFT_WIDTH), BF16)),
        scratch_shapes=[pltpu.VMEM((NJ, TM, LANES), F32)],
        compiler_params=_cparams(("parallel",)),
        name="proj",
    )(x, emb_r, emb_c, lg, lb, m1, s1, wcat, bcat)


def _ctx_proj_kernel(x_ref, lg_ref, lb_ref, m_ref, s_ref, w_ref, b_ref, p_ref, scr_ref):
    h = _layer_norm(x_ref[...], lg_ref[...], lb_ref[...])
    u = (h * m_ref[...] + s_ref[...]).astype(BF16)
    p = jnp.dot(u, w_ref[...], preferred_element_type=F32) + b_ref[...]
    _to_chunk_major(p, scr_ref, p_ref, N_CTX)


def _ctx_proj(ctx, lg, lb, m1, s1, w_s5, b_s5):
    return pl.pallas_call(
        _ctx_proj_kernel,
        out_shape=jax.ShapeDtypeStruct((CH, N_CHUNK_CTX, S5_WIDTH), BF16),
        scratch_shapes=[pltpu.VMEM((NJ, N_CTX, LANES), F32)],
        compiler_params=pltpu.CompilerParams(vmem_limit_bytes=VMEM_LIMIT),
        name="ctxproj",
    )(ctx, lg, lb, m1, s1, w_s5, b_s5)


def _s5_tables(lam_re, lam_im, log_dt, b_re, b_im, c_re, c_im, d_skip):
    dt = jnp.exp(log_dt)[..., None]
    zr = lam_re * dt
    zi = lam_im * dt

    def apow(m):
        m = jnp.asarray(m, F32)
        mag = jnp.exp(zr[..., None] * m)
        return mag * jnp.cos(zi[..., None] * m), mag * jnp.sin(zi[..., None] * m)

    a_re, a_im = apow(jnp.ones((1,), F32))
    a_re, a_im = a_re[..., 0], a_im[..., 0]
    den = lam_re * lam_re + lam_im * lam_im
    num_re = a_re - 1.0
    k_re = (num_re * lam_re + a_im * lam_im) / den
    k_im = (a_im * lam_re - num_re * lam_im) / den
    bb_re = k_re[..., None] * b_re - k_im[..., None] * b_im
    bb_im = k_re[..., None] * b_im + k_im[..., None] * b_re

    ks = jnp.arange(CH + 1, dtype=F32)
    pw_re, pw_im = apow(ks)

    def tap(c, pw, bb):
        return jnp.einsum('dgvp,dgpk,dgph->kdghv', c, pw[..., :CH], bb, precision=HI)
    taps = (tap(c_re, pw_re, bb_re) - tap(c_re, pw_im, bb_im)
            - tap(c_im, pw_re, bb_im) - tap(c_im, pw_im, bb_re))
    skip = d_skip.reshape(S5_GROUPS, S5_GROUP, 1) * jnp.eye(S5_GROUP, dtype=F32)[None]
    taps = taps.at[0, 0].add(skip)
    b_c = jnp.transpose(taps.reshape(CH, 2, NJ, LANES, S5_GROUP), (2, 0, 1, 3, 4))
    b_c = b_c.reshape(NJ, 2 * CH, LANES, S5_GROUP)

    ef = (CH - 1) - jnp.arange(CH)
    eb = jnp.arange(CH)

    def q_part(d, e):
        pr = jnp.transpose(pw_re[d][..., e], (0, 2, 1))[:, :, None, :]
        pi = jnp.transpose(pw_im[d][..., e], (0, 2, 1))[:, :, None, :]
        br = jnp.transpose(bb_re[d], (0, 2, 1))[:, None, :, :]
        bi = jnp.transpose(bb_im[d], (0, 2, 1))[:, None, :, :]
        return pr * br - pi * bi, pr * bi + pi * br

    def q_rows(v):
        v = v.reshape(NJ, GPT, CH, S5_GROUP, S5_STATE)
        return jnp.transpose(v, (0, 2, 1, 3, 4)).reshape(NJ, CL, S5_STATE)
    a_q = jnp.stack([q_rows(v) for v in q_part(0, ef) + q_part(1, eb)], axis=0)

    of = jnp.arange(CH) + 1
    ob = CH - jnp.arange(CH)

    def p_part(d, e):
        pr = pw_re[d][..., e][:, :, :, None]
        pi = pw_im[d][..., e][:, :, :, None]
        cr = jnp.transpose(c_re[d], (0, 2, 1))[:, :, None, :]
        ci = jnp.transpose(c_im[d], (0, 2, 1))[:, :, None, :]
        return cr * pr - ci * pi, -(cr * pi + ci * pr)
    a_p = jnp.stack([v.reshape(NJ, GPT * S5_STATE, CH * S5_GROUP)
                     for v in p_part(0, of) + p_part(1, ob)], axis=0)

    def lanes(v):
        return jnp.transpose(v.reshape(2, NJ, GPT * S5_STATE), (1, 0, 2))
    c_r, c_i = apow(jnp.full((1,), float(CH), F32))
    s_r, s_i = apow(jnp.full((1,), float(CH * SEG), F32))
    cr, ci, sr, si = (lanes(v[..., 0]) for v in (c_r, c_i, s_r, s_i))
    trans = jnp.stack([cr[:, 0], ci[:, 0], cr[:, 1], ci[:, 1],
                       sr[:, 0], si[:, 0], sr[:, 1], si[:, 1]], axis=1)

    cidx = jnp.arange(N_CHUNK_CTX, dtype=F32)
    wf_r, wf_i = apow(CH * (N_CHUNK_CTX - 1 - cidx))
    wb_r, wb_i = apow(CH * cidx)

    def ctx_lanes(v, d):
        return jnp.transpose(v[d].reshape(NJ, GPT * S5_STATE, N_CHUNK_CTX), (0, 2, 1))
    ctx_w = jnp.stack([ctx_lanes(wf_r, 0), ctx_lanes(wf_i, 0),
                       ctx_lanes(wb_r, 1), ctx_lanes(wb_i, 1)], axis=1)
    return b_c, a_q, a_p, trans, ctx_w


def _s5w_kernel(bc_ref, aq_ref, ap_ref, c16_ref, c64_ref, cm_ref, wm_ref, wq_ref, wp_ref):
    def expand(a, c, row_shift, col_shift):
        w = jnp.dot(a, c, preferred_element_type=F32, precision=HI)
        rg = (lax.broadcasted_iota(jnp.int32, (w.shape[0], 1), 0) >> row_shift) & (GPT - 1)
        cg = (lax.broadcasted_iota(jnp.int32, (1, w.shape[1]), 1) >> col_shift) & (GPT - 1)
        return jnp.where(rg == cg, w, 0.0)

    blk = [expand(bc_ref[0, kd], c16_ref[...], 4, 4) for kd in range(2 * CH)]
    for t in range(CH):
        for u in range(CH):
            b = blk[2 * (u - t)] if u > t else blk[2 * (t - u) + 1] if u < t else blk[0] + blk[1]
            wm_ref[0, t * LANES:(t + 1) * LANES, u * LANES:(u + 1) * LANES] = b.astype(BF16)
    half = GPT * S5_STATE
    for s in range(4):
        wq_ref[0, :, s * half:(s + 1) * half] = expand(aq_ref[s, 0], c64_ref[...], 4, 6).astype(BF16)
        wp_ref[0, s * half:(s + 1) * half, :] = expand(ap_ref[s, 0], cm_ref[...], 6, 4).astype(BF16)


def _s5_expand(b_c, a_q, a_p):
    rep = np.ones((1, GPT))
    c16 = jnp.asarray(np.kron(rep, np.eye(S5_GROUP)), F32)
    c64 = jnp.asarray(np.kron(rep, np.eye(S5_STATE)), F32)
    c_m = jnp.asarray(np.kron(np.eye(CH), np.kron(rep, np.eye(S5_GROUP))), F32)
    half = GPT * S5_STATE
    return pl.pallas_call(
        _s5w_kernel,
        grid=(NJ,),
        in_specs=[pl.BlockSpec((1, 2 * CH, LANES, S5_GROUP), lambda j: (j, 0, 0, 0)),
                  pl.BlockSpec((4, 1, CL, S5_STATE), lambda j: (0, j, 0, 0)),
                  pl.BlockSpec((4, 1, half, CH * S5_GROUP), lambda j: (0, j, 0, 0)),
                  pl.BlockSpec(c16.shape, lambda j: (0, 0)),
                  pl.BlockSpec(c64.shape, lambda j: (0, 0)),
                  pl.BlockSpec(c_m.shape, lambda j: (0, 0))],
        out_specs=(pl.BlockSpec((1, CL, CL), lambda j: (j, 0, 0)),
                   pl.BlockSpec((1, CL, SW), lambda j: (j, 0, 0)),
                   pl.BlockSpec((1, SW, CL), lambda j: (j, 0, 0))),
        out_shape=(jax.ShapeDtypeStruct((NJ, CL, CL), BF16),
                   jax.ShapeDtypeStruct((NJ, CL, SW), BF16),
                   jax.ShapeDtypeStruct((NJ, SW, CL), BF16)),
        compiler_params=_cparams(("parallel",)),
        name="s5w",
    )(b_c, a_q, a_p, c16, c64, c_m)


def _s5_kernel(p_ref, pc_ref, wm_ref, wq_ref, wp_ref, tr_ref, cw_ref, y_ref, v_ref):
    nq = NJ
    half = GPT * S5_STATE

    def chunk_rows(ref, r0, nrows):
        return jnp.concatenate([ref[t, pl.ds(r0, nrows), :] for t in range(CH)], axis=-1)

    def fill(k, c):
        r0 = pl.multiple_of(k * SEG, SEG)
        v = jnp.dot(chunk_rows(p_ref, r0, SEG), wq_ref[0], preferred_element_type=F32)
        o0 = pl.multiple_of(k * PITCH, SUBLANES)
        for s in range(4 * nq):
            v_ref[s, pl.ds(o0, SEG), :] = v[:, s * LANES:(s + 1) * LANES]
        return c
    lax.fori_loop(0, NSEG, fill, 0)

    vc = jnp.dot(chunk_rows(pc_ref, 0, N_CHUNK_CTX), wq_ref[0], preferred_element_type=F32)
    vfr, vfi, vbr, vbi = (vc[:, i * half:(i + 1) * half] for i in range(4))
    wfr, wfi, wbr, wbi = (cw_ref[0, i] for i in range(4))
    s0_fr = jnp.sum(wfr * vfr - wfi * vfi, axis=0, keepdims=True)
    s0_fi = jnp.sum(wfr * vfi + wfi * vfr, axis=0, keepdims=True)
    s0_br = jnp.sum(wbr * vbr - wbi * vbi, axis=0, keepdims=True)
    s0_bi = jnp.sum(wbr * vbi + wbi * vbr, axis=0, keepdims=True)

    tr = tr_ref[0]
    afr, afi, abr, abi = (jnp.broadcast_to(tr[i:i + 1], (NSEG, half)) for i in range(4))
    gfr, gfi, gbr, gbi = (tr[i:i + 1] for i in range(4, 8))

    def load_part(part, i):
        return jnp.concatenate(
            [v_ref[part * nq + q, pl.ds(i, NSEG, stride=PITCH), :] for q in range(nq)], axis=-1)

    def store_part(part, i, val):
        for q in range(nq):
            v_ref[part * nq + q, pl.ds(i, NSEG, stride=PITCH), :] = val[:, q * LANES:(q + 1) * LANES]

    def step(i, carry, write):
        fr, fi, br, bi = carry
        ib = SEG - 1 - i
        ufr, ufi = load_part(0, i), load_part(1, i)
        ubr, ubi = load_part(2, ib), load_part(3, ib)
        if write:
            store_part(0, i, fr)
            store_part(1, i, fi)
            store_part(2, ib, br)
            store_part(3, ib, bi)
        return (afr * fr - afi * fi + ufr, afr * fi + afi * fr + ufi,
                abr * br - abi * bi + ubr, abr * bi + abi * br + ubi)

    zero = jnp.zeros((NSEG, half), F32)
    ffr, ffi, fbr, fbi = lax.fori_loop(0, SEG, functools.partial(step, write=False),
                                       (zero, zero, zero, zero))

    rows_fr, rows_fi = [s0_fr], [s0_fi]
    for k in range(1, NSEG):
        pr, pi = rows_fr[-1], rows_fi[-1]
        rows_fr.append(gfr * pr - gfi * pi + ffr[k - 1:k])
        rows_fi.append(gfr * pi + gfi * pr + ffi[k - 1:k])
    rows_br, rows_bi = [s0_br], [s0_bi]
    for k in range(NSEG - 2, -1, -1):
        pr, pi = rows_br[0], rows_bi[0]
        rows_br.insert(0, gbr * pr - gbi * pi + fbr[k + 1:k + 2])
        rows_bi.insert(0, gbr * pi + gbi * pr + fbi[k + 1:k + 2])
    init = tuple(jnp.concatenate(r, axis=0) for r in (rows_fr, rows_fi, rows_br, rows_bi))

    lax.fori_loop(0, SEG, functools.partial(step, write=True), init)

    def emit(k, c):
        r0 = pl.multiple_of(k * SEG, SEG)
        o0 = pl.multiple_of(k * PITCH, SUBLANES)
        b = chunk_rows(p_ref, r0, SEG)
        sin = jnp.concatenate([v_ref[s, pl.ds(o0, SEG), :] for s in range(4 * nq)], axis=-1)
        y = (jnp.dot(b, wm_ref[0], preferred_element_type=F32)
             + jnp.dot(sin.astype(BF16), wp_ref[0], preferred_element_type=F32))
        for t in range(CH):
            y_ref[t, pl.ds(r0, SEG), :] = y[:, t * LANES:(t + 1) * LANES].astype(y_ref.dtype)
        return c
    lax.fori_loop(0, NSEG, emit, 0)


def _s5(p_t, pc_t, w_m, w_q, w_p, trans, ctx_w):
    one = pl.Buffered(1)
    return pl.pallas_call(
        _s5_kernel,
        grid=(NJ,),
        in_specs=[pl.BlockSpec((CH, N_CHUNK, LANES), lambda j: (0, 0, j)),
                  pl.BlockSpec((CH, N_CHUNK_CTX, LANES), lambda j: (0, 0, j)),
                  pl.BlockSpec((1, CL, CL), lambda j: (j, 0, 0), pipeline_mode=one),
                  pl.BlockSpec((1, CL, SW), lambda j: (j, 0, 0), pipeline_mode=one),
                  pl.BlockSpec((1, SW, CL), lambda j: (j, 0, 0), pipeline_mode=one),
                  pl.BlockSpec((1, SUBLANES, GPT * S5_STATE), lambda j: (j, 0, 0)),
                  pl.BlockSpec((1, 4, N_CHUNK_CTX, GPT * S5_STATE), lambda j: (j, 0, 0, 0))],
        out_specs=pl.BlockSpec((CH, N_CHUNK, LANES), lambda j: (0, 0, j)),
        out_shape=jax.ShapeDtypeStruct((CH, N_CHUNK, S5_WIDTH), BF16),
        scratch_shapes=[pltpu.VMEM((4 * NJ, NSEG * PITCH, LANES), F32)],
        compiler_params=_cparams(("parallel",)),
        name="s5",
    )(p_t, pc_t, w_m, w_q, w_p, trans, ctx_w)


def _dft_tables():
    n = np.arange(FN)
    ang = 2.0 * np.pi * np.outer(n, n) / FN
    c, s = np.cos(ang), np.sin(ang)
    st1 = np.block([[c, s], [-s, c]])
    tw = 2.0 * np.pi * np.outer(n, n) / (FN * FN)
    wr, wi = np.cos(tw), -np.sin(tw)
    fr = c[None] * wr[:, None, :] + s[None] * wi[:, None, :]
    fi = c[None] * wi[:, None, :] - s[None] * wr[:, None, :]
    st2 = np.concatenate([fr, -fi], axis=-1)
    scale = 1.0 / math.sqrt(N_TOK * FFT_DIM)
    blk_c = np.kron(np.eye(FFT_GROUPS), c) * scale
    blk_s = np.kron(np.eye(FFT_GROUPS), s) * scale
    fc = np.concatenate([blk_c, -blk_s], axis=1)
    return (jnp.asarray(st1, F32).astype(BF16), jnp.asarray(st2, F32).astype(BF16), jnp.asarray(fc, F32))


FSL = FFT_WIDTH // LANES


def _rows_to_slabs(val, slab_ref, first):
    for s in range(FSL):
        slab_ref[first + s] = val[:, s * LANES:(s + 1) * LANES]


def _slabs_to_block(slab_ref, first):
    val = jnp.concatenate([slab_ref[first + s] for s in range(FSL)], axis=-1)
    return val.reshape(FN, FB, FFT_WIDTH)


def _fft1_kernel(xr_ref, xi_ref, f_ref, yr_ref, yi_ref, in_ref, out_ref):
    _rows_to_slabs(xr_ref[...].astype(F32).reshape(FN * FB, FFT_WIDTH), in_ref, 0)
    _rows_to_slabs(xi_ref[...].astype(F32).reshape(FN * FB, FFT_WIDTH), in_ref, FSL)
    for b in range(FB):
        def part(first):
            return jnp.concatenate([in_ref[first + s, pl.ds(b, FN, stride=FB), :] for s in range(FSL)],
                                   axis=-1)
        xs = jnp.concatenate([part(0), part(FSL)], axis=0).astype(BF16)
        y = jnp.dot(f_ref[...], xs, preferred_element_type=F32)
        for s in range(FSL):
            out_ref[s, pl.ds(b, FN, stride=FB), :] = y[:FN, s * LANES:(s + 1) * LANES]
            out_ref[FSL + s, pl.ds(b, FN, stride=FB), :] = y[FN:, s * LANES:(s + 1) * LANES]
    yr_ref[...] = _slabs_to_block(out_ref, 0).astype(BF16)
    yi_ref[...] = _slabs_to_block(out_ref, FSL).astype(BF16)


def _fft1(xr, xi, st1):
    spec = pl.BlockSpec((FN, FB, FFT_WIDTH), lambda i: (0, i, 0))
    slabs = pltpu.VMEM((2 * FSL, FN * FB, LANES), F32)
    return pl.pallas_call(
        _fft1_kernel,
        grid=(FN // FB,),
        in_specs=[spec, spec, pl.BlockSpec((2 * FN, 2 * FN), lambda i: (0, 0))],
        out_specs=(spec, spec),
        out_shape=(jax.ShapeDtypeStruct((FN, FN, FFT_WIDTH), BF16),) * 2,
        scratch_shapes=[slabs, slabs],
        compiler_params=_cparams(("parallel",)),
        name="fft1",
    )(xr.reshape(FN, FN, FFT_WIDTH), xi.reshape(FN, FN, FFT_WIDTH), st1)


def _fft2_kernel(yr_ref, yi_ref, f_ref, z_ref, out_ref):
    for b in range(FB):
        ys = jnp.concatenate([yr_ref[b * FN:(b + 1) * FN, :], yi_ref[b * FN:(b + 1) * FN, :]], axis=0)
        z = jnp.dot(f_ref[b], ys, preferred_element_type=F32)
        for s in range(FSL):
            out_ref[s, pl.ds(b, FN, stride=FB), :] = z[:, s * LANES:(s + 1) * LANES]
    z_ref[...] = _slabs_to_block(out_ref, 0).astype(BF16)


def _fft2(yr, yi, st2):
    rows = pl.BlockSpec((FB * FN, FFT_WIDTH), lambda i: (i, 0))
    z = pl.pallas_call(
        _fft2_kernel,
        grid=(FN // FB,),
        in_specs=[rows, rows, pl.BlockSpec((FB, FN, 2 * FN), lambda i: (i, 0, 0))],
        out_specs=pl.BlockSpec((FN, FB, FFT_WIDTH), lambda i: (0, i, 0)),
        out_shape=jax.ShapeDtypeStruct((FN, FN, FFT_WIDTH), BF16),
        scratch_shapes=[pltpu.VMEM((FSL, FN * FB, LANES), F32)],
        compiler_params=_cparams(("parallel",)),
        name="fft2",
    )(yr.reshape(N_TOK, FFT_WIDTH), yi.reshape(N_TOK, FFT_WIDTH), st2)
    return z.reshape(N_TOK, FFT_WIDTH)


def _gelu_tanh(x):
    return 0.5 * x * (1.0 + jnp.tanh(math.sqrt(2.0 / math.pi) * (x + 0.044715 * (x * x * x))))


def _mix_kernel(x_ref, er_ref, ec_ref, lg_ref, lb_ref, m1_ref, s1_ref, wg_ref, bg_ref,
                yt_ref, zr_ref, wglu_ref, bglu_ref, wbs_ref, wbf_ref, bbf_ref, wo_ref, bo_ref,
                g1_ref, l1g_ref, l1b_ref, m2_ref, s2_ref, wr_ref, br_ref, tri_ref, etri_ref,
                h1_ref, u2_ref, pos_ref, gate_ref, cnt_ref, scr_ref):
    x = x_ref[...] + _pos_code(er_ref, ec_ref, TM)
    h = _layer_norm(x, lg_ref[...], lb_ref[...])
    u = (h * m1_ref[...] + s1_ref[...]).astype(BF16)
    gates = _sigmoid(jnp.dot(u, wg_ref[...], preferred_element_type=F32) + bg_ref[...])

    for t in range(CH):
        for j in range(NJ):
            scr_ref[j, pl.ds(t, TM // CH, stride=CH), :] = (
                yt_ref[t, :, j * LANES:(j + 1) * LANES].astype(F32))
    ys = jnp.concatenate([scr_ref[j] for j in range(NJ)], axis=-1)
    z = jnp.dot(_gelu_tanh(ys).astype(BF16), wglu_ref[...], preferred_element_type=F32) + bglu_ref[...]
    glu = (z[:, :S5_WIDTH] * _sigmoid(z[:, S5_WIDTH:])).astype(BF16)
    y_s5 = jnp.dot(glu, wbs_ref[...], preferred_element_type=F32)
    y_fft = jnp.dot(zr_ref[...], wbf_ref[...], preferred_element_type=F32) + bbf_ref[...]
    mixed = (gates[:, :D] * y_s5 + gates[:, D:] * y_fft).astype(BF16)
    y = jnp.dot(mixed, wo_ref[...], preferred_element_type=F32) + bo_ref[...]
    h1 = _layer_norm(ALPHA * h + g1_ref[...] * y, l1g_ref[...], l1b_ref[...])
    h1_ref[...] = h1
    u2 = h1 * m2_ref[...] + s2_ref[...]
    u2_ref[...] = u2.astype(BF16)

    logits = lax.dot_general(wr_ref[...], u2, (((1,), (1,)), ((), ())),
                             preferred_element_type=F32, precision=HI) + br_ref[:, 0:1]
    eidx = lax.broadcasted_iota(jnp.int32, (N_EXPERTS, TM), 0)
    vals, hots = [], []
    cur = logits
    for _k in range(TOP_K):
        m = jnp.max(cur, axis=0, keepdims=True)
        sel = jnp.min(jnp.where(cur == m, eidx, N_EXPERTS), axis=0, keepdims=True)
        hot = eidx == sel
        cur = jnp.where(hot, -jnp.inf, cur)
        vals.append(m)
        hots.append(hot)
    exps = [jnp.exp(v - vals[0]) for v in vals]
    den = exps[0] + exps[1] + exps[2] + exps[3]
    gate4 = jnp.concatenate([e / den for e in exps], axis=0)

    hot_sum = (hots[0] | hots[1] | hots[2] | hots[3]).astype(F32)
    before = jnp.dot(hot_sum.astype(BF16), tri_ref[...], preferred_element_type=F32)
    cnt = jnp.broadcast_to(jnp.sum(hot_sum, axis=1, keepdims=True), (N_EXPERTS, LANES))
    cnt8 = jnp.floor((cnt + (SEG_ALIGN - 1)) * (1.0 / SEG_ALIGN)) * SEG_ALIGN
    seg0 = jnp.dot(etri_ref[...], cnt8.astype(BF16), preferred_element_type=F32)
    tot = seg0[:, 0:1] + before
    pos4 = jnp.concatenate(
        [jnp.sum(jnp.where(hk, tot, 0.0), axis=0, keepdims=True) for hk in hots], axis=0)
    pos_ref[...] = pos4.astype(jnp.int32)
    cnt_ref[0] = cnt

    gpad = jnp.concatenate([gate4, pos4, jnp.zeros((LANES - 2 * TOP_K, TM), F32)], axis=0)
    gate_ref[...] = gpad.T


def _mix(x, emb_r, emb_c, lg, lb, m1, s1, wg, bg, y_t, zr, wglu, bglu, wbs, wbf, bbf, wo, bo,
         g1, l1g, l1b, m2, s2, wr_t, br, tri, etri):
    vec = pl.BlockSpec((1, D), lambda i: (0, 0))

    def full(a):
        return pl.BlockSpec(a.shape, lambda i: (0,) * a.ndim)
    return pl.pallas_call(
        _mix_kernel,
        grid=(N_TOK // TM,),
        in_specs=[pl.BlockSpec((TM, D), lambda i: (i, 0)),
                  pl.BlockSpec((TM // GRID_W, D // 2), lambda i: (i, 0)),
                  pl.BlockSpec((GRID_W, D // 2), lambda i: (0, 0)),
                  vec, vec, vec, vec, full(wg), full(bg),
                  pl.BlockSpec((CH, TM // CH, S5_WIDTH), lambda i: (0, i, 0)),
                  pl.BlockSpec((TM, FFT_WIDTH), lambda i: (i, 0)),
                  full(wglu), full(bglu), full(wbs), full(wbf), full(bbf), full(wo), full(bo),
                  vec, vec, vec, vec, vec, full(wr_t), full(br), full(tri), full(etri)],
        out_specs=(pl.BlockSpec((TM, D), lambda i: (i, 0)),
                   pl.BlockSpec((TM, D), lambda i: (i, 0)),
                   pl.BlockSpec((TOP_K, TM), lambda i: (0, i)),
                   pl.BlockSpec((TM, LANES), lambda i: (i, 0)),
                   pl.BlockSpec((1, N_EXPERTS, LANES), lambda i: (i, 0, 0))),
        out_shape=(jax.ShapeDtypeStruct((N_TOK, D), F32),
                   jax.ShapeDtypeStruct((N_TOK, D), BF16),
                   jax.ShapeDtypeStruct((TOP_K, N_TOK), jnp.int32),
                   jax.ShapeDtypeStruct((N_TOK, LANES), F32),
                   jax.ShapeDtypeStruct((N_TILES, N_EXPERTS, LANES), F32)),
        scratch_shapes=[pltpu.VMEM((NJ, TM, LANES), F32)],
        compiler_params=_cparams(("parallel",)),
        name="mix",
    )(x, emb_r, emb_c, lg, lb, m1, s1, wg, bg, y_t, zr, wglu, bglu, wbs, wbf, bbf, wo, bo,
      g1, l1g, l1b, m2, s2, wr_t, br, tri, etri)


def _on_parity(i, fn):
    @pl.when(i % 2 == 0)
    def _():
        fn(0)

    @pl.when(i % 2 == 1)
    def _():
        fn(1)


def _dispatch_kernel(pend_ref, padded_ref, nchk_ref, dest_ref, pos_ref, u_ref, buf_ref,
                     sorted_ref, zero_ref, zsem, sems):
    i = pl.program_id(0)

    @pl.when(i == 0)
    def _():
        zero_ref[...] = jnp.zeros_like(zero_ref)
        n_used = pend_ref[N_EXPERTS - 1] // BM

        def clear_copy(start):
            return pltpu.make_async_copy(
                zero_ref, buf_ref.at[pl.ds(pl.multiple_of(start, BM), BM)], zsem)

        def each(fn):
            def expert(e, c):
                @pl.when(padded_ref[e] > 0)
                def _():
                    fn(clear_copy(pend_ref[e] - BM))
                return c
            lax.fori_loop(0, N_EXPERTS, expert, 0)

            def tail(b, c):
                fn(clear_copy(b * BM))
                return c
            lax.fori_loop(n_used, N_BLOCKS, tail, 0)
        each(lambda cp: cp.start())
        each(lambda cp: cp.wait())

    def drain(slot, tile):
        n = pl.multiple_of(nchk_ref[tile] * SEG_ALIGN, SEG_ALIGN)
        pltpu.make_async_copy(sorted_ref.at[slot, pl.ds(0, n)], buf_ref.at[pl.ds(0, n)],
                              sems.at[slot]).wait()

    def run(slot):
        pos = pos_ref[...]
        u = u_ref[...]
        for rb in range(CAP // CAP_BLOCK):
            rows = lax.broadcasted_iota(jnp.int32, (CAP_BLOCK, TM), 0) + rb * CAP_BLOCK
            hit = rows == pos[0:1]
            for k in range(1, TOP_K):
                hit = hit | (rows == pos[k:k + 1])
            onehot = jnp.where(hit, 1.0, 0.0).astype(BF16)
            sorted_ref[slot, rb * CAP_BLOCK:(rb + 1) * CAP_BLOCK, :] = jnp.dot(
                onehot, u, preferred_element_type=F32)

        def issue(j, c):
            src = pl.multiple_of(j * SEG_ALIGN, SEG_ALIGN)
            dst = pl.multiple_of(dest_ref[0, 0, j], SEG_ALIGN)
            pltpu.make_async_copy(sorted_ref.at[slot, pl.ds(src, SEG_ALIGN)],
                                  buf_ref.at[pl.ds(dst, SEG_ALIGN)], sems.at[slot]).start()
            return c
        lax.fori_loop(0, nchk_ref[i], issue, 0)

        @pl.when(i > 0)
        def _():
            drain(1 - slot, i - 1)

        @pl.when(i == N_TILES - 1)
        def _():
            drain(slot, i)
    _on_parity(i, run)


def _dispatch(pad_ends, padded, nchk, chunk_dest, pos_t, u2):
    return pl.pallas_call(
        _dispatch_kernel,
        grid_spec=pltpu.PrefetchScalarGridSpec(
            num_scalar_prefetch=3,
            grid=(N_TILES,),
            in_specs=[pl.BlockSpec((1, 1, NCHK), lambda i, a, b, c: (i, 0, 0),
                                   memory_space=pltpu.SMEM),
                      pl.BlockSpec((TOP_K, TM), lambda i, a, b, c: (0, i)),
                      pl.BlockSpec((TM, D), lambda i, a, b, c: (i, 0))],
            out_specs=pl.BlockSpec(memory_space=pl.ANY),
            scratch_shapes=[pltpu.VMEM((2, CAP, D), F32),
                            pltpu.VMEM((BM, D), F32),
                            pltpu.SemaphoreType.DMA(()),
                            pltpu.SemaphoreType.DMA((2,))]),
        out_shape=jax.ShapeDtypeStruct((ROWS, D), F32),
        compiler_params=_cparams(("arbitrary",)),
        name="dispatch",
    )(pad_ends, padded, nchk, chunk_dest, pos_t, u2)


def _ffn_kernel(be_ref, nu_ref, x_ref, wu_ref, bu_ref, wd_ref, bd_ref, y_ref, wub_ref, wdb_ref):
    i = pl.program_id(0)
    used = i < nu_ref[0]

    @pl.when(used)
    def _():
        prev = be_ref[jnp.maximum(i - 1, 0)]

        @pl.when((i == 0) | (be_ref[i] != prev))
        def _():
            wub_ref[...] = wu_ref[0].astype(BF16)
            wdb_ref[...] = wd_ref[0].astype(BF16)

        h = jnp.dot(x_ref[...].astype(BF16), wub_ref[...], preferred_element_type=F32) + bu_ref[0]
        h_glu = jnp.minimum(h[:, :D], SWIGLU_LIMIT)
        h_lin = jnp.clip(h[:, D:], -SWIGLU_LIMIT, SWIGLU_LIMIT)
        act = (h_glu * _sigmoid(SWIGLU_ALPHA * h_glu) * (h_lin + 1.0)).astype(BF16)
        y_ref[...] = jnp.dot(act, wdb_ref[...], preferred_element_type=F32) + bd_ref[0]

    @pl.when(jnp.logical_not(used))
    def _():
        y_ref[...] = jnp.zeros_like(y_ref)


def _ffn(block_expert, n_used, buf, w_up, b_up, w_down, b_down):
    def blk(i, be, nu):
        return jnp.minimum(i, nu[0] - 1)
    return pl.pallas_call(
        _ffn_kernel,
        grid_spec=pltpu.PrefetchScalarGridSpec(
            num_scalar_prefetch=2,
            grid=(N_BLOCKS,),
            in_specs=[pl.BlockSpec((BM, D), lambda i, be, nu: (blk(i, be, nu), 0)),
                      pl.BlockSpec((1, D, 2 * D), lambda i, be, nu: (be[blk(i, be, nu)], 0, 0)),
                      pl.BlockSpec((1, 1, 2 * D), lambda i, be, nu: (be[blk(i, be, nu)], 0, 0)),
                      pl.BlockSpec((1, D, D), lambda i, be, nu: (be[blk(i, be, nu)], 0, 0)),
                      pl.BlockSpec((1, 1, D), lambda i, be, nu: (be[blk(i, be, nu)], 0, 0))],
            out_specs=pl.BlockSpec((BM, D), lambda i, be, nu: (i, 0)),
            scratch_shapes=[pltpu.VMEM((D, 2 * D), BF16),
                            pltpu.VMEM((D, D), BF16)]),
        out_shape=jax.ShapeDtypeStruct((ROWS, D), F32),
        compiler_params=_cparams(("arbitrary",)),
        name="ffn",
    )(block_expert, n_used, buf, w_up, b_up, w_down, b_down)


def _combine_kernel(nchk_ref, dest_ref, dnext_ref, y_ref, h1_ref, gate_ref, g2_ref, lg_ref, lb_ref,
                    o_ref, sorted_ref, sems):
    i = pl.program_id(0)

    def fetch(slot, tile, table_ref):
        def issue(j, c):
            src = pl.multiple_of(table_ref[0, 0, j], SEG_ALIGN)
            dst = pl.multiple_of(j * SEG_ALIGN, SEG_ALIGN)
            pltpu.make_async_copy(y_ref.at[pl.ds(src, SEG_ALIGN)],
                                  sorted_ref.at[slot, pl.ds(dst, SEG_ALIGN)], sems.at[slot]).start()
            return c
        lax.fori_loop(0, nchk_ref[tile], issue, 0)

    def drain(slot, tile):
        n = pl.multiple_of(nchk_ref[tile] * SEG_ALIGN, SEG_ALIGN)
        pltpu.make_async_copy(y_ref.at[pl.ds(0, n)], sorted_ref.at[slot, pl.ds(0, n)],
                              sems.at[slot]).wait()

    @pl.when(i == 0)
    def _():
        sorted_ref[...] = jnp.zeros_like(sorted_ref)
        fetch(0, 0, dest_ref)

    def run(slot):
        @pl.when(i + 1 < N_TILES)
        def _():
            fetch(1 - slot, i + 1, dnext_ref)
        drain(slot, i)

        gp = gate_ref[...]
        m = jnp.zeros((TM, D), F32)
        for cb in range(CAP // CAP_BLOCK):
            cols = (lax.broadcasted_iota(jnp.int32, (TM, CAP_BLOCK), 1) + cb * CAP_BLOCK).astype(F32)
            g = jnp.where(cols == gp[:, TOP_K:TOP_K + 1], gp[:, 0:1], 0.0)
            for k in range(1, TOP_K):
                g = g + jnp.where(cols == gp[:, TOP_K + k:TOP_K + k + 1], gp[:, k:k + 1], 0.0)
            rows = sorted_ref[slot, cb * CAP_BLOCK:(cb + 1) * CAP_BLOCK, :].astype(BF16)
            m = m + jnp.dot(g.astype(BF16), rows, preferred_element_type=F32)
        o_ref[...] = _layer_norm(ALPHA * h1_ref[...] + g2_ref[...] * m, lg_ref[...], lb_ref[...])
    _on_parity(i, run)


def _combine(nchk, chunk_dest, y_buf, h1, gate_tok, g2, lg, lb):
    vec = pl.BlockSpec((1, D), lambda i, n: (0, 0))
    return pl.pallas_call(
        _combine_kernel,
        grid_spec=pltpu.PrefetchScalarGridSpec(
            num_scalar_prefetch=1,
            grid=(N_TILES,),
            in_specs=[pl.BlockSpec((1, 1, NCHK), lambda i, n: (i, 0, 0), memory_space=pltpu.SMEM),
                      pl.BlockSpec((1, 1, NCHK), lambda i, n: (jnp.minimum(i + 1, N_TILES - 1), 0, 0),
                                   memory_space=pltpu.SMEM),
                      pl.BlockSpec(memory_space=pl.ANY),
                      pl.BlockSpec((TM, D), lambda i, n: (i, 0)),
                      pl.BlockSpec((TM, LANES), lambda i, n: (i, 0)),
                      vec, vec, vec],
            out_specs=pl.BlockSpec((TM, D), lambda i, n: (i, 0)),
            scratch_shapes=[pltpu.VMEM((2, CAP, D), F32),
                            pltpu.SemaphoreType.DMA((2,))]),
        out_shape=jax.ShapeDtypeStruct((N_TOK, D), F32),
        compiler_params=_cparams(("arbitrary",)),
        name="combine",
    )(nchk, chunk_dest, chunk_dest, y_buf, h1, gate_tok, g2, lg, lb)


def _sincos_tables():
    q = D // 4
    omega = 1.0 / (10000.0 ** (jnp.arange(q, dtype=F32) / q))

    def emb(n):
        ang = jnp.arange(n, dtype=F32)[:, None] * omega[None, :]
        return jnp.concatenate([jnp.sin(ang), jnp.cos(ang)], axis=-1)
    return emb(N_TOK // GRID_W), emb(GRID_W)


def _tile_slots(a, tile):
    return jnp.transpose(a.reshape(TOP_K, N_TOK // tile, tile), (1, 0, 2)).reshape(N_TOK // tile, 1, TOP_K * tile)


def kernel(x, c, ctx, c_ctx, ln_in_g, ln_in_b, w_ada, b_ada, w_in, b_in, s5_lambda_re, s5_lambda_im, s5_log_dt, s5_b_re, s5_b_im, s5_c_re, s5_c_im, s5_d, w_glu, b_glu, w_br_s5, w_br_fft, b_br_fft, w_out, b_out, ln1_g, ln1_b, w_router, b_router, w_up, b_up, w_down, b_down, ln2_g, ln2_b):
    assert x.shape == (1, N_TOK, D) and ctx.shape == (1, N_CTX, D) and w_ada.shape[0] == 1
    row = lambda v: v.reshape(1, -1).astype(F32)

    cc = jnp.concatenate([c.reshape(1, D), c_ctx.reshape(1, D), jnp.zeros((SUBLANES - 2, D), F32)], axis=0)
    ada = _ada(cc, w_ada[0], row(b_ada[0]))
    sh1, sc1, g1, sh2, sc2, g2 = (ada[0:1, k * D:(k + 1) * D] for k in range(6))
    sh1c, sc1c = ada[1:2, 0:D], ada[1:2, D:2 * D]

    emb_r, emb_c = _sincos_tables()
    st1, st2, fc = _dft_tables()
    lg, lb = row(ln_in_g), row(ln_in_b)

    w_s5 = w_in[0][:, :S5_WIDTH]
    w_fft = w_in[0][:, S5_WIDTH:S5_WIDTH + FFT_WIDTH]
    w_g = w_in[0][:, S5_WIDTH + FFT_WIDTH:]
    b_s5 = row(b_in[0][:S5_WIDTH])
    b_fft8 = jnp.concatenate([row(b_in[0][S5_WIDTH:S5_WIDTH + FFT_WIDTH]),
                              jnp.zeros((SUBLANES - 1, FFT_WIDTH), F32)], axis=0)
    b_g = row(b_in[0][S5_WIDTH + FFT_WIDTH:])
    w_fc, b_fc = _fft_weights(w_fft, b_fft8, fc)
    wcat = jnp.concatenate([w_s5, w_fc], axis=1).astype(BF16)
    bcat = jnp.concatenate([b_s5, b_fc[0:1]], axis=1)

    x2 = x[0]
    p_t, xr, xi = _proj(x2, emb_r, emb_c, lg, lb, 1.0 + sc1, sh1, wcat, bcat)
    pc_t = _ctx_proj(ctx[0], lg, lb, 1.0 + sc1c, sh1c, w_s5.astype(BF16), b_s5)

    b_c, a_q, a_p, trans, ctx_w = _s5_tables(
        s5_lambda_re[0], s5_lambda_im[0], s5_log_dt[0], s5_b_re[0], s5_b_im[0],
        s5_c_re[0], s5_c_im[0], s5_d[0])
    w_m, w_q, w_p = _s5_expand(b_c, a_q, a_p)
    y_t = _s5(p_t, pc_t, w_m, w_q, w_p, trans, ctx_w)

    yr, yi = _fft1(xr, xi, st1)
    zr = _fft2(yr, yi, st2)

    tri = (jnp.arange(TM)[:, None] < jnp.arange(TM)[None, :]).astype(BF16)
    br = jnp.broadcast_to(b_router[0].reshape(N_EXPERTS, 1), (N_EXPERTS, LANES))
    etri = (jnp.arange(N_EXPERTS)[:, None] > jnp.arange(N_EXPERTS)[None, :]).astype(BF16)
    h1, u2, pos_t, gate_tok, counts = _mix(
        x2, emb_r, emb_c, lg, lb, 1.0 + sc1, sh1, w_g.astype(BF16), b_g, y_t, zr,
        w_glu[0].astype(BF16), row(b_glu[0]), w_br_s5[0].astype(BF16), w_br_fft[0].astype(BF16),
        row(b_br_fft[0]), w_out[0].astype(BF16), row(b_out[0]), g1, row(ln1_g[0]), row(ln1_b[0]),
        1.0 + sc2, sh2, jnp.transpose(w_router[0]), br, tri, etri)

    cnt = counts[:, :, 0].astype(jnp.int32)
    seg = (cnt + SEG_ALIGN - 1) // SEG_ALIGN * SEG_ALIGN
    seg_end = jnp.cumsum(seg, axis=1)
    seg_start = seg_end - seg
    padded = (jnp.sum(seg, axis=0) + BM - 1) // BM * BM
    pad_ends = jnp.cumsum(padded)
    seg_dest = (pad_ends - padded)[None, :] + jnp.cumsum(seg, axis=0) - seg
    chunk_row = jnp.arange(NCHK, dtype=jnp.int32) * SEG_ALIGN
    chunk_exp = jnp.minimum(jnp.sum(chunk_row[None, :, None] >= seg_end[:, None, :], axis=-1),
                            N_EXPERTS - 1)
    own = chunk_exp[:, :, None] == jnp.arange(N_EXPERTS, dtype=jnp.int32)[None, None, :]
    chunk_dest = (jnp.sum(jnp.where(own, (seg_dest - seg_start)[:, None, :], 0), axis=-1)
                  + chunk_row[None, :]).astype(jnp.int32).reshape(N_TILES, 1, NCHK)
    nchk = (seg_end[:, -1] // SEG_ALIGN).astype(jnp.int32)
    block_start = jnp.arange(N_BLOCKS, dtype=jnp.int32) * BM
    block_expert = jnp.minimum(jnp.sum(block_start[:, None] >= pad_ends[None, :], axis=1),
                               N_EXPERTS - 1).astype(jnp.int32)
    n_used = (pad_ends[-1:] // BM).astype(jnp.int32)

    buf = _dispatch(pad_ends.astype(jnp.int32), padded.astype(jnp.int32), nchk, chunk_dest, pos_t, u2)
    y_buf = _ffn(block_expert, n_used, buf, w_up[0], b_up[0].reshape(N_EXPERTS, 1, 2 * D),
                 w_down[0], b_down[0].reshape(N_EXPERTS, 1, D))
    out = _combine(nchk, chunk_dest, y_buf, h1, gate_tok, g2, row(ln2_g[0]), row(ln2_b[0]))
    return out.reshape(1, N_TOK, D)
```

```python
import functools
import math

import jax
import jax.numpy as jnp
import numpy as np
from jax import lax
from jax.experimental import pallas as pl
from jax.experimental.pallas import tpu as pltpu

F32 = jnp.float32
BF16 = jnp.bfloat16
HI = lax.Precision.HIGHEST

D = 1024
N_TOK = 16384
N_CTX = 256
GRID_W = 64
S5_GROUP = 16
S5_GROUPS = 32
S5_STATE = 64
S5_WIDTH = 512
FFT_GROUPS = 4
FFT_DIM = 128
FFT_WIDTH = 512
N_EXPERTS = 32
TOP_K = 4
LN_EPS = 1e-5
ALPHA = 2.0 ** 0.25
SWIGLU_ALPHA = 1.702
SWIGLU_LIMIT = 7.0

LANES = 128
SUBLANES = 8
VMEM_LIMIT = 56 * 1024 * 1024

CH = 8
N_CHUNK = N_TOK // CH
N_CHUNK_CTX = N_CTX // CH
NSEG = SUBLANES
SEG = N_CHUNK // NSEG
PITCH = SEG + 8
SCAN_UNROLL = 4
GPT = LANES // S5_GROUP
NJ = S5_WIDTH // LANES
CL = CH * LANES
SW = 4 * GPT * S5_STATE

FN = 128
FB = 16

TM = 512
N_TILES = N_TOK // TM
MIX_GROUPS = 1
BM = 512
N_SLOTS = N_TOK * TOP_K
SEG_ALIGN = SUBLANES
CAP_BLOCK = 256
CAP = -(-(TOP_K * TM + N_EXPERTS * (SEG_ALIGN - 1)) // CAP_BLOCK) * CAP_BLOCK
NCHK = CAP // SEG_ALIGN
N_BLOCKS = -(-(N_SLOTS + N_TILES * N_EXPERTS * (SEG_ALIGN - 1)) // BM) + N_EXPERTS
ROWS = N_BLOCKS * BM


def _cparams(sem):
    return pltpu.CompilerParams(dimension_semantics=sem, vmem_limit_bytes=VMEM_LIMIT)


def _layer_norm(x, g, b):
    mu = jnp.mean(x, axis=-1, keepdims=True)
    xc = x - mu
    var = jnp.mean(xc * xc, axis=-1, keepdims=True)
    return xc * lax.rsqrt(var + LN_EPS) * g + b


def _sigmoid(x):
    return 1.0 / (1.0 + jnp.exp(-x))


def _ada_kernel(c_ref, w_ref, b_ref, o_ref):
    c = c_ref[...]
    s = c * _sigmoid(c)
    o_ref[...] = jnp.dot(s, w_ref[...], preferred_element_type=F32, precision=HI) + b_ref[...]


def _ada(cc, w_ada, b_ada):
    nb = 4
    wb = 6 * D // nb
    return pl.pallas_call(
        _ada_kernel,
        grid=(nb,),
        in_specs=[pl.BlockSpec((SUBLANES, D), lambda i: (0, 0)),
                  pl.BlockSpec((D, wb), lambda i: (0, i)),
                  pl.BlockSpec((1, wb), lambda i: (0, i))],
        out_specs=pl.BlockSpec((SUBLANES, wb), lambda i: (0, i)),
        out_shape=jax.ShapeDtypeStruct((SUBLANES, 6 * D), F32),
        compiler_params=_cparams(("parallel",)),
        name="ada",
    )(cc, w_ada, b_ada)


def _fftw_kernel(w_ref, b_ref, f_ref, wo_ref, bo_ref):
    f = f_ref[...]
    wo_ref[...] = jnp.dot(w_ref[...], f, preferred_element_type=F32, precision=HI)
    bo_ref[...] = jnp.dot(b_ref[...], f, preferred_element_type=F32, precision=HI)


def _fft_weights(w_fft, b_fft8, fc):
    return pl.pallas_call(
        _fftw_kernel,
        out_shape=(jax.ShapeDtypeStruct((D, 2 * FFT_WIDTH), F32),
                   jax.ShapeDtypeStruct((SUBLANES, 2 * FFT_WIDTH), F32)),
        compiler_params=pltpu.CompilerParams(vmem_limit_bytes=VMEM_LIMIT),
        name="fftw",
    )(w_fft, b_fft8, fc)


def _pos_code(er_ref, ec_ref, tm):
    nr = tm // GRID_W
    er = er_ref[...]
    row = jnp.broadcast_to(er[:, None, :], (nr, GRID_W, D // 2)).reshape(tm, D // 2)
    col = jnp.concatenate([ec_ref[...]] * nr, axis=0)
    return jnp.concatenate([row, col], axis=-1)


def _to_chunk_major(val, scr_ref, out_ref, tm):
    for j in range(NJ):
        scr_ref[j] = val[:, j * LANES:(j + 1) * LANES]
    for t in range(CH):
        for j in range(NJ):
            piece = scr_ref[j, pl.ds(t, tm // CH, stride=CH), :]
            out_ref[t, :, j * LANES:(j + 1) * LANES] = piece.astype(out_ref.dtype)


def _proj_kernel(x_ref, er_ref, ec_ref, lg_ref, lb_ref, m_ref, s_ref, w_ref, b_ref,
                 p_ref, xr_ref, xi_ref, scr_ref):
    x = x_ref[...] + _pos_code(er_ref, ec_ref, TM)
    h = _layer_norm(x, lg_ref[...], lb_ref[...])
    u = (h * m_ref[...] + s_ref[...]).astype(BF16)
    p = jnp.dot(u, w_ref[...], preferred_element_type=F32) + b_ref[...]
    _to_chunk_major(p[:, :S5_WIDTH], scr_ref, p_ref, TM)
    xr_ref[...] = p[:, S5_WIDTH:S5_WIDTH + FFT_WIDTH].astype(BF16)
    xi_ref[...] = p[:, S5_WIDTH + FFT_WIDTH:].astype(BF16)


def _proj(x, emb_r, emb_c, lg, lb, m1, s1, wcat, bcat):
    nw = wcat.shape[1]
    vec = pl.BlockSpec((1, D), lambda i: (0, 0))
    return pl.pallas_call(
        _proj_kernel,
        grid=(N_TOK // TM,),
        in_specs=[pl.BlockSpec((TM, D), lambda i: (i, 0)),
                  pl.BlockSpec((TM // GRID_W, D // 2), lambda i: (i, 0)),
                  pl.BlockSpec((GRID_W, D // 2), lambda i: (0, 0)),
                  vec, vec, vec, vec,
                  pl.BlockSpec((D, nw), lambda i: (0, 0)),
                  pl.BlockSpec((1, nw), lambda i: (0, 0))],
        out_specs=(pl.BlockSpec((CH, TM // CH, S5_WIDTH), lambda i: (0, i, 0)),
                   pl.BlockSpec((TM, FFT_WIDTH), lambda i: (i, 0)),
                   pl.BlockSpec((TM, FFT_WIDTH), lambda i: (i, 0))),
        out_shape=(jax.ShapeDtypeStruct((CH, N_CHUNK, S5_WIDTH), BF16),
                   jax.ShapeDtypeStruct((N_TOK, FFT_WIDTH), BF16),
                   jax.ShapeDtypeStruct((N_TOK, FFT_WIDTH), BF16)),
        scratch_shapes=[pltpu.VMEM((NJ, TM, LANES), F32)],
        compiler_params=_cparams(("parallel",)),
        name="proj",
    )(x, emb_r, emb_c, lg, lb, m1, s1, wcat, bcat)


def _ctx_proj_kernel(x_ref, lg_ref, lb_ref, m_ref, s_ref, w_ref, b_ref, p_ref, scr_ref):
    h = _layer_norm(x_ref[...], lg_ref[...], lb_ref[...])
    u = (h * m_ref[...] + s_ref[...]).astype(BF16)
    p = jnp.dot(u, w_ref[...], preferred_element_type=F32) + b_ref[...]
    _to_chunk_major(p, scr_ref, p_ref, N_CTX)


def _ctx_proj(ctx, lg, lb, m1, s1, w_s5, b_s5):
    return pl.pallas_call(
        _ctx_proj_kernel,
        out_shape=jax.ShapeDtypeStruct((CH, N_CHUNK_CTX, S5_WIDTH), BF16),
        scratch_shapes=[pltpu.VMEM((NJ, N_CTX, LANES), F32)],
        compiler_params=pltpu.CompilerParams(vmem_limit_bytes=VMEM_LIMIT),
        name="ctxproj",
    )(ctx, lg, lb, m1, s1, w_s5, b_s5)


def _s5_tables(lam_re, lam_im, log_dt, b_re, b_im, c_re, c_im, d_skip):
    dt = jnp.exp(log_dt)[..., None]
    zr = lam_re * dt
    zi = lam_im * dt

    def apow(m):
        m = jnp.asarray(m, F32)
        mag = jnp.exp(zr[..., None] * m)
        return mag * jnp.cos(zi[..., None] * m), mag * jnp.sin(zi[..., None] * m)

    a_re, a_im = apow(jnp.ones((1,), F32))
    a_re, a_im = a_re[..., 0], a_im[..., 0]
    den = lam_re * lam_re + lam_im * lam_im
    num_re = a_re - 1.0
    k_re = (num_re * lam_re + a_im * lam_im) / den
    k_im = (a_im * lam_re - num_re * lam_im) / den
    bb_re = k_re[..., None] * b_re - k_im[..., None] * b_im
    bb_im = k_re[..., None] * b_im + k_im[..., None] * b_re

    ks = jnp.arange(CH + 1, dtype=F32)
    pw_re, pw_im = apow(ks)

    def tap(c, pw, bb):
        return jnp.einsum('dgvp,dgpk,dgph->kdghv', c, pw[..., :CH], bb, precision=HI)
    taps = (tap(c_re, pw_re, bb_re) - tap(c_re, pw_im, bb_im)
            - tap(c_im, pw_re, bb_im) - tap(c_im, pw_im, bb_re))
    skip = d_skip.reshape(S5_GROUPS, S5_GROUP, 1) * jnp.eye(S5_GROUP, dtype=F32)[None]
    taps = taps.at[0, 0].add(skip)
    b_c = jnp.transpose(taps.reshape(CH, 2, NJ, LANES, S5_GROUP), (2, 0, 1, 3, 4))
    b_c = b_c.reshape(NJ, 2 * CH, LANES, S5_GROUP)

    ef = (CH - 1) - jnp.arange(CH)
    eb = jnp.arange(CH)

    def q_part(d, e):
        pr = jnp.transpose(pw_re[d][..., e], (0, 2, 1))[:, :, None, :]
        pi = jnp.transpose(pw_im[d][..., e], (0, 2, 1))[:, :, None, :]
        br = jnp.transpose(bb_re[d], (0, 2, 1))[:, None, :, :]
        bi = jnp.transpose(bb_im[d], (0, 2, 1))[:, None, :, :]
        return pr * br - pi * bi, pr * bi + pi * br

    def q_rows(v):
        v = v.reshape(NJ, GPT, CH, S5_GROUP, S5_STATE)
        return jnp.transpose(v, (0, 2, 1, 3, 4)).reshape(NJ, CL, S5_STATE)
    a_q = jnp.stack([q_rows(v) for v in q_part(0, ef) + q_part(1, eb)], axis=0)

    of = jnp.arange(CH) + 1
    ob = CH - jnp.arange(CH)

    def p_part(d, e):
        pr = pw_re[d][..., e][:, :, :, None]
        pi = pw_im[d][..., e][:, :, :, None]
        cr = jnp.transpose(c_re[d], (0, 2, 1))[:, :, None, :]
        ci = jnp.transpose(c_im[d], (0, 2, 1))[:, :, None, :]
        return cr * pr - ci * pi, -(cr * pi + ci * pr)
    a_p = jnp.stack([v.reshape(NJ, GPT * S5_STATE, CH * S5_GROUP)
                     for v in p_part(0, of) + p_part(1, ob)], axis=0)

    def lanes(v):
        return jnp.transpose(v.reshape(2, NJ, GPT * S5_STATE), (1, 0, 2))
    c_r, c_i = apow(jnp.full((1,), float(CH), F32))
    s_r, s_i = apow(jnp.full((1,), float(CH * SEG), F32))
    cr, ci, sr, si = (lanes(v[..., 0]) for v in (c_r, c_i, s_r, s_i))
    trans = jnp.stack([cr[:, 0], ci[:, 0], cr[:, 1], ci[:, 1],
                       sr[:, 0], si[:, 0], sr[:, 1], si[:, 1]], axis=1)

    cidx = jnp.arange(N_CHUNK_CTX, dtype=F32)
    wf_r, wf_i = apow(CH * (N_CHUNK_CTX - 1 - cidx))
    wb_r, wb_i = apow(CH * cidx)

    def ctx_lanes(v, d):
        return jnp.transpose(v[d].reshape(NJ, GPT * S5_STATE, N_CHUNK_CTX), (0, 2, 1))
    ctx_w = jnp.stack([ctx_lanes(wf_r, 0), ctx_lanes(wf_i, 0),
                       ctx_lanes(wb_r, 1), ctx_lanes(wb_i, 1)], axis=1)
    return b_c, a_q, a_p, trans, ctx_w


def _s5w_kernel(bc_ref, aq_ref, ap_ref, c16_ref, c64_ref, cm_ref, wm_ref, wq_ref, wp_ref):
    def expand(a, c, row_shift, col_shift):
        w = jnp.dot(a, c, preferred_element_type=F32)
        rg = (lax.broadcasted_iota(jnp.int32, (w.shape[0], 1), 0) >> row_shift) & (GPT - 1)
        cg = (lax.broadcasted_iota(jnp.int32, (1, w.shape[1]), 1) >> col_shift) & (GPT - 1)
        return jnp.where(rg == cg, w, 0.0)

    blk = [expand(bc_ref[0, kd], c16_ref[...], 4, 4) for kd in range(2 * CH)]
    for t in range(CH):
        for u in range(CH):
            b = blk[2 * (u - t)] if u > t else blk[2 * (t - u) + 1] if u < t else blk[0] + blk[1]
            wm_ref[0, t * LANES:(t + 1) * LANES, u * LANES:(u + 1) * LANES] = b.astype(BF16)
    half = GPT * S5_STATE
    for s in range(4):
        wq_ref[0, :, s * half:(s + 1) * half] = expand(aq_ref[s, 0], c64_ref[...], 4, 6).astype(BF16)
        wp_ref[0, s * half:(s + 1) * half, :] = expand(ap_ref[s, 0], cm_ref[...], 6, 4).astype(BF16)


def _s5_expand(b_c, a_q, a_p):
    rep = np.ones((1, GPT))
    c16 = jnp.asarray(np.kron(rep, np.eye(S5_GROUP)), F32).astype(BF16)
    c64 = jnp.asarray(np.kron(rep, np.eye(S5_STATE)), F32).astype(BF16)
    c_m = jnp.asarray(np.kron(np.eye(CH), np.kron(rep, np.eye(S5_GROUP))), F32).astype(BF16)
    b_c, a_q, a_p = b_c.astype(BF16), a_q.astype(BF16), a_p.astype(BF16)
    half = GPT * S5_STATE
    return pl.pallas_call(
        _s5w_kernel,
        grid=(NJ,),
        in_specs=[pl.BlockSpec((1, 2 * CH, LANES, S5_GROUP), lambda j: (j, 0, 0, 0)),
                  pl.BlockSpec((4, 1, CL, S5_STATE), lambda j: (0, j, 0, 0)),
                  pl.BlockSpec((4, 1, half, CH * S5_GROUP), lambda j: (0, j, 0, 0)),
                  pl.BlockSpec(c16.shape, lambda j: (0, 0)),
                  pl.BlockSpec(c64.shape, lambda j: (0, 0)),
                  pl.BlockSpec(c_m.shape, lambda j: (0, 0))],
        out_specs=(pl.BlockSpec((1, CL, CL), lambda j: (j, 0, 0)),
                   pl.BlockSpec((1, CL, SW), lambda j: (j, 0, 0)),
                   pl.BlockSpec((1, SW, CL), lambda j: (j, 0, 0))),
        out_shape=(jax.ShapeDtypeStruct((NJ, CL, CL), BF16),
                   jax.ShapeDtypeStruct((NJ, CL, SW), BF16),
                   jax.ShapeDtypeStruct((NJ, SW, CL), BF16)),
        compiler_params=_cparams(("parallel",)),
        name="s5w",
    )(b_c, a_q, a_p, c16, c64, c_m)


def _s5_kernel(p_ref, pc_ref, wm_ref, wq_ref, wp_ref, tr_ref, cw_ref, y_ref, v_ref):
    nq = NJ
    half = GPT * S5_STATE

    def chunk_rows(ref, r0, nrows):
        return jnp.concatenate([ref[t, pl.ds(r0, nrows), :] for t in range(CH)], axis=-1)

    def fill(k, c):
        r0 = pl.multiple_of(k * SEG, SEG)
        v = jnp.dot(chunk_rows(p_ref, r0, SEG), wq_ref[0], preferred_element_type=F32)
        o0 = pl.multiple_of(k * PITCH, SUBLANES)
        for s in range(4 * nq):
            v_ref[s, pl.ds(o0, SEG), :] = v[:, s * LANES:(s + 1) * LANES]
        return c
    lax.fori_loop(0, NSEG, fill, 0)

    vc = jnp.dot(chunk_rows(pc_ref, 0, N_CHUNK_CTX), wq_ref[0], preferred_element_type=F32)
    vfr, vfi, vbr, vbi = (vc[:, i * half:(i + 1) * half] for i in range(4))
    wfr, wfi, wbr, wbi = (cw_ref[0, i] for i in range(4))
    s0_fr = jnp.sum(wfr * vfr - wfi * vfi, axis=0, keepdims=True)
    s0_fi = jnp.sum(wfr * vfi + wfi * vfr, axis=0, keepdims=True)
    s0_br = jnp.sum(wbr * vbr - wbi * vbi, axis=0, keepdims=True)
    s0_bi = jnp.sum(wbr * vbi + wbi * vbr, axis=0, keepdims=True)

    tr = tr_ref[0]
    afr, afi, abr, abi = (jnp.broadcast_to(tr[i:i + 1], (NSEG, half)) for i in range(4))
    gfr, gfi, gbr, gbi = (tr[i:i + 1] for i in range(4, 8))

    def load_part(part, i):
        return jnp.concatenate(
            [v_ref[part * nq + q, pl.ds(i, NSEG, stride=PITCH), :] for q in range(nq)], axis=-1)

    def store_part(part, i, val):
        for q in range(nq):
            v_ref[part * nq + q, pl.ds(i, NSEG, stride=PITCH), :] = val[:, q * LANES:(q + 1) * LANES]

    def step(i, carry, write):
        fr, fi, br, bi = carry
        ib = SEG - 1 - i
        ufr, ufi = load_part(0, i), load_part(1, i)
        ubr, ubi = load_part(2, ib), load_part(3, ib)
        if write:
            store_part(0, i, fr)
            store_part(1, i, fi)
            store_part(2, ib, br)
            store_part(3, ib, bi)
        return (afr * fr - afi * fi + ufr, afr * fi + afi * fr + ufi,
                abr * br - abi * bi + ubr, abr * bi + abi * br + ubi)

    zero = jnp.zeros((NSEG, half), F32)
    ffr, ffi, fbr, fbi = lax.fori_loop(0, SEG, functools.partial(step, write=False),
                                       (zero, zero, zero, zero), unroll=SCAN_UNROLL)

    rows_fr, rows_fi = [s0_fr], [s0_fi]
    for k in range(1, NSEG):
        pr, pi = rows_fr[-1], rows_fi[-1]
        rows_fr.append(gfr * pr - gfi * pi + ffr[k - 1:k])
        rows_fi.append(gfr * pi + gfi * pr + ffi[k - 1:k])
    rows_br, rows_bi = [s0_br], [s0_bi]
    for k in range(NSEG - 2, -1, -1):
        pr, pi = rows_br[0], rows_bi[0]
        rows_br.insert(0, gbr * pr - gbi * pi + fbr[k + 1:k + 2])
        rows_bi.insert(0, gbr * pi + gbi * pr + fbi[k + 1:k + 2])
    init = tuple(jnp.concatenate(r, axis=0) for r in (rows_fr, rows_fi, rows_br, rows_bi))

    lax.fori_loop(0, SEG, functools.partial(step, write=True), init, unroll=SCAN_UNROLL)

    def emit(k, c):
        r0 = pl.multiple_of(k * SEG, SEG)
        o0 = pl.multiple_of(k * PITCH, SUBLANES)
        b = chunk_rows(p_ref, r0, SEG)
        sin = jnp.concatenate([v_ref[s, pl.ds(o0, SEG), :] for s in range(4 * nq)], axis=-1)
        y = (jnp.dot(b, wm_ref[0], preferred_element_type=F32)
             + jnp.dot(sin.astype(BF16), wp_ref[0], preferred_element_type=F32))
        for t in range(CH):
            y_ref[t, pl.ds(r0, SEG), :] = y[:, t * LANES:(t + 1) * LANES].astype(y_ref.dtype)
        return c
    lax.fori_loop(0, NSEG, emit, 0)


def _s5(p_t, pc_t, w_m, w_q, w_p, trans, ctx_w):
    one = pl.Buffered(1)
    return pl.pallas_call(
        _s5_kernel,
        grid=(NJ,),
        in_specs=[pl.BlockSpec((CH, N_CHUNK, LANES), lambda j: (0, 0, j)),
                  pl.BlockSpec((CH, N_CHUNK_CTX, LANES), lambda j: (0, 0, j)),
                  pl.BlockSpec((1, CL, CL), lambda j: (j, 0, 0), pipeline_mode=one),
                  pl.BlockSpec((1, CL, SW), lambda j: (j, 0, 0), pipeline_mode=one),
                  pl.BlockSpec((1, SW, CL), lambda j: (j, 0, 0), pipeline_mode=one),
                  pl.BlockSpec((1, SUBLANES, GPT * S5_STATE), lambda j: (j, 0, 0)),
                  pl.BlockSpec((1, 4, N_CHUNK_CTX, GPT * S5_STATE), lambda j: (j, 0, 0, 0))],
        out_specs=pl.BlockSpec((CH, N_CHUNK, LANES), lambda j: (0, 0, j)),
        out_shape=jax.ShapeDtypeStruct((CH, N_CHUNK, S5_WIDTH), BF16),
        scratch_shapes=[pltpu.VMEM((4 * NJ, NSEG * PITCH, LANES), F32)],
        compiler_params=_cparams(("parallel",)),
        name="s5",
    )(p_t, pc_t, w_m, w_q, w_p, trans, ctx_w)


def _dft_tables():
    n = np.arange(FN)
    ang = 2.0 * np.pi * np.outer(n, n) / FN
    c, s = np.cos(ang), np.sin(ang)
    st1 = np.block([[c, s], [-s, c]])
    tw = 2.0 * np.pi * np.outer(n, n) / (FN * FN)
    wr, wi = np.cos(tw), -np.sin(tw)
    fr = c[None] * wr[:, None, :] + s[None] * wi[:, None, :]
    fi = c[None] * wi[:, None, :] - s[None] * wr[:, None, :]
    st2 = np.concatenate([fr, -fi], axis=-1)
    scale = 1.0 / math.sqrt(N_TOK * FFT_DIM)
    blk_c = np.kron(np.eye(FFT_GROUPS), c) * scale
    blk_s = np.kron(np.eye(FFT_GROUPS), s) * scale
    fc = np.concatenate([blk_c, -blk_s], axis=1)
    return (jnp.asarray(st1, F32).astype(BF16), jnp.asarray(st2, F32).astype(BF16), jnp.asarray(fc, F32))


FSL = FFT_WIDTH // LANES


def _rows_to_slabs(val, slab_ref, first):
    for s in range(FSL):
        slab_ref[first + s] = val[:, s * LANES:(s + 1) * LANES]


def _slabs_to_block(slab_ref, first):
    val = jnp.concatenate([slab_ref[first + s] for s in range(FSL)], axis=-1)
    return val.reshape(FN, FB, FFT_WIDTH)


def _fft1_kernel(xr_ref, xi_ref, f_ref, yr_ref, yi_ref, in_ref, out_ref):
    _rows_to_slabs(xr_ref[...].astype(F32).reshape(FN * FB, FFT_WIDTH), in_ref, 0)
    _rows_to_slabs(xi_ref[...].astype(F32).reshape(FN * FB, FFT_WIDTH), in_ref, FSL)
    for b in range(FB):
        def part(first):
            return jnp.concatenate([in_ref[first + s, pl.ds(b, FN, stride=FB), :] for s in range(FSL)],
                                   axis=-1)
        xs = jnp.concatenate([part(0), part(FSL)], axis=0).astype(BF16)
        y = jnp.dot(f_ref[...], xs, preferred_element_type=F32)
        for s in range(FSL):
            out_ref[s, pl.ds(b, FN, stride=FB), :] = y[:FN, s * LANES:(s + 1) * LANES]
            out_ref[FSL + s, pl.ds(b, FN, stride=FB), :] = y[FN:, s * LANES:(s + 1) * LANES]
    yr_ref[...] = _slabs_to_block(out_ref, 0).astype(BF16)
    yi_ref[...] = _slabs_to_block(out_ref, FSL).astype(BF16)


def _fft1(xr, xi, st1):
    spec = pl.BlockSpec((FN, FB, FFT_WIDTH), lambda i: (0, i, 0))
    slabs = pltpu.VMEM((2 * FSL, FN * FB, LANES), F32)
    return pl.pallas_call(
        _fft1_kernel,
        grid=(FN // FB,),
        in_specs=[spec, spec, pl.BlockSpec((2 * FN, 2 * FN), lambda i: (0, 0))],
        out_specs=(spec, spec),
        out_shape=(jax.ShapeDtypeStruct((FN, FN, FFT_WIDTH), BF16),) * 2,
        scratch_shapes=[slabs, slabs],
        compiler_params=_cparams(("parallel",)),
        name="fft1",
    )(xr.reshape(FN, FN, FFT_WIDTH), xi.reshape(FN, FN, FFT_WIDTH), st1)


def _fft2_kernel(yr_ref, yi_ref, f_ref, z_ref, out_ref):
    for b in range(FB):
        ys = jnp.concatenate([yr_ref[b * FN:(b + 1) * FN, :], yi_ref[b * FN:(b + 1) * FN, :]], axis=0)
        z = jnp.dot(f_ref[b], ys, preferred_element_type=F32)
        for s in range(FSL):
            out_ref[s, pl.ds(b, FN, stride=FB), :] = z[:, s * LANES:(s + 1) * LANES]
    z_ref[...] = _slabs_to_block(out_ref, 0).astype(BF16)


def _fft2(yr, yi, st2):
    rows = pl.BlockSpec((FB * FN, FFT_WIDTH), lambda i: (i, 0))
    z = pl.pallas_call(
        _fft2_kernel,
        grid=(FN // FB,),
        in_specs=[rows, rows, pl.BlockSpec((FB, FN, 2 * FN), lambda i: (i, 0, 0))],
        out_specs=pl.BlockSpec((FN, FB, FFT_WIDTH), lambda i: (0, i, 0)),
        out_shape=jax.ShapeDtypeStruct((FN, FN, FFT_WIDTH), BF16),
        scratch_shapes=[pltpu.VMEM((FSL, FN * FB, LANES), F32)],
        compiler_params=_cparams(("parallel",)),
        name="fft2",
    )(yr.reshape(N_TOK, FFT_WIDTH), yi.reshape(N_TOK, FFT_WIDTH), st2)
    return z.reshape(N_TOK, FFT_WIDTH)


def _gelu_tanh(x):
    return 0.5 * x * (1.0 + jnp.tanh(math.sqrt(2.0 / math.pi) * (x + 0.044715 * (x * x * x))))


def _mix_kernel(x_ref, er_ref, ec_ref, lg_ref, lb_ref, m1_ref, s1_ref, wg_ref, bg_ref,
                yt_ref, zr_ref, wglu_ref, bglu_ref, wbs_ref, wbf_ref, bbf_ref, wo_ref, bo_ref,
                g1_ref, l1g_ref, l1b_ref, m2_ref, s2_ref, wr_ref, br_ref, tri_ref, etri_ref,
                h1_ref, u2_ref, pos_ref, gate_ref, cnt_ref, scr_ref):
    pos = _pos_code(er_ref, ec_ref, TM)

    def front(r0, nr):
        rows = slice(r0, r0 + nr)
        h = _layer_norm(x_ref[rows, :] + pos[rows, :], lg_ref[...], lb_ref[...])
        u = (h * m1_ref[...] + s1_ref[...]).astype(BF16)

        c0, nc = r0 // CH, nr // CH
        for t in range(CH):
            for j in range(NJ):
                scr_ref[j, pl.ds(r0 + t, nc, stride=CH), :] = (
                    yt_ref[t, c0:c0 + nc, j * LANES:(j + 1) * LANES].astype(F32))
        ys = jnp.concatenate([scr_ref[j, rows, :] for j in range(NJ)], axis=-1)
        z = jnp.dot(_gelu_tanh(ys).astype(BF16), wglu_ref[...], preferred_element_type=F32) + bglu_ref[...]
        glu = (z[:, :S5_WIDTH] * _sigmoid(z[:, S5_WIDTH:])).astype(BF16)
        gates = _sigmoid(jnp.dot(u, wg_ref[...], preferred_element_type=F32) + bg_ref[...])
        y_s5 = jnp.dot(glu, wbs_ref[...], preferred_element_type=F32)
        y_fft = jnp.dot(zr_ref[rows, :], wbf_ref[...], preferred_element_type=F32) + bbf_ref[...]
        mixed = (gates[:, :D] * y_s5 + gates[:, D:] * y_fft).astype(BF16)
        y = jnp.dot(mixed, wo_ref[...], preferred_element_type=F32) + bo_ref[...]
        h1 = _layer_norm(ALPHA * h + g1_ref[...] * y, l1g_ref[...], l1b_ref[...])
        h1_ref[rows, :] = h1
        u2 = h1 * m2_ref[...] + s2_ref[...]
        u2_ref[rows, :] = u2.astype(BF16)
        return lax.dot_general(wr_ref[...], u2, (((1,), (1,)), ((), ())),
                               preferred_element_type=F32, precision=HI)

    nr = TM // MIX_GROUPS
    logits = jnp.concatenate([front(g * nr, nr) for g in range(MIX_GROUPS)], axis=-1) + br_ref[:, 0:1]
    eidx = lax.broadcasted_iota(jnp.int32, (N_EXPERTS, TM), 0)
    vals, hots = [], []
    cur = logits
    for _k in range(TOP_K):
        m = jnp.max(cur, axis=0, keepdims=True)
        sel = jnp.min(jnp.where(cur == m, eidx, N_EXPERTS), axis=0, keepdims=True)
        hot = eidx == sel
        cur = jnp.where(hot, -jnp.inf, cur)
        vals.append(m)
        hots.append(hot)
    exps = [jnp.exp(v - vals[0]) for v in vals]
    den = exps[0] + exps[1] + exps[2] + exps[3]
    gate4 = jnp.concatenate([e / den for e in exps], axis=0)

    hot_sum = (hots[0] | hots[1] | hots[2] | hots[3]).astype(F32)
    before = jnp.dot(hot_sum.astype(BF16), tri_ref[...], preferred_element_type=F32)
    cnt = jnp.broadcast_to(jnp.sum(hot_sum, axis=1, keepdims=True), (N_EXPERTS, LANES))
    cnt8 = jnp.floor((cnt + (SEG_ALIGN - 1)) * (1.0 / SEG_ALIGN)) * SEG_ALIGN
    seg0 = jnp.dot(etri_ref[...], cnt8.astype(BF16), preferred_element_type=F32)
    tot = seg0[:, 0:1] + before
    pos4 = jnp.concatenate(
        [jnp.sum(jnp.where(hk, tot, 0.0), axis=0, keepdims=True) for hk in hots], axis=0)
    pos_ref[...] = pos4.astype(jnp.int32)
    cnt_ref[0] = cnt

    gpad = jnp.concatenate([gate4, pos4, jnp.zeros((LANES - 2 * TOP_K, TM), F32)], axis=0)
    gate_ref[...] = gpad.T


def _mix(x, emb_r, emb_c, lg, lb, m1, s1, wg, bg, y_t, zr, wglu, bglu, wbs, wbf, bbf, wo, bo,
         g1, l1g, l1b, m2, s2, wr_t, br, tri, etri):
    vec = pl.BlockSpec((1, D), lambda i: (0, 0))

    def full(a):
        return pl.BlockSpec(a.shape, lambda i: (0,) * a.ndim)
    return pl.pallas_call(
        _mix_kernel,
        grid=(N_TOK // TM,),
        in_specs=[pl.BlockSpec((TM, D), lambda i: (i, 0)),
                  pl.BlockSpec((TM // GRID_W, D // 2), lambda i: (i, 0)),
                  pl.BlockSpec((GRID_W, D // 2), lambda i: (0, 0)),
                  vec, vec, vec, vec, full(wg), full(bg),
                  pl.BlockSpec((CH, TM // CH, S5_WIDTH), lambda i: (0, i, 0)),
                  pl.BlockSpec((TM, FFT_WIDTH), lambda i: (i, 0)),
                  full(wglu), full(bglu), full(wbs), full(wbf), full(bbf), full(wo), full(bo),
                  vec, vec, vec, vec, vec, full(wr_t), full(br), full(tri), full(etri)],
        out_specs=(pl.BlockSpec((TM, D), lambda i: (i, 0)),
                   pl.BlockSpec((TM, D), lambda i: (i, 0)),
                   pl.BlockSpec((TOP_K, TM), lambda i: (0, i)),
                   pl.BlockSpec((TM, LANES), lambda i: (i, 0)),
                   pl.BlockSpec((1, N_EXPERTS, LANES), lambda i: (i, 0, 0))),
        out_shape=(jax.ShapeDtypeStruct((N_TOK, D), F32),
                   jax.ShapeDtypeStruct((N_TOK, D), BF16),
                   jax.ShapeDtypeStruct((TOP_K, N_TOK), jnp.int32),
                   jax.ShapeDtypeStruct((N_TOK, LANES), F32),
                   jax.ShapeDtypeStruct((N_TILES, N_EXPERTS, LANES), F32)),
        scratch_shapes=[pltpu.VMEM((NJ, TM, LANES), F32)],
        compiler_params=_cparams(("parallel",)),
        name="mix",
    )(x, emb_r, emb_c, lg, lb, m1, s1, wg, bg, y_t, zr, wglu, bglu, wbs, wbf, bbf, wo, bo,
      g1, l1g, l1b, m2, s2, wr_t, br, tri, etri)


def _on_parity(i, fn):
    @pl.when(i % 2 == 0)
    def _():
        fn(0)

    @pl.when(i % 2 == 1)
    def _():
        fn(1)


def _dispatch_kernel(pend_ref, padded_ref, nchk_ref, dest_ref, pos_ref, u_ref, buf_ref,
                     sorted_ref, zero_ref, zsem, sems):
    i = pl.program_id(0)

    @pl.when(i == 0)
    def _():
        zero_ref[...] = jnp.zeros_like(zero_ref)
        n_used = pend_ref[N_EXPERTS - 1] // BM

        def clear_copy(start):
            return pltpu.make_async_copy(
                zero_ref, buf_ref.at[pl.ds(pl.multiple_of(start, BM), BM)], zsem)

        def each(fn):
            def expert(e, c):
                @pl.when(padded_ref[e] > 0)
                def _():
                    fn(clear_copy(pend_ref[e] - BM))
                return c
            lax.fori_loop(0, N_EXPERTS, expert, 0)

            def tail(b, c):
                fn(clear_copy(b * BM))
                return c
            lax.fori_loop(n_used, N_BLOCKS, tail, 0)
        each(lambda cp: cp.start())
        each(lambda cp: cp.wait())

    def drain(slot, tile):
        n = pl.multiple_of(nchk_ref[tile] * SEG_ALIGN, SEG_ALIGN)
        pltpu.make_async_copy(sorted_ref.at[slot, pl.ds(0, n)], buf_ref.at[pl.ds(0, n)],
                              sems.at[slot]).wait()

    def run(slot):
        pos = pos_ref[...]
        u = u_ref[...]
        for rb in range(CAP // CAP_BLOCK):
            rows = lax.broadcasted_iota(jnp.int32, (CAP_BLOCK, TM), 0) + rb * CAP_BLOCK
            hit = rows == pos[0:1]
            for k in range(1, TOP_K):
                hit = hit | (rows == pos[k:k + 1])
            onehot = jnp.where(hit, 1.0, 0.0).astype(BF16)
            sorted_ref[slot, rb * CAP_BLOCK:(rb + 1) * CAP_BLOCK, :] = jnp.dot(
                onehot, u, preferred_element_type=F32)

        def issue(j, c):
            src = pl.multiple_of(j * SEG_ALIGN, SEG_ALIGN)
            dst = pl.multiple_of(dest_ref[0, 0, j], SEG_ALIGN)
            pltpu.make_async_copy(sorted_ref.at[slot, pl.ds(src, SEG_ALIGN)],
                                  buf_ref.at[pl.ds(dst, SEG_ALIGN)], sems.at[slot]).start()
            return c
        lax.fori_loop(0, nchk_ref[i], issue, 0)

        @pl.when(i > 0)
        def _():
            drain(1 - slot, i - 1)

        @pl.when(i == N_TILES - 1)
        def _():
            drain(slot, i)
    _on_parity(i, run)


def _dispatch(pad_ends, padded, nchk, chunk_dest, pos_t, u2):
    return pl.pallas_call(
        _dispatch_kernel,
        grid_spec=pltpu.PrefetchScalarGridSpec(
            num_scalar_prefetch=3,
            grid=(N_TILES,),
            in_specs=[pl.BlockSpec((1, 1, NCHK), lambda i, a, b, c: (i, 0, 0),
                                   memory_space=pltpu.SMEM),
                      pl.BlockSpec((TOP_K, TM), lambda i, a, b, c: (0, i)),
                      pl.BlockSpec((TM, D), lambda i, a, b, c: (i, 0))],
            out_specs=pl.BlockSpec(memory_space=pl.ANY),
            scratch_shapes=[pltpu.VMEM((2, CAP, D), F32),
                            pltpu.VMEM((BM, D), F32),
                            pltpu.SemaphoreType.DMA(()),
                            pltpu.SemaphoreType.DMA((2,))]),
        out_shape=jax.ShapeDtypeStruct((ROWS, D), F32),
        compiler_params=_cparams(("arbitrary",)),
        name="dispatch",
    )(pad_ends, padded, nchk, chunk_dest, pos_t, u2)


def _ffn_kernel(be_ref, nu_ref, run_ref, nxt_ref, x_ref, wu_hbm, bu_ref, wd_hbm, bd_ref, y_ref,
                wu_ref, wd_ref, wub_ref, wdb_ref, sems):
    i = pl.program_id(0)
    used = i < nu_ref[0]

    def weight_copies(e, slot):
        return (pltpu.make_async_copy(wu_hbm.at[e], wu_ref.at[slot], sems.at[slot]),
                pltpu.make_async_copy(wd_hbm.at[e], wd_ref.at[slot], sems.at[slot]))

    @pl.when(used)
    def _():
        run = run_ref[i]

        @pl.when(run >= 0)
        def _():
            def open_run(slot):
                @pl.when(run == 0)
                def _():
                    for cp in weight_copies(be_ref[i], slot):
                        cp.start()

                @pl.when(nxt_ref[i] >= 0)
                def _():
                    for cp in weight_copies(nxt_ref[i], 1 - slot):
                        cp.start()
                for cp in weight_copies(be_ref[i], slot):
                    cp.wait()
                wub_ref[...] = wu_ref[slot].astype(BF16)
                wdb_ref[...] = wd_ref[slot].astype(BF16)
            _on_parity(run, open_run)

        h = jnp.dot(x_ref[...].astype(BF16), wub_ref[...], preferred_element_type=F32) + bu_ref[0]
        h_glu = jnp.minimum(h[:, :D], SWIGLU_LIMIT)
        h_lin = jnp.clip(h[:, D:], -SWIGLU_LIMIT, SWIGLU_LIMIT)
        act = (h_glu * _sigmoid(SWIGLU_ALPHA * h_glu) * (h_lin + 1.0)).astype(BF16)
        y_ref[...] = jnp.dot(act, wdb_ref[...], preferred_element_type=F32) + bd_ref[0]

    @pl.when(jnp.logical_not(used))
    def _():
        y_ref[...] = jnp.zeros_like(y_ref)


def _ffn(block_expert, n_used, run_id, next_expert, buf, w_up, b_up, w_down, b_down):
    def blk(i, be, nu, run, nxt):
        return jnp.minimum(i, nu[0] - 1)
    return pl.pallas_call(
        _ffn_kernel,
        grid_spec=pltpu.PrefetchScalarGridSpec(
            num_scalar_prefetch=4,
            grid=(N_BLOCKS,),
            in_specs=[pl.BlockSpec((BM, D), lambda i, *s: (blk(i, *s), 0)),
                      pl.BlockSpec(memory_space=pl.ANY),
                      pl.BlockSpec((1, 1, 2 * D), lambda i, *s: (s[0][blk(i, *s)], 0, 0)),
                      pl.BlockSpec(memory_space=pl.ANY),
                      pl.BlockSpec((1, 1, D), lambda i, *s: (s[0][blk(i, *s)], 0, 0))],
            out_specs=pl.BlockSpec((BM, D), lambda i, *s: (i, 0)),
            scratch_shapes=[pltpu.VMEM((2, D, 2 * D), F32),
                            pltpu.VMEM((2, D, D), F32),
                            pltpu.VMEM((D, 2 * D), BF16),
                            pltpu.VMEM((D, D), BF16),
                            pltpu.SemaphoreType.DMA((2,))]),
        out_shape=jax.ShapeDtypeStruct((ROWS, D), F32),
        compiler_params=_cparams(("arbitrary",)),
        name="ffn",
    )(block_expert, n_used, run_id, next_expert, buf, w_up, b_up, w_down, b_down)


def _combine_kernel(nchk_ref, dest_ref, dnext_ref, y_ref, h1_ref, gate_ref, g2_ref, lg_ref, lb_ref,
                    o_ref, sorted_ref, sems):
    i = pl.program_id(0)

    def fetch(slot, tile, table_ref):
        def issue(j, c):
            src = pl.multiple_of(table_ref[0, 0, j], SEG_ALIGN)
            dst = pl.multiple_of(j * SEG_ALIGN, SEG_ALIGN)
            pltpu.make_async_copy(y_ref.at[pl.ds(src, SEG_ALIGN)],
                                  sorted_ref.at[slot, pl.ds(dst, SEG_ALIGN)], sems.at[slot]).start()
            return c
        lax.fori_loop(0, nchk_ref[tile], issue, 0)

    def drain(slot, tile):
        n = pl.multiple_of(nchk_ref[tile] * SEG_ALIGN, SEG_ALIGN)
        pltpu.make_async_copy(y_ref.at[pl.ds(0, n)], sorted_ref.at[slot, pl.ds(0, n)],
                              sems.at[slot]).wait()

    @pl.when(i == 0)
    def _():
        sorted_ref[...] = jnp.zeros_like(sorted_ref)
        fetch(0, 0, dest_ref)

    def run(slot):
        @pl.when(i + 1 < N_TILES)
        def _():
            fetch(1 - slot, i + 1, dnext_ref)
        drain(slot, i)

        gp = gate_ref[...]
        m = jnp.zeros((TM, D), F32)
        for cb in range(CAP // CAP_BLOCK):
            cols = (lax.broadcasted_iota(jnp.int32, (TM, CAP_BLOCK), 1) + cb * CAP_BLOCK).astype(F32)
            g = jnp.where(cols == gp[:, TOP_K:TOP_K + 1], gp[:, 0:1], 0.0)
            for k in range(1, TOP_K):
                g = g + jnp.where(cols == gp[:, TOP_K + k:TOP_K + k + 1], gp[:, k:k + 1], 0.0)
            rows = sorted_ref[slot, cb * CAP_BLOCK:(cb + 1) * CAP_BLOCK, :].astype(BF16)
            m = m + jnp.dot(g.astype(BF16), rows, preferred_element_type=F32)
        o_ref[...] = _layer_norm(ALPHA * h1_ref[...] + g2_ref[...] * m, lg_ref[...], lb_ref[...])
    _on_parity(i, run)


def _combine(nchk, chunk_dest, y_buf, h1, gate_tok, g2, lg, lb):
    vec = pl.BlockSpec((1, D), lambda i, n: (0, 0))
    return pl.pallas_call(
        _combine_kernel,
        grid_spec=pltpu.PrefetchScalarGridSpec(
            num_scalar_prefetch=1,
            grid=(N_TILES,),
            in_specs=[pl.BlockSpec((1, 1, NCHK), lambda i, n: (i, 0, 0), memory_space=pltpu.SMEM),
                      pl.BlockSpec((1, 1, NCHK), lambda i, n: (jnp.minimum(i + 1, N_TILES - 1), 0, 0),
                                   memory_space=pltpu.SMEM),
                      pl.BlockSpec(memory_space=pl.ANY),
                      pl.BlockSpec((TM, D), lambda i, n: (i, 0)),
                      pl.BlockSpec((TM, LANES), lambda i, n: (i, 0)),
                      vec, vec, vec],
            out_specs=pl.BlockSpec((TM, D), lambda i, n: (i, 0)),
            scratch_shapes=[pltpu.VMEM((2, CAP, D), F32),
                            pltpu.SemaphoreType.DMA((2,))]),
        out_shape=jax.ShapeDtypeStruct((N_TOK, D), F32),
        compiler_params=_cparams(("arbitrary",)),
        name="combine",
    )(nchk, chunk_dest, chunk_dest, y_buf, h1, gate_tok, g2, lg, lb)


def _sincos_tables():
    q = D // 4
    omega = 1.0 / (10000.0 ** (jnp.arange(q, dtype=F32) / q))

    def emb(n):
        ang = jnp.arange(n, dtype=F32)[:, None] * omega[None, :]
        return jnp.concatenate([jnp.sin(ang), jnp.cos(ang)], axis=-1)
    return emb(N_TOK // GRID_W), emb(GRID_W)


def _tile_slots(a, tile):
    return jnp.transpose(a.reshape(TOP_K, N_TOK // tile, tile), (1, 0, 2)).reshape(N_TOK // tile, 1, TOP_K * tile)


def kernel(x, c, ctx, c_ctx, ln_in_g, ln_in_b, w_ada, b_ada, w_in, b_in, s5_lambda_re, s5_lambda_im, s5_log_dt, s5_b_re, s5_b_im, s5_c_re, s5_c_im, s5_d, w_glu, b_glu, w_br_s5, w_br_fft, b_br_fft, w_out, b_out, ln1_g, ln1_b, w_router, b_router, w_up, b_up, w_down, b_down, ln2_g, ln2_b):
    assert x.shape == (1, N_TOK, D) and ctx.shape == (1, N_CTX, D) and w_ada.shape[0] == 1
    row = lambda v: v.reshape(1, -1).astype(F32)

    cc = jnp.concatenate([c.reshape(1, D), c_ctx.reshape(1, D), jnp.zeros((SUBLANES - 2, D), F32)], axis=0)
    ada = _ada(cc, w_ada[0], row(b_ada[0]))
    sh1, sc1, g1, sh2, sc2, g2 = (ada[0:1, k * D:(k + 1) * D] for k in range(6))
    sh1c, sc1c = ada[1:2, 0:D], ada[1:2, D:2 * D]

    emb_r, emb_c = _sincos_tables()
    st1, st2, fc = _dft_tables()
    lg, lb = row(ln_in_g), row(ln_in_b)

    w_s5 = w_in[0][:, :S5_WIDTH]
    w_fft = w_in[0][:, S5_WIDTH:S5_WIDTH + FFT_WIDTH]
    w_g = w_in[0][:, S5_WIDTH + FFT_WIDTH:]
    b_s5 = row(b_in[0][:S5_WIDTH])
    b_fft8 = jnp.concatenate([row(b_in[0][S5_WIDTH:S5_WIDTH + FFT_WIDTH]),
                              jnp.zeros((SUBLANES - 1, FFT_WIDTH), F32)], axis=0)
    b_g = row(b_in[0][S5_WIDTH + FFT_WIDTH:])
    w_fc, b_fc = _fft_weights(w_fft, b_fft8, fc)
    wcat = jnp.concatenate([w_s5, w_fc], axis=1).astype(BF16)
    bcat = jnp.concatenate([b_s5, b_fc[0:1]], axis=1)

    x2 = x[0]
    p_t, xr, xi = _proj(x2, emb_r, emb_c, lg, lb, 1.0 + sc1, sh1, wcat, bcat)
    pc_t = _ctx_proj(ctx[0], lg, lb, 1.0 + sc1c, sh1c, w_s5.astype(BF16), b_s5)

    b_c, a_q, a_p, trans, ctx_w = _s5_tables(
        s5_lambda_re[0], s5_lambda_im[0], s5_log_dt[0], s5_b_re[0], s5_b_im[0],
        s5_c_re[0], s5_c_im[0], s5_d[0])
    w_m, w_q, w_p = _s5_expand(b_c, a_q, a_p)
    y_t = _s5(p_t, pc_t, w_m, w_q, w_p, trans, ctx_w)

    yr, yi = _fft1(xr, xi, st1)
    zr = _fft2(yr, yi, st2)

    tri = (jnp.arange(TM)[:, None] < jnp.arange(TM)[None, :]).astype(BF16)
    br = jnp.broadcast_to(b_router[0].reshape(N_EXPERTS, 1), (N_EXPERTS, LANES))
    etri = (jnp.arange(N_EXPERTS)[:, None] > jnp.arange(N_EXPERTS)[None, :]).astype(BF16)
    h1, u2, pos_t, gate_tok, counts = _mix(
        x2, emb_r, emb_c, lg, lb, 1.0 + sc1, sh1, w_g.astype(BF16), b_g, y_t, zr,
        w_glu[0].astype(BF16), row(b_glu[0]), w_br_s5[0].astype(BF16), w_br_fft[0].astype(BF16),
        row(b_br_fft[0]), w_out[0].astype(BF16), row(b_out[0]), g1, row(ln1_g[0]), row(ln1_b[0]),
        1.0 + sc2, sh2, jnp.transpose(w_router[0]), br, tri, etri)

    cnt = counts[:, :, 0].astype(jnp.int32)
    seg = (cnt + SEG_ALIGN - 1) // SEG_ALIGN * SEG_ALIGN
    seg_end = jnp.cumsum(seg, axis=1)
    seg_start = seg_end - seg
    padded = (jnp.sum(seg, axis=0) + BM - 1) // BM * BM
    pad_ends = jnp.cumsum(padded)
    seg_dest = (pad_ends - padded)[None, :] + jnp.cumsum(seg, axis=0) - seg
    chunk_row = jnp.arange(NCHK, dtype=jnp.int32) * SEG_ALIGN
    chunk_exp = jnp.minimum(jnp.sum(chunk_row[None, :, None] >= seg_end[:, None, :], axis=-1),
                            N_EXPERTS - 1)
    own = chunk_exp[:, :, None] == jnp.arange(N_EXPERTS, dtype=jnp.int32)[None, None, :]
    chunk_dest = (jnp.sum(jnp.where(own, (seg_dest - seg_start)[:, None, :], 0), axis=-1)
                  + chunk_row[None, :]).astype(jnp.int32).reshape(N_TILES, 1, NCHK)
    nchk = (seg_end[:, -1] // SEG_ALIGN).astype(jnp.int32)
    block_start = jnp.arange(N_BLOCKS, dtype=jnp.int32) * BM
    block_expert = jnp.minimum(jnp.sum(block_start[:, None] >= pad_ends[None, :], axis=1),
                               N_EXPERTS - 1).astype(jnp.int32)
    n_used = (pad_ends[-1:] // BM).astype(jnp.int32)
    opens = (block_start < pad_ends[-1]) & (
        block_expert != jnp.concatenate([jnp.full((1,), -1, jnp.int32), block_expert[:-1]]))
    run_id = jnp.where(opens, jnp.cumsum(opens.astype(jnp.int32)) - 1, -1).astype(jnp.int32)
    experts = jnp.arange(N_EXPERTS, dtype=jnp.int32)
    later = (experts[None, :] > block_expert[:, None]) & (padded[None, :] > 0)
    next_expert = jnp.min(jnp.where(later, experts[None, :], N_EXPERTS), axis=1)
    next_expert = jnp.where(next_expert < N_EXPERTS, next_expert, -1).astype(jnp.int32)

    buf = _dispatch(pad_ends.astype(jnp.int32), padded.astype(jnp.int32), nchk, chunk_dest, pos_t, u2)
    y_buf = _ffn(block_expert, n_used, run_id, next_expert, buf, w_up[0],
                 b_up[0].reshape(N_EXPERTS, 1, 2 * D), w_down[0], b_down[0].reshape(N_EXPERTS, 1, D))
    out = _combine(nchk, chunk_dest, y_buf, h1, gate_tok, g2, row(ln2_g[0]), row(ln2_b[0]))
    return out.reshape(1, N_TOK, D)
```

```python
import functools
import math

import jax
import jax.numpy as jnp
import numpy as np
from jax import lax
from jax.experimental import pallas as pl
from jax.experimental.pallas import tpu as pltpu

F32 = jnp.float32
BF16 = jnp.bfloat16
HI = lax.Precision.HIGHEST

D = 1024
N_TOK = 16384
N_CTX = 256
GRID_W = 64
S5_GROUP = 16
S5_GROUPS = 32
S5_STATE = 64
S5_WIDTH = 512
FFT_GROUPS = 4
FFT_DIM = 128
FFT_WIDTH = 512
N_EXPERTS = 32
TOP_K = 4
LN_EPS = 1e-5
ALPHA = 2.0 ** 0.25
SWIGLU_ALPHA = 1.702
SWIGLU_LIMIT = 7.0

LANES = 128
SUBLANES = 8
VMEM_LIMIT = 56 * 1024 * 1024

CH = 8
N_CHUNK = N_TOK // CH
N_CHUNK_CTX = N_CTX // CH
NSEG = SUBLANES
SEG = N_CHUNK // NSEG
PITCH = SEG + 8
SCAN_UNROLL = 4
GPT = LANES // S5_GROUP
NJ = S5_WIDTH // LANES
CL = CH * LANES
SW = 4 * GPT * S5_STATE

FN = 128
FB = 16

TM = 512
N_TILES = N_TOK // TM
MIX_GROUPS = 1
BM = 512
N_SLOTS = N_TOK * TOP_K
SEG_ALIGN = 2 * SUBLANES
CAP_BLOCK = 256
CAP = -(-(TOP_K * TM + N_EXPERTS * (SEG_ALIGN - 1)) // CAP_BLOCK) * CAP_BLOCK
NCHK = CAP // SEG_ALIGN
N_BLOCKS = -(-(N_SLOTS + N_TILES * N_EXPERTS * (SEG_ALIGN - 1)) // BM) + N_EXPERTS
ROWS = N_BLOCKS * BM


def _cparams(sem):
    return pltpu.CompilerParams(dimension_semantics=sem, vmem_limit_bytes=VMEM_LIMIT)


def _layer_norm(x, g, b):
    mu = jnp.mean(x, axis=-1, keepdims=True)
    xc = x - mu
    var = jnp.mean(xc * xc, axis=-1, keepdims=True)
    return xc * lax.rsqrt(var + LN_EPS) * g + b


def _sigmoid(x):
    return 1.0 / (1.0 + jnp.exp(-x))


def _ada_kernel(c_ref, w_ref, b_ref, o_ref):
    c = c_ref[...]
    s = c * _sigmoid(c)
    o_ref[...] = jnp.dot(s, w_ref[...], preferred_element_type=F32, precision=HI) + b_ref[...]


def _ada(cc, w_ada, b_ada):
    nb = 4
    wb = 6 * D // nb
    return pl.pallas_call(
        _ada_kernel,
        grid=(nb,),
        in_specs=[pl.BlockSpec((SUBLANES, D), lambda i: (0, 0)),
                  pl.BlockSpec((D, wb), lambda i: (0, i)),
                  pl.BlockSpec((1, wb), lambda i: (0, i))],
        out_specs=pl.BlockSpec((SUBLANES, wb), lambda i: (0, i)),
        out_shape=jax.ShapeDtypeStruct((SUBLANES, 6 * D), F32),
        compiler_params=_cparams(("parallel",)),
        name="ada",
    )(cc, w_ada, b_ada)


def _fftw_kernel(w_ref, b_ref, f_ref, wo_ref, bo_ref):
    f = f_ref[...]
    wo_ref[...] = jnp.dot(w_ref[...], f, preferred_element_type=F32, precision=HI)
    bo_ref[...] = jnp.dot(b_ref[...], f, preferred_element_type=F32, precision=HI)


def _fft_weights(w_fft, b_fft8, fc):
    return pl.pallas_call(
        _fftw_kernel,
        out_shape=(jax.ShapeDtypeStruct((D, 2 * FFT_WIDTH), F32),
                   jax.ShapeDtypeStruct((SUBLANES, 2 * FFT_WIDTH), F32)),
        compiler_params=pltpu.CompilerParams(vmem_limit_bytes=VMEM_LIMIT),
        name="fftw",
    )(w_fft, b_fft8, fc)


def _pos_code(er_ref, ec_ref, tm):
    nr = tm // GRID_W
    er = er_ref[...]
    row = jnp.broadcast_to(er[:, None, :], (nr, GRID_W, D // 2)).reshape(tm, D // 2)
    col = jnp.concatenate([ec_ref[...]] * nr, axis=0)
    return jnp.concatenate([row, col], axis=-1)


def _to_chunk_major(val, scr_ref, out_ref, tm):
    for j in range(NJ):
        scr_ref[j] = val[:, j * LANES:(j + 1) * LANES]
    for t in range(CH):
        for j in range(NJ):
            piece = scr_ref[j, pl.ds(t, tm // CH, stride=CH), :]
            out_ref[t, :, j * LANES:(j + 1) * LANES] = piece.astype(out_ref.dtype)


def _proj_kernel(x_ref, er_ref, ec_ref, lg_ref, lb_ref, m_ref, s_ref, w_ref, b_ref,
                 p_ref, xr_ref, xi_ref, scr_ref):
    x = x_ref[...] + _pos_code(er_ref, ec_ref, TM)
    h = _layer_norm(x, lg_ref[...], lb_ref[...])
    u = (h * m_ref[...] + s_ref[...]).astype(BF16)
    p = jnp.dot(u, w_ref[...], preferred_element_type=F32) + b_ref[...]
    _to_chunk_major(p[:, :S5_WIDTH], scr_ref, p_ref, TM)
    xr_ref[...] = p[:, S5_WIDTH:S5_WIDTH + FFT_WIDTH].astype(BF16)
    xi_ref[...] = p[:, S5_WIDTH + FFT_WIDTH:].astype(BF16)


def _proj(x, emb_r, emb_c, lg, lb, m1, s1, wcat, bcat):
    nw = wcat.shape[1]
    vec = pl.BlockSpec((1, D), lambda i: (0, 0))
    return pl.pallas_call(
        _proj_kernel,
        grid=(N_TOK // TM,),
        in_specs=[pl.BlockSpec((TM, D), lambda i: (i, 0)),
                  pl.BlockSpec((TM // GRID_W, D // 2), lambda i: (i, 0)),
                  pl.BlockSpec((GRID_W, D // 2), lambda i: (0, 0)),
                  vec, vec, vec, vec,
                  pl.BlockSpec((D, nw), lambda i: (0, 0)),
                  pl.BlockSpec((1, nw), lambda i: (0, 0))],
        out_specs=(pl.BlockSpec((CH, TM // CH, S5_WIDTH), lambda i: (0, i, 0)),
                   pl.BlockSpec((TM, FFT_WIDTH), lambda i: (i, 0)),
                   pl.BlockSpec((TM, FFT_WIDTH), lambda i: (i, 0))),
        out_shape=(jax.ShapeDtypeStruct((CH, N_CHUNK, S5_WIDTH), BF16),
                   jax.ShapeDtypeStruct((N_TOK, FFT_WIDTH), BF16),
                   jax.ShapeDtypeStruct((N_TOK, FFT_WIDTH), BF16)),
        scratch_shapes=[pltpu.VMEM((NJ, TM, LANES), F32)],
        compiler_params=_cparams(("parallel",)),
        name="proj",
    )(x, emb_r, emb_c, lg, lb, m1, s1, wcat, bcat)


def _ctx_proj_kernel(x_ref, lg_ref, lb_ref, m_ref, s_ref, w_ref, b_ref, p_ref, scr_ref):
    h = _layer_norm(x_ref[...], lg_ref[...], lb_ref[...])
    u = (h * m_ref[...] + s_ref[...]).astype(BF16)
    p = jnp.dot(u, w_ref[...], preferred_element_type=F32) + b_ref[...]
    _to_chunk_major(p, scr_ref, p_ref, N_CTX)


def _ctx_proj(ctx, lg, lb, m1, s1, w_s5, b_s5):
    return pl.pallas_call(
        _ctx_proj_kernel,
        out_shape=jax.ShapeDtypeStruct((CH, N_CHUNK_CTX, S5_WIDTH), BF16),
        scratch_shapes=[pltpu.VMEM((NJ, N_CTX, LANES), F32)],
        compiler_params=pltpu.CompilerParams(vmem_limit_bytes=VMEM_LIMIT),
        name="ctxproj",
    )(ctx, lg, lb, m1, s1, w_s5, b_s5)


def _s5_tables(lam_re, lam_im, log_dt, b_re, b_im, c_re, c_im, d_skip):
    dt = jnp.exp(log_dt)[..., None]
    zr = lam_re * dt
    zi = lam_im * dt

    def apow(m):
        m = jnp.asarray(m, F32)
        mag = jnp.exp(zr[..., None] * m)
        return mag * jnp.cos(zi[..., None] * m), mag * jnp.sin(zi[..., None] * m)

    a_re, a_im = apow(jnp.ones((1,), F32))
    a_re, a_im = a_re[..., 0], a_im[..., 0]
    den = lam_re * lam_re + lam_im * lam_im
    num_re = a_re - 1.0
    k_re = (num_re * lam_re + a_im * lam_im) / den
    k_im = (a_im * lam_re - num_re * lam_im) / den
    bb_re = k_re[..., None] * b_re - k_im[..., None] * b_im
    bb_im = k_re[..., None] * b_im + k_im[..., None] * b_re

    ks = jnp.arange(CH + 1, dtype=F32)
    pw_re, pw_im = apow(ks)

    def tap(c, pw, bb):
        return jnp.einsum('dgvp,dgpk,dgph->kdghv', c, pw[..., :CH], bb, precision=HI)
    taps = (tap(c_re, pw_re, bb_re) - tap(c_re, pw_im, bb_im)
            - tap(c_im, pw_re, bb_im) - tap(c_im, pw_im, bb_re))
    skip = d_skip.reshape(S5_GROUPS, S5_GROUP, 1) * jnp.eye(S5_GROUP, dtype=F32)[None]
    taps = taps.at[0, 0].add(skip)
    b_c = jnp.transpose(taps.reshape(CH, 2, NJ, LANES, S5_GROUP), (2, 0, 1, 3, 4))
    b_c = b_c.reshape(NJ, 2 * CH, LANES, S5_GROUP)

    ef = (CH - 1) - jnp.arange(CH)
    eb = jnp.arange(CH)

    def q_part(d, e):
        pr = jnp.transpose(pw_re[d][..., e], (0, 2, 1))[:, :, None, :]
        pi = jnp.transpose(pw_im[d][..., e], (0, 2, 1))[:, :, None, :]
        br = jnp.transpose(bb_re[d], (0, 2, 1))[:, None, :, :]
        bi = jnp.transpose(bb_im[d], (0, 2, 1))[:, None, :, :]
        return pr * br - pi * bi, pr * bi + pi * br

    def q_rows(v):
        v = v.reshape(NJ, GPT, CH, S5_GROUP, S5_STATE)
        return jnp.transpose(v, (0, 2, 1, 3, 4)).reshape(NJ, CL, S5_STATE)
    a_q = jnp.stack([q_rows(v) for v in q_part(0, ef) + q_part(1, eb)], axis=0)

    of = jnp.arange(CH) + 1
    ob = CH - jnp.arange(CH)

    def p_part(d, e):
        pr = pw_re[d][..., e][:, :, :, None]
        pi = pw_im[d][..., e][:, :, :, None]
        cr = jnp.transpose(c_re[d], (0, 2, 1))[:, :, None, :]
        ci = jnp.transpose(c_im[d], (0, 2, 1))[:, :, None, :]
        return cr * pr - ci * pi, -(cr * pi + ci * pr)
    a_p = jnp.stack([v.reshape(NJ, GPT * S5_STATE, CH * S5_GROUP)
                     for v in p_part(0, of) + p_part(1, ob)], axis=0)

    def lanes(v):
        return jnp.transpose(v.reshape(2, NJ, GPT * S5_STATE), (1, 0, 2))
    c_r, c_i = apow(jnp.full((1,), float(CH), F32))
    s_r, s_i = apow(jnp.full((1,), float(CH * SEG), F32))
    cr, ci, sr, si = (lanes(v[..., 0]) for v in (c_r, c_i, s_r, s_i))
    trans = jnp.stack([cr[:, 0], ci[:, 0], cr[:, 1], ci[:, 1],
                       sr[:, 0], si[:, 0], sr[:, 1], si[:, 1]], axis=1)

    cidx = jnp.arange(N_CHUNK_CTX, dtype=F32)
    wf_r, wf_i = apow(CH * (N_CHUNK_CTX - 1 - cidx))
    wb_r, wb_i = apow(CH * cidx)

    def ctx_lanes(v, d):
        return jnp.transpose(v[d].reshape(NJ, GPT * S5_STATE, N_CHUNK_CTX), (0, 2, 1))
    ctx_w = jnp.stack([ctx_lanes(wf_r, 0), ctx_lanes(wf_i, 0),
                       ctx_lanes(wb_r, 1), ctx_lanes(wb_i, 1)], axis=1)
    return b_c, a_q, a_p, trans, ctx_w


def _s5w_kernel(bc_ref, aq_ref, ap_ref, c16_ref, c64_ref, cm_ref, wm_ref, wq_ref, wp_ref):
    def expand(a, c, row_shift, col_shift):
        w = jnp.dot(a, c, preferred_element_type=F32)
        rg = (lax.broadcasted_iota(jnp.int32, (w.shape[0], 1), 0) >> row_shift) & (GPT - 1)
        cg = (lax.broadcasted_iota(jnp.int32, (1, w.shape[1]), 1) >> col_shift) & (GPT - 1)
        return jnp.where(rg == cg, w, 0.0)

    blk = [expand(bc_ref[0, kd], c16_ref[...], 4, 4) for kd in range(2 * CH)]
    for t in range(CH):
        for u in range(CH):
            b = blk[2 * (u - t)] if u > t else blk[2 * (t - u) + 1] if u < t else blk[0] + blk[1]
            wm_ref[0, t * LANES:(t + 1) * LANES, u * LANES:(u + 1) * LANES] = b.astype(BF16)
    half = GPT * S5_STATE
    for s in range(4):
        wq_ref[0, :, s * half:(s + 1) * half] = expand(aq_ref[s, 0], c64_ref[...], 4, 6).astype(BF16)
        wp_ref[0, s * half:(s + 1) * half, :] = expand(ap_ref[s, 0], cm_ref[...], 6, 4).astype(BF16)


def _s5_expand(b_c, a_q, a_p):
    rep = np.ones((1, GPT))
    c16 = jnp.asarray(np.kron(rep, np.eye(S5_GROUP)), F32).astype(BF16)
    c64 = jnp.asarray(np.kron(rep, np.eye(S5_STATE)), F32).astype(BF16)
    c_m = jnp.asarray(np.kron(np.eye(CH), np.kron(rep, np.eye(S5_GROUP))), F32).astype(BF16)
    b_c, a_q, a_p = b_c.astype(BF16), a_q.astype(BF16), a_p.astype(BF16)
    half = GPT * S5_STATE
    return pl.pallas_call(
        _s5w_kernel,
        grid=(NJ,),
        in_specs=[pl.BlockSpec((1, 2 * CH, LANES, S5_GROUP), lambda j: (j, 0, 0, 0)),
                  pl.BlockSpec((4, 1, CL, S5_STATE), lambda j: (0, j, 0, 0)),
                  pl.BlockSpec((4, 1, half, CH * S5_GROUP), lambda j: (0, j, 0, 0)),
                  pl.BlockSpec(c16.shape, lambda j: (0, 0)),
                  pl.BlockSpec(c64.shape, lambda j: (0, 0)),
                  pl.BlockSpec(c_m.shape, lambda j: (0, 0))],
        out_specs=(pl.BlockSpec((1, CL, CL), lambda j: (j, 0, 0)),
                   pl.BlockSpec((1, CL, SW), lambda j: (j, 0, 0)),
                   pl.BlockSpec((1, SW, CL), lambda j: (j, 0, 0))),
        out_shape=(jax.ShapeDtypeStruct((NJ, CL, CL), BF16),
                   jax.ShapeDtypeStruct((NJ, CL, SW), BF16),
                   jax.ShapeDtypeStruct((NJ, SW, CL), BF16)),
        compiler_params=_cparams(("parallel",)),
        name="s5w",
    )(b_c, a_q, a_p, c16, c64, c_m)


def _s5_kernel(p_ref, pc_ref, wm_ref, wq_ref, wp_ref, tr_ref, cw_ref, y_ref, v_ref):
    nq = NJ
    half = GPT * S5_STATE

    def chunk_rows(ref, r0, nrows):
        return jnp.concatenate([ref[t, pl.ds(r0, nrows), :] for t in range(CH)], axis=-1)

    def fill(k, c):
        r0 = pl.multiple_of(k * SEG, SEG)
        v = jnp.dot(chunk_rows(p_ref, r0, SEG), wq_ref[0], preferred_element_type=F32)
        o0 = pl.multiple_of(k * PITCH, SUBLANES)
        for s in range(4 * nq):
            v_ref[s, pl.ds(o0, SEG), :] = v[:, s * LANES:(s + 1) * LANES]
        return c
    lax.fori_loop(0, NSEG, fill, 0)

    vc = jnp.dot(chunk_rows(pc_ref, 0, N_CHUNK_CTX), wq_ref[0], preferred_element_type=F32)
    vfr, vfi, vbr, vbi = (vc[:, i * half:(i + 1) * half] for i in range(4))
    wfr, wfi, wbr, wbi = (cw_ref[0, i] for i in range(4))
    s0_fr = jnp.sum(wfr * vfr - wfi * vfi, axis=0, keepdims=True)
    s0_fi = jnp.sum(wfr * vfi + wfi * vfr, axis=0, keepdims=True)
    s0_br = jnp.sum(wbr * vbr - wbi * vbi, axis=0, keepdims=True)
    s0_bi = jnp.sum(wbr * vbi + wbi * vbr, axis=0, keepdims=True)

    tr = tr_ref[0]
    afr, afi, abr, abi = (jnp.broadcast_to(tr[i:i + 1], (NSEG, half)) for i in range(4))
    gfr, gfi, gbr, gbi = (tr[i:i + 1] for i in range(4, 8))

    def load_part(part, i):
        return jnp.concatenate(
            [v_ref[part * nq + q, pl.ds(i, NSEG, stride=PITCH), :] for q in range(nq)], axis=-1)

    def store_part(part, i, val):
        for q in range(nq):
            v_ref[part * nq + q, pl.ds(i, NSEG, stride=PITCH), :] = val[:, q * LANES:(q + 1) * LANES]

    def step(i, carry, write):
        fr, fi, br, bi = carry
        ib = SEG - 1 - i
        ufr, ufi = load_part(0, i), load_part(1, i)
        ubr, ubi = load_part(2, ib), load_part(3, ib)
        if write:
            store_part(0, i, fr)
            store_part(1, i, fi)
            store_part(2, ib, br)
            store_part(3, ib, bi)
        return (afr * fr - afi * fi + ufr, afr * fi + afi * fr + ufi,
                abr * br - abi * bi + ubr, abr * bi + abi * br + ubi)

    zero = jnp.zeros((NSEG, half), F32)
    ffr, ffi, fbr, fbi = lax.fori_loop(0, SEG, functools.partial(step, write=False),
                                       (zero, zero, zero, zero), unroll=SCAN_UNROLL)

    rows_fr, rows_fi = [s0_fr], [s0_fi]
    for k in range(1, NSEG):
        pr, pi = rows_fr[-1], rows_fi[-1]
        rows_fr.append(gfr * pr - gfi * pi + ffr[k - 1:k])
        rows_fi.append(gfr * pi + gfi * pr + ffi[k - 1:k])
    rows_br, rows_bi = [s0_br], [s0_bi]
    for k in range(NSEG - 2, -1, -1):
        pr, pi = rows_br[0], rows_bi[0]
        rows_br.insert(0, gbr * pr - gbi * pi + fbr[k + 1:k + 2])
        rows_bi.insert(0, gbr * pi + gbi * pr + fbi[k + 1:k + 2])
    init = tuple(jnp.concatenate(r, axis=0) for r in (rows_fr, rows_fi, rows_br, rows_bi))

    lax.fori_loop(0, SEG, functools.partial(step, write=True), init, unroll=SCAN_UNROLL)

    def emit(k, c):
        r0 = pl.multiple_of(k * SEG, SEG)
        o0 = pl.multiple_of(k * PITCH, SUBLANES)
        b = chunk_rows(p_ref, r0, SEG)
        sin = jnp.concatenate([v_ref[s, pl.ds(o0, SEG), :] for s in range(4 * nq)], axis=-1)
        y = (jnp.dot(b, wm_ref[0], preferred_element_type=F32)
             + jnp.dot(sin.astype(BF16), wp_ref[0], preferred_element_type=F32))
        for t in range(CH):
            y_ref[t, pl.ds(r0, SEG), :] = y[:, t * LANES:(t + 1) * LANES].astype(y_ref.dtype)
        return c
    lax.fori_loop(0, NSEG, emit, 0)


def _s5(p_t, pc_t, w_m, w_q, w_p, trans, ctx_w):
    one = pl.Buffered(1)
    return pl.pallas_call(
        _s5_kernel,
        grid=(NJ,),
        in_specs=[pl.BlockSpec((CH, N_CHUNK, LANES), lambda j: (0, 0, j)),
                  pl.BlockSpec((CH, N_CHUNK_CTX, LANES), lambda j: (0, 0, j)),
                  pl.BlockSpec((1, CL, CL), lambda j: (j, 0, 0), pipeline_mode=one),
                  pl.BlockSpec((1, CL, SW), lambda j: (j, 0, 0), pipeline_mode=one),
                  pl.BlockSpec((1, SW, CL), lambda j: (j, 0, 0), pipeline_mode=one),
                  pl.BlockSpec((1, SUBLANES, GPT * S5_STATE), lambda j: (j, 0, 0)),
                  pl.BlockSpec((1, 4, N_CHUNK_CTX, GPT * S5_STATE), lambda j: (j, 0, 0, 0))],
        out_specs=pl.BlockSpec((CH, N_CHUNK, LANES), lambda j: (0, 0, j)),
        out_shape=jax.ShapeDtypeStruct((CH, N_CHUNK, S5_WIDTH), BF16),
        scratch_shapes=[pltpu.VMEM((4 * NJ, NSEG * PITCH, LANES), F32)],
        compiler_params=_cparams(("parallel",)),
        name="s5",
    )(p_t, pc_t, w_m, w_q, w_p, trans, ctx_w)


def _dft_tables():
    n = np.arange(FN)
    ang = 2.0 * np.pi * np.outer(n, n) / FN
    c, s = np.cos(ang), np.sin(ang)
    st1 = np.block([[c, s], [-s, c]])
    tw = 2.0 * np.pi * np.outer(n, n) / (FN * FN)
    wr, wi = np.cos(tw), -np.sin(tw)
    fr = c[None] * wr[:, None, :] + s[None] * wi[:, None, :]
    fi = c[None] * wi[:, None, :] - s[None] * wr[:, None, :]
    st2 = np.concatenate([fr, -fi], axis=-1)
    scale = 1.0 / math.sqrt(N_TOK * FFT_DIM)
    blk_c = np.kron(np.eye(FFT_GROUPS), c) * scale
    blk_s = np.kron(np.eye(FFT_GROUPS), s) * scale
    fc = np.concatenate([blk_c, -blk_s], axis=1)
    return (jnp.asarray(st1, F32).astype(BF16), jnp.asarray(st2, F32).astype(BF16), jnp.asarray(fc, F32))


FSL = FFT_WIDTH // LANES


def _rows_to_slabs(val, slab_ref, first):
    for s in range(FSL):
        slab_ref[first + s] = val[:, s * LANES:(s + 1) * LANES]


def _slabs_to_block(slab_ref, first):
    val = jnp.concatenate([slab_ref[first + s] for s in range(FSL)], axis=-1)
    return val.reshape(FN, FB, FFT_WIDTH)


def _fft1_kernel(xr_ref, xi_ref, f_ref, yr_ref, yi_ref, in_ref, out_ref):
    _rows_to_slabs(xr_ref[...].astype(F32).reshape(FN * FB, FFT_WIDTH), in_ref, 0)
    _rows_to_slabs(xi_ref[...].astype(F32).reshape(FN * FB, FFT_WIDTH), in_ref, FSL)
    for b in range(FB):
        def part(first):
            return jnp.concatenate([in_ref[first + s, pl.ds(b, FN, stride=FB), :] for s in range(FSL)],
                                   axis=-1)
        xs = jnp.concatenate([part(0), part(FSL)], axis=0).astype(BF16)
        y = jnp.dot(f_ref[...], xs, preferred_element_type=F32)
        for s in range(FSL):
            out_ref[s, pl.ds(b, FN, stride=FB), :] = y[:FN, s * LANES:(s + 1) * LANES]
            out_ref[FSL + s, pl.ds(b, FN, stride=FB), :] = y[FN:, s * LANES:(s + 1) * LANES]
    yr_ref[...] = _slabs_to_block(out_ref, 0).astype(BF16)
    yi_ref[...] = _slabs_to_block(out_ref, FSL).astype(BF16)


def _fft1(xr, xi, st1):
    spec = pl.BlockSpec((FN, FB, FFT_WIDTH), lambda i: (0, i, 0))
    slabs = pltpu.VMEM((2 * FSL, FN * FB, LANES), F32)
    return pl.pallas_call(
        _fft1_kernel,
        grid=(FN // FB,),
        in_specs=[spec, spec, pl.BlockSpec((2 * FN, 2 * FN), lambda i: (0, 0))],
        out_specs=(spec, spec),
        out_shape=(jax.ShapeDtypeStruct((FN, FN, FFT_WIDTH), BF16),) * 2,
        scratch_shapes=[slabs, slabs],
        compiler_params=_cparams(("parallel",)),
        name="fft1",
    )(xr.reshape(FN, FN, FFT_WIDTH), xi.reshape(FN, FN, FFT_WIDTH), st1)


def _fft2_kernel(yr_ref, yi_ref, f_ref, z_ref, out_ref):
    for b in range(FB):
        ys = jnp.concatenate([yr_ref[b * FN:(b + 1) * FN, :], yi_ref[b * FN:(b + 1) * FN, :]], axis=0)
        z = jnp.dot(f_ref[b], ys, preferred_element_type=F32)
        for s in range(FSL):
            out_ref[s, pl.ds(b, FN, stride=FB), :] = z[:, s * LANES:(s + 1) * LANES]
    z_ref[...] = _slabs_to_block(out_ref, 0).astype(BF16)


def _fft2(yr, yi, st2):
    rows = pl.BlockSpec((FB * FN, FFT_WIDTH), lambda i: (i, 0))
    z = pl.pallas_call(
        _fft2_kernel,
        grid=(FN // FB,),
        in_specs=[rows, rows, pl.BlockSpec((FB, FN, 2 * FN), lambda i: (i, 0, 0))],
        out_specs=pl.BlockSpec((FN, FB, FFT_WIDTH), lambda i: (0, i, 0)),
        out_shape=jax.ShapeDtypeStruct((FN, FN, FFT_WIDTH), BF16),
        scratch_shapes=[pltpu.VMEM((FSL, FN * FB, LANES), F32)],
        compiler_params=_cparams(("parallel",)),
        name="fft2",
    )(yr.reshape(N_TOK, FFT_WIDTH), yi.reshape(N_TOK, FFT_WIDTH), st2)
    return z.reshape(N_TOK, FFT_WIDTH)


def _gelu_tanh(x):
    return 0.5 * x * (1.0 + jnp.tanh(math.sqrt(2.0 / math.pi) * (x + 0.044715 * (x * x * x))))


def _mix_kernel(x_ref, er_ref, ec_ref, lg_ref, lb_ref, m1_ref, s1_ref, wg_ref, bg_ref,
                yt_ref, zr_ref, wglu_ref, bglu_ref, wbs_ref, wbf_ref, bbf_ref, wo_ref, bo_ref,
                g1_ref, l1g_ref, l1b_ref, m2_ref, s2_ref, wr_ref, br_ref, tri_ref, etri_ref,
                h1_ref, u2_ref, pos_ref, gate_ref, cnt_ref, scr_ref):
    pos = _pos_code(er_ref, ec_ref, TM)

    def front(r0, nr):
        rows = slice(r0, r0 + nr)
        h = _layer_norm(x_ref[rows, :] + pos[rows, :], lg_ref[...], lb_ref[...])
        u = (h * m1_ref[...] + s1_ref[...]).astype(BF16)

        c0, nc = r0 // CH, nr // CH
        for t in range(CH):
            for j in range(NJ):
                scr_ref[j, pl.ds(r0 + t, nc, stride=CH), :] = (
                    yt_ref[t, c0:c0 + nc, j * LANES:(j + 1) * LANES].astype(F32))
        ys = jnp.concatenate([scr_ref[j, rows, :] for j in range(NJ)], axis=-1)
        z = jnp.dot(_gelu_tanh(ys).astype(BF16), wglu_ref[...], preferred_element_type=F32) + bglu_ref[...]
        glu = (z[:, :S5_WIDTH] * _sigmoid(z[:, S5_WIDTH:])).astype(BF16)
        gates = _sigmoid(jnp.dot(u, wg_ref[...], preferred_element_type=F32) + bg_ref[...])
        y_s5 = jnp.dot(glu, wbs_ref[...], preferred_element_type=F32)
        y_fft = jnp.dot(zr_ref[rows, :], wbf_ref[...], preferred_element_type=F32) + bbf_ref[...]
        mixed = (gates[:, :D] * y_s5 + gates[:, D:] * y_fft).astype(BF16)
        y = jnp.dot(mixed, wo_ref[...], preferred_element_type=F32) + bo_ref[...]
        h1 = _layer_norm(ALPHA * h + g1_ref[...] * y, l1g_ref[...], l1b_ref[...])
        h1_ref[rows, :] = h1
        u2 = h1 * m2_ref[...] + s2_ref[...]
        u2_ref[rows, :] = u2.astype(BF16)
        return lax.dot_general(wr_ref[...], u2, (((1,), (1,)), ((), ())),
                               preferred_element_type=F32, precision=HI)

    nr = TM // MIX_GROUPS
    logits = jnp.concatenate([front(g * nr, nr) for g in range(MIX_GROUPS)], axis=-1) + br_ref[:, 0:1]
    eidx = lax.broadcasted_iota(jnp.int32, (N_EXPERTS, TM), 0)
    vals, hots = [], []
    cur = logits
    for _k in range(TOP_K):
        m = jnp.max(cur, axis=0, keepdims=True)
        sel = jnp.min(jnp.where(cur == m, eidx, N_EXPERTS), axis=0, keepdims=True)
        hot = eidx == sel
        cur = jnp.where(hot, -jnp.inf, cur)
        vals.append(m)
        hots.append(hot)
    exps = [jnp.exp(v - vals[0]) for v in vals]
    den = exps[0] + exps[1] + exps[2] + exps[3]
    gate4 = jnp.concatenate([e / den for e in exps], axis=0)

    hot_sum = (hots[0] | hots[1] | hots[2] | hots[3]).astype(F32)
    before = jnp.dot(hot_sum.astype(BF16), tri_ref[...], preferred_element_type=F32)
    cnt = jnp.broadcast_to(jnp.sum(hot_sum, axis=1, keepdims=True), (N_EXPERTS, LANES))
    cnt8 = jnp.floor((cnt + (SEG_ALIGN - 1)) * (1.0 / SEG_ALIGN)) * SEG_ALIGN
    seg0 = jnp.dot(etri_ref[...], cnt8.astype(BF16), preferred_element_type=F32)
    tot = seg0[:, 0:1] + before
    pos4 = jnp.concatenate(
        [jnp.sum(jnp.where(hk, tot, 0.0), axis=0, keepdims=True) for hk in hots], axis=0)
    pos_ref[...] = pos4.astype(jnp.int32)
    cnt_ref[0] = cnt

    gpad = jnp.concatenate([gate4, pos4, jnp.zeros((LANES - 2 * TOP_K, TM), F32)], axis=0)
    gate_ref[...] = gpad.T


def _mix(x, emb_r, emb_c, lg, lb, m1, s1, wg, bg, y_t, zr, wglu, bglu, wbs, wbf, bbf, wo, bo,
         g1, l1g, l1b, m2, s2, wr_t, br, tri, etri):
    vec = pl.BlockSpec((1, D), lambda i: (0, 0))

    def full(a):
        return pl.BlockSpec(a.shape, lambda i: (0,) * a.ndim)
    return pl.pallas_call(
        _mix_kernel,
        grid=(N_TOK // TM,),
        in_specs=[pl.BlockSpec((TM, D), lambda i: (i, 0)),
                  pl.BlockSpec((TM // GRID_W, D // 2), lambda i: (i, 0)),
                  pl.BlockSpec((GRID_W, D // 2), lambda i: (0, 0)),
                  vec, vec, vec, vec, full(wg), full(bg),
                  pl.BlockSpec((CH, TM // CH, S5_WIDTH), lambda i: (0, i, 0)),
                  pl.BlockSpec((TM, FFT_WIDTH), lambda i: (i, 0)),
                  full(wglu), full(bglu), full(wbs), full(wbf), full(bbf), full(wo), full(bo),
                  vec, vec, vec, vec, vec, full(wr_t), full(br), full(tri), full(etri)],
        out_specs=(pl.BlockSpec((TM, D), lambda i: (i, 0)),
                   pl.BlockSpec((TM, D), lambda i: (i, 0)),
                   pl.BlockSpec((TOP_K, TM), lambda i: (0, i)),
                   pl.BlockSpec((TM, LANES), lambda i: (i, 0)),
                   pl.BlockSpec((1, N_EXPERTS, LANES), lambda i: (i, 0, 0))),
        out_shape=(jax.ShapeDtypeStruct((N_TOK, D), F32),
                   jax.ShapeDtypeStruct((N_TOK, D), BF16),
                   jax.ShapeDtypeStruct((TOP_K, N_TOK), jnp.int32),
                   jax.ShapeDtypeStruct((N_TOK, LANES), F32),
                   jax.ShapeDtypeStruct((N_TILES, N_EXPERTS, LANES), F32)),
        scratch_shapes=[pltpu.VMEM((NJ, TM, LANES), F32)],
        compiler_params=_cparams(("parallel",)),
        name="mix",
    )(x, emb_r, emb_c, lg, lb, m1, s1, wg, bg, y_t, zr, wglu, bglu, wbs, wbf, bbf, wo, bo,
      g1, l1g, l1b, m2, s2, wr_t, br, tri, etri)


def _on_parity(i, fn):
    @pl.when(i % 2 == 0)
    def _():
        fn(0)

    @pl.when(i % 2 == 1)
    def _():
        fn(1)


def _dispatch_kernel(pend_ref, padded_ref, nchk_ref, dest_ref, pos_ref, u_ref, buf_ref,
                     sorted_ref, zero_ref, zsem, sems):
    i = pl.program_id(0)

    @pl.when(i == 0)
    def _():
        zero_ref[...] = jnp.zeros_like(zero_ref)
        n_used = pend_ref[N_EXPERTS - 1] // BM

        def clear_copy(start):
            return pltpu.make_async_copy(
                zero_ref, buf_ref.at[pl.ds(pl.multiple_of(start, BM), BM)], zsem)

        def each(fn):
            def expert(e, c):
                @pl.when(padded_ref[e] > 0)
                def _():
                    fn(clear_copy(pend_ref[e] - BM))
                return c
            lax.fori_loop(0, N_EXPERTS, expert, 0)

            def tail(b, c):
                fn(clear_copy(b * BM))
                return c
            lax.fori_loop(n_used, N_BLOCKS, tail, 0)
        each(lambda cp: cp.start())
        each(lambda cp: cp.wait())

    def drain(slot, tile):
        n = pl.multiple_of(nchk_ref[tile] * SEG_ALIGN, SEG_ALIGN)
        pltpu.make_async_copy(sorted_ref.at[slot, pl.ds(0, n)], buf_ref.at[pl.ds(0, n)],
                              sems.at[slot]).wait()

    def run(slot):
        pos = pos_ref[...]
        u = u_ref[...]
        for rb in range(CAP // CAP_BLOCK):
            rows = lax.broadcasted_iota(jnp.int32, (CAP_BLOCK, TM), 0) + rb * CAP_BLOCK
            hit = rows == pos[0:1]
            for k in range(1, TOP_K):
                hit = hit | (rows == pos[k:k + 1])
            onehot = jnp.where(hit, 1.0, 0.0).astype(BF16)
            sorted_ref[slot, rb * CAP_BLOCK:(rb + 1) * CAP_BLOCK, :] = jnp.dot(
                onehot, u, preferred_element_type=F32).astype(BF16)

        def issue(j, c):
            src = pl.multiple_of(j * SEG_ALIGN, SEG_ALIGN)
            dst = pl.multiple_of(dest_ref[0, 0, j], SEG_ALIGN)
            pltpu.make_async_copy(sorted_ref.at[slot, pl.ds(src, SEG_ALIGN)],
                                  buf_ref.at[pl.ds(dst, SEG_ALIGN)], sems.at[slot]).start()
            return c
        lax.fori_loop(0, nchk_ref[i], issue, 0)

        @pl.when(i > 0)
        def _():
            drain(1 - slot, i - 1)

        @pl.when(i == N_TILES - 1)
        def _():
            drain(slot, i)
    _on_parity(i, run)


def _dispatch(pad_ends, padded, nchk, chunk_dest, pos_t, u2):
    return pl.pallas_call(
        _dispatch_kernel,
        grid_spec=pltpu.PrefetchScalarGridSpec(
            num_scalar_prefetch=3,
            grid=(N_TILES,),
            in_specs=[pl.BlockSpec((1, 1, NCHK), lambda i, a, b, c: (i, 0, 0),
                                   memory_space=pltpu.SMEM),
                      pl.BlockSpec((TOP_K, TM), lambda i, a, b, c: (0, i)),
                      pl.BlockSpec((TM, D), lambda i, a, b, c: (i, 0))],
            out_specs=pl.BlockSpec(memory_space=pl.ANY),
            scratch_shapes=[pltpu.VMEM((2, CAP, D), BF16),
                            pltpu.VMEM((BM, D), BF16),
                            pltpu.SemaphoreType.DMA(()),
                            pltpu.SemaphoreType.DMA((2,))]),
        out_shape=jax.ShapeDtypeStruct((ROWS, D), BF16),
        compiler_params=_cparams(("arbitrary",)),
        name="dispatch",
    )(pad_ends, padded, nchk, chunk_dest, pos_t, u2)


def _ffn_kernel(be_ref, nu_ref, run_ref, nxt_ref, x_ref, wu_hbm, bu_ref, wd_hbm, bd_ref, y_ref,
                wu_ref, wd_ref, wub_ref, wdb_ref, sems):
    i = pl.program_id(0)
    used = i < nu_ref[0]

    def weight_copies(e, slot):
        return (pltpu.make_async_copy(wu_hbm.at[e], wu_ref.at[slot], sems.at[slot]),
                pltpu.make_async_copy(wd_hbm.at[e], wd_ref.at[slot], sems.at[slot]))

    @pl.when(used)
    def _():
        run = run_ref[i]

        @pl.when(run >= 0)
        def _():
            def open_run(slot):
                @pl.when(run == 0)
                def _():
                    for cp in weight_copies(be_ref[i], slot):
                        cp.start()

                @pl.when(nxt_ref[i] >= 0)
                def _():
                    for cp in weight_copies(nxt_ref[i], 1 - slot):
                        cp.start()
                for cp in weight_copies(be_ref[i], slot):
                    cp.wait()
                wub_ref[...] = wu_ref[slot].astype(BF16)
                wdb_ref[...] = wd_ref[slot].astype(BF16)
            _on_parity(run, open_run)

        h = jnp.dot(x_ref[...], wub_ref[...], preferred_element_type=F32) + bu_ref[0]
        h_glu = jnp.minimum(h[:, :D], SWIGLU_LIMIT)
        h_lin = jnp.clip(h[:, D:], -SWIGLU_LIMIT, SWIGLU_LIMIT)
        act = (h_glu * _sigmoid(SWIGLU_ALPHA * h_glu) * (h_lin + 1.0)).astype(BF16)
        y_ref[...] = (jnp.dot(act, wdb_ref[...], preferred_element_type=F32) + bd_ref[0]).astype(BF16)

    @pl.when(jnp.logical_not(used))
    def _():
        y_ref[...] = jnp.zeros_like(y_ref)


def _ffn(block_expert, n_used, run_id, next_expert, buf, w_up, b_up, w_down, b_down):
    def blk(i, be, nu, run, nxt):
        return jnp.minimum(i, nu[0] - 1)
    return pl.pallas_call(
        _ffn_kernel,
        grid_spec=pltpu.PrefetchScalarGridSpec(
            num_scalar_prefetch=4,
            grid=(N_BLOCKS,),
            in_specs=[pl.BlockSpec((BM, D), lambda i, *s: (blk(i, *s), 0)),
                      pl.BlockSpec(memory_space=pl.ANY),
                      pl.BlockSpec((1, 1, 2 * D), lambda i, *s: (s[0][blk(i, *s)], 0, 0)),
                      pl.BlockSpec(memory_space=pl.ANY),
                      pl.BlockSpec((1, 1, D), lambda i, *s: (s[0][blk(i, *s)], 0, 0))],
            out_specs=pl.BlockSpec((BM, D), lambda i, *s: (i, 0)),
            scratch_shapes=[pltpu.VMEM((2, D, 2 * D), F32),
                            pltpu.VMEM((2, D, D), F32),
                            pltpu.VMEM((D, 2 * D), BF16),
                            pltpu.VMEM((D, D), BF16),
                            pltpu.SemaphoreType.DMA((2,))]),
        out_shape=jax.ShapeDtypeStruct((ROWS, D), BF16),
        compiler_params=_cparams(("arbitrary",)),
        name="ffn",
    )(block_expert, n_used, run_id, next_expert, buf, w_up, b_up, w_down, b_down)


def _combine_kernel(nchk_ref, dest_ref, dnext_ref, y_ref, h1_ref, gate_ref, g2_ref, lg_ref, lb_ref,
                    o_ref, sorted_ref, sems):
    i = pl.program_id(0)

    def fetch(slot, tile, table_ref):
        def issue(j, c):
            src = pl.multiple_of(table_ref[0, 0, j], SEG_ALIGN)
            dst = pl.multiple_of(j * SEG_ALIGN, SEG_ALIGN)
            pltpu.make_async_copy(y_ref.at[pl.ds(src, SEG_ALIGN)],
                                  sorted_ref.at[slot, pl.ds(dst, SEG_ALIGN)], sems.at[slot]).start()
            return c
        lax.fori_loop(0, nchk_ref[tile], issue, 0)

    def drain(slot, tile):
        n = pl.multiple_of(nchk_ref[tile] * SEG_ALIGN, SEG_ALIGN)
        pltpu.make_async_copy(y_ref.at[pl.ds(0, n)], sorted_ref.at[slot, pl.ds(0, n)],
                              sems.at[slot]).wait()

    @pl.when(i == 0)
    def _():
        sorted_ref[...] = jnp.zeros_like(sorted_ref)
        fetch(0, 0, dest_ref)

    def run(slot):
        @pl.when(i + 1 < N_TILES)
        def _():
            fetch(1 - slot, i + 1, dnext_ref)
        drain(slot, i)

        gp = gate_ref[...]
        m = jnp.zeros((TM, D), F32)
        for cb in range(CAP // CAP_BLOCK):
            cols = (lax.broadcasted_iota(jnp.int32, (TM, CAP_BLOCK), 1) + cb * CAP_BLOCK).astype(F32)
            g = jnp.where(cols == gp[:, TOP_K:TOP_K + 1], gp[:, 0:1], 0.0)
            for k in range(1, TOP_K):
                g = g + jnp.where(cols == gp[:, TOP_K + k:TOP_K + k + 1], gp[:, k:k + 1], 0.0)
            rows = sorted_ref[slot, cb * CAP_BLOCK:(cb + 1) * CAP_BLOCK, :]
            m = m + jnp.dot(g.astype(BF16), rows, preferred_element_type=F32)
        o_ref[...] = _layer_norm(ALPHA * h1_ref[...] + g2_ref[...] * m, lg_ref[...], lb_ref[...])
    _on_parity(i, run)


def _combine(nchk, chunk_dest, y_buf, h1, gate_tok, g2, lg, lb):
    vec = pl.BlockSpec((1, D), lambda i, n: (0, 0))
    return pl.pallas_call(
        _combine_kernel,
        grid_spec=pltpu.PrefetchScalarGridSpec(
            num_scalar_prefetch=1,
            grid=(N_TILES,),
            in_specs=[pl.BlockSpec((1, 1, NCHK), lambda i, n: (i, 0, 0), memory_space=pltpu.SMEM),
                      pl.BlockSpec((1, 1, NCHK), lambda i, n: (jnp.minimum(i + 1, N_TILES - 1), 0, 0),
                                   memory_space=pltpu.SMEM),
                      pl.BlockSpec(memory_space=pl.ANY),
                      pl.BlockSpec((TM, D), lambda i, n: (i, 0)),
                      pl.BlockSpec((TM, LANES), lambda i, n: (i, 0)),
                      vec, vec, vec],
            out_specs=pl.BlockSpec((TM, D), lambda i, n: (i, 0)),
            scratch_shapes=[pltpu.VMEM((2, CAP, D), BF16),
                            pltpu.SemaphoreType.DMA((2,))]),
        out_shape=jax.ShapeDtypeStruct((N_TOK, D), F32),
        compiler_params=_cparams(("arbitrary",)),
        name="combine",
    )(nchk, chunk_dest, chunk_dest, y_buf, h1, gate_tok, g2, lg, lb)


def _sincos_tables():
    q = D // 4
    omega = 1.0 / (10000.0 ** (jnp.arange(q, dtype=F32) / q))

    def emb(n):
        ang = jnp.arange(n, dtype=F32)[:, None] * omega[None, :]
        return jnp.concatenate([jnp.sin(ang), jnp.cos(ang)], axis=-1)
    return emb(N_TOK // GRID_W), emb(GRID_W)


def _tile_slots(a, tile):
    return jnp.transpose(a.reshape(TOP_K, N_TOK // tile, tile), (1, 0, 2)).reshape(N_TOK // tile, 1, TOP_K * tile)


def kernel(x, c, ctx, c_ctx, ln_in_g, ln_in_b, w_ada, b_ada, w_in, b_in, s5_lambda_re, s5_lambda_im, s5_log_dt, s5_b_re, s5_b_im, s5_c_re, s5_c_im, s5_d, w_glu, b_glu, w_br_s5, w_br_fft, b_br_fft, w_out, b_out, ln1_g, ln1_b, w_router, b_router, w_up, b_up, w_down, b_down, ln2_g, ln2_b):
    assert x.shape == (1, N_TOK, D) and ctx.shape == (1, N_CTX, D) and w_ada.shape[0] == 1
    row = lambda v: v.reshape(1, -1).astype(F32)

    cc = jnp.concatenate([c.reshape(1, D), c_ctx.reshape(1, D), jnp.zeros((SUBLANES - 2, D), F32)], axis=0)
    ada = _ada(cc, w_ada[0], row(b_ada[0]))
    sh1, sc1, g1, sh2, sc2, g2 = (ada[0:1, k * D:(k + 1) * D] for k in range(6))
    sh1c, sc1c = ada[1:2, 0:D], ada[1:2, D:2 * D]

    emb_r, emb_c = _sincos_tables()
    st1, st2, fc = _dft_tables()
    lg, lb = row(ln_in_g), row(ln_in_b)

    w_s5 = w_in[0][:, :S5_WIDTH]
    w_fft = w_in[0][:, S5_WIDTH:S5_WIDTH + FFT_WIDTH]
    w_g = w_in[0][:, S5_WIDTH + FFT_WIDTH:]
    b_s5 = row(b_in[0][:S5_WIDTH])
    b_fft8 = jnp.concatenate([row(b_in[0][S5_WIDTH:S5_WIDTH + FFT_WIDTH]),
                              jnp.zeros((SUBLANES - 1, FFT_WIDTH), F32)], axis=0)
    b_g = row(b_in[0][S5_WIDTH + FFT_WIDTH:])
    w_fc, b_fc = _fft_weights(w_fft, b_fft8, fc)
    wcat = jnp.concatenate([w_s5, w_fc], axis=1).astype(BF16)
    bcat = jnp.concatenate([b_s5, b_fc[0:1]], axis=1)

    x2 = x[0]
    p_t, xr, xi = _proj(x2, emb_r, emb_c, lg, lb, 1.0 + sc1, sh1, wcat, bcat)
    pc_t = _ctx_proj(ctx[0], lg, lb, 1.0 + sc1c, sh1c, w_s5.astype(BF16), b_s5)

    b_c, a_q, a_p, trans, ctx_w = _s5_tables(
        s5_lambda_re[0], s5_lambda_im[0], s5_log_dt[0], s5_b_re[0], s5_b_im[0],
        s5_c_re[0], s5_c_im[0], s5_d[0])
    w_m, w_q, w_p = _s5_expand(b_c, a_q, a_p)
    y_t = _s5(p_t, pc_t, w_m, w_q, w_p, trans, ctx_w)

    yr, yi = _fft1(xr, xi, st1)
    zr = _fft2(yr, yi, st2)

    tri = (jnp.arange(TM)[:, None] < jnp.arange(TM)[None, :]).astype(BF16)
    br = jnp.broadcast_to(b_router[0].reshape(N_EXPERTS, 1), (N_EXPERTS, LANES))
    etri = (jnp.arange(N_EXPERTS)[:, None] > jnp.arange(N_EXPERTS)[None, :]).astype(BF16)
    h1, u2, pos_t, gate_tok, counts = _mix(
        x2, emb_r, emb_c, lg, lb, 1.0 + sc1, sh1, w_g.astype(BF16), b_g, y_t, zr,
        w_glu[0].astype(BF16), row(b_glu[0]), w_br_s5[0].astype(BF16), w_br_fft[0].astype(BF16),
        row(b_br_fft[0]), w_out[0].astype(BF16), row(b_out[0]), g1, row(ln1_g[0]), row(ln1_b[0]),
        1.0 + sc2, sh2, jnp.transpose(w_router[0]), br, tri, etri)

    cnt = counts[:, :, 0].astype(jnp.int32)
    seg = (cnt + SEG_ALIGN - 1) // SEG_ALIGN * SEG_ALIGN
    seg_end = jnp.cumsum(seg, axis=1)
    seg_start = seg_end - seg
    padded = (jnp.sum(seg, axis=0) + BM - 1) // BM * BM
    pad_ends = jnp.cumsum(padded)
    seg_dest = (pad_ends - padded)[None, :] + jnp.cumsum(seg, axis=0) - seg
    chunk_row = jnp.arange(NCHK, dtype=jnp.int32) * SEG_ALIGN
    chunk_exp = jnp.minimum(jnp.sum(chunk_row[None, :, None] >= seg_end[:, None, :], axis=-1),
                            N_EXPERTS - 1)
    own = chunk_exp[:, :, None] == jnp.arange(N_EXPERTS, dtype=jnp.int32)[None, None, :]
    chunk_dest = (jnp.sum(jnp.where(own, (seg_dest - seg_start)[:, None, :], 0), axis=-1)
                  + chunk_row[None, :]).astype(jnp.int32).reshape(N_TILES, 1, NCHK)
    nchk = (seg_end[:, -1] // SEG_ALIGN).astype(jnp.int32)
    block_start = jnp.arange(N_BLOCKS, dtype=jnp.int32) * BM
    block_expert = jnp.minimum(jnp.sum(block_start[:, None] >= pad_ends[None, :], axis=1),
                               N_EXPERTS - 1).astype(jnp.int32)
    n_used = (pad_ends[-1:] // BM).astype(jnp.int32)
    opens = (block_start < pad_ends[-1]) & (
        block_expert != jnp.concatenate([jnp.full((1,), -1, jnp.int32), block_expert[:-1]]))
    run_id = jnp.where(opens, jnp.cumsum(opens.astype(jnp.int32)) - 1, -1).astype(jnp.int32)
    experts = jnp.arange(N_EXPERTS, dtype=jnp.int32)
    later = (experts[None, :] > block_expert[:, None]) & (padded[None, :] > 0)
    next_expert = jnp.min(jnp.where(later, experts[None, :], N_EXPERTS), axis=1)
    next_expert = jnp.where(next_expert < N_EXPERTS, next_expert, -1).astype(jnp.int32)

    buf = _dispatch(pad_ends.astype(jnp.int32), padded.astype(jnp.int32), nchk, chunk_dest, pos_t, u2)
    y_buf = _ffn(block_expert, n_used, run_id, next_expert, buf, w_up[0],
                 b_up[0].reshape(N_EXPERTS, 1, 2 * D), w_down[0], b_down[0].reshape(N_EXPERTS, 1, D))
    out = _combine(nchk, chunk_dest, y_buf, h1, gate_tok, g2, row(ln2_g[0]), row(ln2_b[0]))
    return out.reshape(1, N_TOK, D)
```

```python
import functools
import math

import jax
import jax.numpy as jnp
import numpy as np
from jax import lax
from jax.experimental import pallas as pl
from jax.experimental.pallas import tpu as pltpu

F32 = jnp.float32
BF16 = jnp.bfloat16
HI = lax.Precision.HIGHEST

D = 1024
N_TOK = 16384
N_CTX = 256
GRID_W = 64
S5_GROUP = 16
S5_GROUPS = 32
S5_STATE = 64
S5_WIDTH = 512
FFT_GROUPS = 4
FFT_DIM = 128
FFT_WIDTH = 512
N_EXPERTS = 32
TOP_K = 4
LN_EPS = 1e-5
ALPHA = 2.0 ** 0.25
SWIGLU_ALPHA = 1.702
SWIGLU_LIMIT = 7.0

LANES = 128
SUBLANES = 8
VMEM_LIMIT = 56 * 1024 * 1024

CH = 8
N_CHUNK = N_TOK // CH
N_CHUNK_CTX = N_CTX // CH
NSEG = SUBLANES
SEG = N_CHUNK // NSEG
PITCH = SEG + 8
SCAN_UNROLL = 4
GPT = LANES // S5_GROUP
NJ = S5_WIDTH // LANES
CL = CH * LANES
SW = 4 * GPT * S5_STATE

FN = 128
FB = 16

TM = 512
TM_PROJ = 1024
N_TILES = N_TOK // TM
MIX_GROUPS = 1
BM = 512
N_SLOTS = N_TOK * TOP_K
SEG_ALIGN = 2 * SUBLANES
CAP_BLOCK = 256
CAP = -(-(TOP_K * TM + N_EXPERTS * (SEG_ALIGN - 1)) // CAP_BLOCK) * CAP_BLOCK
NCHK = CAP // SEG_ALIGN
N_BLOCKS = -(-(N_SLOTS + N_TILES * N_EXPERTS * (SEG_ALIGN - 1)) // BM) + N_EXPERTS
ROWS = N_BLOCKS * BM


def _cparams(sem):
    return pltpu.CompilerParams(dimension_semantics=sem, vmem_limit_bytes=VMEM_LIMIT)


def _layer_norm(x, g, b):
    mu = jnp.mean(x, axis=-1, keepdims=True)
    xc = x - mu
    var = jnp.mean(xc * xc, axis=-1, keepdims=True)
    return xc * lax.rsqrt(var + LN_EPS) * g + b


def _sigmoid(x):
    return 1.0 / (1.0 + jnp.exp(-x))


def _ada_kernel(c_ref, w_ref, b_ref, o_ref):
    c = c_ref[...]
    s = c * _sigmoid(c)
    o_ref[...] = jnp.dot(s, w_ref[...], preferred_element_type=F32, precision=HI) + b_ref[...]


def _ada(cc, w_ada, b_ada):
    nb = 4
    wb = 6 * D // nb
    return pl.pallas_call(
        _ada_kernel,
        grid=(nb,),
        in_specs=[pl.BlockSpec((SUBLANES, D), lambda i: (0, 0)),
                  pl.BlockSpec((D, wb), lambda i: (0, i)),
                  pl.BlockSpec((1, wb), lambda i: (0, i))],
        out_specs=pl.BlockSpec((SUBLANES, wb), lambda i: (0, i)),
        out_shape=jax.ShapeDtypeStruct((SUBLANES, 6 * D), F32),
        compiler_params=_cparams(("parallel",)),
        name="ada",
    )(cc, w_ada, b_ada)


def _fftw_kernel(w_ref, b_ref, f_ref, wo_ref, bo_ref):
    f = f_ref[...]
    wo_ref[...] = jnp.dot(w_ref[...], f, preferred_element_type=F32, precision=HI)
    bo_ref[...] = jnp.dot(b_ref[...], f, preferred_element_type=F32, precision=HI)


def _fft_weights(w_fft, b_fft8, fc):
    return pl.pallas_call(
        _fftw_kernel,
        out_shape=(jax.ShapeDtypeStruct((D, 2 * FFT_WIDTH), F32),
                   jax.ShapeDtypeStruct((SUBLANES, 2 * FFT_WIDTH), F32)),
        compiler_params=pltpu.CompilerParams(vmem_limit_bytes=VMEM_LIMIT),
        name="fftw",
    )(w_fft, b_fft8, fc)


def _pos_code(er_ref, ec_ref, tm):
    nr = tm // GRID_W
    er = er_ref[...]
    row = jnp.broadcast_to(er[:, None, :], (nr, GRID_W, D // 2)).reshape(tm, D // 2)
    col = jnp.concatenate([ec_ref[...]] * nr, axis=0)
    return jnp.concatenate([row, col], axis=-1)


def _to_chunk_major(val, scr_ref, out_ref, tm):
    for j in range(NJ):
        scr_ref[j] = val[:, j * LANES:(j + 1) * LANES]
    for t in range(CH):
        for j in range(NJ):
            piece = scr_ref[j, pl.ds(t, tm // CH, stride=CH), :]
            out_ref[t, :, j * LANES:(j + 1) * LANES] = piece.astype(out_ref.dtype)


def _proj_kernel(x_ref, er_ref, ec_ref, lg_ref, lb_ref, m_ref, s_ref, w_ref, b_ref,
                 p_ref, xr_ref, xi_ref, scr_ref):
    x = x_ref[...] + _pos_code(er_ref, ec_ref, TM_PROJ)
    h = _layer_norm(x, lg_ref[...], lb_ref[...])
    u = (h * m_ref[...] + s_ref[...]).astype(BF16)
    p = jnp.dot(u, w_ref[...], preferred_element_type=F32) + b_ref[...]
    _to_chunk_major(p[:, :S5_WIDTH], scr_ref, p_ref, TM_PROJ)
    xr_ref[...] = p[:, S5_WIDTH:S5_WIDTH + FFT_WIDTH].astype(BF16)
    xi_ref[...] = p[:, S5_WIDTH + FFT_WIDTH:].astype(BF16)


def _proj(x, emb_r, emb_c, lg, lb, m1, s1, wcat, bcat):
    nw = wcat.shape[1]
    vec = pl.BlockSpec((1, D), lambda i: (0, 0))
    return pl.pallas_call(
        _proj_kernel,
        grid=(N_TOK // TM_PROJ,),
        in_specs=[pl.BlockSpec((TM_PROJ, D), lambda i: (i, 0)),
                  pl.BlockSpec((TM_PROJ // GRID_W, D // 2), lambda i: (i, 0)),
                  pl.BlockSpec((GRID_W, D // 2), lambda i: (0, 0)),
                  vec, vec, vec, vec,
                  pl.BlockSpec((D, nw), lambda i: (0, 0)),
                  pl.BlockSpec((1, nw), lambda i: (0, 0))],
        out_specs=(pl.BlockSpec((CH, TM_PROJ // CH, S5_WIDTH), lambda i: (0, i, 0)),
                   pl.BlockSpec((TM_PROJ, FFT_WIDTH), lambda i: (i, 0)),
                   pl.BlockSpec((TM_PROJ, FFT_WIDTH), lambda i: (i, 0))),
        out_shape=(jax.ShapeDtypeStruct((CH, N_CHUNK, S5_WIDTH), BF16),
                   jax.ShapeDtypeStruct((N_TOK, FFT_WIDTH), BF16),
                   jax.ShapeDtypeStruct((N_TOK, FFT_WIDTH), BF16)),
        scratch_shapes=[pltpu.VMEM((NJ, TM_PROJ, LANES), F32)],
        compiler_params=_cparams(("parallel",)),
        name="proj",
    )(x, emb_r, emb_c, lg, lb, m1, s1, wcat, bcat)


def _ctx_proj_kernel(x_ref, lg_ref, lb_ref, m_ref, s_ref, w_ref, b_ref, p_ref, scr_ref):
    h = _layer_norm(x_ref[...], lg_ref[...], lb_ref[...])
    u = (h * m_ref[...] + s_ref[...]).astype(BF16)
    p = jnp.dot(u, w_ref[...], preferred_element_type=F32) + b_ref[...]
    _to_chunk_major(p, scr_ref, p_ref, N_CTX)


def _ctx_proj(ctx, lg, lb, m1, s1, w_s5, b_s5):
    return pl.pallas_call(
        _ctx_proj_kernel,
        out_shape=jax.ShapeDtypeStruct((CH, N_CHUNK_CTX, S5_WIDTH), BF16),
        scratch_shapes=[pltpu.VMEM((NJ, N_CTX, LANES), F32)],
        compiler_params=pltpu.CompilerParams(vmem_limit_bytes=VMEM_LIMIT),
        name="ctxproj",
    )(ctx, lg, lb, m1, s1, w_s5, b_s5)


def _s5_tables(lam_re, lam_im, log_dt, b_re, b_im, c_re, c_im, d_skip):
    dt = jnp.exp(log_dt)[..., None]
    zr = lam_re * dt
    zi = lam_im * dt

    def apow(m):
        m = jnp.asarray(m, F32)
        mag = jnp.exp(zr[..., None] * m)
        return mag * jnp.cos(zi[..., None] * m), mag * jnp.sin(zi[..., None] * m)

    a_re, a_im = apow(jnp.ones((1,), F32))
    a_re, a_im = a_re[..., 0], a_im[..., 0]
    den = lam_re * lam_re + lam_im * lam_im
    num_re = a_re - 1.0
    k_re = (num_re * lam_re + a_im * lam_im) / den
    k_im = (a_im * lam_re - num_re * lam_im) / den
    bb_re = k_re[..., None] * b_re - k_im[..., None] * b_im
    bb_im = k_re[..., None] * b_im + k_im[..., None] * b_re

    ks = jnp.arange(CH + 1, dtype=F32)
    pw_re, pw_im = apow(ks)

    def tap(c, pw, bb):
        return jnp.einsum('dgvp,dgpk,dgph->kdghv', c, pw[..., :CH], bb, precision=HI)
    taps = (tap(c_re, pw_re, bb_re) - tap(c_re, pw_im, bb_im)
            - tap(c_im, pw_re, bb_im) - tap(c_im, pw_im, bb_re))
    skip = d_skip.reshape(S5_GROUPS, S5_GROUP, 1) * jnp.eye(S5_GROUP, dtype=F32)[None]
    taps = taps.at[0, 0].add(skip)
    b_c = jnp.transpose(taps.reshape(CH, 2, NJ, LANES, S5_GROUP), (2, 0, 1, 3, 4))
    b_c = b_c.reshape(NJ, 2 * CH, LANES, S5_GROUP)

    ef = (CH - 1) - jnp.arange(CH)
    eb = jnp.arange(CH)

    def q_part(d, e):
        pr = jnp.transpose(pw_re[d][..., e], (0, 2, 1))[:, :, None, :]
        pi = jnp.transpose(pw_im[d][..., e], (0, 2, 1))[:, :, None, :]
        br = jnp.transpose(bb_re[d], (0, 2, 1))[:, None, :, :]
        bi = jnp.transpose(bb_im[d], (0, 2, 1))[:, None, :, :]
        return pr * br - pi * bi, pr * bi + pi * br

    def q_rows(v):
        v = v.reshape(NJ, GPT, CH, S5_GROUP, S5_STATE)
        return jnp.transpose(v, (0, 2, 1, 3, 4)).reshape(NJ, CL, S5_STATE)
    a_q = jnp.stack([q_rows(v) for v in q_part(0, ef) + q_part(1, eb)], axis=0)

    of = jnp.arange(CH) + 1
    ob = CH - jnp.arange(CH)

    def p_part(d, e):
        pr = pw_re[d][..., e][:, :, :, None]
        pi = pw_im[d][..., e][:, :, :, None]
        cr = jnp.transpose(c_re[d], (0, 2, 1))[:, :, None, :]
        ci = jnp.transpose(c_im[d], (0, 2, 1))[:, :, None, :]
        return cr * pr - ci * pi, -(cr * pi + ci * pr)
    a_p = jnp.stack([v.reshape(NJ, GPT * S5_STATE, CH * S5_GROUP)
                     for v in p_part(0, of) + p_part(1, ob)], axis=0)

    def lanes(v):
        return jnp.transpose(v.reshape(2, NJ, GPT * S5_STATE), (1, 0, 2))
    c_r, c_i = apow(jnp.full((1,), float(CH), F32))
    s_r, s_i = apow(jnp.full((1,), float(CH * SEG), F32))
    cr, ci, sr, si = (lanes(v[..., 0]) for v in (c_r, c_i, s_r, s_i))
    trans = jnp.stack([cr[:, 0], ci[:, 0], cr[:, 1], ci[:, 1],
                       sr[:, 0], si[:, 0], sr[:, 1], si[:, 1]], axis=1)

    cidx = jnp.arange(N_CHUNK_CTX, dtype=F32)
    wf_r, wf_i = apow(CH * (N_CHUNK_CTX - 1 - cidx))
    wb_r, wb_i = apow(CH * cidx)

    def ctx_lanes(v, d):
        return jnp.transpose(v[d].reshape(NJ, GPT * S5_STATE, N_CHUNK_CTX), (0, 2, 1))
    ctx_w = jnp.stack([ctx_lanes(wf_r, 0), ctx_lanes(wf_i, 0),
                       ctx_lanes(wb_r, 1), ctx_lanes(wb_i, 1)], axis=1)
    return b_c, a_q, a_p, trans, ctx_w


def _s5w_kernel(bc_ref, aq_ref, ap_ref, c16_ref, c64_ref, cm_ref, wm_ref, wq_ref, wp_ref):
    def expand(a, c, row_shift, col_shift):
        w = jnp.dot(a, c, preferred_element_type=F32)
        rg = (lax.broadcasted_iota(jnp.int32, (w.shape[0], 1), 0) >> row_shift) & (GPT - 1)
        cg = (lax.broadcasted_iota(jnp.int32, (1, w.shape[1]), 1) >> col_shift) & (GPT - 1)
        return jnp.where(rg == cg, w, 0.0)

    blk = [expand(bc_ref[0, kd], c16_ref[...], 4, 4) for kd in range(2 * CH)]
    for t in range(CH):
        for u in range(CH):
            b = blk[2 * (u - t)] if u > t else blk[2 * (t - u) + 1] if u < t else blk[0] + blk[1]
            wm_ref[0, t * LANES:(t + 1) * LANES, u * LANES:(u + 1) * LANES] = b.astype(BF16)
    half = GPT * S5_STATE
    for s in range(4):
        wq_ref[0, :, s * half:(s + 1) * half] = expand(aq_ref[s, 0], c64_ref[...], 4, 6).astype(BF16)
        wp_ref[0, s * half:(s + 1) * half, :] = expand(ap_ref[s, 0], cm_ref[...], 6, 4).astype(BF16)


def _s5_expand(b_c, a_q, a_p):
    rep = np.ones((1, GPT))
    c16 = jnp.asarray(np.kron(rep, np.eye(S5_GROUP)), F32).astype(BF16)
    c64 = jnp.asarray(np.kron(rep, np.eye(S5_STATE)), F32).astype(BF16)
    c_m = jnp.asarray(np.kron(np.eye(CH), np.kron(rep, np.eye(S5_GROUP))), F32).astype(BF16)
    b_c, a_q, a_p = b_c.astype(BF16), a_q.astype(BF16), a_p.astype(BF16)
    half = GPT * S5_STATE
    return pl.pallas_call(
        _s5w_kernel,
        grid=(NJ,),
        in_specs=[pl.BlockSpec((1, 2 * CH, LANES, S5_GROUP), lambda j: (j, 0, 0, 0)),
                  pl.BlockSpec((4, 1, CL, S5_STATE), lambda j: (0, j, 0, 0)),
                  pl.BlockSpec((4, 1, half, CH * S5_GROUP), lambda j: (0, j, 0, 0)),
                  pl.BlockSpec(c16.shape, lambda j: (0, 0)),
                  pl.BlockSpec(c64.shape, lambda j: (0, 0)),
                  pl.BlockSpec(c_m.shape, lambda j: (0, 0))],
        out_specs=(pl.BlockSpec((1, CL, CL), lambda j: (j, 0, 0)),
                   pl.BlockSpec((1, CL, SW), lambda j: (j, 0, 0)),
                   pl.BlockSpec((1, SW, CL), lambda j: (j, 0, 0))),
        out_shape=(jax.ShapeDtypeStruct((NJ, CL, CL), BF16),
                   jax.ShapeDtypeStruct((NJ, CL, SW), BF16),
                   jax.ShapeDtypeStruct((NJ, SW, CL), BF16)),
        compiler_params=_cparams(("parallel",)),
        name="s5w",
    )(b_c, a_q, a_p, c16, c64, c_m)


def _s5_kernel(p_ref, pc_ref, wm_ref, wq_ref, wp_ref, tr_ref, cw_ref, y_ref, v_ref):
    nq = NJ
    half = GPT * S5_STATE

    def chunk_rows(ref, r0, nrows):
        return jnp.concatenate([ref[t, pl.ds(r0, nrows), :] for t in range(CH)], axis=-1)

    def fill(k, c):
        r0 = pl.multiple_of(k * SEG, SEG)
        v = jnp.dot(chunk_rows(p_ref, r0, SEG), wq_ref[0], preferred_element_type=F32)
        o0 = pl.multiple_of(k * PITCH, SUBLANES)
        for s in range(4 * nq):
            v_ref[s, pl.ds(o0, SEG), :] = v[:, s * LANES:(s + 1) * LANES]
        return c
    lax.fori_loop(0, NSEG, fill, 0)

    vc = jnp.dot(chunk_rows(pc_ref, 0, N_CHUNK_CTX), wq_ref[0], preferred_element_type=F32)
    vfr, vfi, vbr, vbi = (vc[:, i * half:(i + 1) * half] for i in range(4))
    wfr, wfi, wbr, wbi = (cw_ref[0, i] for i in range(4))
    s0_fr = jnp.sum(wfr * vfr - wfi * vfi, axis=0, keepdims=True)
    s0_fi = jnp.sum(wfr * vfi + wfi * vfr, axis=0, keepdims=True)
    s0_br = jnp.sum(wbr * vbr - wbi * vbi, axis=0, keepdims=True)
    s0_bi = jnp.sum(wbr * vbi + wbi * vbr, axis=0, keepdims=True)

    tr = tr_ref[0]
    afr, afi, abr, abi = (jnp.broadcast_to(tr[i:i + 1], (NSEG, half)) for i in range(4))
    gfr, gfi, gbr, gbi = (tr[i:i + 1] for i in range(4, 8))

    def load_part(part, i):
        return jnp.concatenate(
            [v_ref[part * nq + q, pl.ds(i, NSEG, stride=PITCH), :] for q in range(nq)], axis=-1)

    def store_part(part, i, val):
        for q in range(nq):
            v_ref[part * nq + q, pl.ds(i, NSEG, stride=PITCH), :] = val[:, q * LANES:(q + 1) * LANES]

    def step(i, carry, write):
        fr, fi, br, bi = carry
        ib = SEG - 1 - i
        ufr, ufi = load_part(0, i), load_part(1, i)
        ubr, ubi = load_part(2, ib), load_part(3, ib)
        if write:
            store_part(0, i, fr)
            store_part(1, i, fi)
            store_part(2, ib, br)
            store_part(3, ib, bi)
        return (afr * fr - afi * fi + ufr, afr * fi + afi * fr + ufi,
                abr * br - abi * bi + ubr, abr * bi + abi * br + ubi)

    zero = jnp.zeros((NSEG, half), F32)
    ffr, ffi, fbr, fbi = lax.fori_loop(0, SEG, functools.partial(step, write=False),
                                       (zero, zero, zero, zero), unroll=SCAN_UNROLL)

    rows_fr, rows_fi = [s0_fr], [s0_fi]
    for k in range(1, NSEG):
        pr, pi = rows_fr[-1], rows_fi[-1]
        rows_fr.append(gfr * pr - gfi * pi + ffr[k - 1:k])
        rows_fi.append(gfr * pi + gfi * pr + ffi[k - 1:k])
    rows_br, rows_bi = [s0_br], [s0_bi]
    for k in range(NSEG - 2, -1, -1):
        pr, pi = rows_br[0], rows_bi[0]
        rows_br.insert(0, gbr * pr - gbi * pi + fbr[k + 1:k + 2])
        rows_bi.insert(0, gbr * pi + gbi * pr + fbi[k + 1:k + 2])
    init = tuple(jnp.concatenate(r, axis=0) for r in (rows_fr, rows_fi, rows_br, rows_bi))

    lax.fori_loop(0, SEG, functools.partial(step, write=True), init, unroll=SCAN_UNROLL)

    def emit(k, c):
        r0 = pl.multiple_of(k * SEG, SEG)
        o0 = pl.multiple_of(k * PITCH, SUBLANES)
        b = chunk_rows(p_ref, r0, SEG)
        sin = jnp.concatenate([v_ref[s, pl.ds(o0, SEG), :] for s in range(4 * nq)], axis=-1)
        y = (jnp.dot(b, wm_ref[0], preferred_element_type=F32)
             + jnp.dot(sin.astype(BF16), wp_ref[0], preferred_element_type=F32))
        for t in range(CH):
            y_ref[t, pl.ds(r0, SEG), :] = y[:, t * LANES:(t + 1) * LANES].astype(y_ref.dtype)
        return c
    lax.fori_loop(0, NSEG, emit, 0)


def _s5(p_t, pc_t, w_m, w_q, w_p, trans, ctx_w):
    one = pl.Buffered(1)
    return pl.pallas_call(
        _s5_kernel,
        grid=(NJ,),
        in_specs=[pl.BlockSpec((CH, N_CHUNK, LANES), lambda j: (0, 0, j)),
                  pl.BlockSpec((CH, N_CHUNK_CTX, LANES), lambda j: (0, 0, j)),
                  pl.BlockSpec((1, CL, CL), lambda j: (j, 0, 0), pipeline_mode=one),
                  pl.BlockSpec((1, CL, SW), lambda j: (j, 0, 0), pipeline_mode=one),
                  pl.BlockSpec((1, SW, CL), lambda j: (j, 0, 0), pipeline_mode=one),
                  pl.BlockSpec((1, SUBLANES, GPT * S5_STATE), lambda j: (j, 0, 0)),
                  pl.BlockSpec((1, 4, N_CHUNK_CTX, GPT * S5_STATE), lambda j: (j, 0, 0, 0))],
        out_specs=pl.BlockSpec((CH, N_CHUNK, LANES), lambda j: (0, 0, j)),
        out_shape=jax.ShapeDtypeStruct((CH, N_CHUNK, S5_WIDTH), BF16),
        scratch_shapes=[pltpu.VMEM((4 * NJ, NSEG * PITCH, LANES), F32)],
        compiler_params=_cparams(("parallel",)),
        name="s5",
    )(p_t, pc_t, w_m, w_q, w_p, trans, ctx_w)


def _dft_tables():
    n = np.arange(FN)
    ang = 2.0 * np.pi * np.outer(n, n) / FN
    c, s = np.cos(ang), np.sin(ang)
    st1 = np.block([[c, s], [-s, c]])
    tw = 2.0 * np.pi * np.outer(n, n) / (FN * FN)
    wr, wi = np.cos(tw), -np.sin(tw)
    fr = c[None] * wr[:, None, :] + s[None] * wi[:, None, :]
    fi = c[None] * wi[:, None, :] - s[None] * wr[:, None, :]
    st2 = np.concatenate([fr, -fi], axis=-1)
    scale = 1.0 / math.sqrt(N_TOK * FFT_DIM)
    blk_c = np.kron(np.eye(FFT_GROUPS), c) * scale
    blk_s = np.kron(np.eye(FFT_GROUPS), s) * scale
    fc = np.concatenate([blk_c, -blk_s], axis=1)
    return (jnp.asarray(st1, F32).astype(BF16), jnp.asarray(st2, F32).astype(BF16), jnp.asarray(fc, F32))


FSL = FFT_WIDTH // LANES


def _rows_to_slabs(val, slab_ref, first):
    for s in range(FSL):
        slab_ref[first + s] = val[:, s * LANES:(s + 1) * LANES]


def _slabs_to_block(slab_ref, first):
    val = jnp.concatenate([slab_ref[first + s] for s in range(FSL)], axis=-1)
    return val.reshape(FN, FB, FFT_WIDTH)


def _fft1_kernel(xr_ref, xi_ref, f_ref, yr_ref, yi_ref, in_ref, out_ref):
    _rows_to_slabs(xr_ref[...].astype(F32).reshape(FN * FB, FFT_WIDTH), in_ref, 0)
    _rows_to_slabs(xi_ref[...].astype(F32).reshape(FN * FB, FFT_WIDTH), in_ref, FSL)
    for b in range(FB):
        def part(first):
            return jnp.concatenate([in_ref[first + s, pl.ds(b, FN, stride=FB), :] for s in range(FSL)],
                                   axis=-1)
        xs = jnp.concatenate([part(0), part(FSL)], axis=0).astype(BF16)
        y = jnp.dot(f_ref[...], xs, preferred_element_type=F32)
        for s in range(FSL):
            out_ref[s, pl.ds(b, FN, stride=FB), :] = y[:FN, s * LANES:(s + 1) * LANES]
            out_ref[FSL + s, pl.ds(b, FN, stride=FB), :] = y[FN:, s * LANES:(s + 1) * LANES]
    yr_ref[...] = _slabs_to_block(out_ref, 0).astype(BF16)
    yi_ref[...] = _slabs_to_block(out_ref, FSL).astype(BF16)


def _fft1(xr, xi, st1):
    spec = pl.BlockSpec((FN, FB, FFT_WIDTH), lambda i: (0, i, 0))
    slabs = pltpu.VMEM((2 * FSL, FN * FB, LANES), F32)
    return pl.pallas_call(
        _fft1_kernel,
        grid=(FN // FB,),
        in_specs=[spec, spec, pl.BlockSpec((2 * FN, 2 * FN), lambda i: (0, 0))],
        out_specs=(spec, spec),
        out_shape=(jax.ShapeDtypeStruct((FN, FN, FFT_WIDTH), BF16),) * 2,
        scratch_shapes=[slabs, slabs],
        compiler_params=_cparams(("parallel",)),
        name="fft1",
    )(xr.reshape(FN, FN, FFT_WIDTH), xi.reshape(FN, FN, FFT_WIDTH), st1)


def _fft2_kernel(yr_ref, yi_ref, f_ref, z_ref, out_ref):
    for b in range(FB):
        ys = jnp.concatenate([yr_ref[b * FN:(b + 1) * FN, :], yi_ref[b * FN:(b + 1) * FN, :]], axis=0)
        z = jnp.dot(f_ref[b], ys, preferred_element_type=F32)
        for s in range(FSL):
            out_ref[s, pl.ds(b, FN, stride=FB), :] = z[:, s * LANES:(s + 1) * LANES]
    z_ref[...] = _slabs_to_block(out_ref, 0).astype(BF16)


def _fft2(yr, yi, st2):
    rows = pl.BlockSpec((FB * FN, FFT_WIDTH), lambda i: (i, 0))
    z = pl.pallas_call(
        _fft2_kernel,
        grid=(FN // FB,),
        in_specs=[rows, rows, pl.BlockSpec((FB, FN, 2 * FN), lambda i: (i, 0, 0))],
        out_specs=pl.BlockSpec((FN, FB, FFT_WIDTH), lambda i: (0, i, 0)),
        out_shape=jax.ShapeDtypeStruct((FN, FN, FFT_WIDTH), BF16),
        scratch_shapes=[pltpu.VMEM((FSL, FN * FB, LANES), F32)],
        compiler_params=_cparams(("parallel",)),
        name="fft2",
    )(yr.reshape(N_TOK, FFT_WIDTH), yi.reshape(N_TOK, FFT_WIDTH), st2)
    return z.reshape(N_TOK, FFT_WIDTH)


def _gelu_tanh(x):
    return 0.5 * x * (1.0 + jnp.tanh(math.sqrt(2.0 / math.pi) * (x + 0.044715 * (x * x * x))))


def _mix_kernel(x_ref, er_ref, ec_ref, lg_ref, lb_ref, m1_ref, s1_ref, wg_ref, bg_ref,
                yt_ref, zr_ref, wglu_ref, bglu_ref, wbs_ref, wbf_ref, bbf_ref, wo_ref, bo_ref,
                g1_ref, l1g_ref, l1b_ref, m2_ref, s2_ref, wr_ref, br_ref, tri_ref, etri_ref,
                h1_ref, u2_ref, pos_ref, gate_ref, cnt_ref, scr_ref):
    pos = _pos_code(er_ref, ec_ref, TM)

    def front(r0, nr):
        rows = slice(r0, r0 + nr)
        h = _layer_norm(x_ref[rows, :] + pos[rows, :], lg_ref[...], lb_ref[...])
        u = (h * m1_ref[...] + s1_ref[...]).astype(BF16)

        c0, nc = r0 // CH, nr // CH
        for t in range(CH):
            for j in range(NJ):
                scr_ref[j, pl.ds(r0 + t, nc, stride=CH), :] = (
                    yt_ref[t, c0:c0 + nc, j * LANES:(j + 1) * LANES].astype(F32))
        ys = jnp.concatenate([scr_ref[j, rows, :] for j in range(NJ)], axis=-1)
        z = jnp.dot(_gelu_tanh(ys).astype(BF16), wglu_ref[...], preferred_element_type=F32) + bglu_ref[...]
        glu = (z[:, :S5_WIDTH] * _sigmoid(z[:, S5_WIDTH:])).astype(BF16)
        gates = _sigmoid(jnp.dot(u, wg_ref[...], preferred_element_type=F32) + bg_ref[...])
        y_s5 = jnp.dot(glu, wbs_ref[...], preferred_element_type=F32)
        y_fft = jnp.dot(zr_ref[rows, :], wbf_ref[...], preferred_element_type=F32) + bbf_ref[...]
        mixed = (gates[:, :D] * y_s5 + gates[:, D:] * y_fft).astype(BF16)
        y = jnp.dot(mixed, wo_ref[...], preferred_element_type=F32) + bo_ref[...]
        h1 = _layer_norm(ALPHA * h + g1_ref[...] * y, l1g_ref[...], l1b_ref[...])
        h1_ref[rows, :] = h1
        u2 = h1 * m2_ref[...] + s2_ref[...]
        u2_ref[rows, :] = u2.astype(BF16)
        u_hi = u2.astype(BF16)
        u_lo = (u2 - u_hi.astype(F32)).astype(BF16)

        def nt(a, b):
            return lax.dot_general(a, b, (((1,), (1,)), ((), ())), preferred_element_type=F32)
        return nt(wr_ref[0], u_hi) + nt(wr_ref[0], u_lo) + nt(wr_ref[1], u_hi)

    nr = TM // MIX_GROUPS
    logits = jnp.concatenate([front(g * nr, nr) for g in range(MIX_GROUPS)], axis=-1) + br_ref[:, 0:1]
    eidx = lax.broadcasted_iota(jnp.int32, (N_EXPERTS, TM), 0)
    vals, hots = [], []
    cur = logits
    for _k in range(TOP_K):
        m = jnp.max(cur, axis=0, keepdims=True)
        sel = jnp.min(jnp.where(cur == m, eidx, N_EXPERTS), axis=0, keepdims=True)
        hot = eidx == sel
        cur = jnp.where(hot, -jnp.inf, cur)
        vals.append(m)
        hots.append(hot)
    exps = [jnp.exp(v - vals[0]) for v in vals]
    den = exps[0] + exps[1] + exps[2] + exps[3]
    gate4 = jnp.concatenate([e / den for e in exps], axis=0)

    hot_sum = (hots[0] | hots[1] | hots[2] | hots[3]).astype(F32)
    before = jnp.dot(hot_sum.astype(BF16), tri_ref[...], preferred_element_type=F32)
    cnt = jnp.broadcast_to(jnp.sum(hot_sum, axis=1, keepdims=True), (N_EXPERTS, LANES))
    cnt8 = jnp.floor((cnt + (SEG_ALIGN - 1)) * (1.0 / SEG_ALIGN)) * SEG_ALIGN
    seg0 = jnp.dot(etri_ref[...], cnt8.astype(BF16), preferred_element_type=F32)
    tot = seg0[:, 0:1] + before
    pos4 = jnp.concatenate(
        [jnp.sum(jnp.where(hk, tot, 0.0), axis=0, keepdims=True) for hk in hots], axis=0)
    pos_ref[...] = pos4.astype(jnp.int32)
    cnt_ref[0] = cnt

    gpad = jnp.concatenate([gate4, pos4, jnp.zeros((LANES - 2 * TOP_K, TM), F32)], axis=0)
    gate_ref[...] = gpad.T


def _mix(x, emb_r, emb_c, lg, lb, m1, s1, wg, bg, y_t, zr, wglu, bglu, wbs, wbf, bbf, wo, bo,
         g1, l1g, l1b, m2, s2, wr_t, br, tri, etri):
    vec = pl.BlockSpec((1, D), lambda i: (0, 0))

    def full(a):
        return pl.BlockSpec(a.shape, lambda i: (0,) * a.ndim)
    return pl.pallas_call(
        _mix_kernel,
        grid=(N_TOK // TM,),
        in_specs=[pl.BlockSpec((TM, D), lambda i: (i, 0)),
                  pl.BlockSpec((TM // GRID_W, D // 2), lambda i: (i, 0)),
                  pl.BlockSpec((GRID_W, D // 2), lambda i: (0, 0)),
                  vec, vec, vec, vec, full(wg), full(bg),
                  pl.BlockSpec((CH, TM // CH, S5_WIDTH), lambda i: (0, i, 0)),
                  pl.BlockSpec((TM, FFT_WIDTH), lambda i: (i, 0)),
                  full(wglu), full(bglu), full(wbs), full(wbf), full(bbf), full(wo), full(bo),
                  vec, vec, vec, vec, vec, full(wr_t), full(br), full(tri), full(etri)],
        out_specs=(pl.BlockSpec((TM, D), lambda i: (i, 0)),
                   pl.BlockSpec((TM, D), lambda i: (i, 0)),
                   pl.BlockSpec((TOP_K, TM), lambda i: (0, i)),
                   pl.BlockSpec((TM, LANES), lambda i: (i, 0)),
                   pl.BlockSpec((1, N_EXPERTS, LANES), lambda i: (i, 0, 0))),
        out_shape=(jax.ShapeDtypeStruct((N_TOK, D), F32),
                   jax.ShapeDtypeStruct((N_TOK, D), BF16),
                   jax.ShapeDtypeStruct((TOP_K, N_TOK), jnp.int32),
                   jax.ShapeDtypeStruct((N_TOK, LANES), F32),
                   jax.ShapeDtypeStruct((N_TILES, N_EXPERTS, LANES), F32)),
        scratch_shapes=[pltpu.VMEM((NJ, TM, LANES), F32)],
        compiler_params=_cparams(("parallel",)),
        name="mix",
    )(x, emb_r, emb_c, lg, lb, m1, s1, wg, bg, y_t, zr, wglu, bglu, wbs, wbf, bbf, wo, bo,
      g1, l1g, l1b, m2, s2, wr_t, br, tri, etri)


def _on_parity(i, fn):
    @pl.when(i % 2 == 0)
    def _():
        fn(0)

    @pl.when(i % 2 == 1)
    def _():
        fn(1)


def _dispatch_kernel(pend_ref, padded_ref, nchk_ref, dest_ref, pos_ref, u_ref, buf_ref,
                     sorted_ref, zero_ref, zsem, sems):
    i = pl.program_id(0)

    @pl.when(i == 0)
    def _():
        zero_ref[...] = jnp.zeros_like(zero_ref)
        n_used = pend_ref[N_EXPERTS - 1] // BM

        def clear_copy(start):
            return pltpu.make_async_copy(
                zero_ref, buf_ref.at[pl.ds(pl.multiple_of(start, BM), BM)], zsem)

        def each(fn):
            def expert(e, c):
                @pl.when(padded_ref[e] > 0)
                def _():
                    fn(clear_copy(pend_ref[e] - BM))
                return c
            lax.fori_loop(0, N_EXPERTS, expert, 0)

            def tail(b, c):
                fn(clear_copy(b * BM))
                return c
            lax.fori_loop(n_used, N_BLOCKS, tail, 0)
        each(lambda cp: cp.start())
        each(lambda cp: cp.wait())

    def drain(slot, tile):
        n = pl.multiple_of(nchk_ref[tile] * SEG_ALIGN, SEG_ALIGN)
        pltpu.make_async_copy(sorted_ref.at[slot, pl.ds(0, n)], buf_ref.at[pl.ds(0, n)],
                              sems.at[slot]).wait()

    def run(slot):
        n_rows = nchk_ref[i] * SEG_ALIGN

        def sort_block(rb):
            pos = pos_ref[...]
            rows = lax.broadcasted_iota(jnp.int32, (CAP_BLOCK, TM), 0) + rb * CAP_BLOCK
            hit = rows == pos[0:1]
            for k in range(1, TOP_K):
                hit = hit | (rows == pos[k:k + 1])
            onehot = jnp.where(hit, 1.0, 0.0).astype(BF16)
            sorted_ref[slot, rb * CAP_BLOCK:(rb + 1) * CAP_BLOCK, :] = jnp.dot(
                onehot, u_ref[...], preferred_element_type=F32).astype(BF16)
        for rb in range(CAP // CAP_BLOCK):
            if (rb + 1) * CAP_BLOCK <= TOP_K * TM:
                sort_block(rb)
            else:
                pl.when(rb * CAP_BLOCK < n_rows)(functools.partial(sort_block, rb))

        def issue(j, c):
            src = pl.multiple_of(j * SEG_ALIGN, SEG_ALIGN)
            dst = pl.multiple_of(dest_ref[0, 0, j], SEG_ALIGN)
            pltpu.make_async_copy(sorted_ref.at[slot, pl.ds(src, SEG_ALIGN)],
                                  buf_ref.at[pl.ds(dst, SEG_ALIGN)], sems.at[slot]).start()
            return c
        lax.fori_loop(0, nchk_ref[i], issue, 0)

        @pl.when(i > 0)
        def _():
            drain(1 - slot, i - 1)

        @pl.when(i == N_TILES - 1)
        def _():
            drain(slot, i)
    _on_parity(i, run)


def _dispatch(pad_ends, padded, nchk, chunk_dest, pos_t, u2):
    return pl.pallas_call(
        _dispatch_kernel,
        grid_spec=pltpu.PrefetchScalarGridSpec(
            num_scalar_prefetch=3,
            grid=(N_TILES,),
            in_specs=[pl.BlockSpec((1, 1, NCHK), lambda i, a, b, c: (i, 0, 0),
                                   memory_space=pltpu.SMEM),
                      pl.BlockSpec((TOP_K, TM), lambda i, a, b, c: (0, i)),
                      pl.BlockSpec((TM, D), lambda i, a, b, c: (i, 0))],
            out_specs=pl.BlockSpec(memory_space=pl.ANY),
            scratch_shapes=[pltpu.VMEM((2, CAP, D), BF16),
                            pltpu.VMEM((BM, D), BF16),
                            pltpu.SemaphoreType.DMA(()),
                            pltpu.SemaphoreType.DMA((2,))]),
        out_shape=jax.ShapeDtypeStruct((ROWS, D), BF16),
        compiler_params=_cparams(("arbitrary",)),
        name="dispatch",
    )(pad_ends, padded, nchk, chunk_dest, pos_t, u2)


def _ffn_kernel(be_ref, nu_ref, run_ref, nxt_ref, x_ref, wu_hbm, bu_ref, wd_hbm, bd_ref, y_ref,
                wu_ref, wd_ref, wub_ref, wdb_ref, sems):
    i = pl.program_id(0)
    used = i < nu_ref[0]

    def weight_copies(e, slot):
        return (pltpu.make_async_copy(wu_hbm.at[e], wu_ref.at[slot], sems.at[slot]),
                pltpu.make_async_copy(wd_hbm.at[e], wd_ref.at[slot], sems.at[slot]))

    @pl.when(used)
    def _():
        run = run_ref[i]

        @pl.when(run >= 0)
        def _():
            def open_run(slot):
                @pl.when(run == 0)
                def _():
                    for cp in weight_copies(be_ref[i], slot):
                        cp.start()

                @pl.when(nxt_ref[i] >= 0)
                def _():
                    for cp in weight_copies(nxt_ref[i], 1 - slot):
                        cp.start()
                for cp in weight_copies(be_ref[i], slot):
                    cp.wait()
                wub_ref[...] = wu_ref[slot].astype(BF16)
                wdb_ref[...] = wd_ref[slot].astype(BF16)
            _on_parity(run, open_run)

        h = jnp.dot(x_ref[...], wub_ref[...], preferred_element_type=F32) + bu_ref[0]
        h_glu = jnp.minimum(h[:, :D], SWIGLU_LIMIT)
        h_lin = jnp.clip(h[:, D:], -SWIGLU_LIMIT, SWIGLU_LIMIT)
        act = (h_glu * _sigmoid(SWIGLU_ALPHA * h_glu) * (h_lin + 1.0)).astype(BF16)
        y_ref[...] = (jnp.dot(act, wdb_ref[...], preferred_element_type=F32) + bd_ref[0]).astype(BF16)

    @pl.when(jnp.logical_not(used))
    def _():
        y_ref[...] = jnp.zeros_like(y_ref)


def _ffn(block_expert, n_used, run_id, next_expert, buf, w_up, b_up, w_down, b_down):
    def blk(i, be, nu, run, nxt):
        return jnp.minimum(i, nu[0] - 1)
    return pl.pallas_call(
        _ffn_kernel,
        grid_spec=pltpu.PrefetchScalarGridSpec(
            num_scalar_prefetch=4,
            grid=(N_BLOCKS,),
            in_specs=[pl.BlockSpec((BM, D), lambda i, *s: (blk(i, *s), 0)),
                      pl.BlockSpec(memory_space=pl.ANY),
                      pl.BlockSpec((1, 1, 2 * D), lambda i, *s: (s[0][blk(i, *s)], 0, 0)),
                      pl.BlockSpec(memory_space=pl.ANY),
                      pl.BlockSpec((1, 1, D), lambda i, *s: (s[0][blk(i, *s)], 0, 0))],
            out_specs=pl.BlockSpec((BM, D), lambda i, *s: (i, 0)),
            scratch_shapes=[pltpu.VMEM((2, D, 2 * D), F32),
                            pltpu.VMEM((2, D, D), F32),
                            pltpu.VMEM((D, 2 * D), BF16),
                            pltpu.VMEM((D, D), BF16),
                            pltpu.SemaphoreType.DMA((2,))]),
        out_shape=jax.ShapeDtypeStruct((ROWS, D), BF16),
        compiler_params=_cparams(("arbitrary",)),
        name="ffn",
    )(block_expert, n_used, run_id, next_expert, buf, w_up, b_up, w_down, b_down)


def _combine_kernel(nchk_ref, dest_ref, dnext_ref, y_ref, h1_ref, gate_ref, g2_ref, lg_ref, lb_ref,
                    o_ref, sorted_ref, acc_ref, sems):
    i = pl.program_id(0)

    def fetch(slot, tile, table_ref):
        def issue(j, c):
            src = pl.multiple_of(table_ref[0, 0, j], SEG_ALIGN)
            dst = pl.multiple_of(j * SEG_ALIGN, SEG_ALIGN)
            pltpu.make_async_copy(y_ref.at[pl.ds(src, SEG_ALIGN)],
                                  sorted_ref.at[slot, pl.ds(dst, SEG_ALIGN)], sems.at[slot]).start()
            return c
        lax.fori_loop(0, nchk_ref[tile], issue, 0)

    def drain(slot, tile):
        n = pl.multiple_of(nchk_ref[tile] * SEG_ALIGN, SEG_ALIGN)
        pltpu.make_async_copy(y_ref.at[pl.ds(0, n)], sorted_ref.at[slot, pl.ds(0, n)],
                              sems.at[slot]).wait()

    @pl.when(i == 0)
    def _():
        sorted_ref[...] = jnp.zeros_like(sorted_ref)
        fetch(0, 0, dest_ref)

    def run(slot):
        @pl.when(i + 1 < N_TILES)
        def _():
            fetch(1 - slot, i + 1, dnext_ref)
        drain(slot, i)

        n_rows = nchk_ref[i] * SEG_ALIGN

        def part(cb):
            gp = gate_ref[...]
            cols = (lax.broadcasted_iota(jnp.int32, (TM, CAP_BLOCK), 1) + cb * CAP_BLOCK).astype(F32)
            g = jnp.where(cols == gp[:, TOP_K:TOP_K + 1], gp[:, 0:1], 0.0)
            for k in range(1, TOP_K):
                g = g + jnp.where(cols == gp[:, TOP_K + k:TOP_K + k + 1], gp[:, k:k + 1], 0.0)
            rows = sorted_ref[slot, cb * CAP_BLOCK:(cb + 1) * CAP_BLOCK, :]
            return jnp.dot(g.astype(BF16), rows, preferred_element_type=F32)

        always = (TOP_K * TM) // CAP_BLOCK
        m = part(0)
        for cb in range(1, always):
            m = m + part(cb)
        acc_ref[...] = m
        for cb in range(always, CAP // CAP_BLOCK):
            @pl.when(cb * CAP_BLOCK < n_rows)
            def _(cb=cb):
                acc_ref[...] += part(cb)
        o_ref[...] = _layer_norm(ALPHA * h1_ref[...] + g2_ref[...] * acc_ref[...], lg_ref[...], lb_ref[...])
    _on_parity(i, run)


def _combine(nchk, chunk_dest, y_buf, h1, gate_tok, g2, lg, lb):
    vec = pl.BlockSpec((1, D), lambda i, n: (0, 0))
    return pl.pallas_call(
        _combine_kernel,
        grid_spec=pltpu.PrefetchScalarGridSpec(
            num_scalar_prefetch=1,
            grid=(N_TILES,),
            in_specs=[pl.BlockSpec((1, 1, NCHK), lambda i, n: (i, 0, 0), memory_space=pltpu.SMEM),
                      pl.BlockSpec((1, 1, NCHK), lambda i, n: (jnp.minimum(i + 1, N_TILES - 1), 0, 0),
                                   memory_space=pltpu.SMEM),
                      pl.BlockSpec(memory_space=pl.ANY),
                      pl.BlockSpec((TM, D), lambda i, n: (i, 0)),
                      pl.BlockSpec((TM, LANES), lambda i, n: (i, 0)),
                      vec, vec, vec],
            out_specs=pl.BlockSpec((TM, D), lambda i, n: (i, 0)),
            scratch_shapes=[pltpu.VMEM((2, CAP, D), BF16),
                            pltpu.VMEM((TM, D), F32),
                            pltpu.SemaphoreType.DMA((2,))]),
        out_shape=jax.ShapeDtypeStruct((N_TOK, D), F32),
        compiler_params=_cparams(("arbitrary",)),
        name="combine",
    )(nchk, chunk_dest, chunk_dest, y_buf, h1, gate_tok, g2, lg, lb)


def _sincos_tables():
    q = D // 4
    omega = 1.0 / (10000.0 ** (jnp.arange(q, dtype=F32) / q))

    def emb(n):
        ang = jnp.arange(n, dtype=F32)[:, None] * omega[None, :]
        return jnp.concatenate([jnp.sin(ang), jnp.cos(ang)], axis=-1)
    return emb(N_TOK // GRID_W), emb(GRID_W)


def _tile_slots(a, tile):
    return jnp.transpose(a.reshape(TOP_K, N_TOK // tile, tile), (1, 0, 2)).reshape(N_TOK // tile, 1, TOP_K * tile)


def kernel(x, c, ctx, c_ctx, ln_in_g, ln_in_b, w_ada, b_ada, w_in, b_in, s5_lambda_re, s5_lambda_im, s5_log_dt, s5_b_re, s5_b_im, s5_c_re, s5_c_im, s5_d, w_glu, b_glu, w_br_s5, w_br_fft, b_br_fft, w_out, b_out, ln1_g, ln1_b, w_router, b_router, w_up, b_up, w_down, b_down, ln2_g, ln2_b):
    assert x.shape == (1, N_TOK, D) and ctx.shape == (1, N_CTX, D) and w_ada.shape[0] == 1
    row = lambda v: v.reshape(1, -1).astype(F32)

    cc = jnp.concatenate([c.reshape(1, D), c_ctx.reshape(1, D), jnp.zeros((SUBLANES - 2, D), F32)], axis=0)
    ada = _ada(cc, w_ada[0], row(b_ada[0]))
    sh1, sc1, g1, sh2, sc2, g2 = (ada[0:1, k * D:(k + 1) * D] for k in range(6))
    sh1c, sc1c = ada[1:2, 0:D], ada[1:2, D:2 * D]

    emb_r, emb_c = _sincos_tables()
    st1, st2, fc = _dft_tables()
    lg, lb = row(ln_in_g), row(ln_in_b)

    w_s5 = w_in[0][:, :S5_WIDTH]
    w_fft = w_in[0][:, S5_WIDTH:S5_WIDTH + FFT_WIDTH]
    w_g = w_in[0][:, S5_WIDTH + FFT_WIDTH:]
    b_s5 = row(b_in[0][:S5_WIDTH])
    b_fft8 = jnp.concatenate([row(b_in[0][S5_WIDTH:S5_WIDTH + FFT_WIDTH]),
                              jnp.zeros((SUBLANES - 1, FFT_WIDTH), F32)], axis=0)
    b_g = row(b_in[0][S5_WIDTH + FFT_WIDTH:])
    w_fc, b_fc = _fft_weights(w_fft, b_fft8, fc)
    wcat = jnp.concatenate([w_s5, w_fc], axis=1).astype(BF16)
    bcat = jnp.concatenate([b_s5, b_fc[0:1]], axis=1)

    x2 = x[0]
    p_t, xr, xi = _proj(x2, emb_r, emb_c, lg, lb, 1.0 + sc1, sh1, wcat, bcat)
    pc_t = _ctx_proj(ctx[0], lg, lb, 1.0 + sc1c, sh1c, w_s5.astype(BF16), b_s5)

    b_c, a_q, a_p, trans, ctx_w = _s5_tables(
        s5_lambda_re[0], s5_lambda_im[0], s5_log_dt[0], s5_b_re[0], s5_b_im[0],
        s5_c_re[0], s5_c_im[0], s5_d[0])
    w_m, w_q, w_p = _s5_expand(b_c, a_q, a_p)
    y_t = _s5(p_t, pc_t, w_m, w_q, w_p, trans, ctx_w)

    yr, yi = _fft1(xr, xi, st1)
    zr = _fft2(yr, yi, st2)

    tri = (jnp.arange(TM)[:, None] < jnp.arange(TM)[None, :]).astype(BF16)
    br = jnp.broadcast_to(b_router[0].reshape(N_EXPERTS, 1), (N_EXPERTS, LANES))
    etri = (jnp.arange(N_EXPERTS)[:, None] > jnp.arange(N_EXPERTS)[None, :]).astype(BF16)
    wr_t = jnp.transpose(w_router[0])
    wr_hi = wr_t.astype(BF16)
    wr_split = jnp.stack([wr_hi, (wr_t - wr_hi.astype(F32)).astype(BF16)], axis=0)
    h1, u2, pos_t, gate_tok, counts = _mix(
        x2, emb_r, emb_c, lg, lb, 1.0 + sc1, sh1, w_g.astype(BF16), b_g, y_t, zr,
        w_glu[0].astype(BF16), row(b_glu[0]), w_br_s5[0].astype(BF16), w_br_fft[0].astype(BF16),
        row(b_br_fft[0]), w_out[0].astype(BF16), row(b_out[0]), g1, row(ln1_g[0]), row(ln1_b[0]),
        1.0 + sc2, sh2, wr_split, br, tri, etri)

    cnt = counts[:, :, 0].astype(jnp.int32)
    seg = (cnt + SEG_ALIGN - 1) // SEG_ALIGN * SEG_ALIGN
    seg_end = jnp.cumsum(seg, axis=1)
    seg_start = seg_end - seg
    padded = (jnp.sum(seg, axis=0) + BM - 1) // BM * BM
    pad_ends = jnp.cumsum(padded)
    seg_dest = (pad_ends - padded)[None, :] + jnp.cumsum(seg, axis=0) - seg
    chunk_row = jnp.arange(NCHK, dtype=jnp.int32) * SEG_ALIGN
    chunk_exp = jnp.minimum(jnp.sum(chunk_row[None, :, None] >= seg_end[:, None, :], axis=-1),
                            N_EXPERTS - 1)
    own = chunk_exp[:, :, None] == jnp.arange(N_EXPERTS, dtype=jnp.int32)[None, None, :]
    chunk_dest = (jnp.sum(jnp.where(own, (seg_dest - seg_start)[:, None, :], 0), axis=-1)
                  + chunk_row[None, :]).astype(jnp.int32).reshape(N_TILES, 1, NCHK)
    nchk = (seg_end[:, -1] // SEG_ALIGN).astype(jnp.int32)
    block_start = jnp.arange(N_BLOCKS, dtype=jnp.int32) * BM
    block_expert = jnp.minimum(jnp.sum(block_start[:, None] >= pad_ends[None, :], axis=1),
                               N_EXPERTS - 1).astype(jnp.int32)
    n_used = (pad_ends[-1:] // BM).astype(jnp.int32)
    opens = (block_start < pad_ends[-1]) & (
        block_expert != jnp.concatenate([jnp.full((1,), -1, jnp.int32), block_expert[:-1]]))
    run_id = jnp.where(opens, jnp.cumsum(opens.astype(jnp.int32)) - 1, -1).astype(jnp.int32)
    experts = jnp.arange(N_EXPERTS, dtype=jnp.int32)
    later = (experts[None, :] > block_expert[:, None]) & (padded[None, :] > 0)
    next_expert = jnp.min(jnp.where(later, experts[None, :], N_EXPERTS), axis=1)
    next_expert = jnp.where(next_expert < N_EXPERTS, next_expert, -1).astype(jnp.int32)

    buf = _dispatch(pad_ends.astype(jnp.int32), padded.astype(jnp.int32), nchk, chunk_dest, pos_t, u2)
    y_buf = _ffn(block_expert, n_used, run_id, next_expert, buf, w_up[0],
                 b_up[0].reshape(N_EXPERTS, 1, 2 * D), w_down[0], b_down[0].reshape(N_EXPERTS, 1, D))
    out = _combine(nchk, chunk_dest, y_buf, h1, gate_tok, g2, row(ln2_g[0]), row(ln2_b[0]))
    return out.reshape(1, N_TOK, D)
```

```python
import functools
import math

import jax
import jax.numpy as jnp
import numpy as np
from jax import lax
from jax.experimental import pallas as pl
from jax.experimental.pallas import tpu as pltpu

F32 = jnp.float32
BF16 = jnp.bfloat16
HI = lax.Precision.HIGHEST

D = 1024
N_TOK = 16384
N_CTX = 256
GRID_W = 64
S5_GROUP = 16
S5_GROUPS = 32
S5_STATE = 64
S5_WIDTH = 512
FFT_GROUPS = 4
FFT_DIM = 128
FFT_WIDTH = 512
N_EXPERTS = 32
TOP_K = 4
LN_EPS = 1e-5
ALPHA = 2.0 ** 0.25
SWIGLU_ALPHA = 1.702
SWIGLU_LIMIT = 7.0

LANES = 128
SUBLANES = 8
VMEM_LIMIT = 56 * 1024 * 1024

CH = 8
N_CHUNK = N_TOK // CH
N_CHUNK_CTX = N_CTX // CH
NSEG = SUBLANES
SEG = N_CHUNK // NSEG
SCAN_UNROLL = 4
GPT = LANES // S5_GROUP
NJ = S5_WIDTH // LANES
CL = CH * LANES
SW = 4 * GPT * S5_STATE

FN = 128
FB = 16

TM = 512
TM_PROJ = 1024
N_TILES = N_TOK // TM
MIX_GROUPS = 1
BM = 512
FFN_TAIL = 128
N_SLOTS = N_TOK * TOP_K
SEG_ALIGN = 2 * SUBLANES
CAP_BLOCK = 256
CAP = -(-(TOP_K * TM + N_EXPERTS * (SEG_ALIGN - 1)) // CAP_BLOCK) * CAP_BLOCK
NCHK = CAP // SEG_ALIGN
N_BLOCKS = -(-(N_SLOTS + N_TILES * N_EXPERTS * (SEG_ALIGN - 1)) // BM) + N_EXPERTS
ROWS = N_BLOCKS * BM


def _cparams(sem):
    return pltpu.CompilerParams(dimension_semantics=sem, vmem_limit_bytes=VMEM_LIMIT)


def _layer_norm(x, g, b):
    mu = jnp.mean(x, axis=-1, keepdims=True)
    xc = x - mu
    var = jnp.mean(xc * xc, axis=-1, keepdims=True)
    return xc * lax.rsqrt(var + LN_EPS) * g + b


def _sigmoid(x):
    return 1.0 / (1.0 + jnp.exp(-x))


def _ada_kernel(c_ref, w_ref, b_ref, o_ref):
    c = c_ref[...]
    s = c * _sigmoid(c)
    o_ref[...] = jnp.dot(s, w_ref[...], preferred_element_type=F32, precision=HI) + b_ref[...]


def _ada(cc, w_ada, b_ada):
    nb = 4
    wb = 6 * D // nb
    return pl.pallas_call(
        _ada_kernel,
        grid=(nb,),
        in_specs=[pl.BlockSpec((SUBLANES, D), lambda i: (0, 0)),
                  pl.BlockSpec((D, wb), lambda i: (0, i)),
                  pl.BlockSpec((1, wb), lambda i: (0, i))],
        out_specs=pl.BlockSpec((SUBLANES, wb), lambda i: (0, i)),
        out_shape=jax.ShapeDtypeStruct((SUBLANES, 6 * D), F32),
        compiler_params=_cparams(("parallel",)),
        name="ada",
    )(cc, w_ada, b_ada)


def _fftw_kernel(w_ref, b_ref, f_ref, wo_ref, bo_ref):
    f = f_ref[...]
    wo_ref[...] = jnp.dot(w_ref[...], f, preferred_element_type=F32, precision=HI)
    bo_ref[...] = jnp.dot(b_ref[...], f, preferred_element_type=F32, precision=HI)


def _fft_weights(w_fft, b_fft8, fc):
    return pl.pallas_call(
        _fftw_kernel,
        out_shape=(jax.ShapeDtypeStruct((D, 2 * FFT_WIDTH), F32),
                   jax.ShapeDtypeStruct((SUBLANES, 2 * FFT_WIDTH), F32)),
        compiler_params=pltpu.CompilerParams(vmem_limit_bytes=VMEM_LIMIT),
        name="fftw",
    )(w_fft, b_fft8, fc)


def _pos_code(er_ref, ec_ref, tm):
    nr = tm // GRID_W
    er = er_ref[...]
    row = jnp.broadcast_to(er[:, None, :], (nr, GRID_W, D // 2)).reshape(tm, D // 2)
    col = jnp.concatenate([ec_ref[...]] * nr, axis=0)
    return jnp.concatenate([row, col], axis=-1)


def _to_chunk_major(val, scr_ref, out_ref, tm):
    for j in range(NJ):
        scr_ref[j] = val[:, j * LANES:(j + 1) * LANES]
    for t in range(CH):
        for j in range(NJ):
            piece = scr_ref[j, pl.ds(t, tm // CH, stride=CH), :]
            out_ref[t, :, j * LANES:(j + 1) * LANES] = piece.astype(out_ref.dtype)


def _proj_kernel(x_ref, er_ref, ec_ref, lg_ref, lb_ref, m_ref, s_ref, w_ref, b_ref,
                 p_ref, xr_ref, xi_ref, scr_ref):
    x = x_ref[...] + _pos_code(er_ref, ec_ref, TM_PROJ)
    h = _layer_norm(x, lg_ref[...], lb_ref[...])
    u = (h * m_ref[...] + s_ref[...]).astype(BF16)
    p = jnp.dot(u, w_ref[...], preferred_element_type=F32) + b_ref[...]
    _to_chunk_major(p[:, :S5_WIDTH], scr_ref, p_ref, TM_PROJ)
    xr_ref[...] = p[:, S5_WIDTH:S5_WIDTH + FFT_WIDTH].astype(BF16)
    xi_ref[...] = p[:, S5_WIDTH + FFT_WIDTH:].astype(BF16)


def _proj(x, emb_r, emb_c, lg, lb, m1, s1, wcat, bcat):
    nw = wcat.shape[1]
    vec = pl.BlockSpec((1, D), lambda i: (0, 0))
    return pl.pallas_call(
        _proj_kernel,
        grid=(N_TOK // TM_PROJ,),
        in_specs=[pl.BlockSpec((TM_PROJ, D), lambda i: (i, 0)),
                  pl.BlockSpec((TM_PROJ // GRID_W, D // 2), lambda i: (i, 0)),
                  pl.BlockSpec((GRID_W, D // 2), lambda i: (0, 0)),
                  vec, vec, vec, vec,
                  pl.BlockSpec((D, nw), lambda i: (0, 0)),
                  pl.BlockSpec((1, nw), lambda i: (0, 0))],
        out_specs=(pl.BlockSpec((CH, TM_PROJ // CH, S5_WIDTH), lambda i: (0, i, 0)),
                   pl.BlockSpec((TM_PROJ, FFT_WIDTH), lambda i: (i, 0)),
                   pl.BlockSpec((TM_PROJ, FFT_WIDTH), lambda i: (i, 0))),
        out_shape=(jax.ShapeDtypeStruct((CH, N_CHUNK, S5_WIDTH), BF16),
                   jax.ShapeDtypeStruct((N_TOK, FFT_WIDTH), BF16),
                   jax.ShapeDtypeStruct((N_TOK, FFT_WIDTH), BF16)),
        scratch_shapes=[pltpu.VMEM((NJ, TM_PROJ, LANES), F32)],
        compiler_params=_cparams(("parallel",)),
        name="proj",
    )(x, emb_r, emb_c, lg, lb, m1, s1, wcat, bcat)


def _ctx_proj_kernel(x_ref, lg_ref, lb_ref, m_ref, s_ref, w_ref, b_ref, p_ref, scr_ref):
    h = _layer_norm(x_ref[...], lg_ref[...], lb_ref[...])
    u = (h * m_ref[...] + s_ref[...]).astype(BF16)
    p = jnp.dot(u, w_ref[...], preferred_element_type=F32) + b_ref[...]
    _to_chunk_major(p, scr_ref, p_ref, N_CTX)


def _ctx_proj(ctx, lg, lb, m1, s1, w_s5, b_s5):
    return pl.pallas_call(
        _ctx_proj_kernel,
        out_shape=jax.ShapeDtypeStruct((CH, N_CHUNK_CTX, S5_WIDTH), BF16),
        scratch_shapes=[pltpu.VMEM((NJ, N_CTX, LANES), F32)],
        compiler_params=pltpu.CompilerParams(vmem_limit_bytes=VMEM_LIMIT),
        name="ctxproj",
    )(ctx, lg, lb, m1, s1, w_s5, b_s5)


def _s5_tables(lam_re, lam_im, log_dt, b_re, b_im, c_re, c_im, d_skip):
    dt = jnp.exp(log_dt)[..., None]
    zr = lam_re * dt
    zi = lam_im * dt

    def apow(m):
        m = jnp.asarray(m, F32)
        mag = jnp.exp(zr[..., None] * m)
        return mag * jnp.cos(zi[..., None] * m), mag * jnp.sin(zi[..., None] * m)

    a_re, a_im = apow(jnp.ones((1,), F32))
    a_re, a_im = a_re[..., 0], a_im[..., 0]
    den = lam_re * lam_re + lam_im * lam_im
    num_re = a_re - 1.0
    k_re = (num_re * lam_re + a_im * lam_im) / den
    k_im = (a_im * lam_re - num_re * lam_im) / den
    bb_re = k_re[..., None] * b_re - k_im[..., None] * b_im
    bb_im = k_re[..., None] * b_im + k_im[..., None] * b_re

    ks = jnp.arange(CH + 1, dtype=F32)
    pw_re, pw_im = apow(ks)

    def tap(c, pw, bb):
        return jnp.einsum('dgvp,dgpk,dgph->kdghv', c, pw[..., :CH], bb, precision=HI)
    taps = (tap(c_re, pw_re, bb_re) - tap(c_re, pw_im, bb_im)
            - tap(c_im, pw_re, bb_im) - tap(c_im, pw_im, bb_re))
    skip = d_skip.reshape(S5_GROUPS, S5_GROUP, 1) * jnp.eye(S5_GROUP, dtype=F32)[None]
    taps = taps.at[0, 0].add(skip)
    b_c = jnp.transpose(taps.reshape(CH, 2, NJ, LANES, S5_GROUP), (2, 0, 1, 3, 4))
    b_c = b_c.reshape(NJ, 2 * CH, LANES, S5_GROUP)

    ef = (CH - 1) - jnp.arange(CH)
    eb = jnp.arange(CH)

    def q_part(d, e):
        pr = jnp.transpose(pw_re[d][..., e], (0, 2, 1))[:, :, None, :]
        pi = jnp.transpose(pw_im[d][..., e], (0, 2, 1))[:, :, None, :]
        br = jnp.transpose(bb_re[d], (0, 2, 1))[:, None, :, :]
        bi = jnp.transpose(bb_im[d], (0, 2, 1))[:, None, :, :]
        return pr * br - pi * bi, pr * bi + pi * br

    def q_rows(v):
        v = v.reshape(NJ, GPT, CH, S5_GROUP, S5_STATE)
        return jnp.transpose(v, (0, 2, 1, 3, 4)).reshape(NJ, CL, S5_STATE)
    a_q = jnp.stack([q_rows(v) for v in q_part(0, ef) + q_part(1, eb)], axis=0)

    of = jnp.arange(CH) + 1
    ob = CH - jnp.arange(CH)

    def p_part(d, e):
        pr = pw_re[d][..., e][:, :, :, None]
        pi = pw_im[d][..., e][:, :, :, None]
        cr = jnp.transpose(c_re[d], (0, 2, 1))[:, :, None, :]
        ci = jnp.transpose(c_im[d], (0, 2, 1))[:, :, None, :]
        return cr * pr - ci * pi, -(cr * pi + ci * pr)
    a_p = jnp.stack([v.reshape(NJ, GPT * S5_STATE, CH * S5_GROUP)
                     for v in p_part(0, of) + p_part(1, ob)], axis=0)

    def lanes(v):
        return jnp.transpose(v.reshape(2, NJ, GPT * S5_STATE), (1, 0, 2))
    c_r, c_i = apow(jnp.full((1,), float(CH), F32))
    s_r, s_i = apow(jnp.full((1,), float(CH * SEG), F32))
    cr, ci, sr, si = (lanes(v[..., 0]) for v in (c_r, c_i, s_r, s_i))
    trans = jnp.stack([cr[:, 0], ci[:, 0], cr[:, 1], ci[:, 1],
                       sr[:, 0], si[:, 0], sr[:, 1], si[:, 1]], axis=1)

    cidx = jnp.arange(N_CHUNK_CTX, dtype=F32)
    wf_r, wf_i = apow(CH * (N_CHUNK_CTX - 1 - cidx))
    wb_r, wb_i = apow(CH * cidx)

    def ctx_lanes(v, d):
        return jnp.transpose(v[d].reshape(NJ, GPT * S5_STATE, N_CHUNK_CTX), (0, 2, 1))
    ctx_w = jnp.stack([ctx_lanes(wf_r, 0), ctx_lanes(wf_i, 0),
                       ctx_lanes(wb_r, 1), ctx_lanes(wb_i, 1)], axis=1)
    return b_c, a_q, a_p, trans, ctx_w


def _s5w_kernel(bc_ref, aq_ref, ap_ref, c16_ref, c64_ref, cm_ref, wm_ref, wq_ref, wp_ref):
    def expand(a, c, row_shift, col_shift):
        w = jnp.dot(a, c, preferred_element_type=F32)
        rg = (lax.broadcasted_iota(jnp.int32, (w.shape[0], 1), 0) >> row_shift) & (GPT - 1)
        cg = (lax.broadcasted_iota(jnp.int32, (1, w.shape[1]), 1) >> col_shift) & (GPT - 1)
        return jnp.where(rg == cg, w, 0.0)

    blk = [expand(bc_ref[0, kd], c16_ref[...], 4, 4) for kd in range(2 * CH)]
    for t in range(CH):
        for u in range(CH):
            b = blk[2 * (u - t)] if u > t else blk[2 * (t - u) + 1] if u < t else blk[0] + blk[1]
            wm_ref[0, t * LANES:(t + 1) * LANES, u * LANES:(u + 1) * LANES] = b.astype(BF16)
    half = GPT * S5_STATE
    for s in range(4):
        wq_ref[0, :, s * half:(s + 1) * half] = expand(aq_ref[s, 0], c64_ref[...], 4, 6).astype(BF16)
        wp_ref[0, s * half:(s + 1) * half, :] = expand(ap_ref[s, 0], cm_ref[...], 6, 4).astype(BF16)


def _s5_expand(b_c, a_q, a_p):
    rep = np.ones((1, GPT))
    c16 = jnp.asarray(np.kron(rep, np.eye(S5_GROUP)), F32).astype(BF16)
    c64 = jnp.asarray(np.kron(rep, np.eye(S5_STATE)), F32).astype(BF16)
    c_m = jnp.asarray(np.kron(np.eye(CH), np.kron(rep, np.eye(S5_GROUP))), F32).astype(BF16)
    b_c, a_q, a_p = b_c.astype(BF16), a_q.astype(BF16), a_p.astype(BF16)
    half = GPT * S5_STATE
    return pl.pallas_call(
        _s5w_kernel,
        grid=(NJ,),
        in_specs=[pl.BlockSpec((1, 2 * CH, LANES, S5_GROUP), lambda j: (j, 0, 0, 0)),
                  pl.BlockSpec((4, 1, CL, S5_STATE), lambda j: (0, j, 0, 0)),
                  pl.BlockSpec((4, 1, half, CH * S5_GROUP), lambda j: (0, j, 0, 0)),
                  pl.BlockSpec(c16.shape, lambda j: (0, 0)),
                  pl.BlockSpec(c64.shape, lambda j: (0, 0)),
                  pl.BlockSpec(c_m.shape, lambda j: (0, 0))],
        out_specs=(pl.BlockSpec((1, CL, CL), lambda j: (j, 0, 0)),
                   pl.BlockSpec((1, CL, SW), lambda j: (j, 0, 0)),
                   pl.BlockSpec((1, SW, CL), lambda j: (j, 0, 0))),
        out_shape=(jax.ShapeDtypeStruct((NJ, CL, CL), BF16),
                   jax.ShapeDtypeStruct((NJ, CL, SW), BF16),
                   jax.ShapeDtypeStruct((NJ, SW, CL), BF16)),
        compiler_params=_cparams(("parallel",)),
        name="s5w",
    )(b_c, a_q, a_p, c16, c64, c_m)


def _s5_kernel(p_ref, pc_ref, wm_ref, wq_ref, wp_ref, tr_ref, cw_ref, y_ref, v_ref):
    nq = NJ
    half = GPT * S5_STATE

    def chunk_rows(ref, r0, nrows):
        return jnp.concatenate([ref[t, pl.ds(r0, nrows), :] for t in range(CH)], axis=-1)

    def fill(k, c):
        r0 = pl.multiple_of(k * SEG, SEG)
        v = jnp.dot(chunk_rows(p_ref, r0, SEG), wq_ref[0], preferred_element_type=F32)
        for s in range(4 * nq):
            v_ref[s, pl.ds(k, SEG, stride=NSEG), :] = v[:, s * LANES:(s + 1) * LANES]
        return c
    lax.fori_loop(0, NSEG, fill, 0)

    vc = jnp.dot(chunk_rows(pc_ref, 0, N_CHUNK_CTX), wq_ref[0], preferred_element_type=F32)
    vfr, vfi, vbr, vbi = (vc[:, i * half:(i + 1) * half] for i in range(4))
    wfr, wfi, wbr, wbi = (cw_ref[0, i] for i in range(4))
    s0_fr = jnp.sum(wfr * vfr - wfi * vfi, axis=0, keepdims=True)
    s0_fi = jnp.sum(wfr * vfi + wfi * vfr, axis=0, keepdims=True)
    s0_br = jnp.sum(wbr * vbr - wbi * vbi, axis=0, keepdims=True)
    s0_bi = jnp.sum(wbr * vbi + wbi * vbr, axis=0, keepdims=True)

    tr = tr_ref[0]
    afr, afi, abr, abi = (jnp.broadcast_to(tr[i:i + 1], (NSEG, half)) for i in range(4))
    gfr, gfi, gbr, gbi = (tr[i:i + 1] for i in range(4, 8))

    def load_part(part, i):
        return jnp.concatenate(
            [v_ref[part * nq + q, pl.ds(pl.multiple_of(i * NSEG, NSEG), NSEG), :] for q in range(nq)],
            axis=-1)

    def store_part(part, i, val):
        for q in range(nq):
            v_ref[part * nq + q, pl.ds(pl.multiple_of(i * NSEG, NSEG), NSEG), :] = (
                val[:, q * LANES:(q + 1) * LANES])

    def step(i, carry, write):
        fr, fi, br, bi = carry
        ib = SEG - 1 - i
        ufr, ufi = load_part(0, i), load_part(1, i)
        ubr, ubi = load_part(2, ib), load_part(3, ib)
        if write:
            store_part(0, i, fr)
            store_part(1, i, fi)
            store_part(2, ib, br)
            store_part(3, ib, bi)
        return (afr * fr - afi * fi + ufr, afr * fi + afi * fr + ufi,
                abr * br - abi * bi + ubr, abr * bi + abi * br + ubi)

    zero = jnp.zeros((NSEG, half), F32)
    ffr, ffi, fbr, fbi = lax.fori_loop(0, SEG, functools.partial(step, write=False),
                                       (zero, zero, zero, zero), unroll=SCAN_UNROLL)

    rows_fr, rows_fi = [s0_fr], [s0_fi]
    for k in range(1, NSEG):
        pr, pi = rows_fr[-1], rows_fi[-1]
        rows_fr.append(gfr * pr - gfi * pi + ffr[k - 1:k])
        rows_fi.append(gfr * pi + gfi * pr + ffi[k - 1:k])
    rows_br, rows_bi = [s0_br], [s0_bi]
    for k in range(NSEG - 2, -1, -1):
        pr, pi = rows_br[0], rows_bi[0]
        rows_br.insert(0, gbr * pr - gbi * pi + fbr[k + 1:k + 2])
        rows_bi.insert(0, gbr * pi + gbi * pr + fbi[k + 1:k + 2])
    init = tuple(jnp.concatenate(r, axis=0) for r in (rows_fr, rows_fi, rows_br, rows_bi))

    lax.fori_loop(0, SEG, functools.partial(step, write=True), init, unroll=SCAN_UNROLL)

    def emit(k, c):
        r0 = pl.multiple_of(k * SEG, SEG)
        b = chunk_rows(p_ref, r0, SEG)
        sin = jnp.concatenate([v_ref[s, pl.ds(k, SEG, stride=NSEG), :] for s in range(4 * nq)], axis=-1)
        y = (jnp.dot(b, wm_ref[0], preferred_element_type=F32)
             + jnp.dot(sin.astype(BF16), wp_ref[0], preferred_element_type=F32))
        for t in range(CH):
            y_ref[t, pl.ds(r0, SEG), :] = y[:, t * LANES:(t + 1) * LANES].astype(y_ref.dtype)
        return c
    lax.fori_loop(0, NSEG, emit, 0)


def _s5(p_t, pc_t, w_m, w_q, w_p, trans, ctx_w):
    one = pl.Buffered(1)
    return pl.pallas_call(
        _s5_kernel,
        grid=(NJ,),
        in_specs=[pl.BlockSpec((CH, N_CHUNK, LANES), lambda j: (0, 0, j)),
                  pl.BlockSpec((CH, N_CHUNK_CTX, LANES), lambda j: (0, 0, j)),
                  pl.BlockSpec((1, CL, CL), lambda j: (j, 0, 0), pipeline_mode=one),
                  pl.BlockSpec((1, CL, SW), lambda j: (j, 0, 0), pipeline_mode=one),
                  pl.BlockSpec((1, SW, CL), lambda j: (j, 0, 0), pipeline_mode=one),
                  pl.BlockSpec((1, SUBLANES, GPT * S5_STATE), lambda j: (j, 0, 0)),
                  pl.BlockSpec((1, 4, N_CHUNK_CTX, GPT * S5_STATE), lambda j: (j, 0, 0, 0))],
        out_specs=pl.BlockSpec((CH, N_CHUNK, LANES), lambda j: (0, 0, j)),
        out_shape=jax.ShapeDtypeStruct((CH, N_CHUNK, S5_WIDTH), BF16),
        scratch_shapes=[pltpu.VMEM((4 * NJ, N_CHUNK, LANES), F32)],
        compiler_params=_cparams(("parallel",)),
        name="s5",
    )(p_t, pc_t, w_m, w_q, w_p, trans, ctx_w)


def _dft_tables():
    n = np.arange(FN)
    ang = 2.0 * np.pi * np.outer(n, n) / FN
    c, s = np.cos(ang), np.sin(ang)
    st1 = np.block([[c, s], [-s, c]])
    tw = 2.0 * np.pi * np.outer(n, n) / (FN * FN)
    wr, wi = np.cos(tw), -np.sin(tw)
    fr = c[None] * wr[:, None, :] + s[None] * wi[:, None, :]
    fi = c[None] * wi[:, None, :] - s[None] * wr[:, None, :]
    st2 = np.concatenate([fr, -fi], axis=-1)
    scale = 1.0 / math.sqrt(N_TOK * FFT_DIM)
    blk_c = np.kron(np.eye(FFT_GROUPS), c) * scale
    blk_s = np.kron(np.eye(FFT_GROUPS), s) * scale
    fc = np.concatenate([blk_c, -blk_s], axis=1)
    return (jnp.asarray(st1, F32).astype(BF16), jnp.asarray(st2, F32).astype(BF16), jnp.asarray(fc, F32))


FSL = FFT_WIDTH // LANES


def _rows_to_slabs(val, slab_ref, first):
    for s in range(FSL):
        slab_ref[first + s] = val[:, s * LANES:(s + 1) * LANES]


def _slabs_to_block(slab_ref, first):
    val = jnp.concatenate([slab_ref[first + s] for s in range(FSL)], axis=-1)
    return val.reshape(FN, FB, FFT_WIDTH)


def _fft1_kernel(xr_ref, xi_ref, f_ref, yr_ref, yi_ref, in_ref, out_ref):
    _rows_to_slabs(xr_ref[...].astype(F32).reshape(FN * FB, FFT_WIDTH), in_ref, 0)
    _rows_to_slabs(xi_ref[...].astype(F32).reshape(FN * FB, FFT_WIDTH), in_ref, FSL)
    for b in range(FB):
        def part(first):
            return jnp.concatenate([in_ref[first + s, pl.ds(b, FN, stride=FB), :] for s in range(FSL)],
                                   axis=-1)
        xs = jnp.concatenate([part(0), part(FSL)], axis=0).astype(BF16)
        y = jnp.dot(f_ref[...], xs, preferred_element_type=F32)
        for s in range(FSL):
            out_ref[s, pl.ds(b, FN, stride=FB), :] = y[:FN, s * LANES:(s + 1) * LANES]
            out_ref[FSL + s, pl.ds(b, FN, stride=FB), :] = y[FN:, s * LANES:(s + 1) * LANES]
    yr_ref[...] = _slabs_to_block(out_ref, 0).astype(BF16)
    yi_ref[...] = _slabs_to_block(out_ref, FSL).astype(BF16)


def _fft1(xr, xi, st1):
    spec = pl.BlockSpec((FN, FB, FFT_WIDTH), lambda i: (0, i, 0))
    slabs = pltpu.VMEM((2 * FSL, FN * FB, LANES), F32)
    return pl.pallas_call(
        _fft1_kernel,
        grid=(FN // FB,),
        in_specs=[spec, spec, pl.BlockSpec((2 * FN, 2 * FN), lambda i: (0, 0))],
        out_specs=(spec, spec),
        out_shape=(jax.ShapeDtypeStruct((FN, FN, FFT_WIDTH), BF16),) * 2,
        scratch_shapes=[slabs, slabs],
        compiler_params=_cparams(("parallel",)),
        name="fft1",
    )(xr.reshape(FN, FN, FFT_WIDTH), xi.reshape(FN, FN, FFT_WIDTH), st1)


def _fft2_kernel(yr_ref, yi_ref, f_ref, z_ref, out_ref):
    for b in range(FB):
        ys = jnp.concatenate([yr_ref[b * FN:(b + 1) * FN, :], yi_ref[b * FN:(b + 1) * FN, :]], axis=0)
        z = jnp.dot(f_ref[b], ys, preferred_element_type=F32)
        for s in range(FSL):
            out_ref[s, pl.ds(b, FN, stride=FB), :] = z[:, s * LANES:(s + 1) * LANES]
    z_ref[...] = _slabs_to_block(out_ref, 0).astype(BF16)


def _fft2(yr, yi, st2):
    rows = pl.BlockSpec((FB * FN, FFT_WIDTH), lambda i: (i, 0))
    z = pl.pallas_call(
        _fft2_kernel,
        grid=(FN // FB,),
        in_specs=[rows, rows, pl.BlockSpec((FB, FN, 2 * FN), lambda i: (i, 0, 0))],
        out_specs=pl.BlockSpec((FN, FB, FFT_WIDTH), lambda i: (0, i, 0)),
        out_shape=jax.ShapeDtypeStruct((FN, FN, FFT_WIDTH), BF16),
        scratch_shapes=[pltpu.VMEM((FSL, FN * FB, LANES), F32)],
        compiler_params=_cparams(("parallel",)),
        name="fft2",
    )(yr.reshape(N_TOK, FFT_WIDTH), yi.reshape(N_TOK, FFT_WIDTH), st2)
    return z.reshape(N_TOK, FFT_WIDTH)


def _gelu_tanh(x):
    return 0.5 * x * (1.0 + jnp.tanh(math.sqrt(2.0 / math.pi) * (x + 0.044715 * (x * x * x))))


def _mix_kernel(x_ref, er_ref, ec_ref, lg_ref, lb_ref, m1_ref, s1_ref, wg_ref, bg_ref,
                yt_ref, zr_ref, wglu_ref, bglu_ref, wbs_ref, wbf_ref, bbf_ref, wo_ref, bo_ref,
                g1_ref, l1g_ref, l1b_ref, m2_ref, s2_ref, wr_ref, br_ref, tri_ref, etri_ref,
                h1_ref, u2_ref, pos_ref, gate_ref, cnt_ref, scr_ref):
    pos = _pos_code(er_ref, ec_ref, TM)

    def front(r0, nr):
        rows = slice(r0, r0 + nr)
        h = _layer_norm(x_ref[rows, :] + pos[rows, :], lg_ref[...], lb_ref[...])
        u = (h * m1_ref[...] + s1_ref[...]).astype(BF16)

        c0, nc = r0 // CH, nr // CH
        for t in range(CH):
            for j in range(NJ):
                scr_ref[j, pl.ds(r0 + t, nc, stride=CH), :] = (
                    yt_ref[t, c0:c0 + nc, j * LANES:(j + 1) * LANES].astype(F32))
        ys = jnp.concatenate([scr_ref[j, rows, :] for j in range(NJ)], axis=-1)
        z = jnp.dot(_gelu_tanh(ys).astype(BF16), wglu_ref[...], preferred_element_type=F32) + bglu_ref[...]
        glu = (z[:, :S5_WIDTH] * _sigmoid(z[:, S5_WIDTH:])).astype(BF16)
        gates = _sigmoid(jnp.dot(u, wg_ref[...], preferred_element_type=F32) + bg_ref[...])
        y_s5 = jnp.dot(glu, wbs_ref[...], preferred_element_type=F32)
        y_fft = jnp.dot(zr_ref[rows, :], wbf_ref[...], preferred_element_type=F32) + bbf_ref[...]
        mixed = (gates[:, :D] * y_s5 + gates[:, D:] * y_fft).astype(BF16)
        y = jnp.dot(mixed, wo_ref[...], preferred_element_type=F32) + bo_ref[...]
        h1 = _layer_norm(ALPHA * h + g1_ref[...] * y, l1g_ref[...], l1b_ref[...])
        h1_ref[rows, :] = h1
        u2 = h1 * m2_ref[...] + s2_ref[...]
        u2_ref[rows, :] = u2.astype(BF16)
        u_hi = u2.astype(BF16)
        u_lo = (u2 - u_hi.astype(F32)).astype(BF16)

        def nt(a, b):
            return lax.dot_general(a, b, (((1,), (1,)), ((), ())), preferred_element_type=F32)
        return nt(wr_ref[0], u_hi) + nt(wr_ref[0], u_lo) + nt(wr_ref[1], u_hi)

    nr = TM // MIX_GROUPS
    logits = jnp.concatenate([front(g * nr, nr) for g in range(MIX_GROUPS)], axis=-1) + br_ref[:, 0:1]
    eidx = lax.broadcasted_iota(jnp.int32, (N_EXPERTS, TM), 0)
    vals, hots = [], []
    cur = logits
    for _k in range(TOP_K):
        m = jnp.max(cur, axis=0, keepdims=True)
        sel = jnp.min(jnp.where(cur == m, eidx, N_EXPERTS), axis=0, keepdims=True)
        hot = eidx == sel
        cur = jnp.where(hot, -jnp.inf, cur)
        vals.append(m)
        hots.append(hot)
    exps = [jnp.exp(v - vals[0]) for v in vals]
    den = exps[0] + exps[1] + exps[2] + exps[3]
    gate4 = jnp.concatenate([e / den for e in exps], axis=0)

    hot_sum = (hots[0] | hots[1] | hots[2] | hots[3]).astype(F32)
    before = jnp.dot(hot_sum.astype(BF16), tri_ref[...], preferred_element_type=F32)
    cnt = jnp.broadcast_to(jnp.sum(hot_sum, axis=1, keepdims=True), (N_EXPERTS, LANES))
    cnt8 = jnp.floor((cnt + (SEG_ALIGN - 1)) * (1.0 / SEG_ALIGN)) * SEG_ALIGN
    seg0 = jnp.dot(etri_ref[...], cnt8.astype(BF16), preferred_element_type=F32)
    tot = seg0[:, 0:1] + before
    pos4 = jnp.concatenate(
        [jnp.sum(jnp.where(hk, tot, 0.0), axis=0, keepdims=True) for hk in hots], axis=0)
    pos_ref[...] = pos4.astype(jnp.int32)
    cnt_ref[0] = cnt

    gpad = jnp.concatenate([gate4, pos4, jnp.zeros((LANES - 2 * TOP_K, TM), F32)], axis=0)
    gate_ref[...] = gpad.T


def _mix(x, emb_r, emb_c, lg, lb, m1, s1, wg, bg, y_t, zr, wglu, bglu, wbs, wbf, bbf, wo, bo,
         g1, l1g, l1b, m2, s2, wr_t, br, tri, etri):
    vec = pl.BlockSpec((1, D), lambda i: (0, 0))

    def full(a):
        return pl.BlockSpec(a.shape, lambda i: (0,) * a.ndim)
    return pl.pallas_call(
        _mix_kernel,
        grid=(N_TOK // TM,),
        in_specs=[pl.BlockSpec((TM, D), lambda i: (i, 0)),
                  pl.BlockSpec((TM // GRID_W, D // 2), lambda i: (i, 0)),
                  pl.BlockSpec((GRID_W, D // 2), lambda i: (0, 0)),
                  vec, vec, vec, vec, full(wg), full(bg),
                  pl.BlockSpec((CH, TM // CH, S5_WIDTH), lambda i: (0, i, 0)),
                  pl.BlockSpec((TM, FFT_WIDTH), lambda i: (i, 0)),
                  full(wglu), full(bglu), full(wbs), full(wbf), full(bbf), full(wo), full(bo),
                  vec, vec, vec, vec, vec, full(wr_t), full(br), full(tri), full(etri)],
        out_specs=(pl.BlockSpec((TM, D), lambda i: (i, 0)),
                   pl.BlockSpec((TM, D), lambda i: (i, 0)),
                   pl.BlockSpec((TOP_K, TM), lambda i: (0, i)),
                   pl.BlockSpec((TM, LANES), lambda i: (i, 0)),
                   pl.BlockSpec((1, N_EXPERTS, LANES), lambda i: (i, 0, 0))),
        out_shape=(jax.ShapeDtypeStruct((N_TOK, D), F32),
                   jax.ShapeDtypeStruct((N_TOK, D), BF16),
                   jax.ShapeDtypeStruct((TOP_K, N_TOK), jnp.int32),
                   jax.ShapeDtypeStruct((N_TOK, LANES), F32),
                   jax.ShapeDtypeStruct((N_TILES, N_EXPERTS, LANES), F32)),
        scratch_shapes=[pltpu.VMEM((NJ, TM, LANES), F32)],
        compiler_params=_cparams(("parallel",)),
        name="mix",
    )(x, emb_r, emb_c, lg, lb, m1, s1, wg, bg, y_t, zr, wglu, bglu, wbs, wbf, bbf, wo, bo,
      g1, l1g, l1b, m2, s2, wr_t, br, tri, etri)


def _on_parity(i, fn):
    @pl.when(i % 2 == 0)
    def _():
        fn(0)

    @pl.when(i % 2 == 1)
    def _():
        fn(1)


def _dispatch_kernel(pend_ref, padded_ref, nchk_ref, dest_ref, pos_ref, u_ref, buf_ref,
                     sorted_ref, zero_ref, zsem, sems):
    i = pl.program_id(0)

    @pl.when(i == 0)
    def _():
        zero_ref[...] = jnp.zeros_like(zero_ref)
        n_used = pend_ref[N_EXPERTS - 1] // BM

        def clear_copy(start):
            return pltpu.make_async_copy(
                zero_ref, buf_ref.at[pl.ds(pl.multiple_of(start, BM), BM)], zsem)

        def each(fn):
            def expert(e, c):
                @pl.when(padded_ref[e] > 0)
                def _():
                    fn(clear_copy(pend_ref[e] - BM))
                return c
            lax.fori_loop(0, N_EXPERTS, expert, 0)

            def tail(b, c):
                fn(clear_copy(b * BM))
                return c
            lax.fori_loop(n_used, N_BLOCKS, tail, 0)
        each(lambda cp: cp.start())
        each(lambda cp: cp.wait())

    def drain(slot, tile):
        n = pl.multiple_of(nchk_ref[tile] * SEG_ALIGN, SEG_ALIGN)
        pltpu.make_async_copy(sorted_ref.at[slot, pl.ds(0, n)], buf_ref.at[pl.ds(0, n)],
                              sems.at[slot]).wait()

    def run(slot):
        pos = pos_ref[...]
        u = u_ref[...]
        for rb in range(CAP // CAP_BLOCK):
            rows = lax.broadcasted_iota(jnp.int32, (CAP_BLOCK, TM), 0) + rb * CAP_BLOCK
            hit = rows == pos[0:1]
            for k in range(1, TOP_K):
                hit = hit | (rows == pos[k:k + 1])
            onehot = jnp.where(hit, 1.0, 0.0).astype(BF16)
            sorted_ref[slot, rb * CAP_BLOCK:(rb + 1) * CAP_BLOCK, :] = jnp.dot(
                onehot, u, preferred_element_type=F32).astype(BF16)

        def issue(j, c):
            src = pl.multiple_of(j * SEG_ALIGN, SEG_ALIGN)
            dst = pl.multiple_of(dest_ref[0, 0, j], SEG_ALIGN)
            pltpu.make_async_copy(sorted_ref.at[slot, pl.ds(src, SEG_ALIGN)],
                                  buf_ref.at[pl.ds(dst, SEG_ALIGN)], sems.at[slot]).start()
            return c
        lax.fori_loop(0, nchk_ref[i], issue, 0)

        @pl.when(i > 0)
        def _():
            drain(1 - slot, i - 1)

        @pl.when(i == N_TILES - 1)
        def _():
            drain(slot, i)
    _on_parity(i, run)


def _dispatch(pad_ends, padded, nchk, chunk_dest, pos_t, u2):
    return pl.pallas_call(
        _dispatch_kernel,
        grid_spec=pltpu.PrefetchScalarGridSpec(
            num_scalar_prefetch=3,
            grid=(N_TILES,),
            in_specs=[pl.BlockSpec((1, 1, NCHK), lambda i, a, b, c: (i, 0, 0),
                                   memory_space=pltpu.SMEM),
                      pl.BlockSpec((TOP_K, TM), lambda i, a, b, c: (0, i)),
                      pl.BlockSpec((TM, D), lambda i, a, b, c: (i, 0))],
            out_specs=pl.BlockSpec(memory_space=pl.ANY),
            scratch_shapes=[pltpu.VMEM((2, CAP, D), BF16),
                            pltpu.VMEM((BM, D), BF16),
                            pltpu.SemaphoreType.DMA(()),
                            pltpu.SemaphoreType.DMA((2,))]),
        out_shape=jax.ShapeDtypeStruct((ROWS, D), BF16),
        compiler_params=_cparams(("arbitrary",)),
        name="dispatch",
    )(pad_ends, padded, nchk, chunk_dest, pos_t, u2)


def _ffn_kernel(be_ref, nu_ref, run_ref, nxt_ref, valid_ref, x_ref, wu_hbm, bu_ref, wd_hbm, bd_ref,
                y_ref, wu_ref, wd_ref, wub_ref, wdb_ref, sems):
    i = pl.program_id(0)
    used = i < nu_ref[0]

    def weight_copies(e, slot):
        return (pltpu.make_async_copy(wu_hbm.at[e], wu_ref.at[slot], sems.at[slot]),
                pltpu.make_async_copy(wd_hbm.at[e], wd_ref.at[slot], sems.at[slot]))

    @pl.when(used)
    def _():
        run = run_ref[i]

        @pl.when(run >= 0)
        def _():
            def open_run(slot):
                @pl.when(run == 0)
                def _():
                    for cp in weight_copies(be_ref[i], slot):
                        cp.start()

                @pl.when(nxt_ref[i] >= 0)
                def _():
                    for cp in weight_copies(nxt_ref[i], 1 - slot):
                        cp.start()
                for cp in weight_copies(be_ref[i], slot):
                    cp.wait()
                wub_ref[...] = wu_ref[slot].astype(BF16)
                wdb_ref[...] = wd_ref[slot].astype(BF16)
            _on_parity(run, open_run)

        def expert_rows(r0, nr):
            rows = slice(r0, r0 + nr)
            h = jnp.dot(x_ref[rows, :], wub_ref[...], preferred_element_type=F32) + bu_ref[0]
            h_glu = jnp.minimum(h[:, :D], SWIGLU_LIMIT)
            h_lin = jnp.clip(h[:, D:], -SWIGLU_LIMIT, SWIGLU_LIMIT)
            act = (h_glu * _sigmoid(SWIGLU_ALPHA * h_glu) * (h_lin + 1.0)).astype(BF16)
            y_ref[rows, :] = (jnp.dot(act, wdb_ref[...], preferred_element_type=F32)
                              + bd_ref[0]).astype(BF16)

        valid = valid_ref[i]

        @pl.when(valid == BM)
        def _():
            expert_rows(0, BM)

        @pl.when(valid < BM)
        def _():
            for r0 in range(0, BM, FFN_TAIL):
                @pl.when(r0 < valid)
                def _(r0=r0):
                    expert_rows(r0, FFN_TAIL)

                @pl.when(r0 >= valid)
                def _(r0=r0):
                    y_ref[r0:r0 + FFN_TAIL, :] = jnp.zeros((FFN_TAIL, D), BF16)

    @pl.when(jnp.logical_not(used))
    def _():
        y_ref[...] = jnp.zeros_like(y_ref)


def _ffn(block_expert, n_used, run_id, next_expert, valid, buf, w_up, b_up, w_down, b_down):
    def blk(i, be, nu, *_):
        return jnp.minimum(i, nu[0] - 1)
    return pl.pallas_call(
        _ffn_kernel,
        grid_spec=pltpu.PrefetchScalarGridSpec(
            num_scalar_prefetch=5,
            grid=(N_BLOCKS,),
            in_specs=[pl.BlockSpec((BM, D), lambda i, *s: (blk(i, *s), 0)),
                      pl.BlockSpec(memory_space=pl.ANY),
                      pl.BlockSpec((1, 1, 2 * D), lambda i, *s: (s[0][blk(i, *s)], 0, 0)),
                      pl.BlockSpec(memory_space=pl.ANY),
                      pl.BlockSpec((1, 1, D), lambda i, *s: (s[0][blk(i, *s)], 0, 0))],
            out_specs=pl.BlockSpec((BM, D), lambda i, *s: (i, 0)),
            scratch_shapes=[pltpu.VMEM((2, D, 2 * D), F32),
                            pltpu.VMEM((2, D, D), F32),
                            pltpu.VMEM((D, 2 * D), BF16),
                            pltpu.VMEM((D, D), BF16),
                            pltpu.SemaphoreType.DMA((2,))]),
        out_shape=jax.ShapeDtypeStruct((ROWS, D), BF16),
        compiler_params=_cparams(("arbitrary",)),
        name="ffn",
    )(block_expert, n_used, run_id, next_expert, valid, buf, w_up, b_up, w_down, b_down)


def _combine_kernel(nchk_ref, dest_ref, dnext_ref, y_ref, h1_ref, gate_ref, g2_ref, lg_ref, lb_ref,
                    o_ref, sorted_ref, sems):
    i = pl.program_id(0)

    def fetch(slot, tile, table_ref):
        def issue(j, c):
            src = pl.multiple_of(table_ref[0, 0, j], SEG_ALIGN)
            dst = pl.multiple_of(j * SEG_ALIGN, SEG_ALIGN)
            pltpu.make_async_copy(y_ref.at[pl.ds(src, SEG_ALIGN)],
                                  sorted_ref.at[slot, pl.ds(dst, SEG_ALIGN)], sems.at[slot]).start()
            return c
        lax.fori_loop(0, nchk_ref[tile], issue, 0)

    def drain(slot, tile):
        n = pl.multiple_of(nchk_ref[tile] * SEG_ALIGN, SEG_ALIGN)
        pltpu.make_async_copy(y_ref.at[pl.ds(0, n)], sorted_ref.at[slot, pl.ds(0, n)],
                              sems.at[slot]).wait()

    @pl.when(i == 0)
    def _():
        sorted_ref[...] = jnp.zeros_like(sorted_ref)
        fetch(0, 0, dest_ref)

    def run(slot):
        @pl.when(i + 1 < N_TILES)
        def _():
            fetch(1 - slot, i + 1, dnext_ref)
        drain(slot, i)

        gp = gate_ref[...]
        m = jnp.zeros((TM, D), F32)
        for cb in range(CAP // CAP_BLOCK):
            cols = (lax.broadcasted_iota(jnp.int32, (TM, CAP_BLOCK), 1) + cb * CAP_BLOCK).astype(F32)
            g = jnp.where(cols == gp[:, TOP_K:TOP_K + 1], gp[:, 0:1], 0.0)
            for k in range(1, TOP_K):
                g = g + jnp.where(cols == gp[:, TOP_K + k:TOP_K + k + 1], gp[:, k:k + 1], 0.0)
            rows = sorted_ref[slot, cb * CAP_BLOCK:(cb + 1) * CAP_BLOCK, :]
            m = m + jnp.dot(g.astype(BF16), rows, preferred_element_type=F32)
        o_ref[...] = _layer_norm(ALPHA * h1_ref[...] + g2_ref[...] * m, lg_ref[...], lb_ref[...])
    _on_parity(i, run)


def _combine(nchk, chunk_dest, y_buf, h1, gate_tok, g2, lg, lb):
    vec = pl.BlockSpec((1, D), lambda i, n: (0, 0))
    return pl.pallas_call(
        _combine_kernel,
        grid_spec=pltpu.PrefetchScalarGridSpec(
            num_scalar_prefetch=1,
            grid=(N_TILES,),
            in_specs=[pl.BlockSpec((1, 1, NCHK), lambda i, n: (i, 0, 0), memory_space=pltpu.SMEM),
                      pl.BlockSpec((1, 1, NCHK), lambda i, n: (jnp.minimum(i + 1, N_TILES - 1), 0, 0),
                                   memory_space=pltpu.SMEM),
                      pl.BlockSpec(memory_space=pl.ANY),
                      pl.BlockSpec((TM, D), lambda i, n: (i, 0)),
                      pl.BlockSpec((TM, LANES), lambda i, n: (i, 0)),
                      vec, vec, vec],
            out_specs=pl.BlockSpec((TM, D), lambda i, n: (i, 0)),
            scratch_shapes=[pltpu.VMEM((2, CAP, D), BF16),
                            pltpu.SemaphoreType.DMA((2,))]),
        out_shape=jax.ShapeDtypeStruct((N_TOK, D), F32),
        compiler_params=_cparams(("arbitrary",)),
        name="combine",
    )(nchk, chunk_dest, chunk_dest, y_buf, h1, gate_tok, g2, lg, lb)


def _sincos_tables():
    q = D // 4
    omega = 1.0 / (10000.0 ** (jnp.arange(q, dtype=F32) / q))

    def emb(n):
        ang = jnp.arange(n, dtype=F32)[:, None] * omega[None, :]
        return jnp.concatenate([jnp.sin(ang), jnp.cos(ang)], axis=-1)
    return emb(N_TOK // GRID_W), emb(GRID_W)


def _tile_slots(a, tile):
    return jnp.transpose(a.reshape(TOP_K, N_TOK // tile, tile), (1, 0, 2)).reshape(N_TOK // tile, 1, TOP_K * tile)


def kernel(x, c, ctx, c_ctx, ln_in_g, ln_in_b, w_ada, b_ada, w_in, b_in, s5_lambda_re, s5_lambda_im, s5_log_dt, s5_b_re, s5_b_im, s5_c_re, s5_c_im, s5_d, w_glu, b_glu, w_br_s5, w_br_fft, b_br_fft, w_out, b_out, ln1_g, ln1_b, w_router, b_router, w_up, b_up, w_down, b_down, ln2_g, ln2_b):
    assert x.shape == (1, N_TOK, D) and ctx.shape == (1, N_CTX, D) and w_ada.shape[0] == 1
    row = lambda v: v.reshape(1, -1).astype(F32)

    cc = jnp.concatenate([c.reshape(1, D), c_ctx.reshape(1, D), jnp.zeros((SUBLANES - 2, D), F32)], axis=0)
    ada = _ada(cc, w_ada[0], row(b_ada[0]))
    sh1, sc1, g1, sh2, sc2, g2 = (ada[0:1, k * D:(k + 1) * D] for k in range(6))
    sh1c, sc1c = ada[1:2, 0:D], ada[1:2, D:2 * D]

    emb_r, emb_c = _sincos_tables()
    st1, st2, fc = _dft_tables()
    lg, lb = row(ln_in_g), row(ln_in_b)

    w_s5 = w_in[0][:, :S5_WIDTH]
    w_fft = w_in[0][:, S5_WIDTH:S5_WIDTH + FFT_WIDTH]
    w_g = w_in[0][:, S5_WIDTH + FFT_WIDTH:]
    b_s5 = row(b_in[0][:S5_WIDTH])
    b_fft8 = jnp.concatenate([row(b_in[0][S5_WIDTH:S5_WIDTH + FFT_WIDTH]),
                              jnp.zeros((SUBLANES - 1, FFT_WIDTH), F32)], axis=0)
    b_g = row(b_in[0][S5_WIDTH + FFT_WIDTH:])
    w_fc, b_fc = _fft_weights(w_fft, b_fft8, fc)
    wcat = jnp.concatenate([w_s5, w_fc], axis=1).astype(BF16)
    bcat = jnp.concatenate([b_s5, b_fc[0:1]], axis=1)

    x2 = x[0]
    p_t, xr, xi = _proj(x2, emb_r, emb_c, lg, lb, 1.0 + sc1, sh1, wcat, bcat)
    pc_t = _ctx_proj(ctx[0], lg, lb, 1.0 + sc1c, sh1c, w_s5.astype(BF16), b_s5)

    b_c, a_q, a_p, trans, ctx_w = _s5_tables(
        s5_lambda_re[0], s5_lambda_im[0], s5_log_dt[0], s5_b_re[0], s5_b_im[0],
        s5_c_re[0], s5_c_im[0], s5_d[0])
    w_m, w_q, w_p = _s5_expand(b_c, a_q, a_p)
    y_t = _s5(p_t, pc_t, w_m, w_q, w_p, trans, ctx_w)

    yr, yi = _fft1(xr, xi, st1)
    zr = _fft2(yr, yi, st2)

    tri = (jnp.arange(TM)[:, None] < jnp.arange(TM)[None, :]).astype(BF16)
    br = jnp.broadcast_to(b_router[0].reshape(N_EXPERTS, 1), (N_EXPERTS, LANES))
    etri = (jnp.arange(N_EXPERTS)[:, None] > jnp.arange(N_EXPERTS)[None, :]).astype(BF16)
    wr_t = jnp.transpose(w_router[0])
    wr_hi = wr_t.astype(BF16)
    wr_split = jnp.stack([wr_hi, (wr_t - wr_hi.astype(F32)).astype(BF16)], axis=0)
    h1, u2, pos_t, gate_tok, counts = _mix(
        x2, emb_r, emb_c, lg, lb, 1.0 + sc1, sh1, w_g.astype(BF16), b_g, y_t, zr,
        w_glu[0].astype(BF16), row(b_glu[0]), w_br_s5[0].astype(BF16), w_br_fft[0].astype(BF16),
        row(b_br_fft[0]), w_out[0].astype(BF16), row(b_out[0]), g1, row(ln1_g[0]), row(ln1_b[0]),
        1.0 + sc2, sh2, wr_split, br, tri, etri)

    cnt = counts[:, :, 0].astype(jnp.int32)
    seg = (cnt + SEG_ALIGN - 1) // SEG_ALIGN * SEG_ALIGN
    seg_end = jnp.cumsum(seg, axis=1)
    seg_start = seg_end - seg
    padded = (jnp.sum(seg, axis=0) + BM - 1) // BM * BM
    pad_ends = jnp.cumsum(padded)
    seg_dest = (pad_ends - padded)[None, :] + jnp.cumsum(seg, axis=0) - seg
    chunk_row = jnp.arange(NCHK, dtype=jnp.int32) * SEG_ALIGN
    chunk_exp = jnp.minimum(jnp.sum(chunk_row[None, :, None] >= seg_end[:, None, :], axis=-1),
                            N_EXPERTS - 1)
    own = chunk_exp[:, :, None] == jnp.arange(N_EXPERTS, dtype=jnp.int32)[None, None, :]
    chunk_dest = (jnp.sum(jnp.where(own, (seg_dest - seg_start)[:, None, :], 0), axis=-1)
                  + chunk_row[None, :]).astype(jnp.int32).reshape(N_TILES, 1, NCHK)
    nchk = (seg_end[:, -1] // SEG_ALIGN).astype(jnp.int32)
    block_start = jnp.arange(N_BLOCKS, dtype=jnp.int32) * BM
    block_expert = jnp.minimum(jnp.sum(block_start[:, None] >= pad_ends[None, :], axis=1),
                               N_EXPERTS - 1).astype(jnp.int32)
    n_used = (pad_ends[-1:] // BM).astype(jnp.int32)
    opens = (block_start < pad_ends[-1]) & (
        block_expert != jnp.concatenate([jnp.full((1,), -1, jnp.int32), block_expert[:-1]]))
    run_id = jnp.where(opens, jnp.cumsum(opens.astype(jnp.int32)) - 1, -1).astype(jnp.int32)
    experts = jnp.arange(N_EXPERTS, dtype=jnp.int32)
    later = (experts[None, :] > block_expert[:, None]) & (padded[None, :] > 0)
    next_expert = jnp.min(jnp.where(later, experts[None, :], N_EXPERTS), axis=1)
    next_expert = jnp.where(next_expert < N_EXPERTS, next_expert, -1).astype(jnp.int32)

    buf = _dispatch(pad_ends.astype(jnp.int32), padded.astype(jnp.int32), nchk, chunk_dest, pos_t, u2)
    mine = block_expert[:, None] == experts[None, :]
    filled = jnp.sum(jnp.where(mine, (pad_ends - padded + jnp.sum(seg, axis=0))[None, :], 0), axis=1)
    valid = jnp.clip(filled - block_start, 0, BM).astype(jnp.int32)
    y_buf = _ffn(block_expert, n_used, run_id, next_expert, valid, buf, w_up[0],
                 b_up[0].reshape(N_EXPERTS, 1, 2 * D), w_down[0], b_down[0].reshape(N_EXPERTS, 1, D))
    out = _combine(nchk, chunk_dest, y_buf, h1, gate_tok, g2, row(ln2_g[0]), row(ln2_b[0]))
    return out.reshape(1, N_TOK, D)
```

```python
import functools
import math

import jax
import jax.numpy as jnp
import numpy as np
from jax import lax
from jax.experimental import pallas as pl
from jax.experimental.pallas import tpu as pltpu

F32 = jnp.float32
BF16 = jnp.bfloat16
HI = lax.Precision.HIGHEST

D = 1024
N_TOK = 16384
N_CTX = 256
GRID_W = 64
S5_GROUP = 16
S5_GROUPS = 32
S5_STATE = 64
S5_WIDTH = 512
FFT_GROUPS = 4
FFT_DIM = 128
FFT_WIDTH = 512
N_EXPERTS = 32
TOP_K = 4
LN_EPS = 1e-5
ALPHA = 2.0 ** 0.25
SWIGLU_ALPHA = 1.702
SWIGLU_LIMIT = 7.0

LANES = 128
SUBLANES = 8
VMEM_LIMIT = 56 * 1024 * 1024

CH = 8
N_CHUNK = N_TOK // CH
N_CHUNK_CTX = N_CTX // CH
NSEG = SUBLANES
SEG = N_CHUNK // NSEG
SCAN_UNROLL = 4
GPT = LANES // S5_GROUP
NJ = S5_WIDTH // LANES
CL = CH * LANES
SW = 4 * GPT * S5_STATE

FN = 128
FB = 16

TM = 512
TM_PROJ = 1024
N_TILES = N_TOK // TM
MIX_GROUPS = 1
BM = 512
FFN_TAIL = 128
N_SLOTS = N_TOK * TOP_K
SEG_ALIGN = 2 * SUBLANES
CAP_BLOCK = 256
CAP = -(-(TOP_K * TM + N_EXPERTS * (SEG_ALIGN - 1)) // CAP_BLOCK) * CAP_BLOCK
NCHK = CAP // SEG_ALIGN
N_BLOCKS = -(-(N_SLOTS + N_TILES * N_EXPERTS * (SEG_ALIGN - 1)) // BM) + N_EXPERTS
ROWS = N_BLOCKS * BM


def _cparams(sem):
    return pltpu.CompilerParams(dimension_semantics=sem, vmem_limit_bytes=VMEM_LIMIT)


def _layer_norm(x, g, b):
    mu = jnp.mean(x, axis=-1, keepdims=True)
    xc = x - mu
    var = jnp.mean(xc * xc, axis=-1, keepdims=True)
    return xc * lax.rsqrt(var + LN_EPS) * g + b


def _sigmoid(x):
    return 1.0 / (1.0 + jnp.exp(-x))


def _ada_kernel(c_ref, w_ref, b_ref, o_ref):
    c = c_ref[...]
    s = c * _sigmoid(c)
    o_ref[...] = jnp.dot(s, w_ref[...], preferred_element_type=F32, precision=HI) + b_ref[...]


def _ada(cc, w_ada, b_ada):
    nb = 4
    wb = 6 * D // nb
    return pl.pallas_call(
        _ada_kernel,
        grid=(nb,),
        in_specs=[pl.BlockSpec((SUBLANES, D), lambda i: (0, 0)),
                  pl.BlockSpec((D, wb), lambda i: (0, i)),
                  pl.BlockSpec((1, wb), lambda i: (0, i))],
        out_specs=pl.BlockSpec((SUBLANES, wb), lambda i: (0, i)),
        out_shape=jax.ShapeDtypeStruct((SUBLANES, 6 * D), F32),
        compiler_params=_cparams(("parallel",)),
        name="ada",
    )(cc, w_ada, b_ada)


def _fftw_kernel(w_ref, b_ref, f_ref, wo_ref, bo_ref):
    f = f_ref[...]
    wo_ref[...] = jnp.dot(w_ref[...], f, preferred_element_type=F32, precision=HI)
    bo_ref[...] = jnp.dot(b_ref[...], f, preferred_element_type=F32, precision=HI)


def _fft_weights(w_fft, b_fft8, fc):
    return pl.pallas_call(
        _fftw_kernel,
        out_shape=(jax.ShapeDtypeStruct((D, 2 * FFT_WIDTH), F32),
                   jax.ShapeDtypeStruct((SUBLANES, 2 * FFT_WIDTH), F32)),
        compiler_params=pltpu.CompilerParams(vmem_limit_bytes=VMEM_LIMIT),
        name="fftw",
    )(w_fft, b_fft8, fc)


def _pos_code(er_ref, ec_ref, tm):
    nr = tm // GRID_W
    er = er_ref[...]
    row = jnp.broadcast_to(er[:, None, :], (nr, GRID_W, D // 2)).reshape(tm, D // 2)
    col = jnp.concatenate([ec_ref[...]] * nr, axis=0)
    return jnp.concatenate([row, col], axis=-1)


def _to_chunk_major(val, scr_ref, out_ref, tm):
    for j in range(NJ):
        scr_ref[j] = val[:, j * LANES:(j + 1) * LANES]
    for t in range(CH):
        for j in range(NJ):
            piece = scr_ref[j, pl.ds(t, tm // CH, stride=CH), :]
            out_ref[t, :, j * LANES:(j + 1) * LANES] = piece.astype(out_ref.dtype)


def _proj_kernel(x_ref, er_ref, ec_ref, lg_ref, lb_ref, m_ref, s_ref, w_ref, b_ref,
                 p_ref, xr_ref, xi_ref, scr_ref):
    x = x_ref[...] + _pos_code(er_ref, ec_ref, TM_PROJ)
    h = _layer_norm(x, lg_ref[...], lb_ref[...])
    u = (h * m_ref[...] + s_ref[...]).astype(BF16)
    p = jnp.dot(u, w_ref[...], preferred_element_type=F32) + b_ref[...]
    _to_chunk_major(p[:, :S5_WIDTH], scr_ref, p_ref, TM_PROJ)
    xr_ref[...] = p[:, S5_WIDTH:S5_WIDTH + FFT_WIDTH].astype(BF16)
    xi_ref[...] = p[:, S5_WIDTH + FFT_WIDTH:].astype(BF16)


def _proj(x, emb_r, emb_c, lg, lb, m1, s1, wcat, bcat):
    nw = wcat.shape[1]
    vec = pl.BlockSpec((1, D), lambda i: (0, 0))
    return pl.pallas_call(
        _proj_kernel,
        grid=(N_TOK // TM_PROJ,),
        in_specs=[pl.BlockSpec((TM_PROJ, D), lambda i: (i, 0)),
                  pl.BlockSpec((TM_PROJ // GRID_W, D // 2), lambda i: (i, 0)),
                  pl.BlockSpec((GRID_W, D // 2), lambda i: (0, 0)),
                  vec, vec, vec, vec,
                  pl.BlockSpec((D, nw), lambda i: (0, 0)),
                  pl.BlockSpec((1, nw), lambda i: (0, 0))],
        out_specs=(pl.BlockSpec((CH, TM_PROJ // CH, S5_WIDTH), lambda i: (0, i, 0)),
                   pl.BlockSpec((TM_PROJ, FFT_WIDTH), lambda i: (i, 0)),
                   pl.BlockSpec((TM_PROJ, FFT_WIDTH), lambda i: (i, 0))),
        out_shape=(jax.ShapeDtypeStruct((CH, N_CHUNK, S5_WIDTH), BF16),
                   jax.ShapeDtypeStruct((N_TOK, FFT_WIDTH), BF16),
                   jax.ShapeDtypeStruct((N_TOK, FFT_WIDTH), BF16)),
        scratch_shapes=[pltpu.VMEM((NJ, TM_PROJ, LANES), F32)],
        compiler_params=_cparams(("parallel",)),
        name="proj",
    )(x, emb_r, emb_c, lg, lb, m1, s1, wcat, bcat)


def _ctx_proj_kernel(x_ref, lg_ref, lb_ref, m_ref, s_ref, w_ref, b_ref, p_ref, scr_ref):
    h = _layer_norm(x_ref[...], lg_ref[...], lb_ref[...])
    u = (h * m_ref[...] + s_ref[...]).astype(BF16)
    p = jnp.dot(u, w_ref[...], preferred_element_type=F32) + b_ref[...]
    _to_chunk_major(p, scr_ref, p_ref, N_CTX)


def _ctx_proj(ctx, lg, lb, m1, s1, w_s5, b_s5):
    return pl.pallas_call(
        _ctx_proj_kernel,
        out_shape=jax.ShapeDtypeStruct((CH, N_CHUNK_CTX, S5_WIDTH), BF16),
        scratch_shapes=[pltpu.VMEM((NJ, N_CTX, LANES), F32)],
        compiler_params=pltpu.CompilerParams(vmem_limit_bytes=VMEM_LIMIT),
        name="ctxproj",
    )(ctx, lg, lb, m1, s1, w_s5, b_s5)


def _s5_tables(lam_re, lam_im, log_dt, b_re, b_im, c_re, c_im, d_skip):
    dt = jnp.exp(log_dt)[..., None]
    zr = lam_re * dt
    zi = lam_im * dt

    def apow(m):
        m = jnp.asarray(m, F32)
        mag = jnp.exp(zr[..., None] * m)
        return mag * jnp.cos(zi[..., None] * m), mag * jnp.sin(zi[..., None] * m)

    a_re, a_im = apow(jnp.ones((1,), F32))
    a_re, a_im = a_re[..., 0], a_im[..., 0]
    den = lam_re * lam_re + lam_im * lam_im
    num_re = a_re - 1.0
    k_re = (num_re * lam_re + a_im * lam_im) / den
    k_im = (a_im * lam_re - num_re * lam_im) / den
    bb_re = k_re[..., None] * b_re - k_im[..., None] * b_im
    bb_im = k_re[..., None] * b_im + k_im[..., None] * b_re

    ks = jnp.arange(CH + 1, dtype=F32)
    pw_re, pw_im = apow(ks)
    kmag = jnp.exp(zr[:, :, None, :] * ks[None, None, :, None])
    pk_re = kmag * jnp.cos(zi[:, :, None, :] * ks[None, None, :, None])
    pk_im = kmag * jnp.sin(zi[:, :, None, :] * ks[None, None, :, None])
    bt_re, bt_im = jnp.swapaxes(b_re, 2, 3), jnp.swapaxes(b_im, 2, 3)
    bbt_re = k_re[:, :, None, :] * bt_re - k_im[:, :, None, :] * bt_im
    bbt_im = k_re[:, :, None, :] * bt_im + k_im[:, :, None, :] * bt_re

    ar, ai = pk_re[:, :, :CH, None, :], pk_im[:, :, :CH, None, :]
    cr, ci = c_re[:, :, None, :, :], c_im[:, :, None, :, :]
    ca = jnp.concatenate([cr * ar - ci * ai, -(cr * ai + ci * ar)], axis=-1)
    bb = jnp.concatenate([bb_re, bb_im], axis=2)
    taps = jnp.einsum('dgkvq,dgqh->kdghv', ca, bb, precision=HI)
    skip = d_skip.reshape(S5_GROUPS, S5_GROUP, 1) * jnp.eye(S5_GROUP, dtype=F32)[None]
    taps = taps.at[0, 0].add(skip)
    b_c = jnp.transpose(taps.reshape(CH, 2, NJ, LANES, S5_GROUP), (2, 0, 1, 3, 4))
    b_c = b_c.reshape(NJ, 2 * CH, LANES, S5_GROUP)

    ef = (CH - 1) - jnp.arange(CH)
    eb = jnp.arange(CH)

    def q_part(d, e):
        pr = pk_re[d][:, e, None, :]
        pi = pk_im[d][:, e, None, :]
        br = bbt_re[d][:, None, :, :]
        bi = bbt_im[d][:, None, :, :]
        return pr * br - pi * bi, pr * bi + pi * br

    def q_rows(v):
        v = v.reshape(NJ, GPT, CH, S5_GROUP, S5_STATE)
        return jnp.transpose(v, (0, 2, 1, 3, 4)).reshape(NJ, CL, S5_STATE)
    a_q = jnp.stack([q_rows(v) for v in q_part(0, ef) + q_part(1, eb)], axis=0)

    of = jnp.arange(CH) + 1
    ob = CH - jnp.arange(CH)

    def p_part(d, e):
        pr = pw_re[d][..., e][:, :, :, None]
        pi = pw_im[d][..., e][:, :, :, None]
        return (ct_re[d] * pr - ct_im[d] * pi, -(ct_re[d] * pi + ct_im[d] * pr))
    ct_re = jnp.swapaxes(c_re, 2, 3)[:, :, :, None, :]
    ct_im = jnp.swapaxes(c_im, 2, 3)[:, :, :, None, :]
    a_p = jnp.stack([v.reshape(NJ, GPT * S5_STATE, CH * S5_GROUP)
                     for v in p_part(0, of) + p_part(1, ob)], axis=0)

    def lanes(v):
        return jnp.transpose(v.reshape(2, NJ, GPT * S5_STATE), (1, 0, 2))
    c_r, c_i = apow(jnp.full((1,), float(CH), F32))
    s_r, s_i = apow(jnp.full((1,), float(CH * SEG), F32))
    cr, ci, sr, si = (lanes(v[..., 0]) for v in (c_r, c_i, s_r, s_i))
    trans = jnp.stack([cr[:, 0], ci[:, 0], cr[:, 1], ci[:, 1],
                       sr[:, 0], si[:, 0], sr[:, 1], si[:, 1]], axis=1)

    cidx = jnp.arange(N_CHUNK_CTX, dtype=F32)
    wf_r, wf_i = apow(CH * (N_CHUNK_CTX - 1 - cidx))
    wb_r, wb_i = apow(CH * cidx)

    def ctx_lanes(v, d):
        return jnp.transpose(v[d].reshape(NJ, GPT * S5_STATE, N_CHUNK_CTX), (0, 2, 1))
    ctx_w = jnp.stack([ctx_lanes(wf_r, 0), ctx_lanes(wf_i, 0),
                       ctx_lanes(wb_r, 1), ctx_lanes(wb_i, 1)], axis=1)
    return b_c, a_q, a_p, trans, ctx_w


def _s5w_kernel(bc_ref, aq_ref, ap_ref, c16_ref, c64_ref, cm_ref, wm_ref, wq_ref, wp_ref):
    def expand(a, c, row_shift, col_shift):
        w = jnp.dot(a, c, preferred_element_type=F32)
        rg = (lax.broadcasted_iota(jnp.int32, (w.shape[0], 1), 0) >> row_shift) & (GPT - 1)
        cg = (lax.broadcasted_iota(jnp.int32, (1, w.shape[1]), 1) >> col_shift) & (GPT - 1)
        return jnp.where(rg == cg, w, 0.0)

    blk = [expand(bc_ref[0, kd], c16_ref[...], 4, 4) for kd in range(2 * CH)]
    for t in range(CH):
        for u in range(CH):
            b = blk[2 * (u - t)] if u > t else blk[2 * (t - u) + 1] if u < t else blk[0] + blk[1]
            wm_ref[0, t * LANES:(t + 1) * LANES, u * LANES:(u + 1) * LANES] = b.astype(BF16)
    half = GPT * S5_STATE
    for s in range(4):
        wq_ref[0, :, s * half:(s + 1) * half] = expand(aq_ref[s, 0], c64_ref[...], 4, 6).astype(BF16)
        wp_ref[0, s * half:(s + 1) * half, :] = expand(ap_ref[s, 0], cm_ref[...], 6, 4).astype(BF16)


def _s5_expand(b_c, a_q, a_p):
    rep = np.ones((1, GPT))
    c16 = jnp.asarray(np.kron(rep, np.eye(S5_GROUP)), F32).astype(BF16)
    c64 = jnp.asarray(np.kron(rep, np.eye(S5_STATE)), F32).astype(BF16)
    c_m = jnp.asarray(np.kron(np.eye(CH), np.kron(rep, np.eye(S5_GROUP))), F32).astype(BF16)
    b_c, a_q, a_p = b_c.astype(BF16), a_q.astype(BF16), a_p.astype(BF16)
    half = GPT * S5_STATE
    return pl.pallas_call(
        _s5w_kernel,
        grid=(NJ,),
        in_specs=[pl.BlockSpec((1, 2 * CH, LANES, S5_GROUP), lambda j: (j, 0, 0, 0)),
                  pl.BlockSpec((4, 1, CL, S5_STATE), lambda j: (0, j, 0, 0)),
                  pl.BlockSpec((4, 1, half, CH * S5_GROUP), lambda j: (0, j, 0, 0)),
                  pl.BlockSpec(c16.shape, lambda j: (0, 0)),
                  pl.BlockSpec(c64.shape, lambda j: (0, 0)),
                  pl.BlockSpec(c_m.shape, lambda j: (0, 0))],
        out_specs=(pl.BlockSpec((1, CL, CL), lambda j: (j, 0, 0)),
                   pl.BlockSpec((1, CL, SW), lambda j: (j, 0, 0)),
                   pl.BlockSpec((1, SW, CL), lambda j: (j, 0, 0))),
        out_shape=(jax.ShapeDtypeStruct((NJ, CL, CL), BF16),
                   jax.ShapeDtypeStruct((NJ, CL, SW), BF16),
                   jax.ShapeDtypeStruct((NJ, SW, CL), BF16)),
        compiler_params=_cparams(("parallel",)),
        name="s5w",
    )(b_c, a_q, a_p, c16, c64, c_m)


def _s5_kernel(p_ref, pc_ref, wm_ref, wq_ref, wp_ref, tr_ref, cw_ref, y_ref, v_ref):
    nq = NJ
    half = GPT * S5_STATE

    def chunk_rows(ref, r0, nrows):
        return jnp.concatenate([ref[t, pl.ds(r0, nrows), :] for t in range(CH)], axis=-1)

    def fill(k, c):
        r0 = pl.multiple_of(k * SEG, SEG)
        v = jnp.dot(chunk_rows(p_ref, r0, SEG), wq_ref[0], preferred_element_type=F32)
        for s in range(4 * nq):
            v_ref[s, pl.ds(k, SEG, stride=NSEG), :] = v[:, s * LANES:(s + 1) * LANES]
        return c
    lax.fori_loop(0, NSEG, fill, 0)

    vc = jnp.dot(chunk_rows(pc_ref, 0, N_CHUNK_CTX), wq_ref[0], preferred_element_type=F32)
    vfr, vfi, vbr, vbi = (vc[:, i * half:(i + 1) * half] for i in range(4))
    wfr, wfi, wbr, wbi = (cw_ref[0, i] for i in range(4))
    s0_fr = jnp.sum(wfr * vfr - wfi * vfi, axis=0, keepdims=True)
    s0_fi = jnp.sum(wfr * vfi + wfi * vfr, axis=0, keepdims=True)
    s0_br = jnp.sum(wbr * vbr - wbi * vbi, axis=0, keepdims=True)
    s0_bi = jnp.sum(wbr * vbi + wbi * vbr, axis=0, keepdims=True)

    tr = tr_ref[0]
    afr, afi, abr, abi = (jnp.broadcast_to(tr[i:i + 1], (NSEG, half)) for i in range(4))
    gfr, gfi, gbr, gbi = (tr[i:i + 1] for i in range(4, 8))

    def load_part(part, i):
        return jnp.concatenate(
            [v_ref[part * nq + q, pl.ds(pl.multiple_of(i * NSEG, NSEG), NSEG), :] for q in range(nq)],
            axis=-1)

    def store_part(part, i, val):
        for q in range(nq):
            v_ref[part * nq + q, pl.ds(pl.multiple_of(i * NSEG, NSEG), NSEG), :] = (
                val[:, q * LANES:(q + 1) * LANES])

    def step(i, carry, write):
        fr, fi, br, bi = carry
        ib = SEG - 1 - i
        ufr, ufi = load_part(0, i), load_part(1, i)
        ubr, ubi = load_part(2, ib), load_part(3, ib)
        if write:
            store_part(0, i, fr)
            store_part(1, i, fi)
            store_part(2, ib, br)
            store_part(3, ib, bi)
        return (afr * fr - afi * fi + ufr, afr * fi + afi * fr + ufi,
                abr * br - abi * bi + ubr, abr * bi + abi * br + ubi)

    zero = jnp.zeros((NSEG, half), F32)
    ffr, ffi, fbr, fbi = lax.fori_loop(0, SEG, functools.partial(step, write=False),
                                       (zero, zero, zero, zero), unroll=SCAN_UNROLL)

    rows_fr, rows_fi = [s0_fr], [s0_fi]
    for k in range(1, NSEG):
        pr, pi = rows_fr[-1], rows_fi[-1]
        rows_fr.append(gfr * pr - gfi * pi + ffr[k - 1:k])
        rows_fi.append(gfr * pi + gfi * pr + ffi[k - 1:k])
    rows_br, rows_bi = [s0_br], [s0_bi]
    for k in range(NSEG - 2, -1, -1):
        pr, pi = rows_br[0], rows_bi[0]
        rows_br.insert(0, gbr * pr - gbi * pi + fbr[k + 1:k + 2])
        rows_bi.insert(0, gbr * pi + gbi * pr + fbi[k + 1:k + 2])
    init = tuple(jnp.concatenate(r, axis=0) for r in (rows_fr, rows_fi, rows_br, rows_bi))

    lax.fori_loop(0, SEG, functools.partial(step, write=True), init, unroll=SCAN_UNROLL)

    def emit(k, c):
        r0 = pl.multiple_of(k * SEG, SEG)
        b = chunk_rows(p_ref, r0, SEG)
        sin = jnp.concatenate([v_ref[s, pl.ds(k, SEG, stride=NSEG), :] for s in range(4 * nq)], axis=-1)
        y = (jnp.dot(b, wm_ref[0], preferred_element_type=F32)
             + jnp.dot(sin.astype(BF16), wp_ref[0], preferred_element_type=F32))
        for t in range(CH):
            y_ref[t, pl.ds(r0, SEG), :] = y[:, t * LANES:(t + 1) * LANES].astype(y_ref.dtype)
        return c
    lax.fori_loop(0, NSEG, emit, 0)


def _s5(p_t, pc_t, w_m, w_q, w_p, trans, ctx_w):
    one = pl.Buffered(1)
    return pl.pallas_call(
        _s5_kernel,
        grid=(NJ,),
        in_specs=[pl.BlockSpec((CH, N_CHUNK, LANES), lambda j: (0, 0, j)),
                  pl.BlockSpec((CH, N_CHUNK_CTX, LANES), lambda j: (0, 0, j)),
                  pl.BlockSpec((1, CL, CL), lambda j: (j, 0, 0), pipeline_mode=one),
                  pl.BlockSpec((1, CL, SW), lambda j: (j, 0, 0), pipeline_mode=one),
                  pl.BlockSpec((1, SW, CL), lambda j: (j, 0, 0), pipeline_mode=one),
                  pl.BlockSpec((1, SUBLANES, GPT * S5_STATE), lambda j: (j, 0, 0)),
                  pl.BlockSpec((1, 4, N_CHUNK_CTX, GPT * S5_STATE), lambda j: (j, 0, 0, 0))],
        out_specs=pl.BlockSpec((CH, N_CHUNK, LANES), lambda j: (0, 0, j)),
        out_shape=jax.ShapeDtypeStruct((CH, N_CHUNK, S5_WIDTH), BF16),
        scratch_shapes=[pltpu.VMEM((4 * NJ, N_CHUNK, LANES), F32)],
        compiler_params=_cparams(("parallel",)),
        name="s5",
    )(p_t, pc_t, w_m, w_q, w_p, trans, ctx_w)


def _dft_tables():
    n = np.arange(FN)
    ang = 2.0 * np.pi * np.outer(n, n) / FN
    c, s = np.cos(ang), np.sin(ang)
    st1 = np.block([[c, s], [-s, c]])
    tw = 2.0 * np.pi * np.outer(n, n) / (FN * FN)
    wr, wi = np.cos(tw), -np.sin(tw)
    fr = c[None] * wr[:, None, :] + s[None] * wi[:, None, :]
    fi = c[None] * wi[:, None, :] - s[None] * wr[:, None, :]
    st2 = np.concatenate([fr, -fi], axis=-1)
    scale = 1.0 / math.sqrt(N_TOK * FFT_DIM)
    blk_c = np.kron(np.eye(FFT_GROUPS), c) * scale
    blk_s = np.kron(np.eye(FFT_GROUPS), s) * scale
    fc = np.concatenate([blk_c, -blk_s], axis=1)
    return (jnp.asarray(st1, F32).astype(BF16), jnp.asarray(st2, F32).astype(BF16), jnp.asarray(fc, F32))


FSL = FFT_WIDTH // LANES


FBH = FB // SUBLANES


def _block_to_slabs(blk, slab_ref, first, per_half):
    for bh in range(FBH):
        val = blk[:, bh * SUBLANES:(bh + 1) * SUBLANES, :].reshape(FN * SUBLANES, FFT_WIDTH)
        for s in range(FSL):
            slab_ref[bh * per_half + first + s] = val[:, s * LANES:(s + 1) * LANES]


def _slab_rows(b, first, per_half):
    return (b // SUBLANES) * per_half + first, pl.ds(b % SUBLANES, FN, stride=SUBLANES)


def _slabs_to_block(slab_ref, first, per_half):
    halves = []
    for bh in range(FBH):
        val = jnp.concatenate([slab_ref[bh * per_half + first + s] for s in range(FSL)], axis=-1)
        halves.append(val.reshape(FN, SUBLANES, FFT_WIDTH))
    return jnp.concatenate(halves, axis=1)


def _fft1_kernel(xr_ref, xi_ref, f_ref, yr_ref, yi_ref, in_ref, out_ref):
    _block_to_slabs(xr_ref[...].astype(F32), in_ref, 0, 2 * FSL)
    _block_to_slabs(xi_ref[...].astype(F32), in_ref, FSL, 2 * FSL)
    for b in range(FB):
        def part(first):
            base, rows = _slab_rows(b, first, 2 * FSL)
            return jnp.concatenate([in_ref[base + s, rows, :] for s in range(FSL)], axis=-1)
        xs = jnp.concatenate([part(0), part(FSL)], axis=0).astype(BF16)
        y = jnp.dot(f_ref[...], xs, preferred_element_type=F32)
        base, rows = _slab_rows(b, 0, 2 * FSL)
        for s in range(FSL):
            out_ref[base + s, rows, :] = y[:FN, s * LANES:(s + 1) * LANES]
            out_ref[base + FSL + s, rows, :] = y[FN:, s * LANES:(s + 1) * LANES]
    yr_ref[...] = _slabs_to_block(out_ref, 0, 2 * FSL).astype(BF16)
    yi_ref[...] = _slabs_to_block(out_ref, FSL, 2 * FSL).astype(BF16)


def _fft1(xr, xi, st1):
    spec = pl.BlockSpec((FN, FB, FFT_WIDTH), lambda i: (0, i, 0))
    slabs = pltpu.VMEM((FBH * 2 * FSL, FN * SUBLANES, LANES), F32)
    return pl.pallas_call(
        _fft1_kernel,
        grid=(FN // FB,),
        in_specs=[spec, spec, pl.BlockSpec((2 * FN, 2 * FN), lambda i: (0, 0))],
        out_specs=(spec, spec),
        out_shape=(jax.ShapeDtypeStruct((FN, FN, FFT_WIDTH), BF16),) * 2,
        scratch_shapes=[slabs, slabs],
        compiler_params=_cparams(("parallel",)),
        name="fft1",
    )(xr.reshape(FN, FN, FFT_WIDTH), xi.reshape(FN, FN, FFT_WIDTH), st1)


def _fft2_kernel(yr_ref, yi_ref, f_ref, z_ref, out_ref):
    for b in range(FB):
        ys = jnp.concatenate([yr_ref[b * FN:(b + 1) * FN, :], yi_ref[b * FN:(b + 1) * FN, :]], axis=0)
        z = jnp.dot(f_ref[b], ys, preferred_element_type=F32)
        base, rows = _slab_rows(b, 0, FSL)
        for s in range(FSL):
            out_ref[base + s, rows, :] = z[:, s * LANES:(s + 1) * LANES]
    z_ref[...] = _slabs_to_block(out_ref, 0, FSL).astype(BF16)


def _fft2(yr, yi, st2):
    rows = pl.BlockSpec((FB * FN, FFT_WIDTH), lambda i: (i, 0))
    z = pl.pallas_call(
        _fft2_kernel,
        grid=(FN // FB,),
        in_specs=[rows, rows, pl.BlockSpec((FB, FN, 2 * FN), lambda i: (i, 0, 0))],
        out_specs=pl.BlockSpec((FN, FB, FFT_WIDTH), lambda i: (0, i, 0)),
        out_shape=jax.ShapeDtypeStruct((FN, FN, FFT_WIDTH), BF16),
        scratch_shapes=[pltpu.VMEM((FBH * FSL, FN * SUBLANES, LANES), F32)],
        compiler_params=_cparams(("parallel",)),
        name="fft2",
    )(yr.reshape(N_TOK, FFT_WIDTH), yi.reshape(N_TOK, FFT_WIDTH), st2)
    return z.reshape(N_TOK, FFT_WIDTH)


def _gelu_tanh(x):
    return 0.5 * x * (1.0 + jnp.tanh(math.sqrt(2.0 / math.pi) * (x + 0.044715 * (x * x * x))))


def _mix_kernel(x_ref, er_ref, ec_ref, lg_ref, lb_ref, m1_ref, s1_ref, wg_ref, bg_ref,
                yt_ref, zr_ref, wglu_ref, bglu_ref, wbs_ref, wbf_ref, bbf_ref, wo_ref, bo_ref,
                g1_ref, l1g_ref, l1b_ref, m2_ref, s2_ref, wr_ref, br_ref, tri_ref, etri_ref,
                h1_ref, u2_ref, pos_ref, gate_ref, cnt_ref, scr_ref):
    pos = _pos_code(er_ref, ec_ref, TM)

    def front(r0, nr):
        rows = slice(r0, r0 + nr)
        h = _layer_norm(x_ref[rows, :] + pos[rows, :], lg_ref[...], lb_ref[...])
        u = (h * m1_ref[...] + s1_ref[...]).astype(BF16)

        c0, nc = r0 // CH, nr // CH
        for t in range(CH):
            for j in range(NJ):
                scr_ref[j, pl.ds(r0 + t, nc, stride=CH), :] = (
                    yt_ref[t, c0:c0 + nc, j * LANES:(j + 1) * LANES].astype(F32))
        ys = jnp.concatenate([scr_ref[j, rows, :] for j in range(NJ)], axis=-1)
        z = jnp.dot(_gelu_tanh(ys).astype(BF16), wglu_ref[...], preferred_element_type=F32) + bglu_ref[...]
        glu = (z[:, :S5_WIDTH] * _sigmoid(z[:, S5_WIDTH:])).astype(BF16)
        gates = _sigmoid(jnp.dot(u, wg_ref[...], preferred_element_type=F32) + bg_ref[...])
        y_s5 = jnp.dot(glu, wbs_ref[...], preferred_element_type=F32)
        y_fft = jnp.dot(zr_ref[rows, :], wbf_ref[...], preferred_element_type=F32) + bbf_ref[...]
        mixed = (gates[:, :D] * y_s5 + gates[:, D:] * y_fft).astype(BF16)
        y = jnp.dot(mixed, wo_ref[...], preferred_element_type=F32) + bo_ref[...]
        h1 = _layer_norm(ALPHA * h + g1_ref[...] * y, l1g_ref[...], l1b_ref[...])
        h1_ref[rows, :] = h1
        u2 = h1 * m2_ref[...] + s2_ref[...]
        u2_ref[rows, :] = u2.astype(BF16)
        u_hi = u2.astype(BF16)
        u_lo = (u2 - u_hi.astype(F32)).astype(BF16)

        def nt(a, b):
            return lax.dot_general(a, b, (((1,), (1,)), ((), ())), preferred_element_type=F32)
        return nt(wr_ref[0], u_hi) + nt(wr_ref[0], u_lo) + nt(wr_ref[1], u_hi)

    nr = TM // MIX_GROUPS
    logits = jnp.concatenate([front(g * nr, nr) for g in range(MIX_GROUPS)], axis=-1) + br_ref[:, 0:1]
    eidx = lax.broadcasted_iota(jnp.int32, (N_EXPERTS, TM), 0)
    vals, hots = [], []
    cur = logits
    for _k in range(TOP_K):
        m = jnp.max(cur, axis=0, keepdims=True)
        sel = jnp.min(jnp.where(cur == m, eidx, N_EXPERTS), axis=0, keepdims=True)
        hot = eidx == sel
        cur = jnp.where(hot, -jnp.inf, cur)
        vals.append(m)
        hots.append(hot)
    exps = [jnp.exp(v - vals[0]) for v in vals]
    den = exps[0] + exps[1] + exps[2] + exps[3]
    gate4 = jnp.concatenate([e / den for e in exps], axis=0)

    hot_sum = (hots[0] | hots[1] | hots[2] | hots[3]).astype(F32)
    before = jnp.dot(hot_sum.astype(BF16), tri_ref[...], preferred_element_type=F32)
    cnt = jnp.broadcast_to(jnp.sum(hot_sum, axis=1, keepdims=True), (N_EXPERTS, LANES))
    cnt8 = jnp.floor((cnt + (SEG_ALIGN - 1)) * (1.0 / SEG_ALIGN)) * SEG_ALIGN
    seg0 = jnp.dot(etri_ref[...], cnt8.astype(BF16), preferred_element_type=F32)
    tot = seg0[:, 0:1] + before
    pos4 = jnp.concatenate(
        [jnp.sum(jnp.where(hk, tot, 0.0), axis=0, keepdims=True) for hk in hots], axis=0)
    pos_ref[...] = pos4.astype(jnp.int32)
    cnt_ref[0] = cnt

    gpad = jnp.concatenate([gate4, pos4, jnp.zeros((LANES - 2 * TOP_K, TM), F32)], axis=0)
    gate_ref[...] = gpad.T


def _mix(x, emb_r, emb_c, lg, lb, m1, s1, wg, bg, y_t, zr, wglu, bglu, wbs, wbf, bbf, wo, bo,
         g1, l1g, l1b, m2, s2, wr_t, br, tri, etri):
    vec = pl.BlockSpec((1, D), lambda i: (0, 0))

    def full(a):
        return pl.BlockSpec(a.shape, lambda i: (0,) * a.ndim)
    return pl.pallas_call(
        _mix_kernel,
        grid=(N_TOK // TM,),
        in_specs=[pl.BlockSpec((TM, D), lambda i: (i, 0)),
                  pl.BlockSpec((TM // GRID_W, D // 2), lambda i: (i, 0)),
                  pl.BlockSpec((GRID_W, D // 2), lambda i: (0, 0)),
                  vec, vec, vec, vec, full(wg), full(bg),
                  pl.BlockSpec((CH, TM // CH, S5_WIDTH), lambda i: (0, i, 0)),
                  pl.BlockSpec((TM, FFT_WIDTH), lambda i: (i, 0)),
                  full(wglu), full(bglu), full(wbs), full(wbf), full(bbf), full(wo), full(bo),
                  vec, vec, vec, vec, vec, full(wr_t), full(br), full(tri), full(etri)],
        out_specs=(pl.BlockSpec((TM, D), lambda i: (i, 0)),
                   pl.BlockSpec((TM, D), lambda i: (i, 0)),
                   pl.BlockSpec((TOP_K, TM), lambda i: (0, i)),
                   pl.BlockSpec((TM, LANES), lambda i: (i, 0)),
                   pl.BlockSpec((1, N_EXPERTS, LANES), lambda i: (i, 0, 0))),
        out_shape=(jax.ShapeDtypeStruct((N_TOK, D), F32),
                   jax.ShapeDtypeStruct((N_TOK, D), BF16),
                   jax.ShapeDtypeStruct((TOP_K, N_TOK), jnp.int32),
                   jax.ShapeDtypeStruct((N_TOK, LANES), F32),
                   jax.ShapeDtypeStruct((N_TILES, N_EXPERTS, LANES), F32)),
        scratch_shapes=[pltpu.VMEM((NJ, TM, LANES), F32)],
        compiler_params=_cparams(("parallel",)),
        name="mix",
    )(x, emb_r, emb_c, lg, lb, m1, s1, wg, bg, y_t, zr, wglu, bglu, wbs, wbf, bbf, wo, bo,
      g1, l1g, l1b, m2, s2, wr_t, br, tri, etri)


def _on_parity(i, fn):
    @pl.when(i % 2 == 0)
    def _():
        fn(0)

    @pl.when(i % 2 == 1)
    def _():
        fn(1)


def _dispatch_kernel(pend_ref, padded_ref, nchk_ref, dest_ref, pos_ref, u_ref, buf_ref,
                     sorted_ref, zero_ref, zsem, sems):
    i = pl.program_id(0)

    @pl.when(i == 0)
    def _():
        zero_ref[...] = jnp.zeros_like(zero_ref)
        n_used = pend_ref[N_EXPERTS - 1] // BM

        def clear_copy(start):
            return pltpu.make_async_copy(
                zero_ref, buf_ref.at[pl.ds(pl.multiple_of(start, BM), BM)], zsem)

        def each(fn):
            def expert(e, c):
                @pl.when(padded_ref[e] > 0)
                def _():
                    fn(clear_copy(pend_ref[e] - BM))
                return c
            lax.fori_loop(0, N_EXPERTS, expert, 0)

            def tail(b, c):
                fn(clear_copy(b * BM))
                return c
            lax.fori_loop(n_used, N_BLOCKS, tail, 0)
        each(lambda cp: cp.start())
        each(lambda cp: cp.wait())

    def drain(slot, tile):
        n = pl.multiple_of(nchk_ref[tile] * SEG_ALIGN, SEG_ALIGN)
        pltpu.make_async_copy(sorted_ref.at[slot, pl.ds(0, n)], buf_ref.at[pl.ds(0, n)],
                              sems.at[slot]).wait()

    def run(slot):
        pos = pos_ref[...]
        u = u_ref[...]
        for rb in range(CAP // CAP_BLOCK):
            rows = lax.broadcasted_iota(jnp.int32, (CAP_BLOCK, TM), 0) + rb * CAP_BLOCK
            hit = rows == pos[0:1]
            for k in range(1, TOP_K):
                hit = hit | (rows == pos[k:k + 1])
            onehot = jnp.where(hit, 1.0, 0.0).astype(BF16)
            sorted_ref[slot, rb * CAP_BLOCK:(rb + 1) * CAP_BLOCK, :] = jnp.dot(
                onehot, u, preferred_element_type=F32).astype(BF16)

        def issue(j, c):
            src = pl.multiple_of(j * SEG_ALIGN, SEG_ALIGN)
            dst = pl.multiple_of(dest_ref[0, 0, j], SEG_ALIGN)
            pltpu.make_async_copy(sorted_ref.at[slot, pl.ds(src, SEG_ALIGN)],
                                  buf_ref.at[pl.ds(dst, SEG_ALIGN)], sems.at[slot]).start()
            return c
        lax.fori_loop(0, nchk_ref[i], issue, 0)

        @pl.when(i > 0)
        def _():
            drain(1 - slot, i - 1)

        @pl.when(i == N_TILES - 1)
        def _():
            drain(slot, i)
    _on_parity(i, run)


def _dispatch(pad_ends, padded, nchk, chunk_dest, pos_t, u2):
    return pl.pallas_call(
        _dispatch_kernel,
        grid_spec=pltpu.PrefetchScalarGridSpec(
            num_scalar_prefetch=3,
            grid=(N_TILES,),
            in_specs=[pl.BlockSpec((1, 1, NCHK), lambda i, a, b, c: (i, 0, 0),
                                   memory_space=pltpu.SMEM),
                      pl.BlockSpec((TOP_K, TM), lambda i, a, b, c: (0, i)),
                      pl.BlockSpec((TM, D), lambda i, a, b, c: (i, 0))],
            out_specs=pl.BlockSpec(memory_space=pl.ANY),
            scratch_shapes=[pltpu.VMEM((2, CAP, D), BF16),
                            pltpu.VMEM((BM, D), BF16),
                            pltpu.SemaphoreType.DMA(()),
                            pltpu.SemaphoreType.DMA((2,))]),
        out_shape=jax.ShapeDtypeStruct((ROWS, D), BF16),
        compiler_params=_cparams(("arbitrary",)),
        name="dispatch",
    )(pad_ends, padded, nchk, chunk_dest, pos_t, u2)


def _ffn_kernel(be_ref, nu_ref, run_ref, nxt_ref, valid_ref, x_ref, wu_hbm, bu_ref, wd_hbm, bd_ref,
                y_ref, wu_ref, wd_ref, wub_ref, wdb_ref, sems):
    i = pl.program_id(0)
    used = i < nu_ref[0]

    def weight_copies(e, slot):
        return (pltpu.make_async_copy(wu_hbm.at[e], wu_ref.at[slot], sems.at[slot]),
                pltpu.make_async_copy(wd_hbm.at[e], wd_ref.at[slot], sems.at[slot]))

    @pl.when(used)
    def _():
        run = run_ref[i]

        @pl.when(run >= 0)
        def _():
            def open_run(slot):
                @pl.when(run == 0)
                def _():
                    for cp in weight_copies(be_ref[i], slot):
                        cp.start()

                @pl.when(nxt_ref[i] >= 0)
                def _():
                    for cp in weight_copies(nxt_ref[i], 1 - slot):
                        cp.start()
                for cp in weight_copies(be_ref[i], slot):
                    cp.wait()
                wub_ref[...] = wu_ref[slot].astype(BF16)
                wdb_ref[...] = wd_ref[slot].astype(BF16)
            _on_parity(run, open_run)

        def expert_rows(r0, nr):
            rows = slice(r0, r0 + nr)
            h = jnp.dot(x_ref[rows, :], wub_ref[...], preferred_element_type=F32) + bu_ref[0]
            h_glu = jnp.minimum(h[:, :D], SWIGLU_LIMIT)
            h_lin = jnp.clip(h[:, D:], -SWIGLU_LIMIT, SWIGLU_LIMIT)
            act = (h_glu * _sigmoid(SWIGLU_ALPHA * h_glu) * (h_lin + 1.0)).astype(BF16)
            y_ref[rows, :] = (jnp.dot(act, wdb_ref[...], preferred_element_type=F32)
                              + bd_ref[0]).astype(BF16)

        valid = valid_ref[i]

        @pl.when(valid == BM)
        def _():
            expert_rows(0, BM)

        @pl.when(valid < BM)
        def _():
            for r0 in range(0, BM, FFN_TAIL):
                @pl.when(r0 < valid)
                def _(r0=r0):
                    expert_rows(r0, FFN_TAIL)

                @pl.when(r0 >= valid)
                def _(r0=r0):
                    y_ref[r0:r0 + FFN_TAIL, :] = jnp.zeros((FFN_TAIL, D), BF16)

    @pl.when(jnp.logical_not(used))
    def _():
        y_ref[...] = jnp.zeros_like(y_ref)


def _ffn(block_expert, n_used, run_id, next_expert, valid, buf, w_up, b_up, w_down, b_down):
    def blk(i, be, nu, *_):
        return jnp.minimum(i, nu[0] - 1)
    return pl.pallas_call(
        _ffn_kernel,
        grid_spec=pltpu.PrefetchScalarGridSpec(
            num_scalar_prefetch=5,
            grid=(N_BLOCKS,),
            in_specs=[pl.BlockSpec((BM, D), lambda i, *s: (blk(i, *s), 0)),
                      pl.BlockSpec(memory_space=pl.ANY),
                      pl.BlockSpec((1, 1, 2 * D), lambda i, *s: (s[0][blk(i, *s)], 0, 0)),
                      pl.BlockSpec(memory_space=pl.ANY),
                      pl.BlockSpec((1, 1, D), lambda i, *s: (s[0][blk(i, *s)], 0, 0))],
            out_specs=pl.BlockSpec((BM, D), lambda i, *s: (i, 0)),
            scratch_shapes=[pltpu.VMEM((2, D, 2 * D), F32),
                            pltpu.VMEM((2, D, D), F32),
                            pltpu.VMEM((D, 2 * D), BF16),
                            pltpu.VMEM((D, D), BF16),
                            pltpu.SemaphoreType.DMA((2,))]),
        out_shape=jax.ShapeDtypeStruct((ROWS, D), BF16),
        compiler_params=_cparams(("arbitrary",)),
        name="ffn",
    )(block_expert, n_used, run_id, next_expert, valid, buf, w_up, b_up, w_down, b_down)


def _combine_kernel(nchk_ref, dest_ref, dnext_ref, y_ref, h1_ref, gate_ref, g2_ref, lg_ref, lb_ref,
                    o_ref, sorted_ref, sems):
    i = pl.program_id(0)

    def fetch(slot, tile, table_ref):
        def issue(j, c):
            src = pl.multiple_of(table_ref[0, 0, j], SEG_ALIGN)
            dst = pl.multiple_of(j * SEG_ALIGN, SEG_ALIGN)
            pltpu.make_async_copy(y_ref.at[pl.ds(src, SEG_ALIGN)],
                                  sorted_ref.at[slot, pl.ds(dst, SEG_ALIGN)], sems.at[slot]).start()
            return c
        lax.fori_loop(0, nchk_ref[tile], issue, 0)

    def drain(slot, tile):
        n = pl.multiple_of(nchk_ref[tile] * SEG_ALIGN, SEG_ALIGN)
        pltpu.make_async_copy(y_ref.at[pl.ds(0, n)], sorted_ref.at[slot, pl.ds(0, n)],
                              sems.at[slot]).wait()

    @pl.when(i == 0)
    def _():
        sorted_ref[...] = jnp.zeros_like(sorted_ref)
        fetch(0, 0, dest_ref)

    def run(slot):
        @pl.when(i + 1 < N_TILES)
        def _():
            fetch(1 - slot, i + 1, dnext_ref)
        drain(slot, i)

        gp = gate_ref[...]
        m = jnp.zeros((TM, D), F32)
        for cb in range(CAP // CAP_BLOCK):
            cols = (lax.broadcasted_iota(jnp.int32, (TM, CAP_BLOCK), 1) + cb * CAP_BLOCK).astype(F32)
            g = jnp.where(cols == gp[:, TOP_K:TOP_K + 1], gp[:, 0:1], 0.0)
            for k in range(1, TOP_K):
                g = g + jnp.where(cols == gp[:, TOP_K + k:TOP_K + k + 1], gp[:, k:k + 1], 0.0)
            rows = sorted_ref[slot, cb * CAP_BLOCK:(cb + 1) * CAP_BLOCK, :]
            m = m + jnp.dot(g.astype(BF16), rows, preferred_element_type=F32)
        o_ref[...] = _layer_norm(ALPHA * h1_ref[...] + g2_ref[...] * m, lg_ref[...], lb_ref[...])
    _on_parity(i, run)


def _combine(nchk, chunk_dest, y_buf, h1, gate_tok, g2, lg, lb):
    vec = pl.BlockSpec((1, D), lambda i, n: (0, 0))
    return pl.pallas_call(
        _combine_kernel,
        grid_spec=pltpu.PrefetchScalarGridSpec(
            num_scalar_prefetch=1,
            grid=(N_TILES,),
            in_specs=[pl.BlockSpec((1, 1, NCHK), lambda i, n: (i, 0, 0), memory_space=pltpu.SMEM),
                      pl.BlockSpec((1, 1, NCHK), lambda i, n: (jnp.minimum(i + 1, N_TILES - 1), 0, 0),
                                   memory_space=pltpu.SMEM),
                      pl.BlockSpec(memory_space=pl.ANY),
                      pl.BlockSpec((TM, D), lambda i, n: (i, 0)),
                      pl.BlockSpec((TM, LANES), lambda i, n: (i, 0)),
                      vec, vec, vec],
            out_specs=pl.BlockSpec((TM, D), lambda i, n: (i, 0)),
            scratch_shapes=[pltpu.VMEM((2, CAP, D), BF16),
                            pltpu.SemaphoreType.DMA((2,))]),
        out_shape=jax.ShapeDtypeStruct((N_TOK, D), F32),
        compiler_params=_cparams(("arbitrary",)),
        name="combine",
    )(nchk, chunk_dest, chunk_dest, y_buf, h1, gate_tok, g2, lg, lb)


def _sincos_tables():
    q = D // 4
    omega = 1.0 / (10000.0 ** (jnp.arange(q, dtype=F32) / q))

    def emb(n):
        ang = jnp.arange(n, dtype=F32)[:, None] * omega[None, :]
        return jnp.concatenate([jnp.sin(ang), jnp.cos(ang)], axis=-1)
    return emb(N_TOK // GRID_W), emb(GRID_W)


def _tile_slots(a, tile):
    return jnp.transpose(a.reshape(TOP_K, N_TOK // tile, tile), (1, 0, 2)).reshape(N_TOK // tile, 1, TOP_K * tile)


def kernel(x, c, ctx, c_ctx, ln_in_g, ln_in_b, w_ada, b_ada, w_in, b_in, s5_lambda_re, s5_lambda_im, s5_log_dt, s5_b_re, s5_b_im, s5_c_re, s5_c_im, s5_d, w_glu, b_glu, w_br_s5, w_br_fft, b_br_fft, w_out, b_out, ln1_g, ln1_b, w_router, b_router, w_up, b_up, w_down, b_down, ln2_g, ln2_b):
    assert x.shape == (1, N_TOK, D) and ctx.shape == (1, N_CTX, D) and w_ada.shape[0] == 1
    row = lambda v: v.reshape(1, -1).astype(F32)

    cc = jnp.concatenate([c.reshape(1, D), c_ctx.reshape(1, D), jnp.zeros((SUBLANES - 2, D), F32)], axis=0)
    ada = _ada(cc, w_ada[0], row(b_ada[0]))
    sh1, sc1, g1, sh2, sc2, g2 = (ada[0:1, k * D:(k + 1) * D] for k in range(6))
    sh1c, sc1c = ada[1:2, 0:D], ada[1:2, D:2 * D]

    emb_r, emb_c = _sincos_tables()
    st1, st2, fc = _dft_tables()
    lg, lb = row(ln_in_g), row(ln_in_b)

    w_s5 = w_in[0][:, :S5_WIDTH]
    w_fft = w_in[0][:, S5_WIDTH:S5_WIDTH + FFT_WIDTH]
    w_g = w_in[0][:, S5_WIDTH + FFT_WIDTH:]
    b_s5 = row(b_in[0][:S5_WIDTH])
    b_fft8 = jnp.concatenate([row(b_in[0][S5_WIDTH:S5_WIDTH + FFT_WIDTH]),
                              jnp.zeros((SUBLANES - 1, FFT_WIDTH), F32)], axis=0)
    b_g = row(b_in[0][S5_WIDTH + FFT_WIDTH:])
    w_fc, b_fc = _fft_weights(w_fft, b_fft8, fc)
    wcat = jnp.concatenate([w_s5, w_fc], axis=1).astype(BF16)
    bcat = jnp.concatenate([b_s5, b_fc[0:1]], axis=1)

    x2 = x[0]
    p_t, xr, xi = _proj(x2, emb_r, emb_c, lg, lb, 1.0 + sc1, sh1, wcat, bcat)
    pc_t = _ctx_proj(ctx[0], lg, lb, 1.0 + sc1c, sh1c, w_s5.astype(BF16), b_s5)

    b_c, a_q, a_p, trans, ctx_w = _s5_tables(
        s5_lambda_re[0], s5_lambda_im[0], s5_log_dt[0], s5_b_re[0], s5_b_im[0],
        s5_c_re[0], s5_c_im[0], s5_d[0])
    w_m, w_q, w_p = _s5_expand(b_c, a_q, a_p)
    y_t = _s5(p_t, pc_t, w_m, w_q, w_p, trans, ctx_w)

    yr, yi = _fft1(xr, xi, st1)
    zr = _fft2(yr, yi, st2)

    tri = (jnp.arange(TM)[:, None] < jnp.arange(TM)[None, :]).astype(BF16)
    br = jnp.broadcast_to(b_router[0].reshape(N_EXPERTS, 1), (N_EXPERTS, LANES))
    etri = (jnp.arange(N_EXPERTS)[:, None] > jnp.arange(N_EXPERTS)[None, :]).astype(BF16)
    wr_t = jnp.transpose(w_router[0])
    wr_hi = wr_t.astype(BF16)
    wr_split = jnp.stack([wr_hi, (wr_t - wr_hi.astype(F32)).astype(BF16)], axis=0)
    h1, u2, pos_t, gate_tok, counts = _mix(
        x2, emb_r, emb_c, lg, lb, 1.0 + sc1, sh1, w_g.astype(BF16), b_g, y_t, zr,
        w_glu[0].astype(BF16), row(b_glu[0]), w_br_s5[0].astype(BF16), w_br_fft[0].astype(BF16),
        row(b_br_fft[0]), w_out[0].astype(BF16), row(b_out[0]), g1, row(ln1_g[0]), row(ln1_b[0]),
        1.0 + sc2, sh2, wr_split, br, tri, etri)

    cnt = counts[:, :, 0].astype(jnp.int32)
    seg = (cnt + SEG_ALIGN - 1) // SEG_ALIGN * SEG_ALIGN
    seg_end = jnp.cumsum(seg, axis=1)
    seg_start = seg_end - seg
    padded = (jnp.sum(seg, axis=0) + BM - 1) // BM * BM
    pad_ends = jnp.cumsum(padded)
    seg_dest = (pad_ends - padded)[None, :] + jnp.cumsum(seg, axis=0) - seg
    chunk_row = jnp.arange(NCHK, dtype=jnp.int32) * SEG_ALIGN
    chunk_exp = jnp.minimum(jnp.sum(chunk_row[None, :, None] >= seg_end[:, None, :], axis=-1),
                            N_EXPERTS - 1)
    own = chunk_exp[:, :, None] == jnp.arange(N_EXPERTS, dtype=jnp.int32)[None, None, :]
    chunk_dest = (jnp.sum(jnp.where(own, (seg_dest - seg_start)[:, None, :], 0), axis=-1)
                  + chunk_row[None, :]).astype(jnp.int32).reshape(N_TILES, 1, NCHK)
    nchk = (seg_end[:, -1] // SEG_ALIGN).astype(jnp.int32)
    block_start = jnp.arange(N_BLOCKS, dtype=jnp.int32) * BM
    block_expert = jnp.minimum(jnp.sum(block_start[:, None] >= pad_ends[None, :], axis=1),
                               N_EXPERTS - 1).astype(jnp.int32)
    n_used = (pad_ends[-1:] // BM).astype(jnp.int32)
    opens = (block_start < pad_ends[-1]) & (
        block_expert != jnp.concatenate([jnp.full((1,), -1, jnp.int32), block_expert[:-1]]))
    run_id = jnp.where(opens, jnp.cumsum(opens.astype(jnp.int32)) - 1, -1).astype(jnp.int32)
    experts = jnp.arange(N_EXPERTS, dtype=jnp.int32)
    later = (experts[None, :] > block_expert[:, None]) & (padded[None, :] > 0)
    next_expert = jnp.min(jnp.where(later, experts[None, :], N_EXPERTS), axis=1)
    next_expert = jnp.where(next_expert < N_EXPERTS, next_expert, -1).astype(jnp.int32)

    buf = _dispatch(pad_ends.astype(jnp.int32), padded.astype(jnp.int32), nchk, chunk_dest, pos_t, u2)
    mine = block_expert[:, None] == experts[None, :]
    filled = jnp.sum(jnp.where(mine, (pad_ends - padded + jnp.sum(seg, axis=0))[None, :], 0), axis=1)
    valid = jnp.clip(filled - block_start, 0, BM).astype(jnp.int32)
    y_buf = _ffn(block_expert, n_used, run_id, next_expert, valid, buf, w_up[0],
                 b_up[0].reshape(N_EXPERTS, 1, 2 * D), w_down[0], b_down[0].reshape(N_EXPERTS, 1, D))
    out = _combine(nchk, chunk_dest, y_buf, h1, gate_tok, g2, row(ln2_g[0]), row(ln2_b[0]))
    return out.reshape(1, N_TOK, D)
```

```python
import functools
import math

import jax
import jax.numpy as jnp
import numpy as np
from jax import lax
from jax.experimental import pallas as pl
from jax.experimental.pallas import tpu as pltpu

F32 = jnp.float32
BF16 = jnp.bfloat16
HI = lax.Precision.HIGHEST

D = 1024
N_TOK = 16384
N_CTX = 256
GRID_W = 64
S5_GROUP = 16
S5_GROUPS = 32
S5_STATE = 64
S5_WIDTH = 512
FFT_GROUPS = 4
FFT_DIM = 128
FFT_WIDTH = 512
N_EXPERTS = 32
TOP_K = 4
LN_EPS = 1e-5
ALPHA = 2.0 ** 0.25
SWIGLU_ALPHA = 1.702
SWIGLU_LIMIT = 7.0

LANES = 128
SUBLANES = 8
VMEM_LIMIT = 56 * 1024 * 1024

CH = 8
N_CHUNK = N_TOK // CH
N_CHUNK_CTX = N_CTX // CH
NSEG = SUBLANES
SEG = N_CHUNK // NSEG
SCAN_UNROLL = 4
GPT = LANES // S5_GROUP
NJ = S5_WIDTH // LANES
CL = CH * LANES
SW = 4 * GPT * S5_STATE

FN = 128
FB = 16

TM = 512
TM_PROJ = 1024
N_TILES = N_TOK // TM
MIX_GROUPS = 1
BM = 512
FFN_TAIL = 128
N_SLOTS = N_TOK * TOP_K
SEG_ALIGN = 2 * SUBLANES
CAP_BLOCK = 256
CAP = -(-(TOP_K * TM + N_EXPERTS * (SEG_ALIGN - 1)) // CAP_BLOCK) * CAP_BLOCK
NCHK = CAP // SEG_ALIGN
N_BLOCKS = -(-(N_SLOTS + N_TILES * N_EXPERTS * (SEG_ALIGN - 1)) // BM) + N_EXPERTS
ROWS = N_BLOCKS * BM
N_BLOCKS_ALL = N_BLOCKS + CAP // BM
ROWS_ALL = N_BLOCKS_ALL * BM


def _cparams(sem):
    return pltpu.CompilerParams(dimension_semantics=sem, vmem_limit_bytes=VMEM_LIMIT)


def _layer_norm(x, g, b):
    mu = jnp.mean(x, axis=-1, keepdims=True)
    xc = x - mu
    var = jnp.mean(xc * xc, axis=-1, keepdims=True)
    return xc * lax.rsqrt(var + LN_EPS) * g + b


def _sigmoid(x):
    return 1.0 / (1.0 + jnp.exp(-x))


def _ada_kernel(c_ref, w_ref, b_ref, o_ref):
    c = c_ref[...]
    s = c * _sigmoid(c)
    o_ref[...] = jnp.dot(s, w_ref[...], preferred_element_type=F32, precision=HI) + b_ref[...]


def _ada(cc, w_ada, b_ada):
    nb = 2
    wb = 6 * D // nb
    return pl.pallas_call(
        _ada_kernel,
        grid=(nb,),
        in_specs=[pl.BlockSpec((SUBLANES, D), lambda i: (0, 0)),
                  pl.BlockSpec((D, wb), lambda i: (0, i)),
                  pl.BlockSpec((1, wb), lambda i: (0, i))],
        out_specs=pl.BlockSpec((SUBLANES, wb), lambda i: (0, i)),
        out_shape=jax.ShapeDtypeStruct((SUBLANES, 6 * D), F32),
        compiler_params=_cparams(("parallel",)),
        name="ada",
    )(cc, w_ada, b_ada)


def _fftw_kernel(w_ref, b_ref, f_ref, wo_ref, bo_ref):
    f = f_ref[...]
    wo_ref[...] = jnp.dot(w_ref[...], f, preferred_element_type=F32, precision=HI)
    bo_ref[...] = jnp.dot(b_ref[...], f, preferred_element_type=F32, precision=HI)


def _fft_weights(w_fft, b_fft8, fc):
    return pl.pallas_call(
        _fftw_kernel,
        out_shape=(jax.ShapeDtypeStruct((D, 2 * FFT_WIDTH), F32),
                   jax.ShapeDtypeStruct((SUBLANES, 2 * FFT_WIDTH), F32)),
        compiler_params=pltpu.CompilerParams(vmem_limit_bytes=VMEM_LIMIT),
        name="fftw",
    )(w_fft, b_fft8, fc)


def _pos_code(er_ref, ec_ref, tm):
    nr = tm // GRID_W
    er = er_ref[...]
    row = jnp.broadcast_to(er[:, None, :], (nr, GRID_W, D // 2)).reshape(tm, D // 2)
    col = jnp.concatenate([ec_ref[...]] * nr, axis=0)
    return jnp.concatenate([row, col], axis=-1)


def _to_chunk_major(val, scr_ref, out_ref, tm):
    for j in range(NJ):
        scr_ref[j] = val[:, j * LANES:(j + 1) * LANES]
    for t in range(CH):
        for j in range(NJ):
            piece = scr_ref[j, pl.ds(t, tm // CH, stride=CH), :]
            out_ref[t, :, j * LANES:(j + 1) * LANES] = piece.astype(out_ref.dtype)


def _proj_kernel(x_ref, er_ref, ec_ref, lg_ref, lb_ref, m_ref, s_ref, w_ref, b_ref,
                 p_ref, xr_ref, xi_ref, scr_ref):
    x = x_ref[...] + _pos_code(er_ref, ec_ref, TM_PROJ)
    h = _layer_norm(x, lg_ref[...], lb_ref[...])
    u = (h * m_ref[...] + s_ref[...]).astype(BF16)
    p = jnp.dot(u, w_ref[...], preferred_element_type=F32) + b_ref[...]
    _to_chunk_major(p[:, :S5_WIDTH], scr_ref, p_ref, TM_PROJ)
    xr_ref[...] = p[:, S5_WIDTH:S5_WIDTH + FFT_WIDTH].astype(BF16)
    xi_ref[...] = p[:, S5_WIDTH + FFT_WIDTH:].astype(BF16)


def _proj(x, emb_r, emb_c, lg, lb, m1, s1, wcat, bcat):
    nw = wcat.shape[1]
    vec = pl.BlockSpec((1, D), lambda i: (0, 0))
    return pl.pallas_call(
        _proj_kernel,
        grid=(N_TOK // TM_PROJ,),
        in_specs=[pl.BlockSpec((TM_PROJ, D), lambda i: (i, 0)),
                  pl.BlockSpec((TM_PROJ // GRID_W, D // 2), lambda i: (i, 0)),
                  pl.BlockSpec((GRID_W, D // 2), lambda i: (0, 0)),
                  vec, vec, vec, vec,
                  pl.BlockSpec((D, nw), lambda i: (0, 0)),
                  pl.BlockSpec((1, nw), lambda i: (0, 0))],
        out_specs=(pl.BlockSpec((CH, TM_PROJ // CH, S5_WIDTH), lambda i: (0, i, 0)),
                   pl.BlockSpec((TM_PROJ, FFT_WIDTH), lambda i: (i, 0)),
                   pl.BlockSpec((TM_PROJ, FFT_WIDTH), lambda i: (i, 0))),
        out_shape=(jax.ShapeDtypeStruct((CH, N_CHUNK, S5_WIDTH), BF16),
                   jax.ShapeDtypeStruct((N_TOK, FFT_WIDTH), BF16),
                   jax.ShapeDtypeStruct((N_TOK, FFT_WIDTH), BF16)),
        scratch_shapes=[pltpu.VMEM((NJ, TM_PROJ, LANES), F32)],
        compiler_params=_cparams(("parallel",)),
        name="proj",
    )(x, emb_r, emb_c, lg, lb, m1, s1, wcat, bcat)


def _ctx_proj_kernel(x_ref, lg_ref, lb_ref, m_ref, s_ref, w_ref, b_ref, p_ref, scr_ref):
    h = _layer_norm(x_ref[...], lg_ref[...], lb_ref[...])
    u = (h * m_ref[...] + s_ref[...]).astype(BF16)
    p = jnp.dot(u, w_ref[...], preferred_element_type=F32) + b_ref[...]
    _to_chunk_major(p, scr_ref, p_ref, N_CTX)


def _ctx_proj(ctx, lg, lb, m1, s1, w_s5, b_s5):
    return pl.pallas_call(
        _ctx_proj_kernel,
        out_shape=jax.ShapeDtypeStruct((CH, N_CHUNK_CTX, S5_WIDTH), BF16),
        scratch_shapes=[pltpu.VMEM((NJ, N_CTX, LANES), F32)],
        compiler_params=pltpu.CompilerParams(vmem_limit_bytes=VMEM_LIMIT),
        name="ctxproj",
    )(ctx, lg, lb, m1, s1, w_s5, b_s5)


def _s5_tables(lam_re, lam_im, log_dt, b_re, b_im, c_re, c_im, d_skip):
    dt = jnp.exp(log_dt)[..., None]
    zr = lam_re * dt
    zi = lam_im * dt

    def apow(m):
        m = jnp.asarray(m, F32)
        mag = jnp.exp(zr[..., None] * m)
        return mag * jnp.cos(zi[..., None] * m), mag * jnp.sin(zi[..., None] * m)

    a_re, a_im = apow(jnp.ones((1,), F32))
    a_re, a_im = a_re[..., 0], a_im[..., 0]
    den = lam_re * lam_re + lam_im * lam_im
    num_re = a_re - 1.0
    k_re = (num_re * lam_re + a_im * lam_im) / den
    k_im = (a_im * lam_re - num_re * lam_im) / den
    bb_re = k_re[..., None] * b_re - k_im[..., None] * b_im
    bb_im = k_re[..., None] * b_im + k_im[..., None] * b_re

    ks = jnp.arange(CH + 1, dtype=F32)
    pw_re, pw_im = apow(ks)
    kmag = jnp.exp(zr[:, :, None, :] * ks[None, None, :, None])
    pk_re = kmag * jnp.cos(zi[:, :, None, :] * ks[None, None, :, None])
    pk_im = kmag * jnp.sin(zi[:, :, None, :] * ks[None, None, :, None])
    bt_re, bt_im = jnp.swapaxes(b_re, 2, 3), jnp.swapaxes(b_im, 2, 3)
    bbt_re = k_re[:, :, None, :] * bt_re - k_im[:, :, None, :] * bt_im
    bbt_im = k_re[:, :, None, :] * bt_im + k_im[:, :, None, :] * bt_re

    ar, ai = pw_re[:, :, :, :CH, None], pw_im[:, :, :, :CH, None]
    cr = jnp.swapaxes(c_re, 2, 3)[:, :, :, None, :]
    ci = jnp.swapaxes(c_im, 2, 3)[:, :, :, None, :]
    ca = jnp.concatenate([cr * ar - ci * ai, -(cr * ai + ci * ar)], axis=2)
    ca = ca.reshape(2, S5_GROUPS, 2 * S5_STATE, CH * S5_GROUP)
    bbt = jnp.concatenate([bbt_re, bbt_im], axis=-1)
    taps = jnp.einsum('dghq,dgqn->dghn', bbt, ca, precision=HI)
    skip = (d_skip.reshape(S5_GROUPS, S5_GROUP, 1) * jnp.eye(S5_GROUP, dtype=F32)[None])
    taps = taps.at[0, :, :, :S5_GROUP].add(skip)
    b_c = jnp.transpose(taps.reshape(2, NJ, GPT, S5_GROUP, CH, S5_GROUP), (1, 4, 0, 2, 3, 5))
    b_c = b_c.reshape(NJ, 2 * CH, LANES, S5_GROUP)

    ef = (CH - 1) - jnp.arange(CH)
    eb = jnp.arange(CH)

    def q_part(d, e):
        pr = pk_re[d][:, e, None, :]
        pi = pk_im[d][:, e, None, :]
        br = bbt_re[d][:, None, :, :]
        bi = bbt_im[d][:, None, :, :]
        return pr * br - pi * bi, pr * bi + pi * br

    def q_rows(v):
        v = v.reshape(NJ, GPT, CH, S5_GROUP, S5_STATE)
        return jnp.transpose(v, (0, 2, 1, 3, 4)).reshape(NJ, CL, S5_STATE)
    a_q = jnp.stack([q_rows(v) for v in q_part(0, ef) + q_part(1, eb)], axis=0)

    of = jnp.arange(CH) + 1
    ob = CH - jnp.arange(CH)

    def p_part(d, e):
        pr = pw_re[d][..., e][:, :, :, None]
        pi = pw_im[d][..., e][:, :, :, None]
        return (ct_re[d] * pr - ct_im[d] * pi, -(ct_re[d] * pi + ct_im[d] * pr))
    ct_re = jnp.swapaxes(c_re, 2, 3)[:, :, :, None, :]
    ct_im = jnp.swapaxes(c_im, 2, 3)[:, :, :, None, :]
    a_p = jnp.stack([v.reshape(NJ, GPT * S5_STATE, CH * S5_GROUP)
                     for v in p_part(0, of) + p_part(1, ob)], axis=0)

    def lanes(v):
        return jnp.transpose(v.reshape(2, NJ, GPT * S5_STATE), (1, 0, 2))
    c_r, c_i = apow(jnp.full((1,), float(CH), F32))
    s_r, s_i = apow(jnp.full((1,), float(CH * SEG), F32))
    cr, ci, sr, si = (lanes(v[..., 0]) for v in (c_r, c_i, s_r, s_i))
    trans = jnp.stack([cr[:, 0], ci[:, 0], cr[:, 1], ci[:, 1],
                       sr[:, 0], si[:, 0], sr[:, 1], si[:, 1]], axis=1)

    cidx = jnp.arange(N_CHUNK_CTX, dtype=F32)
    wf_r, wf_i = apow(CH * (N_CHUNK_CTX - 1 - cidx))
    wb_r, wb_i = apow(CH * cidx)

    def ctx_lanes(v, d):
        return jnp.transpose(v[d].reshape(NJ, GPT * S5_STATE, N_CHUNK_CTX), (0, 2, 1))
    ctx_w = jnp.stack([ctx_lanes(wf_r, 0), ctx_lanes(wf_i, 0),
                       ctx_lanes(wb_r, 1), ctx_lanes(wb_i, 1)], axis=1)
    return b_c, a_q, a_p, trans, ctx_w


def _s5w_kernel(bc_ref, aq_ref, ap_ref, c16_ref, c64_ref, cm_ref, wm_ref, wq_ref, wp_ref):
    def expand(a, c, row_shift, col_shift):
        w = jnp.dot(a, c, preferred_element_type=F32)
        rg = (lax.broadcasted_iota(jnp.int32, (w.shape[0], 1), 0) >> row_shift) & (GPT - 1)
        cg = (lax.broadcasted_iota(jnp.int32, (1, w.shape[1]), 1) >> col_shift) & (GPT - 1)
        return jnp.where(rg == cg, w, 0.0)

    blk = [expand(bc_ref[0, kd], c16_ref[...], 4, 4) for kd in range(2 * CH)]
    for t in range(CH):
        for u in range(CH):
            b = blk[2 * (u - t)] if u > t else blk[2 * (t - u) + 1] if u < t else blk[0] + blk[1]
            wm_ref[0, t * LANES:(t + 1) * LANES, u * LANES:(u + 1) * LANES] = b.astype(BF16)
    half = GPT * S5_STATE
    for s in range(4):
        wq_ref[0, :, s * half:(s + 1) * half] = expand(aq_ref[s, 0], c64_ref[...], 4, 6).astype(BF16)
        wp_ref[0, s * half:(s + 1) * half, :] = expand(ap_ref[s, 0], cm_ref[...], 6, 4).astype(BF16)


def _s5_expand(b_c, a_q, a_p):
    rep = np.ones((1, GPT))
    c16 = jnp.asarray(np.kron(rep, np.eye(S5_GROUP)), F32).astype(BF16)
    c64 = jnp.asarray(np.kron(rep, np.eye(S5_STATE)), F32).astype(BF16)
    c_m = jnp.asarray(np.kron(np.eye(CH), np.kron(rep, np.eye(S5_GROUP))), F32).astype(BF16)
    b_c, a_q, a_p = b_c.astype(BF16), a_q.astype(BF16), a_p.astype(BF16)
    half = GPT * S5_STATE
    return pl.pallas_call(
        _s5w_kernel,
        grid=(NJ,),
        in_specs=[pl.BlockSpec((1, 2 * CH, LANES, S5_GROUP), lambda j: (j, 0, 0, 0)),
                  pl.BlockSpec((4, 1, CL, S5_STATE), lambda j: (0, j, 0, 0)),
                  pl.BlockSpec((4, 1, half, CH * S5_GROUP), lambda j: (0, j, 0, 0)),
                  pl.BlockSpec(c16.shape, lambda j: (0, 0)),
                  pl.BlockSpec(c64.shape, lambda j: (0, 0)),
                  pl.BlockSpec(c_m.shape, lambda j: (0, 0))],
        out_specs=(pl.BlockSpec((1, CL, CL), lambda j: (j, 0, 0)),
                   pl.BlockSpec((1, CL, SW), lambda j: (j, 0, 0)),
                   pl.BlockSpec((1, SW, CL), lambda j: (j, 0, 0))),
        out_shape=(jax.ShapeDtypeStruct((NJ, CL, CL), BF16),
                   jax.ShapeDtypeStruct((NJ, CL, SW), BF16),
                   jax.ShapeDtypeStruct((NJ, SW, CL), BF16)),
        compiler_params=_cparams(("parallel",)),
        name="s5w",
    )(b_c, a_q, a_p, c16, c64, c_m)


def _s5_kernel(p_ref, pc_ref, wm_ref, wq_ref, wp_ref, tr_ref, cw_ref, y_ref, v_ref):
    nq = NJ
    half = GPT * S5_STATE

    def chunk_rows(ref, r0, nrows):
        return jnp.concatenate([ref[t, pl.ds(r0, nrows), :] for t in range(CH)], axis=-1)

    def fill(k, c):
        r0 = pl.multiple_of(k * SEG, SEG)
        v = jnp.dot(chunk_rows(p_ref, r0, SEG), wq_ref[0], preferred_element_type=F32)
        for s in range(4 * nq):
            v_ref[s, pl.ds(k, SEG, stride=NSEG), :] = v[:, s * LANES:(s + 1) * LANES]
        return c
    lax.fori_loop(0, NSEG, fill, 0)

    vc = jnp.dot(chunk_rows(pc_ref, 0, N_CHUNK_CTX), wq_ref[0], preferred_element_type=F32)
    vfr, vfi, vbr, vbi = (vc[:, i * half:(i + 1) * half] for i in range(4))
    wfr, wfi, wbr, wbi = (cw_ref[0, i] for i in range(4))
    s0_fr = jnp.sum(wfr * vfr - wfi * vfi, axis=0, keepdims=True)
    s0_fi = jnp.sum(wfr * vfi + wfi * vfr, axis=0, keepdims=True)
    s0_br = jnp.sum(wbr * vbr - wbi * vbi, axis=0, keepdims=True)
    s0_bi = jnp.sum(wbr * vbi + wbi * vbr, axis=0, keepdims=True)

    tr = tr_ref[0]
    afr, afi, abr, abi = (jnp.broadcast_to(tr[i:i + 1], (NSEG, half)) for i in range(4))
    gfr, gfi, gbr, gbi = (tr[i:i + 1] for i in range(4, 8))

    def load_part(part, i):
        return jnp.concatenate(
            [v_ref[part * nq + q, pl.ds(pl.multiple_of(i * NSEG, NSEG), NSEG), :] for q in range(nq)],
            axis=-1)

    def store_part(part, i, val):
        for q in range(nq):
            v_ref[part * nq + q, pl.ds(pl.multiple_of(i * NSEG, NSEG), NSEG), :] = (
                val[:, q * LANES:(q + 1) * LANES])

    def step(i, carry, write):
        fr, fi, br, bi = carry
        ib = SEG - 1 - i
        ufr, ufi = load_part(0, i), load_part(1, i)
        ubr, ubi = load_part(2, ib), load_part(3, ib)
        if write:
            store_part(0, i, fr)
            store_part(1, i, fi)
            store_part(2, ib, br)
            store_part(3, ib, bi)
        return (afr * fr - afi * fi + ufr, afr * fi + afi * fr + ufi,
                abr * br - abi * bi + ubr, abr * bi + abi * br + ubi)

    zero = jnp.zeros((NSEG, half), F32)
    ffr, ffi, fbr, fbi = lax.fori_loop(0, SEG, functools.partial(step, write=False),
                                       (zero, zero, zero, zero), unroll=SCAN_UNROLL)

    rows_fr, rows_fi = [s0_fr], [s0_fi]
    for k in range(1, NSEG):
        pr, pi = rows_fr[-1], rows_fi[-1]
        rows_fr.append(gfr * pr - gfi * pi + ffr[k - 1:k])
        rows_fi.append(gfr * pi + gfi * pr + ffi[k - 1:k])
    rows_br, rows_bi = [s0_br], [s0_bi]
    for k in range(NSEG - 2, -1, -1):
        pr, pi = rows_br[0], rows_bi[0]
        rows_br.insert(0, gbr * pr - gbi * pi + fbr[k + 1:k + 2])
        rows_bi.insert(0, gbr * pi + gbi * pr + fbi[k + 1:k + 2])
    init = tuple(jnp.concatenate(r, axis=0) for r in (rows_fr, rows_fi, rows_br, rows_bi))

    lax.fori_loop(0, SEG, functools.partial(step, write=True), init, unroll=SCAN_UNROLL)

    def emit(k, c):
        r0 = pl.multiple_of(k * SEG, SEG)
        b = chunk_rows(p_ref, r0, SEG)
        sin = jnp.concatenate([v_ref[s, pl.ds(k, SEG, stride=NSEG), :] for s in range(4 * nq)], axis=-1)
        y = (jnp.dot(b, wm_ref[0], preferred_element_type=F32)
             + jnp.dot(sin.astype(BF16), wp_ref[0], preferred_element_type=F32))
        for t in range(CH):
            y_ref[t, pl.ds(r0, SEG), :] = y[:, t * LANES:(t + 1) * LANES].astype(y_ref.dtype)
        return c
    lax.fori_loop(0, NSEG, emit, 0)


def _s5(p_t, pc_t, w_m, w_q, w_p, trans, ctx_w):
    one = pl.Buffered(1)
    return pl.pallas_call(
        _s5_kernel,
        grid=(NJ,),
        in_specs=[pl.BlockSpec((CH, N_CHUNK, LANES), lambda j: (0, 0, j)),
                  pl.BlockSpec((CH, N_CHUNK_CTX, LANES), lambda j: (0, 0, j)),
                  pl.BlockSpec((1, CL, CL), lambda j: (j, 0, 0), pipeline_mode=one),
                  pl.BlockSpec((1, CL, SW), lambda j: (j, 0, 0), pipeline_mode=one),
                  pl.BlockSpec((1, SW, CL), lambda j: (j, 0, 0), pipeline_mode=one),
                  pl.BlockSpec((1, SUBLANES, GPT * S5_STATE), lambda j: (j, 0, 0)),
                  pl.BlockSpec((1, 4, N_CHUNK_CTX, GPT * S5_STATE), lambda j: (j, 0, 0, 0))],
        out_specs=pl.BlockSpec((CH, N_CHUNK, LANES), lambda j: (0, 0, j)),
        out_shape=jax.ShapeDtypeStruct((CH, N_CHUNK, S5_WIDTH), BF16),
        scratch_shapes=[pltpu.VMEM((4 * NJ, N_CHUNK, LANES), F32)],
        compiler_params=_cparams(("parallel",)),
        name="s5",
    )(p_t, pc_t, w_m, w_q, w_p, trans, ctx_w)


def _dft_tables():
    n = np.arange(FN)
    ang = 2.0 * np.pi * np.outer(n, n) / FN
    c, s = np.cos(ang), np.sin(ang)
    st1 = np.block([[c, s], [-s, c]])
    tw = 2.0 * np.pi * np.outer(n, n) / (FN * FN)
    wr, wi = np.cos(tw), -np.sin(tw)
    fr = c[None] * wr[:, None, :] + s[None] * wi[:, None, :]
    fi = c[None] * wi[:, None, :] - s[None] * wr[:, None, :]
    st2 = np.concatenate([fr, -fi], axis=-1)
    scale = 1.0 / math.sqrt(N_TOK * FFT_DIM)
    blk_c = np.kron(np.eye(FFT_GROUPS), c) * scale
    blk_s = np.kron(np.eye(FFT_GROUPS), s) * scale
    fc = np.concatenate([blk_c, -blk_s], axis=1)
    return (jnp.asarray(st1, F32).astype(BF16), jnp.asarray(st2, F32).astype(BF16), jnp.asarray(fc, F32))


FSL = FFT_WIDTH // LANES


FBH = FB // SUBLANES


def _block_to_slabs(blk, slab_ref, first, per_half):
    for bh in range(FBH):
        val = blk[:, bh * SUBLANES:(bh + 1) * SUBLANES, :].reshape(FN * SUBLANES, FFT_WIDTH)
        for s in range(FSL):
            slab_ref[bh * per_half + first + s] = val[:, s * LANES:(s + 1) * LANES]


def _slab_rows(b, first, per_half):
    return (b // SUBLANES) * per_half + first, pl.ds(b % SUBLANES, FN, stride=SUBLANES)


def _slabs_to_block(slab_ref, first, per_half):
    halves = []
    for bh in range(FBH):
        val = jnp.concatenate([slab_ref[bh * per_half + first + s] for s in range(FSL)], axis=-1)
        halves.append(val.reshape(FN, SUBLANES, FFT_WIDTH))
    return jnp.concatenate(halves, axis=1)


def _fft1_kernel(xr_ref, xi_ref, f_ref, yr_ref, yi_ref, in_ref, out_ref):
    _block_to_slabs(xr_ref[...].astype(F32), in_ref, 0, 2 * FSL)
    _block_to_slabs(xi_ref[...].astype(F32), in_ref, FSL, 2 * FSL)
    for b in range(FB):
        def part(first):
            base, rows = _slab_rows(b, first, 2 * FSL)
            return jnp.concatenate([in_ref[base + s, rows, :] for s in range(FSL)], axis=-1)
        xs = jnp.concatenate([part(0), part(FSL)], axis=0).astype(BF16)
        y = jnp.dot(f_ref[...], xs, preferred_element_type=F32)
        base, rows = _slab_rows(b, 0, 2 * FSL)
        for s in range(FSL):
            out_ref[base + s, rows, :] = y[:FN, s * LANES:(s + 1) * LANES]
            out_ref[base + FSL + s, rows, :] = y[FN:, s * LANES:(s + 1) * LANES]
    yr_ref[...] = _slabs_to_block(out_ref, 0, 2 * FSL).astype(BF16)
    yi_ref[...] = _slabs_to_block(out_ref, FSL, 2 * FSL).astype(BF16)


def _fft1(xr, xi, st1):
    spec = pl.BlockSpec((FN, FB, FFT_WIDTH), lambda i: (0, i, 0))
    slabs = pltpu.VMEM((FBH * 2 * FSL, FN * SUBLANES, LANES), F32)
    return pl.pallas_call(
        _fft1_kernel,
        grid=(FN // FB,),
        in_specs=[spec, spec, pl.BlockSpec((2 * FN, 2 * FN), lambda i: (0, 0))],
        out_specs=(spec, spec),
        out_shape=(jax.ShapeDtypeStruct((FN, FN, FFT_WIDTH), BF16),) * 2,
        scratch_shapes=[slabs, slabs],
        compiler_params=_cparams(("parallel",)),
        name="fft1",
    )(xr.reshape(FN, FN, FFT_WIDTH), xi.reshape(FN, FN, FFT_WIDTH), st1)


def _fft2_kernel(yr_ref, yi_ref, f_ref, z_ref, out_ref):
    for b in range(FB):
        ys = jnp.concatenate([yr_ref[b * FN:(b + 1) * FN, :], yi_ref[b * FN:(b + 1) * FN, :]], axis=0)
        z = jnp.dot(f_ref[b], ys, preferred_element_type=F32)
        base, rows = _slab_rows(b, 0, FSL)
        for s in range(FSL):
            out_ref[base + s, rows, :] = z[:, s * LANES:(s + 1) * LANES]
    z_ref[...] = _slabs_to_block(out_ref, 0, FSL).astype(BF16)


def _fft2(yr, yi, st2):
    rows = pl.BlockSpec((FB * FN, FFT_WIDTH), lambda i: (i, 0))
    z = pl.pallas_call(
        _fft2_kernel,
        grid=(FN // FB,),
        in_specs=[rows, rows, pl.BlockSpec((FB, FN, 2 * FN), lambda i: (i, 0, 0))],
        out_specs=pl.BlockSpec((FN, FB, FFT_WIDTH), lambda i: (0, i, 0)),
        out_shape=jax.ShapeDtypeStruct((FN, FN, FFT_WIDTH), BF16),
        scratch_shapes=[pltpu.VMEM((FBH * FSL, FN * SUBLANES, LANES), F32)],
        compiler_params=_cparams(("parallel",)),
        name="fft2",
    )(yr.reshape(N_TOK, FFT_WIDTH), yi.reshape(N_TOK, FFT_WIDTH), st2)
    return z.reshape(N_TOK, FFT_WIDTH)


def _gelu_tanh(x):
    return 0.5 * x * (1.0 + jnp.tanh(math.sqrt(2.0 / math.pi) * (x + 0.044715 * (x * x * x))))


def _mix_kernel(x_ref, er_ref, ec_ref, lg_ref, lb_ref, m1_ref, s1_ref, wg_ref, bg_ref,
                yt_ref, zr_ref, wglu_ref, bglu_ref, wbs_ref, wbf_ref, bbf_ref, wo_ref, bo_ref,
                g1_ref, l1g_ref, l1b_ref, m2_ref, s2_ref, wr_ref, br_ref, tri_ref, etri_ref,
                h1_ref, u2_ref, pos_ref, gate_ref, cnt_ref, scr_ref):
    pos = _pos_code(er_ref, ec_ref, TM)

    def front(r0, nr):
        rows = slice(r0, r0 + nr)
        h = _layer_norm(x_ref[rows, :] + pos[rows, :], lg_ref[...], lb_ref[...])
        u = (h * m1_ref[...] + s1_ref[...]).astype(BF16)

        c0, nc = r0 // CH, nr // CH
        for t in range(CH):
            for j in range(NJ):
                scr_ref[j, pl.ds(r0 + t, nc, stride=CH), :] = (
                    yt_ref[t, c0:c0 + nc, j * LANES:(j + 1) * LANES].astype(F32))
        ys = jnp.concatenate([scr_ref[j, rows, :] for j in range(NJ)], axis=-1)
        z = jnp.dot(_gelu_tanh(ys).astype(BF16), wglu_ref[...], preferred_element_type=F32) + bglu_ref[...]
        glu = (z[:, :S5_WIDTH] * _sigmoid(z[:, S5_WIDTH:])).astype(BF16)
        gates = _sigmoid(jnp.dot(u, wg_ref[...], preferred_element_type=F32) + bg_ref[...])
        y_s5 = jnp.dot(glu, wbs_ref[...], preferred_element_type=F32)
        y_fft = jnp.dot(zr_ref[rows, :], wbf_ref[...], preferred_element_type=F32) + bbf_ref[...]
        mixed = (gates[:, :D] * y_s5 + gates[:, D:] * y_fft).astype(BF16)
        y = jnp.dot(mixed, wo_ref[...], preferred_element_type=F32) + bo_ref[...]
        h1 = _layer_norm(ALPHA * h + g1_ref[...] * y, l1g_ref[...], l1b_ref[...])
        h1_ref[rows, :] = h1
        u2 = h1 * m2_ref[...] + s2_ref[...]
        u2_ref[rows, :] = u2.astype(BF16)
        u_hi = u2.astype(BF16)
        u_lo = (u2 - u_hi.astype(F32)).astype(BF16)

        def nt(a, b):
            return lax.dot_general(a, b, (((1,), (1,)), ((), ())), preferred_element_type=F32)
        return nt(wr_ref[0], u_hi) + nt(wr_ref[0], u_lo) + nt(wr_ref[1], u_hi)

    nr = TM // MIX_GROUPS
    logits = jnp.concatenate([front(g * nr, nr) for g in range(MIX_GROUPS)], axis=-1) + br_ref[:, 0:1]
    eidx = lax.broadcasted_iota(jnp.int32, (N_EXPERTS, TM), 0)
    vals, hots = [], []
    cur = logits
    for _k in range(TOP_K):
        m = jnp.max(cur, axis=0, keepdims=True)
        sel = jnp.min(jnp.where(cur == m, eidx, N_EXPERTS), axis=0, keepdims=True)
        hot = eidx == sel
        cur = jnp.where(hot, -jnp.inf, cur)
        vals.append(m)
        hots.append(hot)
    exps = [jnp.exp(v - vals[0]) for v in vals]
    den = exps[0] + exps[1] + exps[2] + exps[3]
    gate4 = jnp.concatenate([e / den for e in exps], axis=0)

    hot_sum = (hots[0] | hots[1] | hots[2] | hots[3]).astype(F32)
    before = jnp.dot(hot_sum.astype(BF16), tri_ref[...], preferred_element_type=F32)
    cnt = jnp.broadcast_to(jnp.sum(hot_sum, axis=1, keepdims=True), (N_EXPERTS, LANES))
    cnt8 = jnp.floor((cnt + (SEG_ALIGN - 1)) * (1.0 / SEG_ALIGN)) * SEG_ALIGN
    seg0 = jnp.dot(etri_ref[...], cnt8.astype(BF16), preferred_element_type=F32)
    tot = seg0[:, 0:1] + before
    pos4 = jnp.concatenate(
        [jnp.sum(jnp.where(hk, tot, 0.0), axis=0, keepdims=True) for hk in hots], axis=0)
    pos_ref[...] = pos4.astype(jnp.int32)
    cnt_ref[0] = cnt

    gpad = jnp.concatenate([gate4, pos4, jnp.zeros((LANES - 2 * TOP_K, TM), F32)], axis=0)
    gate_ref[...] = gpad.T


def _mix(x, emb_r, emb_c, lg, lb, m1, s1, wg, bg, y_t, zr, wglu, bglu, wbs, wbf, bbf, wo, bo,
         g1, l1g, l1b, m2, s2, wr_t, br, tri, etri):
    vec = pl.BlockSpec((1, D), lambda i: (0, 0))

    def full(a):
        return pl.BlockSpec(a.shape, lambda i: (0,) * a.ndim)
    return pl.pallas_call(
        _mix_kernel,
        grid=(N_TOK // TM,),
        in_specs=[pl.BlockSpec((TM, D), lambda i: (i, 0)),
                  pl.BlockSpec((TM // GRID_W, D // 2), lambda i: (i, 0)),
                  pl.BlockSpec((GRID_W, D // 2), lambda i: (0, 0)),
                  vec, vec, vec, vec, full(wg), full(bg),
                  pl.BlockSpec((CH, TM // CH, S5_WIDTH), lambda i: (0, i, 0)),
                  pl.BlockSpec((TM, FFT_WIDTH), lambda i: (i, 0)),
                  full(wglu), full(bglu), full(wbs), full(wbf), full(bbf), full(wo), full(bo),
                  vec, vec, vec, vec, vec, full(wr_t), full(br), full(tri), full(etri)],
        out_specs=(pl.BlockSpec((TM, D), lambda i: (i, 0)),
                   pl.BlockSpec((TM, D), lambda i: (i, 0)),
                   pl.BlockSpec((TOP_K, TM), lambda i: (0, i)),
                   pl.BlockSpec((TM, LANES), lambda i: (i, 0)),
                   pl.BlockSpec((1, N_EXPERTS, LANES), lambda i: (i, 0, 0))),
        out_shape=(jax.ShapeDtypeStruct((N_TOK, D), F32),
                   jax.ShapeDtypeStruct((N_TOK, D), BF16),
                   jax.ShapeDtypeStruct((TOP_K, N_TOK), jnp.int32),
                   jax.ShapeDtypeStruct((N_TOK, LANES), F32),
                   jax.ShapeDtypeStruct((N_TILES, N_EXPERTS, LANES), F32)),
        scratch_shapes=[pltpu.VMEM((NJ, TM, LANES), F32)],
        compiler_params=_cparams(("parallel",)),
        name="mix",
    )(x, emb_r, emb_c, lg, lb, m1, s1, wg, bg, y_t, zr, wglu, bglu, wbs, wbf, bbf, wo, bo,
      g1, l1g, l1b, m2, s2, wr_t, br, tri, etri)


def _on_parity(i, fn):
    @pl.when(i % 2 == 0)
    def _():
        fn(0)

    @pl.when(i % 2 == 1)
    def _():
        fn(1)


def _dispatch_kernel(pend_ref, padded_ref, dprev_ref, dest_ref, pos_ref, u_ref, buf_ref,
                     sorted_ref, zero_ref, zsem, sems):
    i = pl.program_id(0)

    @pl.when(i == 0)
    def _():
        zero_ref[...] = jnp.zeros_like(zero_ref)
        n_used = pend_ref[N_EXPERTS - 1] // BM

        def clear_copy(start):
            return pltpu.make_async_copy(
                zero_ref, buf_ref.at[pl.ds(pl.multiple_of(start, BM), BM)], zsem)

        def each(fn):
            def expert(e, c):
                @pl.when(padded_ref[e] > 0)
                def _():
                    fn(clear_copy(pend_ref[e] - BM))
                return c
            lax.fori_loop(0, N_EXPERTS, expert, 0)

            def tail(b, c):
                fn(clear_copy(b * BM))
                return c
            lax.fori_loop(n_used, N_BLOCKS_ALL, tail, 0)
        each(lambda cp: cp.start())
        each(lambda cp: cp.wait())

        sorted_ref[...] = jnp.zeros_like(sorted_ref)

    def chunk_copy(slot, table_ref, j):
        dst = pl.multiple_of(table_ref[0, 0, j], SEG_ALIGN)
        return pltpu.make_async_copy(sorted_ref.at[slot, pl.ds(j * SEG_ALIGN, SEG_ALIGN)],
                                     buf_ref.at[pl.ds(dst, SEG_ALIGN)], sems.at[slot])

    def drain(slot):
        pltpu.make_async_copy(sorted_ref.at[slot], buf_ref.at[pl.ds(0, CAP)], sems.at[slot]).wait()

    def run(slot):
        pos = pos_ref[...]
        u = u_ref[...]
        n_rb = CAP // CAP_BLOCK
        per_rb = NCHK // (n_rb // 2)
        for rb in range(n_rb):
            for j in range(rb * per_rb, min((rb + 1) * per_rb, NCHK)):
                chunk_copy(1 - slot, dprev_ref, j).start()
            rows = lax.broadcasted_iota(jnp.int32, (CAP_BLOCK, TM), 0) + rb * CAP_BLOCK
            hit = rows == pos[0:1]
            for k in range(1, TOP_K):
                hit = hit | (rows == pos[k:k + 1])
            onehot = jnp.where(hit, 1.0, 0.0).astype(BF16)
            sorted_ref[slot, rb * CAP_BLOCK:(rb + 1) * CAP_BLOCK, :] = jnp.dot(
                onehot, u, preferred_element_type=F32).astype(BF16)
        drain(1 - slot)

        @pl.when(i == N_TILES - 1)
        def _():
            def issue(j, c):
                chunk_copy(slot, dest_ref, j).start()
                return c
            lax.fori_loop(0, NCHK, issue, 0)
            drain(slot)
    _on_parity(i, run)


def _dispatch(pad_ends, padded, chunk_table, pos_t, u2):
    return pl.pallas_call(
        _dispatch_kernel,
        grid_spec=pltpu.PrefetchScalarGridSpec(
            num_scalar_prefetch=2,
            grid=(N_TILES,),
            in_specs=[pl.BlockSpec((1, 1, NCHK), lambda i, a, b: (i, 0, 0), memory_space=pltpu.SMEM),
                      pl.BlockSpec((1, 1, NCHK), lambda i, a, b: (i + 1, 0, 0), memory_space=pltpu.SMEM),
                      pl.BlockSpec((TOP_K, TM), lambda i, a, b: (0, i)),
                      pl.BlockSpec((TM, D), lambda i, a, b: (i, 0))],
            out_specs=pl.BlockSpec(memory_space=pl.ANY),
            scratch_shapes=[pltpu.VMEM((2, CAP, D), BF16),
                            pltpu.VMEM((BM, D), BF16),
                            pltpu.SemaphoreType.DMA(()),
                            pltpu.SemaphoreType.DMA((2,))]),
        out_shape=jax.ShapeDtypeStruct((ROWS_ALL, D), BF16),
        compiler_params=_cparams(("arbitrary",)),
        name="dispatch",
    )(pad_ends, padded, chunk_table, chunk_table, pos_t, u2)


def _ffn_kernel(be_ref, nu_ref, run_ref, nxt_ref, valid_ref, x_ref, wu_hbm, bu_ref, wd_hbm, bd_ref,
                y_ref, wu_ref, wd_ref, wub_ref, wdb_ref, sems):
    i = pl.program_id(0)
    used = i < nu_ref[0]

    def weight_copies(e, slot):
        return (pltpu.make_async_copy(wu_hbm.at[e], wu_ref.at[slot], sems.at[slot]),
                pltpu.make_async_copy(wd_hbm.at[e], wd_ref.at[slot], sems.at[slot]))

    @pl.when(used)
    def _():
        run = run_ref[i]

        @pl.when(run >= 0)
        def _():
            def open_run(slot):
                @pl.when(run == 0)
                def _():
                    for cp in weight_copies(be_ref[i], slot):
                        cp.start()

                @pl.when(nxt_ref[i] >= 0)
                def _():
                    for cp in weight_copies(nxt_ref[i], 1 - slot):
                        cp.start()
                for cp in weight_copies(be_ref[i], slot):
                    cp.wait()
                wub_ref[...] = wu_ref[slot].astype(BF16)
                wdb_ref[...] = wd_ref[slot].astype(BF16)
            _on_parity(run, open_run)

        def expert_rows(r0, nr):
            rows = slice(r0, r0 + nr)
            h = jnp.dot(x_ref[rows, :], wub_ref[...], preferred_element_type=F32) + bu_ref[0]
            h_glu = jnp.minimum(h[:, :D], SWIGLU_LIMIT)
            h_lin = jnp.clip(h[:, D:], -SWIGLU_LIMIT, SWIGLU_LIMIT)
            act = (h_glu * _sigmoid(SWIGLU_ALPHA * h_glu) * (h_lin + 1.0)).astype(BF16)
            y_ref[rows, :] = (jnp.dot(act, wdb_ref[...], preferred_element_type=F32)
                              + bd_ref[0]).astype(BF16)

        valid = valid_ref[i]

        @pl.when(valid == BM)
        def _():
            expert_rows(0, BM)

        @pl.when(valid < BM)
        def _():
            for r0 in range(0, BM, FFN_TAIL):
                @pl.when(r0 < valid)
                def _(r0=r0):
                    expert_rows(r0, FFN_TAIL)

                @pl.when(r0 >= valid)
                def _(r0=r0):
                    y_ref[r0:r0 + FFN_TAIL, :] = jnp.zeros((FFN_TAIL, D), BF16)

    @pl.when(jnp.logical_not(used))
    def _():
        y_ref[...] = jnp.zeros_like(y_ref)


def _ffn(block_expert, n_used, run_id, next_expert, valid, buf, w_up, b_up, w_down, b_down):
    def blk(i, be, nu, *_):
        return jnp.minimum(i, nu[0] - 1)
    return pl.pallas_call(
        _ffn_kernel,
        grid_spec=pltpu.PrefetchScalarGridSpec(
            num_scalar_prefetch=5,
            grid=(N_BLOCKS_ALL,),
            in_specs=[pl.BlockSpec((BM, D), lambda i, *s: (blk(i, *s), 0)),
                      pl.BlockSpec(memory_space=pl.ANY),
                      pl.BlockSpec((1, 1, 2 * D), lambda i, *s: (s[0][blk(i, *s)], 0, 0)),
                      pl.BlockSpec(memory_space=pl.ANY),
                      pl.BlockSpec((1, 1, D), lambda i, *s: (s[0][blk(i, *s)], 0, 0))],
            out_specs=pl.BlockSpec((BM, D), lambda i, *s: (i, 0)),
            scratch_shapes=[pltpu.VMEM((2, D, 2 * D), F32),
                            pltpu.VMEM((2, D, D), F32),
                            pltpu.VMEM((D, 2 * D), BF16),
                            pltpu.VMEM((D, D), BF16),
                            pltpu.SemaphoreType.DMA((2,))]),
        out_shape=jax.ShapeDtypeStruct((ROWS_ALL, D), BF16),
        compiler_params=_cparams(("arbitrary",)),
        name="ffn",
    )(block_expert, n_used, run_id, next_expert, valid, buf, w_up, b_up, w_down, b_down)


def _combine_kernel(dest_ref, dnext_ref, y_ref, h1_ref, gate_ref, g2_ref, lg_ref, lb_ref,
                    o_ref, sorted_ref, sems):
    i = pl.program_id(0)

    def chunk_copy(slot, table_ref, j):
        src = pl.multiple_of(table_ref[0, 0, j], SEG_ALIGN)
        return pltpu.make_async_copy(y_ref.at[pl.ds(src, SEG_ALIGN)],
                                     sorted_ref.at[slot, pl.ds(j * SEG_ALIGN, SEG_ALIGN)],
                                     sems.at[slot])

    def drain(slot):
        pltpu.make_async_copy(y_ref.at[pl.ds(0, CAP)], sorted_ref.at[slot], sems.at[slot]).wait()

    @pl.when(i == 0)
    def _():
        def issue(j, c):
            chunk_copy(0, dest_ref, j).start()
            return c
        lax.fori_loop(0, NCHK, issue, 0)

    def run(slot):
        drain(slot)

        gp = gate_ref[...]
        m = jnp.zeros((TM, D), F32)
        n_cb = CAP // CAP_BLOCK
        per_cb = NCHK // (n_cb // 2)
        for cb in range(n_cb):
            for j in range(cb * per_cb, min((cb + 1) * per_cb, NCHK)):
                chunk_copy(1 - slot, dnext_ref, j).start()
            cols = (lax.broadcasted_iota(jnp.int32, (TM, CAP_BLOCK), 1) + cb * CAP_BLOCK).astype(F32)
            g = jnp.where(cols == gp[:, TOP_K:TOP_K + 1], gp[:, 0:1], 0.0)
            for k in range(1, TOP_K):
                g = g + jnp.where(cols == gp[:, TOP_K + k:TOP_K + k + 1], gp[:, k:k + 1], 0.0)
            rows = sorted_ref[slot, cb * CAP_BLOCK:(cb + 1) * CAP_BLOCK, :]
            m = m + jnp.dot(g.astype(BF16), rows, preferred_element_type=F32)
        o_ref[...] = _layer_norm(ALPHA * h1_ref[...] + g2_ref[...] * m, lg_ref[...], lb_ref[...])

        @pl.when(i == N_TILES - 1)
        def _():
            drain(1 - slot)
    _on_parity(i, run)


def _combine(chunk_table, y_buf, h1, gate_tok, g2, lg, lb):
    vec = pl.BlockSpec((1, D), lambda i: (0, 0))
    return pl.pallas_call(
        _combine_kernel,
        grid_spec=pltpu.PrefetchScalarGridSpec(
            num_scalar_prefetch=0,
            grid=(N_TILES,),
            in_specs=[pl.BlockSpec((1, 1, NCHK), lambda i: (i + 1, 0, 0), memory_space=pltpu.SMEM),
                      pl.BlockSpec((1, 1, NCHK), lambda i: (i + 2, 0, 0), memory_space=pltpu.SMEM),
                      pl.BlockSpec(memory_space=pl.ANY),
                      pl.BlockSpec((TM, D), lambda i: (i, 0)),
                      pl.BlockSpec((TM, LANES), lambda i: (i, 0)),
                      vec, vec, vec],
            out_specs=pl.BlockSpec((TM, D), lambda i: (i, 0)),
            scratch_shapes=[pltpu.VMEM((2, CAP, D), BF16),
                            pltpu.SemaphoreType.DMA((2,))]),
        out_shape=jax.ShapeDtypeStruct((N_TOK, D), F32),
        compiler_params=_cparams(("arbitrary",)),
        name="combine",
    )(chunk_table, chunk_table, y_buf, h1, gate_tok, g2, lg, lb)


def _sincos_tables():
    q = D // 4
    omega = 1.0 / (10000.0 ** (jnp.arange(q, dtype=F32) / q))

    def emb(n):
        ang = jnp.arange(n, dtype=F32)[:, None] * omega[None, :]
        return jnp.concatenate([jnp.sin(ang), jnp.cos(ang)], axis=-1)
    return emb(N_TOK // GRID_W), emb(GRID_W)


def _tile_slots(a, tile):
    return jnp.transpose(a.reshape(TOP_K, N_TOK // tile, tile), (1, 0, 2)).reshape(N_TOK // tile, 1, TOP_K * tile)


def kernel(x, c, ctx, c_ctx, ln_in_g, ln_in_b, w_ada, b_ada, w_in, b_in, s5_lambda_re, s5_lambda_im, s5_log_dt, s5_b_re, s5_b_im, s5_c_re, s5_c_im, s5_d, w_glu, b_glu, w_br_s5, w_br_fft, b_br_fft, w_out, b_out, ln1_g, ln1_b, w_router, b_router, w_up, b_up, w_down, b_down, ln2_g, ln2_b):
    assert x.shape == (1, N_TOK, D) and ctx.shape == (1, N_CTX, D) and w_ada.shape[0] == 1
    row = lambda v: v.reshape(1, -1).astype(F32)

    cc = jnp.concatenate([c.reshape(1, D), c_ctx.reshape(1, D), jnp.zeros((SUBLANES - 2, D), F32)], axis=0)
    ada = _ada(cc, w_ada[0], row(b_ada[0]))
    sh1, sc1, g1, sh2, sc2, g2 = (ada[0:1, k * D:(k + 1) * D] for k in range(6))
    sh1c, sc1c = ada[1:2, 0:D], ada[1:2, D:2 * D]

    emb_r, emb_c = _sincos_tables()
    st1, st2, fc = _dft_tables()
    lg, lb = row(ln_in_g), row(ln_in_b)

    w_s5 = w_in[0][:, :S5_WIDTH]
    w_fft = w_in[0][:, S5_WIDTH:S5_WIDTH + FFT_WIDTH]
    w_g = w_in[0][:, S5_WIDTH + FFT_WIDTH:]
    b_s5 = row(b_in[0][:S5_WIDTH])
    b_fft8 = jnp.concatenate([row(b_in[0][S5_WIDTH:S5_WIDTH + FFT_WIDTH]),
                              jnp.zeros((SUBLANES - 1, FFT_WIDTH), F32)], axis=0)
    b_g = row(b_in[0][S5_WIDTH + FFT_WIDTH:])
    w_fc, b_fc = _fft_weights(w_fft, b_fft8, fc)
    wcat = jnp.concatenate([w_s5, w_fc], axis=1).astype(BF16)
    bcat = jnp.concatenate([b_s5, b_fc[0:1]], axis=1)

    x2 = x[0]
    p_t, xr, xi = _proj(x2, emb_r, emb_c, lg, lb, 1.0 + sc1, sh1, wcat, bcat)
    pc_t = _ctx_proj(ctx[0], lg, lb, 1.0 + sc1c, sh1c, w_s5.astype(BF16), b_s5)

    b_c, a_q, a_p, trans, ctx_w = _s5_tables(
        s5_lambda_re[0], s5_lambda_im[0], s5_log_dt[0], s5_b_re[0], s5_b_im[0],
        s5_c_re[0], s5_c_im[0], s5_d[0])
    w_m, w_q, w_p = _s5_expand(b_c, a_q, a_p)
    y_t = _s5(p_t, pc_t, w_m, w_q, w_p, trans, ctx_w)

    yr, yi = _fft1(xr, xi, st1)
    zr = _fft2(yr, yi, st2)

    tri = (jnp.arange(TM)[:, None] < jnp.arange(TM)[None, :]).astype(BF16)
    br = jnp.broadcast_to(b_router[0].reshape(N_EXPERTS, 1), (N_EXPERTS, LANES))
    etri = (jnp.arange(N_EXPERTS)[:, None] > jnp.arange(N_EXPERTS)[None, :]).astype(BF16)
    wr_t = jnp.transpose(w_router[0])
    wr_hi = wr_t.astype(BF16)
    wr_split = jnp.stack([wr_hi, (wr_t - wr_hi.astype(F32)).astype(BF16)], axis=0)
    h1, u2, pos_t, gate_tok, counts = _mix(
        x2, emb_r, emb_c, lg, lb, 1.0 + sc1, sh1, w_g.astype(BF16), b_g, y_t, zr,
        w_glu[0].astype(BF16), row(b_glu[0]), w_br_s5[0].astype(BF16), w_br_fft[0].astype(BF16),
        row(b_br_fft[0]), w_out[0].astype(BF16), row(b_out[0]), g1, row(ln1_g[0]), row(ln1_b[0]),
        1.0 + sc2, sh2, wr_split, br, tri, etri)

    cnt = counts[:, :, 0].astype(jnp.int32)
    seg = (cnt + SEG_ALIGN - 1) // SEG_ALIGN * SEG_ALIGN
    seg_end = jnp.cumsum(seg, axis=1)
    seg_start = seg_end - seg
    padded = (jnp.sum(seg, axis=0) + BM - 1) // BM * BM
    pad_ends = jnp.cumsum(padded)
    seg_dest = (pad_ends - padded)[None, :] + jnp.cumsum(seg, axis=0) - seg
    chunk_row = jnp.arange(NCHK, dtype=jnp.int32) * SEG_ALIGN
    chunk_exp = jnp.minimum(jnp.sum(chunk_row[None, :, None] >= seg_end[:, None, :], axis=-1),
                            N_EXPERTS - 1)
    own = chunk_exp[:, :, None] == jnp.arange(N_EXPERTS, dtype=jnp.int32)[None, None, :]
    chunk_dest = (jnp.sum(jnp.where(own, (seg_dest - seg_start)[:, None, :], 0), axis=-1)
                  + chunk_row[None, :]).astype(jnp.int32).reshape(N_TILES, 1, NCHK)
    nchk = (seg_end[:, -1] // SEG_ALIGN).astype(jnp.int32)
    block_start = jnp.arange(N_BLOCKS_ALL, dtype=jnp.int32) * BM
    block_expert = jnp.minimum(jnp.sum(block_start[:, None] >= pad_ends[None, :], axis=1),
                               N_EXPERTS - 1).astype(jnp.int32)
    n_used = (pad_ends[-1:] // BM).astype(jnp.int32)
    opens = (block_start < pad_ends[-1]) & (
        block_expert != jnp.concatenate([jnp.full((1,), -1, jnp.int32), block_expert[:-1]]))
    run_id = jnp.where(opens, jnp.cumsum(opens.astype(jnp.int32)) - 1, -1).astype(jnp.int32)
    experts = jnp.arange(N_EXPERTS, dtype=jnp.int32)
    later = (experts[None, :] > block_expert[:, None]) & (padded[None, :] > 0)
    next_expert = jnp.min(jnp.where(later, experts[None, :], N_EXPERTS), axis=1)
    next_expert = jnp.where(next_expert < N_EXPERTS, next_expert, -1).astype(jnp.int32)

    spare = (ROWS + chunk_row)[None, None, :]
    chunk_table = jnp.concatenate(
        [spare,
         jnp.where(chunk_row[None, None, :] < (nchk * SEG_ALIGN)[:, None, None], chunk_dest, spare),
         spare], axis=0).astype(jnp.int32)
    buf = _dispatch(pad_ends.astype(jnp.int32), padded.astype(jnp.int32), chunk_table, pos_t, u2)
    mine = block_expert[:, None] == experts[None, :]
    filled = jnp.sum(jnp.where(mine, (pad_ends - padded + jnp.sum(seg, axis=0))[None, :], 0), axis=1)
    valid = jnp.clip(filled - block_start, 0, BM).astype(jnp.int32)
    y_buf = _ffn(block_expert, n_used, run_id, next_expert, valid, buf, w_up[0],
                 b_up[0].reshape(N_EXPERTS, 1, 2 * D), w_down[0], b_down[0].reshape(N_EXPERTS, 1, D))
    out = _combine(chunk_table, y_buf, h1, gate_tok, g2, row(ln2_g[0]), row(ln2_b[0]))
    return out.reshape(1, N_TOK, D)
```

```python
import functools
import math

import jax
import jax.numpy as jnp
import numpy as np
from jax import lax
from jax.experimental import pallas as pl
from jax.experimental.pallas import tpu as pltpu

F32 = jnp.float32
BF16 = jnp.bfloat16
HI = lax.Precision.HIGHEST

D = 1024
N_TOK = 16384
N_CTX = 256
GRID_W = 64
S5_GROUP = 16
S5_GROUPS = 32
S5_STATE = 64
S5_WIDTH = 512
FFT_GROUPS = 4
FFT_DIM = 128
FFT_WIDTH = 512
N_EXPERTS = 32
TOP_K = 4
LN_EPS = 1e-5
ALPHA = 2.0 ** 0.25
SWIGLU_ALPHA = 1.702
SWIGLU_LIMIT = 7.0

LANES = 128
SUBLANES = 8
VMEM_LIMIT = 56 * 1024 * 1024

CH = 8
N_CHUNK = N_TOK // CH
N_CHUNK_CTX = N_CTX // CH
NSEG = SUBLANES
SEG = N_CHUNK // NSEG
SCAN_UNROLL = 4
GPT = LANES // S5_GROUP
NJ = S5_WIDTH // LANES
CL = CH * LANES
SW = 4 * GPT * S5_STATE

FN = 128
FB = 16

TM = 512
TM_PROJ = 1024
N_TILES = N_TOK // TM
MIX_GROUPS = 1
BM = 512
FFN_TAIL = 128
N_SLOTS = N_TOK * TOP_K
SEG_ALIGN = 2 * SUBLANES
CAP_BLOCK = 256
CAP = -(-(TOP_K * TM + N_EXPERTS * (SEG_ALIGN - 1)) // CAP_BLOCK) * CAP_BLOCK
NCHK = CAP // SEG_ALIGN
N_BLOCKS = -(-(N_SLOTS + N_TILES * N_EXPERTS * (SEG_ALIGN - 1)) // BM) + N_EXPERTS
ROWS = N_BLOCKS * BM
N_BLOCKS_ALL = N_BLOCKS + CAP // BM
ROWS_ALL = N_BLOCKS_ALL * BM


def _cparams(sem):
    return pltpu.CompilerParams(dimension_semantics=sem, vmem_limit_bytes=VMEM_LIMIT)


def _layer_norm(x, g, b):
    mu = jnp.mean(x, axis=-1, keepdims=True)
    xc = x - mu
    var = jnp.mean(xc * xc, axis=-1, keepdims=True)
    return xc * lax.rsqrt(var + LN_EPS) * g + b


def _sigmoid(x):
    return 0.5 * jnp.tanh(0.5 * x) + 0.5


def _ada_kernel(c_ref, w_ref, b_ref, o_ref):
    c = c_ref[...]
    s = c * _sigmoid(c)
    o_ref[...] = jnp.dot(s, w_ref[...], preferred_element_type=F32, precision=HI) + b_ref[...]


def _ada(cc, w_ada, b_ada):
    nb = 4
    wb = 6 * D // nb
    return pl.pallas_call(
        _ada_kernel,
        grid=(nb,),
        in_specs=[pl.BlockSpec((SUBLANES, D), lambda i: (0, 0)),
                  pl.BlockSpec((D, wb), lambda i: (0, i)),
                  pl.BlockSpec((1, wb), lambda i: (0, i))],
        out_specs=pl.BlockSpec((SUBLANES, wb), lambda i: (0, i)),
        out_shape=jax.ShapeDtypeStruct((SUBLANES, 6 * D), F32),
        compiler_params=_cparams(("parallel",)),
        name="ada",
    )(cc, w_ada, b_ada)


def _fftw_kernel(w_ref, b_ref, f_ref, wo_ref, bo_ref):
    f = f_ref[...]
    wo_ref[...] = jnp.dot(w_ref[...], f, preferred_element_type=F32, precision=HI)
    bo_ref[...] = jnp.dot(b_ref[...], f, preferred_element_type=F32, precision=HI)


def _fft_weights(w_fft, b_fft8, fc):
    return pl.pallas_call(
        _fftw_kernel,
        out_shape=(jax.ShapeDtypeStruct((D, 2 * FFT_WIDTH), F32),
                   jax.ShapeDtypeStruct((SUBLANES, 2 * FFT_WIDTH), F32)),
        compiler_params=pltpu.CompilerParams(vmem_limit_bytes=VMEM_LIMIT),
        name="fftw",
    )(w_fft, b_fft8, fc)


def _pos_code(er_ref, ec_ref, tm):
    nr = tm // GRID_W
    er = er_ref[...]
    row = jnp.broadcast_to(er[:, None, :], (nr, GRID_W, D // 2)).reshape(tm, D // 2)
    col = jnp.concatenate([ec_ref[...]] * nr, axis=0)
    return jnp.concatenate([row, col], axis=-1)


def _to_chunk_major(val, scr_ref, out_ref, tm):
    for j in range(NJ):
        scr_ref[j] = val[:, j * LANES:(j + 1) * LANES]
    for t in range(CH):
        for j in range(NJ):
            piece = scr_ref[j, pl.ds(t, tm // CH, stride=CH), :]
            out_ref[t, :, j * LANES:(j + 1) * LANES] = piece.astype(out_ref.dtype)


def _proj_kernel(x_ref, er_ref, ec_ref, lg_ref, lb_ref, m_ref, s_ref, w_ref, b_ref,
                 p_ref, xr_ref, xi_ref, scr_ref):
    x = x_ref[...] + _pos_code(er_ref, ec_ref, TM_PROJ)
    h = _layer_norm(x, lg_ref[...], lb_ref[...])
    u = (h * m_ref[...] + s_ref[...]).astype(BF16)
    p = jnp.dot(u, w_ref[...], preferred_element_type=F32) + b_ref[...]
    _to_chunk_major(p[:, :S5_WIDTH], scr_ref, p_ref, TM_PROJ)
    xr_ref[...] = p[:, S5_WIDTH:S5_WIDTH + FFT_WIDTH].astype(BF16)
    xi_ref[...] = p[:, S5_WIDTH + FFT_WIDTH:].astype(BF16)


def _proj(x, emb_r, emb_c, lg, lb, m1, s1, wcat, bcat):
    nw = wcat.shape[1]
    vec = pl.BlockSpec((1, D), lambda i: (0, 0))
    return pl.pallas_call(
        _proj_kernel,
        grid=(N_TOK // TM_PROJ,),
        in_specs=[pl.BlockSpec((TM_PROJ, D), lambda i: (i, 0)),
                  pl.BlockSpec((TM_PROJ // GRID_W, D // 2), lambda i: (i, 0)),
                  pl.BlockSpec((GRID_W, D // 2), lambda i: (0, 0)),
                  vec, vec, vec, vec,
                  pl.BlockSpec((D, nw), lambda i: (0, 0)),
                  pl.BlockSpec((1, nw), lambda i: (0, 0))],
        out_specs=(pl.BlockSpec((CH, TM_PROJ // CH, S5_WIDTH), lambda i: (0, i, 0)),
                   pl.BlockSpec((TM_PROJ, FFT_WIDTH), lambda i: (i, 0)),
                   pl.BlockSpec((TM_PROJ, FFT_WIDTH), lambda i: (i, 0))),
        out_shape=(jax.ShapeDtypeStruct((CH, N_CHUNK, S5_WIDTH), BF16),
                   jax.ShapeDtypeStruct((N_TOK, FFT_WIDTH), BF16),
                   jax.ShapeDtypeStruct((N_TOK, FFT_WIDTH), BF16)),
        scratch_shapes=[pltpu.VMEM((NJ, TM_PROJ, LANES), F32)],
        compiler_params=_cparams(("parallel",)),
        name="proj",
    )(x, emb_r, emb_c, lg, lb, m1, s1, wcat, bcat)


def _ctx_proj_kernel(x_ref, lg_ref, lb_ref, m_ref, s_ref, w_ref, b_ref, p_ref, scr_ref):
    h = _layer_norm(x_ref[...], lg_ref[...], lb_ref[...])
    u = (h * m_ref[...] + s_ref[...]).astype(BF16)
    p = jnp.dot(u, w_ref[...], preferred_element_type=F32) + b_ref[...]
    _to_chunk_major(p, scr_ref, p_ref, N_CTX)


def _ctx_proj(ctx, lg, lb, m1, s1, w_s5, b_s5):
    return pl.pallas_call(
        _ctx_proj_kernel,
        out_shape=jax.ShapeDtypeStruct((CH, N_CHUNK_CTX, S5_WIDTH), BF16),
        scratch_shapes=[pltpu.VMEM((NJ, N_CTX, LANES), F32)],
        compiler_params=pltpu.CompilerParams(vmem_limit_bytes=VMEM_LIMIT),
        name="ctxproj",
    )(ctx, lg, lb, m1, s1, w_s5, b_s5)


def _s5_tables(lam_re, lam_im, log_dt, b_re, b_im, c_re, c_im, d_skip):
    dt = jnp.exp(log_dt)[..., None]
    zr = lam_re * dt
    zi = lam_im * dt

    def apow(m):
        m = jnp.asarray(m, F32)
        mag = jnp.exp(zr[..., None] * m)
        return mag * jnp.cos(zi[..., None] * m), mag * jnp.sin(zi[..., None] * m)

    a_re, a_im = apow(jnp.ones((1,), F32))
    a_re, a_im = a_re[..., 0], a_im[..., 0]
    den = lam_re * lam_re + lam_im * lam_im
    num_re = a_re - 1.0
    k_re = (num_re * lam_re + a_im * lam_im) / den
    k_im = (a_im * lam_re - num_re * lam_im) / den
    bb_re = k_re[..., None] * b_re - k_im[..., None] * b_im
    bb_im = k_re[..., None] * b_im + k_im[..., None] * b_re

    ks = jnp.arange(CH + 1, dtype=F32)
    pw_re, pw_im = apow(ks)
    kmag = jnp.exp(zr[:, :, None, :] * ks[None, None, :, None])
    pk_re = kmag * jnp.cos(zi[:, :, None, :] * ks[None, None, :, None])
    pk_im = kmag * jnp.sin(zi[:, :, None, :] * ks[None, None, :, None])
    bt_re, bt_im = jnp.swapaxes(b_re, 2, 3), jnp.swapaxes(b_im, 2, 3)
    bbt_re = k_re[:, :, None, :] * bt_re - k_im[:, :, None, :] * bt_im
    bbt_im = k_re[:, :, None, :] * bt_im + k_im[:, :, None, :] * bt_re

    ar, ai = pw_re[:, :, :, :CH, None], pw_im[:, :, :, :CH, None]
    cr = jnp.swapaxes(c_re, 2, 3)[:, :, :, None, :]
    ci = jnp.swapaxes(c_im, 2, 3)[:, :, :, None, :]
    ca = jnp.concatenate([cr * ar - ci * ai, -(cr * ai + ci * ar)], axis=2)
    ca = ca.reshape(2, S5_GROUPS, 2 * S5_STATE, CH * S5_GROUP)
    bbt = jnp.concatenate([bbt_re, bbt_im], axis=-1)
    taps = jnp.einsum('dghq,dgqn->dghn', bbt, ca, precision=HI)
    skip = (d_skip.reshape(S5_GROUPS, S5_GROUP, 1) * jnp.eye(S5_GROUP, dtype=F32)[None])
    taps = taps.at[0, :, :, :S5_GROUP].add(skip)
    b_c = jnp.transpose(taps.reshape(2, NJ, GPT, S5_GROUP, CH, S5_GROUP), (1, 4, 0, 2, 3, 5))
    b_c = b_c.reshape(NJ, 2 * CH, LANES, S5_GROUP)

    ef = (CH - 1) - jnp.arange(CH)
    eb = jnp.arange(CH)

    def q_part(d, e):
        pr = pk_re[d][:, e, None, :]
        pi = pk_im[d][:, e, None, :]
        br = bbt_re[d][:, None, :, :]
        bi = bbt_im[d][:, None, :, :]
        return pr * br - pi * bi, pr * bi + pi * br

    def q_rows(v):
        v = v.reshape(NJ, GPT, CH, S5_GROUP, S5_STATE)
        return jnp.transpose(v, (0, 2, 1, 3, 4)).reshape(NJ, CL, S5_STATE)
    a_q = jnp.stack([q_rows(v) for v in q_part(0, ef) + q_part(1, eb)], axis=0)

    of = jnp.arange(CH) + 1
    ob = CH - jnp.arange(CH)

    def p_part(d, e):
        pr = pw_re[d][..., e][:, :, :, None]
        pi = pw_im[d][..., e][:, :, :, None]
        return (ct_re[d] * pr - ct_im[d] * pi, -(ct_re[d] * pi + ct_im[d] * pr))
    ct_re = jnp.swapaxes(c_re, 2, 3)[:, :, :, None, :]
    ct_im = jnp.swapaxes(c_im, 2, 3)[:, :, :, None, :]
    a_p = jnp.stack([v.reshape(NJ, GPT * S5_STATE, CH * S5_GROUP)
                     for v in p_part(0, of) + p_part(1, ob)], axis=0)

    def lanes(v):
        return jnp.transpose(v.reshape(2, NJ, GPT * S5_STATE), (1, 0, 2))
    c_r, c_i = apow(jnp.full((1,), float(CH), F32))
    s_r, s_i = apow(jnp.full((1,), float(CH * SEG), F32))
    cr, ci, sr, si = (lanes(v[..., 0]) for v in (c_r, c_i, s_r, s_i))
    trans = jnp.stack([cr[:, 0], ci[:, 0], cr[:, 1], ci[:, 1],
                       sr[:, 0], si[:, 0], sr[:, 1], si[:, 1]], axis=1)

    cidx = jnp.arange(N_CHUNK_CTX, dtype=F32)
    wf_r, wf_i = apow(CH * (N_CHUNK_CTX - 1 - cidx))
    wb_r, wb_i = apow(CH * cidx)

    def ctx_lanes(v, d):
        return jnp.transpose(v[d].reshape(NJ, GPT * S5_STATE, N_CHUNK_CTX), (0, 2, 1))
    ctx_w = jnp.stack([ctx_lanes(wf_r, 0), ctx_lanes(wf_i, 0),
                       ctx_lanes(wb_r, 1), ctx_lanes(wb_i, 1)], axis=1)
    return b_c, a_q, a_p, trans, ctx_w


def _s5w_kernel(bc_ref, aq_ref, ap_ref, c16_ref, c64_ref, cm_ref, wm_ref, wq_ref, wp_ref):
    def expand(a, c, row_shift, col_shift):
        w = jnp.dot(a, c, preferred_element_type=F32)
        rg = (lax.broadcasted_iota(jnp.int32, (w.shape[0], 1), 0) >> row_shift) & (GPT - 1)
        cg = (lax.broadcasted_iota(jnp.int32, (1, w.shape[1]), 1) >> col_shift) & (GPT - 1)
        return jnp.where(rg == cg, w, 0.0)

    blk = [expand(bc_ref[0, kd], c16_ref[...], 4, 4) for kd in range(2 * CH)]
    for t in range(CH):
        for u in range(CH):
            b = blk[2 * (u - t)] if u > t else blk[2 * (t - u) + 1] if u < t else blk[0] + blk[1]
            wm_ref[0, t * LANES:(t + 1) * LANES, u * LANES:(u + 1) * LANES] = b.astype(BF16)
    half = GPT * S5_STATE
    for s in range(4):
        wq_ref[0, :, s * half:(s + 1) * half] = expand(aq_ref[s, 0], c64_ref[...], 4, 6).astype(BF16)
        wp_ref[0, s * half:(s + 1) * half, :] = expand(ap_ref[s, 0], cm_ref[...], 6, 4).astype(BF16)


def _s5_expand(b_c, a_q, a_p):
    rep = np.ones((1, GPT))
    c16 = jnp.asarray(np.kron(rep, np.eye(S5_GROUP)), F32).astype(BF16)
    c64 = jnp.asarray(np.kron(rep, np.eye(S5_STATE)), F32).astype(BF16)
    c_m = jnp.asarray(np.kron(np.eye(CH), np.kron(rep, np.eye(S5_GROUP))), F32).astype(BF16)
    b_c, a_q, a_p = b_c.astype(BF16), a_q.astype(BF16), a_p.astype(BF16)
    half = GPT * S5_STATE
    return pl.pallas_call(
        _s5w_kernel,
        grid=(NJ,),
        in_specs=[pl.BlockSpec((1, 2 * CH, LANES, S5_GROUP), lambda j: (j, 0, 0, 0)),
                  pl.BlockSpec((4, 1, CL, S5_STATE), lambda j: (0, j, 0, 0)),
                  pl.BlockSpec((4, 1, half, CH * S5_GROUP), lambda j: (0, j, 0, 0)),
                  pl.BlockSpec(c16.shape, lambda j: (0, 0)),
                  pl.BlockSpec(c64.shape, lambda j: (0, 0)),
                  pl.BlockSpec(c_m.shape, lambda j: (0, 0))],
        out_specs=(pl.BlockSpec((1, CL, CL), lambda j: (j, 0, 0)),
                   pl.BlockSpec((1, CL, SW), lambda j: (j, 0, 0)),
                   pl.BlockSpec((1, SW, CL), lambda j: (j, 0, 0))),
        out_shape=(jax.ShapeDtypeStruct((NJ, CL, CL), BF16),
                   jax.ShapeDtypeStruct((NJ, CL, SW), BF16),
                   jax.ShapeDtypeStruct((NJ, SW, CL), BF16)),
        compiler_params=_cparams(("parallel",)),
        name="s5w",
    )(b_c, a_q, a_p, c16, c64, c_m)


def _s5_kernel(p_ref, pc_ref, wm_ref, wq_ref, wp_ref, tr_ref, cw_ref, y_ref, v_ref):
    nq = NJ
    half = GPT * S5_STATE

    def chunk_rows(ref, r0, nrows):
        return jnp.concatenate([ref[t, pl.ds(r0, nrows), :] for t in range(CH)], axis=-1)

    def fill(k, c):
        r0 = pl.multiple_of(k * SEG, SEG)
        v = jnp.dot(chunk_rows(p_ref, r0, SEG), wq_ref[0], preferred_element_type=F32)
        for s in range(4 * nq):
            v_ref[s, pl.ds(k, SEG, stride=NSEG), :] = v[:, s * LANES:(s + 1) * LANES]
        return c
    lax.fori_loop(0, NSEG, fill, 0)

    vc = jnp.dot(chunk_rows(pc_ref, 0, N_CHUNK_CTX), wq_ref[0], preferred_element_type=F32)
    vfr, vfi, vbr, vbi = (vc[:, i * half:(i + 1) * half] for i in range(4))
    wfr, wfi, wbr, wbi = (cw_ref[0, i] for i in range(4))
    s0_fr = jnp.sum(wfr * vfr - wfi * vfi, axis=0, keepdims=True)
    s0_fi = jnp.sum(wfr * vfi + wfi * vfr, axis=0, keepdims=True)
    s0_br = jnp.sum(wbr * vbr - wbi * vbi, axis=0, keepdims=True)
    s0_bi = jnp.sum(wbr * vbi + wbi * vbr, axis=0, keepdims=True)

    tr = tr_ref[0]
    afr, afi, abr, abi = (jnp.broadcast_to(tr[i:i + 1], (NSEG, half)) for i in range(4))
    gfr, gfi, gbr, gbi = (tr[i:i + 1] for i in range(4, 8))

    def load_part(part, i):
        return jnp.concatenate(
            [v_ref[part * nq + q, pl.ds(pl.multiple_of(i * NSEG, NSEG), NSEG), :] for q in range(nq)],
            axis=-1)

    def store_part(part, i, val):
        for q in range(nq):
            v_ref[part * nq + q, pl.ds(pl.multiple_of(i * NSEG, NSEG), NSEG), :] = (
                val[:, q * LANES:(q + 1) * LANES])

    def step(i, carry, write):
        fr, fi, br, bi = carry
        ib = SEG - 1 - i
        ufr, ufi = load_part(0, i), load_part(1, i)
        ubr, ubi = load_part(2, ib), load_part(3, ib)
        if write:
            store_part(0, i, fr)
            store_part(1, i, fi)
            store_part(2, ib, br)
            store_part(3, ib, bi)
        return (afr * fr - afi * fi + ufr, afr * fi + afi * fr + ufi,
                abr * br - abi * bi + ubr, abr * bi + abi * br + ubi)

    zero = jnp.zeros((NSEG, half), F32)
    ffr, ffi, fbr, fbi = lax.fori_loop(0, SEG, functools.partial(step, write=False),
                                       (zero, zero, zero, zero), unroll=SCAN_UNROLL)

    rows_fr, rows_fi = [s0_fr], [s0_fi]
    for k in range(1, NSEG):
        pr, pi = rows_fr[-1], rows_fi[-1]
        rows_fr.append(gfr * pr - gfi * pi + ffr[k - 1:k])
        rows_fi.append(gfr * pi + gfi * pr + ffi[k - 1:k])
    rows_br, rows_bi = [s0_br], [s0_bi]
    for k in range(NSEG - 2, -1, -1):
        pr, pi = rows_br[0], rows_bi[0]
        rows_br.insert(0, gbr * pr - gbi * pi + fbr[k + 1:k + 2])
        rows_bi.insert(0, gbr * pi + gbi * pr + fbi[k + 1:k + 2])
    init = tuple(jnp.concatenate(r, axis=0) for r in (rows_fr, rows_fi, rows_br, rows_bi))

    lax.fori_loop(0, SEG, functools.partial(step, write=True), init, unroll=SCAN_UNROLL)

    def emit(k, c):
        r0 = pl.multiple_of(k * SEG, SEG)
        b = chunk_rows(p_ref, r0, SEG)
        sin = jnp.concatenate([v_ref[s, pl.ds(k, SEG, stride=NSEG), :] for s in range(4 * nq)], axis=-1)
        y = (jnp.dot(b, wm_ref[0], preferred_element_type=F32)
             + jnp.dot(sin.astype(BF16), wp_ref[0], preferred_element_type=F32))
        for t in range(CH):
            y_ref[t, pl.ds(r0, SEG), :] = y[:, t * LANES:(t + 1) * LANES].astype(y_ref.dtype)
        return c
    lax.fori_loop(0, NSEG, emit, 0)


def _s5(p_t, pc_t, w_m, w_q, w_p, trans, ctx_w):
    one = pl.Buffered(1)
    return pl.pallas_call(
        _s5_kernel,
        grid=(NJ,),
        in_specs=[pl.BlockSpec((CH, N_CHUNK, LANES), lambda j: (0, 0, j)),
                  pl.BlockSpec((CH, N_CHUNK_CTX, LANES), lambda j: (0, 0, j)),
                  pl.BlockSpec((1, CL, CL), lambda j: (j, 0, 0), pipeline_mode=one),
                  pl.BlockSpec((1, CL, SW), lambda j: (j, 0, 0), pipeline_mode=one),
                  pl.BlockSpec((1, SW, CL), lambda j: (j, 0, 0), pipeline_mode=one),
                  pl.BlockSpec((1, SUBLANES, GPT * S5_STATE), lambda j: (j, 0, 0)),
                  pl.BlockSpec((1, 4, N_CHUNK_CTX, GPT * S5_STATE), lambda j: (j, 0, 0, 0))],
        out_specs=pl.BlockSpec((CH, N_CHUNK, LANES), lambda j: (0, 0, j)),
        out_shape=jax.ShapeDtypeStruct((CH, N_CHUNK, S5_WIDTH), BF16),
        scratch_shapes=[pltpu.VMEM((4 * NJ, N_CHUNK, LANES), F32)],
        compiler_params=_cparams(("parallel",)),
        name="s5",
    )(p_t, pc_t, w_m, w_q, w_p, trans, ctx_w)


def _dft_tables():
    n = np.arange(FN)
    ang = 2.0 * np.pi * np.outer(n, n) / FN
    c, s = np.cos(ang), np.sin(ang)
    st1 = np.block([[c, s], [-s, c]])
    tw = 2.0 * np.pi * np.outer(n, n) / (FN * FN)
    wr, wi = np.cos(tw), -np.sin(tw)
    fr = c[None] * wr[:, None, :] + s[None] * wi[:, None, :]
    fi = c[None] * wi[:, None, :] - s[None] * wr[:, None, :]
    st2 = np.concatenate([fr, -fi], axis=-1)
    scale = 1.0 / math.sqrt(N_TOK * FFT_DIM)
    blk_c = np.kron(np.eye(FFT_GROUPS), c) * scale
    blk_s = np.kron(np.eye(FFT_GROUPS), s) * scale
    fc = np.concatenate([blk_c, -blk_s], axis=1)
    return (jnp.asarray(st1, F32).astype(BF16), jnp.asarray(st2, F32).astype(BF16), jnp.asarray(fc, F32))


FSL = FFT_WIDTH // LANES


FBH = FB // SUBLANES


def _block_to_slabs(blk, slab_ref, first, per_half):
    for bh in range(FBH):
        val = blk[:, bh * SUBLANES:(bh + 1) * SUBLANES, :].reshape(FN * SUBLANES, FFT_WIDTH)
        for s in range(FSL):
            slab_ref[bh * per_half + first + s] = val[:, s * LANES:(s + 1) * LANES]


def _slab_rows(b, first, per_half):
    return (b // SUBLANES) * per_half + first, pl.ds(b % SUBLANES, FN, stride=SUBLANES)


def _slabs_to_block(slab_ref, first, per_half):
    halves = []
    for bh in range(FBH):
        val = jnp.concatenate([slab_ref[bh * per_half + first + s] for s in range(FSL)], axis=-1)
        halves.append(val.reshape(FN, SUBLANES, FFT_WIDTH))
    return jnp.concatenate(halves, axis=1)


def _fft1_kernel(xr_ref, xi_ref, f_ref, yr_ref, yi_ref, in_ref, out_ref):
    _block_to_slabs(xr_ref[...].astype(F32), in_ref, 0, 2 * FSL)
    _block_to_slabs(xi_ref[...].astype(F32), in_ref, FSL, 2 * FSL)
    for b in range(FB):
        def part(first):
            base, rows = _slab_rows(b, first, 2 * FSL)
            return jnp.concatenate([in_ref[base + s, rows, :] for s in range(FSL)], axis=-1)
        xs = jnp.concatenate([part(0), part(FSL)], axis=0).astype(BF16)
        y = jnp.dot(f_ref[...], xs, preferred_element_type=F32)
        base, rows = _slab_rows(b, 0, 2 * FSL)
        for s in range(FSL):
            out_ref[base + s, rows, :] = y[:FN, s * LANES:(s + 1) * LANES]
            out_ref[base + FSL + s, rows, :] = y[FN:, s * LANES:(s + 1) * LANES]
    yr_ref[...] = _slabs_to_block(out_ref, 0, 2 * FSL).astype(BF16)
    yi_ref[...] = _slabs_to_block(out_ref, FSL, 2 * FSL).astype(BF16)


def _fft1(xr, xi, st1):
    spec = pl.BlockSpec((FN, FB, FFT_WIDTH), lambda i: (0, i, 0))
    slabs = pltpu.VMEM((FBH * 2 * FSL, FN * SUBLANES, LANES), F32)
    return pl.pallas_call(
        _fft1_kernel,
        grid=(FN // FB,),
        in_specs=[spec, spec, pl.BlockSpec((2 * FN, 2 * FN), lambda i: (0, 0))],
        out_specs=(spec, spec),
        out_shape=(jax.ShapeDtypeStruct((FN, FN, FFT_WIDTH), BF16),) * 2,
        scratch_shapes=[slabs, slabs],
        compiler_params=_cparams(("parallel",)),
        name="fft1",
    )(xr.reshape(FN, FN, FFT_WIDTH), xi.reshape(FN, FN, FFT_WIDTH), st1)


def _fft2_kernel(yr_ref, yi_ref, f_ref, z_ref, out_ref):
    for b in range(FB):
        ys = jnp.concatenate([yr_ref[b * FN:(b + 1) * FN, :], yi_ref[b * FN:(b + 1) * FN, :]], axis=0)
        z = jnp.dot(f_ref[b], ys, preferred_element_type=F32)
        base, rows = _slab_rows(b, 0, FSL)
        for s in range(FSL):
            out_ref[base + s, rows, :] = z[:, s * LANES:(s + 1) * LANES]
    z_ref[...] = _slabs_to_block(out_ref, 0, FSL).astype(BF16)


def _fft2(yr, yi, st2):
    rows = pl.BlockSpec((FB * FN, FFT_WIDTH), lambda i: (i, 0))
    z = pl.pallas_call(
        _fft2_kernel,
        grid=(FN // FB,),
        in_specs=[rows, rows, pl.BlockSpec((FB, FN, 2 * FN), lambda i: (i, 0, 0))],
        out_specs=pl.BlockSpec((FN, FB, FFT_WIDTH), lambda i: (0, i, 0)),
        out_shape=jax.ShapeDtypeStruct((FN, FN, FFT_WIDTH), BF16),
        scratch_shapes=[pltpu.VMEM((FBH * FSL, FN * SUBLANES, LANES), F32)],
        compiler_params=_cparams(("parallel",)),
        name="fft2",
    )(yr.reshape(N_TOK, FFT_WIDTH), yi.reshape(N_TOK, FFT_WIDTH), st2)
    return z.reshape(N_TOK, FFT_WIDTH)


def _gelu_tanh(x):
    return 0.5 * x * (1.0 + jnp.tanh(math.sqrt(2.0 / math.pi) * (x + 0.044715 * (x * x * x))))


def _mix_kernel(x_ref, er_ref, ec_ref, lg_ref, lb_ref, m1_ref, s1_ref, wg_ref, bg_ref,
                yt_ref, zr_ref, wglu_ref, bglu_ref, wbs_ref, wbf_ref, bbf_ref, wo_ref, bo_ref,
                g1_ref, l1g_ref, l1b_ref, m2_ref, s2_ref, wr_ref, br_ref, tri_ref, etri_ref,
                h1_ref, u2_ref, pos_ref, gate_ref, cnt_ref, scr_ref):
    pos = _pos_code(er_ref, ec_ref, TM)

    def front(r0, nr):
        rows = slice(r0, r0 + nr)
        h = _layer_norm(x_ref[rows, :] + pos[rows, :], lg_ref[...], lb_ref[...])
        u = (h * m1_ref[...] + s1_ref[...]).astype(BF16)

        c0, nc = r0 // CH, nr // CH
        for t in range(CH):
            for j in range(NJ):
                scr_ref[j, pl.ds(r0 + t, nc, stride=CH), :] = (
                    yt_ref[t, c0:c0 + nc, j * LANES:(j + 1) * LANES].astype(F32))
        ys = jnp.concatenate([scr_ref[j, rows, :] for j in range(NJ)], axis=-1)
        z = jnp.dot(_gelu_tanh(ys).astype(BF16), wglu_ref[...], preferred_element_type=F32) + bglu_ref[...]
        glu = (z[:, :S5_WIDTH] * _sigmoid(z[:, S5_WIDTH:])).astype(BF16)
        gates = _sigmoid(jnp.dot(u, wg_ref[...], preferred_element_type=F32) + bg_ref[...])
        y_s5 = jnp.dot(glu, wbs_ref[...], preferred_element_type=F32)
        y_fft = jnp.dot(zr_ref[rows, :], wbf_ref[...], preferred_element_type=F32) + bbf_ref[...]
        mixed = (gates[:, :D] * y_s5 + gates[:, D:] * y_fft).astype(BF16)
        y = jnp.dot(mixed, wo_ref[...], preferred_element_type=F32) + bo_ref[...]
        h1 = _layer_norm(ALPHA * h + g1_ref[...] * y, l1g_ref[...], l1b_ref[...])
        h1_ref[rows, :] = h1
        u2 = h1 * m2_ref[...] + s2_ref[...]
        u2_ref[rows, :] = u2.astype(BF16)
        u_hi = u2.astype(BF16)
        u_lo = (u2 - u_hi.astype(F32)).astype(BF16)

        def nt(a, b):
            return lax.dot_general(a, b, (((1,), (1,)), ((), ())), preferred_element_type=F32)
        return nt(wr_ref[0], u_hi) + nt(wr_ref[0], u_lo) + nt(wr_ref[1], u_hi)

    nr = TM // MIX_GROUPS
    logits = jnp.concatenate([front(g * nr, nr) for g in range(MIX_GROUPS)], axis=-1) + br_ref[:, 0:1]
    eidx = lax.broadcasted_iota(jnp.int32, (N_EXPERTS, TM), 0)
    vals, hots = [], []
    cur = logits
    for _k in range(TOP_K):
        m = jnp.max(cur, axis=0, keepdims=True)
        sel = jnp.min(jnp.where(cur == m, eidx, N_EXPERTS), axis=0, keepdims=True)
        hot = eidx == sel
        cur = jnp.where(hot, -jnp.inf, cur)
        vals.append(m)
        hots.append(hot)
    exps = [jnp.exp(v - vals[0]) for v in vals]
    den = exps[0] + exps[1] + exps[2] + exps[3]
    gate4 = jnp.concatenate([e / den for e in exps], axis=0)

    hot_sum = (hots[0] | hots[1] | hots[2] | hots[3]).astype(F32)
    before = jnp.dot(hot_sum.astype(BF16), tri_ref[...], preferred_element_type=F32)
    cnt = jnp.broadcast_to(jnp.sum(hot_sum, axis=1, keepdims=True), (N_EXPERTS, LANES))
    cnt8 = jnp.floor((cnt + (SEG_ALIGN - 1)) * (1.0 / SEG_ALIGN)) * SEG_ALIGN
    seg0 = jnp.dot(etri_ref[...], cnt8.astype(BF16), preferred_element_type=F32)
    tot = seg0[:, 0:1] + before
    pos4 = jnp.concatenate(
        [jnp.sum(jnp.where(hk, tot, 0.0), axis=0, keepdims=True) for hk in hots], axis=0)
    pos_ref[...] = pos4.astype(jnp.int32)
    cnt_ref[0] = cnt

    gpad = jnp.concatenate([gate4, pos4, jnp.zeros((LANES - 2 * TOP_K, TM), F32)], axis=0)
    gate_ref[...] = gpad.T


def _mix(x, emb_r, emb_c, lg, lb, m1, s1, wg, bg, y_t, zr, wglu, bglu, wbs, wbf, bbf, wo, bo,
         g1, l1g, l1b, m2, s2, wr_t, br, tri, etri):
    vec = pl.BlockSpec((1, D), lambda i: (0, 0))

    def full(a):
        return pl.BlockSpec(a.shape, lambda i: (0,) * a.ndim)
    return pl.pallas_call(
        _mix_kernel,
        grid=(N_TOK // TM,),
        in_specs=[pl.BlockSpec((TM, D), lambda i: (i, 0)),
                  pl.BlockSpec((TM // GRID_W, D // 2), lambda i: (i, 0)),
                  pl.BlockSpec((GRID_W, D // 2), lambda i: (0, 0)),
                  vec, vec, vec, vec, full(wg), full(bg),
                  pl.BlockSpec((CH, TM // CH, S5_WIDTH), lambda i: (0, i, 0)),
                  pl.BlockSpec((TM, FFT_WIDTH), lambda i: (i, 0)),
                  full(wglu), full(bglu), full(wbs), full(wbf), full(bbf), full(wo), full(bo),
                  vec, vec, vec, vec, vec, full(wr_t), full(br), full(tri), full(etri)],
        out_specs=(pl.BlockSpec((TM, D), lambda i: (i, 0)),
                   pl.BlockSpec((TM, D), lambda i: (i, 0)),
                   pl.BlockSpec((TOP_K, TM), lambda i: (0, i)),
                   pl.BlockSpec((TM, LANES), lambda i: (i, 0)),
                   pl.BlockSpec((1, N_EXPERTS, LANES), lambda i: (i, 0, 0))),
        out_shape=(jax.ShapeDtypeStruct((N_TOK, D), F32),
                   jax.ShapeDtypeStruct((N_TOK, D), BF16),
                   jax.ShapeDtypeStruct((TOP_K, N_TOK), jnp.int32),
                   jax.ShapeDtypeStruct((N_TOK, LANES), F32),
                   jax.ShapeDtypeStruct((N_TILES, N_EXPERTS, LANES), F32)),
        scratch_shapes=[pltpu.VMEM((NJ, TM, LANES), F32)],
        compiler_params=_cparams(("parallel",)),
        name="mix",
    )(x, emb_r, emb_c, lg, lb, m1, s1, wg, bg, y_t, zr, wglu, bglu, wbs, wbf, bbf, wo, bo,
      g1, l1g, l1b, m2, s2, wr_t, br, tri, etri)


def _on_parity(i, fn):
    @pl.when(i % 2 == 0)
    def _():
        fn(0)

    @pl.when(i % 2 == 1)
    def _():
        fn(1)


def _dispatch_kernel(pend_ref, padded_ref, dprev_ref, dest_ref, pos_ref, u_ref, buf_ref,
                     sorted_ref, zero_ref, zsem, sems):
    i = pl.program_id(0)

    @pl.when(i == 0)
    def _():
        zero_ref[...] = jnp.zeros_like(zero_ref)
        n_used = pend_ref[N_EXPERTS - 1] // BM

        def clear_copy(start):
            return pltpu.make_async_copy(
                zero_ref, buf_ref.at[pl.ds(pl.multiple_of(start, BM), BM)], zsem)

        def each(fn):
            def expert(e, c):
                @pl.when(padded_ref[e] > 0)
                def _():
                    fn(clear_copy(pend_ref[e] - BM))
                return c
            lax.fori_loop(0, N_EXPERTS, expert, 0)

            def tail(b, c):
                fn(clear_copy(b * BM))
                return c
            lax.fori_loop(n_used, N_BLOCKS_ALL, tail, 0)
        each(lambda cp: cp.start())
        each(lambda cp: cp.wait())

        sorted_ref[...] = jnp.zeros_like(sorted_ref)

    def chunk_copy(slot, table_ref, j):
        dst = pl.multiple_of(table_ref[0, 0, j], SEG_ALIGN)
        return pltpu.make_async_copy(sorted_ref.at[slot, pl.ds(j * SEG_ALIGN, SEG_ALIGN)],
                                     buf_ref.at[pl.ds(dst, SEG_ALIGN)], sems.at[slot])

    def drain(slot):
        pltpu.make_async_copy(sorted_ref.at[slot], buf_ref.at[pl.ds(0, CAP)], sems.at[slot]).wait()

    def run(slot):
        pos = pos_ref[...]
        u = u_ref[...]
        n_rb = CAP // CAP_BLOCK
        per_rb = NCHK // (n_rb // 2)
        for rb in range(n_rb):
            for j in range(rb * per_rb, min((rb + 1) * per_rb, NCHK)):
                chunk_copy(1 - slot, dprev_ref, j).start()
            rows = lax.broadcasted_iota(jnp.int32, (CAP_BLOCK, TM), 0) + rb * CAP_BLOCK
            hit = rows == pos[0:1]
            for k in range(1, TOP_K):
                hit = hit | (rows == pos[k:k + 1])
            onehot = jnp.where(hit, 1.0, 0.0).astype(BF16)
            sorted_ref[slot, rb * CAP_BLOCK:(rb + 1) * CAP_BLOCK, :] = jnp.dot(
                onehot, u, preferred_element_type=F32).astype(BF16)
        drain(1 - slot)

        @pl.when(i == N_TILES - 1)
        def _():
            def issue(j, c):
                chunk_copy(slot, dest_ref, j).start()
                return c
            lax.fori_loop(0, NCHK, issue, 0)
            drain(slot)
    _on_parity(i, run)


def _dispatch(pad_ends, padded, chunk_table, pos_t, u2):
    return pl.pallas_call(
        _dispatch_kernel,
        grid_spec=pltpu.PrefetchScalarGridSpec(
            num_scalar_prefetch=2,
            grid=(N_TILES,),
            in_specs=[pl.BlockSpec((1, 1, NCHK), lambda i, a, b: (i, 0, 0), memory_space=pltpu.SMEM),
                      pl.BlockSpec((1, 1, NCHK), lambda i, a, b: (i + 1, 0, 0), memory_space=pltpu.SMEM),
                      pl.BlockSpec((TOP_K, TM), lambda i, a, b: (0, i)),
                      pl.BlockSpec((TM, D), lambda i, a, b: (i, 0))],
            out_specs=pl.BlockSpec(memory_space=pl.ANY),
            scratch_shapes=[pltpu.VMEM((2, CAP, D), BF16),
                            pltpu.VMEM((BM, D), BF16),
                            pltpu.SemaphoreType.DMA(()),
                            pltpu.SemaphoreType.DMA((2,))]),
        out_shape=jax.ShapeDtypeStruct((ROWS_ALL, D), BF16),
        compiler_params=_cparams(("arbitrary",)),
        name="dispatch",
    )(pad_ends, padded, chunk_table, chunk_table, pos_t, u2)


def _ffn_kernel(be_ref, nu_ref, run_ref, nxt_ref, valid_ref, x_ref, wu_hbm, bu_ref, wd_hbm, bd_ref,
                y_ref, wu_ref, wd_ref, wub_ref, wdb_ref, sems):
    i = pl.program_id(0)
    used = i < nu_ref[0]

    def weight_copies(e, slot):
        return (pltpu.make_async_copy(wu_hbm.at[e], wu_ref.at[slot], sems.at[slot]),
                pltpu.make_async_copy(wd_hbm.at[e], wd_ref.at[slot], sems.at[slot]))

    @pl.when(used)
    def _():
        run = run_ref[i]

        @pl.when(run >= 0)
        def _():
            def open_run(slot):
                @pl.when(run == 0)
                def _():
                    for cp in weight_copies(be_ref[i], slot):
                        cp.start()

                @pl.when(nxt_ref[i] >= 0)
                def _():
                    for cp in weight_copies(nxt_ref[i], 1 - slot):
                        cp.start()
                for cp in weight_copies(be_ref[i], slot):
                    cp.wait()
                wub_ref[...] = wu_ref[slot].astype(BF16)
                wdb_ref[...] = wd_ref[slot].astype(BF16)
            _on_parity(run, open_run)

        def expert_rows(r0, nr):
            rows = slice(r0, r0 + nr)
            e = be_ref[i]
            h = (jnp.dot(x_ref[rows, :], wub_ref[...], preferred_element_type=F32)
                 + bu_ref[pl.ds(e, 1), :])
            h_glu = jnp.minimum(h[:, :D], SWIGLU_LIMIT)
            h_lin = jnp.clip(h[:, D:], -SWIGLU_LIMIT, SWIGLU_LIMIT)
            act = (h_glu * _sigmoid(SWIGLU_ALPHA * h_glu) * (h_lin + 1.0)).astype(BF16)
            y_ref[rows, :] = (jnp.dot(act, wdb_ref[...], preferred_element_type=F32)
                              + bd_ref[pl.ds(e, 1), :]).astype(BF16)

        valid = valid_ref[i]

        @pl.when(valid == BM)
        def _():
            expert_rows(0, BM)

        @pl.when(valid < BM)
        def _():
            for r0 in range(0, BM, FFN_TAIL):
                @pl.when(r0 < valid)
                def _(r0=r0):
                    expert_rows(r0, FFN_TAIL)

                @pl.when(r0 >= valid)
                def _(r0=r0):
                    y_ref[r0:r0 + FFN_TAIL, :] = jnp.zeros((FFN_TAIL, D), BF16)

    @pl.when(jnp.logical_not(used))
    def _():
        y_ref[...] = jnp.zeros_like(y_ref)


def _ffn(block_expert, n_used, run_id, next_expert, valid, buf, w_up, b_up, w_down, b_down):
    def blk(i, be, nu, *_):
        return jnp.minimum(i, nu[0] - 1)
    return pl.pallas_call(
        _ffn_kernel,
        grid_spec=pltpu.PrefetchScalarGridSpec(
            num_scalar_prefetch=5,
            grid=(N_BLOCKS_ALL,),
            in_specs=[pl.BlockSpec((BM, D), lambda i, *s: (blk(i, *s), 0)),
                      pl.BlockSpec(memory_space=pl.ANY),
                      pl.BlockSpec((N_EXPERTS, 2 * D), lambda i, *s: (0, 0)),
                      pl.BlockSpec(memory_space=pl.ANY),
                      pl.BlockSpec((N_EXPERTS, D), lambda i, *s: (0, 0))],
            out_specs=pl.BlockSpec((BM, D), lambda i, *s: (i, 0)),
            scratch_shapes=[pltpu.VMEM((2, D, 2 * D), F32),
                            pltpu.VMEM((2, D, D), F32),
                            pltpu.VMEM((D, 2 * D), BF16),
                            pltpu.VMEM((D, D), BF16),
                            pltpu.SemaphoreType.DMA((2,))]),
        out_shape=jax.ShapeDtypeStruct((ROWS_ALL, D), BF16),
        compiler_params=_cparams(("arbitrary",)),
        name="ffn",
    )(block_expert, n_used, run_id, next_expert, valid, buf, w_up, b_up, w_down, b_down)


def _combine_kernel(dest_ref, dnext_ref, y_ref, h1_ref, gate_ref, g2_ref, lg_ref, lb_ref,
                    o_ref, sorted_ref, sems):
    i = pl.program_id(0)

    def chunk_copy(slot, table_ref, j):
        src = pl.multiple_of(table_ref[0, 0, j], SEG_ALIGN)
        return pltpu.make_async_copy(y_ref.at[pl.ds(src, SEG_ALIGN)],
                                     sorted_ref.at[slot, pl.ds(j * SEG_ALIGN, SEG_ALIGN)],
                                     sems.at[slot])

    def drain(slot):
        pltpu.make_async_copy(y_ref.at[pl.ds(0, CAP)], sorted_ref.at[slot], sems.at[slot]).wait()

    @pl.when(i == 0)
    def _():
        def issue(j, c):
            chunk_copy(0, dest_ref, j).start()
            return c
        lax.fori_loop(0, NCHK, issue, 0)

    def run(slot):
        drain(slot)

        gp = gate_ref[...]
        m = jnp.zeros((TM, D), F32)
        n_cb = CAP // CAP_BLOCK
        per_cb = NCHK // (n_cb // 2)
        for cb in range(n_cb):
            for j in range(cb * per_cb, min((cb + 1) * per_cb, NCHK)):
                chunk_copy(1 - slot, dnext_ref, j).start()
            cols = (lax.broadcasted_iota(jnp.int32, (TM, CAP_BLOCK), 1) + cb * CAP_BLOCK).astype(F32)
            g = jnp.where(cols == gp[:, TOP_K:TOP_K + 1], gp[:, 0:1], 0.0)
            for k in range(1, TOP_K):
                g = g + jnp.where(cols == gp[:, TOP_K + k:TOP_K + k + 1], gp[:, k:k + 1], 0.0)
            rows = sorted_ref[slot, cb * CAP_BLOCK:(cb + 1) * CAP_BLOCK, :]
            m = m + jnp.dot(g.astype(BF16), rows, preferred_element_type=F32)
        o_ref[...] = _layer_norm(ALPHA * h1_ref[...] + g2_ref[...] * m, lg_ref[...], lb_ref[...])

        @pl.when(i == N_TILES - 1)
        def _():
            drain(1 - slot)
    _on_parity(i, run)


def _combine(chunk_table, y_buf, h1, gate_tok, g2, lg, lb):
    vec = pl.BlockSpec((1, D), lambda i: (0, 0))
    return pl.pallas_call(
        _combine_kernel,
        grid_spec=pltpu.PrefetchScalarGridSpec(
            num_scalar_prefetch=0,
            grid=(N_TILES,),
            in_specs=[pl.BlockSpec((1, 1, NCHK), lambda i: (i + 1, 0, 0), memory_space=pltpu.SMEM),
                      pl.BlockSpec((1, 1, NCHK), lambda i: (i + 2, 0, 0), memory_space=pltpu.SMEM),
                      pl.BlockSpec(memory_space=pl.ANY),
                      pl.BlockSpec((TM, D), lambda i: (i, 0)),
                      pl.BlockSpec((TM, LANES), lambda i: (i, 0)),
                      vec, vec, vec],
            out_specs=pl.BlockSpec((TM, D), lambda i: (i, 0)),
            scratch_shapes=[pltpu.VMEM((2, CAP, D), BF16),
                            pltpu.SemaphoreType.DMA((2,))]),
        out_shape=jax.ShapeDtypeStruct((N_TOK, D), F32),
        compiler_params=_cparams(("arbitrary",)),
        name="combine",
    )(chunk_table, chunk_table, y_buf, h1, gate_tok, g2, lg, lb)


def _sincos_tables():
    q = D // 4
    omega = 1.0 / (10000.0 ** (jnp.arange(q, dtype=F32) / q))

    def emb(n):
        ang = jnp.arange(n, dtype=F32)[:, None] * omega[None, :]
        return jnp.concatenate([jnp.sin(ang), jnp.cos(ang)], axis=-1)
    return emb(N_TOK // GRID_W), emb(GRID_W)


def _tile_slots(a, tile):
    return jnp.transpose(a.reshape(TOP_K, N_TOK // tile, tile), (1, 0, 2)).reshape(N_TOK // tile, 1, TOP_K * tile)


def kernel(x, c, ctx, c_ctx, ln_in_g, ln_in_b, w_ada, b_ada, w_in, b_in, s5_lambda_re, s5_lambda_im, s5_log_dt, s5_b_re, s5_b_im, s5_c_re, s5_c_im, s5_d, w_glu, b_glu, w_br_s5, w_br_fft, b_br_fft, w_out, b_out, ln1_g, ln1_b, w_router, b_router, w_up, b_up, w_down, b_down, ln2_g, ln2_b):
    assert x.shape == (1, N_TOK, D) and ctx.shape == (1, N_CTX, D) and w_ada.shape[0] == 1
    row = lambda v: v.reshape(1, -1).astype(F32)

    cc = jnp.concatenate([c.reshape(1, D), c_ctx.reshape(1, D), jnp.zeros((SUBLANES - 2, D), F32)], axis=0)
    ada = _ada(cc, w_ada[0], row(b_ada[0]))
    sh1, sc1, g1, sh2, sc2, g2 = (ada[0:1, k * D:(k + 1) * D] for k in range(6))
    sh1c, sc1c = ada[1:2, 0:D], ada[1:2, D:2 * D]

    emb_r, emb_c = _sincos_tables()
    st1, st2, fc = _dft_tables()
    lg, lb = row(ln_in_g), row(ln_in_b)

    w_s5 = w_in[0][:, :S5_WIDTH]
    w_fft = w_in[0][:, S5_WIDTH:S5_WIDTH + FFT_WIDTH]
    w_g = w_in[0][:, S5_WIDTH + FFT_WIDTH:]
    b_s5 = row(b_in[0][:S5_WIDTH])
    b_fft8 = jnp.concatenate([row(b_in[0][S5_WIDTH:S5_WIDTH + FFT_WIDTH]),
                              jnp.zeros((SUBLANES - 1, FFT_WIDTH), F32)], axis=0)
    b_g = row(b_in[0][S5_WIDTH + FFT_WIDTH:])
    w_fc, b_fc = _fft_weights(w_fft, b_fft8, fc)
    wcat = jnp.concatenate([w_s5, w_fc], axis=1).astype(BF16)
    bcat = jnp.concatenate([b_s5, b_fc[0:1]], axis=1)

    x2 = x[0]
    p_t, xr, xi = _proj(x2, emb_r, emb_c, lg, lb, 1.0 + sc1, sh1, wcat, bcat)
    pc_t = _ctx_proj(ctx[0], lg, lb, 1.0 + sc1c, sh1c, w_s5.astype(BF16), b_s5)

    b_c, a_q, a_p, trans, ctx_w = _s5_tables(
        s5_lambda_re[0], s5_lambda_im[0], s5_log_dt[0], s5_b_re[0], s5_b_im[0],
        s5_c_re[0], s5_c_im[0], s5_d[0])
    w_m, w_q, w_p = _s5_expand(b_c, a_q, a_p)
    y_t = _s5(p_t, pc_t, w_m, w_q, w_p, trans, ctx_w)

    yr, yi = _fft1(xr, xi, st1)
    zr = _fft2(yr, yi, st2)

    tri = (jnp.arange(TM)[:, None] < jnp.arange(TM)[None, :]).astype(BF16)
    br = jnp.broadcast_to(b_router[0].reshape(N_EXPERTS, 1), (N_EXPERTS, LANES))
    etri = (jnp.arange(N_EXPERTS)[:, None] > jnp.arange(N_EXPERTS)[None, :]).astype(BF16)
    wr_t = jnp.transpose(w_router[0])
    wr_hi = wr_t.astype(BF16)
    wr_split = jnp.stack([wr_hi, (wr_t - wr_hi.astype(F32)).astype(BF16)], axis=0)
    h1, u2, pos_t, gate_tok, counts = _mix(
        x2, emb_r, emb_c, lg, lb, 1.0 + sc1, sh1, w_g.astype(BF16), b_g, y_t, zr,
        w_glu[0].astype(BF16), row(b_glu[0]), w_br_s5[0].astype(BF16), w_br_fft[0].astype(BF16),
        row(b_br_fft[0]), w_out[0].astype(BF16), row(b_out[0]), g1, row(ln1_g[0]), row(ln1_b[0]),
        1.0 + sc2, sh2, wr_split, br, tri, etri)

    cnt = counts[:, :, 0].astype(jnp.int32)
    seg = (cnt + SEG_ALIGN - 1) // SEG_ALIGN * SEG_ALIGN
    seg_end = jnp.cumsum(seg, axis=1)
    seg_start = seg_end - seg
    padded = (jnp.sum(seg, axis=0) + BM - 1) // BM * BM
    pad_ends = jnp.cumsum(padded)
    seg_dest = (pad_ends - padded)[None, :] + jnp.cumsum(seg, axis=0) - seg
    chunk_row = jnp.arange(NCHK, dtype=jnp.int32) * SEG_ALIGN
    chunk_exp = jnp.minimum(jnp.sum(chunk_row[None, :, None] >= seg_end[:, None, :], axis=-1),
                            N_EXPERTS - 1)
    own = chunk_exp[:, :, None] == jnp.arange(N_EXPERTS, dtype=jnp.int32)[None, None, :]
    chunk_dest = (jnp.sum(jnp.where(own, (seg_dest - seg_start)[:, None, :], 0), axis=-1)
                  + chunk_row[None, :]).astype(jnp.int32).reshape(N_TILES, 1, NCHK)
    nchk = (seg_end[:, -1] // SEG_ALIGN).astype(jnp.int32)
    block_start = jnp.arange(N_BLOCKS_ALL, dtype=jnp.int32) * BM
    block_expert = jnp.minimum(jnp.sum(block_start[:, None] >= pad_ends[None, :], axis=1),
                               N_EXPERTS - 1).astype(jnp.int32)
    n_used = (pad_ends[-1:] // BM).astype(jnp.int32)
    opens = (block_start < pad_ends[-1]) & (
        block_expert != jnp.concatenate([jnp.full((1,), -1, jnp.int32), block_expert[:-1]]))
    run_id = jnp.where(opens, jnp.cumsum(opens.astype(jnp.int32)) - 1, -1).astype(jnp.int32)
    experts = jnp.arange(N_EXPERTS, dtype=jnp.int32)
    later = (experts[None, :] > block_expert[:, None]) & (padded[None, :] > 0)
    next_expert = jnp.min(jnp.where(later, experts[None, :], N_EXPERTS), axis=1)
    next_expert = jnp.where(next_expert < N_EXPERTS, next_expert, -1).astype(jnp.int32)

    spare = (ROWS + chunk_row)[None, None, :]
    chunk_table = jnp.concatenate(
        [spare,
         jnp.where(chunk_row[None, None, :] < (nchk * SEG_ALIGN)[:, None, None], chunk_dest, spare),
         spare], axis=0).astype(jnp.int32)
    buf = _dispatch(pad_ends.astype(jnp.int32), padded.astype(jnp.int32), chunk_table, pos_t, u2)
    mine = block_expert[:, None] == experts[None, :]
    filled = jnp.sum(jnp.where(mine, (pad_ends - padded + jnp.sum(seg, axis=0))[None, :], 0), axis=1)
    valid = jnp.clip(filled - block_start, 0, BM).astype(jnp.int32)
    y_buf = _ffn(block_expert, n_used, run_id, next_expert, valid, buf, w_up[0],
                 b_up[0], w_down[0], b_down[0])
    out = _combine(chunk_table, y_buf, h1, gate_tok, g2, row(ln2_g[0]), row(ln2_b[0]))
    return out.reshape(1, N_TOK, D)
```

```python
import functools
import math

import jax
import jax.numpy as jnp
import numpy as np
from jax import lax
from jax.experimental import pallas as pl
from jax.experimental.pallas import tpu as pltpu

F32 = jnp.float32
BF16 = jnp.bfloat16
HI = lax.Precision.HIGHEST

D = 1024
N_TOK = 16384
N_CTX = 256
GRID_W = 64
S5_GROUP = 16
S5_GROUPS = 32
S5_STATE = 64
S5_WIDTH = 512
FFT_GROUPS = 4
FFT_DIM = 128
FFT_WIDTH = 512
N_EXPERTS = 32
TOP_K = 4
LN_EPS = 1e-5
ALPHA = 2.0 ** 0.25
SWIGLU_ALPHA = 1.702
SWIGLU_LIMIT = 7.0

LANES = 128
SUBLANES = 8
VMEM_LIMIT = 56 * 1024 * 1024

CH = 8
N_CHUNK = N_TOK // CH
N_CHUNK_CTX = N_CTX // CH
NSEG = SUBLANES
SEG = N_CHUNK // NSEG
SCAN_UNROLL = 4
GPT = LANES // S5_GROUP
NJ = S5_WIDTH // LANES
CL = CH * LANES
SW = 4 * GPT * S5_STATE

FN = 128
FB = 16

TM = 512
TM_PROJ = 1024
N_TILES = N_TOK // TM
MIX_GROUPS = 1
BM = 1024
ZB = 512
FFN_HALF = 512
FFN_TAIL = 128
N_SLOTS = N_TOK * TOP_K
SEG_ALIGN = 2 * SUBLANES
CAP_BLOCK = 256
CAP = -(-(TOP_K * TM + N_EXPERTS * (SEG_ALIGN - 1)) // CAP_BLOCK) * CAP_BLOCK
NCHK = CAP // SEG_ALIGN
N_BLOCKS = -(-(N_SLOTS + N_TILES * N_EXPERTS * (SEG_ALIGN - 1)) // BM) + N_EXPERTS
ROWS = N_BLOCKS * BM
N_BLOCKS_ALL = N_BLOCKS + -(-CAP // BM)
ROWS_ALL = N_BLOCKS_ALL * BM


def _cparams(sem):
    return pltpu.CompilerParams(dimension_semantics=sem, vmem_limit_bytes=VMEM_LIMIT)


def _layer_norm(x, g, b):
    mu = jnp.mean(x, axis=-1, keepdims=True)
    xc = x - mu
    var = jnp.mean(xc * xc, axis=-1, keepdims=True)
    return xc * lax.rsqrt(var + LN_EPS) * g + b


def _sigmoid(x):
    return 0.5 * jnp.tanh(0.5 * x) + 0.5


def _ada_kernel(c_ref, w_ref, b_ref, o_ref):
    c = c_ref[...]
    s = c * _sigmoid(c)
    o_ref[...] = jnp.dot(s, w_ref[...], preferred_element_type=F32, precision=HI) + b_ref[...]


def _ada(cc, w_ada, b_ada):
    nb = 4
    wb = 6 * D // nb
    return pl.pallas_call(
        _ada_kernel,
        grid=(nb,),
        in_specs=[pl.BlockSpec((SUBLANES, D), lambda i: (0, 0)),
                  pl.BlockSpec((D, wb), lambda i: (0, i)),
                  pl.BlockSpec((1, wb), lambda i: (0, i))],
        out_specs=pl.BlockSpec((SUBLANES, wb), lambda i: (0, i)),
        out_shape=jax.ShapeDtypeStruct((SUBLANES, 6 * D), F32),
        compiler_params=_cparams(("parallel",)),
        name="ada",
    )(cc, w_ada, b_ada)


def _fftw_kernel(w_ref, b_ref, f_ref, wo_ref, bo_ref):
    f = f_ref[...]
    wo_ref[...] = jnp.dot(w_ref[...], f, preferred_element_type=F32, precision=HI)
    bo_ref[...] = jnp.dot(b_ref[...], f, preferred_element_type=F32, precision=HI)


def _fft_weights(w_fft, b_fft8, fc):
    return pl.pallas_call(
        _fftw_kernel,
        out_shape=(jax.ShapeDtypeStruct((D, 2 * FFT_WIDTH), F32),
                   jax.ShapeDtypeStruct((SUBLANES, 2 * FFT_WIDTH), F32)),
        compiler_params=pltpu.CompilerParams(vmem_limit_bytes=VMEM_LIMIT),
        name="fftw",
    )(w_fft, b_fft8, fc)


def _pos_code(er_ref, ec_ref, tm):
    nr = tm // GRID_W
    er = er_ref[...]
    row = jnp.broadcast_to(er[:, None, :], (nr, GRID_W, D // 2)).reshape(tm, D // 2)
    col = jnp.concatenate([ec_ref[...]] * nr, axis=0)
    return jnp.concatenate([row, col], axis=-1)


def _to_chunk_major(val, scr_ref, out_ref, tm):
    for j in range(NJ):
        scr_ref[j] = val[:, j * LANES:(j + 1) * LANES]
    for t in range(CH):
        for j in range(NJ):
            piece = scr_ref[j, pl.ds(t, tm // CH, stride=CH), :]
            out_ref[t, :, j * LANES:(j + 1) * LANES] = piece.astype(out_ref.dtype)


def _proj_kernel(x_ref, er_ref, ec_ref, lg_ref, lb_ref, m_ref, s_ref, w_ref, b_ref,
                 p_ref, xr_ref, xi_ref, scr_ref):
    x = x_ref[...] + _pos_code(er_ref, ec_ref, TM_PROJ)
    h = _layer_norm(x, lg_ref[...], lb_ref[...])
    u = (h * m_ref[...] + s_ref[...]).astype(BF16)
    p = jnp.dot(u, w_ref[...], preferred_element_type=F32) + b_ref[...]
    _to_chunk_major(p[:, :S5_WIDTH], scr_ref, p_ref, TM_PROJ)
    xr_ref[...] = p[:, S5_WIDTH:S5_WIDTH + FFT_WIDTH].astype(BF16)
    xi_ref[...] = p[:, S5_WIDTH + FFT_WIDTH:].astype(BF16)


def _proj(x, emb_r, emb_c, lg, lb, m1, s1, wcat, bcat):
    nw = wcat.shape[1]
    vec = pl.BlockSpec((1, D), lambda i: (0, 0))
    return pl.pallas_call(
        _proj_kernel,
        grid=(N_TOK // TM_PROJ,),
        in_specs=[pl.BlockSpec((TM_PROJ, D), lambda i: (i, 0)),
                  pl.BlockSpec((TM_PROJ // GRID_W, D // 2), lambda i: (i, 0)),
                  pl.BlockSpec((GRID_W, D // 2), lambda i: (0, 0)),
                  vec, vec, vec, vec,
                  pl.BlockSpec((D, nw), lambda i: (0, 0)),
                  pl.BlockSpec((1, nw), lambda i: (0, 0))],
        out_specs=(pl.BlockSpec((CH, TM_PROJ // CH, S5_WIDTH), lambda i: (0, i, 0)),
                   pl.BlockSpec((TM_PROJ, FFT_WIDTH), lambda i: (i, 0)),
                   pl.BlockSpec((TM_PROJ, FFT_WIDTH), lambda i: (i, 0))),
        out_shape=(jax.ShapeDtypeStruct((CH, N_CHUNK, S5_WIDTH), BF16),
                   jax.ShapeDtypeStruct((N_TOK, FFT_WIDTH), BF16),
                   jax.ShapeDtypeStruct((N_TOK, FFT_WIDTH), BF16)),
        scratch_shapes=[pltpu.VMEM((NJ, TM_PROJ, LANES), F32)],
        compiler_params=_cparams(("parallel",)),
        name="proj",
    )(x, emb_r, emb_c, lg, lb, m1, s1, wcat, bcat)


def _ctx_proj_kernel(x_ref, lg_ref, lb_ref, m_ref, s_ref, w_ref, b_ref, p_ref, scr_ref):
    h = _layer_norm(x_ref[...], lg_ref[...], lb_ref[...])
    u = (h * m_ref[...] + s_ref[...]).astype(BF16)
    p = jnp.dot(u, w_ref[...], preferred_element_type=F32) + b_ref[...]
    _to_chunk_major(p, scr_ref, p_ref, N_CTX)


def _ctx_proj(ctx, lg, lb, m1, s1, w_s5, b_s5):
    return pl.pallas_call(
        _ctx_proj_kernel,
        out_shape=jax.ShapeDtypeStruct((CH, N_CHUNK_CTX, S5_WIDTH), BF16),
        scratch_shapes=[pltpu.VMEM((NJ, N_CTX, LANES), F32)],
        compiler_params=pltpu.CompilerParams(vmem_limit_bytes=VMEM_LIMIT),
        name="ctxproj",
    )(ctx, lg, lb, m1, s1, w_s5, b_s5)


def _s5_tables(lam_re, lam_im, log_dt, b_re, b_im, c_re, c_im, d_skip):
    dt = jnp.exp(log_dt)[..., None]
    zr = lam_re * dt
    zi = lam_im * dt

    def apow(m):
        m = jnp.asarray(m, F32)
        mag = jnp.exp(zr[..., None] * m)
        return mag * jnp.cos(zi[..., None] * m), mag * jnp.sin(zi[..., None] * m)

    a_re, a_im = apow(jnp.ones((1,), F32))
    a_re, a_im = a_re[..., 0], a_im[..., 0]
    den = lam_re * lam_re + lam_im * lam_im
    num_re = a_re - 1.0
    k_re = (num_re * lam_re + a_im * lam_im) / den
    k_im = (a_im * lam_re - num_re * lam_im) / den
    bb_re = k_re[..., None] * b_re - k_im[..., None] * b_im
    bb_im = k_re[..., None] * b_im + k_im[..., None] * b_re

    ks = jnp.arange(CH + 1, dtype=F32)
    pw_re, pw_im = apow(ks)
    kmag = jnp.exp(zr[:, :, None, :] * ks[None, None, :, None])
    pk_re = kmag * jnp.cos(zi[:, :, None, :] * ks[None, None, :, None])
    pk_im = kmag * jnp.sin(zi[:, :, None, :] * ks[None, None, :, None])
    bt_re, bt_im = jnp.swapaxes(b_re, 2, 3), jnp.swapaxes(b_im, 2, 3)
    bbt_re = k_re[:, :, None, :] * bt_re - k_im[:, :, None, :] * bt_im
    bbt_im = k_re[:, :, None, :] * bt_im + k_im[:, :, None, :] * bt_re

    ar, ai = pw_re[:, :, :, :CH, None], pw_im[:, :, :, :CH, None]
    cr = jnp.swapaxes(c_re, 2, 3)[:, :, :, None, :]
    ci = jnp.swapaxes(c_im, 2, 3)[:, :, :, None, :]
    ca = jnp.concatenate([cr * ar - ci * ai, -(cr * ai + ci * ar)], axis=2)
    ca = ca.reshape(2, S5_GROUPS, 2 * S5_STATE, CH * S5_GROUP)
    bbt = jnp.concatenate([bbt_re, bbt_im], axis=-1)
    taps = jnp.einsum('dghq,dgqn->dghn', bbt, ca, precision=HI)
    skip = (d_skip.reshape(S5_GROUPS, S5_GROUP, 1) * jnp.eye(S5_GROUP, dtype=F32)[None])
    taps = taps.at[0, :, :, :S5_GROUP].add(skip)
    b_c = jnp.transpose(taps.reshape(2, NJ, GPT, S5_GROUP, CH, S5_GROUP), (1, 4, 0, 2, 3, 5))
    b_c = b_c.reshape(NJ, 2 * CH, LANES, S5_GROUP)

    ef = (CH - 1) - jnp.arange(CH)
    eb = jnp.arange(CH)

    def q_part(d, e):
        pr = pk_re[d][:, e, None, :]
        pi = pk_im[d][:, e, None, :]
        br = bbt_re[d][:, None, :, :]
        bi = bbt_im[d][:, None, :, :]
        return pr * br - pi * bi, pr * bi + pi * br

    def q_rows(v):
        v = v.reshape(NJ, GPT, CH, S5_GROUP, S5_STATE)
        return jnp.transpose(v, (0, 2, 1, 3, 4)).reshape(NJ, CL, S5_STATE)
    a_q = jnp.stack([q_rows(v) for v in q_part(0, ef) + q_part(1, eb)], axis=0)

    of = jnp.arange(CH) + 1
    ob = CH - jnp.arange(CH)

    def p_part(d, e):
        pr = pw_re[d][..., e][:, :, :, None]
        pi = pw_im[d][..., e][:, :, :, None]
        return (ct_re[d] * pr - ct_im[d] * pi, -(ct_re[d] * pi + ct_im[d] * pr))
    ct_re = jnp.swapaxes(c_re, 2, 3)[:, :, :, None, :]
    ct_im = jnp.swapaxes(c_im, 2, 3)[:, :, :, None, :]
    a_p = jnp.stack([v.reshape(NJ, GPT * S5_STATE, CH * S5_GROUP)
                     for v in p_part(0, of) + p_part(1, ob)], axis=0)

    def lanes(v):
        return jnp.transpose(v.reshape(2, NJ, GPT * S5_STATE), (1, 0, 2))
    c_r, c_i = apow(jnp.full((1,), float(CH), F32))
    s_r, s_i = apow(jnp.full((1,), float(CH * SEG), F32))
    cr, ci, sr, si = (lanes(v[..., 0]) for v in (c_r, c_i, s_r, s_i))
    trans = jnp.stack([cr[:, 0], ci[:, 0], cr[:, 1], ci[:, 1],
                       sr[:, 0], si[:, 0], sr[:, 1], si[:, 1]], axis=1)

    cidx = jnp.arange(N_CHUNK_CTX, dtype=F32)
    wf_r, wf_i = apow(CH * (N_CHUNK_CTX - 1 - cidx))
    wb_r, wb_i = apow(CH * cidx)

    def ctx_lanes(v, d):
        return jnp.transpose(v[d].reshape(NJ, GPT * S5_STATE, N_CHUNK_CTX), (0, 2, 1))
    ctx_w = jnp.stack([ctx_lanes(wf_r, 0), ctx_lanes(wf_i, 0),
                       ctx_lanes(wb_r, 1), ctx_lanes(wb_i, 1)], axis=1)
    return b_c, a_q, a_p, trans, ctx_w


def _s5w_kernel(bc_ref, aq_ref, ap_ref, c16_ref, c64_ref, cm_ref, wm_ref, wq_ref, wp_ref):
    def expand(a, c, row_shift, col_shift):
        w = jnp.dot(a, c, preferred_element_type=F32)
        rg = (lax.broadcasted_iota(jnp.int32, (w.shape[0], 1), 0) >> row_shift) & (GPT - 1)
        cg = (lax.broadcasted_iota(jnp.int32, (1, w.shape[1]), 1) >> col_shift) & (GPT - 1)
        return jnp.where(rg == cg, w, 0.0)

    blk = [expand(bc_ref[0, kd], c16_ref[...], 4, 4) for kd in range(2 * CH)]
    for t in range(CH):
        for u in range(CH):
            b = blk[2 * (u - t)] if u > t else blk[2 * (t - u) + 1] if u < t else blk[0] + blk[1]
            wm_ref[0, t * LANES:(t + 1) * LANES, u * LANES:(u + 1) * LANES] = b.astype(BF16)
    half = GPT * S5_STATE
    for s in range(4):
        wq_ref[0, :, s * half:(s + 1) * half] = expand(aq_ref[s, 0], c64_ref[...], 4, 6).astype(BF16)
        wp_ref[0, s * half:(s + 1) * half, :] = expand(ap_ref[s, 0], cm_ref[...], 6, 4).astype(BF16)


def _s5_expand(b_c, a_q, a_p):
    rep = np.ones((1, GPT))
    c16 = jnp.asarray(np.kron(rep, np.eye(S5_GROUP)), F32).astype(BF16)
    c64 = jnp.asarray(np.kron(rep, np.eye(S5_STATE)), F32).astype(BF16)
    c_m = jnp.asarray(np.kron(np.eye(CH), np.kron(rep, np.eye(S5_GROUP))), F32).astype(BF16)
    b_c, a_q, a_p = b_c.astype(BF16), a_q.astype(BF16), a_p.astype(BF16)
    half = GPT * S5_STATE
    return pl.pallas_call(
        _s5w_kernel,
        grid=(NJ,),
        in_specs=[pl.BlockSpec((1, 2 * CH, LANES, S5_GROUP), lambda j: (j, 0, 0, 0)),
                  pl.BlockSpec((4, 1, CL, S5_STATE), lambda j: (0, j, 0, 0)),
                  pl.BlockSpec((4, 1, half, CH * S5_GROUP), lambda j: (0, j, 0, 0)),
                  pl.BlockSpec(c16.shape, lambda j: (0, 0)),
                  pl.BlockSpec(c64.shape, lambda j: (0, 0)),
                  pl.BlockSpec(c_m.shape, lambda j: (0, 0))],
        out_specs=(pl.BlockSpec((1, CL, CL), lambda j: (j, 0, 0)),
                   pl.BlockSpec((1, CL, SW), lambda j: (j, 0, 0)),
                   pl.BlockSpec((1, SW, CL), lambda j: (j, 0, 0))),
        out_shape=(jax.ShapeDtypeStruct((NJ, CL, CL), BF16),
                   jax.ShapeDtypeStruct((NJ, CL, SW), BF16),
                   jax.ShapeDtypeStruct((NJ, SW, CL), BF16)),
        compiler_params=_cparams(("parallel",)),
        name="s5w",
    )(b_c, a_q, a_p, c16, c64, c_m)


def _s5_kernel(p_ref, pc_ref, wm_ref, wq_ref, wp_ref, tr_ref, cw_ref, y_ref, v_ref):
    nq = NJ
    half = GPT * S5_STATE

    def chunk_rows(ref, r0, nrows):
        return jnp.concatenate([ref[t, pl.ds(r0, nrows), :] for t in range(CH)], axis=-1)

    def fill(k, c):
        r0 = pl.multiple_of(k * SEG, SEG)
        v = jnp.dot(chunk_rows(p_ref, r0, SEG), wq_ref[0], preferred_element_type=F32)
        for s in range(4 * nq):
            v_ref[s, pl.ds(k, SEG, stride=NSEG), :] = v[:, s * LANES:(s + 1) * LANES]
        return c
    lax.fori_loop(0, NSEG, fill, 0)

    vc = jnp.dot(chunk_rows(pc_ref, 0, N_CHUNK_CTX), wq_ref[0], preferred_element_type=F32)
    vfr, vfi, vbr, vbi = (vc[:, i * half:(i + 1) * half] for i in range(4))
    wfr, wfi, wbr, wbi = (cw_ref[0, i] for i in range(4))
    s0_fr = jnp.sum(wfr * vfr - wfi * vfi, axis=0, keepdims=True)
    s0_fi = jnp.sum(wfr * vfi + wfi * vfr, axis=0, keepdims=True)
    s0_br = jnp.sum(wbr * vbr - wbi * vbi, axis=0, keepdims=True)
    s0_bi = jnp.sum(wbr * vbi + wbi * vbr, axis=0, keepdims=True)

    tr = tr_ref[0]
    afr, afi, abr, abi = (jnp.broadcast_to(tr[i:i + 1], (NSEG, half)) for i in range(4))
    gfr, gfi, gbr, gbi = (tr[i:i + 1] for i in range(4, 8))

    def load_part(part, i):
        return jnp.concatenate(
            [v_ref[part * nq + q, pl.ds(pl.multiple_of(i * NSEG, NSEG), NSEG), :] for q in range(nq)],
            axis=-1)

    def store_part(part, i, val):
        for q in range(nq):
            v_ref[part * nq + q, pl.ds(pl.multiple_of(i * NSEG, NSEG), NSEG), :] = (
                val[:, q * LANES:(q + 1) * LANES])

    def step(i, carry, write):
        fr, fi, br, bi = carry
        ib = SEG - 1 - i
        ufr, ufi = load_part(0, i), load_part(1, i)
        ubr, ubi = load_part(2, ib), load_part(3, ib)
        if write:
            store_part(0, i, fr)
            store_part(1, i, fi)
            store_part(2, ib, br)
            store_part(3, ib, bi)
        return (afr * fr - afi * fi + ufr, afr * fi + afi * fr + ufi,
                abr * br - abi * bi + ubr, abr * bi + abi * br + ubi)

    zero = jnp.zeros((NSEG, half), F32)
    ffr, ffi, fbr, fbi = lax.fori_loop(0, SEG, functools.partial(step, write=False),
                                       (zero, zero, zero, zero), unroll=SCAN_UNROLL)

    rows_fr, rows_fi = [s0_fr], [s0_fi]
    for k in range(1, NSEG):
        pr, pi = rows_fr[-1], rows_fi[-1]
        rows_fr.append(gfr * pr - gfi * pi + ffr[k - 1:k])
        rows_fi.append(gfr * pi + gfi * pr + ffi[k - 1:k])
    rows_br, rows_bi = [s0_br], [s0_bi]
    for k in range(NSEG - 2, -1, -1):
        pr, pi = rows_br[0], rows_bi[0]
        rows_br.insert(0, gbr * pr - gbi * pi + fbr[k + 1:k + 2])
        rows_bi.insert(0, gbr * pi + gbi * pr + fbi[k + 1:k + 2])
    init = tuple(jnp.concatenate(r, axis=0) for r in (rows_fr, rows_fi, rows_br, rows_bi))

    lax.fori_loop(0, SEG, functools.partial(step, write=True), init, unroll=SCAN_UNROLL)

    def emit(k, c):
        r0 = pl.multiple_of(k * SEG, SEG)
        b = chunk_rows(p_ref, r0, SEG)
        sin = jnp.concatenate([v_ref[s, pl.ds(k, SEG, stride=NSEG), :] for s in range(4 * nq)], axis=-1)
        y = (jnp.dot(b, wm_ref[0], preferred_element_type=F32)
             + jnp.dot(sin.astype(BF16), wp_ref[0], preferred_element_type=F32))
        for t in range(CH):
            y_ref[t, pl.ds(r0, SEG), :] = y[:, t * LANES:(t + 1) * LANES].astype(y_ref.dtype)
        return c
    lax.fori_loop(0, NSEG, emit, 0)


def _s5(p_t, pc_t, w_m, w_q, w_p, trans, ctx_w):
    one = pl.Buffered(1)
    return pl.pallas_call(
        _s5_kernel,
        grid=(NJ,),
        in_specs=[pl.BlockSpec((CH, N_CHUNK, LANES), lambda j: (0, 0, j)),
                  pl.BlockSpec((CH, N_CHUNK_CTX, LANES), lambda j: (0, 0, j)),
                  pl.BlockSpec((1, CL, CL), lambda j: (j, 0, 0), pipeline_mode=one),
                  pl.BlockSpec((1, CL, SW), lambda j: (j, 0, 0), pipeline_mode=one),
                  pl.BlockSpec((1, SW, CL), lambda j: (j, 0, 0), pipeline_mode=one),
                  pl.BlockSpec((1, SUBLANES, GPT * S5_STATE), lambda j: (j, 0, 0)),
                  pl.BlockSpec((1, 4, N_CHUNK_CTX, GPT * S5_STATE), lambda j: (j, 0, 0, 0))],
        out_specs=pl.BlockSpec((CH, N_CHUNK, LANES), lambda j: (0, 0, j)),
        out_shape=jax.ShapeDtypeStruct((CH, N_CHUNK, S5_WIDTH), BF16),
        scratch_shapes=[pltpu.VMEM((4 * NJ, N_CHUNK, LANES), F32)],
        compiler_params=_cparams(("parallel",)),
        name="s5",
    )(p_t, pc_t, w_m, w_q, w_p, trans, ctx_w)


def _dft_tables():
    n = np.arange(FN)
    ang = 2.0 * np.pi * np.outer(n, n) / FN
    c, s = np.cos(ang), np.sin(ang)
    st1 = np.block([[c, s], [-s, c]])
    tw = 2.0 * np.pi * np.outer(n, n) / (FN * FN)
    wr, wi = np.cos(tw), -np.sin(tw)
    fr = c[None] * wr[:, None, :] + s[None] * wi[:, None, :]
    fi = c[None] * wi[:, None, :] - s[None] * wr[:, None, :]
    st2 = np.concatenate([fr, -fi], axis=-1)
    scale = 1.0 / math.sqrt(N_TOK * FFT_DIM)
    blk_c = np.kron(np.eye(FFT_GROUPS), c) * scale
    blk_s = np.kron(np.eye(FFT_GROUPS), s) * scale
    fc = np.concatenate([blk_c, -blk_s], axis=1)
    return (jnp.asarray(st1, F32).astype(BF16), jnp.asarray(st2, F32).astype(BF16), jnp.asarray(fc, F32))


FSL = FFT_WIDTH // LANES


FBH = FB // SUBLANES


def _block_to_slabs(blk, slab_ref, first, per_half):
    for bh in range(FBH):
        val = blk[:, bh * SUBLANES:(bh + 1) * SUBLANES, :].reshape(FN * SUBLANES, FFT_WIDTH)
        for s in range(FSL):
            slab_ref[bh * per_half + first + s] = val[:, s * LANES:(s + 1) * LANES]


def _slab_rows(b, first, per_half):
    return (b // SUBLANES) * per_half + first, pl.ds(b % SUBLANES, FN, stride=SUBLANES)


def _slabs_to_block(slab_ref, first, per_half):
    halves = []
    for bh in range(FBH):
        val = jnp.concatenate([slab_ref[bh * per_half + first + s] for s in range(FSL)], axis=-1)
        halves.append(val.reshape(FN, SUBLANES, FFT_WIDTH))
    return jnp.concatenate(halves, axis=1)


def _fft1_kernel(xr_ref, xi_ref, f_ref, yr_ref, yi_ref, in_ref, out_ref):
    _block_to_slabs(xr_ref[...].astype(F32), in_ref, 0, 2 * FSL)
    _block_to_slabs(xi_ref[...].astype(F32), in_ref, FSL, 2 * FSL)
    for b in range(FB):
        def part(first):
            base, rows = _slab_rows(b, first, 2 * FSL)
            return jnp.concatenate([in_ref[base + s, rows, :] for s in range(FSL)], axis=-1)
        xs = jnp.concatenate([part(0), part(FSL)], axis=0).astype(BF16)
        y = jnp.dot(f_ref[...], xs, preferred_element_type=F32)
        base, rows = _slab_rows(b, 0, 2 * FSL)
        for s in range(FSL):
            out_ref[base + s, rows, :] = y[:FN, s * LANES:(s + 1) * LANES]
            out_ref[base + FSL + s, rows, :] = y[FN:, s * LANES:(s + 1) * LANES]
    yr_ref[...] = _slabs_to_block(out_ref, 0, 2 * FSL).astype(BF16)
    yi_ref[...] = _slabs_to_block(out_ref, FSL, 2 * FSL).astype(BF16)


def _fft1(xr, xi, st1):
    spec = pl.BlockSpec((FN, FB, FFT_WIDTH), lambda i: (0, i, 0))
    slabs = pltpu.VMEM((FBH * 2 * FSL, FN * SUBLANES, LANES), F32)
    return pl.pallas_call(
        _fft1_kernel,
        grid=(FN // FB,),
        in_specs=[spec, spec, pl.BlockSpec((2 * FN, 2 * FN), lambda i: (0, 0))],
        out_specs=(spec, spec),
        out_shape=(jax.ShapeDtypeStruct((FN, FN, FFT_WIDTH), BF16),) * 2,
        scratch_shapes=[slabs, slabs],
        compiler_params=_cparams(("parallel",)),
        name="fft1",
    )(xr.reshape(FN, FN, FFT_WIDTH), xi.reshape(FN, FN, FFT_WIDTH), st1)


def _fft2_kernel(yr_ref, yi_ref, f_ref, z_ref, out_ref):
    for b in range(FB):
        ys = jnp.concatenate([yr_ref[b * FN:(b + 1) * FN, :], yi_ref[b * FN:(b + 1) * FN, :]], axis=0)
        z = jnp.dot(f_ref[b], ys, preferred_element_type=F32)
        base, rows = _slab_rows(b, 0, FSL)
        for s in range(FSL):
            out_ref[base + s, rows, :] = z[:, s * LANES:(s + 1) * LANES]
    z_ref[...] = _slabs_to_block(out_ref, 0, FSL).astype(BF16)


def _fft2(yr, yi, st2):
    rows = pl.BlockSpec((FB * FN, FFT_WIDTH), lambda i: (i, 0))
    z = pl.pallas_call(
        _fft2_kernel,
        grid=(FN // FB,),
        in_specs=[rows, rows, pl.BlockSpec((FB, FN, 2 * FN), lambda i: (i, 0, 0))],
        out_specs=pl.BlockSpec((FN, FB, FFT_WIDTH), lambda i: (0, i, 0)),
        out_shape=jax.ShapeDtypeStruct((FN, FN, FFT_WIDTH), BF16),
        scratch_shapes=[pltpu.VMEM((FBH * FSL, FN * SUBLANES, LANES), F32)],
        compiler_params=_cparams(("parallel",)),
        name="fft2",
    )(yr.reshape(N_TOK, FFT_WIDTH), yi.reshape(N_TOK, FFT_WIDTH), st2)
    return z.reshape(N_TOK, FFT_WIDTH)


def _gelu_tanh(x):
    return 0.5 * x * (1.0 + jnp.tanh(math.sqrt(2.0 / math.pi) * (x + 0.044715 * (x * x * x))))


def _mix_kernel(x_ref, er_ref, ec_ref, lg_ref, lb_ref, m1_ref, s1_ref, wg_ref, bg_ref,
                yt_ref, zr_ref, wglu_ref, bglu_ref, wbs_ref, wbf_ref, bbf_ref, wo_ref, bo_ref,
                g1_ref, l1g_ref, l1b_ref, m2_ref, s2_ref, wr_ref, br_ref, tri_ref, etri_ref,
                h1_ref, u2_ref, pos_ref, gate_ref, cnt_ref, scr_ref):
    pos = _pos_code(er_ref, ec_ref, TM)

    def front(r0, nr):
        rows = slice(r0, r0 + nr)
        h = _layer_norm(x_ref[rows, :] + pos[rows, :], lg_ref[...], lb_ref[...])
        u = (h * m1_ref[...] + s1_ref[...]).astype(BF16)

        c0, nc = r0 // CH, nr // CH
        for t in range(CH):
            for j in range(NJ):
                scr_ref[j, pl.ds(r0 + t, nc, stride=CH), :] = (
                    yt_ref[t, c0:c0 + nc, j * LANES:(j + 1) * LANES].astype(F32))
        ys = jnp.concatenate([scr_ref[j, rows, :] for j in range(NJ)], axis=-1)
        z = jnp.dot(_gelu_tanh(ys).astype(BF16), wglu_ref[...], preferred_element_type=F32) + bglu_ref[...]
        glu = (z[:, :S5_WIDTH] * _sigmoid(z[:, S5_WIDTH:])).astype(BF16)
        gates = _sigmoid(jnp.dot(u, wg_ref[...], preferred_element_type=F32) + bg_ref[...])
        y_s5 = jnp.dot(glu, wbs_ref[...], preferred_element_type=F32)
        y_fft = jnp.dot(zr_ref[rows, :], wbf_ref[...], preferred_element_type=F32) + bbf_ref[...]
        mixed = (gates[:, :D] * y_s5 + gates[:, D:] * y_fft).astype(BF16)
        y = jnp.dot(mixed, wo_ref[...], preferred_element_type=F32) + bo_ref[...]
        h1 = _layer_norm(ALPHA * h + g1_ref[...] * y, l1g_ref[...], l1b_ref[...])
        h1_ref[rows, :] = h1
        u2 = h1 * m2_ref[...] + s2_ref[...]
        u2_ref[rows, :] = u2.astype(BF16)
        u_hi = u2.astype(BF16)
        u_lo = (u2 - u_hi.astype(F32)).astype(BF16)

        def nt(a, b):
            return lax.dot_general(a, b, (((1,), (1,)), ((), ())), preferred_element_type=F32)
        return nt(wr_ref[0], u_hi) + nt(wr_ref[0], u_lo) + nt(wr_ref[1], u_hi)

    nr = TM // MIX_GROUPS
    logits = jnp.concatenate([front(g * nr, nr) for g in range(MIX_GROUPS)], axis=-1) + br_ref[:, 0:1]
    eidx = lax.broadcasted_iota(jnp.int32, (N_EXPERTS, TM), 0)
    vals, hots = [], []
    cur = logits
    for _k in range(TOP_K):
        m = jnp.max(cur, axis=0, keepdims=True)
        sel = jnp.min(jnp.where(cur == m, eidx, N_EXPERTS), axis=0, keepdims=True)
        hot = eidx == sel
        cur = jnp.where(hot, -jnp.inf, cur)
        vals.append(m)
        hots.append(hot)
    exps = [jnp.exp(v - vals[0]) for v in vals]
    den = exps[0] + exps[1] + exps[2] + exps[3]
    gate4 = jnp.concatenate([e / den for e in exps], axis=0)

    hot_sum = (hots[0] | hots[1] | hots[2] | hots[3]).astype(F32)
    before = jnp.dot(hot_sum.astype(BF16), tri_ref[...], preferred_element_type=F32)
    cnt = jnp.broadcast_to(jnp.sum(hot_sum, axis=1, keepdims=True), (N_EXPERTS, LANES))
    cnt8 = jnp.floor((cnt + (SEG_ALIGN - 1)) * (1.0 / SEG_ALIGN)) * SEG_ALIGN
    seg0 = jnp.dot(etri_ref[...], cnt8.astype(BF16), preferred_element_type=F32)
    tot = seg0[:, 0:1] + before
    pos4 = jnp.concatenate(
        [jnp.sum(jnp.where(hk, tot, 0.0), axis=0, keepdims=True) for hk in hots], axis=0)
    pos_ref[...] = pos4.astype(jnp.int32)
    cnt_ref[0] = cnt

    gpad = jnp.concatenate([gate4, pos4, jnp.zeros((LANES - 2 * TOP_K, TM), F32)], axis=0)
    gate_ref[...] = gpad.T


def _mix(x, emb_r, emb_c, lg, lb, m1, s1, wg, bg, y_t, zr, wglu, bglu, wbs, wbf, bbf, wo, bo,
         g1, l1g, l1b, m2, s2, wr_t, br, tri, etri):
    vec = pl.BlockSpec((1, D), lambda i: (0, 0))

    def full(a):
        return pl.BlockSpec(a.shape, lambda i: (0,) * a.ndim)
    return pl.pallas_call(
        _mix_kernel,
        grid=(N_TOK // TM,),
        in_specs=[pl.BlockSpec((TM, D), lambda i: (i, 0)),
                  pl.BlockSpec((TM // GRID_W, D // 2), lambda i: (i, 0)),
                  pl.BlockSpec((GRID_W, D // 2), lambda i: (0, 0)),
                  vec, vec, vec, vec, full(wg), full(bg),
                  pl.BlockSpec((CH, TM // CH, S5_WIDTH), lambda i: (0, i, 0)),
                  pl.BlockSpec((TM, FFT_WIDTH), lambda i: (i, 0)),
                  full(wglu), full(bglu), full(wbs), full(wbf), full(bbf), full(wo), full(bo),
                  vec, vec, vec, vec, vec, full(wr_t), full(br), full(tri), full(etri)],
        out_specs=(pl.BlockSpec((TM, D), lambda i: (i, 0)),
                   pl.BlockSpec((TM, D), lambda i: (i, 0)),
                   pl.BlockSpec((TOP_K, TM), lambda i: (0, i)),
                   pl.BlockSpec((TM, LANES), lambda i: (i, 0)),
                   pl.BlockSpec((1, N_EXPERTS, LANES), lambda i: (i, 0, 0))),
        out_shape=(jax.ShapeDtypeStruct((N_TOK, D), F32),
                   jax.ShapeDtypeStruct((N_TOK, D), BF16),
                   jax.ShapeDtypeStruct((TOP_K, N_TOK), jnp.int32),
                   jax.ShapeDtypeStruct((N_TOK, LANES), F32),
                   jax.ShapeDtypeStruct((N_TILES, N_EXPERTS, LANES), F32)),
        scratch_shapes=[pltpu.VMEM((NJ, TM, LANES), F32)],
        compiler_params=_cparams(("parallel",)),
        name="mix",
    )(x, emb_r, emb_c, lg, lb, m1, s1, wg, bg, y_t, zr, wglu, bglu, wbs, wbf, bbf, wo, bo,
      g1, l1g, l1b, m2, s2, wr_t, br, tri, etri)


def _on_parity(i, fn):
    @pl.when(i % 2 == 0)
    def _():
        fn(0)

    @pl.when(i % 2 == 1)
    def _():
        fn(1)


def _dispatch_kernel(pend_ref, fill_ref, dprev_ref, dest_ref, pos_ref, u_ref, buf_ref,
                     sorted_ref, zero_ref, zsem, sems):
    i = pl.program_id(0)

    @pl.when(i == 0)
    def _():
        zero_ref[...] = jnp.zeros_like(zero_ref)
        units = BM // ZB

        def clear_copy(start):
            return pltpu.make_async_copy(
                zero_ref, buf_ref.at[pl.ds(pl.multiple_of(start, ZB), ZB)], zsem)

        def each(fn):
            def expert(e, c):
                for z in range(units):
                    start = pend_ref[e] - (z + 1) * ZB

                    @pl.when(start + ZB > fill_ref[e])
                    def _():
                        fn(clear_copy(start))
                return c
            lax.fori_loop(0, N_EXPERTS, expert, 0)

            def tail(b, c):
                fn(clear_copy(b * ZB))
                return c
            lax.fori_loop(pend_ref[N_EXPERTS - 1] // ZB, N_BLOCKS_ALL * units, tail, 0)
        each(lambda cp: cp.start())
        each(lambda cp: cp.wait())

        sorted_ref[...] = jnp.zeros_like(sorted_ref)

    def chunk_copy(slot, table_ref, j):
        dst = pl.multiple_of(table_ref[0, 0, j], SEG_ALIGN)
        return pltpu.make_async_copy(sorted_ref.at[slot, pl.ds(j * SEG_ALIGN, SEG_ALIGN)],
                                     buf_ref.at[pl.ds(dst, SEG_ALIGN)], sems.at[slot])

    def drain(slot):
        pltpu.make_async_copy(sorted_ref.at[slot], buf_ref.at[pl.ds(0, CAP)], sems.at[slot]).wait()

    def run(slot):
        pos = pos_ref[...]
        u = u_ref[...]
        n_rb = CAP // CAP_BLOCK
        per_rb = NCHK // (n_rb // 2)
        for rb in range(n_rb):
            for j in range(rb * per_rb, min((rb + 1) * per_rb, NCHK)):
                chunk_copy(1 - slot, dprev_ref, j).start()
            rows = lax.broadcasted_iota(jnp.int32, (CAP_BLOCK, TM), 0) + rb * CAP_BLOCK
            hit = rows == pos[0:1]
            for k in range(1, TOP_K):
                hit = hit | (rows == pos[k:k + 1])
            onehot = jnp.where(hit, 1.0, 0.0).astype(BF16)
            sorted_ref[slot, rb * CAP_BLOCK:(rb + 1) * CAP_BLOCK, :] = jnp.dot(
                onehot, u, preferred_element_type=F32).astype(BF16)
        drain(1 - slot)

        @pl.when(i == N_TILES - 1)
        def _():
            def issue(j, c):
                chunk_copy(slot, dest_ref, j).start()
                return c
            lax.fori_loop(0, NCHK, issue, 0)
            drain(slot)
    _on_parity(i, run)


def _dispatch(pad_ends, padded, chunk_table, pos_t, u2):
    return pl.pallas_call(
        _dispatch_kernel,
        grid_spec=pltpu.PrefetchScalarGridSpec(
            num_scalar_prefetch=2,
            grid=(N_TILES,),
            in_specs=[pl.BlockSpec((1, 1, NCHK), lambda i, a, b: (i, 0, 0), memory_space=pltpu.SMEM),
                      pl.BlockSpec((1, 1, NCHK), lambda i, a, b: (i + 1, 0, 0), memory_space=pltpu.SMEM),
                      pl.BlockSpec((TOP_K, TM), lambda i, a, b: (0, i)),
                      pl.BlockSpec((TM, D), lambda i, a, b: (i, 0))],
            out_specs=pl.BlockSpec(memory_space=pl.ANY),
            scratch_shapes=[pltpu.VMEM((2, CAP, D), BF16),
                            pltpu.VMEM((ZB, D), BF16),
                            pltpu.SemaphoreType.DMA(()),
                            pltpu.SemaphoreType.DMA((2,))]),
        out_shape=jax.ShapeDtypeStruct((ROWS_ALL, D), BF16),
        compiler_params=_cparams(("arbitrary",)),
        name="dispatch",
    )(pad_ends, padded, chunk_table, chunk_table, pos_t, u2)


def _ffn_kernel(be_ref, nu_ref, run_ref, nxt_ref, valid_ref, x_ref, wu_hbm, bu_ref, wd_hbm, bd_ref,
                y_ref, wu_ref, wd_ref, wub_ref, wdb_ref, sems):
    i = pl.program_id(0)
    used = i < nu_ref[0]

    def weight_copies(e, slot):
        return (pltpu.make_async_copy(wu_hbm.at[e], wu_ref.at[slot], sems.at[slot]),
                pltpu.make_async_copy(wd_hbm.at[e], wd_ref.at[slot], sems.at[slot]))

    @pl.when(used)
    def _():
        run = run_ref[i]

        @pl.when(run >= 0)
        def _():
            def open_run(slot):
                @pl.when(run == 0)
                def _():
                    for cp in weight_copies(be_ref[i], slot):
                        cp.start()

                @pl.when(nxt_ref[i] >= 0)
                def _():
                    for cp in weight_copies(nxt_ref[i], 1 - slot):
                        cp.start()
                for cp in weight_copies(be_ref[i], slot):
                    cp.wait()
                wub_ref[...] = wu_ref[slot].astype(BF16)
                wdb_ref[...] = wd_ref[slot].astype(BF16)
            _on_parity(run, open_run)

        def expert_rows(r0, nr):
            rows = slice(r0, r0 + nr)
            e = be_ref[i]
            h = (jnp.dot(x_ref[rows, :], wub_ref[...], preferred_element_type=F32)
                 + bu_ref[pl.ds(e, 1), :])
            h_glu = jnp.minimum(h[:, :D], SWIGLU_LIMIT)
            h_lin = jnp.clip(h[:, D:], -SWIGLU_LIMIT, SWIGLU_LIMIT)
            act = (h_glu * _sigmoid(SWIGLU_ALPHA * h_glu) * (h_lin + 1.0)).astype(BF16)
            y_ref[rows, :] = (jnp.dot(act, wdb_ref[...], preferred_element_type=F32)
                              + bd_ref[pl.ds(e, 1), :]).astype(BF16)

        valid = valid_ref[i]

        @pl.when(valid == BM)
        def _():
            expert_rows(0, BM)

        @pl.when(valid < BM)
        def _():
            for h0 in range(0, BM, FFN_HALF):
                @pl.when(valid >= h0 + FFN_HALF)
                def _(h0=h0):
                    expert_rows(h0, FFN_HALF)

                @pl.when(valid < h0 + FFN_HALF)
                def _(h0=h0):
                    for r0 in range(h0, h0 + FFN_HALF, FFN_TAIL):
                        @pl.when(r0 < valid)
                        def _(r0=r0):
                            expert_rows(r0, FFN_TAIL)

                        @pl.when(r0 >= valid)
                        def _(r0=r0):
                            y_ref[r0:r0 + FFN_TAIL, :] = jnp.zeros((FFN_TAIL, D), BF16)

    @pl.when(jnp.logical_not(used))
    def _():
        y_ref[...] = jnp.zeros_like(y_ref)


def _ffn(block_expert, n_used, run_id, next_expert, valid, buf, w_up, b_up, w_down, b_down):
    def blk(i, be, nu, *_):
        return jnp.minimum(i, nu[0] - 1)
    return pl.pallas_call(
        _ffn_kernel,
        grid_spec=pltpu.PrefetchScalarGridSpec(
            num_scalar_prefetch=5,
            grid=(N_BLOCKS_ALL,),
            in_specs=[pl.BlockSpec((BM, D), lambda i, *s: (blk(i, *s), 0)),
                      pl.BlockSpec(memory_space=pl.ANY),
                      pl.BlockSpec((N_EXPERTS, 2 * D), lambda i, *s: (0, 0)),
                      pl.BlockSpec(memory_space=pl.ANY),
                      pl.BlockSpec((N_EXPERTS, D), lambda i, *s: (0, 0))],
            out_specs=pl.BlockSpec((BM, D), lambda i, *s: (i, 0)),
            scratch_shapes=[pltpu.VMEM((2, D, 2 * D), F32),
                            pltpu.VMEM((2, D, D), F32),
                            pltpu.VMEM((D, 2 * D), BF16),
                            pltpu.VMEM((D, D), BF16),
                            pltpu.SemaphoreType.DMA((2,))]),
        out_shape=jax.ShapeDtypeStruct((ROWS_ALL, D), BF16),
        compiler_params=_cparams(("arbitrary",)),
        name="ffn",
    )(block_expert, n_used, run_id, next_expert, valid, buf, w_up, b_up, w_down, b_down)


def _combine_kernel(dest_ref, dnext_ref, y_ref, h1_ref, gate_ref, g2_ref, lg_ref, lb_ref,
                    o_ref, sorted_ref, sems):
    i = pl.program_id(0)

    def chunk_copy(slot, table_ref, j):
        src = pl.multiple_of(table_ref[0, 0, j], SEG_ALIGN)
        return pltpu.make_async_copy(y_ref.at[pl.ds(src, SEG_ALIGN)],
                                     sorted_ref.at[slot, pl.ds(j * SEG_ALIGN, SEG_ALIGN)],
                                     sems.at[slot])

    def drain(slot):
        pltpu.make_async_copy(y_ref.at[pl.ds(0, CAP)], sorted_ref.at[slot], sems.at[slot]).wait()

    @pl.when(i == 0)
    def _():
        def issue(j, c):
            chunk_copy(0, dest_ref, j).start()
            return c
        lax.fori_loop(0, NCHK, issue, 0)

    def run(slot):
        drain(slot)

        gp = gate_ref[...]
        m = jnp.zeros((TM, D), F32)
        n_cb = CAP // CAP_BLOCK
        per_cb = NCHK // (n_cb // 2)
        for cb in range(n_cb):
            for j in range(cb * per_cb, min((cb + 1) * per_cb, NCHK)):
                chunk_copy(1 - slot, dnext_ref, j).start()
            cols = (lax.broadcasted_iota(jnp.int32, (TM, CAP_BLOCK), 1) + cb * CAP_BLOCK).astype(F32)
            g = jnp.where(cols == gp[:, TOP_K:TOP_K + 1], gp[:, 0:1], 0.0)
            for k in range(1, TOP_K):
                g = g + jnp.where(cols == gp[:, TOP_K + k:TOP_K + k + 1], gp[:, k:k + 1], 0.0)
            rows = sorted_ref[slot, cb * CAP_BLOCK:(cb + 1) * CAP_BLOCK, :]
            m = m + jnp.dot(g.astype(BF16), rows, preferred_element_type=F32)
        o_ref[...] = _layer_norm(ALPHA * h1_ref[...] + g2_ref[...] * m, lg_ref[...], lb_ref[...])

        @pl.when(i == N_TILES - 1)
        def _():
            drain(1 - slot)
    _on_parity(i, run)


def _combine(chunk_table, y_buf, h1, gate_tok, g2, lg, lb):
    vec = pl.BlockSpec((1, D), lambda i: (0, 0))
    return pl.pallas_call(
        _combine_kernel,
        grid_spec=pltpu.PrefetchScalarGridSpec(
            num_scalar_prefetch=0,
            grid=(N_TILES,),
            in_specs=[pl.BlockSpec((1, 1, NCHK), lambda i: (i + 1, 0, 0), memory_space=pltpu.SMEM),
                      pl.BlockSpec((1, 1, NCHK), lambda i: (i + 2, 0, 0), memory_space=pltpu.SMEM),
                      pl.BlockSpec(memory_space=pl.ANY),
                      pl.BlockSpec((TM, D), lambda i: (i, 0)),
                      pl.BlockSpec((TM, LANES), lambda i: (i, 0)),
                      vec, vec, vec],
            out_specs=pl.BlockSpec((TM, D), lambda i: (i, 0)),
            scratch_shapes=[pltpu.VMEM((2, CAP, D), BF16),
                            pltpu.SemaphoreType.DMA((2,))]),
        out_shape=jax.ShapeDtypeStruct((N_TOK, D), F32),
        compiler_params=_cparams(("arbitrary",)),
        name="combine",
    )(chunk_table, chunk_table, y_buf, h1, gate_tok, g2, lg, lb)


def _sincos_tables():
    q = D // 4
    omega = 1.0 / (10000.0 ** (jnp.arange(q, dtype=F32) / q))

    def emb(n):
        ang = jnp.arange(n, dtype=F32)[:, None] * omega[None, :]
        return jnp.concatenate([jnp.sin(ang), jnp.cos(ang)], axis=-1)
    return emb(N_TOK // GRID_W), emb(GRID_W)


def _tile_slots(a, tile):
    return jnp.transpose(a.reshape(TOP_K, N_TOK // tile, tile), (1, 0, 2)).reshape(N_TOK // tile, 1, TOP_K * tile)


def kernel(x, c, ctx, c_ctx, ln_in_g, ln_in_b, w_ada, b_ada, w_in, b_in, s5_lambda_re, s5_lambda_im, s5_log_dt, s5_b_re, s5_b_im, s5_c_re, s5_c_im, s5_d, w_glu, b_glu, w_br_s5, w_br_fft, b_br_fft, w_out, b_out, ln1_g, ln1_b, w_router, b_router, w_up, b_up, w_down, b_down, ln2_g, ln2_b):
    assert x.shape == (1, N_TOK, D) and ctx.shape == (1, N_CTX, D) and w_ada.shape[0] == 1
    row = lambda v: v.reshape(1, -1).astype(F32)

    cc = jnp.concatenate([c.reshape(1, D), c_ctx.reshape(1, D), jnp.zeros((SUBLANES - 2, D), F32)], axis=0)
    ada = _ada(cc, w_ada[0], row(b_ada[0]))
    sh1, sc1, g1, sh2, sc2, g2 = (ada[0:1, k * D:(k + 1) * D] for k in range(6))
    sh1c, sc1c = ada[1:2, 0:D], ada[1:2, D:2 * D]

    emb_r, emb_c = _sincos_tables()
    st1, st2, fc = _dft_tables()
    lg, lb = row(ln_in_g), row(ln_in_b)

    w_s5 = w_in[0][:, :S5_WIDTH]
    w_fft = w_in[0][:, S5_WIDTH:S5_WIDTH + FFT_WIDTH]
    w_g = w_in[0][:, S5_WIDTH + FFT_WIDTH:]
    b_s5 = row(b_in[0][:S5_WIDTH])
    b_fft8 = jnp.concatenate([row(b_in[0][S5_WIDTH:S5_WIDTH + FFT_WIDTH]),
                              jnp.zeros((SUBLANES - 1, FFT_WIDTH), F32)], axis=0)
    b_g = row(b_in[0][S5_WIDTH + FFT_WIDTH:])
    w_fc, b_fc = _fft_weights(w_fft, b_fft8, fc)
    wcat = jnp.concatenate([w_s5, w_fc], axis=1).astype(BF16)
    bcat = jnp.concatenate([b_s5, b_fc[0:1]], axis=1)

    x2 = x[0]
    p_t, xr, xi = _proj(x2, emb_r, emb_c, lg, lb, 1.0 + sc1, sh1, wcat, bcat)
    pc_t = _ctx_proj(ctx[0], lg, lb, 1.0 + sc1c, sh1c, w_s5.astype(BF16), b_s5)

    b_c, a_q, a_p, trans, ctx_w = _s5_tables(
        s5_lambda_re[0], s5_lambda_im[0], s5_log_dt[0], s5_b_re[0], s5_b_im[0],
        s5_c_re[0], s5_c_im[0], s5_d[0])
    w_m, w_q, w_p = _s5_expand(b_c, a_q, a_p)
    y_t = _s5(p_t, pc_t, w_m, w_q, w_p, trans, ctx_w)

    yr, yi = _fft1(xr, xi, st1)
    zr = _fft2(yr, yi, st2)

    tri = (jnp.arange(TM)[:, None] < jnp.arange(TM)[None, :]).astype(BF16)
    br = jnp.broadcast_to(b_router[0].reshape(N_EXPERTS, 1), (N_EXPERTS, LANES))
    etri = (jnp.arange(N_EXPERTS)[:, None] > jnp.arange(N_EXPERTS)[None, :]).astype(BF16)
    wr_t = jnp.transpose(w_router[0])
    wr_hi = wr_t.astype(BF16)
    wr_split = jnp.stack([wr_hi, (wr_t - wr_hi.astype(F32)).astype(BF16)], axis=0)
    h1, u2, pos_t, gate_tok, counts = _mix(
        x2, emb_r, emb_c, lg, lb, 1.0 + sc1, sh1, w_g.astype(BF16), b_g, y_t, zr,
        w_glu[0].astype(BF16), row(b_glu[0]), w_br_s5[0].astype(BF16), w_br_fft[0].astype(BF16),
        row(b_br_fft[0]), w_out[0].astype(BF16), row(b_out[0]), g1, row(ln1_g[0]), row(ln1_b[0]),
        1.0 + sc2, sh2, wr_split, br, tri, etri)

    cnt = counts[:, :, 0].astype(jnp.int32)
    seg = (cnt + SEG_ALIGN - 1) // SEG_ALIGN * SEG_ALIGN
    seg_end = jnp.cumsum(seg, axis=1)
    seg_start = seg_end - seg
    padded = (jnp.sum(seg, axis=0) + BM - 1) // BM * BM
    pad_ends = jnp.cumsum(padded)
    seg_dest = (pad_ends - padded)[None, :] + jnp.cumsum(seg, axis=0) - seg
    chunk_row = jnp.arange(NCHK, dtype=jnp.int32) * SEG_ALIGN
    chunk_exp = jnp.minimum(jnp.sum(chunk_row[None, :, None] >= seg_end[:, None, :], axis=-1),
                            N_EXPERTS - 1)
    own = chunk_exp[:, :, None] == jnp.arange(N_EXPERTS, dtype=jnp.int32)[None, None, :]
    chunk_dest = (jnp.sum(jnp.where(own, (seg_dest - seg_start)[:, None, :], 0), axis=-1)
                  + chunk_row[None, :]).astype(jnp.int32).reshape(N_TILES, 1, NCHK)
    nchk = (seg_end[:, -1] // SEG_ALIGN).astype(jnp.int32)
    block_start = jnp.arange(N_BLOCKS_ALL, dtype=jnp.int32) * BM
    block_expert = jnp.minimum(jnp.sum(block_start[:, None] >= pad_ends[None, :], axis=1),
                               N_EXPERTS - 1).astype(jnp.int32)
    n_used = (pad_ends[-1:] // BM).astype(jnp.int32)
    opens = (block_start < pad_ends[-1]) & (
        block_expert != jnp.concatenate([jnp.full((1,), -1, jnp.int32), block_expert[:-1]]))
    run_id = jnp.where(opens, jnp.cumsum(opens.astype(jnp.int32)) - 1, -1).astype(jnp.int32)
    experts = jnp.arange(N_EXPERTS, dtype=jnp.int32)
    later = (experts[None, :] > block_expert[:, None]) & (padded[None, :] > 0)
    next_expert = jnp.min(jnp.where(later, experts[None, :], N_EXPERTS), axis=1)
    next_expert = jnp.where(next_expert < N_EXPERTS, next_expert, -1).astype(jnp.int32)

    spare = (ROWS + chunk_row)[None, None, :]
    chunk_table = jnp.concatenate(
        [spare,
         jnp.where(chunk_row[None, None, :] < (nchk * SEG_ALIGN)[:, None, None], chunk_dest, spare),
         spare], axis=0).astype(jnp.int32)
    fill_ends = (pad_ends - padded + jnp.sum(seg, axis=0)).astype(jnp.int32)
    buf = _dispatch(pad_ends.astype(jnp.int32), fill_ends, chunk_table, pos_t, u2)
    mine = block_expert[:, None] == experts[None, :]
    filled = jnp.sum(jnp.where(mine, fill_ends[None, :], 0), axis=1)
    valid = jnp.clip(filled - block_start, 0, BM).astype(jnp.int32)
    y_buf = _ffn(block_expert, n_used, run_id, next_expert, valid, buf, w_up[0],
                 b_up[0], w_down[0], b_down[0])
    out = _combine(chunk_table, y_buf, h1, gate_tok, g2, row(ln2_g[0]), row(ln2_b[0]))
    return out.reshape(1, N_TOK, D)
```

```python
import functools
import math

import jax
import jax.numpy as jnp
import numpy as np
from jax import lax
from jax.experimental import pallas as pl
from jax.experimental.pallas import tpu as pltpu

F32 = jnp.float32
BF16 = jnp.bfloat16
HI = lax.Precision.HIGHEST

D = 1024
N_TOK = 16384
N_CTX = 256
GRID_W = 64
S5_GROUP = 16
S5_GROUPS = 32
S5_STATE = 64
S5_WIDTH = 512
FFT_GROUPS = 4
FFT_DIM = 128
FFT_WIDTH = 512
N_EXPERTS = 32
TOP_K = 4
LN_EPS = 1e-5
ALPHA = 2.0 ** 0.25
SWIGLU_ALPHA = 1.702
SWIGLU_LIMIT = 7.0

LANES = 128
SUBLANES = 8
VMEM_LIMIT = 56 * 1024 * 1024

CH = 8
N_CHUNK = N_TOK // CH
N_CHUNK_CTX = N_CTX // CH
NSEG = SUBLANES
SEG = N_CHUNK // NSEG
SCAN_UNROLL = 4
GPT = LANES // S5_GROUP
NJ = S5_WIDTH // LANES
CL = CH * LANES
SW = 4 * GPT * S5_STATE

FN = 128
FB = 16

TM = 512
TM_PROJ = 1024
N_TILES = N_TOK // TM
MIX_GROUPS = 1
BM = 1024
ZB = 512
FFN_HALF = 512
FFN_TAIL = 128
N_SLOTS = N_TOK * TOP_K
SEG_ALIGN = 2 * SUBLANES
CAP_BLOCK = 256
CAP = -(-(TOP_K * TM + N_EXPERTS * (SEG_ALIGN - 1)) // CAP_BLOCK) * CAP_BLOCK
NCHK = CAP // SEG_ALIGN
N_BLOCKS = -(-(N_SLOTS + N_TILES * N_EXPERTS * (SEG_ALIGN - 1)) // BM) + N_EXPERTS
ROWS = N_BLOCKS * BM
N_BLOCKS_ALL = N_BLOCKS + -(-CAP // BM)
ROWS_ALL = N_BLOCKS_ALL * BM


def _cparams(sem):
    return pltpu.CompilerParams(dimension_semantics=sem, vmem_limit_bytes=VMEM_LIMIT)


def _layer_norm(x, g, b):
    mu = jnp.mean(x, axis=-1, keepdims=True)
    xc = x - mu
    var = jnp.mean(xc * xc, axis=-1, keepdims=True)
    return xc * lax.rsqrt(var + LN_EPS) * g + b


def _sigmoid(x):
    return 0.5 * jnp.tanh(0.5 * x) + 0.5


def _ada_kernel(c_ref, w_ref, b_ref, o_ref):
    c = c_ref[...]
    s = c * _sigmoid(c)
    o_ref[...] = jnp.dot(s, w_ref[...], preferred_element_type=F32, precision=HI) + b_ref[...]


def _ada(cc, w_ada, b_ada):
    nb = 4
    wb = 6 * D // nb
    return pl.pallas_call(
        _ada_kernel,
        grid=(nb,),
        in_specs=[pl.BlockSpec((SUBLANES, D), lambda i: (0, 0)),
                  pl.BlockSpec((D, wb), lambda i: (0, i)),
                  pl.BlockSpec((1, wb), lambda i: (0, i))],
        out_specs=pl.BlockSpec((SUBLANES, wb), lambda i: (0, i)),
        out_shape=jax.ShapeDtypeStruct((SUBLANES, 6 * D), F32),
        compiler_params=_cparams(("parallel",)),
        name="ada",
    )(cc, w_ada, b_ada)


def _fftw_kernel(w_ref, b_ref, f_ref, wo_ref, bo_ref):
    f = f_ref[...]
    wo_ref[...] = jnp.dot(w_ref[...], f, preferred_element_type=F32, precision=HI)
    bo_ref[...] = jnp.dot(b_ref[...], f, preferred_element_type=F32, precision=HI)


def _fft_weights(w_fft, b_fft8, fc):
    return pl.pallas_call(
        _fftw_kernel,
        out_shape=(jax.ShapeDtypeStruct((D, 2 * FFT_WIDTH), F32),
                   jax.ShapeDtypeStruct((SUBLANES, 2 * FFT_WIDTH), F32)),
        compiler_params=pltpu.CompilerParams(vmem_limit_bytes=VMEM_LIMIT),
        name="fftw",
    )(w_fft, b_fft8, fc)


def _pos_code(er_ref, ec_ref, tm):
    nr = tm // GRID_W
    er = er_ref[...]
    row = jnp.broadcast_to(er[:, None, :], (nr, GRID_W, D // 2)).reshape(tm, D // 2)
    col = jnp.concatenate([ec_ref[...]] * nr, axis=0)
    return jnp.concatenate([row, col], axis=-1)


def _to_chunk_major(val, scr_ref, out_ref, tm):
    for j in range(NJ):
        scr_ref[j] = val[:, j * LANES:(j + 1) * LANES]
    for t in range(CH):
        for j in range(NJ):
            piece = scr_ref[j, pl.ds(t, tm // CH, stride=CH), :]
            out_ref[t, :, j * LANES:(j + 1) * LANES] = piece.astype(out_ref.dtype)


def _proj_kernel(x_ref, er_ref, ec_ref, lg_ref, lb_ref, m_ref, s_ref, w_ref, b_ref,
                 p_ref, xr_ref, xi_ref, scr_ref):
    x = x_ref[...] + _pos_code(er_ref, ec_ref, TM_PROJ)
    h = _layer_norm(x, lg_ref[...], lb_ref[...])
    u = (h * m_ref[...] + s_ref[...]).astype(BF16)
    p = jnp.dot(u, w_ref[...], preferred_element_type=F32) + b_ref[...]
    _to_chunk_major(p[:, :S5_WIDTH], scr_ref, p_ref, TM_PROJ)
    xr_ref[...] = p[:, S5_WIDTH:S5_WIDTH + FFT_WIDTH].astype(BF16)
    xi_ref[...] = p[:, S5_WIDTH + FFT_WIDTH:].astype(BF16)


def _proj(x, emb_r, emb_c, lg, lb, m1, s1, wcat, bcat):
    nw = wcat.shape[1]
    vec = pl.BlockSpec((1, D), lambda i: (0, 0))
    return pl.pallas_call(
        _proj_kernel,
        grid=(N_TOK // TM_PROJ,),
        in_specs=[pl.BlockSpec((TM_PROJ, D), lambda i: (i, 0)),
                  pl.BlockSpec((TM_PROJ // GRID_W, D // 2), lambda i: (i, 0)),
                  pl.BlockSpec((GRID_W, D // 2), lambda i: (0, 0)),
                  vec, vec, vec, vec,
                  pl.BlockSpec((D, nw), lambda i: (0, 0)),
                  pl.BlockSpec((1, nw), lambda i: (0, 0))],
        out_specs=(pl.BlockSpec((CH, TM_PROJ // CH, S5_WIDTH), lambda i: (0, i, 0)),
                   pl.BlockSpec((TM_PROJ, FFT_WIDTH), lambda i: (i, 0)),
                   pl.BlockSpec((TM_PROJ, FFT_WIDTH), lambda i: (i, 0))),
        out_shape=(jax.ShapeDtypeStruct((CH, N_CHUNK, S5_WIDTH), BF16),
                   jax.ShapeDtypeStruct((N_TOK, FFT_WIDTH), BF16),
                   jax.ShapeDtypeStruct((N_TOK, FFT_WIDTH), BF16)),
        scratch_shapes=[pltpu.VMEM((NJ, TM_PROJ, LANES), F32)],
        compiler_params=_cparams(("parallel",)),
        name="proj",
    )(x, emb_r, emb_c, lg, lb, m1, s1, wcat, bcat)


def _ctx_proj_kernel(x_ref, lg_ref, lb_ref, m_ref, s_ref, w_ref, b_ref, p_ref, scr_ref):
    h = _layer_norm(x_ref[...], lg_ref[...], lb_ref[...])
    u = (h * m_ref[...] + s_ref[...]).astype(BF16)
    p = jnp.dot(u, w_ref[...], preferred_element_type=F32) + b_ref[...]
    _to_chunk_major(p, scr_ref, p_ref, N_CTX)


def _ctx_proj(ctx, lg, lb, m1, s1, w_s5, b_s5):
    return pl.pallas_call(
        _ctx_proj_kernel,
        out_shape=jax.ShapeDtypeStruct((CH, N_CHUNK_CTX, S5_WIDTH), BF16),
        scratch_shapes=[pltpu.VMEM((NJ, N_CTX, LANES), F32)],
        compiler_params=pltpu.CompilerParams(vmem_limit_bytes=VMEM_LIMIT),
        name="ctxproj",
    )(ctx, lg, lb, m1, s1, w_s5, b_s5)


def _s5_tables(lam_re, lam_im, log_dt, b_re, b_im, c_re, c_im, d_skip):
    dt = jnp.exp(log_dt)[..., None]
    zr = lam_re * dt
    zi = lam_im * dt

    def apow(m):
        m = jnp.asarray(m, F32)
        mag = jnp.exp(zr[..., None] * m)
        return mag * jnp.cos(zi[..., None] * m), mag * jnp.sin(zi[..., None] * m)

    a_re, a_im = apow(jnp.ones((1,), F32))
    a_re, a_im = a_re[..., 0], a_im[..., 0]
    den = lam_re * lam_re + lam_im * lam_im
    num_re = a_re - 1.0
    k_re = (num_re * lam_re + a_im * lam_im) / den
    k_im = (a_im * lam_re - num_re * lam_im) / den
    bb_re = k_re[..., None] * b_re - k_im[..., None] * b_im
    bb_im = k_re[..., None] * b_im + k_im[..., None] * b_re

    ks = jnp.arange(CH + 1, dtype=F32)
    pw_re, pw_im = apow(ks)
    kmag = jnp.exp(zr[:, :, None, :] * ks[None, None, :, None])
    pk_re = kmag * jnp.cos(zi[:, :, None, :] * ks[None, None, :, None])
    pk_im = kmag * jnp.sin(zi[:, :, None, :] * ks[None, None, :, None])
    bt_re, bt_im = jnp.swapaxes(b_re, 2, 3), jnp.swapaxes(b_im, 2, 3)
    bbt_re = k_re[:, :, None, :] * bt_re - k_im[:, :, None, :] * bt_im
    bbt_im = k_re[:, :, None, :] * bt_im + k_im[:, :, None, :] * bt_re

    ar, ai = pw_re[:, :, :, :CH, None], pw_im[:, :, :, :CH, None]
    cr = jnp.swapaxes(c_re, 2, 3)[:, :, :, None, :]
    ci = jnp.swapaxes(c_im, 2, 3)[:, :, :, None, :]
    ca = jnp.concatenate([cr * ar - ci * ai, -(cr * ai + ci * ar)], axis=2)
    ca = ca.reshape(2, S5_GROUPS, 2 * S5_STATE, CH * S5_GROUP)
    bbt = jnp.concatenate([bbt_re, bbt_im], axis=-1)
    taps = jnp.einsum('dghq,dgqn->dghn', bbt, ca, precision=HI)
    skip = (d_skip.reshape(S5_GROUPS, S5_GROUP, 1) * jnp.eye(S5_GROUP, dtype=F32)[None])
    taps = taps.at[0, :, :, :S5_GROUP].add(skip)
    b_c = jnp.transpose(taps.reshape(2, NJ, GPT, S5_GROUP, CH, S5_GROUP), (1, 4, 0, 2, 3, 5))
    b_c = b_c.reshape(NJ, 2 * CH, LANES, S5_GROUP)

    ef = (CH - 1) - jnp.arange(CH)
    eb = jnp.arange(CH)

    def q_part(d, e):
        pr = pk_re[d][:, e, None, :]
        pi = pk_im[d][:, e, None, :]
        br = bbt_re[d][:, None, :, :]
        bi = bbt_im[d][:, None, :, :]
        return pr * br - pi * bi, pr * bi + pi * br

    def q_rows(v):
        v = v.reshape(NJ, GPT, CH, S5_GROUP, S5_STATE)
        return jnp.transpose(v, (0, 2, 1, 3, 4)).reshape(NJ, CL, S5_STATE)
    a_q = jnp.stack([q_rows(v) for v in q_part(0, ef) + q_part(1, eb)], axis=0)

    of = jnp.arange(CH) + 1
    ob = CH - jnp.arange(CH)

    def p_part(d, e):
        pr = pw_re[d][..., e][:, :, :, None]
        pi = pw_im[d][..., e][:, :, :, None]
        return (ct_re[d] * pr - ct_im[d] * pi, -(ct_re[d] * pi + ct_im[d] * pr))
    ct_re = jnp.swapaxes(c_re, 2, 3)[:, :, :, None, :]
    ct_im = jnp.swapaxes(c_im, 2, 3)[:, :, :, None, :]
    a_p = jnp.stack([v.reshape(NJ, GPT * S5_STATE, CH * S5_GROUP)
                     for v in p_part(0, of) + p_part(1, ob)], axis=0)

    def lanes(v):
        return jnp.transpose(v.reshape(2, NJ, GPT * S5_STATE), (1, 0, 2))
    c_r, c_i = apow(jnp.full((1,), float(CH), F32))
    s_r, s_i = apow(jnp.full((1,), float(CH * SEG), F32))
    cr, ci, sr, si = (lanes(v[..., 0]) for v in (c_r, c_i, s_r, s_i))
    trans = jnp.stack([cr[:, 0], ci[:, 0], cr[:, 1], ci[:, 1],
                       sr[:, 0], si[:, 0], sr[:, 1], si[:, 1]], axis=1)

    cidx = jnp.arange(N_CHUNK_CTX, dtype=F32)
    wf_r, wf_i = apow(CH * (N_CHUNK_CTX - 1 - cidx))
    wb_r, wb_i = apow(CH * cidx)

    def ctx_lanes(v, d):
        return jnp.transpose(v[d].reshape(NJ, GPT * S5_STATE, N_CHUNK_CTX), (0, 2, 1))
    ctx_w = jnp.stack([ctx_lanes(wf_r, 0), ctx_lanes(wf_i, 0),
                       ctx_lanes(wb_r, 1), ctx_lanes(wb_i, 1)], axis=1)
    return b_c, a_q, a_p, trans, ctx_w


def _s5w_kernel(bc_ref, aq_ref, ap_ref, c16_ref, c64_ref, cm_ref, wm_ref, wq_ref, wp_ref):
    def expand(a, c, row_shift, col_shift):
        w = jnp.dot(a, c, preferred_element_type=F32)
        rg = (lax.broadcasted_iota(jnp.int32, (w.shape[0], 1), 0) >> row_shift) & (GPT - 1)
        cg = (lax.broadcasted_iota(jnp.int32, (1, w.shape[1]), 1) >> col_shift) & (GPT - 1)
        return jnp.where(rg == cg, w, 0.0)

    blk = [expand(bc_ref[0, kd], c16_ref[...], 4, 4) for kd in range(2 * CH)]
    for t in range(CH):
        for u in range(CH):
            b = blk[2 * (u - t)] if u > t else blk[2 * (t - u) + 1] if u < t else blk[0] + blk[1]
            wm_ref[0, t * LANES:(t + 1) * LANES, u * LANES:(u + 1) * LANES] = b.astype(BF16)
    half = GPT * S5_STATE
    for s in range(4):
        wq_ref[0, :, s * half:(s + 1) * half] = expand(aq_ref[s, 0], c64_ref[...], 4, 6).astype(BF16)
        wp_ref[0, s * half:(s + 1) * half, :] = expand(ap_ref[s, 0], cm_ref[...], 6, 4).astype(BF16)


def _s5_expand(b_c, a_q, a_p):
    rep = np.ones((1, GPT))
    c16 = jnp.asarray(np.kron(rep, np.eye(S5_GROUP)), F32).astype(BF16)
    c64 = jnp.asarray(np.kron(rep, np.eye(S5_STATE)), F32).astype(BF16)
    c_m = jnp.asarray(np.kron(np.eye(CH), np.kron(rep, np.eye(S5_GROUP))), F32).astype(BF16)
    b_c, a_q, a_p = b_c.astype(BF16), a_q.astype(BF16), a_p.astype(BF16)
    half = GPT * S5_STATE
    return pl.pallas_call(
        _s5w_kernel,
        grid=(NJ,),
        in_specs=[pl.BlockSpec((1, 2 * CH, LANES, S5_GROUP), lambda j: (j, 0, 0, 0)),
                  pl.BlockSpec((4, 1, CL, S5_STATE), lambda j: (0, j, 0, 0)),
                  pl.BlockSpec((4, 1, half, CH * S5_GROUP), lambda j: (0, j, 0, 0)),
                  pl.BlockSpec(c16.shape, lambda j: (0, 0)),
                  pl.BlockSpec(c64.shape, lambda j: (0, 0)),
                  pl.BlockSpec(c_m.shape, lambda j: (0, 0))],
        out_specs=(pl.BlockSpec((1, CL, CL), lambda j: (j, 0, 0)),
                   pl.BlockSpec((1, CL, SW), lambda j: (j, 0, 0)),
                   pl.BlockSpec((1, SW, CL), lambda j: (j, 0, 0))),
        out_shape=(jax.ShapeDtypeStruct((NJ, CL, CL), BF16),
                   jax.ShapeDtypeStruct((NJ, CL, SW), BF16),
                   jax.ShapeDtypeStruct((NJ, SW, CL), BF16)),
        compiler_params=_cparams(("parallel",)),
        name="s5w",
    )(b_c, a_q, a_p, c16, c64, c_m)


def _s5_kernel(p_ref, pc_ref, wm_ref, wq_ref, wp_ref, tr_ref, cw_ref, y_ref, v_ref):
    nq = NJ
    half = GPT * S5_STATE

    def chunk_rows(ref, r0, nrows):
        return jnp.concatenate([ref[t, pl.ds(r0, nrows), :] for t in range(CH)], axis=-1)

    def fill(k, c):
        r0 = pl.multiple_of(k * SEG, SEG)
        v = jnp.dot(chunk_rows(p_ref, r0, SEG), wq_ref[0], preferred_element_type=F32)
        for s in range(4 * nq):
            v_ref[s, pl.ds(k, SEG, stride=NSEG), :] = v[:, s * LANES:(s + 1) * LANES]
        return c
    lax.fori_loop(0, NSEG, fill, 0)

    vc = jnp.dot(chunk_rows(pc_ref, 0, N_CHUNK_CTX), wq_ref[0], preferred_element_type=F32)
    vfr, vfi, vbr, vbi = (vc[:, i * half:(i + 1) * half] for i in range(4))
    wfr, wfi, wbr, wbi = (cw_ref[0, i] for i in range(4))
    s0_fr = jnp.sum(wfr * vfr - wfi * vfi, axis=0, keepdims=True)
    s0_fi = jnp.sum(wfr * vfi + wfi * vfr, axis=0, keepdims=True)
    s0_br = jnp.sum(wbr * vbr - wbi * vbi, axis=0, keepdims=True)
    s0_bi = jnp.sum(wbr * vbi + wbi * vbr, axis=0, keepdims=True)

    tr = tr_ref[0]
    afr, afi, abr, abi = (jnp.broadcast_to(tr[i:i + 1], (NSEG, half)) for i in range(4))
    gfr, gfi, gbr, gbi = (tr[i:i + 1] for i in range(4, 8))

    def load_part(part, i):
        return jnp.concatenate(
            [v_ref[part * nq + q, pl.ds(pl.multiple_of(i * NSEG, NSEG), NSEG), :] for q in range(nq)],
            axis=-1)

    def store_part(part, i, val):
        for q in range(nq):
            v_ref[part * nq + q, pl.ds(pl.multiple_of(i * NSEG, NSEG), NSEG), :] = (
                val[:, q * LANES:(q + 1) * LANES])

    def step(i, carry, write):
        fr, fi, br, bi = carry
        ib = SEG - 1 - i
        ufr, ufi = load_part(0, i), load_part(1, i)
        ubr, ubi = load_part(2, ib), load_part(3, ib)
        if write:
            store_part(0, i, fr)
            store_part(1, i, fi)
            store_part(2, ib, br)
            store_part(3, ib, bi)
        return (afr * fr - afi * fi + ufr, afr * fi + afi * fr + ufi,
                abr * br - abi * bi + ubr, abr * bi + abi * br + ubi)

    zero = jnp.zeros((NSEG, half), F32)
    ffr, ffi, fbr, fbi = lax.fori_loop(0, SEG, functools.partial(step, write=False),
                                       (zero, zero, zero, zero), unroll=SCAN_UNROLL)

    rows_fr, rows_fi = [s0_fr], [s0_fi]
    for k in range(1, NSEG):
        pr, pi = rows_fr[-1], rows_fi[-1]
        rows_fr.append(gfr * pr - gfi * pi + ffr[k - 1:k])
        rows_fi.append(gfr * pi + gfi * pr + ffi[k - 1:k])
    rows_br, rows_bi = [s0_br], [s0_bi]
    for k in range(NSEG - 2, -1, -1):
        pr, pi = rows_br[0], rows_bi[0]
        rows_br.insert(0, gbr * pr - gbi * pi + fbr[k + 1:k + 2])
        rows_bi.insert(0, gbr * pi + gbi * pr + fbi[k + 1:k + 2])
    init = tuple(jnp.concatenate(r, axis=0) for r in (rows_fr, rows_fi, rows_br, rows_bi))

    lax.fori_loop(0, SEG, functools.partial(step, write=True), init, unroll=SCAN_UNROLL)

    def emit(k, c):
        r0 = pl.multiple_of(k * SEG, SEG)
        b = chunk_rows(p_ref, r0, SEG)
        sin = jnp.concatenate([v_ref[s, pl.ds(k, SEG, stride=NSEG), :] for s in range(4 * nq)], axis=-1)
        y = (jnp.dot(b, wm_ref[0], preferred_element_type=F32)
             + jnp.dot(sin.astype(BF16), wp_ref[0], preferred_element_type=F32))
        for t in range(CH):
            y_ref[t, pl.ds(r0, SEG), :] = y[:, t * LANES:(t + 1) * LANES].astype(y_ref.dtype)
        return c
    lax.fori_loop(0, NSEG, emit, 0)


def _s5(p_t, pc_t, w_m, w_q, w_p, trans, ctx_w):
    one = pl.Buffered(1)
    return pl.pallas_call(
        _s5_kernel,
        grid=(NJ,),
        in_specs=[pl.BlockSpec((CH, N_CHUNK, LANES), lambda j: (0, 0, j)),
                  pl.BlockSpec((CH, N_CHUNK_CTX, LANES), lambda j: (0, 0, j)),
                  pl.BlockSpec((1, CL, CL), lambda j: (j, 0, 0), pipeline_mode=one),
                  pl.BlockSpec((1, CL, SW), lambda j: (j, 0, 0), pipeline_mode=one),
                  pl.BlockSpec((1, SW, CL), lambda j: (j, 0, 0), pipeline_mode=one),
                  pl.BlockSpec((1, SUBLANES, GPT * S5_STATE), lambda j: (j, 0, 0)),
                  pl.BlockSpec((1, 4, N_CHUNK_CTX, GPT * S5_STATE), lambda j: (j, 0, 0, 0))],
        out_specs=pl.BlockSpec((CH, N_CHUNK, LANES), lambda j: (0, 0, j)),
        out_shape=jax.ShapeDtypeStruct((CH, N_CHUNK, S5_WIDTH), BF16),
        scratch_shapes=[pltpu.VMEM((4 * NJ, N_CHUNK, LANES), F32)],
        compiler_params=_cparams(("parallel",)),
        name="s5",
    )(p_t, pc_t, w_m, w_q, w_p, trans, ctx_w)


def _dft_tables():
    n = np.arange(FN)
    ang = 2.0 * np.pi * np.outer(n, n) / FN
    c, s = np.cos(ang), np.sin(ang)
    st1 = np.block([[c, s], [-s, c]])
    tw = 2.0 * np.pi * np.outer(n, n) / (FN * FN)
    wr, wi = np.cos(tw), -np.sin(tw)
    fr = c[None] * wr[:, None, :] + s[None] * wi[:, None, :]
    fi = c[None] * wi[:, None, :] - s[None] * wr[:, None, :]
    st2 = np.concatenate([fr, -fi], axis=-1)
    scale = 1.0 / math.sqrt(N_TOK * FFT_DIM)
    blk_c = np.kron(np.eye(FFT_GROUPS), c) * scale
    blk_s = np.kron(np.eye(FFT_GROUPS), s) * scale
    fc = np.concatenate([blk_c, -blk_s], axis=1)
    return (jnp.asarray(st1, F32).astype(BF16), jnp.asarray(st2, F32).astype(BF16), jnp.asarray(fc, F32))


FSL = FFT_WIDTH // LANES


FBH = FB // SUBLANES


def _block_to_slabs(blk, slab_ref, first, per_half):
    for bh in range(FBH):
        val = blk[:, bh * SUBLANES:(bh + 1) * SUBLANES, :].reshape(FN * SUBLANES, FFT_WIDTH)
        for s in range(FSL):
            slab_ref[bh * per_half + first + s] = val[:, s * LANES:(s + 1) * LANES]


def _slab_rows(b, first, per_half):
    return (b // SUBLANES) * per_half + first, pl.ds(b % SUBLANES, FN, stride=SUBLANES)


def _slabs_to_block(slab_ref, first, per_half):
    halves = []
    for bh in range(FBH):
        val = jnp.concatenate([slab_ref[bh * per_half + first + s] for s in range(FSL)], axis=-1)
        halves.append(val.reshape(FN, SUBLANES, FFT_WIDTH))
    return jnp.concatenate(halves, axis=1)


def _fft1_kernel(xr_ref, xi_ref, f_ref, yr_ref, yi_ref, in_ref, out_ref):
    _block_to_slabs(xr_ref[...].astype(F32), in_ref, 0, 2 * FSL)
    _block_to_slabs(xi_ref[...].astype(F32), in_ref, FSL, 2 * FSL)
    for b in range(FB):
        def part(first):
            base, rows = _slab_rows(b, first, 2 * FSL)
            return jnp.concatenate([in_ref[base + s, rows, :] for s in range(FSL)], axis=-1)
        xs = jnp.concatenate([part(0), part(FSL)], axis=0).astype(BF16)
        y = jnp.dot(f_ref[...], xs, preferred_element_type=F32)
        base, rows = _slab_rows(b, 0, 2 * FSL)
        for s in range(FSL):
            out_ref[base + s, rows, :] = y[:FN, s * LANES:(s + 1) * LANES]
            out_ref[base + FSL + s, rows, :] = y[FN:, s * LANES:(s + 1) * LANES]
    yr_ref[...] = _slabs_to_block(out_ref, 0, 2 * FSL).astype(BF16)
    yi_ref[...] = _slabs_to_block(out_ref, FSL, 2 * FSL).astype(BF16)


def _fft1(xr, xi, st1):
    spec = pl.BlockSpec((FN, FB, FFT_WIDTH), lambda i: (0, i, 0))
    slabs = pltpu.VMEM((FBH * 2 * FSL, FN * SUBLANES, LANES), F32)
    return pl.pallas_call(
        _fft1_kernel,
        grid=(FN // FB,),
        in_specs=[spec, spec, pl.BlockSpec((2 * FN, 2 * FN), lambda i: (0, 0))],
        out_specs=(spec, spec),
        out_shape=(jax.ShapeDtypeStruct((FN, FN, FFT_WIDTH), BF16),) * 2,
        scratch_shapes=[slabs, slabs],
        compiler_params=_cparams(("parallel",)),
        name="fft1",
    )(xr.reshape(FN, FN, FFT_WIDTH), xi.reshape(FN, FN, FFT_WIDTH), st1)


def _fft2_kernel(yr_ref, yi_ref, f_ref, z_ref, out_ref):
    for b in range(FB):
        ys = jnp.concatenate([yr_ref[b * FN:(b + 1) * FN, :], yi_ref[b * FN:(b + 1) * FN, :]], axis=0)
        z = jnp.dot(f_ref[b], ys, preferred_element_type=F32)
        base, rows = _slab_rows(b, 0, FSL)
        for s in range(FSL):
            out_ref[base + s, rows, :] = z[:, s * LANES:(s + 1) * LANES]
    z_ref[...] = _slabs_to_block(out_ref, 0, FSL).astype(BF16)


def _fft2(yr, yi, st2):
    rows = pl.BlockSpec((FB * FN, FFT_WIDTH), lambda i: (i, 0))
    z = pl.pallas_call(
        _fft2_kernel,
        grid=(FN // FB,),
        in_specs=[rows, rows, pl.BlockSpec((FB, FN, 2 * FN), lambda i: (i, 0, 0))],
        out_specs=pl.BlockSpec((FN, FB, FFT_WIDTH), lambda i: (0, i, 0)),
        out_shape=jax.ShapeDtypeStruct((FN, FN, FFT_WIDTH), BF16),
        scratch_shapes=[pltpu.VMEM((FBH * FSL, FN * SUBLANES, LANES), F32)],
        compiler_params=_cparams(("parallel",)),
        name="fft2",
    )(yr.reshape(N_TOK, FFT_WIDTH), yi.reshape(N_TOK, FFT_WIDTH), st2)
    return z.reshape(N_TOK, FFT_WIDTH)


def _gelu_tanh(x):
    return 0.5 * x * (1.0 + jnp.tanh(math.sqrt(2.0 / math.pi) * (x + 0.044715 * (x * x * x))))


def _mix_kernel(x_ref, er_ref, ec_ref, lg_ref, lb_ref, m1_ref, s1_ref, wg_ref, bg_ref,
                yt_ref, zr_ref, wglu_ref, bglu_ref, wbs_ref, wbf_ref, bbf_ref, wo_ref, bo_ref,
                g1_ref, l1g_ref, l1b_ref, m2_ref, s2_ref, wr_ref, br_ref, tri_ref, etri_ref,
                h1_ref, u2_ref, pos_ref, gate_ref, cnt_ref, scr_ref):
    pos = _pos_code(er_ref, ec_ref, TM)

    def front(r0, nr):
        rows = slice(r0, r0 + nr)
        h = _layer_norm(x_ref[rows, :] + pos[rows, :], lg_ref[...], lb_ref[...])
        u = (h * m1_ref[...] + s1_ref[...]).astype(BF16)

        c0, nc = r0 // CH, nr // CH
        for t in range(CH):
            for j in range(NJ):
                scr_ref[j, pl.ds(r0 + t, nc, stride=CH), :] = (
                    yt_ref[t, c0:c0 + nc, j * LANES:(j + 1) * LANES].astype(F32))
        ys = jnp.concatenate([scr_ref[j, rows, :] for j in range(NJ)], axis=-1)
        z = jnp.dot(_gelu_tanh(ys).astype(BF16), wglu_ref[...], preferred_element_type=F32) + bglu_ref[...]
        glu = (z[:, :S5_WIDTH] * _sigmoid(z[:, S5_WIDTH:])).astype(BF16)
        gates = _sigmoid(jnp.dot(u, wg_ref[...], preferred_element_type=F32) + bg_ref[...])
        y_s5 = jnp.dot(glu, wbs_ref[...], preferred_element_type=F32)
        y_fft = jnp.dot(zr_ref[rows, :], wbf_ref[...], preferred_element_type=F32) + bbf_ref[...]
        mixed = (gates[:, :D] * y_s5 + gates[:, D:] * y_fft).astype(BF16)
        y = jnp.dot(mixed, wo_ref[...], preferred_element_type=F32) + bo_ref[...]
        h1 = _layer_norm(ALPHA * h + g1_ref[...] * y, l1g_ref[...], l1b_ref[...])
        h1_ref[rows, :] = h1
        u2 = h1 * m2_ref[...] + s2_ref[...]
        u2_ref[rows, :] = u2.astype(BF16)
        u_hi = u2.astype(BF16)
        u_lo = (u2 - u_hi.astype(F32)).astype(BF16)

        def nt(a, b):
            return lax.dot_general(a, b, (((1,), (1,)), ((), ())), preferred_element_type=F32)
        return nt(wr_ref[0], u_hi) + nt(wr_ref[0], u_lo) + nt(wr_ref[1], u_hi)

    nr = TM // MIX_GROUPS
    logits = jnp.concatenate([front(g * nr, nr) for g in range(MIX_GROUPS)], axis=-1) + br_ref[:, 0:1]
    eidx = lax.broadcasted_iota(jnp.int32, (N_EXPERTS, TM), 0)
    vals, hots = [], []
    cur = logits
    for _k in range(TOP_K):
        m = jnp.max(cur, axis=0, keepdims=True)
        sel = jnp.min(jnp.where(cur == m, eidx, N_EXPERTS), axis=0, keepdims=True)
        hot = eidx == sel
        cur = jnp.where(hot, -jnp.inf, cur)
        vals.append(m)
        hots.append(hot)
    exps = [jnp.exp(v - vals[0]) for v in vals]
    den = exps[0] + exps[1] + exps[2] + exps[3]
    gate4 = jnp.concatenate([e / den for e in exps], axis=0)

    hot_sum = (hots[0] | hots[1] | hots[2] | hots[3]).astype(F32)
    before = jnp.dot(hot_sum.astype(BF16), tri_ref[...], preferred_element_type=F32)
    cnt = jnp.broadcast_to(jnp.sum(hot_sum, axis=1, keepdims=True), (N_EXPERTS, LANES))
    cnt8 = jnp.floor((cnt + (SEG_ALIGN - 1)) * (1.0 / SEG_ALIGN)) * SEG_ALIGN
    seg0 = jnp.dot(etri_ref[...], cnt8.astype(BF16), preferred_element_type=F32)
    tot = seg0[:, 0:1] + before
    pos4 = jnp.concatenate(
        [jnp.sum(jnp.where(hk, tot, 0.0), axis=0, keepdims=True) for hk in hots], axis=0)
    pos_ref[...] = pos4.astype(jnp.int32)
    cnt_ref[0] = cnt

    gpad = jnp.concatenate([gate4, pos4, jnp.zeros((LANES - 2 * TOP_K, TM), F32)], axis=0)
    gate_ref[...] = gpad.T


def _mix(x, emb_r, emb_c, lg, lb, m1, s1, wg, bg, y_t, zr, wglu, bglu, wbs, wbf, bbf, wo, bo,
         g1, l1g, l1b, m2, s2, wr_t, br, tri, etri):
    vec = pl.BlockSpec((1, D), lambda i: (0, 0))

    def full(a):
        return pl.BlockSpec(a.shape, lambda i: (0,) * a.ndim)
    return pl.pallas_call(
        _mix_kernel,
        grid=(N_TOK // TM,),
        in_specs=[pl.BlockSpec((TM, D), lambda i: (i, 0)),
                  pl.BlockSpec((TM // GRID_W, D // 2), lambda i: (i, 0)),
                  pl.BlockSpec((GRID_W, D // 2), lambda i: (0, 0)),
                  vec, vec, vec, vec, full(wg), full(bg),
                  pl.BlockSpec((CH, TM // CH, S5_WIDTH), lambda i: (0, i, 0)),
                  pl.BlockSpec((TM, FFT_WIDTH), lambda i: (i, 0)),
                  full(wglu), full(bglu), full(wbs), full(wbf), full(bbf), full(wo), full(bo),
                  vec, vec, vec, vec, vec, full(wr_t), full(br), full(tri), full(etri)],
        out_specs=(pl.BlockSpec((TM, D), lambda i: (i, 0)),
                   pl.BlockSpec((TM, D), lambda i: (i, 0)),
                   pl.BlockSpec((TOP_K, TM), lambda i: (0, i)),
                   pl.BlockSpec((TM, LANES), lambda i: (i, 0)),
                   pl.BlockSpec((1, N_EXPERTS, LANES), lambda i: (i, 0, 0))),
        out_shape=(jax.ShapeDtypeStruct((N_TOK, D), F32),
                   jax.ShapeDtypeStruct((N_TOK, D), BF16),
                   jax.ShapeDtypeStruct((TOP_K, N_TOK), jnp.int32),
                   jax.ShapeDtypeStruct((N_TOK, LANES), F32),
                   jax.ShapeDtypeStruct((N_TILES, N_EXPERTS, LANES), F32)),
        scratch_shapes=[pltpu.VMEM((NJ, TM, LANES), F32)],
        compiler_params=_cparams(("parallel",)),
        name="mix",
    )(x, emb_r, emb_c, lg, lb, m1, s1, wg, bg, y_t, zr, wglu, bglu, wbs, wbf, bbf, wo, bo,
      g1, l1g, l1b, m2, s2, wr_t, br, tri, etri)


def _on_parity(i, fn):
    @pl.when(i % 2 == 0)
    def _():
        fn(0)

    @pl.when(i % 2 == 1)
    def _():
        fn(1)


def _dispatch_kernel(pend_ref, fill_ref, dprev_ref, dest_ref, pos_ref, u_ref, buf_ref,
                     sorted_ref, zero_ref, zsem, tsem, sems):
    i = pl.program_id(0)

    units = BM // ZB

    def clear_copy(start, sem):
        return pltpu.make_async_copy(
            zero_ref, buf_ref.at[pl.ds(pl.multiple_of(start, ZB), ZB)], sem)

    def unused_units(fn):
        def unit(b, c):
            fn(clear_copy(b * ZB, tsem))
            return c
        lax.fori_loop(pend_ref[N_EXPERTS - 1] // ZB, N_BLOCKS * units, unit, 0)

    @pl.when(i == 0)
    def _():
        zero_ref[...] = jnp.zeros_like(zero_ref)

        def each(fn):
            def expert(e, c):
                for z in range(units):
                    start = pend_ref[e] - (z + 1) * ZB

                    @pl.when(start + ZB > fill_ref[e])
                    def _():
                        fn(clear_copy(start, zsem))
                return c
            lax.fori_loop(0, N_EXPERTS, expert, 0)
            for b in range(N_BLOCKS * units, N_BLOCKS_ALL * units):
                fn(clear_copy(b * ZB, zsem))
        each(lambda cp: cp.start())
        unused_units(lambda cp: cp.start())
        each(lambda cp: cp.wait())

        sorted_ref[...] = jnp.zeros_like(sorted_ref)

    def chunk_copy(slot, table_ref, j):
        dst = pl.multiple_of(table_ref[0, 0, j], SEG_ALIGN)
        return pltpu.make_async_copy(sorted_ref.at[slot, pl.ds(j * SEG_ALIGN, SEG_ALIGN)],
                                     buf_ref.at[pl.ds(dst, SEG_ALIGN)], sems.at[slot])

    def drain(slot):
        pltpu.make_async_copy(sorted_ref.at[slot], buf_ref.at[pl.ds(0, CAP)], sems.at[slot]).wait()

    def run(slot):
        pos = pos_ref[...]
        u = u_ref[...]
        n_rb = CAP // CAP_BLOCK
        per_rb = NCHK // (n_rb // 2)
        for rb in range(n_rb):
            for j in range(rb * per_rb, min((rb + 1) * per_rb, NCHK)):
                chunk_copy(1 - slot, dprev_ref, j).start()
            rows = lax.broadcasted_iota(jnp.int32, (CAP_BLOCK, TM), 0) + rb * CAP_BLOCK
            hit = rows == pos[0:1]
            for k in range(1, TOP_K):
                hit = hit | (rows == pos[k:k + 1])
            onehot = jnp.where(hit, 1.0, 0.0).astype(BF16)
            sorted_ref[slot, rb * CAP_BLOCK:(rb + 1) * CAP_BLOCK, :] = jnp.dot(
                onehot, u, preferred_element_type=F32).astype(BF16)
        drain(1 - slot)

        @pl.when(i == N_TILES - 1)
        def _():
            def issue(j, c):
                chunk_copy(slot, dest_ref, j).start()
                return c
            lax.fori_loop(0, NCHK, issue, 0)
            drain(slot)
            unused_units(lambda cp: cp.wait())
    _on_parity(i, run)


def _dispatch(pad_ends, padded, chunk_table, pos_t, u2):
    return pl.pallas_call(
        _dispatch_kernel,
        grid_spec=pltpu.PrefetchScalarGridSpec(
            num_scalar_prefetch=2,
            grid=(N_TILES,),
            in_specs=[pl.BlockSpec((1, 1, NCHK), lambda i, a, b: (i, 0, 0), memory_space=pltpu.SMEM),
                      pl.BlockSpec((1, 1, NCHK), lambda i, a, b: (i + 1, 0, 0), memory_space=pltpu.SMEM),
                      pl.BlockSpec((TOP_K, TM), lambda i, a, b: (0, i)),
                      pl.BlockSpec((TM, D), lambda i, a, b: (i, 0))],
            out_specs=pl.BlockSpec(memory_space=pl.ANY),
            scratch_shapes=[pltpu.VMEM((2, CAP, D), BF16),
                            pltpu.VMEM((ZB, D), BF16),
                            pltpu.SemaphoreType.DMA(()),
                            pltpu.SemaphoreType.DMA(()),
                            pltpu.SemaphoreType.DMA((2,))]),
        out_shape=jax.ShapeDtypeStruct((ROWS_ALL, D), BF16),
        compiler_params=_cparams(("arbitrary",)),
        name="dispatch",
    )(pad_ends, padded, chunk_table, chunk_table, pos_t, u2)


def _ffn_kernel(be_ref, nu_ref, run_ref, nxt_ref, valid_ref, x_ref, wu_hbm, bu_ref, wd_hbm, bd_ref,
                y_ref, wu_ref, wd_ref, wub_ref, wdb_ref, sems):
    i = pl.program_id(0)
    used = i < nu_ref[0]

    def weight_copies(e, slot):
        return (pltpu.make_async_copy(wu_hbm.at[e], wu_ref.at[slot], sems.at[slot]),
                pltpu.make_async_copy(wd_hbm.at[e], wd_ref.at[slot], sems.at[slot]))

    @pl.when(used)
    def _():
        run = run_ref[i]

        @pl.when(run >= 0)
        def _():
            def open_run(slot):
                @pl.when(run == 0)
                def _():
                    for cp in weight_copies(be_ref[i], slot):
                        cp.start()

                @pl.when(nxt_ref[i] >= 0)
                def _():
                    for cp in weight_copies(nxt_ref[i], 1 - slot):
                        cp.start()
                for cp in weight_copies(be_ref[i], slot):
                    cp.wait()
                wub_ref[...] = wu_ref[slot].astype(BF16)
                wdb_ref[...] = wd_ref[slot].astype(BF16)
            _on_parity(run, open_run)

        def expert_rows(r0, nr):
            rows = slice(r0, r0 + nr)
            e = be_ref[i]
            h = (jnp.dot(x_ref[rows, :], wub_ref[...], preferred_element_type=F32)
                 + bu_ref[pl.ds(e, 1), :])
            h_glu = jnp.minimum(h[:, :D], SWIGLU_LIMIT)
            h_lin = jnp.clip(h[:, D:], -SWIGLU_LIMIT, SWIGLU_LIMIT)
            act = (h_glu * _sigmoid(SWIGLU_ALPHA * h_glu) * (h_lin + 1.0)).astype(BF16)
            y_ref[rows, :] = (jnp.dot(act, wdb_ref[...], preferred_element_type=F32)
                              + bd_ref[pl.ds(e, 1), :]).astype(BF16)

        valid = valid_ref[i]

        @pl.when(valid == BM)
        def _():
            expert_rows(0, BM)

        @pl.when(valid < BM)
        def _():
            for h0 in range(0, BM, FFN_HALF):
                @pl.when(valid >= h0 + FFN_HALF)
                def _(h0=h0):
                    expert_rows(h0, FFN_HALF)

                @pl.when(valid < h0 + FFN_HALF)
                def _(h0=h0):
                    for r0 in range(h0, h0 + FFN_HALF, FFN_TAIL):
                        @pl.when(r0 < valid)
                        def _(r0=r0):
                            expert_rows(r0, FFN_TAIL)

                        @pl.when(r0 >= valid)
                        def _(r0=r0):
                            y_ref[r0:r0 + FFN_TAIL, :] = jnp.zeros((FFN_TAIL, D), BF16)

    @pl.when(jnp.logical_not(used))
    def _():
        y_ref[...] = jnp.zeros_like(y_ref)


def _ffn(block_expert, n_used, run_id, next_expert, valid, buf, w_up, b_up, w_down, b_down):
    def blk(i, be, nu, *_):
        return jnp.minimum(i, nu[0] - 1)
    return pl.pallas_call(
        _ffn_kernel,
        grid_spec=pltpu.PrefetchScalarGridSpec(
            num_scalar_prefetch=5,
            grid=(N_BLOCKS_ALL,),
            in_specs=[pl.BlockSpec((BM, D), lambda i, *s: (blk(i, *s), 0)),
                      pl.BlockSpec(memory_space=pl.ANY),
                      pl.BlockSpec((N_EXPERTS, 2 * D), lambda i, *s: (0, 0)),
                      pl.BlockSpec(memory_space=pl.ANY),
                      pl.BlockSpec((N_EXPERTS, D), lambda i, *s: (0, 0))],
            out_specs=pl.BlockSpec((BM, D), lambda i, *s: (i, 0)),
            scratch_shapes=[pltpu.VMEM((2, D, 2 * D), F32),
                            pltpu.VMEM((2, D, D), F32),
                            pltpu.VMEM((D, 2 * D), BF16),
                            pltpu.VMEM((D, D), BF16),
                            pltpu.SemaphoreType.DMA((2,))]),
        out_shape=jax.ShapeDtypeStruct((ROWS_ALL, D), BF16),
        compiler_params=_cparams(("arbitrary",)),
        name="ffn",
    )(block_expert, n_used, run_id, next_expert, valid, buf, w_up, b_up, w_down, b_down)


def _combine_kernel(dest_ref, dnext_ref, y_ref, h1_ref, gate_ref, g2_ref, lg_ref, lb_ref,
                    o_ref, sorted_ref, sems):
    i = pl.program_id(0)

    def chunk_copy(slot, table_ref, j):
        src = pl.multiple_of(table_ref[0, 0, j], SEG_ALIGN)
        return pltpu.make_async_copy(y_ref.at[pl.ds(src, SEG_ALIGN)],
                                     sorted_ref.at[slot, pl.ds(j * SEG_ALIGN, SEG_ALIGN)],
                                     sems.at[slot])

    def drain(slot):
        pltpu.make_async_copy(y_ref.at[pl.ds(0, CAP)], sorted_ref.at[slot], sems.at[slot]).wait()

    @pl.when(i == 0)
    def _():
        def issue(j, c):
            chunk_copy(0, dest_ref, j).start()
            return c
        lax.fori_loop(0, NCHK, issue, 0)

    def run(slot):
        drain(slot)

        gp = gate_ref[...]
        m = jnp.zeros((TM, D), F32)
        n_cb = CAP // CAP_BLOCK
        per_cb = NCHK // (n_cb // 2)
        for cb in range(n_cb):
            for j in range(cb * per_cb, min((cb + 1) * per_cb, NCHK)):
                chunk_copy(1 - slot, dnext_ref, j).start()
            cols = (lax.broadcasted_iota(jnp.int32, (TM, CAP_BLOCK), 1) + cb * CAP_BLOCK).astype(F32)
            g = jnp.where(cols == gp[:, TOP_K:TOP_K + 1], gp[:, 0:1], 0.0)
            for k in range(1, TOP_K):
                g = g + jnp.where(cols == gp[:, TOP_K + k:TOP_K + k + 1], gp[:, k:k + 1], 0.0)
            rows = sorted_ref[slot, cb * CAP_BLOCK:(cb + 1) * CAP_BLOCK, :]
            m = m + jnp.dot(g.astype(BF16), rows, preferred_element_type=F32)
        o_ref[...] = _layer_norm(ALPHA * h1_ref[...] + g2_ref[...] * m, lg_ref[...], lb_ref[...])

        @pl.when(i == N_TILES - 1)
        def _():
            drain(1 - slot)
    _on_parity(i, run)


def _combine(chunk_table, y_buf, h1, gate_tok, g2, lg, lb):
    vec = pl.BlockSpec((1, D), lambda i: (0, 0))
    return pl.pallas_call(
        _combine_kernel,
        grid_spec=pltpu.PrefetchScalarGridSpec(
            num_scalar_prefetch=0,
            grid=(N_TILES,),
            in_specs=[pl.BlockSpec((1, 1, NCHK), lambda i: (i + 1, 0, 0), memory_space=pltpu.SMEM),
                      pl.BlockSpec((1, 1, NCHK), lambda i: (i + 2, 0, 0), memory_space=pltpu.SMEM),
                      pl.BlockSpec(memory_space=pl.ANY),
                      pl.BlockSpec((TM, D), lambda i: (i, 0)),
                      pl.BlockSpec((TM, LANES), lambda i: (i, 0)),
                      vec, vec, vec],
            out_specs=pl.BlockSpec((TM, D), lambda i: (i, 0)),
            scratch_shapes=[pltpu.VMEM((2, CAP, D), BF16),
                            pltpu.SemaphoreType.DMA((2,))]),
        out_shape=jax.ShapeDtypeStruct((N_TOK, D), F32),
        compiler_params=_cparams(("arbitrary",)),
        name="combine",
    )(chunk_table, chunk_table, y_buf, h1, gate_tok, g2, lg, lb)


def _sincos_tables():
    q = D // 4
    omega = 1.0 / (10000.0 ** (jnp.arange(q, dtype=F32) / q))

    def emb(n):
        ang = jnp.arange(n, dtype=F32)[:, None] * omega[None, :]
        return jnp.concatenate([jnp.sin(ang), jnp.cos(ang)], axis=-1)
    return emb(N_TOK // GRID_W), emb(GRID_W)


def _tile_slots(a, tile):
    return jnp.transpose(a.reshape(TOP_K, N_TOK // tile, tile), (1, 0, 2)).reshape(N_TOK // tile, 1, TOP_K * tile)


def kernel(x, c, ctx, c_ctx, ln_in_g, ln_in_b, w_ada, b_ada, w_in, b_in, s5_lambda_re, s5_lambda_im, s5_log_dt, s5_b_re, s5_b_im, s5_c_re, s5_c_im, s5_d, w_glu, b_glu, w_br_s5, w_br_fft, b_br_fft, w_out, b_out, ln1_g, ln1_b, w_router, b_router, w_up, b_up, w_down, b_down, ln2_g, ln2_b):
    assert x.shape == (1, N_TOK, D) and ctx.shape == (1, N_CTX, D) and w_ada.shape[0] == 1
    row = lambda v: v.reshape(1, -1).astype(F32)

    cc = jnp.concatenate([c.reshape(1, D), c_ctx.reshape(1, D), jnp.zeros((SUBLANES - 2, D), F32)], axis=0)
    ada = _ada(cc, w_ada[0], row(b_ada[0]))
    sh1, sc1, g1, sh2, sc2, g2 = (ada[0:1, k * D:(k + 1) * D] for k in range(6))
    sh1c, sc1c = ada[1:2, 0:D], ada[1:2, D:2 * D]

    emb_r, emb_c = _sincos_tables()
    st1, st2, fc = _dft_tables()
    lg, lb = row(ln_in_g), row(ln_in_b)

    w_s5 = w_in[0][:, :S5_WIDTH]
    w_fft = w_in[0][:, S5_WIDTH:S5_WIDTH + FFT_WIDTH]
    w_g = w_in[0][:, S5_WIDTH + FFT_WIDTH:]
    b_s5 = row(b_in[0][:S5_WIDTH])
    b_fft8 = jnp.concatenate([row(b_in[0][S5_WIDTH:S5_WIDTH + FFT_WIDTH]),
                              jnp.zeros((SUBLANES - 1, FFT_WIDTH), F32)], axis=0)
    b_g = row(b_in[0][S5_WIDTH + FFT_WIDTH:])
    w_fc, b_fc = _fft_weights(w_fft, b_fft8, fc)
    wcat = jnp.concatenate([w_s5, w_fc], axis=1).astype(BF16)
    bcat = jnp.concatenate([b_s5, b_fc[0:1]], axis=1)

    x2 = x[0]
    p_t, xr, xi = _proj(x2, emb_r, emb_c, lg, lb, 1.0 + sc1, sh1, wcat, bcat)
    pc_t = _ctx_proj(ctx[0], lg, lb, 1.0 + sc1c, sh1c, w_s5.astype(BF16), b_s5)

    b_c, a_q, a_p, trans, ctx_w = _s5_tables(
        s5_lambda_re[0], s5_lambda_im[0], s5_log_dt[0], s5_b_re[0], s5_b_im[0],
        s5_c_re[0], s5_c_im[0], s5_d[0])
    w_m, w_q, w_p = _s5_expand(b_c, a_q, a_p)
    y_t = _s5(p_t, pc_t, w_m, w_q, w_p, trans, ctx_w)

    yr, yi = _fft1(xr, xi, st1)
    zr = _fft2(yr, yi, st2)

    tri = (jnp.arange(TM)[:, None] < jnp.arange(TM)[None, :]).astype(BF16)
    br = jnp.broadcast_to(b_router[0].reshape(N_EXPERTS, 1), (N_EXPERTS, LANES))
    etri = (jnp.arange(N_EXPERTS)[:, None] > jnp.arange(N_EXPERTS)[None, :]).astype(BF16)
    wr_t = jnp.transpose(w_router[0])
    wr_hi = wr_t.astype(BF16)
    wr_split = jnp.stack([wr_hi, (wr_t - wr_hi.astype(F32)).astype(BF16)], axis=0)
    h1, u2, pos_t, gate_tok, counts = _mix(
        x2, emb_r, emb_c, lg, lb, 1.0 + sc1, sh1, w_g.astype(BF16), b_g, y_t, zr,
        w_glu[0].astype(BF16), row(b_glu[0]), w_br_s5[0].astype(BF16), w_br_fft[0].astype(BF16),
        row(b_br_fft[0]), w_out[0].astype(BF16), row(b_out[0]), g1, row(ln1_g[0]), row(ln1_b[0]),
        1.0 + sc2, sh2, wr_split, br, tri, etri)

    cnt = counts[:, :, 0].astype(jnp.int32)
    seg = (cnt + SEG_ALIGN - 1) // SEG_ALIGN * SEG_ALIGN
    seg_end = jnp.cumsum(seg, axis=1)
    seg_start = seg_end - seg
    padded = (jnp.sum(seg, axis=0) + BM - 1) // BM * BM
    pad_ends = jnp.cumsum(padded)
    seg_dest = (pad_ends - padded)[None, :] + jnp.cumsum(seg, axis=0) - seg
    chunk_row = jnp.arange(NCHK, dtype=jnp.int32) * SEG_ALIGN
    chunk_exp = jnp.minimum(jnp.sum(chunk_row[None, :, None] >= seg_end[:, None, :], axis=-1),
                            N_EXPERTS - 1)
    own = chunk_exp[:, :, None] == jnp.arange(N_EXPERTS, dtype=jnp.int32)[None, None, :]
    chunk_dest = (jnp.sum(jnp.where(own, (seg_dest - seg_start)[:, None, :], 0), axis=-1)
                  + chunk_row[None, :]).astype(jnp.int32).reshape(N_TILES, 1, NCHK)
    nchk = (seg_end[:, -1] // SEG_ALIGN).astype(jnp.int32)
    block_start = jnp.arange(N_BLOCKS_ALL, dtype=jnp.int32) * BM
    block_expert = jnp.minimum(jnp.sum(block_start[:, None] >= pad_ends[None, :], axis=1),
                               N_EXPERTS - 1).astype(jnp.int32)
    n_used = (pad_ends[-1:] // BM).astype(jnp.int32)
    opens = (block_start < pad_ends[-1]) & (
        block_expert != jnp.concatenate([jnp.full((1,), -1, jnp.int32), block_expert[:-1]]))
    run_id = jnp.where(opens, jnp.cumsum(opens.astype(jnp.int32)) - 1, -1).astype(jnp.int32)
    experts = jnp.arange(N_EXPERTS, dtype=jnp.int32)
    later = (experts[None, :] > block_expert[:, None]) & (padded[None, :] > 0)
    next_expert = jnp.min(jnp.where(later, experts[None, :], N_EXPERTS), axis=1)
    next_expert = jnp.where(next_expert < N_EXPERTS, next_expert, -1).astype(jnp.int32)

    spare = (ROWS + chunk_row)[None, None, :]
    chunk_table = jnp.concatenate(
        [spare,
         jnp.where(chunk_row[None, None, :] < (nchk * SEG_ALIGN)[:, None, None], chunk_dest, spare),
         spare], axis=0).astype(jnp.int32)
    fill_ends = (pad_ends - padded + jnp.sum(seg, axis=0)).astype(jnp.int32)
    buf = _dispatch(pad_ends.astype(jnp.int32), fill_ends, chunk_table, pos_t, u2)
    mine = block_expert[:, None] == experts[None, :]
    filled = jnp.sum(jnp.where(mine, fill_ends[None, :], 0), axis=1)
    valid = jnp.clip(filled - block_start, 0, BM).astype(jnp.int32)
    y_buf = _ffn(block_expert, n_used, run_id, next_expert, valid, buf, w_up[0],
                 b_up[0], w_down[0], b_down[0])
    out = _combine(chunk_table, y_buf, h1, gate_tok, g2, row(ln2_g[0]), row(ln2_b[0]))
    return out.reshape(1, N_TOK, D)
```

```python
import functools
import math

import jax
import jax.numpy as jnp
import numpy as np
from jax import lax
from jax.experimental import pallas as pl
from jax.experimental.pallas import tpu as pltpu

F32 = jnp.float32
BF16 = jnp.bfloat16
HI = lax.Precision.HIGHEST

D = 1024
N_TOK = 16384
N_CTX = 256
GRID_W = 64
S5_GROUP = 16
S5_GROUPS = 32
S5_STATE = 64
S5_WIDTH = 512
FFT_GROUPS = 4
FFT_DIM = 128
FFT_WIDTH = 512
N_EXPERTS = 32
TOP_K = 4
LN_EPS = 1e-5
ALPHA = 2.0 ** 0.25
SWIGLU_ALPHA = 1.702
SWIGLU_LIMIT = 7.0

LANES = 128
SUBLANES = 8
VMEM_LIMIT = 56 * 1024 * 1024

CH = 8
N_CHUNK = N_TOK // CH
N_CHUNK_CTX = N_CTX // CH
NSEG = SUBLANES
SEG = N_CHUNK // NSEG
SCAN_UNROLL = 4
GPT = LANES // S5_GROUP
NJ = S5_WIDTH // LANES
CL = CH * LANES
SW = 4 * GPT * S5_STATE

FN = 128
FB = 16

TM = 512
TM_PROJ = 1024
N_TILES = N_TOK // TM
BM = 1024
ZB = 512
FFN_HALF = 512
FFN_TAIL = 128
N_SLOTS = N_TOK * TOP_K
SEG_ALIGN = 2 * SUBLANES
CAP_BLOCK = 256
CAP = -(-(TOP_K * TM + N_EXPERTS * (SEG_ALIGN - 1)) // CAP_BLOCK) * CAP_BLOCK
NCHK = CAP // SEG_ALIGN
N_BLOCKS = -(-(N_SLOTS + N_TILES * N_EXPERTS * (SEG_ALIGN - 1)) // BM) + N_EXPERTS
ROWS = N_BLOCKS * BM
N_BLOCKS_ALL = N_BLOCKS + -(-CAP // BM)
ROWS_ALL = N_BLOCKS_ALL * BM


def _cparams(sem):
    return pltpu.CompilerParams(dimension_semantics=sem, vmem_limit_bytes=VMEM_LIMIT)


def _layer_norm(x, g, b):
    mu = jnp.mean(x, axis=-1, keepdims=True)
    xc = x - mu
    var = jnp.mean(xc * xc, axis=-1, keepdims=True)
    return xc * lax.rsqrt(var + LN_EPS) * g + b


def _sigmoid(x):
    return 0.5 * jnp.tanh(0.5 * x) + 0.5


def _ada_kernel(c_ref, w_ref, b_ref, o_ref):
    c = c_ref[...]
    s = c * _sigmoid(c)
    o_ref[...] = jnp.dot(s, w_ref[...], preferred_element_type=F32, precision=HI) + b_ref[...]


def _ada(cc, w_ada, b_ada):
    nb = 4
    wb = 6 * D // nb
    return pl.pallas_call(
        _ada_kernel,
        grid=(nb,),
        in_specs=[pl.BlockSpec((SUBLANES, D), lambda i: (0, 0)),
                  pl.BlockSpec((D, wb), lambda i: (0, i)),
                  pl.BlockSpec((1, wb), lambda i: (0, i))],
        out_specs=pl.BlockSpec((SUBLANES, wb), lambda i: (0, i)),
        out_shape=jax.ShapeDtypeStruct((SUBLANES, 6 * D), F32),
        compiler_params=_cparams(("parallel",)),
        name="ada",
    )(cc, w_ada, b_ada)


def _fftw_kernel(w_ref, b_ref, f_ref, wo_ref, bo_ref):
    f = f_ref[...]
    wo_ref[...] = jnp.dot(w_ref[...], f, preferred_element_type=F32, precision=HI).astype(BF16)
    bo_ref[...] = jnp.dot(b_ref[...], f, preferred_element_type=F32, precision=HI)


def _fft_weights(w_in, b_fft8, fc):
    def full(a):
        return pl.BlockSpec(a.shape, lambda i: (0,) * a.ndim)
    outs = (jax.ShapeDtypeStruct((D, 2 * FFT_WIDTH), BF16),
            jax.ShapeDtypeStruct((SUBLANES, 2 * FFT_WIDTH), F32))
    return pl.pallas_call(
        _fftw_kernel,
        grid=(1,),
        in_specs=[pl.BlockSpec((D, FFT_WIDTH), lambda i: (0, S5_WIDTH // FFT_WIDTH)),
                  full(b_fft8), full(fc)],
        out_specs=tuple(pl.BlockSpec(o.shape, lambda i: (0, 0)) for o in outs),
        out_shape=outs,
        compiler_params=_cparams(("arbitrary",)),
        name="fftw",
    )(w_in, b_fft8, fc)


def _pos_code(er_ref, ec_ref, tm):
    nr = tm // GRID_W
    er = er_ref[...]
    row = jnp.broadcast_to(er[:, None, :], (nr, GRID_W, D // 2)).reshape(tm, D // 2)
    col = jnp.concatenate([ec_ref[...]] * nr, axis=0)
    return jnp.concatenate([row, col], axis=-1)


def _to_chunk_major(val, scr_ref, out_ref, tm):
    for j in range(NJ):
        scr_ref[j] = val[:, j * LANES:(j + 1) * LANES]
    for t in range(CH):
        for j in range(NJ):
            piece = scr_ref[j, pl.ds(t, tm // CH, stride=CH), :]
            out_ref[t, :, j * LANES:(j + 1) * LANES] = piece.astype(out_ref.dtype)


def _proj_kernel(x_ref, er_ref, ec_ref, lg_ref, lb_ref, m_ref, s_ref, ws_ref, wf_ref, b_ref,
                 p_ref, xr_ref, xi_ref, scr_ref):
    x = x_ref[...] + _pos_code(er_ref, ec_ref, TM_PROJ)
    h = _layer_norm(x, lg_ref[...], lb_ref[...])
    u = (h * m_ref[...] + s_ref[...]).astype(BF16)
    p_s5 = jnp.dot(u, ws_ref[...], preferred_element_type=F32) + b_ref[:, :S5_WIDTH]
    _to_chunk_major(p_s5, scr_ref, p_ref, TM_PROJ)
    p_f = jnp.dot(u, wf_ref[...], preferred_element_type=F32) + b_ref[:, S5_WIDTH:]
    xr_ref[...] = p_f[:, :FFT_WIDTH].astype(BF16)
    xi_ref[...] = p_f[:, FFT_WIDTH:].astype(BF16)


def _proj(x, emb_r, emb_c, lg, lb, m1, s1, w_in, w_fc, bcat):
    nw = bcat.shape[1]
    vec = pl.BlockSpec((1, D), lambda i: (0, 0))
    return pl.pallas_call(
        _proj_kernel,
        grid=(N_TOK // TM_PROJ,),
        in_specs=[pl.BlockSpec((TM_PROJ, D), lambda i: (i, 0)),
                  pl.BlockSpec((TM_PROJ // GRID_W, D // 2), lambda i: (i, 0)),
                  pl.BlockSpec((GRID_W, D // 2), lambda i: (0, 0)),
                  vec, vec, vec, vec,
                  pl.BlockSpec((D, S5_WIDTH), lambda i: (0, 0)),
                  pl.BlockSpec((D, 2 * FFT_WIDTH), lambda i: (0, 0)),
                  pl.BlockSpec((1, nw), lambda i: (0, 0))],
        out_specs=(pl.BlockSpec((CH, TM_PROJ // CH, S5_WIDTH), lambda i: (0, i, 0)),
                   pl.BlockSpec((TM_PROJ, FFT_WIDTH), lambda i: (i, 0)),
                   pl.BlockSpec((TM_PROJ, FFT_WIDTH), lambda i: (i, 0))),
        out_shape=(jax.ShapeDtypeStruct((CH, N_CHUNK, S5_WIDTH), BF16),
                   jax.ShapeDtypeStruct((N_TOK, FFT_WIDTH), BF16),
                   jax.ShapeDtypeStruct((N_TOK, FFT_WIDTH), BF16)),
        scratch_shapes=[pltpu.VMEM((NJ, TM_PROJ, LANES), F32)],
        compiler_params=_cparams(("parallel",)),
        name="proj",
    )(x, emb_r, emb_c, lg, lb, m1, s1, w_in, w_fc, bcat)


def _ctx_proj_kernel(x_ref, lg_ref, lb_ref, m_ref, s_ref, w_ref, b_ref, p_ref, scr_ref):
    h = _layer_norm(x_ref[...], lg_ref[...], lb_ref[...])
    u = (h * m_ref[...] + s_ref[...]).astype(BF16)
    p = jnp.dot(u, w_ref[...], preferred_element_type=F32) + b_ref[...]
    _to_chunk_major(p, scr_ref, p_ref, N_CTX)


def _ctx_proj(ctx, lg, lb, m1, s1, w_s5, b_s5):
    return pl.pallas_call(
        _ctx_proj_kernel,
        out_shape=jax.ShapeDtypeStruct((CH, N_CHUNK_CTX, S5_WIDTH), BF16),
        scratch_shapes=[pltpu.VMEM((NJ, N_CTX, LANES), F32)],
        compiler_params=pltpu.CompilerParams(vmem_limit_bytes=VMEM_LIMIT),
        name="ctxproj",
    )(ctx, lg, lb, m1, s1, w_s5, b_s5)


def _s5_tables(lam_re, lam_im, log_dt, b_re, b_im, c_re, c_im, d_skip):
    dt = jnp.exp(log_dt)[..., None]
    zr = lam_re * dt
    zi = lam_im * dt

    def apow(m):
        m = jnp.asarray(m, F32)
        mag = jnp.exp(zr[..., None] * m)
        return mag * jnp.cos(zi[..., None] * m), mag * jnp.sin(zi[..., None] * m)

    a_re, a_im = apow(jnp.ones((1,), F32))
    a_re, a_im = a_re[..., 0], a_im[..., 0]
    den = lam_re * lam_re + lam_im * lam_im
    num_re = a_re - 1.0
    k_re = (num_re * lam_re + a_im * lam_im) / den
    k_im = (a_im * lam_re - num_re * lam_im) / den
    bb_re = k_re[..., None] * b_re - k_im[..., None] * b_im
    bb_im = k_re[..., None] * b_im + k_im[..., None] * b_re

    ks = jnp.arange(CH + 1, dtype=F32)
    pw_re, pw_im = apow(ks)
    kmag = jnp.exp(zr[:, :, None, :] * ks[None, None, :, None])
    pk_re = kmag * jnp.cos(zi[:, :, None, :] * ks[None, None, :, None])
    pk_im = kmag * jnp.sin(zi[:, :, None, :] * ks[None, None, :, None])
    bt_re, bt_im = jnp.swapaxes(b_re, 2, 3), jnp.swapaxes(b_im, 2, 3)
    bbt_re = k_re[:, :, None, :] * bt_re - k_im[:, :, None, :] * bt_im
    bbt_im = k_re[:, :, None, :] * bt_im + k_im[:, :, None, :] * bt_re

    ar, ai = pw_re[:, :, :, :CH, None], pw_im[:, :, :, :CH, None]
    cr = jnp.swapaxes(c_re, 2, 3)[:, :, :, None, :]
    ci = jnp.swapaxes(c_im, 2, 3)[:, :, :, None, :]
    ca = jnp.concatenate([cr * ar - ci * ai, -(cr * ai + ci * ar)], axis=2)
    ca = ca.reshape(2, S5_GROUPS, 2 * S5_STATE, CH * S5_GROUP)
    bbt = jnp.concatenate([bbt_re, bbt_im], axis=-1)
    taps = jnp.einsum('dghq,dgqn->dghn', bbt, ca, precision=HI)
    skip = (d_skip.reshape(S5_GROUPS, S5_GROUP, 1) * jnp.eye(S5_GROUP, dtype=F32)[None])
    taps = taps.at[0, :, :, :S5_GROUP].add(skip)
    b_c = jnp.transpose(taps.reshape(2, NJ, GPT, S5_GROUP, CH, S5_GROUP), (1, 4, 0, 2, 3, 5))
    b_c = b_c.reshape(NJ, 2 * CH, LANES, S5_GROUP)

    ef = (CH - 1) - jnp.arange(CH)
    eb = jnp.arange(CH)

    def q_part(d, e):
        pr = pk_re[d][:, e, None, :]
        pi = pk_im[d][:, e, None, :]
        br = bbt_re[d][:, None, :, :]
        bi = bbt_im[d][:, None, :, :]
        return pr * br - pi * bi, pr * bi + pi * br

    def q_rows(v):
        v = v.reshape(NJ, GPT, CH, S5_GROUP, S5_STATE)
        return jnp.transpose(v, (0, 2, 1, 3, 4)).reshape(NJ, CL, S5_STATE)
    a_q = jnp.stack([q_rows(v) for v in q_part(0, ef) + q_part(1, eb)], axis=0)

    of = jnp.arange(CH) + 1
    ob = CH - jnp.arange(CH)

    def p_part(d, e):
        pr = pw_re[d][..., e][:, :, :, None]
        pi = pw_im[d][..., e][:, :, :, None]
        return (ct_re[d] * pr - ct_im[d] * pi, -(ct_re[d] * pi + ct_im[d] * pr))
    ct_re = jnp.swapaxes(c_re, 2, 3)[:, :, :, None, :]
    ct_im = jnp.swapaxes(c_im, 2, 3)[:, :, :, None, :]
    a_p = jnp.stack([v.reshape(NJ, GPT * S5_STATE, CH * S5_GROUP)
                     for v in p_part(0, of) + p_part(1, ob)], axis=0)

    def lanes(v):
        return jnp.transpose(v.reshape(2, NJ, GPT * S5_STATE), (1, 0, 2))
    c_r, c_i = apow(jnp.full((1,), float(CH), F32))
    s_r, s_i = apow(jnp.full((1,), float(CH * SEG), F32))
    cr, ci, sr, si = (lanes(v[..., 0]) for v in (c_r, c_i, s_r, s_i))
    trans = jnp.stack([cr[:, 0], ci[:, 0], cr[:, 1], ci[:, 1],
                       sr[:, 0], si[:, 0], sr[:, 1], si[:, 1]], axis=1)

    cidx = jnp.arange(N_CHUNK_CTX, dtype=F32)
    wf_r, wf_i = apow(CH * (N_CHUNK_CTX - 1 - cidx))
    wb_r, wb_i = apow(CH * cidx)

    def ctx_lanes(v, d):
        return jnp.transpose(v[d].reshape(NJ, GPT * S5_STATE, N_CHUNK_CTX), (0, 2, 1))
    ctx_w = jnp.stack([ctx_lanes(wf_r, 0), ctx_lanes(wf_i, 0),
                       ctx_lanes(wb_r, 1), ctx_lanes(wb_i, 1)], axis=1)
    return b_c, a_q, a_p, trans, ctx_w


def _s5w_kernel(bc_ref, aq_ref, ap_ref, c16_ref, c64_ref, cm_ref, wm_ref, wq_ref, wp_ref):
    def expand(a, c, row_shift, col_shift):
        w = jnp.dot(a, c, preferred_element_type=F32)
        rg = (lax.broadcasted_iota(jnp.int32, (w.shape[0], 1), 0) >> row_shift) & (GPT - 1)
        cg = (lax.broadcasted_iota(jnp.int32, (1, w.shape[1]), 1) >> col_shift) & (GPT - 1)
        return jnp.where(rg == cg, w, 0.0)

    blk = [expand(bc_ref[0, kd], c16_ref[...], 4, 4) for kd in range(2 * CH)]
    for t in range(CH):
        for u in range(CH):
            b = blk[2 * (u - t)] if u > t else blk[2 * (t - u) + 1] if u < t else blk[0] + blk[1]
            wm_ref[0, t * LANES:(t + 1) * LANES, u * LANES:(u + 1) * LANES] = b.astype(BF16)
    half = GPT * S5_STATE
    for s in range(4):
        wq_ref[0, :, s * half:(s + 1) * half] = expand(aq_ref[s, 0], c64_ref[...], 4, 6).astype(BF16)
        wp_ref[0, s * half:(s + 1) * half, :] = expand(ap_ref[s, 0], cm_ref[...], 6, 4).astype(BF16)


def _s5_expand(b_c, a_q, a_p):
    rep = np.ones((1, GPT))
    c16 = jnp.asarray(np.kron(rep, np.eye(S5_GROUP)), F32).astype(BF16)
    c64 = jnp.asarray(np.kron(rep, np.eye(S5_STATE)), F32).astype(BF16)
    c_m = jnp.asarray(np.kron(np.eye(CH), np.kron(rep, np.eye(S5_GROUP))), F32).astype(BF16)
    b_c, a_q, a_p = b_c.astype(BF16), a_q.astype(BF16), a_p.astype(BF16)
    half = GPT * S5_STATE
    return pl.pallas_call(
        _s5w_kernel,
        grid=(NJ,),
        in_specs=[pl.BlockSpec((1, 2 * CH, LANES, S5_GROUP), lambda j: (j, 0, 0, 0)),
                  pl.BlockSpec((4, 1, CL, S5_STATE), lambda j: (0, j, 0, 0)),
                  pl.BlockSpec((4, 1, half, CH * S5_GROUP), lambda j: (0, j, 0, 0)),
                  pl.BlockSpec(c16.shape, lambda j: (0, 0)),
                  pl.BlockSpec(c64.shape, lambda j: (0, 0)),
                  pl.BlockSpec(c_m.shape, lambda j: (0, 0))],
        out_specs=(pl.BlockSpec((1, CL, CL), lambda j: (j, 0, 0)),
                   pl.BlockSpec((1, CL, SW), lambda j: (j, 0, 0)),
                   pl.BlockSpec((1, SW, CL), lambda j: (j, 0, 0))),
        out_shape=(jax.ShapeDtypeStruct((NJ, CL, CL), BF16),
                   jax.ShapeDtypeStruct((NJ, CL, SW), BF16),
                   jax.ShapeDtypeStruct((NJ, SW, CL), BF16)),
        compiler_params=_cparams(("parallel",)),
        name="s5w",
    )(b_c, a_q, a_p, c16, c64, c_m)


def _s5_kernel(p_ref, pc_ref, wm_ref, wq_ref, wp_ref, tr_ref, cw_ref, y_ref, v_ref):
    nq = NJ
    half = GPT * S5_STATE

    def chunk_rows(ref, r0, nrows):
        return jnp.concatenate([ref[t, pl.ds(r0, nrows), :] for t in range(CH)], axis=-1)

    def fill(k, c):
        r0 = pl.multiple_of(k * SEG, SEG)
        v = jnp.dot(chunk_rows(p_ref, r0, SEG), wq_ref[0], preferred_element_type=F32)
        for s in range(4 * nq):
            v_ref[s, pl.ds(k, SEG, stride=NSEG), :] = v[:, s * LANES:(s + 1) * LANES]
        return c
    lax.fori_loop(0, NSEG, fill, 0)

    vc = jnp.dot(chunk_rows(pc_ref, 0, N_CHUNK_CTX), wq_ref[0], preferred_element_type=F32)
    vfr, vfi, vbr, vbi = (vc[:, i * half:(i + 1) * half] for i in range(4))
    wfr, wfi, wbr, wbi = (cw_ref[0, i] for i in range(4))
    s0_fr = jnp.sum(wfr * vfr - wfi * vfi, axis=0, keepdims=True)
    s0_fi = jnp.sum(wfr * vfi + wfi * vfr, axis=0, keepdims=True)
    s0_br = jnp.sum(wbr * vbr - wbi * vbi, axis=0, keepdims=True)
    s0_bi = jnp.sum(wbr * vbi + wbi * vbr, axis=0, keepdims=True)

    tr = tr_ref[0]
    afr, afi, abr, abi = (jnp.broadcast_to(tr[i:i + 1], (NSEG, half)) for i in range(4))
    gfr, gfi, gbr, gbi = (tr[i:i + 1] for i in range(4, 8))

    def load_part(part, i):
        return jnp.concatenate(
            [v_ref[part * nq + q, pl.ds(pl.multiple_of(i * NSEG, NSEG), NSEG), :] for q in range(nq)],
            axis=-1)

    def store_part(part, i, val):
        for q in range(nq):
            v_ref[part * nq + q, pl.ds(pl.multiple_of(i * NSEG, NSEG), NSEG), :] = (
                val[:, q * LANES:(q + 1) * LANES])

    def step(i, carry, write):
        fr, fi, br, bi = carry
        ib = SEG - 1 - i
        ufr, ufi = load_part(0, i), load_part(1, i)
        ubr, ubi = load_part(2, ib), load_part(3, ib)
        if write:
            store_part(0, i, fr)
            store_part(1, i, fi)
            store_part(2, ib, br)
            store_part(3, ib, bi)
        return (afr * fr - afi * fi + ufr, afr * fi + afi * fr + ufi,
                abr * br - abi * bi + ubr, abr * bi + abi * br + ubi)

    zero = jnp.zeros((NSEG, half), F32)
    ffr, ffi, fbr, fbi = lax.fori_loop(0, SEG, functools.partial(step, write=False),
                                       (zero, zero, zero, zero), unroll=SCAN_UNROLL)

    rows_fr, rows_fi = [s0_fr], [s0_fi]
    for k in range(1, NSEG):
        pr, pi = rows_fr[-1], rows_fi[-1]
        rows_fr.append(gfr * pr - gfi * pi + ffr[k - 1:k])
        rows_fi.append(gfr * pi + gfi * pr + ffi[k - 1:k])
    rows_br, rows_bi = [s0_br], [s0_bi]
    for k in range(NSEG - 2, -1, -1):
        pr, pi = rows_br[0], rows_bi[0]
        rows_br.insert(0, gbr * pr - gbi * pi + fbr[k + 1:k + 2])
        rows_bi.insert(0, gbr * pi + gbi * pr + fbi[k + 1:k + 2])
    init = tuple(jnp.concatenate(r, axis=0) for r in (rows_fr, rows_fi, rows_br, rows_bi))

    lax.fori_loop(0, SEG, functools.partial(step, write=True), init, unroll=SCAN_UNROLL)

    def emit(k, c):
        r0 = pl.multiple_of(k * SEG, SEG)
        b = chunk_rows(p_ref, r0, SEG)
        sin = jnp.concatenate([v_ref[s, pl.ds(k, SEG, stride=NSEG), :] for s in range(4 * nq)], axis=-1)
        y = (jnp.dot(b, wm_ref[0], preferred_element_type=F32)
             + jnp.dot(sin.astype(BF16), wp_ref[0], preferred_element_type=F32))
        for t in range(CH):
            y_ref[t, pl.ds(r0, SEG), :] = y[:, t * LANES:(t + 1) * LANES].astype(y_ref.dtype)
        return c
    lax.fori_loop(0, NSEG, emit, 0)


def _s5(p_t, pc_t, w_m, w_q, w_p, trans, ctx_w):
    one = pl.Buffered(1)
    return pl.pallas_call(
        _s5_kernel,
        grid=(NJ,),
        in_specs=[pl.BlockSpec((CH, N_CHUNK, LANES), lambda j: (0, 0, j)),
                  pl.BlockSpec((CH, N_CHUNK_CTX, LANES), lambda j: (0, 0, j)),
                  pl.BlockSpec((1, CL, CL), lambda j: (j, 0, 0), pipeline_mode=one),
                  pl.BlockSpec((1, CL, SW), lambda j: (j, 0, 0), pipeline_mode=one),
                  pl.BlockSpec((1, SW, CL), lambda j: (j, 0, 0), pipeline_mode=one),
                  pl.BlockSpec((1, SUBLANES, GPT * S5_STATE), lambda j: (j, 0, 0)),
                  pl.BlockSpec((1, 4, N_CHUNK_CTX, GPT * S5_STATE), lambda j: (j, 0, 0, 0))],
        out_specs=pl.BlockSpec((CH, N_CHUNK, LANES), lambda j: (0, 0, j)),
        out_shape=jax.ShapeDtypeStruct((CH, N_CHUNK, S5_WIDTH), BF16),
        scratch_shapes=[pltpu.VMEM((4 * NJ, N_CHUNK, LANES), F32)],
        compiler_params=_cparams(("parallel",)),
        name="s5",
    )(p_t, pc_t, w_m, w_q, w_p, trans, ctx_w)


def _dft_tables():
    n = np.arange(FN)
    ang = 2.0 * np.pi * np.outer(n, n) / FN
    c, s = np.cos(ang), np.sin(ang)
    st1 = np.block([[c, s], [-s, c]])
    tw = 2.0 * np.pi * np.outer(n, n) / (FN * FN)
    wr, wi = np.cos(tw), -np.sin(tw)
    fr = c[None] * wr[:, None, :] + s[None] * wi[:, None, :]
    fi = c[None] * wi[:, None, :] - s[None] * wr[:, None, :]
    st2 = np.concatenate([fr, -fi], axis=-1)
    scale = 1.0 / math.sqrt(N_TOK * FFT_DIM)
    blk_c = np.kron(np.eye(FFT_GROUPS), c) * scale
    blk_s = np.kron(np.eye(FFT_GROUPS), s) * scale
    fc = np.concatenate([blk_c, -blk_s], axis=1)
    return (jnp.asarray(st1, F32).astype(BF16), jnp.asarray(st2, F32).astype(BF16), jnp.asarray(fc, F32))


FSL = FFT_WIDTH // LANES


FBH = FB // SUBLANES


def _block_to_slabs(blk, slab_ref, first, per_half):
    for bh in range(FBH):
        val = blk[:, bh * SUBLANES:(bh + 1) * SUBLANES, :].reshape(FN * SUBLANES, FFT_WIDTH)
        for s in range(FSL):
            slab_ref[bh * per_half + first + s] = val[:, s * LANES:(s + 1) * LANES]


def _slab_rows(b, first, per_half):
    return (b // SUBLANES) * per_half + first, pl.ds(b % SUBLANES, FN, stride=SUBLANES)


def _slabs_to_block(slab_ref, first, per_half):
    halves = []
    for bh in range(FBH):
        val = jnp.concatenate([slab_ref[bh * per_half + first + s] for s in range(FSL)], axis=-1)
        halves.append(val.reshape(FN, SUBLANES, FFT_WIDTH))
    return jnp.concatenate(halves, axis=1)


def _fft1_kernel(xr_ref, xi_ref, f_ref, yr_ref, yi_ref, in_ref, out_ref):
    _block_to_slabs(xr_ref[...].astype(F32), in_ref, 0, 2 * FSL)
    _block_to_slabs(xi_ref[...].astype(F32), in_ref, FSL, 2 * FSL)
    for b in range(FB):
        def part(first):
            base, rows = _slab_rows(b, first, 2 * FSL)
            return jnp.concatenate([in_ref[base + s, rows, :] for s in range(FSL)], axis=-1)
        xs = jnp.concatenate([part(0), part(FSL)], axis=0).astype(BF16)
        y = jnp.dot(f_ref[...], xs, preferred_element_type=F32)
        base, rows = _slab_rows(b, 0, 2 * FSL)
        for s in range(FSL):
            out_ref[base + s, rows, :] = y[:FN, s * LANES:(s + 1) * LANES]
            out_ref[base + FSL + s, rows, :] = y[FN:, s * LANES:(s + 1) * LANES]
    yr_ref[...] = _slabs_to_block(out_ref, 0, 2 * FSL).astype(BF16)
    yi_ref[...] = _slabs_to_block(out_ref, FSL, 2 * FSL).astype(BF16)


def _fft1(xr, xi, st1):
    spec = pl.BlockSpec((FN, FB, FFT_WIDTH), lambda i: (0, i, 0))
    slabs = pltpu.VMEM((FBH * 2 * FSL, FN * SUBLANES, LANES), F32)
    return pl.pallas_call(
        _fft1_kernel,
        grid=(FN // FB,),
        in_specs=[spec, spec, pl.BlockSpec((2 * FN, 2 * FN), lambda i: (0, 0))],
        out_specs=(spec, spec),
        out_shape=(jax.ShapeDtypeStruct((FN, FN, FFT_WIDTH), BF16),) * 2,
        scratch_shapes=[slabs, slabs],
        compiler_params=_cparams(("parallel",)),
        name="fft1",
    )(xr.reshape(FN, FN, FFT_WIDTH), xi.reshape(FN, FN, FFT_WIDTH), st1)


def _fft2_kernel(yr_ref, yi_ref, f_ref, z_ref, out_ref):
    for b in range(FB):
        ys = jnp.concatenate([yr_ref[b * FN:(b + 1) * FN, :], yi_ref[b * FN:(b + 1) * FN, :]], axis=0)
        z = jnp.dot(f_ref[b], ys, preferred_element_type=F32)
        base, rows = _slab_rows(b, 0, FSL)
        for s in range(FSL):
            out_ref[base + s, rows, :] = z[:, s * LANES:(s + 1) * LANES]
    z_ref[...] = _slabs_to_block(out_ref, 0, FSL).astype(BF16)


def _fft2(yr, yi, st2):
    rows = pl.BlockSpec((FB * FN, FFT_WIDTH), lambda i: (i, 0))
    z = pl.pallas_call(
        _fft2_kernel,
        grid=(FN // FB,),
        in_specs=[rows, rows, pl.BlockSpec((FB, FN, 2 * FN), lambda i: (i, 0, 0))],
        out_specs=pl.BlockSpec((FN, FB, FFT_WIDTH), lambda i: (0, i, 0)),
        out_shape=jax.ShapeDtypeStruct((FN, FN, FFT_WIDTH), BF16),
        scratch_shapes=[pltpu.VMEM((FBH * FSL, FN * SUBLANES, LANES), F32)],
        compiler_params=_cparams(("parallel",)),
        name="fft2",
    )(yr.reshape(N_TOK, FFT_WIDTH), yi.reshape(N_TOK, FFT_WIDTH), st2)
    return z.reshape(N_TOK, FFT_WIDTH)


def _gelu_tanh(x):
    return 0.5 * x * (1.0 + jnp.tanh(math.sqrt(2.0 / math.pi) * (x + 0.044715 * (x * x * x))))


def _mix_kernel(x_ref, er_ref, ec_ref, lg_ref, lb_ref, m1_ref, s1_ref, wgs_ref, wgf_ref, bg_ref,
                yt_ref, zr_ref, wglu_ref, bglu_ref, wbs_ref, wbf_ref, bbf_ref, wo_ref, bo_ref,
                g1_ref, l1g_ref, l1b_ref, m2_ref, s2_ref, wr_ref, br_ref, tri_ref, etri_ref,
                h1_ref, u2_ref, pos_ref, gate_ref, cnt_ref, scr_ref):
    pos = _pos_code(er_ref, ec_ref, TM)

    def front(r0, nr):
        rows = slice(r0, r0 + nr)
        h = _layer_norm(x_ref[rows, :] + pos[rows, :], lg_ref[...], lb_ref[...])
        u = (h * m1_ref[...] + s1_ref[...]).astype(BF16)

        c0, nc = r0 // CH, nr // CH
        for t in range(CH):
            for j in range(NJ):
                scr_ref[j, pl.ds(r0 + t, nc, stride=CH), :] = (
                    yt_ref[t, c0:c0 + nc, j * LANES:(j + 1) * LANES].astype(F32))
        ys = jnp.concatenate([scr_ref[j, rows, :] for j in range(NJ)], axis=-1)
        z = jnp.dot(_gelu_tanh(ys).astype(BF16), wglu_ref[...], preferred_element_type=F32) + bglu_ref[...]
        glu = (z[:, :S5_WIDTH] * _sigmoid(z[:, S5_WIDTH:])).astype(BF16)
        g_s5 = _sigmoid(jnp.dot(u, wgs_ref[...], preferred_element_type=F32) + bg_ref[:, :D])
        g_fft = _sigmoid(jnp.dot(u, wgf_ref[...], preferred_element_type=F32) + bg_ref[:, D:])
        y_s5 = jnp.dot(glu, wbs_ref[...], preferred_element_type=F32)
        y_fft = jnp.dot(zr_ref[rows, :], wbf_ref[...], preferred_element_type=F32) + bbf_ref[...]
        mixed = (g_s5 * y_s5 + g_fft * y_fft).astype(BF16)
        y = jnp.dot(mixed, wo_ref[...], preferred_element_type=F32) + bo_ref[...]
        h1 = _layer_norm(ALPHA * h + g1_ref[...] * y, l1g_ref[...], l1b_ref[...])
        h1_ref[rows, :] = h1
        u2 = h1 * m2_ref[...] + s2_ref[...]
        u2_ref[rows, :] = u2.astype(BF16)
        u_hi = u2.astype(BF16)
        u_lo = (u2 - u_hi.astype(F32)).astype(BF16)

        def nt(a, b):
            return lax.dot_general(a, b, (((1,), (1,)), ((), ())), preferred_element_type=F32)
        return nt(wr_ref[0], u_hi) + nt(wr_ref[0], u_lo) + nt(wr_ref[1], u_hi)

    logits = front(0, TM) + br_ref[:, 0:1]
    eidx = lax.broadcasted_iota(jnp.int32, (N_EXPERTS, TM), 0)
    vals, hots = [], []
    cur = logits
    for _k in range(TOP_K):
        m = jnp.max(cur, axis=0, keepdims=True)
        sel = jnp.min(jnp.where(cur == m, eidx, N_EXPERTS), axis=0, keepdims=True)
        hot = eidx == sel
        cur = jnp.where(hot, -jnp.inf, cur)
        vals.append(m)
        hots.append(hot)
    exps = [jnp.exp(v - vals[0]) for v in vals]
    den = exps[0] + exps[1] + exps[2] + exps[3]
    gate4 = jnp.concatenate([e / den for e in exps], axis=0)

    hot_sum = (hots[0] | hots[1] | hots[2] | hots[3]).astype(F32)
    before = jnp.dot(hot_sum.astype(BF16), tri_ref[...], preferred_element_type=F32)
    cnt = jnp.broadcast_to(jnp.sum(hot_sum, axis=1, keepdims=True), (N_EXPERTS, LANES))
    cnt8 = jnp.floor((cnt + (SEG_ALIGN - 1)) * (1.0 / SEG_ALIGN)) * SEG_ALIGN
    seg0 = jnp.dot(etri_ref[...], cnt8.astype(BF16), preferred_element_type=F32)
    tot = seg0[:, 0:1] + before
    pos4 = jnp.concatenate(
        [jnp.sum(jnp.where(hk, tot, 0.0), axis=0, keepdims=True) for hk in hots], axis=0)
    pos_ref[...] = pos4.astype(jnp.int32)
    cnt_ref[0] = cnt

    gpad = jnp.concatenate([gate4, pos4, jnp.zeros((LANES - 2 * TOP_K, TM), F32)], axis=0)
    gate_ref[...] = gpad.T


def _mix(x, emb_r, emb_c, lg, lb, m1, s1, w_in, bg, y_t, zr, wglu, bglu, wbs, wbf, bbf, wo, bo,
         g1, l1g, l1b, m2, s2, wr_t, br, tri, etri):
    gate_cols = (S5_WIDTH + FFT_WIDTH) // D
    vec = pl.BlockSpec((1, D), lambda i: (0, 0))

    def full(a):
        return pl.BlockSpec(a.shape, lambda i: (0,) * a.ndim)
    return pl.pallas_call(
        _mix_kernel,
        grid=(N_TOK // TM,),
        in_specs=[pl.BlockSpec((TM, D), lambda i: (i, 0)),
                  pl.BlockSpec((TM // GRID_W, D // 2), lambda i: (i, 0)),
                  pl.BlockSpec((GRID_W, D // 2), lambda i: (0, 0)),
                  vec, vec, vec, vec,
                  pl.BlockSpec((D, D), lambda i: (0, gate_cols)),
                  pl.BlockSpec((D, D), lambda i: (0, gate_cols + 1)), full(bg),
                  pl.BlockSpec((CH, TM // CH, S5_WIDTH), lambda i: (0, i, 0)),
                  pl.BlockSpec((TM, FFT_WIDTH), lambda i: (i, 0)),
                  full(wglu), full(bglu), full(wbs), full(wbf), full(bbf), full(wo), full(bo),
                  vec, vec, vec, vec, vec, full(wr_t), full(br), full(tri), full(etri)],
        out_specs=(pl.BlockSpec((TM, D), lambda i: (i, 0)),
                   pl.BlockSpec((TM, D), lambda i: (i, 0)),
                   pl.BlockSpec((TOP_K, TM), lambda i: (0, i)),
                   pl.BlockSpec((TM, LANES), lambda i: (i, 0)),
                   pl.BlockSpec((1, N_EXPERTS, LANES), lambda i: (i, 0, 0))),
        out_shape=(jax.ShapeDtypeStruct((N_TOK, D), F32),
                   jax.ShapeDtypeStruct((N_TOK, D), BF16),
                   jax.ShapeDtypeStruct((TOP_K, N_TOK), jnp.int32),
                   jax.ShapeDtypeStruct((N_TOK, LANES), F32),
                   jax.ShapeDtypeStruct((N_TILES, N_EXPERTS, LANES), F32)),
        scratch_shapes=[pltpu.VMEM((NJ, TM, LANES), F32)],
        compiler_params=_cparams(("parallel",)),
        name="mix",
    )(x, emb_r, emb_c, lg, lb, m1, s1, w_in, w_in, bg, y_t, zr, wglu, bglu, wbs, wbf, bbf, wo, bo,
      g1, l1g, l1b, m2, s2, wr_t, br, tri, etri)


def _on_parity(i, fn):
    @pl.when(i % 2 == 0)
    def _():
        fn(0)

    @pl.when(i % 2 == 1)
    def _():
        fn(1)


def _dispatch_kernel(pend_ref, fill_ref, dprev_ref, dest_ref, pos_ref, u_ref, buf_ref,
                     sorted_ref, zero_ref, zsem, tsem, sems):
    i = pl.program_id(0)

    units = BM // ZB

    def clear_copy(start, sem):
        return pltpu.make_async_copy(
            zero_ref, buf_ref.at[pl.ds(pl.multiple_of(start, ZB), ZB)], sem)

    def unused_units(fn):
        def unit(b, c):
            fn(clear_copy(b * ZB, tsem))
            return c
        lax.fori_loop(pend_ref[N_EXPERTS - 1] // ZB, N_BLOCKS * units, unit, 0)

    @pl.when(i == 0)
    def _():
        zero_ref[...] = jnp.zeros_like(zero_ref)

        def each(fn):
            def expert(e, c):
                for z in range(units):
                    start = pend_ref[e] - (z + 1) * ZB

                    @pl.when(start + ZB > fill_ref[e])
                    def _():
                        fn(clear_copy(start, zsem))
                return c
            lax.fori_loop(0, N_EXPERTS, expert, 0)
            for b in range(N_BLOCKS * units, N_BLOCKS_ALL * units):
                fn(clear_copy(b * ZB, zsem))
        each(lambda cp: cp.start())
        unused_units(lambda cp: cp.start())
        each(lambda cp: cp.wait())

        sorted_ref[...] = jnp.zeros_like(sorted_ref)

    def chunk_copy(slot, table_ref, j):
        dst = pl.multiple_of(table_ref[0, 0, j], SEG_ALIGN)
        return pltpu.make_async_copy(sorted_ref.at[slot, pl.ds(j * SEG_ALIGN, SEG_ALIGN)],
                                     buf_ref.at[pl.ds(dst, SEG_ALIGN)], sems.at[slot])

    def drain(slot):
        pltpu.make_async_copy(sorted_ref.at[slot], buf_ref.at[pl.ds(0, CAP)], sems.at[slot]).wait()

    def run(slot):
        pos = pos_ref[...]
        u = u_ref[...]
        n_rb = CAP // CAP_BLOCK
        per_rb = NCHK // (n_rb // 2)
        for rb in range(n_rb):
            for j in range(rb * per_rb, min((rb + 1) * per_rb, NCHK)):
                chunk_copy(1 - slot, dprev_ref, j).start()
            rows = lax.broadcasted_iota(jnp.int32, (CAP_BLOCK, TM), 0) + rb * CAP_BLOCK
            hit = rows == pos[0:1]
            for k in range(1, TOP_K):
                hit = hit | (rows == pos[k:k + 1])
            onehot = jnp.where(hit, 1.0, 0.0).astype(BF16)
            sorted_ref[slot, rb * CAP_BLOCK:(rb + 1) * CAP_BLOCK, :] = jnp.dot(
                onehot, u, preferred_element_type=F32).astype(BF16)
        drain(1 - slot)

        @pl.when(i == N_TILES - 1)
        def _():
            def issue(j, c):
                chunk_copy(slot, dest_ref, j).start()
                return c
            lax.fori_loop(0, NCHK, issue, 0)
            drain(slot)
            unused_units(lambda cp: cp.wait())
    _on_parity(i, run)


def _dispatch(pad_ends, padded, chunk_table, pos_t, u2):
    return pl.pallas_call(
        _dispatch_kernel,
        grid_spec=pltpu.PrefetchScalarGridSpec(
            num_scalar_prefetch=2,
            grid=(N_TILES,),
            in_specs=[pl.BlockSpec((1, 1, NCHK), lambda i, a, b: (i, 0, 0), memory_space=pltpu.SMEM),
                      pl.BlockSpec((1, 1, NCHK), lambda i, a, b: (i + 1, 0, 0), memory_space=pltpu.SMEM),
                      pl.BlockSpec((TOP_K, TM), lambda i, a, b: (0, i)),
                      pl.BlockSpec((TM, D), lambda i, a, b: (i, 0))],
            out_specs=pl.BlockSpec(memory_space=pl.ANY),
            scratch_shapes=[pltpu.VMEM((2, CAP, D), BF16),
                            pltpu.VMEM((ZB, D), BF16),
                            pltpu.SemaphoreType.DMA(()),
                            pltpu.SemaphoreType.DMA(()),
                            pltpu.SemaphoreType.DMA((2,))]),
        out_shape=jax.ShapeDtypeStruct((ROWS_ALL, D), BF16),
        compiler_params=_cparams(("arbitrary",)),
        name="dispatch",
    )(pad_ends, padded, chunk_table, chunk_table, pos_t, u2)


def _ffn_kernel(be_ref, nu_ref, run_ref, nxt_ref, valid_ref, x_ref, wu_hbm, bu_ref, wd_hbm, bd_ref,
                y_ref, wu_ref, wd_ref, wub_ref, wdb_ref, sems):
    i = pl.program_id(0)
    used = i < nu_ref[0]

    def weight_copies(e, slot):
        return (pltpu.make_async_copy(wu_hbm.at[e], wu_ref.at[slot], sems.at[slot]),
                pltpu.make_async_copy(wd_hbm.at[e], wd_ref.at[slot], sems.at[slot]))

    @pl.when(used)
    def _():
        run = run_ref[i]

        @pl.when(run >= 0)
        def _():
            def open_run(slot):
                @pl.when(run == 0)
                def _():
                    for cp in weight_copies(be_ref[i], slot):
                        cp.start()

                @pl.when(nxt_ref[i] >= 0)
                def _():
                    for cp in weight_copies(nxt_ref[i], 1 - slot):
                        cp.start()
                for cp in weight_copies(be_ref[i], slot):
                    cp.wait()
                wub_ref[...] = wu_ref[slot].astype(BF16)
                wdb_ref[...] = wd_ref[slot].astype(BF16)
            _on_parity(run, open_run)

        def expert_rows(r0, nr):
            rows = slice(r0, r0 + nr)
            e = be_ref[i]
            h = (jnp.dot(x_ref[rows, :], wub_ref[...], preferred_element_type=F32)
                 + bu_ref[pl.ds(e, 1), :])
            h_glu = jnp.minimum(h[:, :D], SWIGLU_LIMIT)
            h_lin = jnp.clip(h[:, D:], -SWIGLU_LIMIT, SWIGLU_LIMIT)
            act = (h_glu * _sigmoid(SWIGLU_ALPHA * h_glu) * (h_lin + 1.0)).astype(BF16)
            y_ref[rows, :] = (jnp.dot(act, wdb_ref[...], preferred_element_type=F32)
                              + bd_ref[pl.ds(e, 1), :]).astype(BF16)

        valid = valid_ref[i]

        @pl.when(valid == BM)
        def _():
            expert_rows(0, BM)

        @pl.when(valid < BM)
        def _():
            for h0 in range(0, BM, FFN_HALF):
                @pl.when(valid >= h0 + FFN_HALF)
                def _(h0=h0):
                    expert_rows(h0, FFN_HALF)

                @pl.when(valid < h0 + FFN_HALF)
                def _(h0=h0):
                    for r0 in range(h0, h0 + FFN_HALF, FFN_TAIL):
                        @pl.when(r0 < valid)
                        def _(r0=r0):
                            expert_rows(r0, FFN_TAIL)

                        @pl.when(r0 >= valid)
                        def _(r0=r0):
                            y_ref[r0:r0 + FFN_TAIL, :] = jnp.zeros((FFN_TAIL, D), BF16)

    @pl.when(jnp.logical_not(used))
    def _():
        y_ref[...] = jnp.zeros_like(y_ref)


def _ffn(block_expert, n_used, run_id, next_expert, valid, buf, w_up, b_up, w_down, b_down):
    def blk(i, be, nu, *_):
        return jnp.minimum(i, nu[0] - 1)
    return pl.pallas_call(
        _ffn_kernel,
        grid_spec=pltpu.PrefetchScalarGridSpec(
            num_scalar_prefetch=5,
            grid=(N_BLOCKS_ALL,),
            in_specs=[pl.BlockSpec((BM, D), lambda i, *s: (blk(i, *s), 0)),
                      pl.BlockSpec(memory_space=pl.ANY),
                      pl.BlockSpec((N_EXPERTS, 2 * D), lambda i, *s: (0, 0)),
                      pl.BlockSpec(memory_space=pl.ANY),
                      pl.BlockSpec((N_EXPERTS, D), lambda i, *s: (0, 0))],
            out_specs=pl.BlockSpec((BM, D), lambda i, *s: (i, 0)),
            scratch_shapes=[pltpu.VMEM((2, D, 2 * D), F32),
                            pltpu.VMEM((2, D, D), F32),
                            pltpu.VMEM((D, 2 * D), BF16),
                            pltpu.VMEM((D, D), BF16),
                            pltpu.SemaphoreType.DMA((2,))]),
        out_shape=jax.ShapeDtypeStruct((ROWS_ALL, D), BF16),
        compiler_params=_cparams(("arbitrary",)),
        name="ffn",
    )(block_expert, n_used, run_id, next_expert, valid, buf, w_up, b_up, w_down, b_down)


def _combine_kernel(dest_ref, dnext_ref, y_ref, h1_ref, gate_ref, g2_ref, lg_ref, lb_ref,
                    o_ref, sorted_ref, sems):
    i = pl.program_id(0)

    def chunk_copy(slot, table_ref, j):
        src = pl.multiple_of(table_ref[0, 0, j], SEG_ALIGN)
        return pltpu.make_async_copy(y_ref.at[pl.ds(src, SEG_ALIGN)],
                                     sorted_ref.at[slot, pl.ds(j * SEG_ALIGN, SEG_ALIGN)],
                                     sems.at[slot])

    def drain(slot):
        pltpu.make_async_copy(y_ref.at[pl.ds(0, CAP)], sorted_ref.at[slot], sems.at[slot]).wait()

    @pl.when(i == 0)
    def _():
        def issue(j, c):
            chunk_copy(0, dest_ref, j).start()
            return c
        lax.fori_loop(0, NCHK, issue, 0)

    def run(slot):
        drain(slot)

        gp = gate_ref[...]
        m = jnp.zeros((TM, D), F32)
        n_cb = CAP // CAP_BLOCK
        per_cb = NCHK // (n_cb // 2)
        for cb in range(n_cb):
            for j in range(cb * per_cb, min((cb + 1) * per_cb, NCHK)):
                chunk_copy(1 - slot, dnext_ref, j).start()
            cols = (lax.broadcasted_iota(jnp.int32, (TM, CAP_BLOCK), 1) + cb * CAP_BLOCK).astype(F32)
            g = jnp.where(cols == gp[:, TOP_K:TOP_K + 1], gp[:, 0:1], 0.0)
            for k in range(1, TOP_K):
                g = g + jnp.where(cols == gp[:, TOP_K + k:TOP_K + k + 1], gp[:, k:k + 1], 0.0)
            rows = sorted_ref[slot, cb * CAP_BLOCK:(cb + 1) * CAP_BLOCK, :]
            m = m + jnp.dot(g.astype(BF16), rows, preferred_element_type=F32)
        o_ref[...] = _layer_norm(ALPHA * h1_ref[...] + g2_ref[...] * m, lg_ref[...], lb_ref[...])

        @pl.when(i == N_TILES - 1)
        def _():
            drain(1 - slot)
    _on_parity(i, run)


def _combine(chunk_table, y_buf, h1, gate_tok, g2, lg, lb):
    vec = pl.BlockSpec((1, D), lambda i: (0, 0))
    return pl.pallas_call(
        _combine_kernel,
        grid_spec=pltpu.PrefetchScalarGridSpec(
            num_scalar_prefetch=0,
            grid=(N_TILES,),
            in_specs=[pl.BlockSpec((1, 1, NCHK), lambda i: (i + 1, 0, 0), memory_space=pltpu.SMEM),
                      pl.BlockSpec((1, 1, NCHK), lambda i: (i + 2, 0, 0), memory_space=pltpu.SMEM),
                      pl.BlockSpec(memory_space=pl.ANY),
                      pl.BlockSpec((TM, D), lambda i: (i, 0)),
                      pl.BlockSpec((TM, LANES), lambda i: (i, 0)),
                      vec, vec, vec],
            out_specs=pl.BlockSpec((TM, D), lambda i: (i, 0)),
            scratch_shapes=[pltpu.VMEM((2, CAP, D), BF16),
                            pltpu.SemaphoreType.DMA((2,))]),
        out_shape=jax.ShapeDtypeStruct((N_TOK, D), F32),
        compiler_params=_cparams(("arbitrary",)),
        name="combine",
    )(chunk_table, chunk_table, y_buf, h1, gate_tok, g2, lg, lb)


def _sincos_tables():
    q = D // 4
    omega = 1.0 / (10000.0 ** (jnp.arange(q, dtype=F32) / q))

    def emb(n):
        ang = jnp.arange(n, dtype=F32)[:, None] * omega[None, :]
        return jnp.concatenate([jnp.sin(ang), jnp.cos(ang)], axis=-1)
    return emb(N_TOK // GRID_W), emb(GRID_W)


def kernel(x, c, ctx, c_ctx, ln_in_g, ln_in_b, w_ada, b_ada, w_in, b_in, s5_lambda_re, s5_lambda_im, s5_log_dt, s5_b_re, s5_b_im, s5_c_re, s5_c_im, s5_d, w_glu, b_glu, w_br_s5, w_br_fft, b_br_fft, w_out, b_out, ln1_g, ln1_b, w_router, b_router, w_up, b_up, w_down, b_down, ln2_g, ln2_b):
    assert x.shape == (1, N_TOK, D) and ctx.shape == (1, N_CTX, D) and w_ada.shape[0] == 1
    row = lambda v: v.reshape(1, -1).astype(F32)

    cc = jnp.concatenate([c.reshape(1, D), c_ctx.reshape(1, D), jnp.zeros((SUBLANES - 2, D), F32)], axis=0)
    ada = _ada(cc, w_ada[0], row(b_ada[0]))
    sh1, sc1, g1, sh2, sc2, g2 = (ada[0:1, k * D:(k + 1) * D] for k in range(6))
    sh1c, sc1c = ada[1:2, 0:D], ada[1:2, D:2 * D]

    emb_r, emb_c = _sincos_tables()
    st1, st2, fc = _dft_tables()
    lg, lb = row(ln_in_g), row(ln_in_b)

    w_in_bf = w_in[0].astype(BF16)
    b_s5 = row(b_in[0][:S5_WIDTH])
    b_fft8 = jnp.concatenate([row(b_in[0][S5_WIDTH:S5_WIDTH + FFT_WIDTH]),
                              jnp.zeros((SUBLANES - 1, FFT_WIDTH), F32)], axis=0)
    b_g = row(b_in[0][S5_WIDTH + FFT_WIDTH:])
    w_fc, b_fc = _fft_weights(w_in[0], b_fft8, fc)
    bcat = jnp.concatenate([b_s5, b_fc[0:1]], axis=1)

    x2 = x[0]
    p_t, xr, xi = _proj(x2, emb_r, emb_c, lg, lb, 1.0 + sc1, sh1, w_in_bf, w_fc, bcat)
    pc_t = _ctx_proj(ctx[0], lg, lb, 1.0 + sc1c, sh1c, w_in_bf[:, :S5_WIDTH], b_s5)

    b_c, a_q, a_p, trans, ctx_w = _s5_tables(
        s5_lambda_re[0], s5_lambda_im[0], s5_log_dt[0], s5_b_re[0], s5_b_im[0],
        s5_c_re[0], s5_c_im[0], s5_d[0])
    w_m, w_q, w_p = _s5_expand(b_c, a_q, a_p)
    y_t = _s5(p_t, pc_t, w_m, w_q, w_p, trans, ctx_w)

    yr, yi = _fft1(xr, xi, st1)
    zr = _fft2(yr, yi, st2)

    tri = (jnp.arange(TM)[:, None] < jnp.arange(TM)[None, :]).astype(BF16)
    br = jnp.broadcast_to(b_router[0].reshape(N_EXPERTS, 1), (N_EXPERTS, LANES))
    etri = (jnp.arange(N_EXPERTS)[:, None] > jnp.arange(N_EXPERTS)[None, :]).astype(BF16)
    wr_t = jnp.transpose(w_router[0])
    wr_hi = wr_t.astype(BF16)
    wr_split = jnp.stack([wr_hi, (wr_t - wr_hi.astype(F32)).astype(BF16)], axis=0)
    h1, u2, pos_t, gate_tok, counts = _mix(
        x2, emb_r, emb_c, lg, lb, 1.0 + sc1, sh1, w_in_bf, b_g, y_t, zr,
        w_glu[0].astype(BF16), row(b_glu[0]), w_br_s5[0].astype(BF16), w_br_fft[0].astype(BF16),
        row(b_br_fft[0]), w_out[0].astype(BF16), row(b_out[0]), g1, row(ln1_g[0]), row(ln1_b[0]),
        1.0 + sc2, sh2, wr_split, br, tri, etri)

    cnt = counts[:, :, 0].astype(jnp.int32)
    seg = (cnt + SEG_ALIGN - 1) // SEG_ALIGN * SEG_ALIGN
    seg_end = jnp.cumsum(seg, axis=1)
    seg_start = seg_end - seg
    padded = (jnp.sum(seg, axis=0) + BM - 1) // BM * BM
    pad_ends = jnp.cumsum(padded)
    seg_dest = (pad_ends - padded)[None, :] + jnp.cumsum(seg, axis=0) - seg
    chunk_row = jnp.arange(NCHK, dtype=jnp.int32) * SEG_ALIGN
    chunk_exp = jnp.minimum(jnp.sum(chunk_row[None, :, None] >= seg_end[:, None, :], axis=-1),
                            N_EXPERTS - 1)
    own = chunk_exp[:, :, None] == jnp.arange(N_EXPERTS, dtype=jnp.int32)[None, None, :]
    chunk_dest = (jnp.sum(jnp.where(own, (seg_dest - seg_start)[:, None, :], 0), axis=-1)
                  + chunk_row[None, :]).astype(jnp.int32).reshape(N_TILES, 1, NCHK)
    nchk = (seg_end[:, -1] // SEG_ALIGN).astype(jnp.int32)
    block_start = jnp.arange(N_BLOCKS_ALL, dtype=jnp.int32) * BM
    block_expert = jnp.minimum(jnp.sum(block_start[:, None] >= pad_ends[None, :], axis=1),
                               N_EXPERTS - 1).astype(jnp.int32)
    n_used = (pad_ends[-1:] // BM).astype(jnp.int32)
    opens = (block_start < pad_ends[-1]) & (
        block_expert != jnp.concatenate([jnp.full((1,), -1, jnp.int32), block_expert[:-1]]))
    run_id = jnp.where(opens, jnp.cumsum(opens.astype(jnp.int32)) - 1, -1).astype(jnp.int32)
    experts = jnp.arange(N_EXPERTS, dtype=jnp.int32)
    later = (experts[None, :] > block_expert[:, None]) & (padded[None, :] > 0)
    next_expert = jnp.min(jnp.where(later, experts[None, :], N_EXPERTS), axis=1)
    next_expert = jnp.where(next_expert < N_EXPERTS, next_expert, -1).astype(jnp.int32)

    spare = (ROWS + chunk_row)[None, None, :]
    chunk_table = jnp.concatenate(
        [spare,
         jnp.where(chunk_row[None, None, :] < (nchk * SEG_ALIGN)[:, None, None], chunk_dest, spare),
         spare], axis=0).astype(jnp.int32)
    fill_ends = (pad_ends - padded + jnp.sum(seg, axis=0)).astype(jnp.int32)
    buf = _dispatch(pad_ends.astype(jnp.int32), fill_ends, chunk_table, pos_t, u2)
    mine = block_expert[:, None] == experts[None, :]
    filled = jnp.sum(jnp.where(mine, fill_ends[None, :], 0), axis=1)
    valid = jnp.clip(filled - block_start, 0, BM).astype(jnp.int32)
    y_buf = _ffn(block_expert, n_used, run_id, next_expert, valid, buf, w_up[0],
                 b_up[0], w_down[0], b_down[0])
    out = _combine(chunk_table, y_buf, h1, gate_tok, g2, row(ln2_g[0]), row(ln2_b[0]))
    return out.reshape(1, N_TOK, D)
```

```python
import functools
import math

import jax
import jax.numpy as jnp
import numpy as np
from jax import lax
from jax.experimental import pallas as pl
from jax.experimental.pallas import tpu as pltpu

F32 = jnp.float32
BF16 = jnp.bfloat16
HI = lax.Precision.HIGHEST

D = 1024
N_TOK = 16384
N_CTX = 256
GRID_W = 64
S5_GROUP = 16
S5_GROUPS = 32
S5_STATE = 64
S5_WIDTH = 512
FFT_GROUPS = 4
FFT_DIM = 128
FFT_WIDTH = 512
N_EXPERTS = 32
TOP_K = 4
LN_EPS = 1e-5
ALPHA = 2.0 ** 0.25
SWIGLU_ALPHA = 1.702
SWIGLU_LIMIT = 7.0

LANES = 128
SUBLANES = 8
VMEM_LIMIT = 56 * 1024 * 1024

CH = 8
N_CHUNK = N_TOK // CH
N_CHUNK_CTX = N_CTX // CH
NSEG = SUBLANES
SEG = N_CHUNK // NSEG
SCAN_UNROLL = 4
GPT = LANES // S5_GROUP
NJ = S5_WIDTH // LANES
CL = CH * LANES
SW = 4 * GPT * S5_STATE

FN = 128
FB = 16

TM = 512
TM_PROJ = 1024
N_TILES = N_TOK // TM
BM = 1024
ZB = 512
FFN_HALF = 512
FFN_TAIL = 128
N_SLOTS = N_TOK * TOP_K
SEG_ALIGN = 2 * SUBLANES
CAP_BLOCK = 256
CAP = -(-(TOP_K * TM + N_EXPERTS * (SEG_ALIGN - 1)) // CAP_BLOCK) * CAP_BLOCK
NCHK = CAP // SEG_ALIGN
N_BLOCKS = -(-(N_SLOTS + N_TILES * N_EXPERTS * (SEG_ALIGN - 1)) // BM) + N_EXPERTS
ROWS = N_BLOCKS * BM
N_BLOCKS_ALL = N_BLOCKS + -(-CAP // BM)
ROWS_ALL = N_BLOCKS_ALL * BM


def _cparams(sem):
    return pltpu.CompilerParams(dimension_semantics=sem, vmem_limit_bytes=VMEM_LIMIT)


def _layer_norm(x, g, b):
    mu = jnp.mean(x, axis=-1, keepdims=True)
    xc = x - mu
    var = jnp.mean(xc * xc, axis=-1, keepdims=True)
    return xc * lax.rsqrt(var + LN_EPS) * g + b


def _sigmoid(x):
    return 0.5 * jnp.tanh(0.5 * x) + 0.5


def _ada_kernel(c_ref, w_ref, b_ref, o_ref):
    c = c_ref[...]
    s = c * _sigmoid(c)
    o_ref[...] = jnp.dot(s, w_ref[...], preferred_element_type=F32, precision=HI) + b_ref[...]


def _ada(cc, w_ada, b_ada):
    nb = 4
    wb = 6 * D // nb
    return pl.pallas_call(
        _ada_kernel,
        grid=(nb,),
        in_specs=[pl.BlockSpec((SUBLANES, D), lambda i: (0, 0)),
                  pl.BlockSpec((D, wb), lambda i: (0, i)),
                  pl.BlockSpec((1, wb), lambda i: (0, i))],
        out_specs=pl.BlockSpec((SUBLANES, wb), lambda i: (0, i)),
        out_shape=jax.ShapeDtypeStruct((SUBLANES, 6 * D), F32),
        compiler_params=_cparams(("parallel",)),
        name="ada",
    )(cc, w_ada, b_ada)


def _fftw_kernel(w_ref, b_ref, f_ref, wo_ref, bo_ref):
    f = f_ref[...]
    wo_ref[...] = jnp.dot(w_ref[...], f, preferred_element_type=F32, precision=HI).astype(BF16)
    bo_ref[...] = jnp.dot(b_ref[...], f, preferred_element_type=F32, precision=HI)


def _fft_weights(w_in, b_fft8, fc):
    def full(a):
        return pl.BlockSpec(a.shape, lambda i: (0,) * a.ndim)
    outs = (jax.ShapeDtypeStruct((D, 2 * FFT_WIDTH), BF16),
            jax.ShapeDtypeStruct((SUBLANES, 2 * FFT_WIDTH), F32))
    return pl.pallas_call(
        _fftw_kernel,
        grid=(1,),
        in_specs=[pl.BlockSpec((D, FFT_WIDTH), lambda i: (0, S5_WIDTH // FFT_WIDTH)),
                  full(b_fft8), full(fc)],
        out_specs=tuple(pl.BlockSpec(o.shape, lambda i: (0, 0)) for o in outs),
        out_shape=outs,
        compiler_params=_cparams(("arbitrary",)),
        name="fftw",
    )(w_in, b_fft8, fc)


def _pos_code(er_ref, ec_ref, tm):
    nr = tm // GRID_W
    er = er_ref[...]
    row = jnp.broadcast_to(er[:, None, :], (nr, GRID_W, D // 2)).reshape(tm, D // 2)
    col = jnp.concatenate([ec_ref[...]] * nr, axis=0)
    return jnp.concatenate([row, col], axis=-1)


def _to_chunk_major(val, scr_ref, out_ref, tm):
    for j in range(NJ):
        scr_ref[j] = val[:, j * LANES:(j + 1) * LANES]
    for t in range(CH):
        for j in range(NJ):
            piece = scr_ref[j, pl.ds(t, tm // CH, stride=CH), :]
            out_ref[t, :, j * LANES:(j + 1) * LANES] = piece.astype(out_ref.dtype)


def _proj_kernel(x_ref, er_ref, ec_ref, lg_ref, lb_ref, m_ref, s_ref, ws_ref, wf_ref, b_ref,
                 p_ref, xr_ref, xi_ref, scr_ref):
    x = x_ref[...] + _pos_code(er_ref, ec_ref, TM_PROJ)
    h = _layer_norm(x, lg_ref[...], lb_ref[...])
    u = (h * m_ref[...] + s_ref[...]).astype(BF16)
    p_s5 = jnp.dot(u, ws_ref[...], preferred_element_type=F32) + b_ref[:, :S5_WIDTH]
    _to_chunk_major(p_s5, scr_ref, p_ref, TM_PROJ)
    p_f = jnp.dot(u, wf_ref[...], preferred_element_type=F32) + b_ref[:, S5_WIDTH:]
    xr_ref[...] = p_f[:, :FFT_WIDTH].astype(BF16)
    xi_ref[...] = p_f[:, FFT_WIDTH:].astype(BF16)


def _proj(x, emb_r, emb_c, lg, lb, m1, s1, w_in, w_fc, bcat):
    nw = bcat.shape[1]
    vec = pl.BlockSpec((1, D), lambda i: (0, 0))
    return pl.pallas_call(
        _proj_kernel,
        grid=(N_TOK // TM_PROJ,),
        in_specs=[pl.BlockSpec((TM_PROJ, D), lambda i: (i, 0)),
                  pl.BlockSpec((TM_PROJ // GRID_W, D // 2), lambda i: (i, 0)),
                  pl.BlockSpec((GRID_W, D // 2), lambda i: (0, 0)),
                  vec, vec, vec, vec,
                  pl.BlockSpec((D, S5_WIDTH), lambda i: (0, 0)),
                  pl.BlockSpec((D, 2 * FFT_WIDTH), lambda i: (0, 0)),
                  pl.BlockSpec((1, nw), lambda i: (0, 0))],
        out_specs=(pl.BlockSpec((CH, TM_PROJ // CH, S5_WIDTH), lambda i: (0, i, 0)),
                   pl.BlockSpec((TM_PROJ, FFT_WIDTH), lambda i: (i, 0)),
                   pl.BlockSpec((TM_PROJ, FFT_WIDTH), lambda i: (i, 0))),
        out_shape=(jax.ShapeDtypeStruct((CH, N_CHUNK, S5_WIDTH), BF16),
                   jax.ShapeDtypeStruct((N_TOK, FFT_WIDTH), BF16),
                   jax.ShapeDtypeStruct((N_TOK, FFT_WIDTH), BF16)),
        scratch_shapes=[pltpu.VMEM((NJ, TM_PROJ, LANES), F32)],
        compiler_params=_cparams(("parallel",)),
        name="proj",
    )(x, emb_r, emb_c, lg, lb, m1, s1, w_in, w_fc, bcat)


def _ctx_proj_kernel(x_ref, lg_ref, lb_ref, m_ref, s_ref, w_ref, b_ref, p_ref, scr_ref):
    h = _layer_norm(x_ref[...], lg_ref[...], lb_ref[...])
    u = (h * m_ref[...] + s_ref[...]).astype(BF16)
    p = jnp.dot(u, w_ref[...], preferred_element_type=F32) + b_ref[...]
    _to_chunk_major(p, scr_ref, p_ref, N_CTX)


def _ctx_proj(ctx, lg, lb, m1, s1, w_s5, b_s5):
    return pl.pallas_call(
        _ctx_proj_kernel,
        out_shape=jax.ShapeDtypeStruct((CH, N_CHUNK_CTX, S5_WIDTH), BF16),
        scratch_shapes=[pltpu.VMEM((NJ, N_CTX, LANES), F32)],
        compiler_params=pltpu.CompilerParams(vmem_limit_bytes=VMEM_LIMIT),
        name="ctxproj",
    )(ctx, lg, lb, m1, s1, w_s5, b_s5)


def _s5_tables(lam_re, lam_im, log_dt, b_re, b_im, c_re, c_im, d_skip):
    dt = jnp.exp(log_dt)[..., None]
    zr = lam_re * dt
    zi = lam_im * dt

    def apow(m):
        m = jnp.asarray(m, F32)
        mag = jnp.exp(zr[..., None] * m)
        return mag * jnp.cos(zi[..., None] * m), mag * jnp.sin(zi[..., None] * m)

    a_re, a_im = apow(jnp.ones((1,), F32))
    a_re, a_im = a_re[..., 0], a_im[..., 0]
    den = lam_re * lam_re + lam_im * lam_im
    num_re = a_re - 1.0
    k_re = (num_re * lam_re + a_im * lam_im) / den
    k_im = (a_im * lam_re - num_re * lam_im) / den
    bb_re = k_re[..., None] * b_re - k_im[..., None] * b_im
    bb_im = k_re[..., None] * b_im + k_im[..., None] * b_re

    ks = jnp.arange(CH + 1, dtype=F32)
    pw_re, pw_im = apow(ks)
    kmag = jnp.exp(zr[:, :, None, :] * ks[None, None, :, None])
    pk_re = kmag * jnp.cos(zi[:, :, None, :] * ks[None, None, :, None])
    pk_im = kmag * jnp.sin(zi[:, :, None, :] * ks[None, None, :, None])
    bt_re, bt_im = jnp.swapaxes(b_re, 2, 3), jnp.swapaxes(b_im, 2, 3)
    bbt_re = k_re[:, :, None, :] * bt_re - k_im[:, :, None, :] * bt_im
    bbt_im = k_re[:, :, None, :] * bt_im + k_im[:, :, None, :] * bt_re

    ar, ai = pw_re[:, :, :, :CH, None], pw_im[:, :, :, :CH, None]
    cr = jnp.swapaxes(c_re, 2, 3)[:, :, :, None, :]
    ci = jnp.swapaxes(c_im, 2, 3)[:, :, :, None, :]
    ca = jnp.concatenate([cr * ar - ci * ai, -(cr * ai + ci * ar)], axis=2)
    ca = ca.reshape(2, S5_GROUPS, 2 * S5_STATE, CH * S5_GROUP)
    bbt = jnp.concatenate([bbt_re, bbt_im], axis=-1)
    taps = jnp.einsum('dghq,dgqn->dghn', bbt, ca, precision=HI)
    skip = (d_skip.reshape(S5_GROUPS, S5_GROUP, 1) * jnp.eye(S5_GROUP, dtype=F32)[None])
    taps = taps.at[0, :, :, :S5_GROUP].add(skip)
    b_c = jnp.transpose(taps.reshape(2, NJ, GPT, S5_GROUP, CH, S5_GROUP), (1, 4, 0, 2, 3, 5))
    b_c = b_c.reshape(NJ, 2 * CH, LANES, S5_GROUP)

    ef = (CH - 1) - jnp.arange(CH)
    eb = jnp.arange(CH)

    def q_part(d, e):
        pr = pk_re[d][:, e, None, :]
        pi = pk_im[d][:, e, None, :]
        br = bbt_re[d][:, None, :, :]
        bi = bbt_im[d][:, None, :, :]
        return pr * br - pi * bi, pr * bi + pi * br

    def q_rows(v):
        v = v.reshape(NJ, GPT, CH, S5_GROUP, S5_STATE)
        return jnp.transpose(v, (0, 2, 1, 3, 4)).reshape(NJ, CL, S5_STATE)
    a_q = jnp.stack([q_rows(v) for v in q_part(0, ef) + q_part(1, eb)], axis=0)

    of = jnp.arange(CH) + 1
    ob = CH - jnp.arange(CH)

    def p_part(d, e):
        pr = pw_re[d][..., e][:, :, :, None]
        pi = pw_im[d][..., e][:, :, :, None]
        return (ct_re[d] * pr - ct_im[d] * pi, -(ct_re[d] * pi + ct_im[d] * pr))
    ct_re = jnp.swapaxes(c_re, 2, 3)[:, :, :, None, :]
    ct_im = jnp.swapaxes(c_im, 2, 3)[:, :, :, None, :]
    a_p = jnp.stack([v.reshape(NJ, GPT * S5_STATE, CH * S5_GROUP)
                     for v in p_part(0, of) + p_part(1, ob)], axis=0)

    def lanes(v):
        return jnp.transpose(v.reshape(2, NJ, GPT * S5_STATE), (1, 0, 2))
    c_r, c_i = apow(jnp.full((1,), float(CH), F32))
    s_r, s_i = apow(jnp.full((1,), float(CH * SEG), F32))
    cr, ci, sr, si = (lanes(v[..., 0]) for v in (c_r, c_i, s_r, s_i))
    trans = jnp.stack([cr[:, 0], ci[:, 0], cr[:, 1], ci[:, 1],
                       sr[:, 0], si[:, 0], sr[:, 1], si[:, 1]], axis=1)

    cidx = jnp.arange(N_CHUNK_CTX, dtype=F32)
    wf_r, wf_i = apow(CH * (N_CHUNK_CTX - 1 - cidx))
    wb_r, wb_i = apow(CH * cidx)

    def ctx_lanes(v, d):
        return jnp.transpose(v[d].reshape(NJ, GPT * S5_STATE, N_CHUNK_CTX), (0, 2, 1))
    ctx_w = jnp.stack([ctx_lanes(wf_r, 0), ctx_lanes(wf_i, 0),
                       ctx_lanes(wb_r, 1), ctx_lanes(wb_i, 1)], axis=1)
    return b_c, a_q, a_p, trans, ctx_w


def _s5w_kernel(bc_ref, aq_ref, ap_ref, c16_ref, c64_ref, cm_ref, wm_ref, wq_ref, wp_ref):
    def expand(a, c, row_shift, col_shift):
        w = jnp.dot(a, c, preferred_element_type=F32)
        rg = (lax.broadcasted_iota(jnp.int32, (w.shape[0], 1), 0) >> row_shift) & (GPT - 1)
        cg = (lax.broadcasted_iota(jnp.int32, (1, w.shape[1]), 1) >> col_shift) & (GPT - 1)
        return jnp.where(rg == cg, w, 0.0)

    blk = [expand(bc_ref[0, kd], c16_ref[...], 4, 4) for kd in range(2 * CH)]
    for t in range(CH):
        for u in range(CH):
            b = blk[2 * (u - t)] if u > t else blk[2 * (t - u) + 1] if u < t else blk[0] + blk[1]
            wm_ref[0, t * LANES:(t + 1) * LANES, u * LANES:(u + 1) * LANES] = b.astype(BF16)
    half = GPT * S5_STATE
    for s in range(4):
        wq_ref[0, :, s * half:(s + 1) * half] = expand(aq_ref[s, 0], c64_ref[...], 4, 6).astype(BF16)
        wp_ref[0, s * half:(s + 1) * half, :] = expand(ap_ref[s, 0], cm_ref[...], 6, 4).astype(BF16)


def _s5_expand(b_c, a_q, a_p):
    rep = np.ones((1, GPT))
    c16 = jnp.asarray(np.kron(rep, np.eye(S5_GROUP)), F32).astype(BF16)
    c64 = jnp.asarray(np.kron(rep, np.eye(S5_STATE)), F32).astype(BF16)
    c_m = jnp.asarray(np.kron(np.eye(CH), np.kron(rep, np.eye(S5_GROUP))), F32).astype(BF16)
    b_c, a_q, a_p = b_c.astype(BF16), a_q.astype(BF16), a_p.astype(BF16)
    half = GPT * S5_STATE
    return pl.pallas_call(
        _s5w_kernel,
        grid=(NJ,),
        in_specs=[pl.BlockSpec((1, 2 * CH, LANES, S5_GROUP), lambda j: (j, 0, 0, 0)),
                  pl.BlockSpec((4, 1, CL, S5_STATE), lambda j: (0, j, 0, 0)),
                  pl.BlockSpec((4, 1, half, CH * S5_GROUP), lambda j: (0, j, 0, 0)),
                  pl.BlockSpec(c16.shape, lambda j: (0, 0)),
                  pl.BlockSpec(c64.shape, lambda j: (0, 0)),
                  pl.BlockSpec(c_m.shape, lambda j: (0, 0))],
        out_specs=(pl.BlockSpec((1, CL, CL), lambda j: (j, 0, 0)),
                   pl.BlockSpec((1, CL, SW), lambda j: (j, 0, 0)),
                   pl.BlockSpec((1, SW, CL), lambda j: (j, 0, 0))),
        out_shape=(jax.ShapeDtypeStruct((NJ, CL, CL), BF16),
                   jax.ShapeDtypeStruct((NJ, CL, SW), BF16),
                   jax.ShapeDtypeStruct((NJ, SW, CL), BF16)),
        compiler_params=_cparams(("parallel",)),
        name="s5w",
    )(b_c, a_q, a_p, c16, c64, c_m)


def _s5_kernel(p_ref, pc_ref, wm_ref, wq_ref, wp_ref, tr_ref, cw_ref, y_ref, v_ref):
    nq = NJ
    half = GPT * S5_STATE

    def chunk_rows(ref, r0, nrows):
        return jnp.concatenate([ref[t, pl.ds(r0, nrows), :] for t in range(CH)], axis=-1)

    def fill(k, c):
        r0 = pl.multiple_of(k * SEG, SEG)
        v = jnp.dot(chunk_rows(p_ref, r0, SEG), wq_ref[0], preferred_element_type=F32)
        for s in range(4 * nq):
            v_ref[s, pl.ds(k, SEG, stride=NSEG), :] = v[:, s * LANES:(s + 1) * LANES]
        return c
    lax.fori_loop(0, NSEG, fill, 0)

    vc = jnp.dot(chunk_rows(pc_ref, 0, N_CHUNK_CTX), wq_ref[0], preferred_element_type=F32)
    vfr, vfi, vbr, vbi = (vc[:, i * half:(i + 1) * half] for i in range(4))
    wfr, wfi, wbr, wbi = (cw_ref[0, i] for i in range(4))
    s0_fr = jnp.sum(wfr * vfr - wfi * vfi, axis=0, keepdims=True)
    s0_fi = jnp.sum(wfr * vfi + wfi * vfr, axis=0, keepdims=True)
    s0_br = jnp.sum(wbr * vbr - wbi * vbi, axis=0, keepdims=True)
    s0_bi = jnp.sum(wbr * vbi + wbi * vbr, axis=0, keepdims=True)

    tr = tr_ref[0]
    afr, afi, abr, abi = (jnp.broadcast_to(tr[i:i + 1], (NSEG, half)) for i in range(4))
    gfr, gfi, gbr, gbi = (tr[i:i + 1] for i in range(4, 8))

    def load_part(part, i):
        return jnp.concatenate(
            [v_ref[part * nq + q, pl.ds(pl.multiple_of(i * NSEG, NSEG), NSEG), :] for q in range(nq)],
            axis=-1)

    def store_part(part, i, val):
        for q in range(nq):
            v_ref[part * nq + q, pl.ds(pl.multiple_of(i * NSEG, NSEG), NSEG), :] = (
                val[:, q * LANES:(q + 1) * LANES])

    def step(i, carry, write):
        fr, fi, br, bi = carry
        ib = SEG - 1 - i
        ufr, ufi = load_part(0, i), load_part(1, i)
        ubr, ubi = load_part(2, ib), load_part(3, ib)
        if write:
            store_part(0, i, fr)
            store_part(1, i, fi)
            store_part(2, ib, br)
            store_part(3, ib, bi)
        return (afr * fr - afi * fi + ufr, afr * fi + afi * fr + ufi,
                abr * br - abi * bi + ubr, abr * bi + abi * br + ubi)

    zero = jnp.zeros((NSEG, half), F32)
    ffr, ffi, fbr, fbi = lax.fori_loop(0, SEG, functools.partial(step, write=False),
                                       (zero, zero, zero, zero), unroll=SCAN_UNROLL)

    rows_fr, rows_fi = [s0_fr], [s0_fi]
    for k in range(1, NSEG):
        pr, pi = rows_fr[-1], rows_fi[-1]
        rows_fr.append(gfr * pr - gfi * pi + ffr[k - 1:k])
        rows_fi.append(gfr * pi + gfi * pr + ffi[k - 1:k])
    rows_br, rows_bi = [s0_br], [s0_bi]
    for k in range(NSEG - 2, -1, -1):
        pr, pi = rows_br[0], rows_bi[0]
        rows_br.insert(0, gbr * pr - gbi * pi + fbr[k + 1:k + 2])
        rows_bi.insert(0, gbr * pi + gbi * pr + fbi[k + 1:k + 2])
    init = tuple(jnp.concatenate(r, axis=0) for r in (rows_fr, rows_fi, rows_br, rows_bi))

    lax.fori_loop(0, SEG, functools.partial(step, write=True), init, unroll=SCAN_UNROLL)

    def emit(k, c):
        r0 = pl.multiple_of(k * SEG, SEG)
        b = chunk_rows(p_ref, r0, SEG)
        sin = jnp.concatenate([v_ref[s, pl.ds(k, SEG, stride=NSEG), :] for s in range(4 * nq)], axis=-1)
        y = (jnp.dot(b, wm_ref[0], preferred_element_type=F32)
             + jnp.dot(sin.astype(BF16), wp_ref[0], preferred_element_type=F32))
        for t in range(CH):
            y_ref[t, pl.ds(r0, SEG), :] = y[:, t * LANES:(t + 1) * LANES].astype(y_ref.dtype)
        return c
    lax.fori_loop(0, NSEG, emit, 0)


def _s5(p_t, pc_t, w_m, w_q, w_p, trans, ctx_w):
    one = pl.Buffered(1)
    return pl.pallas_call(
        _s5_kernel,
        grid=(NJ,),
        in_specs=[pl.BlockSpec((CH, N_CHUNK, LANES), lambda j: (0, 0, j)),
                  pl.BlockSpec((CH, N_CHUNK_CTX, LANES), lambda j: (0, 0, j)),
                  pl.BlockSpec((1, CL, CL), lambda j: (j, 0, 0), pipeline_mode=one),
                  pl.BlockSpec((1, CL, SW), lambda j: (j, 0, 0), pipeline_mode=one),
                  pl.BlockSpec((1, SW, CL), lambda j: (j, 0, 0), pipeline_mode=one),
                  pl.BlockSpec((1, SUBLANES, GPT * S5_STATE), lambda j: (j, 0, 0)),
                  pl.BlockSpec((1, 4, N_CHUNK_CTX, GPT * S5_STATE), lambda j: (j, 0, 0, 0))],
        out_specs=pl.BlockSpec((CH, N_CHUNK, LANES), lambda j: (0, 0, j)),
        out_shape=jax.ShapeDtypeStruct((CH, N_CHUNK, S5_WIDTH), BF16),
        scratch_shapes=[pltpu.VMEM((4 * NJ, N_CHUNK, LANES), F32)],
        compiler_params=_cparams(("parallel",)),
        name="s5",
    )(p_t, pc_t, w_m, w_q, w_p, trans, ctx_w)


def _dft_tables():
    n = np.arange(FN)
    ang = 2.0 * np.pi * np.outer(n, n) / FN
    c, s = np.cos(ang), np.sin(ang)
    st1 = np.block([[c, s], [-s, c]])
    tw = 2.0 * np.pi * np.outer(n, n) / (FN * FN)
    wr, wi = np.cos(tw), -np.sin(tw)
    fr = c[None] * wr[:, None, :] + s[None] * wi[:, None, :]
    fi = c[None] * wi[:, None, :] - s[None] * wr[:, None, :]
    st2 = np.concatenate([fr, -fi], axis=-1)
    scale = 1.0 / math.sqrt(N_TOK * FFT_DIM)
    blk_c = np.kron(np.eye(FFT_GROUPS), c) * scale
    blk_s = np.kron(np.eye(FFT_GROUPS), s) * scale
    fc = np.concatenate([blk_c, -blk_s], axis=1)
    return (jnp.asarray(st1, F32).astype(BF16), jnp.asarray(st2, F32).astype(BF16), jnp.asarray(fc, F32))


FSL = FFT_WIDTH // LANES


FBH = FB // SUBLANES


def _block_to_slabs(blk, slab_ref, first, per_half):
    for bh in range(FBH):
        val = blk[:, bh * SUBLANES:(bh + 1) * SUBLANES, :].reshape(FN * SUBLANES, FFT_WIDTH)
        for s in range(FSL):
            slab_ref[bh * per_half + first + s] = val[:, s * LANES:(s + 1) * LANES]


def _slab_rows(b, first, per_half):
    return (b // SUBLANES) * per_half + first, pl.ds(b % SUBLANES, FN, stride=SUBLANES)


def _slabs_to_block(slab_ref, first, per_half):
    halves = []
    for bh in range(FBH):
        val = jnp.concatenate([slab_ref[bh * per_half + first + s] for s in range(FSL)], axis=-1)
        halves.append(val.reshape(FN, SUBLANES, FFT_WIDTH))
    return jnp.concatenate(halves, axis=1)


def _fft1_kernel(xr_ref, xi_ref, f_ref, yr_ref, yi_ref, in_ref, out_ref):
    _block_to_slabs(xr_ref[...].astype(F32), in_ref, 0, 2 * FSL)
    _block_to_slabs(xi_ref[...].astype(F32), in_ref, FSL, 2 * FSL)
    for b in range(FB):
        def part(first):
            base, rows = _slab_rows(b, first, 2 * FSL)
            return jnp.concatenate([in_ref[base + s, rows, :] for s in range(FSL)], axis=-1)
        xs = jnp.concatenate([part(0), part(FSL)], axis=0).astype(BF16)
        y = jnp.dot(f_ref[...], xs, preferred_element_type=F32)
        base, rows = _slab_rows(b, 0, 2 * FSL)
        for s in range(FSL):
            out_ref[base + s, rows, :] = y[:FN, s * LANES:(s + 1) * LANES]
            out_ref[base + FSL + s, rows, :] = y[FN:, s * LANES:(s + 1) * LANES]
    yr_ref[...] = _slabs_to_block(out_ref, 0, 2 * FSL).astype(BF16)
    yi_ref[...] = _slabs_to_block(out_ref, FSL, 2 * FSL).astype(BF16)


def _fft1(xr, xi, st1):
    spec = pl.BlockSpec((FN, FB, FFT_WIDTH), lambda i: (0, i, 0))
    slabs = pltpu.VMEM((FBH * 2 * FSL, FN * SUBLANES, LANES), F32)
    return pl.pallas_call(
        _fft1_kernel,
        grid=(FN // FB,),
        in_specs=[spec, spec, pl.BlockSpec((2 * FN, 2 * FN), lambda i: (0, 0))],
        out_specs=(spec, spec),
        out_shape=(jax.ShapeDtypeStruct((FN, FN, FFT_WIDTH), BF16),) * 2,
        scratch_shapes=[slabs, slabs],
        compiler_params=_cparams(("parallel",)),
        name="fft1",
    )(xr.reshape(FN, FN, FFT_WIDTH), xi.reshape(FN, FN, FFT_WIDTH), st1)


def _fft2_kernel(yr_ref, yi_ref, f_ref, z_ref, out_ref):
    for b in range(FB):
        ys = jnp.concatenate([yr_ref[b * FN:(b + 1) * FN, :], yi_ref[b * FN:(b + 1) * FN, :]], axis=0)
        z = jnp.dot(f_ref[b], ys, preferred_element_type=F32)
        base, rows = _slab_rows(b, 0, FSL)
        for s in range(FSL):
            out_ref[base + s, rows, :] = z[:, s * LANES:(s + 1) * LANES]
    z_ref[...] = _slabs_to_block(out_ref, 0, FSL).astype(BF16)


def _fft2(yr, yi, st2):
    rows = pl.BlockSpec((FB * FN, FFT_WIDTH), lambda i: (i, 0))
    z = pl.pallas_call(
        _fft2_kernel,
        grid=(FN // FB,),
        in_specs=[rows, rows, pl.BlockSpec((FB, FN, 2 * FN), lambda i: (i, 0, 0))],
        out_specs=pl.BlockSpec((FN, FB, FFT_WIDTH), lambda i: (0, i, 0)),
        out_shape=jax.ShapeDtypeStruct((FN, FN, FFT_WIDTH), BF16),
        scratch_shapes=[pltpu.VMEM((FBH * FSL, FN * SUBLANES, LANES), F32)],
        compiler_params=_cparams(("parallel",)),
        name="fft2",
    )(yr.reshape(N_TOK, FFT_WIDTH), yi.reshape(N_TOK, FFT_WIDTH), st2)
    return z.reshape(N_TOK, FFT_WIDTH)


def _gelu_tanh(x):
    return 0.5 * x * (1.0 + jnp.tanh(math.sqrt(2.0 / math.pi) * (x + 0.044715 * (x * x * x))))


def _mix_kernel(x_ref, er_ref, ec_ref, lg_ref, lb_ref, m1_ref, s1_ref, wgs_ref, wgf_ref, bg_ref,
                yt_ref, zr_ref, wglu_ref, bglu_ref, wbs_ref, wbf_ref, bbf_ref, wo_ref, bo_ref,
                g1_ref, l1g_ref, l1b_ref, m2_ref, s2_ref, wr_ref, br_ref, tri_ref, etri_ref,
                h1_ref, u2_ref, pos_ref, gate_ref, cnt_ref, scr_ref):
    pos = _pos_code(er_ref, ec_ref, TM)

    def front(r0, nr):
        rows = slice(r0, r0 + nr)
        h = _layer_norm(x_ref[rows, :] + pos[rows, :], lg_ref[...], lb_ref[...])
        u = (h * m1_ref[...] + s1_ref[...]).astype(BF16)

        c0, nc = r0 // CH, nr // CH
        for t in range(CH):
            for j in range(NJ):
                scr_ref[j, pl.ds(r0 + t, nc, stride=CH), :] = (
                    yt_ref[t, c0:c0 + nc, j * LANES:(j + 1) * LANES].astype(F32))
        ys = jnp.concatenate([scr_ref[j, rows, :] for j in range(NJ)], axis=-1)
        z = jnp.dot(_gelu_tanh(ys).astype(BF16), wglu_ref[...], preferred_element_type=F32) + bglu_ref[...]
        glu = (z[:, :S5_WIDTH] * _sigmoid(z[:, S5_WIDTH:])).astype(BF16)
        g_s5 = _sigmoid(jnp.dot(u, wgs_ref[...], preferred_element_type=F32) + bg_ref[:, :D])
        g_fft = _sigmoid(jnp.dot(u, wgf_ref[...], preferred_element_type=F32) + bg_ref[:, D:])
        y_s5 = jnp.dot(glu, wbs_ref[...], preferred_element_type=F32)
        y_fft = jnp.dot(zr_ref[rows, :], wbf_ref[...], preferred_element_type=F32) + bbf_ref[...]
        mixed = (g_s5 * y_s5 + g_fft * y_fft).astype(BF16)
        y = jnp.dot(mixed, wo_ref[...], preferred_element_type=F32) + bo_ref[...]
        h1 = _layer_norm(ALPHA * h + g1_ref[...] * y, l1g_ref[...], l1b_ref[...])
        h1_ref[rows, :] = h1
        u2 = h1 * m2_ref[...] + s2_ref[...]
        u2_ref[rows, :] = u2.astype(BF16)
        u_hi = u2.astype(BF16)
        u_lo = (u2 - u_hi.astype(F32)).astype(BF16)

        def nt(a, b):
            return lax.dot_general(a, b, (((1,), (1,)), ((), ())), preferred_element_type=F32)
        return nt(wr_ref[0], u_hi) + nt(wr_ref[0], u_lo) + nt(wr_ref[1], u_hi)

    logits = front(0, TM) + br_ref[:, 0:1]
    eidx = lax.broadcasted_iota(jnp.int32, (N_EXPERTS, TM), 0)
    vals, hots = [], []
    cur = logits
    for _k in range(TOP_K):
        m = jnp.max(cur, axis=0, keepdims=True)
        sel = jnp.min(jnp.where(cur == m, eidx, N_EXPERTS), axis=0, keepdims=True)
        hot = eidx == sel
        cur = jnp.where(hot, -jnp.inf, cur)
        vals.append(m)
        hots.append(hot)
    exps = [jnp.exp(v - vals[0]) for v in vals]
    den = exps[0] + exps[1] + exps[2] + exps[3]
    gate4 = jnp.concatenate([e / den for e in exps], axis=0)

    hot_sum = (hots[0] | hots[1] | hots[2] | hots[3]).astype(F32)
    before = jnp.dot(hot_sum.astype(BF16), tri_ref[...], preferred_element_type=F32)
    cnt = jnp.broadcast_to(jnp.sum(hot_sum, axis=1, keepdims=True), (N_EXPERTS, LANES))
    cnt8 = jnp.floor((cnt + (SEG_ALIGN - 1)) * (1.0 / SEG_ALIGN)) * SEG_ALIGN
    seg0 = jnp.dot(etri_ref[...], cnt8.astype(BF16), preferred_element_type=F32)
    tot = seg0[:, 0:1] + before
    pos4 = jnp.concatenate(
        [jnp.sum(jnp.where(hk, tot, 0.0), axis=0, keepdims=True) for hk in hots], axis=0)
    pos_ref[...] = pos4.astype(jnp.int32)
    cnt_ref[0] = cnt

    gpad = jnp.concatenate([gate4, pos4, jnp.zeros((LANES - 2 * TOP_K, TM), F32)], axis=0)
    gate_ref[...] = gpad.T


def _mix(x, emb_r, emb_c, lg, lb, m1, s1, w_in, bg, y_t, zr, wglu, bglu, wbs, wbf, bbf, wo, bo,
         g1, l1g, l1b, m2, s2, wr_t, br, tri, etri):
    gate_cols = (S5_WIDTH + FFT_WIDTH) // D
    vec = pl.BlockSpec((1, D), lambda i: (0, 0))

    def full(a):
        return pl.BlockSpec(a.shape, lambda i: (0,) * a.ndim)
    return pl.pallas_call(
        _mix_kernel,
        grid=(N_TOK // TM,),
        in_specs=[pl.BlockSpec((TM, D), lambda i: (i, 0)),
                  pl.BlockSpec((TM // GRID_W, D // 2), lambda i: (i, 0)),
                  pl.BlockSpec((GRID_W, D // 2), lambda i: (0, 0)),
                  vec, vec, vec, vec,
                  pl.BlockSpec((D, D), lambda i: (0, gate_cols)),
                  pl.BlockSpec((D, D), lambda i: (0, gate_cols + 1)), full(bg),
                  pl.BlockSpec((CH, TM // CH, S5_WIDTH), lambda i: (0, i, 0)),
                  pl.BlockSpec((TM, FFT_WIDTH), lambda i: (i, 0)),
                  full(wglu), full(bglu), full(wbs), full(wbf), full(bbf), full(wo), full(bo),
                  vec, vec, vec, vec, vec, full(wr_t), full(br), full(tri), full(etri)],
        out_specs=(pl.BlockSpec((TM, D), lambda i: (i, 0)),
                   pl.BlockSpec((TM, D), lambda i: (i, 0)),
                   pl.BlockSpec((TOP_K, TM), lambda i: (0, i)),
                   pl.BlockSpec((TM, LANES), lambda i: (i, 0)),
                   pl.BlockSpec((1, N_EXPERTS, LANES), lambda i: (i, 0, 0))),
        out_shape=(jax.ShapeDtypeStruct((N_TOK, D), F32),
                   jax.ShapeDtypeStruct((N_TOK, D), BF16),
                   jax.ShapeDtypeStruct((TOP_K, N_TOK), jnp.int32),
                   jax.ShapeDtypeStruct((N_TOK, LANES), F32),
                   jax.ShapeDtypeStruct((N_TILES, N_EXPERTS, LANES), F32)),
        scratch_shapes=[pltpu.VMEM((NJ, TM, LANES), F32)],
        compiler_params=_cparams(("parallel",)),
        name="mix",
    )(x, emb_r, emb_c, lg, lb, m1, s1, w_in, w_in, bg, y_t, zr, wglu, bglu, wbs, wbf, bbf, wo, bo,
      g1, l1g, l1b, m2, s2, wr_t, br, tri, etri)


def _on_parity(i, fn):
    @pl.when(i % 2 == 0)
    def _():
        fn(0)

    @pl.when(i % 2 == 1)
    def _():
        fn(1)


def _dispatch_kernel(pend_ref, fill_ref, dprev_ref, dest_ref, pos_ref, u_ref, buf_ref,
                     sorted_ref, zero_ref, zsem, tsem, sems):
    i = pl.program_id(0)

    units = BM // ZB

    def clear_copy(start, sem):
        return pltpu.make_async_copy(
            zero_ref, buf_ref.at[pl.ds(pl.multiple_of(start, ZB), ZB)], sem)

    def unused_units(fn):
        def unit(b, c):
            fn(clear_copy(b * ZB, tsem))
            return c
        lax.fori_loop(pend_ref[N_EXPERTS - 1] // ZB, N_BLOCKS * units, unit, 0)

    @pl.when(i == 0)
    def _():
        zero_ref[...] = jnp.zeros_like(zero_ref)

        def each(fn):
            def expert(e, c):
                for z in range(units):
                    start = pend_ref[e] - (z + 1) * ZB

                    @pl.when(start + ZB > fill_ref[e])
                    def _():
                        fn(clear_copy(start, zsem))
                return c
            lax.fori_loop(0, N_EXPERTS, expert, 0)
            for b in range(N_BLOCKS * units, N_BLOCKS_ALL * units):
                fn(clear_copy(b * ZB, zsem))
        each(lambda cp: cp.start())
        unused_units(lambda cp: cp.start())
        each(lambda cp: cp.wait())

        sorted_ref[...] = jnp.zeros_like(sorted_ref)

    def chunk_copy(slot, table_ref, j):
        dst = pl.multiple_of(table_ref[0, 0, j], SEG_ALIGN)
        return pltpu.make_async_copy(sorted_ref.at[slot, pl.ds(j * SEG_ALIGN, SEG_ALIGN)],
                                     buf_ref.at[pl.ds(dst, SEG_ALIGN)], sems.at[slot])

    def drain(slot):
        pltpu.make_async_copy(sorted_ref.at[slot], buf_ref.at[pl.ds(0, CAP)], sems.at[slot]).wait()

    def run(slot):
        prev, older = (slot + 2) % 3, (slot + 1) % 3
        pos = pos_ref[...]
        u = u_ref[...]
        n_rb = CAP // CAP_BLOCK
        per_rb = NCHK // (n_rb // 2)
        for rb in range(n_rb):
            for j in range(rb * per_rb, min((rb + 1) * per_rb, NCHK)):
                chunk_copy(prev, dprev_ref, j).start()
            rows = lax.broadcasted_iota(jnp.int32, (CAP_BLOCK, TM), 0) + rb * CAP_BLOCK
            hit = rows == pos[0:1]
            for k in range(1, TOP_K):
                hit = hit | (rows == pos[k:k + 1])
            onehot = jnp.where(hit, 1.0, 0.0).astype(BF16)
            sorted_ref[slot, rb * CAP_BLOCK:(rb + 1) * CAP_BLOCK, :] = jnp.dot(
                onehot, u, preferred_element_type=F32).astype(BF16)

        @pl.when(i > 0)
        def _():
            drain(older)

        @pl.when(i == N_TILES - 1)
        def _():
            def issue(j, c):
                chunk_copy(slot, dest_ref, j).start()
                return c
            lax.fori_loop(0, NCHK, issue, 0)
            drain(prev)
            drain(slot)
            unused_units(lambda cp: cp.wait())
    for s in range(3):
        pl.when(i % 3 == s)(functools.partial(run, s))


def _dispatch(pad_ends, padded, chunk_table, pos_t, u2):
    return pl.pallas_call(
        _dispatch_kernel,
        grid_spec=pltpu.PrefetchScalarGridSpec(
            num_scalar_prefetch=2,
            grid=(N_TILES,),
            in_specs=[pl.BlockSpec((1, 1, NCHK), lambda i, a, b: (i, 0, 0), memory_space=pltpu.SMEM),
                      pl.BlockSpec((1, 1, NCHK), lambda i, a, b: (i + 1, 0, 0), memory_space=pltpu.SMEM),
                      pl.BlockSpec((TOP_K, TM), lambda i, a, b: (0, i)),
                      pl.BlockSpec((TM, D), lambda i, a, b: (i, 0))],
            out_specs=pl.BlockSpec(memory_space=pl.ANY),
            scratch_shapes=[pltpu.VMEM((3, CAP, D), BF16),
                            pltpu.VMEM((ZB, D), BF16),
                            pltpu.SemaphoreType.DMA(()),
                            pltpu.SemaphoreType.DMA(()),
                            pltpu.SemaphoreType.DMA((3,))]),
        out_shape=jax.ShapeDtypeStruct((ROWS_ALL, D), BF16),
        compiler_params=_cparams(("arbitrary",)),
        name="dispatch",
    )(pad_ends, padded, chunk_table, chunk_table, pos_t, u2)


def _ffn_kernel(be_ref, nu_ref, run_ref, nxt_ref, valid_ref, x_ref, wu_hbm, bu_ref, wd_hbm, bd_ref,
                y_ref, wu_ref, wd_ref, wub_ref, wdb_ref, sems):
    i = pl.program_id(0)
    used = i < nu_ref[0]

    def weight_copies(e, slot):
        return (pltpu.make_async_copy(wu_hbm.at[e], wu_ref.at[slot], sems.at[slot]),
                pltpu.make_async_copy(wd_hbm.at[e], wd_ref.at[slot], sems.at[slot]))

    @pl.when(used)
    def _():
        run = run_ref[i]

        @pl.when(run >= 0)
        def _():
            def open_run(slot):
                @pl.when(run == 0)
                def _():
                    for cp in weight_copies(be_ref[i], slot):
                        cp.start()

                @pl.when(nxt_ref[i] >= 0)
                def _():
                    for cp in weight_copies(nxt_ref[i], 1 - slot):
                        cp.start()
                for cp in weight_copies(be_ref[i], slot):
                    cp.wait()
                wub_ref[...] = wu_ref[slot].astype(BF16)
                wdb_ref[...] = wd_ref[slot].astype(BF16)
            _on_parity(run, open_run)

        def expert_rows(r0, nr):
            rows = slice(r0, r0 + nr)
            e = be_ref[i]
            h = (jnp.dot(x_ref[rows, :], wub_ref[...], preferred_element_type=F32)
                 + bu_ref[pl.ds(e, 1), :])
            h_glu = jnp.minimum(h[:, :D], SWIGLU_LIMIT)
            h_lin = jnp.clip(h[:, D:], -SWIGLU_LIMIT, SWIGLU_LIMIT)
            act = (h_glu * _sigmoid(SWIGLU_ALPHA * h_glu) * (h_lin + 1.0)).astype(BF16)
            y_ref[rows, :] = (jnp.dot(act, wdb_ref[...], preferred_element_type=F32)
                              + bd_ref[pl.ds(e, 1), :]).astype(BF16)

        valid = valid_ref[i]

        @pl.when(valid == BM)
        def _():
            expert_rows(0, BM)

        @pl.when(valid < BM)
        def _():
            for h0 in range(0, BM, FFN_HALF):
                @pl.when(valid >= h0 + FFN_HALF)
                def _(h0=h0):
                    expert_rows(h0, FFN_HALF)

                @pl.when(valid < h0 + FFN_HALF)
                def _(h0=h0):
                    for r0 in range(h0, h0 + FFN_HALF, FFN_TAIL):
                        @pl.when(r0 < valid)
                        def _(r0=r0):
                            expert_rows(r0, FFN_TAIL)

                        @pl.when(r0 >= valid)
                        def _(r0=r0):
                            y_ref[r0:r0 + FFN_TAIL, :] = jnp.zeros((FFN_TAIL, D), BF16)

    @pl.when(jnp.logical_not(used))
    def _():
        y_ref[...] = jnp.zeros_like(y_ref)


def _ffn(block_expert, n_used, run_id, next_expert, valid, buf, w_up, b_up, w_down, b_down):
    def blk(i, be, nu, *_):
        return jnp.minimum(i, nu[0] - 1)
    return pl.pallas_call(
        _ffn_kernel,
        grid_spec=pltpu.PrefetchScalarGridSpec(
            num_scalar_prefetch=5,
            grid=(N_BLOCKS_ALL,),
            in_specs=[pl.BlockSpec((BM, D), lambda i, *s: (blk(i, *s), 0)),
                      pl.BlockSpec(memory_space=pl.ANY),
                      pl.BlockSpec((N_EXPERTS, 2 * D), lambda i, *s: (0, 0)),
                      pl.BlockSpec(memory_space=pl.ANY),
                      pl.BlockSpec((N_EXPERTS, D), lambda i, *s: (0, 0))],
            out_specs=pl.BlockSpec((BM, D), lambda i, *s: (i, 0)),
            scratch_shapes=[pltpu.VMEM((2, D, 2 * D), F32),
                            pltpu.VMEM((2, D, D), F32),
                            pltpu.VMEM((D, 2 * D), BF16),
                            pltpu.VMEM((D, D), BF16),
                            pltpu.SemaphoreType.DMA((2,))]),
        out_shape=jax.ShapeDtypeStruct((ROWS_ALL, D), BF16),
        compiler_params=_cparams(("arbitrary",)),
        name="ffn",
    )(block_expert, n_used, run_id, next_expert, valid, buf, w_up, b_up, w_down, b_down)


def _combine_kernel(dest_ref, dnext_ref, y_ref, h1_ref, gate_ref, g2_ref, lg_ref, lb_ref,
                    o_ref, sorted_ref, sems):
    i = pl.program_id(0)

    def chunk_copy(slot, table_ref, j):
        src = pl.multiple_of(table_ref[0, 0, j], SEG_ALIGN)
        return pltpu.make_async_copy(y_ref.at[pl.ds(src, SEG_ALIGN)],
                                     sorted_ref.at[slot, pl.ds(j * SEG_ALIGN, SEG_ALIGN)],
                                     sems.at[slot])

    def drain(slot):
        pltpu.make_async_copy(y_ref.at[pl.ds(0, CAP)], sorted_ref.at[slot], sems.at[slot]).wait()

    @pl.when(i == 0)
    def _():
        def issue(j, c):
            chunk_copy(0, dest_ref, j).start()
            return c
        lax.fori_loop(0, NCHK, issue, 0)

    def run(slot):
        drain(slot)

        gp = gate_ref[...]
        m = jnp.zeros((TM, D), F32)
        n_cb = CAP // CAP_BLOCK
        per_cb = NCHK // (n_cb // 2)
        for cb in range(n_cb):
            for j in range(cb * per_cb, min((cb + 1) * per_cb, NCHK)):
                chunk_copy(1 - slot, dnext_ref, j).start()
            cols = (lax.broadcasted_iota(jnp.int32, (TM, CAP_BLOCK), 1) + cb * CAP_BLOCK).astype(F32)
            g = jnp.where(cols == gp[:, TOP_K:TOP_K + 1], gp[:, 0:1], 0.0)
            for k in range(1, TOP_K):
                g = g + jnp.where(cols == gp[:, TOP_K + k:TOP_K + k + 1], gp[:, k:k + 1], 0.0)
            rows = sorted_ref[slot, cb * CAP_BLOCK:(cb + 1) * CAP_BLOCK, :]
            m = m + jnp.dot(g.astype(BF16), rows, preferred_element_type=F32)
        o_ref[...] = _layer_norm(ALPHA * h1_ref[...] + g2_ref[...] * m, lg_ref[...], lb_ref[...])

        @pl.when(i == N_TILES - 1)
        def _():
            drain(1 - slot)
    _on_parity(i, run)


def _combine(chunk_table, y_buf, h1, gate_tok, g2, lg, lb):
    vec = pl.BlockSpec((1, D), lambda i: (0, 0))
    return pl.pallas_call(
        _combine_kernel,
        grid_spec=pltpu.PrefetchScalarGridSpec(
            num_scalar_prefetch=0,
            grid=(N_TILES,),
            in_specs=[pl.BlockSpec((1, 1, NCHK), lambda i: (i + 1, 0, 0), memory_space=pltpu.SMEM),
                      pl.BlockSpec((1, 1, NCHK), lambda i: (i + 2, 0, 0), memory_space=pltpu.SMEM),
                      pl.BlockSpec(memory_space=pl.ANY),
                      pl.BlockSpec((TM, D), lambda i: (i, 0)),
                      pl.BlockSpec((TM, LANES), lambda i: (i, 0)),
                      vec, vec, vec],
            out_specs=pl.BlockSpec((TM, D), lambda i: (i, 0)),
            scratch_shapes=[pltpu.VMEM((2, CAP, D), BF16),
                            pltpu.SemaphoreType.DMA((2,))]),
        out_shape=jax.ShapeDtypeStruct((N_TOK, D), F32),
        compiler_params=_cparams(("arbitrary",)),
        name="combine",
    )(chunk_table, chunk_table, y_buf, h1, gate_tok, g2, lg, lb)


def _sincos_tables():
    q = D // 4
    omega = 1.0 / (10000.0 ** (jnp.arange(q, dtype=F32) / q))

    def emb(n):
        ang = jnp.arange(n, dtype=F32)[:, None] * omega[None, :]
        return jnp.concatenate([jnp.sin(ang), jnp.cos(ang)], axis=-1)
    return emb(N_TOK // GRID_W), emb(GRID_W)


def kernel(x, c, ctx, c_ctx, ln_in_g, ln_in_b, w_ada, b_ada, w_in, b_in, s5_lambda_re, s5_lambda_im, s5_log_dt, s5_b_re, s5_b_im, s5_c_re, s5_c_im, s5_d, w_glu, b_glu, w_br_s5, w_br_fft, b_br_fft, w_out, b_out, ln1_g, ln1_b, w_router, b_router, w_up, b_up, w_down, b_down, ln2_g, ln2_b):
    assert x.shape == (1, N_TOK, D) and ctx.shape == (1, N_CTX, D) and w_ada.shape[0] == 1
    row = lambda v: v.reshape(1, -1).astype(F32)

    cc = jnp.concatenate([c.reshape(1, D), c_ctx.reshape(1, D), jnp.zeros((SUBLANES - 2, D), F32)], axis=0)
    ada = _ada(cc, w_ada[0], row(b_ada[0]))
    sh1, sc1, g1, sh2, sc2, g2 = (ada[0:1, k * D:(k + 1) * D] for k in range(6))
    sh1c, sc1c = ada[1:2, 0:D], ada[1:2, D:2 * D]

    emb_r, emb_c = _sincos_tables()
    st1, st2, fc = _dft_tables()
    lg, lb = row(ln_in_g), row(ln_in_b)

    w_in_bf = w_in[0].astype(BF16)
    b_s5 = row(b_in[0][:S5_WIDTH])
    b_fft8 = jnp.concatenate([row(b_in[0][S5_WIDTH:S5_WIDTH + FFT_WIDTH]),
                              jnp.zeros((SUBLANES - 1, FFT_WIDTH), F32)], axis=0)
    b_g = row(b_in[0][S5_WIDTH + FFT_WIDTH:])
    w_fc, b_fc = _fft_weights(w_in[0], b_fft8, fc)
    bcat = jnp.concatenate([b_s5, b_fc[0:1]], axis=1)

    x2 = x[0]
    p_t, xr, xi = _proj(x2, emb_r, emb_c, lg, lb, 1.0 + sc1, sh1, w_in_bf, w_fc, bcat)
    pc_t = _ctx_proj(ctx[0], lg, lb, 1.0 + sc1c, sh1c, w_in_bf[:, :S5_WIDTH], b_s5)

    b_c, a_q, a_p, trans, ctx_w = _s5_tables(
        s5_lambda_re[0], s5_lambda_im[0], s5_log_dt[0], s5_b_re[0], s5_b_im[0],
        s5_c_re[0], s5_c_im[0], s5_d[0])
    w_m, w_q, w_p = _s5_expand(b_c, a_q, a_p)
    y_t = _s5(p_t, pc_t, w_m, w_q, w_p, trans, ctx_w)

    yr, yi = _fft1(xr, xi, st1)
    zr = _fft2(yr, yi, st2)

    tri = (jnp.arange(TM)[:, None] < jnp.arange(TM)[None, :]).astype(BF16)
    br = jnp.broadcast_to(b_router[0].reshape(N_EXPERTS, 1), (N_EXPERTS, LANES))
    etri = (jnp.arange(N_EXPERTS)[:, None] > jnp.arange(N_EXPERTS)[None, :]).astype(BF16)
    wr_t = jnp.transpose(w_router[0])
    wr_hi = wr_t.astype(BF16)
    wr_split = jnp.stack([wr_hi, (wr_t - wr_hi.astype(F32)).astype(BF16)], axis=0)
    h1, u2, pos_t, gate_tok, counts = _mix(
        x2, emb_r, emb_c, lg, lb, 1.0 + sc1, sh1, w_in_bf, b_g, y_t, zr,
        w_glu[0].astype(BF16), row(b_glu[0]), w_br_s5[0].astype(BF16), w_br_fft[0].astype(BF16),
        row(b_br_fft[0]), w_out[0].astype(BF16), row(b_out[0]), g1, row(ln1_g[0]), row(ln1_b[0]),
        1.0 + sc2, sh2, wr_split, br, tri, etri)

    cnt = counts[:, :, 0].astype(jnp.int32)
    seg = (cnt + SEG_ALIGN - 1) // SEG_ALIGN * SEG_ALIGN
    seg_end = jnp.cumsum(seg, axis=1)
    seg_start = seg_end - seg
    padded = (jnp.sum(seg, axis=0) + BM - 1) // BM * BM
    pad_ends = jnp.cumsum(padded)
    seg_dest = (pad_ends - padded)[None, :] + jnp.cumsum(seg, axis=0) - seg
    chunk_row = jnp.arange(NCHK, dtype=jnp.int32) * SEG_ALIGN
    chunk_exp = jnp.minimum(jnp.sum(chunk_row[None, :, None] >= seg_end[:, None, :], axis=-1),
                            N_EXPERTS - 1)
    own = chunk_exp[:, :, None] == jnp.arange(N_EXPERTS, dtype=jnp.int32)[None, None, :]
    chunk_dest = (jnp.sum(jnp.where(own, (seg_dest - seg_start)[:, None, :], 0), axis=-1)
                  + chunk_row[None, :]).astype(jnp.int32).reshape(N_TILES, 1, NCHK)
    nchk = (seg_end[:, -1] // SEG_ALIGN).astype(jnp.int32)
    block_start = jnp.arange(N_BLOCKS_ALL, dtype=jnp.int32) * BM
    block_expert = jnp.minimum(jnp.sum(block_start[:, None] >= pad_ends[None, :], axis=1),
                               N_EXPERTS - 1).astype(jnp.int32)
    n_used = (pad_ends[-1:] // BM).astype(jnp.int32)
    opens = (block_start < pad_ends[-1]) & (
        block_expert != jnp.concatenate([jnp.full((1,), -1, jnp.int32), block_expert[:-1]]))
    run_id = jnp.where(opens, jnp.cumsum(opens.astype(jnp.int32)) - 1, -1).astype(jnp.int32)
    experts = jnp.arange(N_EXPERTS, dtype=jnp.int32)
    later = (experts[None, :] > block_expert[:, None]) & (padded[None, :] > 0)
    next_expert = jnp.min(jnp.where(later, experts[None, :], N_EXPERTS), axis=1)
    next_expert = jnp.where(next_expert < N_EXPERTS, next_expert, -1).astype(jnp.int32)

    spare = (ROWS + chunk_row)[None, None, :]
    chunk_table = jnp.concatenate(
        [spare,
         jnp.where(chunk_row[None, None, :] < (nchk * SEG_ALIGN)[:, None, None], chunk_dest, spare),
         spare], axis=0).astype(jnp.int32)
    fill_ends = (pad_ends - padded + jnp.sum(seg, axis=0)).astype(jnp.int32)
    buf = _dispatch(pad_ends.astype(jnp.int32), fill_ends, chunk_table, pos_t, u2)
    mine = block_expert[:, None] == experts[None, :]
    filled = jnp.sum(jnp.where(mine, fill_ends[None, :], 0), axis=1)
    valid = jnp.clip(filled - block_start, 0, BM).astype(jnp.int32)
    y_buf = _ffn(block_expert, n_used, run_id, next_expert, valid, buf, w_up[0],
                 b_up[0], w_down[0], b_down[0])
    out = _combine(chunk_table, y_buf, h1, gate_tok, g2, row(ln2_g[0]), row(ln2_b[0]))
    return out.reshape(1, N_TOK, D)
```

```python
import functools
import math

import jax
import jax.numpy as jnp
import numpy as np
from jax import lax
from jax.experimental import pallas as pl
from jax.experimental.pallas import tpu as pltpu

F32 = jnp.float32
BF16 = jnp.bfloat16
HI = lax.Precision.HIGHEST

D = 1024
N_TOK = 16384
N_CTX = 256
GRID_W = 64
S5_GROUP = 16
S5_GROUPS = 32
S5_STATE = 64
S5_WIDTH = 512
FFT_GROUPS = 4
FFT_DIM = 128
FFT_WIDTH = 512
N_EXPERTS = 32
TOP_K = 4
LN_EPS = 1e-5
ALPHA = 2.0 ** 0.25
SWIGLU_ALPHA = 1.702
SWIGLU_LIMIT = 7.0

LANES = 128
SUBLANES = 8
VMEM_LIMIT = 56 * 1024 * 1024

CH = 8
N_CHUNK = N_TOK // CH
N_CHUNK_CTX = N_CTX // CH
NSEG = SUBLANES
SEG = N_CHUNK // NSEG
SCAN_UNROLL = 4
GPT = LANES // S5_GROUP
NJ = S5_WIDTH // LANES
CL = CH * LANES
SW = 4 * GPT * S5_STATE

FN = 128
FB = 16

TM = 512
TM_PROJ = 1024
N_TILES = N_TOK // TM
BM = 1024
ZB = 512
FFN_HALF = 512
FFN_TAIL = 128
N_SLOTS = N_TOK * TOP_K
SEG_ALIGN = 2 * SUBLANES
CAP_BLOCK = 256
CAP = -(-(TOP_K * TM + N_EXPERTS * (SEG_ALIGN - 1)) // CAP_BLOCK) * CAP_BLOCK
NCHK = CAP // SEG_ALIGN
N_BLOCKS = -(-(N_SLOTS + N_TILES * N_EXPERTS * (SEG_ALIGN - 1)) // BM) + N_EXPERTS
ROWS = N_BLOCKS * BM
N_BLOCKS_ALL = N_BLOCKS + -(-CAP // BM)
ROWS_ALL = N_BLOCKS_ALL * BM


def _cparams(sem):
    return pltpu.CompilerParams(dimension_semantics=sem, vmem_limit_bytes=VMEM_LIMIT)


def _layer_norm(x, g, b):
    mu = jnp.mean(x, axis=-1, keepdims=True)
    xc = x - mu
    var = jnp.mean(xc * xc, axis=-1, keepdims=True)
    return xc * lax.rsqrt(var + LN_EPS) * g + b


def _sigmoid(x):
    return 0.5 * jnp.tanh(0.5 * x) + 0.5


def _ada_kernel(c_ref, w_ref, b_ref, o_ref):
    c = c_ref[...]
    s = c * _sigmoid(c)
    o_ref[...] = jnp.dot(s, w_ref[...], preferred_element_type=F32, precision=HI) + b_ref[...]


def _ada(cc, w_ada, b_ada):
    nb = 4
    wb = 6 * D // nb
    return pl.pallas_call(
        _ada_kernel,
        grid=(nb,),
        in_specs=[pl.BlockSpec((SUBLANES, D), lambda i: (0, 0)),
                  pl.BlockSpec((D, wb), lambda i: (0, i)),
                  pl.BlockSpec((1, wb), lambda i: (0, i))],
        out_specs=pl.BlockSpec((SUBLANES, wb), lambda i: (0, i)),
        out_shape=jax.ShapeDtypeStruct((SUBLANES, 6 * D), F32),
        compiler_params=_cparams(("parallel",)),
        name="ada",
    )(cc, w_ada, b_ada)


def _fftw_kernel(w_ref, b_ref, f_ref, wo_ref, bo_ref):
    f = f_ref[...]
    wo_ref[...] = jnp.dot(w_ref[...], f, preferred_element_type=F32, precision=HI).astype(BF16)
    bo_ref[...] = jnp.dot(b_ref[...], f, preferred_element_type=F32, precision=HI)


def _fft_weights(w_in, b_fft8, fc):
    def full(a):
        return pl.BlockSpec(a.shape, lambda i: (0,) * a.ndim)
    outs = (jax.ShapeDtypeStruct((D, 2 * FFT_WIDTH), BF16),
            jax.ShapeDtypeStruct((SUBLANES, 2 * FFT_WIDTH), F32))
    return pl.pallas_call(
        _fftw_kernel,
        grid=(1,),
        in_specs=[pl.BlockSpec((D, FFT_WIDTH), lambda i: (0, S5_WIDTH // FFT_WIDTH)),
                  full(b_fft8), full(fc)],
        out_specs=tuple(pl.BlockSpec(o.shape, lambda i: (0, 0)) for o in outs),
        out_shape=outs,
        compiler_params=_cparams(("arbitrary",)),
        name="fftw",
    )(w_in, b_fft8, fc)


def _pos_code(er_ref, ec_ref, tm):
    nr = tm // GRID_W
    er = er_ref[...]
    row = jnp.broadcast_to(er[:, None, :], (nr, GRID_W, D // 2)).reshape(tm, D // 2)
    col = jnp.concatenate([ec_ref[...]] * nr, axis=0)
    return jnp.concatenate([row, col], axis=-1)


def _to_chunk_major(val, scr_ref, out_ref, tm):
    for j in range(NJ):
        scr_ref[j] = val[:, j * LANES:(j + 1) * LANES]
    for t in range(CH):
        for j in range(NJ):
            piece = scr_ref[j, pl.ds(t, tm // CH, stride=CH), :]
            out_ref[t, :, j * LANES:(j + 1) * LANES] = piece.astype(out_ref.dtype)


def _proj_kernel(x_ref, er_ref, ec_ref, lg_ref, lb_ref, m_ref, s_ref, ws_ref, wf_ref, b_ref,
                 p_ref, xr_ref, xi_ref, scr_ref):
    x = x_ref[...] + _pos_code(er_ref, ec_ref, TM_PROJ)
    h = _layer_norm(x, lg_ref[...], lb_ref[...])
    u = (h * m_ref[...] + s_ref[...]).astype(BF16)
    p_s5 = jnp.dot(u, ws_ref[...], preferred_element_type=F32) + b_ref[:, :S5_WIDTH]
    _to_chunk_major(p_s5, scr_ref, p_ref, TM_PROJ)
    p_f = jnp.dot(u, wf_ref[...], preferred_element_type=F32) + b_ref[:, S5_WIDTH:]
    xr_ref[...] = p_f[:, :FFT_WIDTH].astype(BF16)
    xi_ref[...] = p_f[:, FFT_WIDTH:].astype(BF16)


def _proj(x, emb_r, emb_c, lg, lb, m1, s1, w_in, w_fc, bcat):
    nw = bcat.shape[1]
    vec = pl.BlockSpec((1, D), lambda i: (0, 0))
    return pl.pallas_call(
        _proj_kernel,
        grid=(N_TOK // TM_PROJ,),
        in_specs=[pl.BlockSpec((TM_PROJ, D), lambda i: (i, 0)),
                  pl.BlockSpec((TM_PROJ // GRID_W, D // 2), lambda i: (i, 0)),
                  pl.BlockSpec((GRID_W, D // 2), lambda i: (0, 0)),
                  vec, vec, vec, vec,
                  pl.BlockSpec((D, S5_WIDTH), lambda i: (0, 0)),
                  pl.BlockSpec((D, 2 * FFT_WIDTH), lambda i: (0, 0)),
                  pl.BlockSpec((1, nw), lambda i: (0, 0))],
        out_specs=(pl.BlockSpec((CH, TM_PROJ // CH, S5_WIDTH), lambda i: (0, i, 0)),
                   pl.BlockSpec((TM_PROJ, FFT_WIDTH), lambda i: (i, 0)),
                   pl.BlockSpec((TM_PROJ, FFT_WIDTH), lambda i: (i, 0))),
        out_shape=(jax.ShapeDtypeStruct((CH, N_CHUNK, S5_WIDTH), BF16),
                   jax.ShapeDtypeStruct((N_TOK, FFT_WIDTH), BF16),
                   jax.ShapeDtypeStruct((N_TOK, FFT_WIDTH), BF16)),
        scratch_shapes=[pltpu.VMEM((NJ, TM_PROJ, LANES), F32)],
        compiler_params=_cparams(("parallel",)),
        name="proj",
    )(x, emb_r, emb_c, lg, lb, m1, s1, w_in, w_fc, bcat)


def _ctx_proj_kernel(x_ref, lg_ref, lb_ref, m_ref, s_ref, w_ref, b_ref, p_ref, scr_ref):
    h = _layer_norm(x_ref[...], lg_ref[...], lb_ref[...])
    u = (h * m_ref[...] + s_ref[...]).astype(BF16)
    p = jnp.dot(u, w_ref[...], preferred_element_type=F32) + b_ref[...]
    _to_chunk_major(p, scr_ref, p_ref, N_CTX)


def _ctx_proj(ctx, lg, lb, m1, s1, w_s5, b_s5):
    return pl.pallas_call(
        _ctx_proj_kernel,
        out_shape=jax.ShapeDtypeStruct((CH, N_CHUNK_CTX, S5_WIDTH), BF16),
        scratch_shapes=[pltpu.VMEM((NJ, N_CTX, LANES), F32)],
        compiler_params=pltpu.CompilerParams(vmem_limit_bytes=VMEM_LIMIT),
        name="ctxproj",
    )(ctx, lg, lb, m1, s1, w_s5, b_s5)


def _s5_tables(lam_re, lam_im, log_dt, b_re, b_im, c_re, c_im, d_skip):
    dt = jnp.exp(log_dt)[..., None]
    zr = lam_re * dt
    zi = lam_im * dt

    def apow(m):
        m = jnp.asarray(m, F32)
        mag = jnp.exp(zr[..., None] * m)
        return mag * jnp.cos(zi[..., None] * m), mag * jnp.sin(zi[..., None] * m)

    a_re, a_im = apow(jnp.ones((1,), F32))
    a_re, a_im = a_re[..., 0], a_im[..., 0]
    den = lam_re * lam_re + lam_im * lam_im
    num_re = a_re - 1.0
    k_re = (num_re * lam_re + a_im * lam_im) / den
    k_im = (a_im * lam_re - num_re * lam_im) / den
    bb_re = k_re[..., None] * b_re - k_im[..., None] * b_im
    bb_im = k_re[..., None] * b_im + k_im[..., None] * b_re

    ks = jnp.arange(CH + 1, dtype=F32)
    pw_re, pw_im = apow(ks)
    kmag = jnp.exp(zr[:, :, None, :] * ks[None, None, :, None])
    pk_re = kmag * jnp.cos(zi[:, :, None, :] * ks[None, None, :, None])
    pk_im = kmag * jnp.sin(zi[:, :, None, :] * ks[None, None, :, None])
    bt_re, bt_im = jnp.swapaxes(b_re, 2, 3), jnp.swapaxes(b_im, 2, 3)
    bbt_re = k_re[:, :, None, :] * bt_re - k_im[:, :, None, :] * bt_im
    bbt_im = k_re[:, :, None, :] * bt_im + k_im[:, :, None, :] * bt_re

    ar, ai = pw_re[:, :, :, :CH, None], pw_im[:, :, :, :CH, None]
    cr = jnp.swapaxes(c_re, 2, 3)[:, :, :, None, :]
    ci = jnp.swapaxes(c_im, 2, 3)[:, :, :, None, :]
    ca = jnp.concatenate([cr * ar - ci * ai, -(cr * ai + ci * ar)], axis=2)
    ca = ca.reshape(2, S5_GROUPS, 2 * S5_STATE, CH * S5_GROUP)
    bbt = jnp.concatenate([bbt_re, bbt_im], axis=-1)
    taps = jnp.einsum('dghq,dgqn->dghn', bbt, ca, precision=HI)
    skip = (d_skip.reshape(S5_GROUPS, S5_GROUP, 1) * jnp.eye(S5_GROUP, dtype=F32)[None])
    taps = taps.at[0, :, :, :S5_GROUP].add(skip)
    b_c = jnp.transpose(taps.reshape(2, NJ, GPT, S5_GROUP, CH, S5_GROUP), (1, 4, 0, 2, 3, 5))
    b_c = b_c.reshape(NJ, 2 * CH, LANES, S5_GROUP)

    ef = (CH - 1) - jnp.arange(CH)
    eb = jnp.arange(CH)

    def q_part(d, e):
        pr = pk_re[d][:, e, None, :]
        pi = pk_im[d][:, e, None, :]
        br = bbt_re[d][:, None, :, :]
        bi = bbt_im[d][:, None, :, :]
        return pr * br - pi * bi, pr * bi + pi * br

    def q_rows(v):
        v = v.reshape(NJ, GPT, CH, S5_GROUP, S5_STATE)
        return jnp.transpose(v, (0, 2, 1, 3, 4)).reshape(NJ, CL, S5_STATE)
    a_q = jnp.stack([q_rows(v) for v in q_part(0, ef) + q_part(1, eb)], axis=0)

    of = jnp.arange(CH) + 1
    ob = CH - jnp.arange(CH)

    def p_part(d, e):
        pr = pw_re[d][..., e][:, :, :, None]
        pi = pw_im[d][..., e][:, :, :, None]
        return (ct_re[d] * pr - ct_im[d] * pi, -(ct_re[d] * pi + ct_im[d] * pr))
    ct_re = jnp.swapaxes(c_re, 2, 3)[:, :, :, None, :]
    ct_im = jnp.swapaxes(c_im, 2, 3)[:, :, :, None, :]
    a_p = jnp.stack([v.reshape(NJ, GPT * S5_STATE, CH * S5_GROUP)
                     for v in p_part(0, of) + p_part(1, ob)], axis=0)

    def lanes(v):
        return jnp.transpose(v.reshape(2, NJ, GPT * S5_STATE), (1, 0, 2))
    c_r, c_i = apow(jnp.full((1,), float(CH), F32))
    s_r, s_i = apow(jnp.full((1,), float(CH * SEG), F32))
    cr, ci, sr, si = (lanes(v[..., 0]) for v in (c_r, c_i, s_r, s_i))
    trans = jnp.stack([cr[:, 0], ci[:, 0], cr[:, 1], ci[:, 1],
                       sr[:, 0], si[:, 0], sr[:, 1], si[:, 1]], axis=1)

    cidx = jnp.arange(N_CHUNK_CTX, dtype=F32)
    wf_r, wf_i = apow(CH * (N_CHUNK_CTX - 1 - cidx))
    wb_r, wb_i = apow(CH * cidx)

    def ctx_lanes(v, d):
        return jnp.transpose(v[d].reshape(NJ, GPT * S5_STATE, N_CHUNK_CTX), (0, 2, 1))
    ctx_w = jnp.stack([ctx_lanes(wf_r, 0), ctx_lanes(wf_i, 0),
                       ctx_lanes(wb_r, 1), ctx_lanes(wb_i, 1)], axis=1)
    return b_c, a_q, a_p, trans, ctx_w


def _build_operators(bc_ref, aq_ref, ap_ref, c16_ref, c64_ref, cm_ref, wm_ref, wq_ref, wp_ref):
    def expand(a, c, row_shift, col_shift):
        w = jnp.dot(a, c, preferred_element_type=F32)
        rg = (lax.broadcasted_iota(jnp.int32, (w.shape[0], 1), 0) >> row_shift) & (GPT - 1)
        cg = (lax.broadcasted_iota(jnp.int32, (1, w.shape[1]), 1) >> col_shift) & (GPT - 1)
        return jnp.where(rg == cg, w, 0.0)

    blk = [expand(bc_ref[0, kd], c16_ref[...], 4, 4) for kd in range(2 * CH)]
    for t in range(CH):
        for u in range(CH):
            b = blk[2 * (u - t)] if u > t else blk[2 * (t - u) + 1] if u < t else blk[0] + blk[1]
            wm_ref[0, t * LANES:(t + 1) * LANES, u * LANES:(u + 1) * LANES] = b.astype(BF16)
    half = GPT * S5_STATE
    for s in range(4):
        wq_ref[0, :, s * half:(s + 1) * half] = expand(aq_ref[s, 0], c64_ref[...], 4, 6).astype(BF16)
        wp_ref[0, s * half:(s + 1) * half, :] = expand(ap_ref[s, 0], cm_ref[...], 6, 4).astype(BF16)


def _s5_kernel(p_ref, pc_ref, bc_ref, aq_ref, ap_ref, c16_ref, c64_ref, cm_ref, tr_ref, cw_ref,
               y_ref, v_ref, wm_ref, wq_ref, wp_ref):
    nq = NJ
    half = GPT * S5_STATE
    _build_operators(bc_ref, aq_ref, ap_ref, c16_ref, c64_ref, cm_ref, wm_ref, wq_ref, wp_ref)

    def chunk_rows(ref, r0, nrows):
        return jnp.concatenate([ref[t, pl.ds(r0, nrows), :] for t in range(CH)], axis=-1)

    def fill(k, c):
        r0 = pl.multiple_of(k * SEG, SEG)
        v = jnp.dot(chunk_rows(p_ref, r0, SEG), wq_ref[0], preferred_element_type=F32)
        for s in range(4 * nq):
            v_ref[s, pl.ds(k, SEG, stride=NSEG), :] = v[:, s * LANES:(s + 1) * LANES]
        return c
    lax.fori_loop(0, NSEG, fill, 0)

    vc = jnp.dot(chunk_rows(pc_ref, 0, N_CHUNK_CTX), wq_ref[0], preferred_element_type=F32)
    vfr, vfi, vbr, vbi = (vc[:, i * half:(i + 1) * half] for i in range(4))
    wfr, wfi, wbr, wbi = (cw_ref[0, i] for i in range(4))
    s0_fr = jnp.sum(wfr * vfr - wfi * vfi, axis=0, keepdims=True)
    s0_fi = jnp.sum(wfr * vfi + wfi * vfr, axis=0, keepdims=True)
    s0_br = jnp.sum(wbr * vbr - wbi * vbi, axis=0, keepdims=True)
    s0_bi = jnp.sum(wbr * vbi + wbi * vbr, axis=0, keepdims=True)

    tr = tr_ref[0]
    afr, afi, abr, abi = (jnp.broadcast_to(tr[i:i + 1], (NSEG, half)) for i in range(4))
    gfr, gfi, gbr, gbi = (tr[i:i + 1] for i in range(4, 8))

    def load_part(part, i):
        return jnp.concatenate(
            [v_ref[part * nq + q, pl.ds(pl.multiple_of(i * NSEG, NSEG), NSEG), :] for q in range(nq)],
            axis=-1)

    def store_part(part, i, val):
        for q in range(nq):
            v_ref[part * nq + q, pl.ds(pl.multiple_of(i * NSEG, NSEG), NSEG), :] = (
                val[:, q * LANES:(q + 1) * LANES])

    def step(i, carry, write):
        fr, fi, br, bi = carry
        ib = SEG - 1 - i
        ufr, ufi = load_part(0, i), load_part(1, i)
        ubr, ubi = load_part(2, ib), load_part(3, ib)
        if write:
            store_part(0, i, fr)
            store_part(1, i, fi)
            store_part(2, ib, br)
            store_part(3, ib, bi)
        return (afr * fr - afi * fi + ufr, afr * fi + afi * fr + ufi,
                abr * br - abi * bi + ubr, abr * bi + abi * br + ubi)

    zero = jnp.zeros((NSEG, half), F32)
    ffr, ffi, fbr, fbi = lax.fori_loop(0, SEG, functools.partial(step, write=False),
                                       (zero, zero, zero, zero), unroll=SCAN_UNROLL)

    rows_fr, rows_fi = [s0_fr], [s0_fi]
    for k in range(1, NSEG):
        pr, pi = rows_fr[-1], rows_fi[-1]
        rows_fr.append(gfr * pr - gfi * pi + ffr[k - 1:k])
        rows_fi.append(gfr * pi + gfi * pr + ffi[k - 1:k])
    rows_br, rows_bi = [s0_br], [s0_bi]
    for k in range(NSEG - 2, -1, -1):
        pr, pi = rows_br[0], rows_bi[0]
        rows_br.insert(0, gbr * pr - gbi * pi + fbr[k + 1:k + 2])
        rows_bi.insert(0, gbr * pi + gbi * pr + fbi[k + 1:k + 2])
    init = tuple(jnp.concatenate(r, axis=0) for r in (rows_fr, rows_fi, rows_br, rows_bi))

    lax.fori_loop(0, SEG, functools.partial(step, write=True), init, unroll=SCAN_UNROLL)

    def emit(k, c):
        r0 = pl.multiple_of(k * SEG, SEG)
        b = chunk_rows(p_ref, r0, SEG)
        sin = jnp.concatenate([v_ref[s, pl.ds(k, SEG, stride=NSEG), :] for s in range(4 * nq)], axis=-1)
        y = (jnp.dot(b, wm_ref[0], preferred_element_type=F32)
             + jnp.dot(sin.astype(BF16), wp_ref[0], preferred_element_type=F32))
        for t in range(CH):
            y_ref[t, pl.ds(r0, SEG), :] = y[:, t * LANES:(t + 1) * LANES].astype(y_ref.dtype)
        return c
    lax.fori_loop(0, NSEG, emit, 0)


def _s5(p_t, pc_t, b_c, a_q, a_p, trans, ctx_w):
    rep = np.ones((1, GPT))
    c16 = jnp.asarray(np.kron(rep, np.eye(S5_GROUP)), F32).astype(BF16)
    c64 = jnp.asarray(np.kron(rep, np.eye(S5_STATE)), F32).astype(BF16)
    c_m = jnp.asarray(np.kron(np.eye(CH), np.kron(rep, np.eye(S5_GROUP))), F32).astype(BF16)
    b_c, a_q, a_p = b_c.astype(BF16), a_q.astype(BF16), a_p.astype(BF16)
    half = GPT * S5_STATE
    return pl.pallas_call(
        _s5_kernel,
        grid=(NJ,),
        in_specs=[pl.BlockSpec((CH, N_CHUNK, LANES), lambda j: (0, 0, j)),
                  pl.BlockSpec((CH, N_CHUNK_CTX, LANES), lambda j: (0, 0, j)),
                  pl.BlockSpec((1, 2 * CH, LANES, S5_GROUP), lambda j: (j, 0, 0, 0)),
                  pl.BlockSpec((4, 1, CL, S5_STATE), lambda j: (0, j, 0, 0)),
                  pl.BlockSpec((4, 1, half, CH * S5_GROUP), lambda j: (0, j, 0, 0)),
                  pl.BlockSpec(c16.shape, lambda j: (0, 0)),
                  pl.BlockSpec(c64.shape, lambda j: (0, 0)),
                  pl.BlockSpec(c_m.shape, lambda j: (0, 0)),
                  pl.BlockSpec((1, SUBLANES, half), lambda j: (j, 0, 0)),
                  pl.BlockSpec((1, 4, N_CHUNK_CTX, half), lambda j: (j, 0, 0, 0))],
        out_specs=pl.BlockSpec((CH, N_CHUNK, LANES), lambda j: (0, 0, j)),
        out_shape=jax.ShapeDtypeStruct((CH, N_CHUNK, S5_WIDTH), BF16),
        scratch_shapes=[pltpu.VMEM((4 * NJ, N_CHUNK, LANES), F32),
                        pltpu.VMEM((1, CL, CL), BF16),
                        pltpu.VMEM((1, CL, SW), BF16),
                        pltpu.VMEM((1, SW, CL), BF16)],
        compiler_params=_cparams(("parallel",)),
        name="s5",
    )(p_t, pc_t, b_c, a_q, a_p, c16, c64, c_m, trans, ctx_w)


def _dft_tables():
    n = np.arange(FN)
    ang = 2.0 * np.pi * np.outer(n, n) / FN
    c, s = np.cos(ang), np.sin(ang)
    st1 = np.block([[c, s], [-s, c]])
    tw = 2.0 * np.pi * np.outer(n, n) / (FN * FN)
    wr, wi = np.cos(tw), -np.sin(tw)
    fr = c[None] * wr[:, None, :] + s[None] * wi[:, None, :]
    fi = c[None] * wi[:, None, :] - s[None] * wr[:, None, :]
    st2 = np.concatenate([fr, -fi], axis=-1)
    scale = 1.0 / math.sqrt(N_TOK * FFT_DIM)
    blk_c = np.kron(np.eye(FFT_GROUPS), c) * scale
    blk_s = np.kron(np.eye(FFT_GROUPS), s) * scale
    fc = np.concatenate([blk_c, -blk_s], axis=1)
    return (jnp.asarray(st1, F32).astype(BF16), jnp.asarray(st2, F32).astype(BF16), jnp.asarray(fc, F32))


FSL = FFT_WIDTH // LANES


FBH = FB // SUBLANES


def _block_to_slabs(blk, slab_ref, first, per_half):
    for bh in range(FBH):
        val = blk[:, bh * SUBLANES:(bh + 1) * SUBLANES, :].reshape(FN * SUBLANES, FFT_WIDTH)
        for s in range(FSL):
            slab_ref[bh * per_half + first + s] = val[:, s * LANES:(s + 1) * LANES]


def _slab_rows(b, first, per_half):
    return (b // SUBLANES) * per_half + first, pl.ds(b % SUBLANES, FN, stride=SUBLANES)


def _slabs_to_block(slab_ref, first, per_half):
    halves = []
    for bh in range(FBH):
        val = jnp.concatenate([slab_ref[bh * per_half + first + s] for s in range(FSL)], axis=-1)
        halves.append(val.reshape(FN, SUBLANES, FFT_WIDTH))
    return jnp.concatenate(halves, axis=1)


def _fft1_kernel(xr_ref, xi_ref, f_ref, yr_ref, yi_ref, in_ref, out_ref):
    _block_to_slabs(xr_ref[...].astype(F32), in_ref, 0, 2 * FSL)
    _block_to_slabs(xi_ref[...].astype(F32), in_ref, FSL, 2 * FSL)
    for b in range(FB):
        def part(first):
            base, rows = _slab_rows(b, first, 2 * FSL)
            return jnp.concatenate([in_ref[base + s, rows, :] for s in range(FSL)], axis=-1)
        xs = jnp.concatenate([part(0), part(FSL)], axis=0).astype(BF16)
        y = jnp.dot(f_ref[...], xs, preferred_element_type=F32)
        base, rows = _slab_rows(b, 0, 2 * FSL)
        for s in range(FSL):
            out_ref[base + s, rows, :] = y[:FN, s * LANES:(s + 1) * LANES]
            out_ref[base + FSL + s, rows, :] = y[FN:, s * LANES:(s + 1) * LANES]
    yr_ref[...] = _slabs_to_block(out_ref, 0, 2 * FSL).astype(BF16)
    yi_ref[...] = _slabs_to_block(out_ref, FSL, 2 * FSL).astype(BF16)


def _fft1(xr, xi, st1):
    spec = pl.BlockSpec((FN, FB, FFT_WIDTH), lambda i: (0, i, 0))
    slabs = pltpu.VMEM((FBH * 2 * FSL, FN * SUBLANES, LANES), F32)
    return pl.pallas_call(
        _fft1_kernel,
        grid=(FN // FB,),
        in_specs=[spec, spec, pl.BlockSpec((2 * FN, 2 * FN), lambda i: (0, 0))],
        out_specs=(spec, spec),
        out_shape=(jax.ShapeDtypeStruct((FN, FN, FFT_WIDTH), BF16),) * 2,
        scratch_shapes=[slabs, slabs],
        compiler_params=_cparams(("parallel",)),
        name="fft1",
    )(xr.reshape(FN, FN, FFT_WIDTH), xi.reshape(FN, FN, FFT_WIDTH), st1)


def _fft2_kernel(yr_ref, yi_ref, f_ref, z_ref, out_ref):
    for b in range(FB):
        ys = jnp.concatenate([yr_ref[b * FN:(b + 1) * FN, :], yi_ref[b * FN:(b + 1) * FN, :]], axis=0)
        z = jnp.dot(f_ref[b], ys, preferred_element_type=F32)
        base, rows = _slab_rows(b, 0, FSL)
        for s in range(FSL):
            out_ref[base + s, rows, :] = z[:, s * LANES:(s + 1) * LANES]
    z_ref[...] = _slabs_to_block(out_ref, 0, FSL).astype(BF16)


def _fft2(yr, yi, st2):
    rows = pl.BlockSpec((FB * FN, FFT_WIDTH), lambda i: (i, 0))
    z = pl.pallas_call(
        _fft2_kernel,
        grid=(FN // FB,),
        in_specs=[rows, rows, pl.BlockSpec((FB, FN, 2 * FN), lambda i: (i, 0, 0))],
        out_specs=pl.BlockSpec((FN, FB, FFT_WIDTH), lambda i: (0, i, 0)),
        out_shape=jax.ShapeDtypeStruct((FN, FN, FFT_WIDTH), BF16),
        scratch_shapes=[pltpu.VMEM((FBH * FSL, FN * SUBLANES, LANES), F32)],
        compiler_params=_cparams(("parallel",)),
        name="fft2",
    )(yr.reshape(N_TOK, FFT_WIDTH), yi.reshape(N_TOK, FFT_WIDTH), st2)
    return z.reshape(N_TOK, FFT_WIDTH)


def _gelu_tanh(x):
    return 0.5 * x * (1.0 + jnp.tanh(math.sqrt(2.0 / math.pi) * (x + 0.044715 * (x * x * x))))


def _mix_kernel(x_ref, er_ref, ec_ref, lg_ref, lb_ref, m1_ref, s1_ref, wgs_ref, wgf_ref, bg_ref,
                yt_ref, zr_ref, wglu_ref, bglu_ref, wbs_ref, wbf_ref, bbf_ref, wo_ref, bo_ref,
                g1_ref, l1g_ref, l1b_ref, m2_ref, s2_ref, wr_ref, br_ref, tri_ref, etri_ref,
                h1_ref, u2_ref, pos_ref, gate_ref, cnt_ref, scr_ref):
    pos = _pos_code(er_ref, ec_ref, TM)

    def front(r0, nr):
        rows = slice(r0, r0 + nr)
        h = _layer_norm(x_ref[rows, :] + pos[rows, :], lg_ref[...], lb_ref[...])
        u = (h * m1_ref[...] + s1_ref[...]).astype(BF16)

        c0, nc = r0 // CH, nr // CH
        for t in range(CH):
            for j in range(NJ):
                scr_ref[j, pl.ds(r0 + t, nc, stride=CH), :] = (
                    yt_ref[t, c0:c0 + nc, j * LANES:(j + 1) * LANES].astype(F32))
        ys = jnp.concatenate([scr_ref[j, rows, :] for j in range(NJ)], axis=-1)
        z = jnp.dot(_gelu_tanh(ys).astype(BF16), wglu_ref[...], preferred_element_type=F32) + bglu_ref[...]
        glu = (z[:, :S5_WIDTH] * _sigmoid(z[:, S5_WIDTH:])).astype(BF16)
        g_s5 = _sigmoid(jnp.dot(u, wgs_ref[...], preferred_element_type=F32) + bg_ref[:, :D])
        g_fft = _sigmoid(jnp.dot(u, wgf_ref[...], preferred_element_type=F32) + bg_ref[:, D:])
        y_s5 = jnp.dot(glu, wbs_ref[...], preferred_element_type=F32)
        y_fft = jnp.dot(zr_ref[rows, :], wbf_ref[...], preferred_element_type=F32) + bbf_ref[...]
        mixed = (g_s5 * y_s5 + g_fft * y_fft).astype(BF16)
        y = jnp.dot(mixed, wo_ref[...], preferred_element_type=F32) + bo_ref[...]
        h1 = _layer_norm(ALPHA * h + g1_ref[...] * y, l1g_ref[...], l1b_ref[...])
        h1_ref[rows, :] = h1
        u2 = h1 * m2_ref[...] + s2_ref[...]
        u2_ref[rows, :] = u2.astype(BF16)
        u_hi = u2.astype(BF16)
        u_lo = (u2 - u_hi.astype(F32)).astype(BF16)

        def nt(a, b):
            return lax.dot_general(a, b, (((1,), (1,)), ((), ())), preferred_element_type=F32)
        return nt(wr_ref[0], u_hi) + nt(wr_ref[0], u_lo) + nt(wr_ref[1], u_hi)

    logits = front(0, TM) + br_ref[:, 0:1]
    eidx = lax.broadcasted_iota(jnp.int32, (N_EXPERTS, TM), 0)
    vals, hots = [], []
    cur = logits
    for _k in range(TOP_K):
        m = jnp.max(cur, axis=0, keepdims=True)
        sel = jnp.min(jnp.where(cur == m, eidx, N_EXPERTS), axis=0, keepdims=True)
        hot = eidx == sel
        cur = jnp.where(hot, -jnp.inf, cur)
        vals.append(m)
        hots.append(hot)
    exps = [jnp.exp(v - vals[0]) for v in vals]
    den = exps[0] + exps[1] + exps[2] + exps[3]
    gate4 = jnp.concatenate([e / den for e in exps], axis=0)

    hot_sum = (hots[0] | hots[1] | hots[2] | hots[3]).astype(F32)
    before = jnp.dot(hot_sum.astype(BF16), tri_ref[...], preferred_element_type=F32)
    cnt = jnp.broadcast_to(jnp.sum(hot_sum, axis=1, keepdims=True), (N_EXPERTS, LANES))
    cnt8 = jnp.floor((cnt + (SEG_ALIGN - 1)) * (1.0 / SEG_ALIGN)) * SEG_ALIGN
    seg0 = jnp.dot(etri_ref[...], cnt8.astype(BF16), preferred_element_type=F32)
    tot = seg0[:, 0:1] + before
    pos4 = jnp.concatenate(
        [jnp.sum(jnp.where(hk, tot, 0.0), axis=0, keepdims=True) for hk in hots], axis=0)
    pos_ref[...] = pos4.astype(jnp.int32)
    cnt_ref[0] = cnt

    gpad = jnp.concatenate([gate4, pos4, jnp.zeros((LANES - 2 * TOP_K, TM), F32)], axis=0)
    gate_ref[...] = gpad.T


def _mix(x, emb_r, emb_c, lg, lb, m1, s1, w_in, bg, y_t, zr, wglu, bglu, wbs, wbf, bbf, wo, bo,
         g1, l1g, l1b, m2, s2, wr_t, br, tri, etri):
    gate_cols = (S5_WIDTH + FFT_WIDTH) // D
    vec = pl.BlockSpec((1, D), lambda i: (0, 0))

    def full(a):
        return pl.BlockSpec(a.shape, lambda i: (0,) * a.ndim)
    return pl.pallas_call(
        _mix_kernel,
        grid=(N_TOK // TM,),
        in_specs=[pl.BlockSpec((TM, D), lambda i: (i, 0)),
                  pl.BlockSpec((TM // GRID_W, D // 2), lambda i: (i, 0)),
                  pl.BlockSpec((GRID_W, D // 2), lambda i: (0, 0)),
                  vec, vec, vec, vec,
                  pl.BlockSpec((D, D), lambda i: (0, gate_cols)),
                  pl.BlockSpec((D, D), lambda i: (0, gate_cols + 1)), full(bg),
                  pl.BlockSpec((CH, TM // CH, S5_WIDTH), lambda i: (0, i, 0)),
                  pl.BlockSpec((TM, FFT_WIDTH), lambda i: (i, 0)),
                  full(wglu), full(bglu), full(wbs), full(wbf), full(bbf), full(wo), full(bo),
                  vec, vec, vec, vec, vec, full(wr_t), full(br), full(tri), full(etri)],
        out_specs=(pl.BlockSpec((TM, D), lambda i: (i, 0)),
                   pl.BlockSpec((TM, D), lambda i: (i, 0)),
                   pl.BlockSpec((TOP_K, TM), lambda i: (0, i)),
                   pl.BlockSpec((TM, LANES), lambda i: (i, 0)),
                   pl.BlockSpec((1, N_EXPERTS, LANES), lambda i: (i, 0, 0))),
        out_shape=(jax.ShapeDtypeStruct((N_TOK, D), F32),
                   jax.ShapeDtypeStruct((N_TOK, D), BF16),
                   jax.ShapeDtypeStruct((TOP_K, N_TOK), jnp.int32),
                   jax.ShapeDtypeStruct((N_TOK, LANES), F32),
                   jax.ShapeDtypeStruct((N_TILES, N_EXPERTS, LANES), F32)),
        scratch_shapes=[pltpu.VMEM((NJ, TM, LANES), F32)],
        compiler_params=_cparams(("parallel",)),
        name="mix",
    )(x, emb_r, emb_c, lg, lb, m1, s1, w_in, w_in, bg, y_t, zr, wglu, bglu, wbs, wbf, bbf, wo, bo,
      g1, l1g, l1b, m2, s2, wr_t, br, tri, etri)


def _on_parity(i, fn):
    @pl.when(i % 2 == 0)
    def _():
        fn(0)

    @pl.when(i % 2 == 1)
    def _():
        fn(1)


def _dispatch_kernel(pend_ref, fill_ref, dprev_ref, dest_ref, pos_ref, u_ref, buf_ref,
                     sorted_ref, zero_ref, zsem, tsem, sems):
    i = pl.program_id(0)

    units = BM // ZB

    def clear_copy(start, sem):
        return pltpu.make_async_copy(
            zero_ref, buf_ref.at[pl.ds(pl.multiple_of(start, ZB), ZB)], sem)

    def unused_units(fn):
        def unit(b, c):
            fn(clear_copy(b * ZB, tsem))
            return c
        lax.fori_loop(pend_ref[N_EXPERTS - 1] // ZB, N_BLOCKS * units, unit, 0)

    @pl.when(i == 0)
    def _():
        zero_ref[...] = jnp.zeros_like(zero_ref)

        def each(fn):
            def expert(e, c):
                for z in range(units):
                    start = pend_ref[e] - (z + 1) * ZB

                    @pl.when(start + ZB > fill_ref[e])
                    def _():
                        fn(clear_copy(start, zsem))
                return c
            lax.fori_loop(0, N_EXPERTS, expert, 0)
            for b in range(N_BLOCKS * units, N_BLOCKS_ALL * units):
                fn(clear_copy(b * ZB, zsem))
        each(lambda cp: cp.start())
        unused_units(lambda cp: cp.start())
        each(lambda cp: cp.wait())

        sorted_ref[...] = jnp.zeros_like(sorted_ref)

    def chunk_copy(slot, table_ref, j):
        dst = pl.multiple_of(table_ref[0, 0, j], SEG_ALIGN)
        return pltpu.make_async_copy(sorted_ref.at[slot, pl.ds(j * SEG_ALIGN, SEG_ALIGN)],
                                     buf_ref.at[pl.ds(dst, SEG_ALIGN)], sems.at[slot])

    def drain(slot):
        pltpu.make_async_copy(sorted_ref.at[slot], buf_ref.at[pl.ds(0, CAP)], sems.at[slot]).wait()

    def run(slot):
        pos = pos_ref[...]
        u = u_ref[...]
        n_rb = CAP // CAP_BLOCK
        per_rb = NCHK // (n_rb // 2)
        for rb in range(n_rb):
            for j in range(rb * per_rb, min((rb + 1) * per_rb, NCHK)):
                chunk_copy(1 - slot, dprev_ref, j).start()
            rows = lax.broadcasted_iota(jnp.int32, (CAP_BLOCK, TM), 0) + rb * CAP_BLOCK
            hit = rows == pos[0:1]
            for k in range(1, TOP_K):
                hit = hit | (rows == pos[k:k + 1])
            onehot = jnp.where(hit, 1.0, 0.0).astype(BF16)
            sorted_ref[slot, rb * CAP_BLOCK:(rb + 1) * CAP_BLOCK, :] = jnp.dot(
                onehot, u, preferred_element_type=F32).astype(BF16)
        drain(1 - slot)

        @pl.when(i == N_TILES - 1)
        def _():
            def issue(j, c):
                chunk_copy(slot, dest_ref, j).start()
                return c
            lax.fori_loop(0, NCHK, issue, 0)
            drain(slot)
            unused_units(lambda cp: cp.wait())
    _on_parity(i, run)


def _dispatch(pad_ends, padded, chunk_table, pos_t, u2):
    return pl.pallas_call(
        _dispatch_kernel,
        grid_spec=pltpu.PrefetchScalarGridSpec(
            num_scalar_prefetch=2,
            grid=(N_TILES,),
            in_specs=[pl.BlockSpec((1, 1, NCHK), lambda i, a, b: (i, 0, 0), memory_space=pltpu.SMEM),
                      pl.BlockSpec((1, 1, NCHK), lambda i, a, b: (i + 1, 0, 0), memory_space=pltpu.SMEM),
                      pl.BlockSpec((TOP_K, TM), lambda i, a, b: (0, i)),
                      pl.BlockSpec((TM, D), lambda i, a, b: (i, 0))],
            out_specs=pl.BlockSpec(memory_space=pl.ANY),
            scratch_shapes=[pltpu.VMEM((2, CAP, D), BF16),
                            pltpu.VMEM((ZB, D), BF16),
                            pltpu.SemaphoreType.DMA(()),
                            pltpu.SemaphoreType.DMA(()),
                            pltpu.SemaphoreType.DMA((2,))]),
        out_shape=jax.ShapeDtypeStruct((ROWS_ALL, D), BF16),
        compiler_params=_cparams(("arbitrary",)),
        name="dispatch",
    )(pad_ends, padded, chunk_table, chunk_table, pos_t, u2)


def _ffn_kernel(be_ref, nu_ref, run_ref, nxt_ref, valid_ref, x_ref, wu_hbm, bu_ref, wd_hbm, bd_ref,
                y_ref, wu_ref, wd_ref, wub_ref, wdb_ref, sems):
    i = pl.program_id(0)
    used = i < nu_ref[0]

    def weight_copies(e, slot):
        return (pltpu.make_async_copy(wu_hbm.at[e], wu_ref.at[slot], sems.at[slot]),
                pltpu.make_async_copy(wd_hbm.at[e], wd_ref.at[slot], sems.at[slot]))

    @pl.when(used)
    def _():
        run = run_ref[i]

        @pl.when(run >= 0)
        def _():
            def open_run(slot):
                @pl.when(run == 0)
                def _():
                    for cp in weight_copies(be_ref[i], slot):
                        cp.start()

                @pl.when(nxt_ref[i] >= 0)
                def _():
                    for cp in weight_copies(nxt_ref[i], 1 - slot):
                        cp.start()
                for cp in weight_copies(be_ref[i], slot):
                    cp.wait()
                wub_ref[...] = wu_ref[slot].astype(BF16)
                wdb_ref[...] = wd_ref[slot].astype(BF16)
            _on_parity(run, open_run)

        def expert_rows(r0, nr):
            rows = slice(r0, r0 + nr)
            e = be_ref[i]
            h = (jnp.dot(x_ref[rows, :], wub_ref[...], preferred_element_type=F32)
                 + bu_ref[pl.ds(e, 1), :])
            h_glu = jnp.minimum(h[:, :D], SWIGLU_LIMIT)
            h_lin = jnp.clip(h[:, D:], -SWIGLU_LIMIT, SWIGLU_LIMIT)
            act = (h_glu * _sigmoid(SWIGLU_ALPHA * h_glu) * (h_lin + 1.0)).astype(BF16)
            y_ref[rows, :] = (jnp.dot(act, wdb_ref[...], preferred_element_type=F32)
                              + bd_ref[pl.ds(e, 1), :]).astype(BF16)

        valid = valid_ref[i]

        @pl.when(valid == BM)
        def _():
            expert_rows(0, BM)

        @pl.when(valid < BM)
        def _():
            for h0 in range(0, BM, FFN_HALF):
                @pl.when(valid >= h0 + FFN_HALF)
                def _(h0=h0):
                    expert_rows(h0, FFN_HALF)

                @pl.when(valid < h0 + FFN_HALF)
                def _(h0=h0):
                    for r0 in range(h0, h0 + FFN_HALF, FFN_TAIL):
                        @pl.when(r0 < valid)
                        def _(r0=r0):
                            expert_rows(r0, FFN_TAIL)

                        @pl.when(r0 >= valid)
                        def _(r0=r0):
                            y_ref[r0:r0 + FFN_TAIL, :] = jnp.zeros((FFN_TAIL, D), BF16)

    @pl.when(jnp.logical_not(used))
    def _():
        y_ref[...] = jnp.zeros_like(y_ref)


def _ffn(block_expert, n_used, run_id, next_expert, valid, buf, w_up, b_up, w_down, b_down):
    def blk(i, be, nu, *_):
        return jnp.minimum(i, nu[0] - 1)
    return pl.pallas_call(
        _ffn_kernel,
        grid_spec=pltpu.PrefetchScalarGridSpec(
            num_scalar_prefetch=5,
            grid=(N_BLOCKS_ALL,),
            in_specs=[pl.BlockSpec((BM, D), lambda i, *s: (blk(i, *s), 0)),
                      pl.BlockSpec(memory_space=pl.ANY),
                      pl.BlockSpec((N_EXPERTS, 2 * D), lambda i, *s: (0, 0)),
                      pl.BlockSpec(memory_space=pl.ANY),
                      pl.BlockSpec((N_EXPERTS, D), lambda i, *s: (0, 0))],
            out_specs=pl.BlockSpec((BM, D), lambda i, *s: (i, 0)),
            scratch_shapes=[pltpu.VMEM((2, D, 2 * D), F32),
                            pltpu.VMEM((2, D, D), F32),
                            pltpu.VMEM((D, 2 * D), BF16),
                            pltpu.VMEM((D, D), BF16),
                            pltpu.SemaphoreType.DMA((2,))]),
        out_shape=jax.ShapeDtypeStruct((ROWS_ALL, D), BF16),
        compiler_params=_cparams(("arbitrary",)),
        name="ffn",
    )(block_expert, n_used, run_id, next_expert, valid, buf, w_up, b_up, w_down, b_down)


def _combine_kernel(dest_ref, dnext_ref, y_ref, h1_ref, gate_ref, g2_ref, lg_ref, lb_ref,
                    o_ref, sorted_ref, sems):
    i = pl.program_id(0)

    def chunk_copy(slot, table_ref, j):
        src = pl.multiple_of(table_ref[0, 0, j], SEG_ALIGN)
        return pltpu.make_async_copy(y_ref.at[pl.ds(src, SEG_ALIGN)],
                                     sorted_ref.at[slot, pl.ds(j * SEG_ALIGN, SEG_ALIGN)],
                                     sems.at[slot])

    def drain(slot):
        pltpu.make_async_copy(y_ref.at[pl.ds(0, CAP)], sorted_ref.at[slot], sems.at[slot]).wait()

    @pl.when(i == 0)
    def _():
        def issue(j, c):
            chunk_copy(0, dest_ref, j).start()
            return c
        lax.fori_loop(0, NCHK, issue, 0)

    def run(slot):
        drain(slot)

        gp = gate_ref[...]
        m = jnp.zeros((TM, D), F32)
        n_cb = CAP // CAP_BLOCK
        per_cb = NCHK // (n_cb // 2)
        for cb in range(n_cb):
            for j in range(cb * per_cb, min((cb + 1) * per_cb, NCHK)):
                chunk_copy(1 - slot, dnext_ref, j).start()
            cols = (lax.broadcasted_iota(jnp.int32, (TM, CAP_BLOCK), 1) + cb * CAP_BLOCK).astype(F32)
            g = jnp.where(cols == gp[:, TOP_K:TOP_K + 1], gp[:, 0:1], 0.0)
            for k in range(1, TOP_K):
                g = g + jnp.where(cols == gp[:, TOP_K + k:TOP_K + k + 1], gp[:, k:k + 1], 0.0)
            rows = sorted_ref[slot, cb * CAP_BLOCK:(cb + 1) * CAP_BLOCK, :]
            m = m + jnp.dot(g.astype(BF16), rows, preferred_element_type=F32)
        o_ref[...] = _layer_norm(ALPHA * h1_ref[...] + g2_ref[...] * m, lg_ref[...], lb_ref[...])

        @pl.when(i == N_TILES - 1)
        def _():
            drain(1 - slot)
    _on_parity(i, run)


def _combine(chunk_table, y_buf, h1, gate_tok, g2, lg, lb):
    vec = pl.BlockSpec((1, D), lambda i: (0, 0))
    return pl.pallas_call(
        _combine_kernel,
        grid_spec=pltpu.PrefetchScalarGridSpec(
            num_scalar_prefetch=0,
            grid=(N_TILES,),
            in_specs=[pl.BlockSpec((1, 1, NCHK), lambda i: (i + 1, 0, 0), memory_space=pltpu.SMEM),
                      pl.BlockSpec((1, 1, NCHK), lambda i: (i + 2, 0, 0), memory_space=pltpu.SMEM),
                      pl.BlockSpec(memory_space=pl.ANY),
                      pl.BlockSpec((TM, D), lambda i: (i, 0)),
                      pl.BlockSpec((TM, LANES), lambda i: (i, 0)),
                      vec, vec, vec],
            out_specs=pl.BlockSpec((TM, D), lambda i: (i, 0)),
            scratch_shapes=[pltpu.VMEM((2, CAP, D), BF16),
                            pltpu.SemaphoreType.DMA((2,))]),
        out_shape=jax.ShapeDtypeStruct((N_TOK, D), F32),
        compiler_params=_cparams(("arbitrary",)),
        name="combine",
    )(chunk_table, chunk_table, y_buf, h1, gate_tok, g2, lg, lb)


def _sincos_tables():
    q = D // 4
    omega = 1.0 / (10000.0 ** (np.arange(q) / q))

    def emb(n):
        ang = np.arange(n)[:, None] * omega[None, :]
        return jnp.asarray(np.concatenate([np.sin(ang), np.cos(ang)], axis=-1), F32)
    return emb(N_TOK // GRID_W), emb(GRID_W)


def kernel(x, c, ctx, c_ctx, ln_in_g, ln_in_b, w_ada, b_ada, w_in, b_in, s5_lambda_re, s5_lambda_im, s5_log_dt, s5_b_re, s5_b_im, s5_c_re, s5_c_im, s5_d, w_glu, b_glu, w_br_s5, w_br_fft, b_br_fft, w_out, b_out, ln1_g, ln1_b, w_router, b_router, w_up, b_up, w_down, b_down, ln2_g, ln2_b):
    assert x.shape == (1, N_TOK, D) and ctx.shape == (1, N_CTX, D) and w_ada.shape[0] == 1
    row = lambda v: v.reshape(1, -1).astype(F32)

    cc = jnp.concatenate([c.reshape(1, D), c_ctx.reshape(1, D), jnp.zeros((SUBLANES - 2, D), F32)], axis=0)
    ada = _ada(cc, w_ada[0], row(b_ada[0]))
    sh1, sc1, g1, sh2, sc2, g2 = (ada[0:1, k * D:(k + 1) * D] for k in range(6))
    sh1c, sc1c = ada[1:2, 0:D], ada[1:2, D:2 * D]

    emb_r, emb_c = _sincos_tables()
    st1, st2, fc = _dft_tables()
    lg, lb = row(ln_in_g), row(ln_in_b)

    w_in_bf = w_in[0].astype(BF16)
    b_s5 = row(b_in[0][:S5_WIDTH])
    b_fft8 = jnp.concatenate([row(b_in[0][S5_WIDTH:S5_WIDTH + FFT_WIDTH]),
                              jnp.zeros((SUBLANES - 1, FFT_WIDTH), F32)], axis=0)
    b_g = row(b_in[0][S5_WIDTH + FFT_WIDTH:])
    w_fc, b_fc = _fft_weights(w_in[0], b_fft8, fc)
    bcat = jnp.concatenate([b_s5, b_fc[0:1]], axis=1)

    x2 = x[0]
    p_t, xr, xi = _proj(x2, emb_r, emb_c, lg, lb, 1.0 + sc1, sh1, w_in_bf, w_fc, bcat)
    pc_t = _ctx_proj(ctx[0], lg, lb, 1.0 + sc1c, sh1c, w_in_bf[:, :S5_WIDTH], b_s5)

    b_c, a_q, a_p, trans, ctx_w = _s5_tables(
        s5_lambda_re[0], s5_lambda_im[0], s5_log_dt[0], s5_b_re[0], s5_b_im[0],
        s5_c_re[0], s5_c_im[0], s5_d[0])
    y_t = _s5(p_t, pc_t, b_c, a_q, a_p, trans, ctx_w)

    yr, yi = _fft1(xr, xi, st1)
    zr = _fft2(yr, yi, st2)

    tri = (jnp.arange(TM)[:, None] < jnp.arange(TM)[None, :]).astype(BF16)
    br = jnp.broadcast_to(b_router[0].reshape(N_EXPERTS, 1), (N_EXPERTS, LANES))
    etri = (jnp.arange(N_EXPERTS)[:, None] > jnp.arange(N_EXPERTS)[None, :]).astype(BF16)
    wr_t = jnp.transpose(w_router[0])
    wr_hi = wr_t.astype(BF16)
    wr_split = jnp.stack([wr_hi, (wr_t - wr_hi.astype(F32)).astype(BF16)], axis=0)
    h1, u2, pos_t, gate_tok, counts = _mix(
        x2, emb_r, emb_c, lg, lb, 1.0 + sc1, sh1, w_in_bf, b_g, y_t, zr,
        w_glu[0].astype(BF16), row(b_glu[0]), w_br_s5[0].astype(BF16), w_br_fft[0].astype(BF16),
        row(b_br_fft[0]), w_out[0].astype(BF16), row(b_out[0]), g1, row(ln1_g[0]), row(ln1_b[0]),
        1.0 + sc2, sh2, wr_split, br, tri, etri)

    cnt = counts[:, :, 0].astype(jnp.int32)
    seg = (cnt + SEG_ALIGN - 1) // SEG_ALIGN * SEG_ALIGN
    seg_end = jnp.cumsum(seg, axis=1)
    seg_start = seg_end - seg
    padded = (jnp.sum(seg, axis=0) + BM - 1) // BM * BM
    pad_ends = jnp.cumsum(padded)
    seg_dest = (pad_ends - padded)[None, :] + jnp.cumsum(seg, axis=0) - seg
    chunk_row = jnp.arange(NCHK, dtype=jnp.int32) * SEG_ALIGN
    chunk_exp = jnp.minimum(jnp.sum(chunk_row[None, :, None] >= seg_end[:, None, :], axis=-1),
                            N_EXPERTS - 1)
    own = chunk_exp[:, :, None] == jnp.arange(N_EXPERTS, dtype=jnp.int32)[None, None, :]
    chunk_dest = (jnp.sum(jnp.where(own, (seg_dest - seg_start)[:, None, :], 0), axis=-1)
                  + chunk_row[None, :]).astype(jnp.int32).reshape(N_TILES, 1, NCHK)
    nchk = (seg_end[:, -1] // SEG_ALIGN).astype(jnp.int32)
    block_start = jnp.arange(N_BLOCKS_ALL, dtype=jnp.int32) * BM
    block_expert = jnp.minimum(jnp.sum(block_start[:, None] >= pad_ends[None, :], axis=1),
                               N_EXPERTS - 1).astype(jnp.int32)
    n_used = (pad_ends[-1:] // BM).astype(jnp.int32)
    opens = (block_start < pad_ends[-1]) & (
        block_expert != jnp.concatenate([jnp.full((1,), -1, jnp.int32), block_expert[:-1]]))
    run_id = jnp.where(opens, jnp.cumsum(opens.astype(jnp.int32)) - 1, -1).astype(jnp.int32)
    experts = jnp.arange(N_EXPERTS, dtype=jnp.int32)
    later = (experts[None, :] > block_expert[:, None]) & (padded[None, :] > 0)
    next_expert = jnp.min(jnp.where(later, experts[None, :], N_EXPERTS), axis=1)
    next_expert = jnp.where(next_expert < N_EXPERTS, next_expert, -1).astype(jnp.int32)

    spare = (ROWS + chunk_row)[None, None, :]
    chunk_table = jnp.concatenate(
        [spare,
         jnp.where(chunk_row[None, None, :] < (nchk * SEG_ALIGN)[:, None, None], chunk_dest, spare),
         spare], axis=0).astype(jnp.int32)
    fill_ends = (pad_ends - padded + jnp.sum(seg, axis=0)).astype(jnp.int32)
    buf = _dispatch(pad_ends.astype(jnp.int32), fill_ends, chunk_table, pos_t, u2)
    mine = block_expert[:, None] == experts[None, :]
    filled = jnp.sum(jnp.where(mine, fill_ends[None, :], 0), axis=1)
    valid = jnp.clip(filled - block_start, 0, BM).astype(jnp.int32)
    y_buf = _ffn(block_expert, n_used, run_id, next_expert, valid, buf, w_up[0],
                 b_up[0], w_down[0], b_down[0])
    out = _combine(chunk_table, y_buf, h1, gate_tok, g2, row(ln2_g[0]), row(ln2_b[0]))
    return out.reshape(1, N_TOK, D)
```

```python
import functools
import math

import jax
import jax.numpy as jnp
import numpy as np
from jax import lax
from jax.experimental import pallas as pl
from jax.experimental.pallas import tpu as pltpu

F32 = jnp.float32
BF16 = jnp.bfloat16
HI = lax.Precision.HIGHEST

D = 1024
N_TOK = 16384
N_CTX = 256
GRID_W = 64
S5_GROUP = 16
S5_GROUPS = 32
S5_STATE = 64
S5_WIDTH = 512
FFT_GROUPS = 4
FFT_DIM = 128
FFT_WIDTH = 512
N_EXPERTS = 32
TOP_K = 4
LN_EPS = 1e-5
ALPHA = 2.0 ** 0.25
SWIGLU_ALPHA = 1.702
SWIGLU_LIMIT = 7.0

LANES = 128
SUBLANES = 8
VMEM_LIMIT = 56 * 1024 * 1024

CH = 8
N_CHUNK = N_TOK // CH
N_CHUNK_CTX = N_CTX // CH
NSEG = SUBLANES
SEG = N_CHUNK // NSEG
SCAN_UNROLL = 4
GPT = LANES // S5_GROUP
NJ = S5_WIDTH // LANES
CL = CH * LANES
SW = 4 * GPT * S5_STATE

FN = 128
FB = 16

TM = 512
TM_PROJ = 1024
N_TILES = N_TOK // TM
BM = 1024
ZB = 512
FFN_HALF = 512
FFN_TAIL = 128
N_SLOTS = N_TOK * TOP_K
SEG_ALIGN = 2 * SUBLANES
CAP_BLOCK = 256
CAP = -(-(TOP_K * TM + N_EXPERTS * (SEG_ALIGN - 1)) // CAP_BLOCK) * CAP_BLOCK
NCHK = CAP // SEG_ALIGN
N_BLOCKS = -(-(N_SLOTS + N_TILES * N_EXPERTS * (SEG_ALIGN - 1)) // BM) + N_EXPERTS
ROWS = N_BLOCKS * BM
N_BLOCKS_ALL = N_BLOCKS + -(-CAP // BM)
ROWS_ALL = N_BLOCKS_ALL * BM


def _cparams(sem):
    return pltpu.CompilerParams(dimension_semantics=sem, vmem_limit_bytes=VMEM_LIMIT)


def _layer_norm(x, g, b):
    mu = jnp.mean(x, axis=-1, keepdims=True)
    xc = x - mu
    var = jnp.mean(xc * xc, axis=-1, keepdims=True)
    return xc * lax.rsqrt(var + LN_EPS) * g + b


def _sigmoid(x):
    return 0.5 * jnp.tanh(0.5 * x) + 0.5


def _ada_kernel(c_ref, w_ref, b_ref, o_ref):
    c = c_ref[...]
    s = c * _sigmoid(c)
    o_ref[...] = jnp.dot(s, w_ref[...], preferred_element_type=F32, precision=HI) + b_ref[...]


def _ada(cc, w_ada, b_ada):
    nb = 4
    wb = 6 * D // nb
    return pl.pallas_call(
        _ada_kernel,
        grid=(nb,),
        in_specs=[pl.BlockSpec((SUBLANES, D), lambda i: (0, 0)),
                  pl.BlockSpec((D, wb), lambda i: (0, i)),
                  pl.BlockSpec((1, wb), lambda i: (0, i))],
        out_specs=pl.BlockSpec((SUBLANES, wb), lambda i: (0, i)),
        out_shape=jax.ShapeDtypeStruct((SUBLANES, 6 * D), F32),
        compiler_params=_cparams(("parallel",)),
        name="ada",
    )(cc, w_ada, b_ada)


def _fftw_kernel(w_ref, b_ref, f_ref, wo_ref, bo_ref):
    f = f_ref[...]
    wo_ref[...] = jnp.dot(w_ref[...], f, preferred_element_type=F32, precision=HI).astype(BF16)
    bo_ref[...] = jnp.dot(b_ref[...], f, preferred_element_type=F32, precision=HI)


def _fft_weights(w_in, b_fft8, fc):
    def full(a):
        return pl.BlockSpec(a.shape, lambda i: (0,) * a.ndim)
    outs = (jax.ShapeDtypeStruct((D, 2 * FFT_WIDTH), BF16),
            jax.ShapeDtypeStruct((SUBLANES, 2 * FFT_WIDTH), F32))
    return pl.pallas_call(
        _fftw_kernel,
        grid=(1,),
        in_specs=[pl.BlockSpec((D, FFT_WIDTH), lambda i: (0, S5_WIDTH // FFT_WIDTH)),
                  full(b_fft8), full(fc)],
        out_specs=tuple(pl.BlockSpec(o.shape, lambda i: (0, 0)) for o in outs),
        out_shape=outs,
        compiler_params=_cparams(("arbitrary",)),
        name="fftw",
    )(w_in, b_fft8, fc)


def _pos_code(er_ref, ec_ref, tm):
    nr = tm // GRID_W
    er = er_ref[...]
    row = jnp.broadcast_to(er[:, None, :], (nr, GRID_W, D // 2)).reshape(tm, D // 2)
    col = jnp.concatenate([ec_ref[...]] * nr, axis=0)
    return jnp.concatenate([row, col], axis=-1)


def _to_chunk_major(val, scr_ref, out_ref, tm):
    for j in range(NJ):
        scr_ref[j] = val[:, j * LANES:(j + 1) * LANES]
    for t in range(CH):
        for j in range(NJ):
            piece = scr_ref[j, pl.ds(t, tm // CH, stride=CH), :]
            out_ref[t, :, j * LANES:(j + 1) * LANES] = piece.astype(out_ref.dtype)


def _proj_kernel(x_ref, er_ref, ec_ref, lg_ref, lb_ref, m_ref, s_ref, ws_ref, wf_ref, b_ref,
                 p_ref, xr_ref, xi_ref, scr_ref):
    x = x_ref[...] + _pos_code(er_ref, ec_ref, TM_PROJ)
    h = _layer_norm(x, lg_ref[...], lb_ref[...])
    u = (h * m_ref[...] + s_ref[...]).astype(BF16)
    p_s5 = jnp.dot(u, ws_ref[...], preferred_element_type=F32) + b_ref[:, :S5_WIDTH]
    _to_chunk_major(p_s5, scr_ref, p_ref, TM_PROJ)
    p_f = jnp.dot(u, wf_ref[...], preferred_element_type=F32) + b_ref[:, S5_WIDTH:]
    xr_ref[...] = p_f[:, :FFT_WIDTH].astype(BF16)
    xi_ref[...] = p_f[:, FFT_WIDTH:].astype(BF16)


def _proj(x, emb_r, emb_c, lg, lb, m1, s1, w_in, w_fc, bcat):
    nw = bcat.shape[1]
    vec = pl.BlockSpec((1, D), lambda i: (0, 0))
    return pl.pallas_call(
        _proj_kernel,
        grid=(N_TOK // TM_PROJ,),
        in_specs=[pl.BlockSpec((TM_PROJ, D), lambda i: (i, 0)),
                  pl.BlockSpec((TM_PROJ // GRID_W, D // 2), lambda i: (i, 0)),
                  pl.BlockSpec((GRID_W, D // 2), lambda i: (0, 0)),
                  vec, vec, vec, vec,
                  pl.BlockSpec((D, S5_WIDTH), lambda i: (0, 0)),
                  pl.BlockSpec((D, 2 * FFT_WIDTH), lambda i: (0, 0)),
                  pl.BlockSpec((1, nw), lambda i: (0, 0))],
        out_specs=(pl.BlockSpec((CH, TM_PROJ // CH, S5_WIDTH), lambda i: (0, i, 0)),
                   pl.BlockSpec((TM_PROJ, FFT_WIDTH), lambda i: (i, 0)),
                   pl.BlockSpec((TM_PROJ, FFT_WIDTH), lambda i: (i, 0))),
        out_shape=(jax.ShapeDtypeStruct((CH, N_CHUNK, S5_WIDTH), BF16),
                   jax.ShapeDtypeStruct((N_TOK, FFT_WIDTH), BF16),
                   jax.ShapeDtypeStruct((N_TOK, FFT_WIDTH), BF16)),
        scratch_shapes=[pltpu.VMEM((NJ, TM_PROJ, LANES), F32)],
        compiler_params=_cparams(("parallel",)),
        name="proj",
    )(x, emb_r, emb_c, lg, lb, m1, s1, w_in, w_fc, bcat)


def _ctx_proj_kernel(x_ref, lg_ref, lb_ref, m_ref, s_ref, w_ref, b_ref, p_ref, scr_ref):
    h = _layer_norm(x_ref[...], lg_ref[...], lb_ref[...])
    u = (h * m_ref[...] + s_ref[...]).astype(BF16)
    p = jnp.dot(u, w_ref[...], preferred_element_type=F32) + b_ref[...]
    _to_chunk_major(p, scr_ref, p_ref, N_CTX)


def _ctx_proj(ctx, lg, lb, m1, s1, w_s5, b_s5):
    return pl.pallas_call(
        _ctx_proj_kernel,
        out_shape=jax.ShapeDtypeStruct((CH, N_CHUNK_CTX, S5_WIDTH), BF16),
        scratch_shapes=[pltpu.VMEM((NJ, N_CTX, LANES), F32)],
        compiler_params=pltpu.CompilerParams(vmem_limit_bytes=VMEM_LIMIT),
        name="ctxproj",
    )(ctx, lg, lb, m1, s1, w_s5, b_s5)


def _s5_tables(lam_re, lam_im, log_dt, b_re, b_im, c_re, c_im, d_skip):
    dt = jnp.exp(log_dt)[..., None]
    zr = lam_re * dt
    zi = lam_im * dt

    def apow(m):
        m = jnp.asarray(m, F32)
        mag = jnp.exp(zr[..., None] * m)
        return mag * jnp.cos(zi[..., None] * m), mag * jnp.sin(zi[..., None] * m)

    a_re, a_im = apow(jnp.ones((1,), F32))
    a_re, a_im = a_re[..., 0], a_im[..., 0]
    den = lam_re * lam_re + lam_im * lam_im
    num_re = a_re - 1.0
    k_re = (num_re * lam_re + a_im * lam_im) / den
    k_im = (a_im * lam_re - num_re * lam_im) / den
    bb_re = k_re[..., None] * b_re - k_im[..., None] * b_im
    bb_im = k_re[..., None] * b_im + k_im[..., None] * b_re

    ks = jnp.arange(CH + 1, dtype=F32)
    pw_re, pw_im = apow(ks)
    kmag = jnp.exp(zr[:, :, None, :] * ks[None, None, :, None])
    pk_re = kmag * jnp.cos(zi[:, :, None, :] * ks[None, None, :, None])
    pk_im = kmag * jnp.sin(zi[:, :, None, :] * ks[None, None, :, None])
    bt_re, bt_im = jnp.swapaxes(b_re, 2, 3), jnp.swapaxes(b_im, 2, 3)
    bbt_re = k_re[:, :, None, :] * bt_re - k_im[:, :, None, :] * bt_im
    bbt_im = k_re[:, :, None, :] * bt_im + k_im[:, :, None, :] * bt_re

    ar, ai = pw_re[:, :, :, :CH, None], pw_im[:, :, :, :CH, None]
    cr = jnp.swapaxes(c_re, 2, 3)[:, :, :, None, :]
    ci = jnp.swapaxes(c_im, 2, 3)[:, :, :, None, :]
    ca = jnp.concatenate([cr * ar - ci * ai, -(cr * ai + ci * ar)], axis=2)
    ca = ca.reshape(2, S5_GROUPS, 2 * S5_STATE, CH * S5_GROUP)
    bbt = jnp.concatenate([bbt_re, bbt_im], axis=-1)
    taps = jnp.einsum('dghq,dgqn->dghn', bbt, ca, precision=HI)
    skip = (d_skip.reshape(S5_GROUPS, S5_GROUP, 1) * jnp.eye(S5_GROUP, dtype=F32)[None])
    taps = taps.at[0, :, :, :S5_GROUP].add(skip)
    b_c = jnp.transpose(taps.reshape(2, NJ, GPT, S5_GROUP, CH, S5_GROUP), (1, 4, 0, 2, 3, 5))
    b_c = b_c.reshape(NJ, 2 * CH, LANES, S5_GROUP)

    ef = (CH - 1) - jnp.arange(CH)
    eb = jnp.arange(CH)

    def q_part(d, e):
        pr = pk_re[d][:, e, None, :]
        pi = pk_im[d][:, e, None, :]
        br = bbt_re[d][:, None, :, :]
        bi = bbt_im[d][:, None, :, :]
        return pr * br - pi * bi, pr * bi + pi * br

    def q_rows(v):
        v = v.reshape(NJ, GPT, CH, S5_GROUP, S5_STATE)
        return jnp.transpose(v, (0, 2, 1, 3, 4)).reshape(NJ, CL, S5_STATE)
    a_q = jnp.stack([q_rows(v) for v in q_part(0, ef) + q_part(1, eb)], axis=0)

    of = jnp.arange(CH) + 1
    ob = CH - jnp.arange(CH)

    def p_part(d, e):
        pr = pw_re[d][..., e][:, :, :, None]
        pi = pw_im[d][..., e][:, :, :, None]
        return (ct_re[d] * pr - ct_im[d] * pi, -(ct_re[d] * pi + ct_im[d] * pr))
    ct_re = jnp.swapaxes(c_re, 2, 3)[:, :, :, None, :]
    ct_im = jnp.swapaxes(c_im, 2, 3)[:, :, :, None, :]
    a_p = jnp.stack([v.reshape(NJ, GPT * S5_STATE, CH * S5_GROUP)
                     for v in p_part(0, of) + p_part(1, ob)], axis=0)

    def lanes(v):
        return jnp.transpose(v.reshape(2, NJ, GPT * S5_STATE), (1, 0, 2))
    c_r, c_i = apow(jnp.full((1,), float(CH), F32))
    s_r, s_i = apow(jnp.full((1,), float(CH * SEG), F32))
    cr, ci, sr, si = (lanes(v[..., 0]) for v in (c_r, c_i, s_r, s_i))
    trans = jnp.stack([cr[:, 0], ci[:, 0], cr[:, 1], ci[:, 1],
                       sr[:, 0], si[:, 0], sr[:, 1], si[:, 1]], axis=1)

    cidx = jnp.arange(N_CHUNK_CTX, dtype=F32)
    wf_r, wf_i = apow(CH * (N_CHUNK_CTX - 1 - cidx))
    wb_r, wb_i = apow(CH * cidx)

    def ctx_lanes(v, d):
        return jnp.transpose(v[d].reshape(NJ, GPT * S5_STATE, N_CHUNK_CTX), (0, 2, 1))
    ctx_w = jnp.stack([ctx_lanes(wf_r, 0), ctx_lanes(wf_i, 0),
                       ctx_lanes(wb_r, 1), ctx_lanes(wb_i, 1)], axis=1)
    return b_c, a_q, a_p, trans, ctx_w


def _build_operators(bc_ref, aq_ref, ap_ref, c16_ref, c64_ref, cm_ref, wm_ref, wq_ref, wp_ref):
    def expand(a, c, row_shift, col_shift):
        w = jnp.dot(a, c, preferred_element_type=F32)
        rg = (lax.broadcasted_iota(jnp.int32, (w.shape[0], 1), 0) >> row_shift) & (GPT - 1)
        cg = (lax.broadcasted_iota(jnp.int32, (1, w.shape[1]), 1) >> col_shift) & (GPT - 1)
        return jnp.where(rg == cg, w, 0.0)

    blk = [expand(bc_ref[0, kd], c16_ref[...], 4, 4) for kd in range(2 * CH)]
    for t in range(CH):
        for u in range(CH):
            b = blk[2 * (u - t)] if u > t else blk[2 * (t - u) + 1] if u < t else blk[0] + blk[1]
            wm_ref[0, t * LANES:(t + 1) * LANES, u * LANES:(u + 1) * LANES] = b.astype(BF16)
    half = GPT * S5_STATE
    for s in range(4):
        wq_ref[0, :, s * half:(s + 1) * half] = expand(aq_ref[s, 0], c64_ref[...], 4, 6).astype(BF16)
        wp_ref[0, s * half:(s + 1) * half, :] = expand(ap_ref[s, 0], cm_ref[...], 6, 4).astype(BF16)


def _s5_kernel(p_ref, pc_ref, bc_ref, aq_ref, ap_ref, c16_ref, c64_ref, cm_ref, tr_ref, cw_ref,
               y_ref, v_ref, wm_ref, wq_ref, wp_ref):
    nq = NJ
    half = GPT * S5_STATE
    _build_operators(bc_ref, aq_ref, ap_ref, c16_ref, c64_ref, cm_ref, wm_ref, wq_ref, wp_ref)

    def chunk_rows(ref, r0, nrows):
        return jnp.concatenate([ref[t, pl.ds(r0, nrows), :] for t in range(CH)], axis=-1)

    def fill(k, c):
        r0 = pl.multiple_of(k * SEG, SEG)
        v = jnp.dot(chunk_rows(p_ref, r0, SEG), wq_ref[0], preferred_element_type=F32)
        for s in range(4 * nq):
            v_ref[s, pl.ds(k, SEG, stride=NSEG), :] = v[:, s * LANES:(s + 1) * LANES]
        return c
    lax.fori_loop(0, NSEG, fill, 0)

    vc = jnp.dot(chunk_rows(pc_ref, 0, N_CHUNK_CTX), wq_ref[0], preferred_element_type=F32)
    vfr, vfi, vbr, vbi = (vc[:, i * half:(i + 1) * half] for i in range(4))
    wfr, wfi, wbr, wbi = (cw_ref[0, i] for i in range(4))
    s0_fr = jnp.sum(wfr * vfr - wfi * vfi, axis=0, keepdims=True)
    s0_fi = jnp.sum(wfr * vfi + wfi * vfr, axis=0, keepdims=True)
    s0_br = jnp.sum(wbr * vbr - wbi * vbi, axis=0, keepdims=True)
    s0_bi = jnp.sum(wbr * vbi + wbi * vbr, axis=0, keepdims=True)

    tr = tr_ref[0]
    afr, afi, abr, abi = (jnp.broadcast_to(tr[i:i + 1], (NSEG, half)) for i in range(4))
    gfr, gfi, gbr, gbi = (tr[i:i + 1] for i in range(4, 8))

    def load_part(part, i):
        return jnp.concatenate(
            [v_ref[part * nq + q, pl.ds(pl.multiple_of(i * NSEG, NSEG), NSEG), :] for q in range(nq)],
            axis=-1)

    def store_part(part, i, val):
        for q in range(nq):
            v_ref[part * nq + q, pl.ds(pl.multiple_of(i * NSEG, NSEG), NSEG), :] = (
                val[:, q * LANES:(q + 1) * LANES])

    def step(i, carry, write):
        fr, fi, br, bi = carry
        ib = SEG - 1 - i
        ufr, ufi = load_part(0, i), load_part(1, i)
        ubr, ubi = load_part(2, ib), load_part(3, ib)
        if write:
            store_part(0, i, fr)
            store_part(1, i, fi)
            store_part(2, ib, br)
            store_part(3, ib, bi)
        return (afr * fr - afi * fi + ufr, afr * fi + afi * fr + ufi,
                abr * br - abi * bi + ubr, abr * bi + abi * br + ubi)

    zero = jnp.zeros((NSEG, half), F32)
    ffr, ffi, fbr, fbi = lax.fori_loop(0, SEG, functools.partial(step, write=False),
                                       (zero, zero, zero, zero), unroll=SCAN_UNROLL)

    rows_fr, rows_fi = [s0_fr], [s0_fi]
    for k in range(1, NSEG):
        pr, pi = rows_fr[-1], rows_fi[-1]
        rows_fr.append(gfr * pr - gfi * pi + ffr[k - 1:k])
        rows_fi.append(gfr * pi + gfi * pr + ffi[k - 1:k])
    rows_br, rows_bi = [s0_br], [s0_bi]
    for k in range(NSEG - 2, -1, -1):
        pr, pi = rows_br[0], rows_bi[0]
        rows_br.insert(0, gbr * pr - gbi * pi + fbr[k + 1:k + 2])
        rows_bi.insert(0, gbr * pi + gbi * pr + fbi[k + 1:k + 2])
    init = tuple(jnp.concatenate(r, axis=0) for r in (rows_fr, rows_fi, rows_br, rows_bi))

    lax.fori_loop(0, SEG, functools.partial(step, write=True), init, unroll=SCAN_UNROLL)

    def emit(k, c):
        r0 = pl.multiple_of(k * SEG, SEG)
        b = chunk_rows(p_ref, r0, SEG)
        sin = jnp.concatenate([v_ref[s, pl.ds(k, SEG, stride=NSEG), :] for s in range(4 * nq)], axis=-1)
        y = (jnp.dot(b, wm_ref[0], preferred_element_type=F32)
             + jnp.dot(sin.astype(BF16), wp_ref[0], preferred_element_type=F32))
        for t in range(CH):
            y_ref[t, pl.ds(r0, SEG), :] = y[:, t * LANES:(t + 1) * LANES].astype(y_ref.dtype)
        return c
    lax.fori_loop(0, NSEG, emit, 0)


def _s5(p_t, pc_t, b_c, a_q, a_p, trans, ctx_w):
    rep = np.ones((1, GPT))
    c16 = jnp.asarray(np.kron(rep, np.eye(S5_GROUP)), F32).astype(BF16)
    c64 = jnp.asarray(np.kron(rep, np.eye(S5_STATE)), F32).astype(BF16)
    c_m = jnp.asarray(np.kron(np.eye(CH), np.kron(rep, np.eye(S5_GROUP))), F32).astype(BF16)
    b_c, a_q, a_p = b_c.astype(BF16), a_q.astype(BF16), a_p.astype(BF16)
    half = GPT * S5_STATE
    return pl.pallas_call(
        _s5_kernel,
        grid=(NJ,),
        in_specs=[pl.BlockSpec((CH, N_CHUNK, LANES), lambda j: (0, 0, j)),
                  pl.BlockSpec((CH, N_CHUNK_CTX, LANES), lambda j: (0, 0, j)),
                  pl.BlockSpec((1, 2 * CH, LANES, S5_GROUP), lambda j: (j, 0, 0, 0)),
                  pl.BlockSpec((4, 1, CL, S5_STATE), lambda j: (0, j, 0, 0)),
                  pl.BlockSpec((4, 1, half, CH * S5_GROUP), lambda j: (0, j, 0, 0)),
                  pl.BlockSpec(c16.shape, lambda j: (0, 0)),
                  pl.BlockSpec(c64.shape, lambda j: (0, 0)),
                  pl.BlockSpec(c_m.shape, lambda j: (0, 0)),
                  pl.BlockSpec((1, SUBLANES, half), lambda j: (j, 0, 0)),
                  pl.BlockSpec((1, 4, N_CHUNK_CTX, half), lambda j: (j, 0, 0, 0))],
        out_specs=pl.BlockSpec((CH, N_CHUNK, LANES), lambda j: (0, 0, j)),
        out_shape=jax.ShapeDtypeStruct((CH, N_CHUNK, S5_WIDTH), BF16),
        scratch_shapes=[pltpu.VMEM((4 * NJ, N_CHUNK, LANES), F32),
                        pltpu.VMEM((1, CL, CL), BF16),
                        pltpu.VMEM((1, CL, SW), BF16),
                        pltpu.VMEM((1, SW, CL), BF16)],
        compiler_params=_cparams(("parallel",)),
        name="s5",
    )(p_t, pc_t, b_c, a_q, a_p, c16, c64, c_m, trans, ctx_w)


def _dft_tables():
    n = np.arange(FN)
    ang = 2.0 * np.pi * np.outer(n, n) / FN
    c, s = np.cos(ang), np.sin(ang)
    st1 = np.block([[c, s], [-s, c]])
    tw = 2.0 * np.pi * np.outer(n, n) / (FN * FN)
    wr, wi = np.cos(tw), -np.sin(tw)
    fr = c[None] * wr[:, None, :] + s[None] * wi[:, None, :]
    fi = c[None] * wi[:, None, :] - s[None] * wr[:, None, :]
    st2 = np.concatenate([fr, -fi], axis=-1)
    scale = 1.0 / math.sqrt(N_TOK * FFT_DIM)
    blk_c = np.kron(np.eye(FFT_GROUPS), c) * scale
    blk_s = np.kron(np.eye(FFT_GROUPS), s) * scale
    fc = np.concatenate([blk_c, -blk_s], axis=1)
    return (jnp.asarray(st1, F32).astype(BF16), jnp.asarray(st2, F32).astype(BF16), jnp.asarray(fc, F32))


FSL = FFT_WIDTH // LANES


FBH = FB // SUBLANES


def _block_to_slabs(blk, slab_ref, first, per_half):
    for bh in range(FBH):
        val = blk[:, bh * SUBLANES:(bh + 1) * SUBLANES, :].reshape(FN * SUBLANES, FFT_WIDTH)
        for s in range(FSL):
            slab_ref[bh * per_half + first + s] = val[:, s * LANES:(s + 1) * LANES]


def _slab_rows(b, first, per_half):
    return (b // SUBLANES) * per_half + first, pl.ds(b % SUBLANES, FN, stride=SUBLANES)


def _slabs_to_block(slab_ref, first, per_half):
    halves = []
    for bh in range(FBH):
        val = jnp.concatenate([slab_ref[bh * per_half + first + s] for s in range(FSL)], axis=-1)
        halves.append(val.reshape(FN, SUBLANES, FFT_WIDTH))
    return jnp.concatenate(halves, axis=1)


def _fft1_kernel(xr_ref, xi_ref, f_ref, yr_ref, yi_ref, in_ref, out_ref):
    _block_to_slabs(xr_ref[...].astype(F32), in_ref, 0, 2 * FSL)
    _block_to_slabs(xi_ref[...].astype(F32), in_ref, FSL, 2 * FSL)
    for b in range(FB):
        def part(first):
            base, rows = _slab_rows(b, first, 2 * FSL)
            return jnp.concatenate([in_ref[base + s, rows, :] for s in range(FSL)], axis=-1)
        xs = jnp.concatenate([part(0), part(FSL)], axis=0).astype(BF16)
        y = jnp.dot(f_ref[...], xs, preferred_element_type=F32)
        base, rows = _slab_rows(b, 0, 2 * FSL)
        for s in range(FSL):
            out_ref[base + s, rows, :] = y[:FN, s * LANES:(s + 1) * LANES]
            out_ref[base + FSL + s, rows, :] = y[FN:, s * LANES:(s + 1) * LANES]
    yr_ref[...] = _slabs_to_block(out_ref, 0, 2 * FSL).astype(BF16)
    yi_ref[...] = _slabs_to_block(out_ref, FSL, 2 * FSL).astype(BF16)


def _fft1(xr, xi, st1):
    spec = pl.BlockSpec((FN, FB, FFT_WIDTH), lambda i: (0, i, 0))
    slabs = pltpu.VMEM((FBH * 2 * FSL, FN * SUBLANES, LANES), F32)
    return pl.pallas_call(
        _fft1_kernel,
        grid=(FN // FB,),
        in_specs=[spec, spec, pl.BlockSpec((2 * FN, 2 * FN), lambda i: (0, 0))],
        out_specs=(spec, spec),
        out_shape=(jax.ShapeDtypeStruct((FN, FN, FFT_WIDTH), BF16),) * 2,
        scratch_shapes=[slabs, slabs],
        compiler_params=_cparams(("parallel",)),
        name="fft1",
    )(xr.reshape(FN, FN, FFT_WIDTH), xi.reshape(FN, FN, FFT_WIDTH), st1)


def _fft2_kernel(yr_ref, yi_ref, f_ref, z_ref, out_ref):
    for b in range(FB):
        ys = jnp.concatenate([yr_ref[b * FN:(b + 1) * FN, :], yi_ref[b * FN:(b + 1) * FN, :]], axis=0)
        z = jnp.dot(f_ref[b], ys, preferred_element_type=F32)
        base, rows = _slab_rows(b, 0, FSL)
        for s in range(FSL):
            out_ref[base + s, rows, :] = z[:, s * LANES:(s + 1) * LANES]
    z_ref[...] = _slabs_to_block(out_ref, 0, FSL).astype(BF16)


def _fft2(yr, yi, st2):
    rows = pl.BlockSpec((FB * FN, FFT_WIDTH), lambda i: (i, 0))
    z = pl.pallas_call(
        _fft2_kernel,
        grid=(FN // FB,),
        in_specs=[rows, rows, pl.BlockSpec((FB, FN, 2 * FN), lambda i: (i, 0, 0))],
        out_specs=pl.BlockSpec((FN, FB, FFT_WIDTH), lambda i: (0, i, 0)),
        out_shape=jax.ShapeDtypeStruct((FN, FN, FFT_WIDTH), BF16),
        scratch_shapes=[pltpu.VMEM((FBH * FSL, FN * SUBLANES, LANES), F32)],
        compiler_params=_cparams(("parallel",)),
        name="fft2",
    )(yr.reshape(N_TOK, FFT_WIDTH), yi.reshape(N_TOK, FFT_WIDTH), st2)
    return z.reshape(N_TOK, FFT_WIDTH)


def _gelu_tanh(x):
    return 0.5 * x * (1.0 + jnp.tanh(math.sqrt(2.0 / math.pi) * (x + 0.044715 * (x * x * x))))


def _mix_kernel(x_ref, er_ref, ec_ref, lg_ref, lb_ref, m1_ref, s1_ref, wgs_ref, wgf_ref, bg_ref,
                yt_ref, zr_ref, wglu_ref, bglu_ref, wbs_ref, wbf_ref, bbf_ref, wo_ref, bo_ref,
                g1_ref, l1g_ref, l1b_ref, m2_ref, s2_ref, wr_ref, br_ref, tri_ref, etri_ref,
                h1_ref, u2_ref, pos_ref, gate_ref, cnt_ref, scr_ref):
    pos = _pos_code(er_ref, ec_ref, TM)

    def front(r0, nr):
        rows = slice(r0, r0 + nr)
        h = _layer_norm(x_ref[rows, :] + pos[rows, :], lg_ref[...], lb_ref[...])
        u = (h * m1_ref[...] + s1_ref[...]).astype(BF16)

        c0, nc = r0 // CH, nr // CH
        for t in range(CH):
            for j in range(NJ):
                scr_ref[j, pl.ds(r0 + t, nc, stride=CH), :] = (
                    yt_ref[t, c0:c0 + nc, j * LANES:(j + 1) * LANES].astype(F32))
        ys = jnp.concatenate([scr_ref[j, rows, :] for j in range(NJ)], axis=-1)
        z = jnp.dot(_gelu_tanh(ys).astype(BF16), wglu_ref[...], preferred_element_type=F32) + bglu_ref[...]
        glu = (z[:, :S5_WIDTH] * _sigmoid(z[:, S5_WIDTH:])).astype(BF16)
        g_s5 = _sigmoid(jnp.dot(u, wgs_ref[...], preferred_element_type=F32) + bg_ref[:, :D])
        g_fft = _sigmoid(jnp.dot(u, wgf_ref[...], preferred_element_type=F32) + bg_ref[:, D:])
        y_s5 = jnp.dot(glu, wbs_ref[...], preferred_element_type=F32)
        y_fft = jnp.dot(zr_ref[rows, :], wbf_ref[...], preferred_element_type=F32) + bbf_ref[...]
        mixed = (g_s5 * y_s5 + g_fft * y_fft).astype(BF16)
        y = jnp.dot(mixed, wo_ref[...], preferred_element_type=F32) + bo_ref[...]
        h1 = _layer_norm(ALPHA * h + g1_ref[...] * y, l1g_ref[...], l1b_ref[...])
        h1_ref[rows, :] = h1
        u2 = h1 * m2_ref[...] + s2_ref[...]
        u2_ref[rows, :] = u2.astype(BF16)
        u_hi = u2.astype(BF16)
        u_lo = (u2 - u_hi.astype(F32)).astype(BF16)

        def nt(a, b):
            return lax.dot_general(a, b, (((1,), (1,)), ((), ())), preferred_element_type=F32)
        return nt(wr_ref[0], u_hi) + nt(wr_ref[0], u_lo) + nt(wr_ref[1], u_hi)

    logits = front(0, TM) + br_ref[:, 0:1]
    eidx = lax.broadcasted_iota(jnp.int32, (N_EXPERTS, TM), 0)
    vals, hots = [], []
    cur = logits
    for _k in range(TOP_K):
        m = jnp.max(cur, axis=0, keepdims=True)
        sel = jnp.min(jnp.where(cur == m, eidx, N_EXPERTS), axis=0, keepdims=True)
        hot = eidx == sel
        cur = jnp.where(hot, -jnp.inf, cur)
        vals.append(m)
        hots.append(hot)
    exps = [jnp.exp(v - vals[0]) for v in vals]
    den = exps[0] + exps[1] + exps[2] + exps[3]
    gate4 = jnp.concatenate([e / den for e in exps], axis=0)

    hot_sum = (hots[0] | hots[1] | hots[2] | hots[3]).astype(F32)
    before = jnp.dot(hot_sum.astype(BF16), tri_ref[...], preferred_element_type=F32)
    cnt = jnp.broadcast_to(jnp.sum(hot_sum, axis=1, keepdims=True), (N_EXPERTS, LANES))
    cnt8 = jnp.floor((cnt + (SEG_ALIGN - 1)) * (1.0 / SEG_ALIGN)) * SEG_ALIGN
    seg0 = jnp.dot(etri_ref[...], cnt8.astype(BF16), preferred_element_type=F32)
    tot = seg0[:, 0:1] + before
    pos4 = jnp.concatenate(
        [jnp.sum(jnp.where(hk, tot, 0.0), axis=0, keepdims=True) for hk in hots], axis=0)
    pos_ref[...] = pos4.astype(jnp.int32)
    cnt_ref[0] = cnt

    gate_ref[...] = gate4


def _mix(x, emb_r, emb_c, lg, lb, m1, s1, w_in, bg, y_t, zr, wglu, bglu, wbs, wbf, bbf, wo, bo,
         g1, l1g, l1b, m2, s2, wr_t, br, tri, etri):
    gate_cols = (S5_WIDTH + FFT_WIDTH) // D
    vec = pl.BlockSpec((1, D), lambda i: (0, 0))

    def full(a):
        return pl.BlockSpec(a.shape, lambda i: (0,) * a.ndim)
    return pl.pallas_call(
        _mix_kernel,
        grid=(N_TOK // TM,),
        in_specs=[pl.BlockSpec((TM, D), lambda i: (i, 0)),
                  pl.BlockSpec((TM // GRID_W, D // 2), lambda i: (i, 0)),
                  pl.BlockSpec((GRID_W, D // 2), lambda i: (0, 0)),
                  vec, vec, vec, vec,
                  pl.BlockSpec((D, D), lambda i: (0, gate_cols)),
                  pl.BlockSpec((D, D), lambda i: (0, gate_cols + 1)), full(bg),
                  pl.BlockSpec((CH, TM // CH, S5_WIDTH), lambda i: (0, i, 0)),
                  pl.BlockSpec((TM, FFT_WIDTH), lambda i: (i, 0)),
                  full(wglu), full(bglu), full(wbs), full(wbf), full(bbf), full(wo), full(bo),
                  vec, vec, vec, vec, vec, full(wr_t), full(br), full(tri), full(etri)],
        out_specs=(pl.BlockSpec((TM, D), lambda i: (i, 0)),
                   pl.BlockSpec((TM, D), lambda i: (i, 0)),
                   pl.BlockSpec((TOP_K, TM), lambda i: (0, i)),
                   pl.BlockSpec((TOP_K, TM), lambda i: (0, i)),
                   pl.BlockSpec((1, N_EXPERTS, LANES), lambda i: (i, 0, 0))),
        out_shape=(jax.ShapeDtypeStruct((N_TOK, D), F32),
                   jax.ShapeDtypeStruct((N_TOK, D), BF16),
                   jax.ShapeDtypeStruct((TOP_K, N_TOK), jnp.int32),
                   jax.ShapeDtypeStruct((TOP_K, N_TOK), F32),
                   jax.ShapeDtypeStruct((N_TILES, N_EXPERTS, LANES), F32)),
        scratch_shapes=[pltpu.VMEM((NJ, TM, LANES), F32)],
        compiler_params=_cparams(("parallel",)),
        name="mix",
    )(x, emb_r, emb_c, lg, lb, m1, s1, w_in, w_in, bg, y_t, zr, wglu, bglu, wbs, wbf, bbf, wo, bo,
      g1, l1g, l1b, m2, s2, wr_t, br, tri, etri)


def _on_parity(i, fn):
    @pl.when(i % 2 == 0)
    def _():
        fn(0)

    @pl.when(i % 2 == 1)
    def _():
        fn(1)


def _dispatch_kernel(pend_ref, fill_ref, dprev_ref, dest_ref, pos_ref, u_ref, buf_ref,
                     sorted_ref, zero_ref, zsem, tsem, sems):
    i = pl.program_id(0)

    units = BM // ZB

    def clear_copy(start, sem):
        return pltpu.make_async_copy(
            zero_ref, buf_ref.at[pl.ds(pl.multiple_of(start, ZB), ZB)], sem)

    def unused_units(fn):
        def unit(b, c):
            fn(clear_copy(b * ZB, tsem))
            return c
        lax.fori_loop(pend_ref[N_EXPERTS - 1] // ZB, N_BLOCKS * units, unit, 0)

    @pl.when(i == 0)
    def _():
        zero_ref[...] = jnp.zeros_like(zero_ref)

        def each(fn):
            def expert(e, c):
                for z in range(units):
                    start = pend_ref[e] - (z + 1) * ZB

                    @pl.when(start + ZB > fill_ref[e])
                    def _():
                        fn(clear_copy(start, zsem))
                return c
            lax.fori_loop(0, N_EXPERTS, expert, 0)
            for b in range(N_BLOCKS * units, N_BLOCKS_ALL * units):
                fn(clear_copy(b * ZB, zsem))
        each(lambda cp: cp.start())
        unused_units(lambda cp: cp.start())
        each(lambda cp: cp.wait())

        sorted_ref[...] = jnp.zeros_like(sorted_ref)

    def chunk_copy(slot, table_ref, j):
        dst = pl.multiple_of(table_ref[0, 0, j], SEG_ALIGN)
        return pltpu.make_async_copy(sorted_ref.at[slot, pl.ds(j * SEG_ALIGN, SEG_ALIGN)],
                                     buf_ref.at[pl.ds(dst, SEG_ALIGN)], sems.at[slot])

    def drain(slot):
        pltpu.make_async_copy(sorted_ref.at[slot], buf_ref.at[pl.ds(0, CAP)], sems.at[slot]).wait()

    def run(slot):
        pos = pos_ref[...]
        u = u_ref[...]
        n_rb = CAP // CAP_BLOCK
        per_rb = NCHK // (n_rb // 2)
        for rb in range(n_rb):
            for j in range(rb * per_rb, min((rb + 1) * per_rb, NCHK)):
                chunk_copy(1 - slot, dprev_ref, j).start()
            rows = lax.broadcasted_iota(jnp.int32, (CAP_BLOCK, TM), 0) + rb * CAP_BLOCK
            hit = rows == pos[0:1]
            for k in range(1, TOP_K):
                hit = hit | (rows == pos[k:k + 1])
            onehot = jnp.where(hit, 1.0, 0.0).astype(BF16)
            sorted_ref[slot, rb * CAP_BLOCK:(rb + 1) * CAP_BLOCK, :] = jnp.dot(
                onehot, u, preferred_element_type=F32).astype(BF16)
        drain(1 - slot)

        @pl.when(i == N_TILES - 1)
        def _():
            def issue(j, c):
                chunk_copy(slot, dest_ref, j).start()
                return c
            lax.fori_loop(0, NCHK, issue, 0)
            drain(slot)
            unused_units(lambda cp: cp.wait())
    _on_parity(i, run)


def _dispatch(pad_ends, padded, chunk_table, pos_t, u2):
    return pl.pallas_call(
        _dispatch_kernel,
        grid_spec=pltpu.PrefetchScalarGridSpec(
            num_scalar_prefetch=2,
            grid=(N_TILES,),
            in_specs=[pl.BlockSpec((1, 1, NCHK), lambda i, a, b: (i, 0, 0), memory_space=pltpu.SMEM),
                      pl.BlockSpec((1, 1, NCHK), lambda i, a, b: (i + 1, 0, 0), memory_space=pltpu.SMEM),
                      pl.BlockSpec((TOP_K, TM), lambda i, a, b: (0, i)),
                      pl.BlockSpec((TM, D), lambda i, a, b: (i, 0))],
            out_specs=pl.BlockSpec(memory_space=pl.ANY),
            scratch_shapes=[pltpu.VMEM((2, CAP, D), BF16),
                            pltpu.VMEM((ZB, D), BF16),
                            pltpu.SemaphoreType.DMA(()),
                            pltpu.SemaphoreType.DMA(()),
                            pltpu.SemaphoreType.DMA((2,))]),
        out_shape=jax.ShapeDtypeStruct((ROWS_ALL, D), BF16),
        compiler_params=_cparams(("arbitrary",)),
        name="dispatch",
    )(pad_ends, padded, chunk_table, chunk_table, pos_t, u2)


def _ffn_kernel(be_ref, nu_ref, run_ref, nxt_ref, valid_ref, x_ref, wu_hbm, bu_ref, wd_hbm, bd_ref,
                y_ref, wu_ref, wd_ref, wub_ref, wdb_ref, sems):
    i = pl.program_id(0)
    used = i < nu_ref[0]

    def weight_copies(e, slot):
        return (pltpu.make_async_copy(wu_hbm.at[e], wu_ref.at[slot], sems.at[slot]),
                pltpu.make_async_copy(wd_hbm.at[e], wd_ref.at[slot], sems.at[slot]))

    @pl.when(used)
    def _():
        run = run_ref[i]

        @pl.when(run >= 0)
        def _():
            def open_run(slot):
                @pl.when(run == 0)
                def _():
                    for cp in weight_copies(be_ref[i], slot):
                        cp.start()

                @pl.when(nxt_ref[i] >= 0)
                def _():
                    for cp in weight_copies(nxt_ref[i], 1 - slot):
                        cp.start()
                for cp in weight_copies(be_ref[i], slot):
                    cp.wait()
                wub_ref[...] = wu_ref[slot].astype(BF16)
                wdb_ref[...] = wd_ref[slot].astype(BF16)
            _on_parity(run, open_run)

        def expert_rows(r0, nr):
            rows = slice(r0, r0 + nr)
            e = be_ref[i]
            h = (jnp.dot(x_ref[rows, :], wub_ref[...], preferred_element_type=F32)
                 + bu_ref[pl.ds(e, 1), :])
            h_glu = jnp.minimum(h[:, :D], SWIGLU_LIMIT)
            h_lin = jnp.clip(h[:, D:], -SWIGLU_LIMIT, SWIGLU_LIMIT)
            act = (h_glu * _sigmoid(SWIGLU_ALPHA * h_glu) * (h_lin + 1.0)).astype(BF16)
            y_ref[rows, :] = (jnp.dot(act, wdb_ref[...], preferred_element_type=F32)
                              + bd_ref[pl.ds(e, 1), :]).astype(BF16)

        valid = valid_ref[i]

        @pl.when(valid == BM)
        def _():
            expert_rows(0, BM)

        @pl.when(valid < BM)
        def _():
            for h0 in range(0, BM, FFN_HALF):
                @pl.when(valid >= h0 + FFN_HALF)
                def _(h0=h0):
                    expert_rows(h0, FFN_HALF)

                @pl.when(valid < h0 + FFN_HALF)
                def _(h0=h0):
                    for r0 in range(h0, h0 + FFN_HALF, FFN_TAIL):
                        @pl.when(r0 < valid)
                        def _(r0=r0):
                            expert_rows(r0, FFN_TAIL)

                        @pl.when(r0 >= valid)
                        def _(r0=r0):
                            y_ref[r0:r0 + FFN_TAIL, :] = jnp.zeros((FFN_TAIL, D), BF16)

    @pl.when(jnp.logical_not(used))
    def _():
        y_ref[...] = jnp.zeros_like(y_ref)


def _ffn(block_expert, n_used, run_id, next_expert, valid, buf, w_up, b_up, w_down, b_down):
    def blk(i, be, nu, *_):
        return jnp.minimum(i, nu[0] - 1)
    return pl.pallas_call(
        _ffn_kernel,
        grid_spec=pltpu.PrefetchScalarGridSpec(
            num_scalar_prefetch=5,
            grid=(N_BLOCKS_ALL,),
            in_specs=[pl.BlockSpec((BM, D), lambda i, *s: (blk(i, *s), 0)),
                      pl.BlockSpec(memory_space=pl.ANY),
                      pl.BlockSpec((N_EXPERTS, 2 * D), lambda i, *s: (0, 0)),
                      pl.BlockSpec(memory_space=pl.ANY),
                      pl.BlockSpec((N_EXPERTS, D), lambda i, *s: (0, 0))],
            out_specs=pl.BlockSpec((BM, D), lambda i, *s: (i, 0)),
            scratch_shapes=[pltpu.VMEM((2, D, 2 * D), F32),
                            pltpu.VMEM((2, D, D), F32),
                            pltpu.VMEM((D, 2 * D), BF16),
                            pltpu.VMEM((D, D), BF16),
                            pltpu.SemaphoreType.DMA((2,))]),
        out_shape=jax.ShapeDtypeStruct((ROWS_ALL, D), BF16),
        compiler_params=_cparams(("arbitrary",)),
        name="ffn",
    )(block_expert, n_used, run_id, next_expert, valid, buf, w_up, b_up, w_down, b_down)


def _combine_kernel(dest_ref, dnext_ref, y_ref, h1_ref, pos_ref, gate_ref, g2_ref, lg_ref, lb_ref,
                    o_ref, sorted_ref, sems):
    i = pl.program_id(0)

    def chunk_copy(slot, table_ref, j):
        src = pl.multiple_of(table_ref[0, 0, j], SEG_ALIGN)
        return pltpu.make_async_copy(y_ref.at[pl.ds(src, SEG_ALIGN)],
                                     sorted_ref.at[slot, pl.ds(j * SEG_ALIGN, SEG_ALIGN)],
                                     sems.at[slot])

    def drain(slot):
        pltpu.make_async_copy(y_ref.at[pl.ds(0, CAP)], sorted_ref.at[slot], sems.at[slot]).wait()

    @pl.when(i == 0)
    def _():
        def issue(j, c):
            chunk_copy(0, dest_ref, j).start()
            return c
        lax.fori_loop(0, NCHK, issue, 0)

    def run(slot):
        drain(slot)

        pos = pos_ref[...]
        gate = gate_ref[...]
        m = jnp.zeros((TM, D), F32)
        n_cb = CAP // CAP_BLOCK
        per_cb = NCHK // (n_cb // 2)
        for cb in range(n_cb):
            for j in range(cb * per_cb, min((cb + 1) * per_cb, NCHK)):
                chunk_copy(1 - slot, dnext_ref, j).start()
            rws = lax.broadcasted_iota(jnp.int32, (CAP_BLOCK, TM), 0) + cb * CAP_BLOCK
            g = jnp.where(rws == pos[0:1], gate[0:1], 0.0)
            for k in range(1, TOP_K):
                g = g + jnp.where(rws == pos[k:k + 1], gate[k:k + 1], 0.0)
            rows = sorted_ref[slot, cb * CAP_BLOCK:(cb + 1) * CAP_BLOCK, :]
            m = m + lax.dot_general(g.astype(BF16), rows, (((0,), (0,)), ((), ())),
                                    preferred_element_type=F32)
        o_ref[...] = _layer_norm(ALPHA * h1_ref[...] + g2_ref[...] * m, lg_ref[...], lb_ref[...])

        @pl.when(i == N_TILES - 1)
        def _():
            drain(1 - slot)
    _on_parity(i, run)


def _combine(chunk_table, y_buf, h1, pos_t, gate_t, g2, lg, lb):
    vec = pl.BlockSpec((1, D), lambda i: (0, 0))
    return pl.pallas_call(
        _combine_kernel,
        grid_spec=pltpu.PrefetchScalarGridSpec(
            num_scalar_prefetch=0,
            grid=(N_TILES,),
            in_specs=[pl.BlockSpec((1, 1, NCHK), lambda i: (i + 1, 0, 0), memory_space=pltpu.SMEM),
                      pl.BlockSpec((1, 1, NCHK), lambda i: (i + 2, 0, 0), memory_space=pltpu.SMEM),
                      pl.BlockSpec(memory_space=pl.ANY),
                      pl.BlockSpec((TM, D), lambda i: (i, 0)),
                      pl.BlockSpec((TOP_K, TM), lambda i: (0, i)),
                      pl.BlockSpec((TOP_K, TM), lambda i: (0, i)),
                      vec, vec, vec],
            out_specs=pl.BlockSpec((TM, D), lambda i: (i, 0)),
            scratch_shapes=[pltpu.VMEM((2, CAP, D), BF16),
                            pltpu.SemaphoreType.DMA((2,))]),
        out_shape=jax.ShapeDtypeStruct((N_TOK, D), F32),
        compiler_params=_cparams(("arbitrary",)),
        name="combine",
    )(chunk_table, chunk_table, y_buf, h1, pos_t, gate_t, g2, lg, lb)


def _sincos_tables():
    q = D // 4
    omega = 1.0 / (10000.0 ** (np.arange(q) / q))

    def emb(n):
        ang = np.arange(n)[:, None] * omega[None, :]
        return jnp.asarray(np.concatenate([np.sin(ang), np.cos(ang)], axis=-1), F32)
    return emb(N_TOK // GRID_W), emb(GRID_W)


def kernel(x, c, ctx, c_ctx, ln_in_g, ln_in_b, w_ada, b_ada, w_in, b_in, s5_lambda_re, s5_lambda_im, s5_log_dt, s5_b_re, s5_b_im, s5_c_re, s5_c_im, s5_d, w_glu, b_glu, w_br_s5, w_br_fft, b_br_fft, w_out, b_out, ln1_g, ln1_b, w_router, b_router, w_up, b_up, w_down, b_down, ln2_g, ln2_b):
    assert x.shape == (1, N_TOK, D) and ctx.shape == (1, N_CTX, D) and w_ada.shape[0] == 1
    row = lambda v: v.reshape(1, -1).astype(F32)

    cc = jnp.concatenate([c.reshape(1, D), c_ctx.reshape(1, D), jnp.zeros((SUBLANES - 2, D), F32)], axis=0)
    ada = _ada(cc, w_ada[0], row(b_ada[0]))
    sh1, sc1, g1, sh2, sc2, g2 = (ada[0:1, k * D:(k + 1) * D] for k in range(6))
    sh1c, sc1c = ada[1:2, 0:D], ada[1:2, D:2 * D]

    emb_r, emb_c = _sincos_tables()
    st1, st2, fc = _dft_tables()
    lg, lb = row(ln_in_g), row(ln_in_b)

    w_in_bf = w_in[0].astype(BF16)
    b_s5 = row(b_in[0][:S5_WIDTH])
    b_fft8 = jnp.concatenate([row(b_in[0][S5_WIDTH:S5_WIDTH + FFT_WIDTH]),
                              jnp.zeros((SUBLANES - 1, FFT_WIDTH), F32)], axis=0)
    b_g = row(b_in[0][S5_WIDTH + FFT_WIDTH:])
    w_fc, b_fc = _fft_weights(w_in[0], b_fft8, fc)
    bcat = jnp.concatenate([b_s5, b_fc[0:1]], axis=1)

    x2 = x[0]
    p_t, xr, xi = _proj(x2, emb_r, emb_c, lg, lb, 1.0 + sc1, sh1, w_in_bf, w_fc, bcat)
    pc_t = _ctx_proj(ctx[0], lg, lb, 1.0 + sc1c, sh1c, w_in_bf[:, :S5_WIDTH], b_s5)

    b_c, a_q, a_p, trans, ctx_w = _s5_tables(
        s5_lambda_re[0], s5_lambda_im[0], s5_log_dt[0], s5_b_re[0], s5_b_im[0],
        s5_c_re[0], s5_c_im[0], s5_d[0])
    y_t = _s5(p_t, pc_t, b_c, a_q, a_p, trans, ctx_w)

    yr, yi = _fft1(xr, xi, st1)
    zr = _fft2(yr, yi, st2)

    tri = (jnp.arange(TM)[:, None] < jnp.arange(TM)[None, :]).astype(BF16)
    br = jnp.broadcast_to(b_router[0].reshape(N_EXPERTS, 1), (N_EXPERTS, LANES))
    etri = (jnp.arange(N_EXPERTS)[:, None] > jnp.arange(N_EXPERTS)[None, :]).astype(BF16)
    wr_t = jnp.transpose(w_router[0])
    wr_hi = wr_t.astype(BF16)
    wr_split = jnp.stack([wr_hi, (wr_t - wr_hi.astype(F32)).astype(BF16)], axis=0)
    h1, u2, pos_t, gate_t, counts = _mix(
        x2, emb_r, emb_c, lg, lb, 1.0 + sc1, sh1, w_in_bf, b_g, y_t, zr,
        w_glu[0].astype(BF16), row(b_glu[0]), w_br_s5[0].astype(BF16), w_br_fft[0].astype(BF16),
        row(b_br_fft[0]), w_out[0].astype(BF16), row(b_out[0]), g1, row(ln1_g[0]), row(ln1_b[0]),
        1.0 + sc2, sh2, wr_split, br, tri, etri)

    cnt = counts[:, :, 0].astype(jnp.int32)
    seg = (cnt + SEG_ALIGN - 1) // SEG_ALIGN * SEG_ALIGN
    seg_end = jnp.cumsum(seg, axis=1)
    seg_start = seg_end - seg
    padded = (jnp.sum(seg, axis=0) + BM - 1) // BM * BM
    pad_ends = jnp.cumsum(padded)
    seg_dest = (pad_ends - padded)[None, :] + jnp.cumsum(seg, axis=0) - seg
    chunk_row = jnp.arange(NCHK, dtype=jnp.int32) * SEG_ALIGN
    chunk_exp = jnp.minimum(jnp.sum(chunk_row[None, :, None] >= seg_end[:, None, :], axis=-1),
                            N_EXPERTS - 1)
    own = chunk_exp[:, :, None] == jnp.arange(N_EXPERTS, dtype=jnp.int32)[None, None, :]
    chunk_dest = (jnp.sum(jnp.where(own, (seg_dest - seg_start)[:, None, :], 0), axis=-1)
                  + chunk_row[None, :]).astype(jnp.int32).reshape(N_TILES, 1, NCHK)
    nchk = (seg_end[:, -1] // SEG_ALIGN).astype(jnp.int32)
    block_start = jnp.arange(N_BLOCKS_ALL, dtype=jnp.int32) * BM
    block_expert = jnp.minimum(jnp.sum(block_start[:, None] >= pad_ends[None, :], axis=1),
                               N_EXPERTS - 1).astype(jnp.int32)
    n_used = (pad_ends[-1:] // BM).astype(jnp.int32)
    opens = (block_start < pad_ends[-1]) & (
        block_expert != jnp.concatenate([jnp.full((1,), -1, jnp.int32), block_expert[:-1]]))
    run_id = jnp.where(opens, jnp.cumsum(opens.astype(jnp.int32)) - 1, -1).astype(jnp.int32)
    experts = jnp.arange(N_EXPERTS, dtype=jnp.int32)
    later = (experts[None, :] > block_expert[:, None]) & (padded[None, :] > 0)
    next_expert = jnp.min(jnp.where(later, experts[None, :], N_EXPERTS), axis=1)
    next_expert = jnp.where(next_expert < N_EXPERTS, next_expert, -1).astype(jnp.int32)

    spare = (ROWS + chunk_row)[None, None, :]
    chunk_table = jnp.concatenate(
        [spare,
         jnp.where(chunk_row[None, None, :] < (nchk * SEG_ALIGN)[:, None, None], chunk_dest, spare),
         spare], axis=0).astype(jnp.int32)
    fill_ends = (pad_ends - padded + jnp.sum(seg, axis=0)).astype(jnp.int32)
    buf = _dispatch(pad_ends.astype(jnp.int32), fill_ends, chunk_table, pos_t, u2)
    mine = block_expert[:, None] == experts[None, :]
    filled = jnp.sum(jnp.where(mine, fill_ends[None, :], 0), axis=1)
    valid = jnp.clip(filled - block_start, 0, BM).astype(jnp.int32)
    y_buf = _ffn(block_expert, n_used, run_id, next_expert, valid, buf, w_up[0],
                 b_up[0], w_down[0], b_down[0])
    out = _combine(chunk_table, y_buf, h1, pos_t, gate_t, g2, row(ln2_g[0]), row(ln2_b[0]))
    return out.reshape(1, N_TOK, D)
```

```python
import functools
import math

import jax
import jax.numpy as jnp
import numpy as np
from jax import lax
from jax.experimental import pallas as pl
from jax.experimental.pallas import tpu as pltpu

F32 = jnp.float32
BF16 = jnp.bfloat16
HI = lax.Precision.HIGHEST

D = 1024
N_TOK = 16384
N_CTX = 256
GRID_W = 64
S5_GROUP = 16
S5_GROUPS = 32
S5_STATE = 64
S5_WIDTH = 512
FFT_GROUPS = 4
FFT_DIM = 128
FFT_WIDTH = 512
N_EXPERTS = 32
TOP_K = 4
LN_EPS = 1e-5
ALPHA = 2.0 ** 0.25
SWIGLU_ALPHA = 1.702
SWIGLU_LIMIT = 7.0

LANES = 128
SUBLANES = 8
VMEM_LIMIT = 56 * 1024 * 1024

CH = 8
N_CHUNK = N_TOK // CH
N_CHUNK_CTX = N_CTX // CH
NSEG = SUBLANES
SEG = N_CHUNK // NSEG
SCAN_UNROLL = 4
GPT = LANES // S5_GROUP
NJ = S5_WIDTH // LANES
CL = CH * LANES
SW = 4 * GPT * S5_STATE

FN = 128
FB = 16

TM = 512
TM_PROJ = 1024
N_TILES = N_TOK // TM
BM = 1024
ZB = 128
FFN_HALF = 512
FFN_TAIL = 128
N_SLOTS = N_TOK * TOP_K
SEG_ALIGN = 2 * SUBLANES
CAP_BLOCK = 256
CAP = -(-(TOP_K * TM + N_EXPERTS * (SEG_ALIGN - 1)) // CAP_BLOCK) * CAP_BLOCK
NCHK = CAP // SEG_ALIGN
N_BLOCKS = -(-(N_SLOTS + N_TILES * N_EXPERTS * (SEG_ALIGN - 1)) // BM) + N_EXPERTS
ROWS = N_BLOCKS * BM
N_BLOCKS_ALL = N_BLOCKS + -(-CAP // BM)
ROWS_ALL = N_BLOCKS_ALL * BM


def _cparams(sem):
    return pltpu.CompilerParams(dimension_semantics=sem, vmem_limit_bytes=VMEM_LIMIT)


def _layer_norm(x, g, b):
    mu = jnp.mean(x, axis=-1, keepdims=True)
    xc = x - mu
    var = jnp.mean(xc * xc, axis=-1, keepdims=True)
    return xc * lax.rsqrt(var + LN_EPS) * g + b


def _sigmoid(x):
    return 0.5 * jnp.tanh(0.5 * x) + 0.5


def _ada_kernel(c_ref, w_ref, b_ref, o_ref):
    c = c_ref[...]
    s = c * _sigmoid(c)
    w = w_ref[...]
    s_hi, w_hi = s.astype(BF16), w.astype(BF16)
    s_lo = (s - s_hi.astype(F32)).astype(BF16)
    w_lo = (w - w_hi.astype(F32)).astype(BF16)

    def mm(a, b):
        return jnp.dot(a, b, preferred_element_type=F32)
    o_ref[...] = mm(s_hi, w_hi) + mm(s_lo, w_hi) + mm(s_hi, w_lo) + b_ref[...]


def _ada(cc, w_ada, b_ada):
    nb = 4
    wb = 6 * D // nb
    return pl.pallas_call(
        _ada_kernel,
        grid=(nb,),
        in_specs=[pl.BlockSpec((SUBLANES, D), lambda i: (0, 0)),
                  pl.BlockSpec((D, wb), lambda i: (0, i)),
                  pl.BlockSpec((1, wb), lambda i: (0, i))],
        out_specs=pl.BlockSpec((SUBLANES, wb), lambda i: (0, i)),
        out_shape=jax.ShapeDtypeStruct((SUBLANES, 6 * D), F32),
        compiler_params=_cparams(("parallel",)),
        name="ada",
    )(cc, w_ada, b_ada)


def _fftw_kernel(w_ref, b_ref, f_ref, wo_ref, bo_ref):
    f = f_ref[...]
    w = w_ref[...]
    w_hi, f_hi = w.astype(BF16), f.astype(BF16)
    w_lo = (w - w_hi.astype(F32)).astype(BF16)
    f_lo = (f - f_hi.astype(F32)).astype(BF16)

    def mm(a, b):
        return jnp.dot(a, b, preferred_element_type=F32)
    wo_ref[...] = (mm(w_hi, f_hi) + mm(w_lo, f_hi) + mm(w_hi, f_lo)).astype(BF16)
    bo_ref[...] = jnp.dot(b_ref[...], f, preferred_element_type=F32, precision=HI)


def _fft_weights(w_in, b_fft8, fc):
    def full(a):
        return pl.BlockSpec(a.shape, lambda i: (0,) * a.ndim)
    outs = (jax.ShapeDtypeStruct((D, 2 * FFT_WIDTH), BF16),
            jax.ShapeDtypeStruct((SUBLANES, 2 * FFT_WIDTH), F32))
    return pl.pallas_call(
        _fftw_kernel,
        grid=(1,),
        in_specs=[pl.BlockSpec((D, FFT_WIDTH), lambda i: (0, S5_WIDTH // FFT_WIDTH)),
                  full(b_fft8), full(fc)],
        out_specs=tuple(pl.BlockSpec(o.shape, lambda i: (0, 0)) for o in outs),
        out_shape=outs,
        compiler_params=_cparams(("arbitrary",)),
        name="fftw",
    )(w_in, b_fft8, fc)


def _pos_code(er_ref, ec_ref, tm):
    nr = tm // GRID_W
    er = er_ref[...]
    row = jnp.broadcast_to(er[:, None, :], (nr, GRID_W, D // 2)).reshape(tm, D // 2)
    col = jnp.concatenate([ec_ref[...]] * nr, axis=0)
    return jnp.concatenate([row, col], axis=-1)


def _to_chunk_major(val, scr_ref, out_ref, tm):
    for j in range(NJ):
        scr_ref[j] = val[:, j * LANES:(j + 1) * LANES]
    for t in range(CH):
        for j in range(NJ):
            piece = scr_ref[j, pl.ds(t, tm // CH, stride=CH), :]
            out_ref[t, :, j * LANES:(j + 1) * LANES] = piece.astype(out_ref.dtype)


def _proj_kernel(x_ref, er_ref, ec_ref, lg_ref, lb_ref, m_ref, s_ref, ws_ref, wf_ref, b_ref,
                 p_ref, xr_ref, xi_ref, scr_ref):
    x = x_ref[...] + _pos_code(er_ref, ec_ref, TM_PROJ)
    h = _layer_norm(x, lg_ref[...], lb_ref[...])
    u = (h * m_ref[...] + s_ref[...]).astype(BF16)
    p_s5 = jnp.dot(u, ws_ref[...], preferred_element_type=F32) + b_ref[:, :S5_WIDTH]
    _to_chunk_major(p_s5, scr_ref, p_ref, TM_PROJ)
    p_f = jnp.dot(u, wf_ref[...], preferred_element_type=F32) + b_ref[:, S5_WIDTH:]
    xr_ref[...] = p_f[:, :FFT_WIDTH].astype(BF16)
    xi_ref[...] = p_f[:, FFT_WIDTH:].astype(BF16)


def _proj(x, emb_r, emb_c, lg, lb, m1, s1, w_in, w_fc, bcat):
    nw = bcat.shape[1]
    vec = pl.BlockSpec((1, D), lambda i: (0, 0))
    return pl.pallas_call(
        _proj_kernel,
        grid=(N_TOK // TM_PROJ,),
        in_specs=[pl.BlockSpec((TM_PROJ, D), lambda i: (i, 0)),
                  pl.BlockSpec((TM_PROJ // GRID_W, D // 2), lambda i: (i, 0)),
                  pl.BlockSpec((GRID_W, D // 2), lambda i: (0, 0)),
                  vec, vec, vec, vec,
                  pl.BlockSpec((D, S5_WIDTH), lambda i: (0, 0)),
                  pl.BlockSpec((D, 2 * FFT_WIDTH), lambda i: (0, 0)),
                  pl.BlockSpec((1, nw), lambda i: (0, 0))],
        out_specs=(pl.BlockSpec((CH, TM_PROJ // CH, S5_WIDTH), lambda i: (0, i, 0)),
                   pl.BlockSpec((TM_PROJ, FFT_WIDTH), lambda i: (i, 0)),
                   pl.BlockSpec((TM_PROJ, FFT_WIDTH), lambda i: (i, 0))),
        out_shape=(jax.ShapeDtypeStruct((CH, N_CHUNK, S5_WIDTH), BF16),
                   jax.ShapeDtypeStruct((N_TOK, FFT_WIDTH), BF16),
                   jax.ShapeDtypeStruct((N_TOK, FFT_WIDTH), BF16)),
        scratch_shapes=[pltpu.VMEM((NJ, TM_PROJ, LANES), F32)],
        compiler_params=_cparams(("parallel",)),
        name="proj",
    )(x, emb_r, emb_c, lg, lb, m1, s1, w_in, w_fc, bcat)


def _ctx_proj_kernel(x_ref, lg_ref, lb_ref, m_ref, s_ref, w_ref, b_ref, p_ref, scr_ref):
    h = _layer_norm(x_ref[...], lg_ref[...], lb_ref[...])
    u = (h * m_ref[...] + s_ref[...]).astype(BF16)
    p = jnp.dot(u, w_ref[...], preferred_element_type=F32) + b_ref[...]
    _to_chunk_major(p, scr_ref, p_ref, N_CTX)


def _ctx_proj(ctx, lg, lb, m1, s1, w_s5, b_s5):
    return pl.pallas_call(
        _ctx_proj_kernel,
        out_shape=jax.ShapeDtypeStruct((CH, N_CHUNK_CTX, S5_WIDTH), BF16),
        scratch_shapes=[pltpu.VMEM((NJ, N_CTX, LANES), F32)],
        compiler_params=pltpu.CompilerParams(vmem_limit_bytes=VMEM_LIMIT),
        name="ctxproj",
    )(ctx, lg, lb, m1, s1, w_s5, b_s5)


def _s5_tables(lam_re, lam_im, log_dt, b_re, b_im, c_re, c_im, d_skip):
    dt = jnp.exp(log_dt)[..., None]
    zr = lam_re * dt
    zi = lam_im * dt

    def apow(m):
        m = jnp.asarray(m, F32)
        mag = jnp.exp(zr[..., None] * m)
        return mag * jnp.cos(zi[..., None] * m), mag * jnp.sin(zi[..., None] * m)

    a_re, a_im = apow(jnp.ones((1,), F32))
    a_re, a_im = a_re[..., 0], a_im[..., 0]
    den = lam_re * lam_re + lam_im * lam_im
    num_re = a_re - 1.0
    k_re = (num_re * lam_re + a_im * lam_im) / den
    k_im = (a_im * lam_re - num_re * lam_im) / den
    bb_re = k_re[..., None] * b_re - k_im[..., None] * b_im
    bb_im = k_re[..., None] * b_im + k_im[..., None] * b_re

    ks = jnp.arange(CH + 1, dtype=F32)
    pw_re, pw_im = apow(ks)
    kmag = jnp.exp(zr[:, :, None, :] * ks[None, None, :, None])
    pk_re = kmag * jnp.cos(zi[:, :, None, :] * ks[None, None, :, None])
    pk_im = kmag * jnp.sin(zi[:, :, None, :] * ks[None, None, :, None])
    bt_re, bt_im = jnp.swapaxes(b_re, 2, 3), jnp.swapaxes(b_im, 2, 3)
    bbt_re = k_re[:, :, None, :] * bt_re - k_im[:, :, None, :] * bt_im
    bbt_im = k_re[:, :, None, :] * bt_im + k_im[:, :, None, :] * bt_re

    ar, ai = pw_re[:, :, :, :CH, None], pw_im[:, :, :, :CH, None]
    cr = jnp.swapaxes(c_re, 2, 3)[:, :, :, None, :]
    ci = jnp.swapaxes(c_im, 2, 3)[:, :, :, None, :]
    ca = jnp.concatenate([cr * ar - ci * ai, -(cr * ai + ci * ar)], axis=2)
    ca = ca.reshape(2, S5_GROUPS, 2 * S5_STATE, CH * S5_GROUP)
    bbt = jnp.concatenate([bbt_re, bbt_im], axis=-1)
    taps = jnp.einsum('dghq,dgqn->dghn', bbt, ca, precision=HI)
    skip = (d_skip.reshape(S5_GROUPS, S5_GROUP, 1) * jnp.eye(S5_GROUP, dtype=F32)[None])
    taps = taps.at[0, :, :, :S5_GROUP].add(skip)
    b_c = jnp.transpose(taps.reshape(2, NJ, GPT, S5_GROUP, CH, S5_GROUP), (1, 4, 0, 2, 3, 5))
    b_c = b_c.reshape(NJ, 2 * CH, LANES, S5_GROUP)

    ef = (CH - 1) - jnp.arange(CH)
    eb = jnp.arange(CH)

    def q_part(d, e):
        pr = pk_re[d][:, e, None, :]
        pi = pk_im[d][:, e, None, :]
        br = bbt_re[d][:, None, :, :]
        bi = bbt_im[d][:, None, :, :]
        return pr * br - pi * bi, pr * bi + pi * br

    def q_rows(v):
        v = v.reshape(NJ, GPT, CH, S5_GROUP, S5_STATE)
        return jnp.transpose(v, (0, 2, 1, 3, 4)).reshape(NJ, CL, S5_STATE)
    a_q = jnp.stack([q_rows(v) for v in q_part(0, ef) + q_part(1, eb)], axis=0)

    of = jnp.arange(CH) + 1
    ob = CH - jnp.arange(CH)

    def p_part(d, e):
        pr = pw_re[d][..., e][:, :, :, None]
        pi = pw_im[d][..., e][:, :, :, None]
        return (ct_re[d] * pr - ct_im[d] * pi, -(ct_re[d] * pi + ct_im[d] * pr))
    ct_re = jnp.swapaxes(c_re, 2, 3)[:, :, :, None, :]
    ct_im = jnp.swapaxes(c_im, 2, 3)[:, :, :, None, :]
    a_p = jnp.stack([v.reshape(NJ, GPT * S5_STATE, CH * S5_GROUP)
                     for v in p_part(0, of) + p_part(1, ob)], axis=0)

    def lanes(v):
        return jnp.transpose(v.reshape(2, NJ, GPT * S5_STATE), (1, 0, 2))
    c_r, c_i = apow(jnp.full((1,), float(CH), F32))
    s_r, s_i = apow(jnp.full((1,), float(CH * SEG), F32))
    cr, ci, sr, si = (lanes(v[..., 0]) for v in (c_r, c_i, s_r, s_i))
    trans = jnp.stack([cr[:, 0], ci[:, 0], cr[:, 1], ci[:, 1],
                       sr[:, 0], si[:, 0], sr[:, 1], si[:, 1]], axis=1)

    cidx = jnp.arange(N_CHUNK_CTX, dtype=F32)
    wf_r, wf_i = apow(CH * (N_CHUNK_CTX - 1 - cidx))
    wb_r, wb_i = apow(CH * cidx)

    def ctx_lanes(v, d):
        return jnp.transpose(v[d].reshape(NJ, GPT * S5_STATE, N_CHUNK_CTX), (0, 2, 1))
    ctx_w = jnp.stack([ctx_lanes(wf_r, 0), ctx_lanes(wf_i, 0),
                       ctx_lanes(wb_r, 1), ctx_lanes(wb_i, 1)], axis=1)
    return b_c, a_q, a_p, trans, ctx_w


def _build_operators(bc_ref, aq_ref, ap_ref, c16_ref, c64_ref, cm_ref, wm_ref, wq_ref, wp_ref):
    def expand(a, c, row_shift, col_shift):
        w = jnp.dot(a, c, preferred_element_type=F32)
        rg = (lax.broadcasted_iota(jnp.int32, (w.shape[0], 1), 0) >> row_shift) & (GPT - 1)
        cg = (lax.broadcasted_iota(jnp.int32, (1, w.shape[1]), 1) >> col_shift) & (GPT - 1)
        return jnp.where(rg == cg, w, 0.0)

    blk = [expand(bc_ref[0, kd], c16_ref[...], 4, 4) for kd in range(2 * CH)]
    for t in range(CH):
        for u in range(CH):
            b = blk[2 * (u - t)] if u > t else blk[2 * (t - u) + 1] if u < t else blk[0] + blk[1]
            wm_ref[0, t * LANES:(t + 1) * LANES, u * LANES:(u + 1) * LANES] = b.astype(BF16)
    half = GPT * S5_STATE
    for s in range(4):
        wq_ref[0, :, s * half:(s + 1) * half] = expand(aq_ref[s, 0], c64_ref[...], 4, 6).astype(BF16)
        wp_ref[0, s * half:(s + 1) * half, :] = expand(ap_ref[s, 0], cm_ref[...], 6, 4).astype(BF16)


def _s5_kernel(p_ref, pc_ref, bc_ref, aq_ref, ap_ref, c16_ref, c64_ref, cm_ref, tr_ref, cw_ref,
               y_ref, v_ref, wm_ref, wq_ref, wp_ref):
    nq = NJ
    half = GPT * S5_STATE
    _build_operators(bc_ref, aq_ref, ap_ref, c16_ref, c64_ref, cm_ref, wm_ref, wq_ref, wp_ref)

    def chunk_rows(ref, r0, nrows):
        return jnp.concatenate([ref[t, pl.ds(r0, nrows), :] for t in range(CH)], axis=-1)

    def fill(k, c):
        r0 = pl.multiple_of(k * SEG, SEG)
        v = jnp.dot(chunk_rows(p_ref, r0, SEG), wq_ref[0], preferred_element_type=F32)
        for s in range(4 * nq):
            v_ref[s, pl.ds(k, SEG, stride=NSEG), :] = v[:, s * LANES:(s + 1) * LANES]
        return c
    lax.fori_loop(0, NSEG, fill, 0)

    vc = jnp.dot(chunk_rows(pc_ref, 0, N_CHUNK_CTX), wq_ref[0], preferred_element_type=F32)
    vfr, vfi, vbr, vbi = (vc[:, i * half:(i + 1) * half] for i in range(4))
    wfr, wfi, wbr, wbi = (cw_ref[0, i] for i in range(4))
    s0_fr = jnp.sum(wfr * vfr - wfi * vfi, axis=0, keepdims=True)
    s0_fi = jnp.sum(wfr * vfi + wfi * vfr, axis=0, keepdims=True)
    s0_br = jnp.sum(wbr * vbr - wbi * vbi, axis=0, keepdims=True)
    s0_bi = jnp.sum(wbr * vbi + wbi * vbr, axis=0, keepdims=True)

    tr = tr_ref[0]
    afr, afi, abr, abi = (jnp.broadcast_to(tr[i:i + 1], (NSEG, half)) for i in range(4))
    gfr, gfi, gbr, gbi = (tr[i:i + 1] for i in range(4, 8))

    def load_part(part, i):
        return jnp.concatenate(
            [v_ref[part * nq + q, pl.ds(pl.multiple_of(i * NSEG, NSEG), NSEG), :] for q in range(nq)],
            axis=-1)

    def store_part(part, i, val):
        for q in range(nq):
            v_ref[part * nq + q, pl.ds(pl.multiple_of(i * NSEG, NSEG), NSEG), :] = (
                val[:, q * LANES:(q + 1) * LANES])

    def step(i, carry, write):
        fr, fi, br, bi = carry
        ib = SEG - 1 - i
        ufr, ufi = load_part(0, i), load_part(1, i)
        ubr, ubi = load_part(2, ib), load_part(3, ib)
        if write:
            store_part(0, i, fr)
            store_part(1, i, fi)
            store_part(2, ib, br)
            store_part(3, ib, bi)
        return (afr * fr - afi * fi + ufr, afr * fi + afi * fr + ufi,
                abr * br - abi * bi + ubr, abr * bi + abi * br + ubi)

    zero = jnp.zeros((NSEG, half), F32)
    ffr, ffi, fbr, fbi = lax.fori_loop(0, SEG, functools.partial(step, write=False),
                                       (zero, zero, zero, zero), unroll=SCAN_UNROLL)

    rows_fr, rows_fi = [s0_fr], [s0_fi]
    for k in range(1, NSEG):
        pr, pi = rows_fr[-1], rows_fi[-1]
        rows_fr.append(gfr * pr - gfi * pi + ffr[k - 1:k])
        rows_fi.append(gfr * pi + gfi * pr + ffi[k - 1:k])
    rows_br, rows_bi = [s0_br], [s0_bi]
    for k in range(NSEG - 2, -1, -1):
        pr, pi = rows_br[0], rows_bi[0]
        rows_br.insert(0, gbr * pr - gbi * pi + fbr[k + 1:k + 2])
        rows_bi.insert(0, gbr * pi + gbi * pr + fbi[k + 1:k + 2])
    init = tuple(jnp.concatenate(r, axis=0) for r in (rows_fr, rows_fi, rows_br, rows_bi))

    lax.fori_loop(0, SEG, functools.partial(step, write=True), init, unroll=SCAN_UNROLL)

    def emit(k, c):
        r0 = pl.multiple_of(k * SEG, SEG)
        b = chunk_rows(p_ref, r0, SEG)
        sin = jnp.concatenate([v_ref[s, pl.ds(k, SEG, stride=NSEG), :] for s in range(4 * nq)], axis=-1)
        y = (jnp.dot(b, wm_ref[0], preferred_element_type=F32)
             + jnp.dot(sin.astype(BF16), wp_ref[0], preferred_element_type=F32))
        for t in range(CH):
            y_ref[t, pl.ds(r0, SEG), :] = y[:, t * LANES:(t + 1) * LANES].astype(y_ref.dtype)
        return c
    lax.fori_loop(0, NSEG, emit, 0)


def _s5(p_t, pc_t, b_c, a_q, a_p, trans, ctx_w):
    rep = np.ones((1, GPT))
    c16 = jnp.asarray(np.kron(rep, np.eye(S5_GROUP)), F32).astype(BF16)
    c64 = jnp.asarray(np.kron(rep, np.eye(S5_STATE)), F32).astype(BF16)
    c_m = jnp.asarray(np.kron(np.eye(CH), np.kron(rep, np.eye(S5_GROUP))), F32).astype(BF16)
    b_c, a_q, a_p = b_c.astype(BF16), a_q.astype(BF16), a_p.astype(BF16)
    half = GPT * S5_STATE
    return pl.pallas_call(
        _s5_kernel,
        grid=(NJ,),
        in_specs=[pl.BlockSpec((CH, N_CHUNK, LANES), lambda j: (0, 0, j)),
                  pl.BlockSpec((CH, N_CHUNK_CTX, LANES), lambda j: (0, 0, j)),
                  pl.BlockSpec((1, 2 * CH, LANES, S5_GROUP), lambda j: (j, 0, 0, 0)),
                  pl.BlockSpec((4, 1, CL, S5_STATE), lambda j: (0, j, 0, 0)),
                  pl.BlockSpec((4, 1, half, CH * S5_GROUP), lambda j: (0, j, 0, 0)),
                  pl.BlockSpec(c16.shape, lambda j: (0, 0)),
                  pl.BlockSpec(c64.shape, lambda j: (0, 0)),
                  pl.BlockSpec(c_m.shape, lambda j: (0, 0)),
                  pl.BlockSpec((1, SUBLANES, half), lambda j: (j, 0, 0)),
                  pl.BlockSpec((1, 4, N_CHUNK_CTX, half), lambda j: (j, 0, 0, 0))],
        out_specs=pl.BlockSpec((CH, N_CHUNK, LANES), lambda j: (0, 0, j)),
        out_shape=jax.ShapeDtypeStruct((CH, N_CHUNK, S5_WIDTH), BF16),
        scratch_shapes=[pltpu.VMEM((4 * NJ, N_CHUNK, LANES), F32),
                        pltpu.VMEM((1, CL, CL), BF16),
                        pltpu.VMEM((1, CL, SW), BF16),
                        pltpu.VMEM((1, SW, CL), BF16)],
        compiler_params=_cparams(("parallel",)),
        name="s5",
    )(p_t, pc_t, b_c, a_q, a_p, c16, c64, c_m, trans, ctx_w)


def _dft_tables():
    n = np.arange(FN)
    ang = 2.0 * np.pi * np.outer(n, n) / FN
    c, s = np.cos(ang), np.sin(ang)
    st1 = np.block([[c, s], [-s, c]])
    tw = 2.0 * np.pi * np.outer(n, n) / (FN * FN)
    wr, wi = np.cos(tw), -np.sin(tw)
    fr = c[None] * wr[:, None, :] + s[None] * wi[:, None, :]
    fi = c[None] * wi[:, None, :] - s[None] * wr[:, None, :]
    st2 = np.concatenate([fr, -fi], axis=-1)
    scale = 1.0 / math.sqrt(N_TOK * FFT_DIM)
    blk_c = np.kron(np.eye(FFT_GROUPS), c) * scale
    blk_s = np.kron(np.eye(FFT_GROUPS), s) * scale
    fc = np.concatenate([blk_c, -blk_s], axis=1)
    return (jnp.asarray(st1, F32).astype(BF16), jnp.asarray(st2, F32).astype(BF16), jnp.asarray(fc, F32))


FSL = FFT_WIDTH // LANES


FBH = FB // SUBLANES


def _block_to_slabs(blk, slab_ref, first, per_half):
    for bh in range(FBH):
        val = blk[:, bh * SUBLANES:(bh + 1) * SUBLANES, :].reshape(FN * SUBLANES, FFT_WIDTH)
        for s in range(FSL):
            slab_ref[bh * per_half + first + s] = val[:, s * LANES:(s + 1) * LANES]


def _slab_rows(b, first, per_half):
    return (b // SUBLANES) * per_half + first, pl.ds(b % SUBLANES, FN, stride=SUBLANES)


def _slabs_to_block(slab_ref, first, per_half):
    halves = []
    for bh in range(FBH):
        val = jnp.concatenate([slab_ref[bh * per_half + first + s] for s in range(FSL)], axis=-1)
        halves.append(val.reshape(FN, SUBLANES, FFT_WIDTH))
    return jnp.concatenate(halves, axis=1)


def _fft1_kernel(xr_ref, xi_ref, f_ref, yr_ref, yi_ref, in_ref, out_ref):
    _block_to_slabs(xr_ref[...].astype(F32), in_ref, 0, 2 * FSL)
    _block_to_slabs(xi_ref[...].astype(F32), in_ref, FSL, 2 * FSL)
    for b in range(FB):
        def part(first):
            base, rows = _slab_rows(b, first, 2 * FSL)
            return jnp.concatenate([in_ref[base + s, rows, :] for s in range(FSL)], axis=-1)
        xs = jnp.concatenate([part(0), part(FSL)], axis=0).astype(BF16)
        y = jnp.dot(f_ref[...], xs, preferred_element_type=F32)
        base, rows = _slab_rows(b, 0, 2 * FSL)
        for s in range(FSL):
            out_ref[base + s, rows, :] = y[:FN, s * LANES:(s + 1) * LANES]
            out_ref[base + FSL + s, rows, :] = y[FN:, s * LANES:(s + 1) * LANES]
    yr_ref[...] = _slabs_to_block(out_ref, 0, 2 * FSL).astype(BF16)
    yi_ref[...] = _slabs_to_block(out_ref, FSL, 2 * FSL).astype(BF16)


def _fft1(xr, xi, st1):
    spec = pl.BlockSpec((FN, FB, FFT_WIDTH), lambda i: (0, i, 0))
    slabs = pltpu.VMEM((FBH * 2 * FSL, FN * SUBLANES, LANES), F32)
    return pl.pallas_call(
        _fft1_kernel,
        grid=(FN // FB,),
        in_specs=[spec, spec, pl.BlockSpec((2 * FN, 2 * FN), lambda i: (0, 0))],
        out_specs=(spec, spec),
        out_shape=(jax.ShapeDtypeStruct((FN, FN, FFT_WIDTH), BF16),) * 2,
        scratch_shapes=[slabs, slabs],
        compiler_params=_cparams(("parallel",)),
        name="fft1",
    )(xr.reshape(FN, FN, FFT_WIDTH), xi.reshape(FN, FN, FFT_WIDTH), st1)


def _fft2_kernel(yr_ref, yi_ref, f_ref, z_ref, out_ref):
    for b in range(FB):
        ys = jnp.concatenate([yr_ref[b * FN:(b + 1) * FN, :], yi_ref[b * FN:(b + 1) * FN, :]], axis=0)
        z = jnp.dot(f_ref[b], ys, preferred_element_type=F32)
        base, rows = _slab_rows(b, 0, FSL)
        for s in range(FSL):
            out_ref[base + s, rows, :] = z[:, s * LANES:(s + 1) * LANES]
    z_ref[...] = _slabs_to_block(out_ref, 0, FSL).astype(BF16)


def _fft2(yr, yi, st2):
    rows = pl.BlockSpec((FB * FN, FFT_WIDTH), lambda i: (i, 0))
    z = pl.pallas_call(
        _fft2_kernel,
        grid=(FN // FB,),
        in_specs=[rows, rows, pl.BlockSpec((FB, FN, 2 * FN), lambda i: (i, 0, 0))],
        out_specs=pl.BlockSpec((FN, FB, FFT_WIDTH), lambda i: (0, i, 0)),
        out_shape=jax.ShapeDtypeStruct((FN, FN, FFT_WIDTH), BF16),
        scratch_shapes=[pltpu.VMEM((FBH * FSL, FN * SUBLANES, LANES), F32)],
        compiler_params=_cparams(("parallel",)),
        name="fft2",
    )(yr.reshape(N_TOK, FFT_WIDTH), yi.reshape(N_TOK, FFT_WIDTH), st2)
    return z.reshape(N_TOK, FFT_WIDTH)


def _gelu_tanh(x):
    return 0.5 * x * (1.0 + jnp.tanh(math.sqrt(2.0 / math.pi) * (x + 0.044715 * (x * x * x))))


def _mix_kernel(x_ref, er_ref, ec_ref, lg_ref, lb_ref, m1_ref, s1_ref, wgs_ref, wgf_ref, bg_ref,
                yt_ref, zr_ref, wglu_ref, bglu_ref, wbs_ref, wbf_ref, bbf_ref, wo_ref, bo_ref,
                g1_ref, l1g_ref, l1b_ref, m2_ref, s2_ref, wr_ref, br_ref, tri_ref, etri_ref,
                h1_ref, u2_ref, pos_ref, gate_ref, cnt_ref, scr_ref):
    pos = _pos_code(er_ref, ec_ref, TM)

    def front(r0, nr):
        rows = slice(r0, r0 + nr)
        h = _layer_norm(x_ref[rows, :] + pos[rows, :], lg_ref[...], lb_ref[...])
        u = (h * m1_ref[...] + s1_ref[...]).astype(BF16)

        c0, nc = r0 // CH, nr // CH
        for t in range(CH):
            for j in range(NJ):
                scr_ref[j, pl.ds(r0 + t, nc, stride=CH), :] = (
                    yt_ref[t, c0:c0 + nc, j * LANES:(j + 1) * LANES].astype(F32))
        ys = jnp.concatenate([scr_ref[j, rows, :] for j in range(NJ)], axis=-1)
        z = jnp.dot(_gelu_tanh(ys).astype(BF16), wglu_ref[...], preferred_element_type=F32) + bglu_ref[...]
        glu = (z[:, :S5_WIDTH] * _sigmoid(z[:, S5_WIDTH:])).astype(BF16)
        g_s5 = _sigmoid(jnp.dot(u, wgs_ref[...], preferred_element_type=F32) + bg_ref[:, :D])
        g_fft = _sigmoid(jnp.dot(u, wgf_ref[...], preferred_element_type=F32) + bg_ref[:, D:])
        y_s5 = jnp.dot(glu, wbs_ref[...], preferred_element_type=F32)
        y_fft = jnp.dot(zr_ref[rows, :], wbf_ref[...], preferred_element_type=F32) + bbf_ref[...]
        mixed = (g_s5 * y_s5 + g_fft * y_fft).astype(BF16)
        y = jnp.dot(mixed, wo_ref[...], preferred_element_type=F32) + bo_ref[...]
        h1 = _layer_norm(ALPHA * h + g1_ref[...] * y, l1g_ref[...], l1b_ref[...])
        h1_ref[rows, :] = h1
        u2 = h1 * m2_ref[...] + s2_ref[...]
        u2_ref[rows, :] = u2.astype(BF16)
        u_hi = u2.astype(BF16)
        u_lo = (u2 - u_hi.astype(F32)).astype(BF16)

        def nt(a, b):
            return lax.dot_general(a, b, (((1,), (1,)), ((), ())), preferred_element_type=F32)
        return nt(wr_ref[0], u_hi) + nt(wr_ref[0], u_lo) + nt(wr_ref[1], u_hi)

    logits = front(0, TM) + br_ref[:, 0:1]
    eidx = lax.broadcasted_iota(jnp.int32, (N_EXPERTS, TM), 0)
    vals, hots = [], []
    cur = logits
    for _k in range(TOP_K):
        m = jnp.max(cur, axis=0, keepdims=True)
        sel = jnp.min(jnp.where(cur == m, eidx, N_EXPERTS), axis=0, keepdims=True)
        hot = eidx == sel
        cur = jnp.where(hot, -jnp.inf, cur)
        vals.append(m)
        hots.append(hot)
    exps = [jnp.exp(v - vals[0]) for v in vals]
    den = exps[0] + exps[1] + exps[2] + exps[3]
    gate4 = jnp.concatenate([e / den for e in exps], axis=0)

    hot_sum = (hots[0] | hots[1] | hots[2] | hots[3]).astype(F32)
    before = jnp.dot(hot_sum.astype(BF16), tri_ref[...], preferred_element_type=F32)
    cnt = jnp.broadcast_to(jnp.sum(hot_sum, axis=1, keepdims=True), (N_EXPERTS, LANES))
    cnt8 = jnp.floor((cnt + (SEG_ALIGN - 1)) * (1.0 / SEG_ALIGN)) * SEG_ALIGN
    seg0 = jnp.dot(etri_ref[...], cnt8.astype(BF16), preferred_element_type=F32)
    tot = seg0[:, 0:1] + before
    pos4 = jnp.concatenate(
        [jnp.sum(jnp.where(hk, tot, 0.0), axis=0, keepdims=True) for hk in hots], axis=0)
    pos_ref[...] = pos4.astype(jnp.int32)
    cnt_ref[0] = cnt

    gate_ref[...] = gate4


def _mix(x, emb_r, emb_c, lg, lb, m1, s1, w_in, bg, y_t, zr, wglu, bglu, wbs, wbf, bbf, wo, bo,
         g1, l1g, l1b, m2, s2, wr_t, br, tri, etri):
    gate_cols = (S5_WIDTH + FFT_WIDTH) // D
    vec = pl.BlockSpec((1, D), lambda i: (0, 0))

    def full(a):
        return pl.BlockSpec(a.shape, lambda i: (0,) * a.ndim)
    return pl.pallas_call(
        _mix_kernel,
        grid=(N_TOK // TM,),
        in_specs=[pl.BlockSpec((TM, D), lambda i: (i, 0)),
                  pl.BlockSpec((TM // GRID_W, D // 2), lambda i: (i, 0)),
                  pl.BlockSpec((GRID_W, D // 2), lambda i: (0, 0)),
                  vec, vec, vec, vec,
                  pl.BlockSpec((D, D), lambda i: (0, gate_cols)),
                  pl.BlockSpec((D, D), lambda i: (0, gate_cols + 1)), full(bg),
                  pl.BlockSpec((CH, TM // CH, S5_WIDTH), lambda i: (0, i, 0)),
                  pl.BlockSpec((TM, FFT_WIDTH), lambda i: (i, 0)),
                  full(wglu), full(bglu), full(wbs), full(wbf), full(bbf), full(wo), full(bo),
                  vec, vec, vec, vec, vec, full(wr_t), full(br), full(tri), full(etri)],
        out_specs=(pl.BlockSpec((TM, D), lambda i: (i, 0)),
                   pl.BlockSpec((TM, D), lambda i: (i, 0)),
                   pl.BlockSpec((TOP_K, TM), lambda i: (0, i)),
                   pl.BlockSpec((TOP_K, TM), lambda i: (0, i)),
                   pl.BlockSpec((1, N_EXPERTS, LANES), lambda i: (i, 0, 0))),
        out_shape=(jax.ShapeDtypeStruct((N_TOK, D), F32),
                   jax.ShapeDtypeStruct((N_TOK, D), BF16),
                   jax.ShapeDtypeStruct((TOP_K, N_TOK), jnp.int32),
                   jax.ShapeDtypeStruct((TOP_K, N_TOK), F32),
                   jax.ShapeDtypeStruct((N_TILES, N_EXPERTS, LANES), F32)),
        scratch_shapes=[pltpu.VMEM((NJ, TM, LANES), F32)],
        compiler_params=_cparams(("parallel",)),
        name="mix",
    )(x, emb_r, emb_c, lg, lb, m1, s1, w_in, w_in, bg, y_t, zr, wglu, bglu, wbs, wbf, bbf, wo, bo,
      g1, l1g, l1b, m2, s2, wr_t, br, tri, etri)


def _on_parity(i, fn):
    @pl.when(i % 2 == 0)
    def _():
        fn(0)

    @pl.when(i % 2 == 1)
    def _():
        fn(1)


def _dispatch_kernel(pend_ref, fill_ref, dprev_ref, dest_ref, pos_ref, u_ref, buf_ref,
                     sorted_ref, zero_ref, zsem, esem, tsem, sems):
    i = pl.program_id(0)

    units = BM // ZB

    def clear_copy(start, sem):
        return pltpu.make_async_copy(
            zero_ref, buf_ref.at[pl.ds(pl.multiple_of(start, ZB), ZB)], sem)

    def unused_units(fn):
        def unit(b, c):
            fn(clear_copy(b * ZB, tsem))
            return c
        lax.fori_loop(pend_ref[N_EXPERTS - 1] // ZB, N_BLOCKS * units, unit, 0)

    def padding_units(fn):
        def expert(e, c):
            for z in range(units):
                start = pend_ref[e] - (z + 1) * ZB

                @pl.when(start + ZB > fill_ref[e])
                def _():
                    fn(clear_copy(start, esem))
            return c
        lax.fori_loop(0, N_EXPERTS, expert, 0)

    def spare_units(fn):
        def unit(b, c):
            fn(clear_copy(b * ZB, zsem))
            return c
        lax.fori_loop(N_BLOCKS * units, N_BLOCKS_ALL * units, unit, 0)

    @pl.when(i == 0)
    def _():
        zero_ref[...] = jnp.zeros_like(zero_ref)
        spare_units(lambda cp: cp.start())
        padding_units(lambda cp: cp.start())
        unused_units(lambda cp: cp.start())
        spare_units(lambda cp: cp.wait())
        sorted_ref[...] = jnp.zeros_like(sorted_ref)

    @pl.when(i == 1)
    def _():
        padding_units(lambda cp: cp.wait())

    def chunk_copy(slot, table_ref, j):
        dst = pl.multiple_of(table_ref[0, 0, j], SEG_ALIGN)
        return pltpu.make_async_copy(sorted_ref.at[slot, pl.ds(j * SEG_ALIGN, SEG_ALIGN)],
                                     buf_ref.at[pl.ds(dst, SEG_ALIGN)], sems.at[slot])

    def drain(slot):
        pltpu.make_async_copy(sorted_ref.at[slot], buf_ref.at[pl.ds(0, CAP)], sems.at[slot]).wait()

    def run(slot):
        pos = pos_ref[...]
        u = u_ref[...]
        n_rb = CAP // CAP_BLOCK
        per_rb = NCHK // (n_rb // 2)
        for rb in range(n_rb):
            for j in range(rb * per_rb, min((rb + 1) * per_rb, NCHK)):
                chunk_copy(1 - slot, dprev_ref, j).start()
            rows = lax.broadcasted_iota(jnp.int32, (CAP_BLOCK, TM), 0) + rb * CAP_BLOCK
            hit = rows == pos[0:1]
            for k in range(1, TOP_K):
                hit = hit | (rows == pos[k:k + 1])
            onehot = jnp.where(hit, 1.0, 0.0).astype(BF16)
            sorted_ref[slot, rb * CAP_BLOCK:(rb + 1) * CAP_BLOCK, :] = jnp.dot(
                onehot, u, preferred_element_type=F32).astype(BF16)
        drain(1 - slot)

        @pl.when(i == N_TILES - 1)
        def _():
            def issue(j, c):
                chunk_copy(slot, dest_ref, j).start()
                return c
            lax.fori_loop(0, NCHK, issue, 0)
            drain(slot)
            unused_units(lambda cp: cp.wait())
    _on_parity(i, run)


def _dispatch(pad_ends, padded, chunk_table, pos_t, u2):
    return pl.pallas_call(
        _dispatch_kernel,
        grid_spec=pltpu.PrefetchScalarGridSpec(
            num_scalar_prefetch=2,
            grid=(N_TILES,),
            in_specs=[pl.BlockSpec((1, 1, NCHK), lambda i, a, b: (i, 0, 0), memory_space=pltpu.SMEM),
                      pl.BlockSpec((1, 1, NCHK), lambda i, a, b: (i + 1, 0, 0), memory_space=pltpu.SMEM),
                      pl.BlockSpec((TOP_K, TM), lambda i, a, b: (0, i)),
                      pl.BlockSpec((TM, D), lambda i, a, b: (i, 0))],
            out_specs=pl.BlockSpec(memory_space=pl.ANY),
            scratch_shapes=[pltpu.VMEM((2, CAP, D), BF16),
                            pltpu.VMEM((ZB, D), BF16),
                            pltpu.SemaphoreType.DMA(()),
                            pltpu.SemaphoreType.DMA(()),
                            pltpu.SemaphoreType.DMA(()),
                            pltpu.SemaphoreType.DMA((2,))]),
        out_shape=jax.ShapeDtypeStruct((ROWS_ALL, D), BF16),
        compiler_params=_cparams(("arbitrary",)),
        name="dispatch",
    )(pad_ends, padded, chunk_table, chunk_table, pos_t, u2)


def _ffn_kernel(be_ref, nu_ref, run_ref, nxt_ref, valid_ref, x_ref, wu_hbm, bu_ref, wd_hbm, bd_ref,
                y_ref, wu_ref, wd_ref, wub_ref, wdb_ref, sems):
    i = pl.program_id(0)
    used = i < nu_ref[0]

    def weight_copies(e, slot):
        return (pltpu.make_async_copy(wu_hbm.at[e], wu_ref.at[slot], sems.at[slot]),
                pltpu.make_async_copy(wd_hbm.at[e], wd_ref.at[slot], sems.at[slot]))

    @pl.when(used)
    def _():
        run = run_ref[i]

        @pl.when(run >= 0)
        def _():
            def open_run(slot):
                @pl.when(run == 0)
                def _():
                    for cp in weight_copies(be_ref[i], slot):
                        cp.start()

                @pl.when(nxt_ref[i] >= 0)
                def _():
                    for cp in weight_copies(nxt_ref[i], 1 - slot):
                        cp.start()
                for cp in weight_copies(be_ref[i], slot):
                    cp.wait()
                wub_ref[...] = wu_ref[slot].astype(BF16)
                wdb_ref[...] = wd_ref[slot].astype(BF16)
            _on_parity(run, open_run)

        def expert_rows(r0, nr):
            rows = slice(r0, r0 + nr)
            e = be_ref[i]
            h = (jnp.dot(x_ref[rows, :], wub_ref[...], preferred_element_type=F32)
                 + bu_ref[pl.ds(e, 1), :])
            h_glu = jnp.minimum(h[:, :D], SWIGLU_LIMIT)
            h_lin = jnp.clip(h[:, D:], -SWIGLU_LIMIT, SWIGLU_LIMIT)
            act = (h_glu * _sigmoid(SWIGLU_ALPHA * h_glu) * (h_lin + 1.0)).astype(BF16)
            y_ref[rows, :] = (jnp.dot(act, wdb_ref[...], preferred_element_type=F32)
                              + bd_ref[pl.ds(e, 1), :]).astype(BF16)

        valid = valid_ref[i]

        @pl.when(valid == BM)
        def _():
            expert_rows(0, BM)

        @pl.when(valid < BM)
        def _():
            for h0 in range(0, BM, FFN_HALF):
                @pl.when(valid >= h0 + FFN_HALF)
                def _(h0=h0):
                    expert_rows(h0, FFN_HALF)

                @pl.when(valid < h0 + FFN_HALF)
                def _(h0=h0):
                    for r0 in range(h0, h0 + FFN_HALF, FFN_TAIL):
                        @pl.when(r0 < valid)
                        def _(r0=r0):
                            expert_rows(r0, FFN_TAIL)

                        @pl.when(r0 >= valid)
                        def _(r0=r0):
                            y_ref[r0:r0 + FFN_TAIL, :] = jnp.zeros((FFN_TAIL, D), BF16)

    @pl.when(jnp.logical_not(used))
    def _():
        y_ref[...] = jnp.zeros_like(y_ref)


def _ffn(block_expert, n_used, run_id, next_expert, valid, buf, w_up, b_up, w_down, b_down):
    def blk(i, be, nu, *_):
        return jnp.minimum(i, nu[0] - 1)
    return pl.pallas_call(
        _ffn_kernel,
        grid_spec=pltpu.PrefetchScalarGridSpec(
            num_scalar_prefetch=5,
            grid=(N_BLOCKS_ALL,),
            in_specs=[pl.BlockSpec((BM, D), lambda i, *s: (blk(i, *s), 0)),
                      pl.BlockSpec(memory_space=pl.ANY),
                      pl.BlockSpec((N_EXPERTS, 2 * D), lambda i, *s: (0, 0)),
                      pl.BlockSpec(memory_space=pl.ANY),
                      pl.BlockSpec((N_EXPERTS, D), lambda i, *s: (0, 0))],
            out_specs=pl.BlockSpec((BM, D), lambda i, *s: (i, 0)),
            scratch_shapes=[pltpu.VMEM((2, D, 2 * D), F32),
                            pltpu.VMEM((2, D, D), F32),
                            pltpu.VMEM((D, 2 * D), BF16),
                            pltpu.VMEM((D, D), BF16),
                            pltpu.SemaphoreType.DMA((2,))]),
        out_shape=jax.ShapeDtypeStruct((ROWS_ALL, D), BF16),
        compiler_params=_cparams(("arbitrary",)),
        name="ffn",
    )(block_expert, n_used, run_id, next_expert, valid, buf, w_up, b_up, w_down, b_down)


def _combine_kernel(dest_ref, dnext_ref, y_ref, h1_ref, pos_ref, gate_ref, g2_ref, lg_ref, lb_ref,
                    o_ref, sorted_ref, sems):
    i = pl.program_id(0)

    def chunk_copy(slot, table_ref, j):
        src = pl.multiple_of(table_ref[0, 0, j], SEG_ALIGN)
        return pltpu.make_async_copy(y_ref.at[pl.ds(src, SEG_ALIGN)],
                                     sorted_ref.at[slot, pl.ds(j * SEG_ALIGN, SEG_ALIGN)],
                                     sems.at[slot])

    def drain(slot):
        pltpu.make_async_copy(y_ref.at[pl.ds(0, CAP)], sorted_ref.at[slot], sems.at[slot]).wait()

    @pl.when(i == 0)
    def _():
        def issue(j, c):
            chunk_copy(0, dest_ref, j).start()
            return c
        lax.fori_loop(0, NCHK, issue, 0)

    def run(slot):
        drain(slot)

        pos = pos_ref[...]
        gate = gate_ref[...]
        m = jnp.zeros((TM, D), F32)
        n_cb = CAP // CAP_BLOCK
        per_cb = NCHK // (n_cb // 2)
        for cb in range(n_cb):
            for j in range(cb * per_cb, min((cb + 1) * per_cb, NCHK)):
                chunk_copy(1 - slot, dnext_ref, j).start()
            rws = lax.broadcasted_iota(jnp.int32, (CAP_BLOCK, TM), 0) + cb * CAP_BLOCK
            g = jnp.where(rws == pos[0:1], gate[0:1], 0.0)
            for k in range(1, TOP_K):
                g = g + jnp.where(rws == pos[k:k + 1], gate[k:k + 1], 0.0)
            rows = sorted_ref[slot, cb * CAP_BLOCK:(cb + 1) * CAP_BLOCK, :]
            m = m + lax.dot_general(g.astype(BF16), rows, (((0,), (0,)), ((), ())),
                                    preferred_element_type=F32)
        o_ref[...] = _layer_norm(ALPHA * h1_ref[...] + g2_ref[...] * m, lg_ref[...], lb_ref[...])

        @pl.when(i == N_TILES - 1)
        def _():
            drain(1 - slot)
    _on_parity(i, run)


def _combine(chunk_table, y_buf, h1, pos_t, gate_t, g2, lg, lb):
    vec = pl.BlockSpec((1, D), lambda i: (0, 0))
    return pl.pallas_call(
        _combine_kernel,
        grid_spec=pltpu.PrefetchScalarGridSpec(
            num_scalar_prefetch=0,
            grid=(N_TILES,),
            in_specs=[pl.BlockSpec((1, 1, NCHK), lambda i: (i + 1, 0, 0), memory_space=pltpu.SMEM),
                      pl.BlockSpec((1, 1, NCHK), lambda i: (i + 2, 0, 0), memory_space=pltpu.SMEM),
                      pl.BlockSpec(memory_space=pl.ANY),
                      pl.BlockSpec((TM, D), lambda i: (i, 0)),
                      pl.BlockSpec((TOP_K, TM), lambda i: (0, i)),
                      pl.BlockSpec((TOP_K, TM), lambda i: (0, i)),
                      vec, vec, vec],
            out_specs=pl.BlockSpec((TM, D), lambda i: (i, 0)),
            scratch_shapes=[pltpu.VMEM((2, CAP, D), BF16),
                            pltpu.SemaphoreType.DMA((2,))]),
        out_shape=jax.ShapeDtypeStruct((N_TOK, D), F32),
        compiler_params=_cparams(("arbitrary",)),
        name="combine",
    )(chunk_table, chunk_table, y_buf, h1, pos_t, gate_t, g2, lg, lb)


def _sincos_tables():
    q = D // 4
    omega = 1.0 / (10000.0 ** (np.arange(q) / q))

    def emb(n):
        ang = np.arange(n)[:, None] * omega[None, :]
        return jnp.asarray(np.concatenate([np.sin(ang), np.cos(ang)], axis=-1), F32)
    return emb(N_TOK // GRID_W), emb(GRID_W)


def kernel(x, c, ctx, c_ctx, ln_in_g, ln_in_b, w_ada, b_ada, w_in, b_in, s5_lambda_re, s5_lambda_im, s5_log_dt, s5_b_re, s5_b_im, s5_c_re, s5_c_im, s5_d, w_glu, b_glu, w_br_s5, w_br_fft, b_br_fft, w_out, b_out, ln1_g, ln1_b, w_router, b_router, w_up, b_up, w_down, b_down, ln2_g, ln2_b):
    assert x.shape == (1, N_TOK, D) and ctx.shape == (1, N_CTX, D) and w_ada.shape[0] == 1
    row = lambda v: v.reshape(1, -1).astype(F32)

    cc = jnp.concatenate([c.reshape(1, D), c_ctx.reshape(1, D), jnp.zeros((SUBLANES - 2, D), F32)], axis=0)
    ada = _ada(cc, w_ada[0], row(b_ada[0]))
    sh1, sc1, g1, sh2, sc2, g2 = (ada[0:1, k * D:(k + 1) * D] for k in range(6))
    sh1c, sc1c = ada[1:2, 0:D], ada[1:2, D:2 * D]

    emb_r, emb_c = _sincos_tables()
    st1, st2, fc = _dft_tables()
    lg, lb = row(ln_in_g), row(ln_in_b)

    w_in_bf = w_in[0].astype(BF16)
    b_s5 = row(b_in[0][:S5_WIDTH])
    b_fft8 = jnp.concatenate([row(b_in[0][S5_WIDTH:S5_WIDTH + FFT_WIDTH]),
                              jnp.zeros((SUBLANES - 1, FFT_WIDTH), F32)], axis=0)
    b_g = row(b_in[0][S5_WIDTH + FFT_WIDTH:])
    w_fc, b_fc = _fft_weights(w_in[0], b_fft8, fc)
    bcat = jnp.concatenate([b_s5, b_fc[0:1]], axis=1)

    x2 = x[0]
    p_t, xr, xi = _proj(x2, emb_r, emb_c, lg, lb, 1.0 + sc1, sh1, w_in_bf, w_fc, bcat)
    pc_t = _ctx_proj(ctx[0], lg, lb, 1.0 + sc1c, sh1c, w_in_bf[:, :S5_WIDTH], b_s5)

    b_c, a_q, a_p, trans, ctx_w = _s5_tables(
        s5_lambda_re[0], s5_lambda_im[0], s5_log_dt[0], s5_b_re[0], s5_b_im[0],
        s5_c_re[0], s5_c_im[0], s5_d[0])
    y_t = _s5(p_t, pc_t, b_c, a_q, a_p, trans, ctx_w)

    yr, yi = _fft1(xr, xi, st1)
    zr = _fft2(yr, yi, st2)

    tri = jnp.asarray(np.arange(TM)[:, None] < np.arange(TM)[None, :], BF16)
    br = jnp.broadcast_to(b_router[0].reshape(N_EXPERTS, 1), (N_EXPERTS, LANES))
    etri = jnp.asarray(np.arange(N_EXPERTS)[:, None] > np.arange(N_EXPERTS)[None, :], BF16)
    wr_t = jnp.transpose(w_router[0])
    wr_hi = wr_t.astype(BF16)
    wr_split = jnp.stack([wr_hi, (wr_t - wr_hi.astype(F32)).astype(BF16)], axis=0)
    h1, u2, pos_t, gate_t, counts = _mix(
        x2, emb_r, emb_c, lg, lb, 1.0 + sc1, sh1, w_in_bf, b_g, y_t, zr,
        w_glu[0].astype(BF16), row(b_glu[0]), w_br_s5[0].astype(BF16), w_br_fft[0].astype(BF16),
        row(b_br_fft[0]), w_out[0].astype(BF16), row(b_out[0]), g1, row(ln1_g[0]), row(ln1_b[0]),
        1.0 + sc2, sh2, wr_split, br, tri, etri)

    cnt = counts[:, :, 0].astype(jnp.int32)
    seg = (cnt + SEG_ALIGN - 1) // SEG_ALIGN * SEG_ALIGN
    seg_end = jnp.cumsum(seg, axis=1)
    seg_start = seg_end - seg
    padded = (jnp.sum(seg, axis=0) + BM - 1) // BM * BM
    pad_ends = jnp.cumsum(padded)
    seg_dest = (pad_ends - padded)[None, :] + jnp.cumsum(seg, axis=0) - seg
    chunk_row = jnp.arange(NCHK, dtype=jnp.int32) * SEG_ALIGN
    chunk_exp = jnp.minimum(jnp.sum(chunk_row[None, :, None] >= seg_end[:, None, :], axis=-1),
                            N_EXPERTS - 1)
    own = chunk_exp[:, :, None] == jnp.arange(N_EXPERTS, dtype=jnp.int32)[None, None, :]
    chunk_dest = (jnp.sum(jnp.where(own, (seg_dest - seg_start)[:, None, :], 0), axis=-1)
                  + chunk_row[None, :]).astype(jnp.int32).reshape(N_TILES, 1, NCHK)
    nchk = (seg_end[:, -1] // SEG_ALIGN).astype(jnp.int32)
    block_start = jnp.arange(N_BLOCKS_ALL, dtype=jnp.int32) * BM
    block_expert = jnp.minimum(jnp.sum(block_start[:, None] >= pad_ends[None, :], axis=1),
                               N_EXPERTS - 1).astype(jnp.int32)
    n_used = (pad_ends[-1:] // BM).astype(jnp.int32)
    opens = (block_start < pad_ends[-1]) & (
        block_expert != jnp.concatenate([jnp.full((1,), -1, jnp.int32), block_expert[:-1]]))
    run_id = jnp.where(opens, jnp.cumsum(opens.astype(jnp.int32)) - 1, -1).astype(jnp.int32)
    experts = jnp.arange(N_EXPERTS, dtype=jnp.int32)
    later = (experts[None, :] > block_expert[:, None]) & (padded[None, :] > 0)
    next_expert = jnp.min(jnp.where(later, experts[None, :], N_EXPERTS), axis=1)
    next_expert = jnp.where(next_expert < N_EXPERTS, next_expert, -1).astype(jnp.int32)

    spare = (ROWS + chunk_row)[None, None, :]
    chunk_table = jnp.concatenate(
        [spare,
         jnp.where(chunk_row[None, None, :] < (nchk * SEG_ALIGN)[:, None, None], chunk_dest, spare),
         spare], axis=0).astype(jnp.int32)
    fill_ends = (pad_ends - padded + jnp.sum(seg, axis=0)).astype(jnp.int32)
    buf = _dispatch(pad_ends.astype(jnp.int32), fill_ends, chunk_table, pos_t, u2)
    mine = block_expert[:, None] == experts[None, :]
    filled = jnp.sum(jnp.where(mine, fill_ends[None, :], 0), axis=1)
    valid = jnp.clip(filled - block_start, 0, BM).astype(jnp.int32)
    y_buf = _ffn(block_expert, n_used, run_id, next_expert, valid, buf, w_up[0],
                 b_up[0], w_down[0], b_down[0])
    out = _combine(chunk_table, y_buf, h1, pos_t, gate_t, g2, row(ln2_g[0]), row(ln2_b[0]))
    return out.reshape(1, N_TOK, D)
```

```python
import functools
import math

import jax
import jax.numpy as jnp
import numpy as np
from jax import lax
from jax.experimental import pallas as pl
from jax.experimental.pallas import tpu as pltpu

F32 = jnp.float32
BF16 = jnp.bfloat16
HI = lax.Precision.HIGHEST

D = 1024
N_TOK = 16384
N_CTX = 256
GRID_W = 64
S5_GROUP = 16
S5_GROUPS = 32
S5_STATE = 64
S5_WIDTH = 512
FFT_GROUPS = 4
FFT_DIM = 128
FFT_WIDTH = 512
N_EXPERTS = 32
TOP_K = 4
LN_EPS = 1e-5
ALPHA = 2.0 ** 0.25
SWIGLU_ALPHA = 1.702
SWIGLU_LIMIT = 7.0

LANES = 128
SUBLANES = 8
VMEM_LIMIT = 56 * 1024 * 1024

CH = 8
N_CHUNK = N_TOK // CH
N_CHUNK_CTX = N_CTX // CH
NSEG = SUBLANES
SEG = N_CHUNK // NSEG
SCAN_UNROLL = 4
GPT = LANES // S5_GROUP
NJ = S5_WIDTH // LANES
CL = CH * LANES
SW = 4 * GPT * S5_STATE

FN = 128
FB = 16

TM = 512
TM_PROJ = 1024
N_TILES = N_TOK // TM
BM = 1024
FFN_HALF = 512
FFN_TAIL = 128
N_SLOTS = N_TOK * TOP_K
SEG_ALIGN = 2 * SUBLANES
CAP_BLOCK = 256
CAP = -(-(TOP_K * TM + N_EXPERTS * (SEG_ALIGN - 1)) // CAP_BLOCK) * CAP_BLOCK
NCHK = CAP // SEG_ALIGN
N_BLOCKS = -(-(N_SLOTS + N_TILES * N_EXPERTS * (SEG_ALIGN - 1)) // BM) + N_EXPERTS
ROWS = N_BLOCKS * BM
N_BLOCKS_ALL = N_BLOCKS + -(-CAP // BM)
ROWS_ALL = N_BLOCKS_ALL * BM
ZERO_PARTS = 6
ZERO_UNIT = ROWS_ALL // (N_TILES * ZERO_PARTS)
assert ZERO_UNIT * ZERO_PARTS * N_TILES == ROWS_ALL and ZERO_UNIT % SEG_ALIGN == 0


def _cparams(sem):
    return pltpu.CompilerParams(dimension_semantics=sem, vmem_limit_bytes=VMEM_LIMIT)


def _layer_norm(x, g, b):
    mu = jnp.mean(x, axis=-1, keepdims=True)
    xc = x - mu
    var = jnp.mean(xc * xc, axis=-1, keepdims=True)
    return xc * lax.rsqrt(var + LN_EPS) * g + b


def _sigmoid(x):
    return 0.5 * jnp.tanh(0.5 * x) + 0.5


def _ada_kernel(c_ref, w_ref, b_ref, o_ref):
    c = c_ref[...]
    s = c * _sigmoid(c)
    w = w_ref[...]
    s_hi, w_hi = s.astype(BF16), w.astype(BF16)
    s_lo = (s - s_hi.astype(F32)).astype(BF16)
    w_lo = (w - w_hi.astype(F32)).astype(BF16)

    def mm(a, b):
        return jnp.dot(a, b, preferred_element_type=F32)
    o_ref[...] = mm(s_hi, w_hi) + mm(s_lo, w_hi) + mm(s_hi, w_lo) + b_ref[...]


def _ada(cc, w_ada, b_ada):
    nb = 4
    wb = 6 * D // nb
    return pl.pallas_call(
        _ada_kernel,
        grid=(nb,),
        in_specs=[pl.BlockSpec((SUBLANES, D), lambda i: (0, 0)),
                  pl.BlockSpec((D, wb), lambda i: (0, i)),
                  pl.BlockSpec((1, wb), lambda i: (0, i))],
        out_specs=pl.BlockSpec((SUBLANES, wb), lambda i: (0, i)),
        out_shape=jax.ShapeDtypeStruct((SUBLANES, 6 * D), F32),
        compiler_params=_cparams(("parallel",)),
        name="ada",
    )(cc, w_ada, b_ada)


def _fftw_kernel(w_ref, b_ref, f_ref, wo_ref, bo_ref):
    f = f_ref[...]
    w = w_ref[...]
    w_hi, f_hi = w.astype(BF16), f.astype(BF16)
    w_lo = (w - w_hi.astype(F32)).astype(BF16)
    f_lo = (f - f_hi.astype(F32)).astype(BF16)

    def mm(a, b):
        return jnp.dot(a, b, preferred_element_type=F32)
    wo_ref[...] = (mm(w_hi, f_hi) + mm(w_lo, f_hi) + mm(w_hi, f_lo)).astype(BF16)
    bo_ref[...] = jnp.dot(b_ref[...], f, preferred_element_type=F32, precision=HI)


def _fft_weights(w_in, b_fft8, fc):
    def full(a):
        return pl.BlockSpec(a.shape, lambda i: (0,) * a.ndim)
    outs = (jax.ShapeDtypeStruct((D, 2 * FFT_WIDTH), BF16),
            jax.ShapeDtypeStruct((SUBLANES, 2 * FFT_WIDTH), F32))
    return pl.pallas_call(
        _fftw_kernel,
        grid=(1,),
        in_specs=[pl.BlockSpec((D, FFT_WIDTH), lambda i: (0, S5_WIDTH // FFT_WIDTH)),
                  full(b_fft8), full(fc)],
        out_specs=tuple(pl.BlockSpec(o.shape, lambda i: (0, 0)) for o in outs),
        out_shape=outs,
        compiler_params=_cparams(("arbitrary",)),
        name="fftw",
    )(w_in, b_fft8, fc)


def _pos_code(er_ref, ec_ref, tm):
    nr = tm // GRID_W
    er = er_ref[...]
    row = jnp.broadcast_to(er[:, None, :], (nr, GRID_W, D // 2)).reshape(tm, D // 2)
    col = jnp.concatenate([ec_ref[...]] * nr, axis=0)
    return jnp.concatenate([row, col], axis=-1)


def _to_chunk_major(val, scr_ref, out_ref, tm):
    for j in range(NJ):
        scr_ref[j] = val[:, j * LANES:(j + 1) * LANES]
    for t in range(CH):
        for j in range(NJ):
            piece = scr_ref[j, pl.ds(t, tm // CH, stride=CH), :]
            out_ref[t, :, j * LANES:(j + 1) * LANES] = piece.astype(out_ref.dtype)


def _proj_kernel(x_ref, er_ref, ec_ref, lg_ref, lb_ref, m_ref, s_ref, ws_ref, wf_ref, b_ref,
                 p_ref, xr_ref, xi_ref, scr_ref):
    x = x_ref[...] + _pos_code(er_ref, ec_ref, TM_PROJ)
    h = _layer_norm(x, lg_ref[...], lb_ref[...])
    u = (h * m_ref[...] + s_ref[...]).astype(BF16)
    p_s5 = jnp.dot(u, ws_ref[...], preferred_element_type=F32) + b_ref[:, :S5_WIDTH]
    _to_chunk_major(p_s5, scr_ref, p_ref, TM_PROJ)
    p_f = jnp.dot(u, wf_ref[...], preferred_element_type=F32) + b_ref[:, S5_WIDTH:]
    xr_ref[...] = p_f[:, :FFT_WIDTH].astype(BF16)
    xi_ref[...] = p_f[:, FFT_WIDTH:].astype(BF16)


def _proj(x, emb_r, emb_c, lg, lb, m1, s1, w_in, w_fc, bcat):
    nw = bcat.shape[1]
    vec = pl.BlockSpec((1, D), lambda i: (0, 0))
    return pl.pallas_call(
        _proj_kernel,
        grid=(N_TOK // TM_PROJ,),
        in_specs=[pl.BlockSpec((TM_PROJ, D), lambda i: (i, 0)),
                  pl.BlockSpec((TM_PROJ // GRID_W, D // 2), lambda i: (i, 0)),
                  pl.BlockSpec((GRID_W, D // 2), lambda i: (0, 0)),
                  vec, vec, vec, vec,
                  pl.BlockSpec((D, S5_WIDTH), lambda i: (0, 0)),
                  pl.BlockSpec((D, 2 * FFT_WIDTH), lambda i: (0, 0)),
                  pl.BlockSpec((1, nw), lambda i: (0, 0))],
        out_specs=(pl.BlockSpec((CH, TM_PROJ // CH, S5_WIDTH), lambda i: (0, i, 0)),
                   pl.BlockSpec((TM_PROJ, FFT_WIDTH), lambda i: (i, 0)),
                   pl.BlockSpec((TM_PROJ, FFT_WIDTH), lambda i: (i, 0))),
        out_shape=(jax.ShapeDtypeStruct((CH, N_CHUNK, S5_WIDTH), BF16),
                   jax.ShapeDtypeStruct((N_TOK, FFT_WIDTH), BF16),
                   jax.ShapeDtypeStruct((N_TOK, FFT_WIDTH), BF16)),
        scratch_shapes=[pltpu.VMEM((NJ, TM_PROJ, LANES), F32)],
        compiler_params=_cparams(("parallel",)),
        name="proj",
    )(x, emb_r, emb_c, lg, lb, m1, s1, w_in, w_fc, bcat)


def _ctx_proj_kernel(x_ref, lg_ref, lb_ref, m_ref, s_ref, w_ref, b_ref, p_ref, scr_ref):
    h = _layer_norm(x_ref[...], lg_ref[...], lb_ref[...])
    u = (h * m_ref[...] + s_ref[...]).astype(BF16)
    p = jnp.dot(u, w_ref[...], preferred_element_type=F32) + b_ref[...]
    _to_chunk_major(p, scr_ref, p_ref, N_CTX)


def _ctx_proj(ctx, lg, lb, m1, s1, w_s5, b_s5):
    return pl.pallas_call(
        _ctx_proj_kernel,
        out_shape=jax.ShapeDtypeStruct((CH, N_CHUNK_CTX, S5_WIDTH), BF16),
        scratch_shapes=[pltpu.VMEM((NJ, N_CTX, LANES), F32)],
        compiler_params=pltpu.CompilerParams(vmem_limit_bytes=VMEM_LIMIT),
        name="ctxproj",
    )(ctx, lg, lb, m1, s1, w_s5, b_s5)


def _s5_tables(lam_re, lam_im, log_dt, b_re, b_im, c_re, c_im, d_skip):
    dt = jnp.exp(log_dt)[..., None]
    zr = lam_re * dt
    zi = lam_im * dt

    def apow(m):
        m = jnp.asarray(m, F32)
        mag = jnp.exp(zr[..., None] * m)
        return mag * jnp.cos(zi[..., None] * m), mag * jnp.sin(zi[..., None] * m)

    a_re, a_im = apow(jnp.ones((1,), F32))
    a_re, a_im = a_re[..., 0], a_im[..., 0]
    den = lam_re * lam_re + lam_im * lam_im
    num_re = a_re - 1.0
    k_re = (num_re * lam_re + a_im * lam_im) / den
    k_im = (a_im * lam_re - num_re * lam_im) / den
    bb_re = k_re[..., None] * b_re - k_im[..., None] * b_im
    bb_im = k_re[..., None] * b_im + k_im[..., None] * b_re

    ks = jnp.arange(CH + 1, dtype=F32)
    pw_re, pw_im = apow(ks)
    kmag = jnp.exp(zr[:, :, None, :] * ks[None, None, :, None])
    pk_re = kmag * jnp.cos(zi[:, :, None, :] * ks[None, None, :, None])
    pk_im = kmag * jnp.sin(zi[:, :, None, :] * ks[None, None, :, None])
    bt_re, bt_im = jnp.swapaxes(b_re, 2, 3), jnp.swapaxes(b_im, 2, 3)
    bbt_re = k_re[:, :, None, :] * bt_re - k_im[:, :, None, :] * bt_im
    bbt_im = k_re[:, :, None, :] * bt_im + k_im[:, :, None, :] * bt_re

    ar, ai = pw_re[:, :, :, :CH, None], pw_im[:, :, :, :CH, None]
    cr = jnp.swapaxes(c_re, 2, 3)[:, :, :, None, :]
    ci = jnp.swapaxes(c_im, 2, 3)[:, :, :, None, :]
    ca = jnp.concatenate([cr * ar - ci * ai, -(cr * ai + ci * ar)], axis=2)
    ca = ca.reshape(2, S5_GROUPS, 2 * S5_STATE, CH * S5_GROUP)
    bbt = jnp.concatenate([bbt_re, bbt_im], axis=-1)
    taps = jnp.einsum('dghq,dgqn->dghn', bbt, ca, precision=HI)
    skip = (d_skip.reshape(S5_GROUPS, S5_GROUP, 1) * jnp.eye(S5_GROUP, dtype=F32)[None])
    taps = taps.at[0, :, :, :S5_GROUP].add(skip)
    b_c = jnp.transpose(taps.reshape(2, NJ, GPT, S5_GROUP, CH, S5_GROUP), (1, 4, 0, 2, 3, 5))
    b_c = b_c.reshape(NJ, 2 * CH, LANES, S5_GROUP)

    ef = (CH - 1) - jnp.arange(CH)
    eb = jnp.arange(CH)

    def q_part(d, e):
        pr = pk_re[d][:, e, None, :]
        pi = pk_im[d][:, e, None, :]
        br = bbt_re[d][:, None, :, :]
        bi = bbt_im[d][:, None, :, :]
        return pr * br - pi * bi, pr * bi + pi * br

    def q_rows(v):
        v = v.reshape(NJ, GPT, CH, S5_GROUP, S5_STATE)
        return jnp.transpose(v, (0, 2, 1, 3, 4)).reshape(NJ, CL, S5_STATE)
    a_q = jnp.stack([q_rows(v) for v in q_part(0, ef) + q_part(1, eb)], axis=0)

    of = jnp.arange(CH) + 1
    ob = CH - jnp.arange(CH)

    def p_part(d, e):
        pr = pw_re[d][..., e][:, :, :, None]
        pi = pw_im[d][..., e][:, :, :, None]
        return (ct_re[d] * pr - ct_im[d] * pi, -(ct_re[d] * pi + ct_im[d] * pr))
    ct_re = jnp.swapaxes(c_re, 2, 3)[:, :, :, None, :]
    ct_im = jnp.swapaxes(c_im, 2, 3)[:, :, :, None, :]
    a_p = jnp.stack([v.reshape(NJ, GPT * S5_STATE, CH * S5_GROUP)
                     for v in p_part(0, of) + p_part(1, ob)], axis=0)

    def lanes(v):
        return jnp.transpose(v.reshape(2, NJ, GPT * S5_STATE), (1, 0, 2))
    c_r, c_i = apow(jnp.full((1,), float(CH), F32))
    s_r, s_i = apow(jnp.full((1,), float(CH * SEG), F32))
    cr, ci, sr, si = (lanes(v[..., 0]) for v in (c_r, c_i, s_r, s_i))
    trans = jnp.stack([cr[:, 0], ci[:, 0], cr[:, 1], ci[:, 1],
                       sr[:, 0], si[:, 0], sr[:, 1], si[:, 1]], axis=1)

    cidx = jnp.arange(N_CHUNK_CTX, dtype=F32)
    wf_r, wf_i = apow(CH * (N_CHUNK_CTX - 1 - cidx))
    wb_r, wb_i = apow(CH * cidx)

    def ctx_lanes(v, d):
        return jnp.transpose(v[d].reshape(NJ, GPT * S5_STATE, N_CHUNK_CTX), (0, 2, 1))
    ctx_w = jnp.stack([ctx_lanes(wf_r, 0), ctx_lanes(wf_i, 0),
                       ctx_lanes(wb_r, 1), ctx_lanes(wb_i, 1)], axis=1)
    return b_c, a_q, a_p, trans, ctx_w


def _build_operators(bc_ref, aq_ref, ap_ref, c16_ref, c64_ref, cm_ref, wm_ref, wq_ref, wp_ref):
    def expand(a, c, row_shift, col_shift):
        w = jnp.dot(a, c, preferred_element_type=F32)
        rg = (lax.broadcasted_iota(jnp.int32, (w.shape[0], 1), 0) >> row_shift) & (GPT - 1)
        cg = (lax.broadcasted_iota(jnp.int32, (1, w.shape[1]), 1) >> col_shift) & (GPT - 1)
        return jnp.where(rg == cg, w, 0.0)

    blk = [expand(bc_ref[0, kd], c16_ref[...], 4, 4) for kd in range(2 * CH)]
    for t in range(CH):
        for u in range(CH):
            b = blk[2 * (u - t)] if u > t else blk[2 * (t - u) + 1] if u < t else blk[0] + blk[1]
            wm_ref[0, t * LANES:(t + 1) * LANES, u * LANES:(u + 1) * LANES] = b.astype(BF16)
    half = GPT * S5_STATE
    for s in range(4):
        wq_ref[0, :, s * half:(s + 1) * half] = expand(aq_ref[s, 0], c64_ref[...], 4, 6).astype(BF16)
        wp_ref[0, s * half:(s + 1) * half, :] = expand(ap_ref[s, 0], cm_ref[...], 6, 4).astype(BF16)


def _s5_kernel(p_ref, pc_ref, bc_ref, aq_ref, ap_ref, c16_ref, c64_ref, cm_ref, tr_ref, cw_ref,
               y_ref, v_ref, wm_ref, wq_ref, wp_ref):
    nq = NJ
    half = GPT * S5_STATE
    _build_operators(bc_ref, aq_ref, ap_ref, c16_ref, c64_ref, cm_ref, wm_ref, wq_ref, wp_ref)

    def chunk_rows(ref, r0, nrows):
        return jnp.concatenate([ref[t, pl.ds(r0, nrows), :] for t in range(CH)], axis=-1)

    def fill(k, c):
        r0 = pl.multiple_of(k * SEG, SEG)
        v = jnp.dot(chunk_rows(p_ref, r0, SEG), wq_ref[0], preferred_element_type=F32)
        for s in range(4 * nq):
            v_ref[s, pl.ds(k, SEG, stride=NSEG), :] = v[:, s * LANES:(s + 1) * LANES]
        return c
    lax.fori_loop(0, NSEG, fill, 0)

    vc = jnp.dot(chunk_rows(pc_ref, 0, N_CHUNK_CTX), wq_ref[0], preferred_element_type=F32)
    vfr, vfi, vbr, vbi = (vc[:, i * half:(i + 1) * half] for i in range(4))
    wfr, wfi, wbr, wbi = (cw_ref[0, i] for i in range(4))
    s0_fr = jnp.sum(wfr * vfr - wfi * vfi, axis=0, keepdims=True)
    s0_fi = jnp.sum(wfr * vfi + wfi * vfr, axis=0, keepdims=True)
    s0_br = jnp.sum(wbr * vbr - wbi * vbi, axis=0, keepdims=True)
    s0_bi = jnp.sum(wbr * vbi + wbi * vbr, axis=0, keepdims=True)

    tr = tr_ref[0]
    afr, afi, abr, abi = (jnp.broadcast_to(tr[i:i + 1], (NSEG, half)) for i in range(4))
    gfr, gfi, gbr, gbi = (tr[i:i + 1] for i in range(4, 8))

    def load_part(part, i):
        return jnp.concatenate(
            [v_ref[part * nq + q, pl.ds(pl.multiple_of(i * NSEG, NSEG), NSEG), :] for q in range(nq)],
            axis=-1)

    def store_part(part, i, val):
        for q in range(nq):
            v_ref[part * nq + q, pl.ds(pl.multiple_of(i * NSEG, NSEG), NSEG), :] = (
                val[:, q * LANES:(q + 1) * LANES])

    def step(i, carry, write):
        fr, fi, br, bi = carry
        ib = SEG - 1 - i
        ufr, ufi = load_part(0, i), load_part(1, i)
        ubr, ubi = load_part(2, ib), load_part(3, ib)
        if write:
            store_part(0, i, fr)
            store_part(1, i, fi)
            store_part(2, ib, br)
            store_part(3, ib, bi)
        return (afr * fr - afi * fi + ufr, afr * fi + afi * fr + ufi,
                abr * br - abi * bi + ubr, abr * bi + abi * br + ubi)

    zero = jnp.zeros((NSEG, half), F32)
    ffr, ffi, fbr, fbi = lax.fori_loop(0, SEG, functools.partial(step, write=False),
                                       (zero, zero, zero, zero), unroll=SCAN_UNROLL)

    rows_fr, rows_fi = [s0_fr], [s0_fi]
    for k in range(1, NSEG):
        pr, pi = rows_fr[-1], rows_fi[-1]
        rows_fr.append(gfr * pr - gfi * pi + ffr[k - 1:k])
        rows_fi.append(gfr * pi + gfi * pr + ffi[k - 1:k])
    rows_br, rows_bi = [s0_br], [s0_bi]
    for k in range(NSEG - 2, -1, -1):
        pr, pi = rows_br[0], rows_bi[0]
        rows_br.insert(0, gbr * pr - gbi * pi + fbr[k + 1:k + 2])
        rows_bi.insert(0, gbr * pi + gbi * pr + fbi[k + 1:k + 2])
    init = tuple(jnp.concatenate(r, axis=0) for r in (rows_fr, rows_fi, rows_br, rows_bi))

    lax.fori_loop(0, SEG, functools.partial(step, write=True), init, unroll=SCAN_UNROLL)

    def emit(k, c):
        r0 = pl.multiple_of(k * SEG, SEG)
        b = chunk_rows(p_ref, r0, SEG)
        sin = jnp.concatenate([v_ref[s, pl.ds(k, SEG, stride=NSEG), :] for s in range(4 * nq)], axis=-1)
        y = (jnp.dot(b, wm_ref[0], preferred_element_type=F32)
             + jnp.dot(sin.astype(BF16), wp_ref[0], preferred_element_type=F32))
        for t in range(CH):
            y_ref[t, pl.ds(r0, SEG), :] = y[:, t * LANES:(t + 1) * LANES].astype(y_ref.dtype)
        return c
    lax.fori_loop(0, NSEG, emit, 0)


def _s5(p_t, pc_t, b_c, a_q, a_p, trans, ctx_w):
    rep = np.ones((1, GPT))
    c16 = jnp.asarray(np.kron(rep, np.eye(S5_GROUP)), F32).astype(BF16)
    c64 = jnp.asarray(np.kron(rep, np.eye(S5_STATE)), F32).astype(BF16)
    c_m = jnp.asarray(np.kron(np.eye(CH), np.kron(rep, np.eye(S5_GROUP))), F32).astype(BF16)
    b_c, a_q, a_p = b_c.astype(BF16), a_q.astype(BF16), a_p.astype(BF16)
    half = GPT * S5_STATE
    return pl.pallas_call(
        _s5_kernel,
        grid=(NJ,),
        in_specs=[pl.BlockSpec((CH, N_CHUNK, LANES), lambda j: (0, 0, j)),
                  pl.BlockSpec((CH, N_CHUNK_CTX, LANES), lambda j: (0, 0, j)),
                  pl.BlockSpec((1, 2 * CH, LANES, S5_GROUP), lambda j: (j, 0, 0, 0)),
                  pl.BlockSpec((4, 1, CL, S5_STATE), lambda j: (0, j, 0, 0)),
                  pl.BlockSpec((4, 1, half, CH * S5_GROUP), lambda j: (0, j, 0, 0)),
                  pl.BlockSpec(c16.shape, lambda j: (0, 0)),
                  pl.BlockSpec(c64.shape, lambda j: (0, 0)),
                  pl.BlockSpec(c_m.shape, lambda j: (0, 0)),
                  pl.BlockSpec((1, SUBLANES, half), lambda j: (j, 0, 0)),
                  pl.BlockSpec((1, 4, N_CHUNK_CTX, half), lambda j: (j, 0, 0, 0))],
        out_specs=pl.BlockSpec((CH, N_CHUNK, LANES), lambda j: (0, 0, j)),
        out_shape=jax.ShapeDtypeStruct((CH, N_CHUNK, S5_WIDTH), BF16),
        scratch_shapes=[pltpu.VMEM((4 * NJ, N_CHUNK, LANES), F32),
                        pltpu.VMEM((1, CL, CL), BF16),
                        pltpu.VMEM((1, CL, SW), BF16),
                        pltpu.VMEM((1, SW, CL), BF16)],
        compiler_params=_cparams(("parallel",)),
        name="s5",
    )(p_t, pc_t, b_c, a_q, a_p, c16, c64, c_m, trans, ctx_w)


def _dft_tables():
    n = np.arange(FN)
    ang = 2.0 * np.pi * np.outer(n, n) / FN
    c, s = np.cos(ang), np.sin(ang)
    st1 = np.block([[c, s], [-s, c]])
    tw = 2.0 * np.pi * np.outer(n, n) / (FN * FN)
    wr, wi = np.cos(tw), -np.sin(tw)
    fr = c[None] * wr[:, None, :] + s[None] * wi[:, None, :]
    fi = c[None] * wi[:, None, :] - s[None] * wr[:, None, :]
    st2 = np.concatenate([fr, -fi], axis=-1)
    scale = 1.0 / math.sqrt(N_TOK * FFT_DIM)
    blk_c = np.kron(np.eye(FFT_GROUPS), c) * scale
    blk_s = np.kron(np.eye(FFT_GROUPS), s) * scale
    fc = np.concatenate([blk_c, -blk_s], axis=1)
    return (jnp.asarray(st1, F32).astype(BF16), jnp.asarray(st2, F32).astype(BF16), jnp.asarray(fc, F32))


FSL = FFT_WIDTH // LANES


FBH = FB // SUBLANES


def _block_to_slabs(blk, slab_ref, first, per_half):
    for bh in range(FBH):
        val = blk[:, bh * SUBLANES:(bh + 1) * SUBLANES, :].reshape(FN * SUBLANES, FFT_WIDTH)
        for s in range(FSL):
            slab_ref[bh * per_half + first + s] = val[:, s * LANES:(s + 1) * LANES]


def _slab_rows(b, first, per_half):
    return (b // SUBLANES) * per_half + first, pl.ds(b % SUBLANES, FN, stride=SUBLANES)


def _slabs_to_block(slab_ref, first, per_half):
    halves = []
    for bh in range(FBH):
        val = jnp.concatenate([slab_ref[bh * per_half + first + s] for s in range(FSL)], axis=-1)
        halves.append(val.reshape(FN, SUBLANES, FFT_WIDTH))
    return jnp.concatenate(halves, axis=1)


def _fft1_kernel(xr_ref, xi_ref, f_ref, yr_ref, yi_ref, in_ref, out_ref):
    _block_to_slabs(xr_ref[...].astype(F32), in_ref, 0, 2 * FSL)
    _block_to_slabs(xi_ref[...].astype(F32), in_ref, FSL, 2 * FSL)
    for b in range(FB):
        def part(first):
            base, rows = _slab_rows(b, first, 2 * FSL)
            return jnp.concatenate([in_ref[base + s, rows, :] for s in range(FSL)], axis=-1)
        xs = jnp.concatenate([part(0), part(FSL)], axis=0).astype(BF16)
        y = jnp.dot(f_ref[...], xs, preferred_element_type=F32)
        base, rows = _slab_rows(b, 0, 2 * FSL)
        for s in range(FSL):
            out_ref[base + s, rows, :] = y[:FN, s * LANES:(s + 1) * LANES]
            out_ref[base + FSL + s, rows, :] = y[FN:, s * LANES:(s + 1) * LANES]
    yr_ref[...] = _slabs_to_block(out_ref, 0, 2 * FSL).astype(BF16)
    yi_ref[...] = _slabs_to_block(out_ref, FSL, 2 * FSL).astype(BF16)


def _fft1(xr, xi, st1):
    spec = pl.BlockSpec((FN, FB, FFT_WIDTH), lambda i: (0, i, 0))
    slabs = pltpu.VMEM((FBH * 2 * FSL, FN * SUBLANES, LANES), F32)
    return pl.pallas_call(
        _fft1_kernel,
        grid=(FN // FB,),
        in_specs=[spec, spec, pl.BlockSpec((2 * FN, 2 * FN), lambda i: (0, 0))],
        out_specs=(spec, spec),
        out_shape=(jax.ShapeDtypeStruct((FN, FN, FFT_WIDTH), BF16),) * 2,
        scratch_shapes=[slabs, slabs],
        compiler_params=_cparams(("parallel",)),
        name="fft1",
    )(xr.reshape(FN, FN, FFT_WIDTH), xi.reshape(FN, FN, FFT_WIDTH), st1)


def _fft2_kernel(yr_ref, yi_ref, f_ref, z_ref, out_ref):
    for b in range(FB):
        ys = jnp.concatenate([yr_ref[b * FN:(b + 1) * FN, :], yi_ref[b * FN:(b + 1) * FN, :]], axis=0)
        z = jnp.dot(f_ref[b], ys, preferred_element_type=F32)
        base, rows = _slab_rows(b, 0, FSL)
        for s in range(FSL):
            out_ref[base + s, rows, :] = z[:, s * LANES:(s + 1) * LANES]
    z_ref[...] = _slabs_to_block(out_ref, 0, FSL).astype(BF16)


def _fft2(yr, yi, st2):
    rows = pl.BlockSpec((FB * FN, FFT_WIDTH), lambda i: (i, 0))
    z = pl.pallas_call(
        _fft2_kernel,
        grid=(FN // FB,),
        in_specs=[rows, rows, pl.BlockSpec((FB, FN, 2 * FN), lambda i: (i, 0, 0))],
        out_specs=pl.BlockSpec((FN, FB, FFT_WIDTH), lambda i: (0, i, 0)),
        out_shape=jax.ShapeDtypeStruct((FN, FN, FFT_WIDTH), BF16),
        scratch_shapes=[pltpu.VMEM((FBH * FSL, FN * SUBLANES, LANES), F32)],
        compiler_params=_cparams(("parallel",)),
        name="fft2",
    )(yr.reshape(N_TOK, FFT_WIDTH), yi.reshape(N_TOK, FFT_WIDTH), st2)
    return z.reshape(N_TOK, FFT_WIDTH)


def _gelu_tanh(x):
    return 0.5 * x * (1.0 + jnp.tanh(math.sqrt(2.0 / math.pi) * (x + 0.044715 * (x * x * x))))


def _mix_kernel(x_ref, er_ref, ec_ref, lg_ref, lb_ref, m1_ref, s1_ref, wgs_ref, wgf_ref, bg_ref,
                yt_ref, zr_ref, wglu_ref, bglu_ref, wbs_ref, wbf_ref, bbf_ref, wo_ref, bo_ref,
                g1_ref, l1g_ref, l1b_ref, m2_ref, s2_ref, wr_ref, br_ref, tri_ref, etri_ref,
                h1_ref, u2_ref, pos_ref, gate_ref, cnt_ref, rows_ref, scr_ref, zero_ref, zsem):
    zero_ref[...] = jnp.zeros_like(zero_ref)
    base = pl.program_id(0) * (ZERO_PARTS * ZERO_UNIT)
    clears = [pltpu.make_async_copy(
        zero_ref, rows_ref.at[pl.ds(pl.multiple_of(base + part * ZERO_UNIT, SEG_ALIGN), ZERO_UNIT)],
        zsem) for part in range(ZERO_PARTS)]
    for cp in clears:
        cp.start()
    pos = _pos_code(er_ref, ec_ref, TM)

    def front(r0, nr):
        rows = slice(r0, r0 + nr)
        h = _layer_norm(x_ref[rows, :] + pos[rows, :], lg_ref[...], lb_ref[...])
        u = (h * m1_ref[...] + s1_ref[...]).astype(BF16)

        c0, nc = r0 // CH, nr // CH
        for t in range(CH):
            for j in range(NJ):
                scr_ref[j, pl.ds(r0 + t, nc, stride=CH), :] = (
                    yt_ref[t, c0:c0 + nc, j * LANES:(j + 1) * LANES].astype(F32))
        ys = jnp.concatenate([scr_ref[j, rows, :] for j in range(NJ)], axis=-1)
        z = jnp.dot(_gelu_tanh(ys).astype(BF16), wglu_ref[...], preferred_element_type=F32) + bglu_ref[...]
        glu = (z[:, :S5_WIDTH] * _sigmoid(z[:, S5_WIDTH:])).astype(BF16)
        g_s5 = _sigmoid(jnp.dot(u, wgs_ref[...], preferred_element_type=F32) + bg_ref[:, :D])
        g_fft = _sigmoid(jnp.dot(u, wgf_ref[...], preferred_element_type=F32) + bg_ref[:, D:])
        y_s5 = jnp.dot(glu, wbs_ref[...], preferred_element_type=F32)
        y_fft = jnp.dot(zr_ref[rows, :], wbf_ref[...], preferred_element_type=F32) + bbf_ref[...]
        mixed = (g_s5 * y_s5 + g_fft * y_fft).astype(BF16)
        y = jnp.dot(mixed, wo_ref[...], preferred_element_type=F32) + bo_ref[...]
        h1 = _layer_norm(ALPHA * h + g1_ref[...] * y, l1g_ref[...], l1b_ref[...])
        h1_ref[rows, :] = h1
        u2 = h1 * m2_ref[...] + s2_ref[...]
        u2_ref[rows, :] = u2.astype(BF16)
        u_hi = u2.astype(BF16)
        u_lo = (u2 - u_hi.astype(F32)).astype(BF16)

        def nt(a, b):
            return lax.dot_general(a, b, (((1,), (1,)), ((), ())), preferred_element_type=F32)
        return nt(wr_ref[0], u_hi) + nt(wr_ref[0], u_lo) + nt(wr_ref[1], u_hi)

    logits = front(0, TM) + br_ref[:, 0:1]
    eidx = lax.broadcasted_iota(jnp.int32, (N_EXPERTS, TM), 0)
    vals, hots = [], []
    cur = logits
    for _k in range(TOP_K):
        m = jnp.max(cur, axis=0, keepdims=True)
        sel = jnp.min(jnp.where(cur == m, eidx, N_EXPERTS), axis=0, keepdims=True)
        hot = eidx == sel
        cur = jnp.where(hot, -jnp.inf, cur)
        vals.append(m)
        hots.append(hot)
    exps = [jnp.exp(v - vals[0]) for v in vals]
    den = exps[0] + exps[1] + exps[2] + exps[3]
    gate4 = jnp.concatenate([e / den for e in exps], axis=0)

    hot_sum = (hots[0] | hots[1] | hots[2] | hots[3]).astype(F32)
    before = jnp.dot(hot_sum.astype(BF16), tri_ref[...], preferred_element_type=F32)
    cnt = jnp.broadcast_to(jnp.sum(hot_sum, axis=1, keepdims=True), (N_EXPERTS, LANES))
    cnt8 = jnp.floor((cnt + (SEG_ALIGN - 1)) * (1.0 / SEG_ALIGN)) * SEG_ALIGN
    seg0 = jnp.dot(etri_ref[...], cnt8.astype(BF16), preferred_element_type=F32)
    tot = seg0[:, 0:1] + before
    pos4 = jnp.concatenate(
        [jnp.sum(jnp.where(hk, tot, 0.0), axis=0, keepdims=True) for hk in hots], axis=0)
    pos_ref[...] = pos4.astype(jnp.int32)
    cnt_ref[0] = cnt

    gate_ref[...] = gate4
    for cp in clears:
        cp.wait()


def _mix(x, emb_r, emb_c, lg, lb, m1, s1, w_in, bg, y_t, zr, wglu, bglu, wbs, wbf, bbf, wo, bo,
         g1, l1g, l1b, m2, s2, wr_t, br, tri, etri):
    gate_cols = (S5_WIDTH + FFT_WIDTH) // D
    vec = pl.BlockSpec((1, D), lambda i: (0, 0))

    def full(a):
        return pl.BlockSpec(a.shape, lambda i: (0,) * a.ndim)
    return pl.pallas_call(
        _mix_kernel,
        grid=(N_TOK // TM,),
        in_specs=[pl.BlockSpec((TM, D), lambda i: (i, 0)),
                  pl.BlockSpec((TM // GRID_W, D // 2), lambda i: (i, 0)),
                  pl.BlockSpec((GRID_W, D // 2), lambda i: (0, 0)),
                  vec, vec, vec, vec,
                  pl.BlockSpec((D, D), lambda i: (0, gate_cols)),
                  pl.BlockSpec((D, D), lambda i: (0, gate_cols + 1)), full(bg),
                  pl.BlockSpec((CH, TM // CH, S5_WIDTH), lambda i: (0, i, 0)),
                  pl.BlockSpec((TM, FFT_WIDTH), lambda i: (i, 0)),
                  full(wglu), full(bglu), full(wbs), full(wbf), full(bbf), full(wo), full(bo),
                  vec, vec, vec, vec, vec, full(wr_t), full(br), full(tri), full(etri)],
        out_specs=(pl.BlockSpec((TM, D), lambda i: (i, 0)),
                   pl.BlockSpec((TM, D), lambda i: (i, 0)),
                   pl.BlockSpec((TOP_K, TM), lambda i: (0, i)),
                   pl.BlockSpec((TOP_K, TM), lambda i: (0, i)),
                   pl.BlockSpec((1, N_EXPERTS, LANES), lambda i: (i, 0, 0)),
                   pl.BlockSpec(memory_space=pl.ANY)),
        out_shape=(jax.ShapeDtypeStruct((N_TOK, D), F32),
                   jax.ShapeDtypeStruct((N_TOK, D), BF16),
                   jax.ShapeDtypeStruct((TOP_K, N_TOK), jnp.int32),
                   jax.ShapeDtypeStruct((TOP_K, N_TOK), F32),
                   jax.ShapeDtypeStruct((N_TILES, N_EXPERTS, LANES), F32),
                   jax.ShapeDtypeStruct((ROWS_ALL, D), BF16)),
        scratch_shapes=[pltpu.VMEM((NJ, TM, LANES), F32),
                        pltpu.VMEM((ZERO_UNIT, D), BF16),
                        pltpu.SemaphoreType.DMA(())],
        compiler_params=_cparams(("parallel",)),
        name="mix",
    )(x, emb_r, emb_c, lg, lb, m1, s1, w_in, w_in, bg, y_t, zr, wglu, bglu, wbs, wbf, bbf, wo, bo,
      g1, l1g, l1b, m2, s2, wr_t, br, tri, etri)


def _on_parity(i, fn):
    @pl.when(i % 2 == 0)
    def _():
        fn(0)

    @pl.when(i % 2 == 1)
    def _():
        fn(1)


def _dispatch_kernel(dprev_ref, dest_ref, pos_ref, u_ref, zeroed_ref, buf_ref, sorted_ref, sems):
    del zeroed_ref
    i = pl.program_id(0)

    @pl.when(i == 0)
    def _():
        sorted_ref[...] = jnp.zeros_like(sorted_ref)

    def chunk_copy(slot, table_ref, j):
        dst = pl.multiple_of(table_ref[0, 0, j], SEG_ALIGN)
        return pltpu.make_async_copy(sorted_ref.at[slot, pl.ds(j * SEG_ALIGN, SEG_ALIGN)],
                                     buf_ref.at[pl.ds(dst, SEG_ALIGN)], sems.at[slot])

    def drain(slot):
        pltpu.make_async_copy(sorted_ref.at[slot], buf_ref.at[pl.ds(0, CAP)], sems.at[slot]).wait()

    def run(slot):
        pos = pos_ref[...]
        u = u_ref[...]
        n_rb = CAP // CAP_BLOCK
        per_rb = NCHK // (n_rb // 2)
        for rb in range(n_rb):
            for j in range(rb * per_rb, min((rb + 1) * per_rb, NCHK)):
                chunk_copy(1 - slot, dprev_ref, j).start()
            rows = lax.broadcasted_iota(jnp.int32, (CAP_BLOCK, TM), 0) + rb * CAP_BLOCK
            hit = rows == pos[0:1]
            for k in range(1, TOP_K):
                hit = hit | (rows == pos[k:k + 1])
            onehot = jnp.where(hit, 1.0, 0.0).astype(BF16)
            sorted_ref[slot, rb * CAP_BLOCK:(rb + 1) * CAP_BLOCK, :] = jnp.dot(
                onehot, u, preferred_element_type=F32).astype(BF16)
        drain(1 - slot)

        @pl.when(i == N_TILES - 1)
        def _():
            def issue(j, c):
                chunk_copy(slot, dest_ref, j).start()
                return c
            lax.fori_loop(0, NCHK, issue, 0)
            drain(slot)
    _on_parity(i, run)


def _dispatch(chunk_table, pos_t, u2, zeroed):
    return pl.pallas_call(
        _dispatch_kernel,
        grid=(N_TILES,),
        in_specs=[pl.BlockSpec((1, 1, NCHK), lambda i: (i, 0, 0), memory_space=pltpu.SMEM),
                  pl.BlockSpec((1, 1, NCHK), lambda i: (i + 1, 0, 0), memory_space=pltpu.SMEM),
                  pl.BlockSpec((TOP_K, TM), lambda i: (0, i)),
                  pl.BlockSpec((TM, D), lambda i: (i, 0)),
                  pl.BlockSpec(memory_space=pl.ANY)],
        out_specs=pl.BlockSpec(memory_space=pl.ANY),
        out_shape=jax.ShapeDtypeStruct((ROWS_ALL, D), BF16),
        scratch_shapes=[pltpu.VMEM((2, CAP, D), BF16),
                        pltpu.SemaphoreType.DMA((2,))],
        input_output_aliases={4: 0},
        compiler_params=_cparams(("arbitrary",)),
        name="dispatch",
    )(chunk_table, chunk_table, pos_t, u2, zeroed)


def _ffn_kernel(be_ref, nu_ref, run_ref, nxt_ref, valid_ref, x_ref, wu_hbm, bu_ref, wd_hbm, bd_ref,
                y_ref, wu_ref, wd_ref, wub_ref, wdb_ref, sems):
    i = pl.program_id(0)
    used = i < nu_ref[0]

    def weight_copies(e, slot):
        return (pltpu.make_async_copy(wu_hbm.at[e], wu_ref.at[slot], sems.at[slot]),
                pltpu.make_async_copy(wd_hbm.at[e], wd_ref.at[slot], sems.at[slot]))

    @pl.when(used)
    def _():
        run = run_ref[i]

        @pl.when(run >= 0)
        def _():
            def open_run(slot):
                @pl.when(run == 0)
                def _():
                    for cp in weight_copies(be_ref[i], slot):
                        cp.start()

                @pl.when(nxt_ref[i] >= 0)
                def _():
                    for cp in weight_copies(nxt_ref[i], 1 - slot):
                        cp.start()
                for cp in weight_copies(be_ref[i], slot):
                    cp.wait()
                wub_ref[...] = wu_ref[slot].astype(BF16)
                wdb_ref[...] = wd_ref[slot].astype(BF16)
            _on_parity(run, open_run)

        def expert_rows(r0, nr):
            rows = slice(r0, r0 + nr)
            e = be_ref[i]
            h = (jnp.dot(x_ref[rows, :], wub_ref[...], preferred_element_type=F32)
                 + bu_ref[pl.ds(e, 1), :])
            h_glu = jnp.minimum(h[:, :D], SWIGLU_LIMIT)
            h_lin = jnp.clip(h[:, D:], -SWIGLU_LIMIT, SWIGLU_LIMIT)
            act = (h_glu * _sigmoid(SWIGLU_ALPHA * h_glu) * (h_lin + 1.0)).astype(BF16)
            y_ref[rows, :] = (jnp.dot(act, wdb_ref[...], preferred_element_type=F32)
                              + bd_ref[pl.ds(e, 1), :]).astype(BF16)

        valid = valid_ref[i]

        @pl.when(valid == BM)
        def _():
            expert_rows(0, BM)

        @pl.when(valid < BM)
        def _():
            for h0 in range(0, BM, FFN_HALF):
                @pl.when(valid >= h0 + FFN_HALF)
                def _(h0=h0):
                    expert_rows(h0, FFN_HALF)

                @pl.when(valid < h0 + FFN_HALF)
                def _(h0=h0):
                    for r0 in range(h0, h0 + FFN_HALF, FFN_TAIL):
                        @pl.when(r0 < valid)
                        def _(r0=r0):
                            expert_rows(r0, FFN_TAIL)

                        @pl.when(r0 >= valid)
                        def _(r0=r0):
                            y_ref[r0:r0 + FFN_TAIL, :] = jnp.zeros((FFN_TAIL, D), BF16)

    @pl.when(jnp.logical_not(used))
    def _():
        y_ref[...] = jnp.zeros_like(y_ref)


def _ffn(block_expert, n_used, run_id, next_expert, valid, buf, w_up, b_up, w_down, b_down):
    def blk(i, be, nu, *_):
        return jnp.minimum(i, nu[0] - 1)
    return pl.pallas_call(
        _ffn_kernel,
        grid_spec=pltpu.PrefetchScalarGridSpec(
            num_scalar_prefetch=5,
            grid=(N_BLOCKS_ALL,),
            in_specs=[pl.BlockSpec((BM, D), lambda i, *s: (blk(i, *s), 0)),
                      pl.BlockSpec(memory_space=pl.ANY),
                      pl.BlockSpec((N_EXPERTS, 2 * D), lambda i, *s: (0, 0)),
                      pl.BlockSpec(memory_space=pl.ANY),
                      pl.BlockSpec((N_EXPERTS, D), lambda i, *s: (0, 0))],
            out_specs=pl.BlockSpec((BM, D), lambda i, *s: (i, 0)),
            scratch_shapes=[pltpu.VMEM((2, D, 2 * D), F32),
                            pltpu.VMEM((2, D, D), F32),
                            pltpu.VMEM((D, 2 * D), BF16),
                            pltpu.VMEM((D, D), BF16),
                            pltpu.SemaphoreType.DMA((2,))]),
        out_shape=jax.ShapeDtypeStruct((ROWS_ALL, D), BF16),
        compiler_params=_cparams(("arbitrary",)),
        name="ffn",
    )(block_expert, n_used, run_id, next_expert, valid, buf, w_up, b_up, w_down, b_down)


def _combine_kernel(dest_ref, dnext_ref, y_ref, h1_ref, pos_ref, gate_ref, g2_ref, lg_ref, lb_ref,
                    o_ref, sorted_ref, sems):
    i = pl.program_id(0)

    def chunk_copy(slot, table_ref, j):
        src = pl.multiple_of(table_ref[0, 0, j], SEG_ALIGN)
        return pltpu.make_async_copy(y_ref.at[pl.ds(src, SEG_ALIGN)],
                                     sorted_ref.at[slot, pl.ds(j * SEG_ALIGN, SEG_ALIGN)],
                                     sems.at[slot])

    def drain(slot):
        pltpu.make_async_copy(y_ref.at[pl.ds(0, CAP)], sorted_ref.at[slot], sems.at[slot]).wait()

    @pl.when(i == 0)
    def _():
        def issue(j, c):
            chunk_copy(0, dest_ref, j).start()
            return c
        lax.fori_loop(0, NCHK, issue, 0)

    def run(slot):
        drain(slot)

        pos = pos_ref[...]
        gate = gate_ref[...]
        m = jnp.zeros((TM, D), F32)
        n_cb = CAP // CAP_BLOCK
        per_cb = NCHK // (n_cb // 2)
        for cb in range(n_cb):
            for j in range(cb * per_cb, min((cb + 1) * per_cb, NCHK)):
                chunk_copy(1 - slot, dnext_ref, j).start()
            rws = lax.broadcasted_iota(jnp.int32, (CAP_BLOCK, TM), 0) + cb * CAP_BLOCK
            g = jnp.where(rws == pos[0:1], gate[0:1], 0.0)
            for k in range(1, TOP_K):
                g = g + jnp.where(rws == pos[k:k + 1], gate[k:k + 1], 0.0)
            rows = sorted_ref[slot, cb * CAP_BLOCK:(cb + 1) * CAP_BLOCK, :]
            m = m + lax.dot_general(g.astype(BF16), rows, (((0,), (0,)), ((), ())),
                                    preferred_element_type=F32)
        o_ref[...] = _layer_norm(ALPHA * h1_ref[...] + g2_ref[...] * m, lg_ref[...], lb_ref[...])

        @pl.when(i == N_TILES - 1)
        def _():
            drain(1 - slot)
    _on_parity(i, run)


def _combine(chunk_table, y_buf, h1, pos_t, gate_t, g2, lg, lb):
    vec = pl.BlockSpec((1, D), lambda i: (0, 0))
    return pl.pallas_call(
        _combine_kernel,
        grid_spec=pltpu.PrefetchScalarGridSpec(
            num_scalar_prefetch=0,
            grid=(N_TILES,),
            in_specs=[pl.BlockSpec((1, 1, NCHK), lambda i: (i + 1, 0, 0), memory_space=pltpu.SMEM),
                      pl.BlockSpec((1, 1, NCHK), lambda i: (i + 2, 0, 0), memory_space=pltpu.SMEM),
                      pl.BlockSpec(memory_space=pl.ANY),
                      pl.BlockSpec((TM, D), lambda i: (i, 0)),
                      pl.BlockSpec((TOP_K, TM), lambda i: (0, i)),
                      pl.BlockSpec((TOP_K, TM), lambda i: (0, i)),
                      vec, vec, vec],
            out_specs=pl.BlockSpec((TM, D), lambda i: (i, 0)),
            scratch_shapes=[pltpu.VMEM((2, CAP, D), BF16),
                            pltpu.SemaphoreType.DMA((2,))]),
        out_shape=jax.ShapeDtypeStruct((N_TOK, D), F32),
        compiler_params=_cparams(("arbitrary",)),
        name="combine",
    )(chunk_table, chunk_table, y_buf, h1, pos_t, gate_t, g2, lg, lb)


def _sincos_tables():
    q = D // 4
    omega = 1.0 / (10000.0 ** (np.arange(q) / q))

    def emb(n):
        ang = np.arange(n)[:, None] * omega[None, :]
        return jnp.asarray(np.concatenate([np.sin(ang), np.cos(ang)], axis=-1), F32)
    return emb(N_TOK // GRID_W), emb(GRID_W)


def kernel(x, c, ctx, c_ctx, ln_in_g, ln_in_b, w_ada, b_ada, w_in, b_in, s5_lambda_re, s5_lambda_im, s5_log_dt, s5_b_re, s5_b_im, s5_c_re, s5_c_im, s5_d, w_glu, b_glu, w_br_s5, w_br_fft, b_br_fft, w_out, b_out, ln1_g, ln1_b, w_router, b_router, w_up, b_up, w_down, b_down, ln2_g, ln2_b):
    assert x.shape == (1, N_TOK, D) and ctx.shape == (1, N_CTX, D) and w_ada.shape[0] == 1
    row = lambda v: v.reshape(1, -1).astype(F32)

    cc = jnp.concatenate([c.reshape(1, D), c_ctx.reshape(1, D), jnp.zeros((SUBLANES - 2, D), F32)], axis=0)
    ada = _ada(cc, w_ada[0], row(b_ada[0]))
    sh1, sc1, g1, sh2, sc2, g2 = (ada[0:1, k * D:(k + 1) * D] for k in range(6))
    sh1c, sc1c = ada[1:2, 0:D], ada[1:2, D:2 * D]

    emb_r, emb_c = _sincos_tables()
    st1, st2, fc = _dft_tables()
    lg, lb = row(ln_in_g), row(ln_in_b)

    w_in_bf = w_in[0].astype(BF16)
    b_s5 = row(b_in[0][:S5_WIDTH])
    b_fft8 = jnp.concatenate([row(b_in[0][S5_WIDTH:S5_WIDTH + FFT_WIDTH]),
                              jnp.zeros((SUBLANES - 1, FFT_WIDTH), F32)], axis=0)
    b_g = row(b_in[0][S5_WIDTH + FFT_WIDTH:])
    w_fc, b_fc = _fft_weights(w_in[0], b_fft8, fc)
    bcat = jnp.concatenate([b_s5, b_fc[0:1]], axis=1)

    x2 = x[0]
    p_t, xr, xi = _proj(x2, emb_r, emb_c, lg, lb, 1.0 + sc1, sh1, w_in_bf, w_fc, bcat)
    pc_t = _ctx_proj(ctx[0], lg, lb, 1.0 + sc1c, sh1c, w_in_bf[:, :S5_WIDTH], b_s5)

    b_c, a_q, a_p, trans, ctx_w = _s5_tables(
        s5_lambda_re[0], s5_lambda_im[0], s5_log_dt[0], s5_b_re[0], s5_b_im[0],
        s5_c_re[0], s5_c_im[0], s5_d[0])
    y_t = _s5(p_t, pc_t, b_c, a_q, a_p, trans, ctx_w)

    yr, yi = _fft1(xr, xi, st1)
    zr = _fft2(yr, yi, st2)

    tri = jnp.asarray(np.arange(TM)[:, None] < np.arange(TM)[None, :], BF16)
    br = jnp.broadcast_to(b_router[0].reshape(N_EXPERTS, 1), (N_EXPERTS, LANES))
    etri = jnp.asarray(np.arange(N_EXPERTS)[:, None] > np.arange(N_EXPERTS)[None, :], BF16)
    wr_t = jnp.transpose(w_router[0])
    wr_hi = wr_t.astype(BF16)
    wr_split = jnp.stack([wr_hi, (wr_t - wr_hi.astype(F32)).astype(BF16)], axis=0)
    h1, u2, pos_t, gate_t, counts, zeroed = _mix(
        x2, emb_r, emb_c, lg, lb, 1.0 + sc1, sh1, w_in_bf, b_g, y_t, zr,
        w_glu[0].astype(BF16), row(b_glu[0]), w_br_s5[0].astype(BF16), w_br_fft[0].astype(BF16),
        row(b_br_fft[0]), w_out[0].astype(BF16), row(b_out[0]), g1, row(ln1_g[0]), row(ln1_b[0]),
        1.0 + sc2, sh2, wr_split, br, tri, etri)

    cnt = counts[:, :, 0].astype(jnp.int32)
    seg = (cnt + SEG_ALIGN - 1) // SEG_ALIGN * SEG_ALIGN
    seg_end = jnp.cumsum(seg, axis=1)
    seg_start = seg_end - seg
    padded = (jnp.sum(seg, axis=0) + BM - 1) // BM * BM
    pad_ends = jnp.cumsum(padded)
    seg_dest = (pad_ends - padded)[None, :] + jnp.cumsum(seg, axis=0) - seg
    chunk_row = jnp.arange(NCHK, dtype=jnp.int32) * SEG_ALIGN
    chunk_exp = jnp.minimum(jnp.sum(chunk_row[None, :, None] >= seg_end[:, None, :], axis=-1),
                            N_EXPERTS - 1)
    own = chunk_exp[:, :, None] == jnp.arange(N_EXPERTS, dtype=jnp.int32)[None, None, :]
    chunk_dest = (jnp.sum(jnp.where(own, (seg_dest - seg_start)[:, None, :], 0), axis=-1)
                  + chunk_row[None, :]).astype(jnp.int32).reshape(N_TILES, 1, NCHK)
    nchk = (seg_end[:, -1] // SEG_ALIGN).astype(jnp.int32)
    block_start = jnp.arange(N_BLOCKS_ALL, dtype=jnp.int32) * BM
    block_expert = jnp.minimum(jnp.sum(block_start[:, None] >= pad_ends[None, :], axis=1),
                               N_EXPERTS - 1).astype(jnp.int32)
    n_used = (pad_ends[-1:] // BM).astype(jnp.int32)
    opens = (block_start < pad_ends[-1]) & (
        block_expert != jnp.concatenate([jnp.full((1,), -1, jnp.int32), block_expert[:-1]]))
    run_id = jnp.where(opens, jnp.cumsum(opens.astype(jnp.int32)) - 1, -1).astype(jnp.int32)
    experts = jnp.arange(N_EXPERTS, dtype=jnp.int32)
    later = (experts[None, :] > block_expert[:, None]) & (padded[None, :] > 0)
    next_expert = jnp.min(jnp.where(later, experts[None, :], N_EXPERTS), axis=1)
    next_expert = jnp.where(next_expert < N_EXPERTS, next_expert, -1).astype(jnp.int32)

    spare = (ROWS + chunk_row)[None, None, :]
    chunk_table = jnp.concatenate(
        [spare,
         jnp.where(chunk_row[None, None, :] < (nchk * SEG_ALIGN)[:, None, None], chunk_dest, spare),
         spare], axis=0).astype(jnp.int32)
    fill_ends = (pad_ends - padded + jnp.sum(seg, axis=0)).astype(jnp.int32)
    buf = _dispatch(chunk_table, pos_t, u2, zeroed)
    mine = block_expert[:, None] == experts[None, :]
    filled = jnp.sum(jnp.where(mine, fill_ends[None, :], 0), axis=1)
    valid = jnp.clip(filled - block_start, 0, BM).astype(jnp.int32)
    y_buf = _ffn(block_expert, n_used, run_id, next_expert, valid, buf, w_up[0],
                 b_up[0], w_down[0], b_down[0])
    out = _combine(chunk_table, y_buf, h1, pos_t, gate_t, g2, row(ln2_g[0]), row(ln2_b[0]))
    return out.reshape(1, N_TOK, D)
```

```python
import functools
import math

import jax
import jax.numpy as jnp
import numpy as np
from jax import lax
from jax.experimental import pallas as pl
from jax.experimental.pallas import tpu as pltpu

F32 = jnp.float32
BF16 = jnp.bfloat16
HI = lax.Precision.HIGHEST

D = 1024
N_TOK = 16384
N_CTX = 256
GRID_W = 64
S5_GROUP = 16
S5_GROUPS = 32
S5_STATE = 64
S5_WIDTH = 512
FFT_GROUPS = 4
FFT_DIM = 128
FFT_WIDTH = 512
N_EXPERTS = 32
TOP_K = 4
LN_EPS = 1e-5
ALPHA = 2.0 ** 0.25
SWIGLU_ALPHA = 1.702
SWIGLU_LIMIT = 7.0

LANES = 128
SUBLANES = 8
VMEM_LIMIT = 56 * 1024 * 1024

CH = 8
N_CHUNK = N_TOK // CH
N_CHUNK_CTX = N_CTX // CH
NSEG = SUBLANES
SEG = N_CHUNK // NSEG
SCAN_UNROLL = 4
GPT = LANES // S5_GROUP
NJ = S5_WIDTH // LANES
CL = CH * LANES
SW = 4 * GPT * S5_STATE

FN = 128
FB = 16

TM = 512
TM_PROJ = 1024
N_TILES = N_TOK // TM
BM = 1024
FFN_HALF = 512
FFN_TAIL = 128
N_SLOTS = N_TOK * TOP_K
SEG_ALIGN = 2 * SUBLANES
CAP_BLOCK = 256
CAP = -(-(TOP_K * TM + N_EXPERTS * (SEG_ALIGN - 1)) // CAP_BLOCK) * CAP_BLOCK
NCHK = CAP // SEG_ALIGN
N_BLOCKS = -(-(N_SLOTS + N_TILES * N_EXPERTS * (SEG_ALIGN - 1)) // BM) + N_EXPERTS
ROWS = N_BLOCKS * BM
N_BLOCKS_ALL = N_BLOCKS + -(-CAP // BM)
ROWS_ALL = N_BLOCKS_ALL * BM
ZERO_UNIT = 512
ZERO_PARTS = ROWS_ALL // (NJ * ZERO_UNIT)
assert ZERO_UNIT * ZERO_PARTS * NJ == ROWS_ALL


def _cparams(sem):
    return pltpu.CompilerParams(dimension_semantics=sem, vmem_limit_bytes=VMEM_LIMIT)


def _layer_norm(x, g, b):
    mu = jnp.mean(x, axis=-1, keepdims=True)
    xc = x - mu
    var = jnp.mean(xc * xc, axis=-1, keepdims=True)
    return xc * lax.rsqrt(var + LN_EPS) * g + b


def _sigmoid(x):
    return 0.5 * jnp.tanh(0.5 * x) + 0.5


def _ada_kernel(c_ref, w_ref, b_ref, o_ref):
    c = c_ref[...]
    s = c * _sigmoid(c)
    w = w_ref[...]
    s_hi, w_hi = s.astype(BF16), w.astype(BF16)
    s_lo = (s - s_hi.astype(F32)).astype(BF16)
    w_lo = (w - w_hi.astype(F32)).astype(BF16)

    def mm(a, b):
        return jnp.dot(a, b, preferred_element_type=F32)
    o_ref[...] = mm(s_hi, w_hi) + mm(s_lo, w_hi) + mm(s_hi, w_lo) + b_ref[...]


def _ada(cc, w_ada, b_ada):
    nb = 4
    wb = 6 * D // nb
    return pl.pallas_call(
        _ada_kernel,
        grid=(nb,),
        in_specs=[pl.BlockSpec((SUBLANES, D), lambda i: (0, 0)),
                  pl.BlockSpec((D, wb), lambda i: (0, i)),
                  pl.BlockSpec((1, wb), lambda i: (0, i))],
        out_specs=pl.BlockSpec((SUBLANES, wb), lambda i: (0, i)),
        out_shape=jax.ShapeDtypeStruct((SUBLANES, 6 * D), F32),
        compiler_params=_cparams(("parallel",)),
        name="ada",
    )(cc, w_ada, b_ada)


def _fftw_kernel(w_ref, b_ref, f_ref, wo_ref, bo_ref):
    f = f_ref[...]
    w = w_ref[...]
    w_hi, f_hi = w.astype(BF16), f.astype(BF16)
    w_lo = (w - w_hi.astype(F32)).astype(BF16)
    f_lo = (f - f_hi.astype(F32)).astype(BF16)

    def mm(a, b):
        return jnp.dot(a, b, preferred_element_type=F32)
    wo_ref[...] = (mm(w_hi, f_hi) + mm(w_lo, f_hi) + mm(w_hi, f_lo)).astype(BF16)
    bo_ref[...] = jnp.dot(b_ref[...], f, preferred_element_type=F32, precision=HI)


def _fft_weights(w_in, b_fft8, fc):
    def full(a):
        return pl.BlockSpec(a.shape, lambda i: (0,) * a.ndim)
    outs = (jax.ShapeDtypeStruct((D, 2 * FFT_WIDTH), BF16),
            jax.ShapeDtypeStruct((SUBLANES, 2 * FFT_WIDTH), F32))
    return pl.pallas_call(
        _fftw_kernel,
        grid=(1,),
        in_specs=[pl.BlockSpec((D, FFT_WIDTH), lambda i: (0, S5_WIDTH // FFT_WIDTH)),
                  full(b_fft8), full(fc)],
        out_specs=tuple(pl.BlockSpec(o.shape, lambda i: (0, 0)) for o in outs),
        out_shape=outs,
        compiler_params=_cparams(("arbitrary",)),
        name="fftw",
    )(w_in, b_fft8, fc)


def _pos_code(er_ref, ec_ref, tm):
    nr = tm // GRID_W
    er = er_ref[...]
    row = jnp.broadcast_to(er[:, None, :], (nr, GRID_W, D // 2)).reshape(tm, D // 2)
    col = jnp.concatenate([ec_ref[...]] * nr, axis=0)
    return jnp.concatenate([row, col], axis=-1)


def _to_chunk_major(val, scr_ref, out_ref, tm):
    for j in range(NJ):
        scr_ref[j] = val[:, j * LANES:(j + 1) * LANES]
    for t in range(CH):
        for j in range(NJ):
            piece = scr_ref[j, pl.ds(t, tm // CH, stride=CH), :]
            out_ref[t, :, j * LANES:(j + 1) * LANES] = piece.astype(out_ref.dtype)


def _proj_kernel(x_ref, er_ref, ec_ref, lg_ref, lb_ref, m_ref, s_ref, ws_ref, wf_ref, b_ref,
                 p_ref, xr_ref, xi_ref, scr_ref):
    x = x_ref[...] + _pos_code(er_ref, ec_ref, TM_PROJ)
    h = _layer_norm(x, lg_ref[...], lb_ref[...])
    u = (h * m_ref[...] + s_ref[...]).astype(BF16)
    p_s5 = jnp.dot(u, ws_ref[...], preferred_element_type=F32) + b_ref[:, :S5_WIDTH]
    _to_chunk_major(p_s5, scr_ref, p_ref, TM_PROJ)
    p_f = jnp.dot(u, wf_ref[...], preferred_element_type=F32) + b_ref[:, S5_WIDTH:]
    xr_ref[...] = p_f[:, :FFT_WIDTH].astype(BF16)
    xi_ref[...] = p_f[:, FFT_WIDTH:].astype(BF16)


def _proj(x, emb_r, emb_c, lg, lb, m1, s1, w_in, w_fc, bcat):
    nw = bcat.shape[1]
    vec = pl.BlockSpec((1, D), lambda i: (0, 0))
    return pl.pallas_call(
        _proj_kernel,
        grid=(N_TOK // TM_PROJ,),
        in_specs=[pl.BlockSpec((TM_PROJ, D), lambda i: (i, 0)),
                  pl.BlockSpec((TM_PROJ // GRID_W, D // 2), lambda i: (i, 0)),
                  pl.BlockSpec((GRID_W, D // 2), lambda i: (0, 0)),
                  vec, vec, vec, vec,
                  pl.BlockSpec((D, S5_WIDTH), lambda i: (0, 0)),
                  pl.BlockSpec((D, 2 * FFT_WIDTH), lambda i: (0, 0)),
                  pl.BlockSpec((1, nw), lambda i: (0, 0))],
        out_specs=(pl.BlockSpec((CH, TM_PROJ // CH, S5_WIDTH), lambda i: (0, i, 0)),
                   pl.BlockSpec((TM_PROJ, FFT_WIDTH), lambda i: (i, 0)),
                   pl.BlockSpec((TM_PROJ, FFT_WIDTH), lambda i: (i, 0))),
        out_shape=(jax.ShapeDtypeStruct((CH, N_CHUNK, S5_WIDTH), BF16),
                   jax.ShapeDtypeStruct((N_TOK, FFT_WIDTH), BF16),
                   jax.ShapeDtypeStruct((N_TOK, FFT_WIDTH), BF16)),
        scratch_shapes=[pltpu.VMEM((NJ, TM_PROJ, LANES), F32)],
        compiler_params=_cparams(("parallel",)),
        name="proj",
    )(x, emb_r, emb_c, lg, lb, m1, s1, w_in, w_fc, bcat)


def _ctx_proj_kernel(x_ref, lg_ref, lb_ref, m_ref, s_ref, w_ref, b_ref, p_ref, scr_ref):
    h = _layer_norm(x_ref[...], lg_ref[...], lb_ref[...])
    u = (h * m_ref[...] + s_ref[...]).astype(BF16)
    p = jnp.dot(u, w_ref[...], preferred_element_type=F32) + b_ref[...]
    _to_chunk_major(p, scr_ref, p_ref, N_CTX)


def _ctx_proj(ctx, lg, lb, m1, s1, w_s5, b_s5):
    return pl.pallas_call(
        _ctx_proj_kernel,
        out_shape=jax.ShapeDtypeStruct((CH, N_CHUNK_CTX, S5_WIDTH), BF16),
        scratch_shapes=[pltpu.VMEM((NJ, N_CTX, LANES), F32)],
        compiler_params=pltpu.CompilerParams(vmem_limit_bytes=VMEM_LIMIT),
        name="ctxproj",
    )(ctx, lg, lb, m1, s1, w_s5, b_s5)


def _s5_tables(lam_re, lam_im, log_dt, b_re, b_im, c_re, c_im, d_skip):
    dt = jnp.exp(log_dt)[..., None]
    zr = lam_re * dt
    zi = lam_im * dt

    def apow(m):
        m = jnp.asarray(m, F32)
        mag = jnp.exp(zr[..., None] * m)
        return mag * jnp.cos(zi[..., None] * m), mag * jnp.sin(zi[..., None] * m)

    a_re, a_im = apow(jnp.ones((1,), F32))
    a_re, a_im = a_re[..., 0], a_im[..., 0]
    den = lam_re * lam_re + lam_im * lam_im
    num_re = a_re - 1.0
    k_re = (num_re * lam_re + a_im * lam_im) / den
    k_im = (a_im * lam_re - num_re * lam_im) / den
    bb_re = k_re[..., None] * b_re - k_im[..., None] * b_im
    bb_im = k_re[..., None] * b_im + k_im[..., None] * b_re

    ks = jnp.arange(CH + 1, dtype=F32)
    pw_re, pw_im = apow(ks)
    kmag = jnp.exp(zr[:, :, None, :] * ks[None, None, :, None])
    pk_re = kmag * jnp.cos(zi[:, :, None, :] * ks[None, None, :, None])
    pk_im = kmag * jnp.sin(zi[:, :, None, :] * ks[None, None, :, None])
    bt_re, bt_im = jnp.swapaxes(b_re, 2, 3), jnp.swapaxes(b_im, 2, 3)
    bbt_re = k_re[:, :, None, :] * bt_re - k_im[:, :, None, :] * bt_im
    bbt_im = k_re[:, :, None, :] * bt_im + k_im[:, :, None, :] * bt_re

    ar, ai = pw_re[:, :, :, :CH, None], pw_im[:, :, :, :CH, None]
    cr = jnp.swapaxes(c_re, 2, 3)[:, :, :, None, :]
    ci = jnp.swapaxes(c_im, 2, 3)[:, :, :, None, :]
    ca = jnp.concatenate([cr * ar - ci * ai, -(cr * ai + ci * ar)], axis=2)
    ca = ca.reshape(2, S5_GROUPS, 2 * S5_STATE, CH * S5_GROUP)
    bbt = jnp.concatenate([bbt_re, bbt_im], axis=-1)
    taps = jnp.einsum('dghq,dgqn->dghn', bbt, ca, precision=HI)
    skip = (d_skip.reshape(S5_GROUPS, S5_GROUP, 1) * jnp.eye(S5_GROUP, dtype=F32)[None])
    taps = taps.at[0, :, :, :S5_GROUP].add(skip)
    b_c = jnp.transpose(taps.reshape(2, NJ, GPT, S5_GROUP, CH, S5_GROUP), (1, 4, 0, 2, 3, 5))
    b_c = b_c.reshape(NJ, 2 * CH, LANES, S5_GROUP)

    ef = (CH - 1) - jnp.arange(CH)
    eb = jnp.arange(CH)

    def q_part(d, e):
        pr = pk_re[d][:, e, None, :]
        pi = pk_im[d][:, e, None, :]
        br = bbt_re[d][:, None, :, :]
        bi = bbt_im[d][:, None, :, :]
        return pr * br - pi * bi, pr * bi + pi * br

    def q_rows(v):
        v = v.reshape(NJ, GPT, CH, S5_GROUP, S5_STATE)
        return jnp.transpose(v, (0, 2, 1, 3, 4)).reshape(NJ, CL, S5_STATE)
    a_q = jnp.stack([q_rows(v) for v in q_part(0, ef) + q_part(1, eb)], axis=0)

    of = jnp.arange(CH) + 1
    ob = CH - jnp.arange(CH)

    def p_part(d, e):
        pr = pw_re[d][..., e][:, :, :, None]
        pi = pw_im[d][..., e][:, :, :, None]
        return (ct_re[d] * pr - ct_im[d] * pi, -(ct_re[d] * pi + ct_im[d] * pr))
    ct_re = jnp.swapaxes(c_re, 2, 3)[:, :, :, None, :]
    ct_im = jnp.swapaxes(c_im, 2, 3)[:, :, :, None, :]
    a_p = jnp.stack([v.reshape(NJ, GPT * S5_STATE, CH * S5_GROUP)
                     for v in p_part(0, of) + p_part(1, ob)], axis=0)

    def lanes(v):
        return jnp.transpose(v.reshape(2, NJ, GPT * S5_STATE), (1, 0, 2))
    c_r, c_i = apow(jnp.full((1,), float(CH), F32))
    s_r, s_i = apow(jnp.full((1,), float(CH * SEG), F32))
    cr, ci, sr, si = (lanes(v[..., 0]) for v in (c_r, c_i, s_r, s_i))
    trans = jnp.stack([cr[:, 0], ci[:, 0], cr[:, 1], ci[:, 1],
                       sr[:, 0], si[:, 0], sr[:, 1], si[:, 1]], axis=1)

    cidx = jnp.arange(N_CHUNK_CTX, dtype=F32)
    wf_r, wf_i = apow(CH * (N_CHUNK_CTX - 1 - cidx))
    wb_r, wb_i = apow(CH * cidx)

    def ctx_lanes(v, d):
        return jnp.transpose(v[d].reshape(NJ, GPT * S5_STATE, N_CHUNK_CTX), (0, 2, 1))
    ctx_w = jnp.stack([ctx_lanes(wf_r, 0), ctx_lanes(wf_i, 0),
                       ctx_lanes(wb_r, 1), ctx_lanes(wb_i, 1)], axis=1)
    return b_c, a_q, a_p, trans, ctx_w


def _build_operators(bc_ref, aq_ref, ap_ref, c16_ref, c64_ref, cm_ref, wm_ref, wq_ref, wp_ref):
    def expand(a, c, row_shift, col_shift):
        w = jnp.dot(a, c, preferred_element_type=F32)
        rg = (lax.broadcasted_iota(jnp.int32, (w.shape[0], 1), 0) >> row_shift) & (GPT - 1)
        cg = (lax.broadcasted_iota(jnp.int32, (1, w.shape[1]), 1) >> col_shift) & (GPT - 1)
        return jnp.where(rg == cg, w, 0.0)

    blk = [expand(bc_ref[0, kd], c16_ref[...], 4, 4) for kd in range(2 * CH)]
    for t in range(CH):
        for u in range(CH):
            b = blk[2 * (u - t)] if u > t else blk[2 * (t - u) + 1] if u < t else blk[0] + blk[1]
            wm_ref[0, t * LANES:(t + 1) * LANES, u * LANES:(u + 1) * LANES] = b.astype(BF16)
    half = GPT * S5_STATE
    for s in range(4):
        wq_ref[0, :, s * half:(s + 1) * half] = expand(aq_ref[s, 0], c64_ref[...], 4, 6).astype(BF16)
        wp_ref[0, s * half:(s + 1) * half, :] = expand(ap_ref[s, 0], cm_ref[...], 6, 4).astype(BF16)


def _s5_kernel(p_ref, pc_ref, bc_ref, aq_ref, ap_ref, c16_ref, c64_ref, cm_ref, tr_ref, cw_ref,
               y_ref, rows_ref, v_ref, wm_ref, wq_ref, wp_ref, zero_ref, zsem):
    nq = NJ
    half = GPT * S5_STATE

    zero_ref[...] = jnp.zeros_like(zero_ref)

    def clears(fn):
        def unit(b, c):
            start = pl.multiple_of((pl.program_id(0) * ZERO_PARTS + b) * ZERO_UNIT, ZERO_UNIT)
            fn(pltpu.make_async_copy(zero_ref, rows_ref.at[pl.ds(start, ZERO_UNIT)], zsem))
            return c
        lax.fori_loop(0, ZERO_PARTS, unit, 0)
    clears(lambda cp: cp.start())
    _build_operators(bc_ref, aq_ref, ap_ref, c16_ref, c64_ref, cm_ref, wm_ref, wq_ref, wp_ref)

    def chunk_rows(ref, r0, nrows):
        return jnp.concatenate([ref[t, pl.ds(r0, nrows), :] for t in range(CH)], axis=-1)

    def fill(k, c):
        r0 = pl.multiple_of(k * SEG, SEG)
        v = jnp.dot(chunk_rows(p_ref, r0, SEG), wq_ref[0], preferred_element_type=F32)
        for s in range(4 * nq):
            v_ref[s, pl.ds(k, SEG, stride=NSEG), :] = v[:, s * LANES:(s + 1) * LANES]
        return c
    lax.fori_loop(0, NSEG, fill, 0)

    vc = jnp.dot(chunk_rows(pc_ref, 0, N_CHUNK_CTX), wq_ref[0], preferred_element_type=F32)
    vfr, vfi, vbr, vbi = (vc[:, i * half:(i + 1) * half] for i in range(4))
    wfr, wfi, wbr, wbi = (cw_ref[0, i] for i in range(4))
    s0_fr = jnp.sum(wfr * vfr - wfi * vfi, axis=0, keepdims=True)
    s0_fi = jnp.sum(wfr * vfi + wfi * vfr, axis=0, keepdims=True)
    s0_br = jnp.sum(wbr * vbr - wbi * vbi, axis=0, keepdims=True)
    s0_bi = jnp.sum(wbr * vbi + wbi * vbr, axis=0, keepdims=True)

    tr = tr_ref[0]
    afr, afi, abr, abi = (jnp.broadcast_to(tr[i:i + 1], (NSEG, half)) for i in range(4))
    gfr, gfi, gbr, gbi = (tr[i:i + 1] for i in range(4, 8))

    def load_part(part, i):
        return jnp.concatenate(
            [v_ref[part * nq + q, pl.ds(pl.multiple_of(i * NSEG, NSEG), NSEG), :] for q in range(nq)],
            axis=-1)

    def store_part(part, i, val):
        for q in range(nq):
            v_ref[part * nq + q, pl.ds(pl.multiple_of(i * NSEG, NSEG), NSEG), :] = (
                val[:, q * LANES:(q + 1) * LANES])

    def step(i, carry, write):
        fr, fi, br, bi = carry
        ib = SEG - 1 - i
        ufr, ufi = load_part(0, i), load_part(1, i)
        ubr, ubi = load_part(2, ib), load_part(3, ib)
        if write:
            store_part(0, i, fr)
            store_part(1, i, fi)
            store_part(2, ib, br)
            store_part(3, ib, bi)
        return (afr * fr - afi * fi + ufr, afr * fi + afi * fr + ufi,
                abr * br - abi * bi + ubr, abr * bi + abi * br + ubi)

    zero = jnp.zeros((NSEG, half), F32)
    ffr, ffi, fbr, fbi = lax.fori_loop(0, SEG, functools.partial(step, write=False),
                                       (zero, zero, zero, zero), unroll=SCAN_UNROLL)

    rows_fr, rows_fi = [s0_fr], [s0_fi]
    for k in range(1, NSEG):
        pr, pi = rows_fr[-1], rows_fi[-1]
        rows_fr.append(gfr * pr - gfi * pi + ffr[k - 1:k])
        rows_fi.append(gfr * pi + gfi * pr + ffi[k - 1:k])
    rows_br, rows_bi = [s0_br], [s0_bi]
    for k in range(NSEG - 2, -1, -1):
        pr, pi = rows_br[0], rows_bi[0]
        rows_br.insert(0, gbr * pr - gbi * pi + fbr[k + 1:k + 2])
        rows_bi.insert(0, gbr * pi + gbi * pr + fbi[k + 1:k + 2])
    init = tuple(jnp.concatenate(r, axis=0) for r in (rows_fr, rows_fi, rows_br, rows_bi))

    lax.fori_loop(0, SEG, functools.partial(step, write=True), init, unroll=SCAN_UNROLL)

    def emit(k, c):
        r0 = pl.multiple_of(k * SEG, SEG)
        b = chunk_rows(p_ref, r0, SEG)
        sin = jnp.concatenate([v_ref[s, pl.ds(k, SEG, stride=NSEG), :] for s in range(4 * nq)], axis=-1)
        y = (jnp.dot(b, wm_ref[0], preferred_element_type=F32)
             + jnp.dot(sin.astype(BF16), wp_ref[0], preferred_element_type=F32))
        for t in range(CH):
            y_ref[t, pl.ds(r0, SEG), :] = y[:, t * LANES:(t + 1) * LANES].astype(y_ref.dtype)
        return c
    lax.fori_loop(0, NSEG, emit, 0)
    clears(lambda cp: cp.wait())


def _s5(p_t, pc_t, b_c, a_q, a_p, trans, ctx_w):
    rep = np.ones((1, GPT))
    c16 = jnp.asarray(np.kron(rep, np.eye(S5_GROUP)), F32).astype(BF16)
    c64 = jnp.asarray(np.kron(rep, np.eye(S5_STATE)), F32).astype(BF16)
    c_m = jnp.asarray(np.kron(np.eye(CH), np.kron(rep, np.eye(S5_GROUP))), F32).astype(BF16)
    b_c, a_q, a_p = b_c.astype(BF16), a_q.astype(BF16), a_p.astype(BF16)
    half = GPT * S5_STATE
    return pl.pallas_call(
        _s5_kernel,
        grid=(NJ,),
        in_specs=[pl.BlockSpec((CH, N_CHUNK, LANES), lambda j: (0, 0, j)),
                  pl.BlockSpec((CH, N_CHUNK_CTX, LANES), lambda j: (0, 0, j)),
                  pl.BlockSpec((1, 2 * CH, LANES, S5_GROUP), lambda j: (j, 0, 0, 0)),
                  pl.BlockSpec((4, 1, CL, S5_STATE), lambda j: (0, j, 0, 0)),
                  pl.BlockSpec((4, 1, half, CH * S5_GROUP), lambda j: (0, j, 0, 0)),
                  pl.BlockSpec(c16.shape, lambda j: (0, 0)),
                  pl.BlockSpec(c64.shape, lambda j: (0, 0)),
                  pl.BlockSpec(c_m.shape, lambda j: (0, 0)),
                  pl.BlockSpec((1, SUBLANES, half), lambda j: (j, 0, 0)),
                  pl.BlockSpec((1, 4, N_CHUNK_CTX, half), lambda j: (j, 0, 0, 0))],
        out_specs=(pl.BlockSpec((CH, N_CHUNK, LANES), lambda j: (0, 0, j)),
                   pl.BlockSpec(memory_space=pl.ANY)),
        out_shape=(jax.ShapeDtypeStruct((CH, N_CHUNK, S5_WIDTH), BF16),
                   jax.ShapeDtypeStruct((ROWS_ALL, D), BF16)),
        scratch_shapes=[pltpu.VMEM((4 * NJ, N_CHUNK, LANES), F32),
                        pltpu.VMEM((1, CL, CL), BF16),
                        pltpu.VMEM((1, CL, SW), BF16),
                        pltpu.VMEM((1, SW, CL), BF16),
                        pltpu.VMEM((ZERO_UNIT, D), BF16),
                        pltpu.SemaphoreType.DMA(())],
        compiler_params=_cparams(("parallel",)),
        name="s5",
    )(p_t, pc_t, b_c, a_q, a_p, c16, c64, c_m, trans, ctx_w)


def _dft_tables():
    n = np.arange(FN)
    ang = 2.0 * np.pi * np.outer(n, n) / FN
    c, s = np.cos(ang), np.sin(ang)
    st1 = np.block([[c, s], [-s, c]])
    tw = 2.0 * np.pi * np.outer(n, n) / (FN * FN)
    wr, wi = np.cos(tw), -np.sin(tw)
    fr = c[None] * wr[:, None, :] + s[None] * wi[:, None, :]
    fi = c[None] * wi[:, None, :] - s[None] * wr[:, None, :]
    st2 = np.concatenate([fr, -fi], axis=-1)
    scale = 1.0 / math.sqrt(N_TOK * FFT_DIM)
    blk_c = np.kron(np.eye(FFT_GROUPS), c) * scale
    blk_s = np.kron(np.eye(FFT_GROUPS), s) * scale
    fc = np.concatenate([blk_c, -blk_s], axis=1)
    return (jnp.asarray(st1, F32).astype(BF16), jnp.asarray(st2, F32).astype(BF16), jnp.asarray(fc, F32))


FSL = FFT_WIDTH // LANES


FBH = FB // SUBLANES


def _block_to_slabs(blk, slab_ref, first, per_half):
    for bh in range(FBH):
        val = blk[:, bh * SUBLANES:(bh + 1) * SUBLANES, :].reshape(FN * SUBLANES, FFT_WIDTH)
        for s in range(FSL):
            slab_ref[bh * per_half + first + s] = val[:, s * LANES:(s + 1) * LANES]


def _slab_rows(b, first, per_half):
    return (b // SUBLANES) * per_half + first, pl.ds(b % SUBLANES, FN, stride=SUBLANES)


def _slabs_to_block(slab_ref, first, per_half):
    halves = []
    for bh in range(FBH):
        val = jnp.concatenate([slab_ref[bh * per_half + first + s] for s in range(FSL)], axis=-1)
        halves.append(val.reshape(FN, SUBLANES, FFT_WIDTH))
    return jnp.concatenate(halves, axis=1)


def _fft1_kernel(xr_ref, xi_ref, f_ref, yr_ref, yi_ref, in_ref, out_ref):
    _block_to_slabs(xr_ref[...].astype(F32), in_ref, 0, 2 * FSL)
    _block_to_slabs(xi_ref[...].astype(F32), in_ref, FSL, 2 * FSL)
    for b in range(FB):
        def part(first):
            base, rows = _slab_rows(b, first, 2 * FSL)
            return jnp.concatenate([in_ref[base + s, rows, :] for s in range(FSL)], axis=-1)
        xs = jnp.concatenate([part(0), part(FSL)], axis=0).astype(BF16)
        y = jnp.dot(f_ref[...], xs, preferred_element_type=F32)
        base, rows = _slab_rows(b, 0, 2 * FSL)
        for s in range(FSL):
            out_ref[base + s, rows, :] = y[:FN, s * LANES:(s + 1) * LANES]
            out_ref[base + FSL + s, rows, :] = y[FN:, s * LANES:(s + 1) * LANES]
    yr_ref[...] = _slabs_to_block(out_ref, 0, 2 * FSL).astype(BF16)
    yi_ref[...] = _slabs_to_block(out_ref, FSL, 2 * FSL).astype(BF16)


def _fft1(xr, xi, st1):
    spec = pl.BlockSpec((FN, FB, FFT_WIDTH), lambda i: (0, i, 0))
    slabs = pltpu.VMEM((FBH * 2 * FSL, FN * SUBLANES, LANES), F32)
    return pl.pallas_call(
        _fft1_kernel,
        grid=(FN // FB,),
        in_specs=[spec, spec, pl.BlockSpec((2 * FN, 2 * FN), lambda i: (0, 0))],
        out_specs=(spec, spec),
        out_shape=(jax.ShapeDtypeStruct((FN, FN, FFT_WIDTH), BF16),) * 2,
        scratch_shapes=[slabs, slabs],
        compiler_params=_cparams(("parallel",)),
        name="fft1",
    )(xr.reshape(FN, FN, FFT_WIDTH), xi.reshape(FN, FN, FFT_WIDTH), st1)


def _fft2_kernel(yr_ref, yi_ref, f_ref, z_ref, out_ref):
    for b in range(FB):
        ys = jnp.concatenate([yr_ref[b * FN:(b + 1) * FN, :], yi_ref[b * FN:(b + 1) * FN, :]], axis=0)
        z = jnp.dot(f_ref[b], ys, preferred_element_type=F32)
        base, rows = _slab_rows(b, 0, FSL)
        for s in range(FSL):
            out_ref[base + s, rows, :] = z[:, s * LANES:(s + 1) * LANES]
    z_ref[...] = _slabs_to_block(out_ref, 0, FSL).astype(BF16)


def _fft2(yr, yi, st2):
    rows = pl.BlockSpec((FB * FN, FFT_WIDTH), lambda i: (i, 0))
    z = pl.pallas_call(
        _fft2_kernel,
        grid=(FN // FB,),
        in_specs=[rows, rows, pl.BlockSpec((FB, FN, 2 * FN), lambda i: (i, 0, 0))],
        out_specs=pl.BlockSpec((FN, FB, FFT_WIDTH), lambda i: (0, i, 0)),
        out_shape=jax.ShapeDtypeStruct((FN, FN, FFT_WIDTH), BF16),
        scratch_shapes=[pltpu.VMEM((FBH * FSL, FN * SUBLANES, LANES), F32)],
        compiler_params=_cparams(("parallel",)),
        name="fft2",
    )(yr.reshape(N_TOK, FFT_WIDTH), yi.reshape(N_TOK, FFT_WIDTH), st2)
    return z.reshape(N_TOK, FFT_WIDTH)


def _gelu_tanh(x):
    return 0.5 * x * (1.0 + jnp.tanh(math.sqrt(2.0 / math.pi) * (x + 0.044715 * (x * x * x))))


def _mix_kernel(x_ref, er_ref, ec_ref, lg_ref, lb_ref, m1_ref, s1_ref, wgs_ref, wgf_ref, bg_ref,
                yt_ref, zr_ref, wglu_ref, bglu_ref, wbs_ref, wbf_ref, bbf_ref, wo_ref, bo_ref,
                g1_ref, l1g_ref, l1b_ref, m2_ref, s2_ref, wr_ref, br_ref, tri_ref, etri_ref,
                h1_ref, u2_ref, pos_ref, gate_ref, cnt_ref, scr_ref):
    pos = _pos_code(er_ref, ec_ref, TM)

    def front(r0, nr):
        rows = slice(r0, r0 + nr)
        h = _layer_norm(x_ref[rows, :] + pos[rows, :], lg_ref[...], lb_ref[...])
        u = (h * m1_ref[...] + s1_ref[...]).astype(BF16)

        c0, nc = r0 // CH, nr // CH
        for t in range(CH):
            for j in range(NJ):
                scr_ref[j, pl.ds(r0 + t, nc, stride=CH), :] = (
                    yt_ref[t, c0:c0 + nc, j * LANES:(j + 1) * LANES].astype(F32))
        ys = jnp.concatenate([scr_ref[j, rows, :] for j in range(NJ)], axis=-1)
        z = jnp.dot(_gelu_tanh(ys).astype(BF16), wglu_ref[...], preferred_element_type=F32) + bglu_ref[...]
        glu = (z[:, :S5_WIDTH] * _sigmoid(z[:, S5_WIDTH:])).astype(BF16)
        g_s5 = _sigmoid(jnp.dot(u, wgs_ref[...], preferred_element_type=F32) + bg_ref[:, :D])
        g_fft = _sigmoid(jnp.dot(u, wgf_ref[...], preferred_element_type=F32) + bg_ref[:, D:])
        y_s5 = jnp.dot(glu, wbs_ref[...], preferred_element_type=F32)
        y_fft = jnp.dot(zr_ref[rows, :], wbf_ref[...], preferred_element_type=F32) + bbf_ref[...]
        mixed = (g_s5 * y_s5 + g_fft * y_fft).astype(BF16)
        y = jnp.dot(mixed, wo_ref[...], preferred_element_type=F32) + bo_ref[...]
        h1 = _layer_norm(ALPHA * h + g1_ref[...] * y, l1g_ref[...], l1b_ref[...])
        h1_ref[rows, :] = h1
        u2 = h1 * m2_ref[...] + s2_ref[...]
        u2_ref[rows, :] = u2.astype(BF16)
        u_hi = u2.astype(BF16)
        u_lo = (u2 - u_hi.astype(F32)).astype(BF16)

        def nt(a, b):
            return lax.dot_general(a, b, (((1,), (1,)), ((), ())), preferred_element_type=F32)
        return nt(wr_ref[0], u_hi) + nt(wr_ref[0], u_lo) + nt(wr_ref[1], u_hi)

    logits = front(0, TM) + br_ref[:, 0:1]
    eidx = lax.broadcasted_iota(jnp.int32, (N_EXPERTS, TM), 0)
    vals, hots = [], []
    cur = logits
    for _k in range(TOP_K):
        m = jnp.max(cur, axis=0, keepdims=True)
        sel = jnp.min(jnp.where(cur == m, eidx, N_EXPERTS), axis=0, keepdims=True)
        hot = eidx == sel
        cur = jnp.where(hot, -jnp.inf, cur)
        vals.append(m)
        hots.append(hot)
    exps = [jnp.exp(v - vals[0]) for v in vals]
    den = exps[0] + exps[1] + exps[2] + exps[3]
    gate4 = jnp.concatenate([e / den for e in exps], axis=0)

    hot_sum = (hots[0] | hots[1] | hots[2] | hots[3]).astype(F32)
    before = jnp.dot(hot_sum.astype(BF16), tri_ref[...], preferred_element_type=F32)
    cnt = jnp.broadcast_to(jnp.sum(hot_sum, axis=1, keepdims=True), (N_EXPERTS, LANES))
    cnt8 = jnp.floor((cnt + (SEG_ALIGN - 1)) * (1.0 / SEG_ALIGN)) * SEG_ALIGN
    seg0 = jnp.dot(etri_ref[...], cnt8.astype(BF16), preferred_element_type=F32)
    tot = seg0[:, 0:1] + before
    pos4 = jnp.concatenate(
        [jnp.sum(jnp.where(hk, tot, 0.0), axis=0, keepdims=True) for hk in hots], axis=0)
    pos_ref[...] = pos4.astype(jnp.int32)
    cnt_ref[0] = cnt

    gate_ref[...] = gate4


def _mix(x, emb_r, emb_c, lg, lb, m1, s1, w_in, bg, y_t, zr, wglu, bglu, wbs, wbf, bbf, wo, bo,
         g1, l1g, l1b, m2, s2, wr_t, br, tri, etri):
    gate_cols = (S5_WIDTH + FFT_WIDTH) // D
    vec = pl.BlockSpec((1, D), lambda i: (0, 0))

    def full(a):
        return pl.BlockSpec(a.shape, lambda i: (0,) * a.ndim)
    return pl.pallas_call(
        _mix_kernel,
        grid=(N_TOK // TM,),
        in_specs=[pl.BlockSpec((TM, D), lambda i: (i, 0)),
                  pl.BlockSpec((TM // GRID_W, D // 2), lambda i: (i, 0)),
                  pl.BlockSpec((GRID_W, D // 2), lambda i: (0, 0)),
                  vec, vec, vec, vec,
                  pl.BlockSpec((D, D), lambda i: (0, gate_cols)),
                  pl.BlockSpec((D, D), lambda i: (0, gate_cols + 1)), full(bg),
                  pl.BlockSpec((CH, TM // CH, S5_WIDTH), lambda i: (0, i, 0)),
                  pl.BlockSpec((TM, FFT_WIDTH), lambda i: (i, 0)),
                  full(wglu), full(bglu), full(wbs), full(wbf), full(bbf), full(wo), full(bo),
                  vec, vec, vec, vec, vec, full(wr_t), full(br), full(tri), full(etri)],
        out_specs=(pl.BlockSpec((TM, D), lambda i: (i, 0)),
                   pl.BlockSpec((TM, D), lambda i: (i, 0)),
                   pl.BlockSpec((TOP_K, TM), lambda i: (0, i)),
                   pl.BlockSpec((TOP_K, TM), lambda i: (0, i)),
                   pl.BlockSpec((1, N_EXPERTS, LANES), lambda i: (i, 0, 0))),
        out_shape=(jax.ShapeDtypeStruct((N_TOK, D), F32),
                   jax.ShapeDtypeStruct((N_TOK, D), BF16),
                   jax.ShapeDtypeStruct((TOP_K, N_TOK), jnp.int32),
                   jax.ShapeDtypeStruct((TOP_K, N_TOK), F32),
                   jax.ShapeDtypeStruct((N_TILES, N_EXPERTS, LANES), F32)),
        scratch_shapes=[pltpu.VMEM((NJ, TM, LANES), F32)],
        compiler_params=_cparams(("parallel",)),
        name="mix",
    )(x, emb_r, emb_c, lg, lb, m1, s1, w_in, w_in, bg, y_t, zr, wglu, bglu, wbs, wbf, bbf, wo, bo,
      g1, l1g, l1b, m2, s2, wr_t, br, tri, etri)


def _on_parity(i, fn):
    @pl.when(i % 2 == 0)
    def _():
        fn(0)

    @pl.when(i % 2 == 1)
    def _():
        fn(1)


def _dispatch_kernel(dprev_ref, dest_ref, pos_ref, u_ref, zeroed_ref, buf_ref, sorted_ref, sems):
    del zeroed_ref
    i = pl.program_id(0)

    @pl.when(i == 0)
    def _():
        sorted_ref[...] = jnp.zeros_like(sorted_ref)

    def chunk_copy(slot, table_ref, j):
        dst = pl.multiple_of(table_ref[0, 0, j], SEG_ALIGN)
        return pltpu.make_async_copy(sorted_ref.at[slot, pl.ds(j * SEG_ALIGN, SEG_ALIGN)],
                                     buf_ref.at[pl.ds(dst, SEG_ALIGN)], sems.at[slot])

    def drain(slot):
        pltpu.make_async_copy(sorted_ref.at[slot], buf_ref.at[pl.ds(0, CAP)], sems.at[slot]).wait()

    def run(slot):
        pos = pos_ref[...]
        u = u_ref[...]
        n_rb = CAP // CAP_BLOCK
        per_rb = NCHK // (n_rb // 2)
        for rb in range(n_rb):
            for j in range(rb * per_rb, min((rb + 1) * per_rb, NCHK)):
                chunk_copy(1 - slot, dprev_ref, j).start()
            rows = lax.broadcasted_iota(jnp.int32, (CAP_BLOCK, TM), 0) + rb * CAP_BLOCK
            hit = rows == pos[0:1]
            for k in range(1, TOP_K):
                hit = hit | (rows == pos[k:k + 1])
            onehot = jnp.where(hit, 1.0, 0.0).astype(BF16)
            sorted_ref[slot, rb * CAP_BLOCK:(rb + 1) * CAP_BLOCK, :] = jnp.dot(
                onehot, u, preferred_element_type=F32).astype(BF16)
        drain(1 - slot)

        @pl.when(i == N_TILES - 1)
        def _():
            def issue(j, c):
                chunk_copy(slot, dest_ref, j).start()
                return c
            lax.fori_loop(0, NCHK, issue, 0)
            drain(slot)
    _on_parity(i, run)


def _dispatch(chunk_table, pos_t, u2, zeroed):
    return pl.pallas_call(
        _dispatch_kernel,
        grid=(N_TILES,),
        in_specs=[pl.BlockSpec((1, 1, NCHK), lambda i: (i, 0, 0), memory_space=pltpu.SMEM),
                  pl.BlockSpec((1, 1, NCHK), lambda i: (i + 1, 0, 0), memory_space=pltpu.SMEM),
                  pl.BlockSpec((TOP_K, TM), lambda i: (0, i)),
                  pl.BlockSpec((TM, D), lambda i: (i, 0)),
                  pl.BlockSpec(memory_space=pl.ANY)],
        out_specs=pl.BlockSpec(memory_space=pl.ANY),
        out_shape=jax.ShapeDtypeStruct((ROWS_ALL, D), BF16),
        scratch_shapes=[pltpu.VMEM((2, CAP, D), BF16),
                        pltpu.SemaphoreType.DMA((2,))],
        input_output_aliases={4: 0},
        compiler_params=_cparams(("arbitrary",)),
        name="dispatch",
    )(chunk_table, chunk_table, pos_t, u2, zeroed)


def _ffn_kernel(be_ref, nu_ref, run_ref, nxt_ref, valid_ref, x_ref, wu_hbm, bu_ref, wd_hbm, bd_ref,
                y_ref, wu_ref, wd_ref, wub_ref, wdb_ref, sems):
    i = pl.program_id(0)
    used = i < nu_ref[0]

    def weight_copies(e, slot):
        return (pltpu.make_async_copy(wu_hbm.at[e], wu_ref.at[slot], sems.at[slot]),
                pltpu.make_async_copy(wd_hbm.at[e], wd_ref.at[slot], sems.at[slot]))

    @pl.when(used)
    def _():
        run = run_ref[i]

        @pl.when(run >= 0)
        def _():
            def open_run(slot):
                @pl.when(run == 0)
                def _():
                    for cp in weight_copies(be_ref[i], slot):
                        cp.start()

                @pl.when(nxt_ref[i] >= 0)
                def _():
                    for cp in weight_copies(nxt_ref[i], 1 - slot):
                        cp.start()
                for cp in weight_copies(be_ref[i], slot):
                    cp.wait()
                wub_ref[...] = wu_ref[slot].astype(BF16)
                wdb_ref[...] = wd_ref[slot].astype(BF16)
            _on_parity(run, open_run)

        def expert_rows(r0, nr):
            rows = slice(r0, r0 + nr)
            e = be_ref[i]
            h = (jnp.dot(x_ref[rows, :], wub_ref[...], preferred_element_type=F32)
                 + bu_ref[pl.ds(e, 1), :])
            h_glu = jnp.minimum(h[:, :D], SWIGLU_LIMIT)
            h_lin = jnp.clip(h[:, D:], -SWIGLU_LIMIT, SWIGLU_LIMIT)
            act = (h_glu * _sigmoid(SWIGLU_ALPHA * h_glu) * (h_lin + 1.0)).astype(BF16)
            y_ref[rows, :] = (jnp.dot(act, wdb_ref[...], preferred_element_type=F32)
                              + bd_ref[pl.ds(e, 1), :]).astype(BF16)

        valid = valid_ref[i]

        @pl.when(valid == BM)
        def _():
            expert_rows(0, BM)

        @pl.when(valid < BM)
        def _():
            for h0 in range(0, BM, FFN_HALF):
                @pl.when(valid >= h0 + FFN_HALF)
                def _(h0=h0):
                    expert_rows(h0, FFN_HALF)

                @pl.when(valid < h0 + FFN_HALF)
                def _(h0=h0):
                    for r0 in range(h0, h0 + FFN_HALF, FFN_TAIL):
                        @pl.when(r0 < valid)
                        def _(r0=r0):
                            expert_rows(r0, FFN_TAIL)

                        @pl.when(r0 >= valid)
                        def _(r0=r0):
                            y_ref[r0:r0 + FFN_TAIL, :] = jnp.zeros((FFN_TAIL, D), BF16)

    @pl.when(jnp.logical_not(used))
    def _():
        y_ref[...] = jnp.zeros_like(y_ref)


def _ffn(block_expert, n_used, run_id, next_expert, valid, buf, w_up, b_up, w_down, b_down):
    def blk(i, be, nu, *_):
        return jnp.minimum(i, nu[0] - 1)
    return pl.pallas_call(
        _ffn_kernel,
        grid_spec=pltpu.PrefetchScalarGridSpec(
            num_scalar_prefetch=5,
            grid=(N_BLOCKS_ALL,),
            in_specs=[pl.BlockSpec((BM, D), lambda i, *s: (blk(i, *s), 0)),
                      pl.BlockSpec(memory_space=pl.ANY),
                      pl.BlockSpec((N_EXPERTS, 2 * D), lambda i, *s: (0, 0)),
                      pl.BlockSpec(memory_space=pl.ANY),
                      pl.BlockSpec((N_EXPERTS, D), lambda i, *s: (0, 0))],
            out_specs=pl.BlockSpec((BM, D), lambda i, *s: (i, 0)),
            scratch_shapes=[pltpu.VMEM((2, D, 2 * D), F32),
                            pltpu.VMEM((2, D, D), F32),
                            pltpu.VMEM((D, 2 * D), BF16),
                            pltpu.VMEM((D, D), BF16),
                            pltpu.SemaphoreType.DMA((2,))]),
        out_shape=jax.ShapeDtypeStruct((ROWS_ALL, D), BF16),
        compiler_params=_cparams(("arbitrary",)),
        name="ffn",
    )(block_expert, n_used, run_id, next_expert, valid, buf, w_up, b_up, w_down, b_down)


def _combine_kernel(dest_ref, dnext_ref, y_ref, h1_ref, pos_ref, gate_ref, g2_ref, lg_ref, lb_ref,
                    o_ref, sorted_ref, sems):
    i = pl.program_id(0)

    def chunk_copy(slot, table_ref, j):
        src = pl.multiple_of(table_ref[0, 0, j], SEG_ALIGN)
        return pltpu.make_async_copy(y_ref.at[pl.ds(src, SEG_ALIGN)],
                                     sorted_ref.at[slot, pl.ds(j * SEG_ALIGN, SEG_ALIGN)],
                                     sems.at[slot])

    def drain(slot):
        pltpu.make_async_copy(y_ref.at[pl.ds(0, CAP)], sorted_ref.at[slot], sems.at[slot]).wait()

    @pl.when(i == 0)
    def _():
        def issue(j, c):
            chunk_copy(0, dest_ref, j).start()
            return c
        lax.fori_loop(0, NCHK, issue, 0)

    def run(slot):
        drain(slot)

        pos = pos_ref[...]
        gate = gate_ref[...]
        m = jnp.zeros((TM, D), F32)
        n_cb = CAP // CAP_BLOCK
        per_cb = NCHK // (n_cb // 2)
        for cb in range(n_cb):
            for j in range(cb * per_cb, min((cb + 1) * per_cb, NCHK)):
                chunk_copy(1 - slot, dnext_ref, j).start()
            rws = lax.broadcasted_iota(jnp.int32, (CAP_BLOCK, TM), 0) + cb * CAP_BLOCK
            g = jnp.where(rws == pos[0:1], gate[0:1], 0.0)
            for k in range(1, TOP_K):
                g = g + jnp.where(rws == pos[k:k + 1], gate[k:k + 1], 0.0)
            rows = sorted_ref[slot, cb * CAP_BLOCK:(cb + 1) * CAP_BLOCK, :]
            m = m + lax.dot_general(g.astype(BF16), rows, (((0,), (0,)), ((), ())),
                                    preferred_element_type=F32)
        o_ref[...] = _layer_norm(ALPHA * h1_ref[...] + g2_ref[...] * m, lg_ref[...], lb_ref[...])

        @pl.when(i == N_TILES - 1)
        def _():
            drain(1 - slot)
    _on_parity(i, run)


def _combine(chunk_table, y_buf, h1, pos_t, gate_t, g2, lg, lb):
    vec = pl.BlockSpec((1, D), lambda i: (0, 0))
    return pl.pallas_call(
        _combine_kernel,
        grid_spec=pltpu.PrefetchScalarGridSpec(
            num_scalar_prefetch=0,
            grid=(N_TILES,),
            in_specs=[pl.BlockSpec((1, 1, NCHK), lambda i: (i + 1, 0, 0), memory_space=pltpu.SMEM),
                      pl.BlockSpec((1, 1, NCHK), lambda i: (i + 2, 0, 0), memory_space=pltpu.SMEM),
                      pl.BlockSpec(memory_space=pl.ANY),
                      pl.BlockSpec((TM, D), lambda i: (i, 0)),
                      pl.BlockSpec((TOP_K, TM), lambda i: (0, i)),
                      pl.BlockSpec((TOP_K, TM), lambda i: (0, i)),
                      vec, vec, vec],
            out_specs=pl.BlockSpec((TM, D), lambda i: (i, 0)),
            scratch_shapes=[pltpu.VMEM((2, CAP, D), BF16),
                            pltpu.SemaphoreType.DMA((2,))]),
        out_shape=jax.ShapeDtypeStruct((N_TOK, D), F32),
        compiler_params=_cparams(("arbitrary",)),
        name="combine",
    )(chunk_table, chunk_table, y_buf, h1, pos_t, gate_t, g2, lg, lb)


def _sincos_tables():
    q = D // 4
    omega = 1.0 / (10000.0 ** (np.arange(q) / q))

    def emb(n):
        ang = np.arange(n)[:, None] * omega[None, :]
        return jnp.asarray(np.concatenate([np.sin(ang), np.cos(ang)], axis=-1), F32)
    return emb(N_TOK // GRID_W), emb(GRID_W)


def kernel(x, c, ctx, c_ctx, ln_in_g, ln_in_b, w_ada, b_ada, w_in, b_in, s5_lambda_re, s5_lambda_im, s5_log_dt, s5_b_re, s5_b_im, s5_c_re, s5_c_im, s5_d, w_glu, b_glu, w_br_s5, w_br_fft, b_br_fft, w_out, b_out, ln1_g, ln1_b, w_router, b_router, w_up, b_up, w_down, b_down, ln2_g, ln2_b):
    assert x.shape == (1, N_TOK, D) and ctx.shape == (1, N_CTX, D) and w_ada.shape[0] == 1
    row = lambda v: v.reshape(1, -1).astype(F32)

    cc = jnp.concatenate([c.reshape(1, D), c_ctx.reshape(1, D), jnp.zeros((SUBLANES - 2, D), F32)], axis=0)
    ada = _ada(cc, w_ada[0], row(b_ada[0]))
    sh1, sc1, g1, sh2, sc2, g2 = (ada[0:1, k * D:(k + 1) * D] for k in range(6))
    sh1c, sc1c = ada[1:2, 0:D], ada[1:2, D:2 * D]

    emb_r, emb_c = _sincos_tables()
    st1, st2, fc = _dft_tables()
    lg, lb = row(ln_in_g), row(ln_in_b)

    w_in_bf = w_in[0].astype(BF16)
    b_s5 = row(b_in[0][:S5_WIDTH])
    b_fft8 = jnp.concatenate([row(b_in[0][S5_WIDTH:S5_WIDTH + FFT_WIDTH]),
                              jnp.zeros((SUBLANES - 1, FFT_WIDTH), F32)], axis=0)
    b_g = row(b_in[0][S5_WIDTH + FFT_WIDTH:])
    w_fc, b_fc = _fft_weights(w_in[0], b_fft8, fc)
    bcat = jnp.concatenate([b_s5, b_fc[0:1]], axis=1)

    x2 = x[0]
    p_t, xr, xi = _proj(x2, emb_r, emb_c, lg, lb, 1.0 + sc1, sh1, w_in_bf, w_fc, bcat)
    pc_t = _ctx_proj(ctx[0], lg, lb, 1.0 + sc1c, sh1c, w_in_bf[:, :S5_WIDTH], b_s5)

    b_c, a_q, a_p, trans, ctx_w = _s5_tables(
        s5_lambda_re[0], s5_lambda_im[0], s5_log_dt[0], s5_b_re[0], s5_b_im[0],
        s5_c_re[0], s5_c_im[0], s5_d[0])
    y_t, zeroed = _s5(p_t, pc_t, b_c, a_q, a_p, trans, ctx_w)

    yr, yi = _fft1(xr, xi, st1)
    zr = _fft2(yr, yi, st2)

    tri = jnp.asarray(np.arange(TM)[:, None] < np.arange(TM)[None, :], BF16)
    br = jnp.broadcast_to(b_router[0].reshape(N_EXPERTS, 1), (N_EXPERTS, LANES))
    etri = jnp.asarray(np.arange(N_EXPERTS)[:, None] > np.arange(N_EXPERTS)[None, :], BF16)
    wr_t = jnp.transpose(w_router[0])
    wr_hi = wr_t.astype(BF16)
    wr_split = jnp.stack([wr_hi, (wr_t - wr_hi.astype(F32)).astype(BF16)], axis=0)
    h1, u2, pos_t, gate_t, counts = _mix(
        x2, emb_r, emb_c, lg, lb, 1.0 + sc1, sh1, w_in_bf, b_g, y_t, zr,
        w_glu[0].astype(BF16), row(b_glu[0]), w_br_s5[0].astype(BF16), w_br_fft[0].astype(BF16),
        row(b_br_fft[0]), w_out[0].astype(BF16), row(b_out[0]), g1, row(ln1_g[0]), row(ln1_b[0]),
        1.0 + sc2, sh2, wr_split, br, tri, etri)

    cnt = counts[:, :, 0].astype(jnp.int32)
    seg = (cnt + SEG_ALIGN - 1) // SEG_ALIGN * SEG_ALIGN
    seg_end = jnp.cumsum(seg, axis=1)
    seg_start = seg_end - seg
    padded = (jnp.sum(seg, axis=0) + BM - 1) // BM * BM
    pad_ends = jnp.cumsum(padded)
    seg_dest = (pad_ends - padded)[None, :] + jnp.cumsum(seg, axis=0) - seg
    chunk_row = jnp.arange(NCHK, dtype=jnp.int32) * SEG_ALIGN
    chunk_exp = jnp.minimum(jnp.sum(chunk_row[None, :, None] >= seg_end[:, None, :], axis=-1),
                            N_EXPERTS - 1)
    own = chunk_exp[:, :, None] == jnp.arange(N_EXPERTS, dtype=jnp.int32)[None, None, :]
    chunk_dest = (jnp.sum(jnp.where(own, (seg_dest - seg_start)[:, None, :], 0), axis=-1)
                  + chunk_row[None, :]).astype(jnp.int32).reshape(N_TILES, 1, NCHK)
    nchk = (seg_end[:, -1] // SEG_ALIGN).astype(jnp.int32)
    block_start = jnp.arange(N_BLOCKS_ALL, dtype=jnp.int32) * BM
    block_expert = jnp.minimum(jnp.sum(block_start[:, None] >= pad_ends[None, :], axis=1),
                               N_EXPERTS - 1).astype(jnp.int32)
    n_used = (pad_ends[-1:] // BM).astype(jnp.int32)
    opens = (block_start < pad_ends[-1]) & (
        block_expert != jnp.concatenate([jnp.full((1,), -1, jnp.int32), block_expert[:-1]]))
    run_id = jnp.where(opens, jnp.cumsum(opens.astype(jnp.int32)) - 1, -1).astype(jnp.int32)
    experts = jnp.arange(N_EXPERTS, dtype=jnp.int32)
    later = (experts[None, :] > block_expert[:, None]) & (padded[None, :] > 0)
    next_expert = jnp.min(jnp.where(later, experts[None, :], N_EXPERTS), axis=1)
    next_expert = jnp.where(next_expert < N_EXPERTS, next_expert, -1).astype(jnp.int32)

    spare = (ROWS + chunk_row)[None, None, :]
    chunk_table = jnp.concatenate(
        [spare,
         jnp.where(chunk_row[None, None, :] < (nchk * SEG_ALIGN)[:, None, None], chunk_dest, spare),
         spare], axis=0).astype(jnp.int32)
    fill_ends = (pad_ends - padded + jnp.sum(seg, axis=0)).astype(jnp.int32)
    buf = _dispatch(chunk_table, pos_t, u2, zeroed)
    mine = block_expert[:, None] == experts[None, :]
    filled = jnp.sum(jnp.where(mine, fill_ends[None, :], 0), axis=1)
    valid = jnp.clip(filled - block_start, 0, BM).astype(jnp.int32)
    y_buf = _ffn(block_expert, n_used, run_id, next_expert, valid, buf, w_up[0],
                 b_up[0], w_down[0], b_down[0])
    out = _combine(chunk_table, y_buf, h1, pos_t, gate_t, g2, row(ln2_g[0]), row(ln2_b[0]))
    return out.reshape(1, N_TOK, D)
```

```python
import functools
import math

import jax
import jax.numpy as jnp
import numpy as np
from jax import lax
from jax.experimental import pallas as pl
from jax.experimental.pallas import tpu as pltpu

F32 = jnp.float32
BF16 = jnp.bfloat16
HI = lax.Precision.HIGHEST

D = 1024
N_TOK = 16384
N_CTX = 256
GRID_W = 64
S5_GROUP = 16
S5_GROUPS = 32
S5_STATE = 64
S5_WIDTH = 512
FFT_GROUPS = 4
FFT_DIM = 128
FFT_WIDTH = 512
N_EXPERTS = 32
TOP_K = 4
LN_EPS = 1e-5
ALPHA = 2.0 ** 0.25
SWIGLU_ALPHA = 1.702
SWIGLU_LIMIT = 7.0

LANES = 128
SUBLANES = 8
VMEM_LIMIT = 56 * 1024 * 1024

CH = 8
N_CHUNK = N_TOK // CH
N_CHUNK_CTX = N_CTX // CH
NSEG = SUBLANES
SEG = N_CHUNK // NSEG
SCAN_UNROLL = 4
GPT = LANES // S5_GROUP
NJ = S5_WIDTH // LANES
CL = CH * LANES
SW = 4 * GPT * S5_STATE

FN = 128
FB = 16

TM = 512
TM_PROJ = 1024
N_TILES = N_TOK // TM
BM = 1024
FFN_HALF = 512
FFN_TAIL = 128
N_SLOTS = N_TOK * TOP_K
SEG_ALIGN = 2 * SUBLANES
CAP_BLOCK = 256
CAP = -(-(TOP_K * TM + N_EXPERTS * (SEG_ALIGN - 1)) // CAP_BLOCK) * CAP_BLOCK
NCHK = CAP // SEG_ALIGN
N_BLOCKS = -(-(N_SLOTS + N_TILES * N_EXPERTS * (SEG_ALIGN - 1)) // BM) + N_EXPERTS
ROWS = N_BLOCKS * BM
N_BLOCKS_ALL = N_BLOCKS + -(-CAP // BM)
ROWS_ALL = N_BLOCKS_ALL * BM
ZERO_UNIT = 512
ZERO_PARTS = ROWS_ALL // (NJ * ZERO_UNIT)
assert ZERO_UNIT * ZERO_PARTS * NJ == ROWS_ALL


def _cparams(sem):
    return pltpu.CompilerParams(dimension_semantics=sem, vmem_limit_bytes=VMEM_LIMIT)


def _layer_norm(x, g, b):
    mu = jnp.mean(x, axis=-1, keepdims=True)
    xc = x - mu
    var = jnp.mean(xc * xc, axis=-1, keepdims=True)
    return xc * lax.rsqrt(var + LN_EPS) * g + b


def _sigmoid(x):
    return 0.5 * jnp.tanh(0.5 * x) + 0.5


def _ada_kernel(c_ref, w_ref, b_ref, o_ref):
    c = c_ref[...]
    s = c * _sigmoid(c)
    w = w_ref[...]
    s_hi, w_hi = s.astype(BF16), w.astype(BF16)
    s_lo = (s - s_hi.astype(F32)).astype(BF16)
    w_lo = (w - w_hi.astype(F32)).astype(BF16)

    def mm(a, b):
        return jnp.dot(a, b, preferred_element_type=F32)
    o_ref[...] = mm(s_hi, w_hi) + mm(s_lo, w_hi) + mm(s_hi, w_lo) + b_ref[...]


def _ada(cc, w_ada, b_ada):
    nb = 4
    wb = 6 * D // nb
    return pl.pallas_call(
        _ada_kernel,
        grid=(nb,),
        in_specs=[pl.BlockSpec((SUBLANES, D), lambda i: (0, 0)),
                  pl.BlockSpec((D, wb), lambda i: (0, i)),
                  pl.BlockSpec((1, wb), lambda i: (0, i))],
        out_specs=pl.BlockSpec((SUBLANES, wb), lambda i: (0, i)),
        out_shape=jax.ShapeDtypeStruct((SUBLANES, 6 * D), F32),
        compiler_params=_cparams(("parallel",)),
        name="ada",
    )(cc, w_ada, b_ada)


def _fftw_kernel(w_ref, b_ref, f_ref, wo_ref, bo_ref):
    f = f_ref[...]
    w = w_ref[...]
    w_hi, f_hi = w.astype(BF16), f.astype(BF16)
    w_lo = (w - w_hi.astype(F32)).astype(BF16)
    f_lo = (f - f_hi.astype(F32)).astype(BF16)

    def mm(a, b):
        return jnp.dot(a, b, preferred_element_type=F32)
    wo_ref[...] = (mm(w_hi, f_hi) + mm(w_lo, f_hi) + mm(w_hi, f_lo)).astype(BF16)
    bo_ref[...] = jnp.dot(b_ref[...], f, preferred_element_type=F32, precision=HI)


def _fft_weights(w_in, b_fft8, fc):
    def full(a):
        return pl.BlockSpec(a.shape, lambda i: (0,) * a.ndim)
    outs = (jax.ShapeDtypeStruct((D, 2 * FFT_WIDTH), BF16),
            jax.ShapeDtypeStruct((SUBLANES, 2 * FFT_WIDTH), F32))
    return pl.pallas_call(
        _fftw_kernel,
        grid=(1,),
        in_specs=[pl.BlockSpec((D, FFT_WIDTH), lambda i: (0, S5_WIDTH // FFT_WIDTH)),
                  full(b_fft8), full(fc)],
        out_specs=tuple(pl.BlockSpec(o.shape, lambda i: (0, 0)) for o in outs),
        out_shape=outs,
        compiler_params=_cparams(("arbitrary",)),
        name="fftw",
    )(w_in, b_fft8, fc)


def _pos_code(er_ref, ec_ref, tm):
    nr = tm // GRID_W
    er = er_ref[...]
    row = jnp.broadcast_to(er[:, None, :], (nr, GRID_W, D // 2)).reshape(tm, D // 2)
    col = jnp.concatenate([ec_ref[...]] * nr, axis=0)
    return jnp.concatenate([row, col], axis=-1)


def _to_chunk_major(val, scr_ref, out_ref, tm):
    for j in range(NJ):
        scr_ref[j] = val[:, j * LANES:(j + 1) * LANES]
    for t in range(CH):
        for j in range(NJ):
            piece = scr_ref[j, pl.ds(t, tm // CH, stride=CH), :]
            out_ref[t, :, j * LANES:(j + 1) * LANES] = piece.astype(out_ref.dtype)


def _proj_kernel(x_ref, er_ref, ec_ref, lg_ref, lb_ref, m_ref, s_ref, ws_ref, wf_ref, b_ref,
                 p_ref, xr_ref, xi_ref, scr_ref):
    x = x_ref[...] + _pos_code(er_ref, ec_ref, TM_PROJ)
    h = _layer_norm(x, lg_ref[...], lb_ref[...])
    u = (h * m_ref[...] + s_ref[...]).astype(BF16)
    p_s5 = jnp.dot(u, ws_ref[...], preferred_element_type=F32) + b_ref[:, :S5_WIDTH]
    _to_chunk_major(p_s5, scr_ref, p_ref, TM_PROJ)
    p_f = jnp.dot(u, wf_ref[...], preferred_element_type=F32) + b_ref[:, S5_WIDTH:]
    xr_ref[...] = p_f[:, :FFT_WIDTH].astype(BF16)
    xi_ref[...] = p_f[:, FFT_WIDTH:].astype(BF16)


def _proj(x, emb_r, emb_c, lg, lb, m1, s1, w_in, w_fc, bcat):
    nw = bcat.shape[1]
    vec = pl.BlockSpec((1, D), lambda i: (0, 0))
    return pl.pallas_call(
        _proj_kernel,
        grid=(N_TOK // TM_PROJ,),
        in_specs=[pl.BlockSpec((TM_PROJ, D), lambda i: (i, 0)),
                  pl.BlockSpec((TM_PROJ // GRID_W, D // 2), lambda i: (i, 0)),
                  pl.BlockSpec((GRID_W, D // 2), lambda i: (0, 0)),
                  vec, vec, vec, vec,
                  pl.BlockSpec((D, S5_WIDTH), lambda i: (0, 0)),
                  pl.BlockSpec((D, 2 * FFT_WIDTH), lambda i: (0, 0)),
                  pl.BlockSpec((1, nw), lambda i: (0, 0))],
        out_specs=(pl.BlockSpec((CH, TM_PROJ // CH, S5_WIDTH), lambda i: (0, i, 0)),
                   pl.BlockSpec((TM_PROJ, FFT_WIDTH), lambda i: (i, 0)),
                   pl.BlockSpec((TM_PROJ, FFT_WIDTH), lambda i: (i, 0))),
        out_shape=(jax.ShapeDtypeStruct((CH, N_CHUNK, S5_WIDTH), BF16),
                   jax.ShapeDtypeStruct((N_TOK, FFT_WIDTH), BF16),
                   jax.ShapeDtypeStruct((N_TOK, FFT_WIDTH), BF16)),
        scratch_shapes=[pltpu.VMEM((NJ, TM_PROJ, LANES), F32)],
        compiler_params=_cparams(("parallel",)),
        name="proj",
    )(x, emb_r, emb_c, lg, lb, m1, s1, w_in, w_fc, bcat)


def _ctx_proj_kernel(x_ref, lg_ref, lb_ref, m_ref, s_ref, w_ref, b_ref, p_ref, scr_ref):
    h = _layer_norm(x_ref[...], lg_ref[...], lb_ref[...])
    u = (h * m_ref[...] + s_ref[...]).astype(BF16)
    p = jnp.dot(u, w_ref[...], preferred_element_type=F32) + b_ref[...]
    _to_chunk_major(p, scr_ref, p_ref, N_CTX)


def _ctx_proj(ctx, lg, lb, m1, s1, w_s5, b_s5):
    return pl.pallas_call(
        _ctx_proj_kernel,
        out_shape=jax.ShapeDtypeStruct((CH, N_CHUNK_CTX, S5_WIDTH), BF16),
        scratch_shapes=[pltpu.VMEM((NJ, N_CTX, LANES), F32)],
        compiler_params=pltpu.CompilerParams(vmem_limit_bytes=VMEM_LIMIT),
        name="ctxproj",
    )(ctx, lg, lb, m1, s1, w_s5, b_s5)


def _s5_tables(lam_re, lam_im, log_dt, b_re, b_im, c_re, c_im, d_skip):
    dt = jnp.exp(log_dt)[..., None]
    zr = lam_re * dt
    zi = lam_im * dt

    def apow(m):
        m = jnp.asarray(m, F32)
        mag = jnp.exp(zr[..., None] * m)
        return mag * jnp.cos(zi[..., None] * m), mag * jnp.sin(zi[..., None] * m)

    a_re, a_im = apow(jnp.ones((1,), F32))
    a_re, a_im = a_re[..., 0], a_im[..., 0]
    den = lam_re * lam_re + lam_im * lam_im
    num_re = a_re - 1.0
    k_re = (num_re * lam_re + a_im * lam_im) / den
    k_im = (a_im * lam_re - num_re * lam_im) / den
    bb_re = k_re[..., None] * b_re - k_im[..., None] * b_im
    bb_im = k_re[..., None] * b_im + k_im[..., None] * b_re

    ks = jnp.arange(CH + 1, dtype=F32)
    pw_re, pw_im = apow(ks)
    kmag = jnp.exp(zr[:, :, None, :] * ks[None, None, :, None])
    pk_re = kmag * jnp.cos(zi[:, :, None, :] * ks[None, None, :, None])
    pk_im = kmag * jnp.sin(zi[:, :, None, :] * ks[None, None, :, None])
    bt_re, bt_im = jnp.swapaxes(b_re, 2, 3), jnp.swapaxes(b_im, 2, 3)
    bbt_re = k_re[:, :, None, :] * bt_re - k_im[:, :, None, :] * bt_im
    bbt_im = k_re[:, :, None, :] * bt_im + k_im[:, :, None, :] * bt_re

    ar, ai = pw_re[:, :, :, :CH, None], pw_im[:, :, :, :CH, None]
    cr = jnp.swapaxes(c_re, 2, 3)[:, :, :, None, :]
    ci = jnp.swapaxes(c_im, 2, 3)[:, :, :, None, :]
    ca = jnp.concatenate([cr * ar - ci * ai, -(cr * ai + ci * ar)], axis=2)
    ca = ca.reshape(2, S5_GROUPS, 2 * S5_STATE, CH * S5_GROUP)
    bbt = jnp.concatenate([bbt_re, bbt_im], axis=-1)
    taps = jnp.einsum('dghq,dgqn->dghn', bbt, ca, precision=HI)
    skip = (d_skip.reshape(S5_GROUPS, S5_GROUP, 1) * jnp.eye(S5_GROUP, dtype=F32)[None])
    taps = taps.at[0, :, :, :S5_GROUP].add(skip)
    b_c = jnp.transpose(taps.reshape(2, NJ, GPT, S5_GROUP, CH, S5_GROUP), (1, 4, 0, 2, 3, 5))
    b_c = b_c.reshape(NJ, 2 * CH, LANES, S5_GROUP)

    ef = (CH - 1) - jnp.arange(CH)
    eb = jnp.arange(CH)

    def q_part(d, e):
        pr = pk_re[d][:, e, None, :]
        pi = pk_im[d][:, e, None, :]
        br = bbt_re[d][:, None, :, :]
        bi = bbt_im[d][:, None, :, :]
        return pr * br - pi * bi, pr * bi + pi * br

    def q_rows(v):
        v = v.reshape(NJ, GPT, CH, S5_GROUP, S5_STATE)
        return jnp.transpose(v, (0, 2, 1, 3, 4)).reshape(NJ, CL, S5_STATE)
    a_q = jnp.stack([q_rows(v) for v in q_part(0, ef) + q_part(1, eb)], axis=0)

    of = jnp.arange(CH) + 1
    ob = CH - jnp.arange(CH)

    def p_part(d, e):
        pr = pw_re[d][..., e][:, :, :, None]
        pi = pw_im[d][..., e][:, :, :, None]
        return (ct_re[d] * pr - ct_im[d] * pi, -(ct_re[d] * pi + ct_im[d] * pr))
    ct_re = jnp.swapaxes(c_re, 2, 3)[:, :, :, None, :]
    ct_im = jnp.swapaxes(c_im, 2, 3)[:, :, :, None, :]
    a_p = jnp.stack([v.reshape(NJ, GPT * S5_STATE, CH * S5_GROUP)
                     for v in p_part(0, of) + p_part(1, ob)], axis=0)

    def lanes(v):
        return jnp.transpose(v.reshape(2, NJ, GPT * S5_STATE), (1, 0, 2))
    c_r, c_i = apow(jnp.full((1,), float(CH), F32))
    s_r, s_i = apow(jnp.full((1,), float(CH * SEG), F32))
    cr, ci, sr, si = (lanes(v[..., 0]) for v in (c_r, c_i, s_r, s_i))
    trans = jnp.stack([cr[:, 0], ci[:, 0], cr[:, 1], ci[:, 1],
                       sr[:, 0], si[:, 0], sr[:, 1], si[:, 1]], axis=1)

    cidx = jnp.arange(N_CHUNK_CTX, dtype=F32)
    wf_r, wf_i = apow(CH * (N_CHUNK_CTX - 1 - cidx))
    wb_r, wb_i = apow(CH * cidx)

    def ctx_lanes(v, d):
        return jnp.transpose(v[d].reshape(NJ, GPT * S5_STATE, N_CHUNK_CTX), (0, 2, 1))
    ctx_w = jnp.stack([ctx_lanes(wf_r, 0), ctx_lanes(wf_i, 0),
                       ctx_lanes(wb_r, 1), ctx_lanes(wb_i, 1)], axis=1)
    return b_c, a_q, a_p, trans, ctx_w


def _build_operators(bc_ref, aq_ref, ap_ref, c16_ref, c64_ref, cm_ref, wm_ref, wq_ref, wp_ref):
    def expand(a, c, row_shift, col_shift):
        w = jnp.dot(a, c, preferred_element_type=F32)
        rg = (lax.broadcasted_iota(jnp.int32, (w.shape[0], 1), 0) >> row_shift) & (GPT - 1)
        cg = (lax.broadcasted_iota(jnp.int32, (1, w.shape[1]), 1) >> col_shift) & (GPT - 1)
        return jnp.where(rg == cg, w, 0.0)

    blk = [expand(bc_ref[0, kd], c16_ref[...], 4, 4) for kd in range(2 * CH)]
    for t in range(CH):
        for u in range(CH):
            b = blk[2 * (u - t)] if u > t else blk[2 * (t - u) + 1] if u < t else blk[0] + blk[1]
            wm_ref[0, t * LANES:(t + 1) * LANES, u * LANES:(u + 1) * LANES] = b.astype(BF16)
    half = GPT * S5_STATE
    for s in range(4):
        wq_ref[0, :, s * half:(s + 1) * half] = expand(aq_ref[s, 0], c64_ref[...], 4, 6).astype(BF16)
        wp_ref[0, s * half:(s + 1) * half, :] = expand(ap_ref[s, 0], cm_ref[...], 6, 4).astype(BF16)


def _s5_kernel(p_ref, pc_ref, bc_ref, aq_ref, ap_ref, c16_ref, c64_ref, cm_ref, tr_ref, cw_ref,
               y_ref, rows_ref, v_ref, wm_ref, wq_ref, wp_ref, zero_ref, zsem):
    nq = NJ
    half = GPT * S5_STATE

    zero_ref[...] = jnp.zeros_like(zero_ref)

    def clears(fn):
        def unit(b, c):
            start = pl.multiple_of((pl.program_id(0) * ZERO_PARTS + b) * ZERO_UNIT, ZERO_UNIT)
            fn(pltpu.make_async_copy(zero_ref, rows_ref.at[pl.ds(start, ZERO_UNIT)], zsem))
            return c
        lax.fori_loop(0, ZERO_PARTS, unit, 0)
    clears(lambda cp: cp.start())
    _build_operators(bc_ref, aq_ref, ap_ref, c16_ref, c64_ref, cm_ref, wm_ref, wq_ref, wp_ref)

    def chunk_rows(ref, r0, nrows):
        return jnp.concatenate([ref[t, pl.ds(r0, nrows), :] for t in range(CH)], axis=-1)

    def fill(k, c):
        r0 = pl.multiple_of(k * SEG, SEG)
        v = jnp.dot(chunk_rows(p_ref, r0, SEG), wq_ref[0], preferred_element_type=F32)
        for s in range(4 * nq):
            v_ref[s, pl.ds(k, SEG, stride=NSEG), :] = v[:, s * LANES:(s + 1) * LANES]
        return c
    lax.fori_loop(0, NSEG, fill, 0)

    vc = jnp.dot(chunk_rows(pc_ref, 0, N_CHUNK_CTX), wq_ref[0], preferred_element_type=F32)
    vfr, vfi, vbr, vbi = (vc[:, i * half:(i + 1) * half] for i in range(4))
    wfr, wfi, wbr, wbi = (cw_ref[0, i] for i in range(4))
    s0_fr = jnp.sum(wfr * vfr - wfi * vfi, axis=0, keepdims=True)
    s0_fi = jnp.sum(wfr * vfi + wfi * vfr, axis=0, keepdims=True)
    s0_br = jnp.sum(wbr * vbr - wbi * vbi, axis=0, keepdims=True)
    s0_bi = jnp.sum(wbr * vbi + wbi * vbr, axis=0, keepdims=True)

    tr = tr_ref[0]
    afr, afi, abr, abi = (jnp.broadcast_to(tr[i:i + 1], (NSEG, half)) for i in range(4))
    gfr, gfi, gbr, gbi = (tr[i:i + 1] for i in range(4, 8))

    def load_part(part, i):
        return jnp.concatenate(
            [v_ref[part * nq + q, pl.ds(pl.multiple_of(i * NSEG, NSEG), NSEG), :] for q in range(nq)],
            axis=-1)

    def store_part(part, i, val):
        for q in range(nq):
            v_ref[part * nq + q, pl.ds(pl.multiple_of(i * NSEG, NSEG), NSEG), :] = (
                val[:, q * LANES:(q + 1) * LANES])

    def step(i, carry, write):
        fr, fi, br, bi = carry
        ib = SEG - 1 - i
        ufr, ufi = load_part(0, i), load_part(1, i)
        ubr, ubi = load_part(2, ib), load_part(3, ib)
        if write:
            store_part(0, i, fr)
            store_part(1, i, fi)
            store_part(2, ib, br)
            store_part(3, ib, bi)
        return (afr * fr - afi * fi + ufr, afr * fi + afi * fr + ufi,
                abr * br - abi * bi + ubr, abr * bi + abi * br + ubi)

    zero = jnp.zeros((NSEG, half), F32)
    ffr, ffi, fbr, fbi = lax.fori_loop(0, SEG, functools.partial(step, write=False),
                                       (zero, zero, zero, zero), unroll=SCAN_UNROLL)

    rows_fr, rows_fi = [s0_fr], [s0_fi]
    for k in range(1, NSEG):
        pr, pi = rows_fr[-1], rows_fi[-1]
        rows_fr.append(gfr * pr - gfi * pi + ffr[k - 1:k])
        rows_fi.append(gfr * pi + gfi * pr + ffi[k - 1:k])
    rows_br, rows_bi = [s0_br], [s0_bi]
    for k in range(NSEG - 2, -1, -1):
        pr, pi = rows_br[0], rows_bi[0]
        rows_br.insert(0, gbr * pr - gbi * pi + fbr[k + 1:k + 2])
        rows_bi.insert(0, gbr * pi + gbi * pr + fbi[k + 1:k + 2])
    init = tuple(jnp.concatenate(r, axis=0) for r in (rows_fr, rows_fi, rows_br, rows_bi))

    lax.fori_loop(0, SEG, functools.partial(step, write=True), init, unroll=SCAN_UNROLL)

    def emit(k, c):
        r0 = pl.multiple_of(k * SEG, SEG)
        b = chunk_rows(p_ref, r0, SEG)
        sin = jnp.concatenate([v_ref[s, pl.ds(k, SEG, stride=NSEG), :] for s in range(4 * nq)], axis=-1)
        y = (jnp.dot(b, wm_ref[0], preferred_element_type=F32)
             + jnp.dot(sin.astype(BF16), wp_ref[0], preferred_element_type=F32))
        for t in range(CH):
            y_ref[t, pl.ds(r0, SEG), :] = y[:, t * LANES:(t + 1) * LANES].astype(y_ref.dtype)
        return c
    lax.fori_loop(0, NSEG, emit, 0)
    clears(lambda cp: cp.wait())


def _s5(p_t, pc_t, b_c, a_q, a_p, trans, ctx_w):
    rep = np.ones((1, GPT))
    c16 = jnp.asarray(np.kron(rep, np.eye(S5_GROUP)), F32).astype(BF16)
    c64 = jnp.asarray(np.kron(rep, np.eye(S5_STATE)), F32).astype(BF16)
    c_m = jnp.asarray(np.kron(np.eye(CH), np.kron(rep, np.eye(S5_GROUP))), F32).astype(BF16)
    b_c, a_q, a_p = b_c.astype(BF16), a_q.astype(BF16), a_p.astype(BF16)
    half = GPT * S5_STATE
    return pl.pallas_call(
        _s5_kernel,
        grid=(NJ,),
        in_specs=[pl.BlockSpec((CH, N_CHUNK, LANES), lambda j: (0, 0, j)),
                  pl.BlockSpec((CH, N_CHUNK_CTX, LANES), lambda j: (0, 0, j)),
                  pl.BlockSpec((1, 2 * CH, LANES, S5_GROUP), lambda j: (j, 0, 0, 0)),
                  pl.BlockSpec((4, 1, CL, S5_STATE), lambda j: (0, j, 0, 0)),
                  pl.BlockSpec((4, 1, half, CH * S5_GROUP), lambda j: (0, j, 0, 0)),
                  pl.BlockSpec(c16.shape, lambda j: (0, 0)),
                  pl.BlockSpec(c64.shape, lambda j: (0, 0)),
                  pl.BlockSpec(c_m.shape, lambda j: (0, 0)),
                  pl.BlockSpec((1, SUBLANES, half), lambda j: (j, 0, 0)),
                  pl.BlockSpec((1, 4, N_CHUNK_CTX, half), lambda j: (j, 0, 0, 0))],
        out_specs=(pl.BlockSpec((CH, N_CHUNK, LANES), lambda j: (0, 0, j)),
                   pl.BlockSpec(memory_space=pl.ANY)),
        out_shape=(jax.ShapeDtypeStruct((CH, N_CHUNK, S5_WIDTH), BF16),
                   jax.ShapeDtypeStruct((ROWS_ALL, D), BF16)),
        scratch_shapes=[pltpu.VMEM((4 * NJ, N_CHUNK, LANES), F32),
                        pltpu.VMEM((1, CL, CL), BF16),
                        pltpu.VMEM((1, CL, SW), BF16),
                        pltpu.VMEM((1, SW, CL), BF16),
                        pltpu.VMEM((ZERO_UNIT, D), BF16),
                        pltpu.SemaphoreType.DMA(())],
        compiler_params=_cparams(("parallel",)),
        name="s5",
    )(p_t, pc_t, b_c, a_q, a_p, c16, c64, c_m, trans, ctx_w)


def _dft_tables():
    n = np.arange(FN)
    ang = 2.0 * np.pi * np.outer(n, n) / FN
    c, s = np.cos(ang), np.sin(ang)
    st1 = np.block([[c, s], [-s, c]])
    tw = 2.0 * np.pi * np.outer(n, n) / (FN * FN)
    wr, wi = np.cos(tw), -np.sin(tw)
    fr = c[None] * wr[:, None, :] + s[None] * wi[:, None, :]
    fi = c[None] * wi[:, None, :] - s[None] * wr[:, None, :]
    st2 = np.concatenate([fr, -fi], axis=-1)
    scale = 1.0 / math.sqrt(N_TOK * FFT_DIM)
    blk_c = np.kron(np.eye(FFT_GROUPS), c) * scale
    blk_s = np.kron(np.eye(FFT_GROUPS), s) * scale
    fc = np.concatenate([blk_c, -blk_s], axis=1)
    return (jnp.asarray(st1, F32).astype(BF16), jnp.asarray(st2, F32).astype(BF16), jnp.asarray(fc, F32))


FSL = FFT_WIDTH // LANES


FBH = FB // SUBLANES


def _block_to_slabs(blk, slab_ref, first, per_half):
    for bh in range(FBH):
        val = blk[:, bh * SUBLANES:(bh + 1) * SUBLANES, :].reshape(FN * SUBLANES, FFT_WIDTH)
        for s in range(FSL):
            slab_ref[bh * per_half + first + s] = val[:, s * LANES:(s + 1) * LANES]


def _slab_rows(b, first, per_half):
    return (b // SUBLANES) * per_half + first, pl.ds(b % SUBLANES, FN, stride=SUBLANES)


def _slabs_to_block(slab_ref, first, per_half):
    halves = []
    for bh in range(FBH):
        val = jnp.concatenate([slab_ref[bh * per_half + first + s] for s in range(FSL)], axis=-1)
        halves.append(val.reshape(FN, SUBLANES, FFT_WIDTH))
    return jnp.concatenate(halves, axis=1)


def _fft1_kernel(xr_ref, xi_ref, f_ref, yr_ref, yi_ref, in_ref, out_ref):
    _block_to_slabs(xr_ref[...].astype(F32), in_ref, 0, 2 * FSL)
    _block_to_slabs(xi_ref[...].astype(F32), in_ref, FSL, 2 * FSL)
    for b in range(FB):
        def part(first):
            base, rows = _slab_rows(b, first, 2 * FSL)
            return jnp.concatenate([in_ref[base + s, rows, :] for s in range(FSL)], axis=-1)
        xs = jnp.concatenate([part(0), part(FSL)], axis=0).astype(BF16)
        y = jnp.dot(f_ref[...], xs, preferred_element_type=F32)
        base, rows = _slab_rows(b, 0, 2 * FSL)
        for s in range(FSL):
            out_ref[base + s, rows, :] = y[:FN, s * LANES:(s + 1) * LANES]
            out_ref[base + FSL + s, rows, :] = y[FN:, s * LANES:(s + 1) * LANES]
    yr_ref[...] = _slabs_to_block(out_ref, 0, 2 * FSL).astype(BF16)
    yi_ref[...] = _slabs_to_block(out_ref, FSL, 2 * FSL).astype(BF16)


def _fft1(xr, xi, st1):
    spec = pl.BlockSpec((FN, FB, FFT_WIDTH), lambda i: (0, i, 0))
    slabs = pltpu.VMEM((FBH * 2 * FSL, FN * SUBLANES, LANES), F32)
    return pl.pallas_call(
        _fft1_kernel,
        grid=(FN // FB,),
        in_specs=[spec, spec, pl.BlockSpec((2 * FN, 2 * FN), lambda i: (0, 0))],
        out_specs=(spec, spec),
        out_shape=(jax.ShapeDtypeStruct((FN, FN, FFT_WIDTH), BF16),) * 2,
        scratch_shapes=[slabs, slabs],
        compiler_params=_cparams(("parallel",)),
        name="fft1",
    )(xr.reshape(FN, FN, FFT_WIDTH), xi.reshape(FN, FN, FFT_WIDTH), st1)


def _fft2_kernel(yr_ref, yi_ref, f_ref, z_ref, out_ref):
    for b in range(FB):
        ys = jnp.concatenate([yr_ref[b * FN:(b + 1) * FN, :], yi_ref[b * FN:(b + 1) * FN, :]], axis=0)
        z = jnp.dot(f_ref[b], ys, preferred_element_type=F32)
        base, rows = _slab_rows(b, 0, FSL)
        for s in range(FSL):
            out_ref[base + s, rows, :] = z[:, s * LANES:(s + 1) * LANES]
    z_ref[...] = _slabs_to_block(out_ref, 0, FSL).astype(BF16)


def _fft2(yr, yi, st2):
    rows = pl.BlockSpec((FB * FN, FFT_WIDTH), lambda i: (i, 0))
    z = pl.pallas_call(
        _fft2_kernel,
        grid=(FN // FB,),
        in_specs=[rows, rows, pl.BlockSpec((FB, FN, 2 * FN), lambda i: (i, 0, 0))],
        out_specs=pl.BlockSpec((FN, FB, FFT_WIDTH), lambda i: (0, i, 0)),
        out_shape=jax.ShapeDtypeStruct((FN, FN, FFT_WIDTH), BF16),
        scratch_shapes=[pltpu.VMEM((FBH * FSL, FN * SUBLANES, LANES), F32)],
        compiler_params=_cparams(("parallel",)),
        name="fft2",
    )(yr.reshape(N_TOK, FFT_WIDTH), yi.reshape(N_TOK, FFT_WIDTH), st2)
    return z.reshape(N_TOK, FFT_WIDTH)


def _gelu_tanh(x):
    return 0.5 * x * (1.0 + jnp.tanh(math.sqrt(2.0 / math.pi) * (x + 0.044715 * (x * x * x))))


def _mix_kernel(x_ref, er_ref, ec_ref, lg_ref, lb_ref, m1_ref, s1_ref, wgs_ref, wgf_ref, bg_ref,
                yt_ref, zr_ref, wglu_ref, bglu_ref, wbs_ref, wbf_ref, bbf_ref, wo_ref, bo_ref,
                g1_ref, l1g_ref, l1b_ref, m2_ref, s2_ref, wr_ref, br_ref, tri_ref, etri_ref,
                h1_ref, u2_ref, pos_ref, gate_ref, cnt_ref, scr_ref):
    pos = _pos_code(er_ref, ec_ref, TM)

    def front(r0, nr):
        rows = slice(r0, r0 + nr)
        h = _layer_norm(x_ref[rows, :] + pos[rows, :], lg_ref[...], lb_ref[...])
        u = (h * m1_ref[...] + s1_ref[...]).astype(BF16)

        c0, nc = r0 // CH, nr // CH
        for t in range(CH):
            for j in range(NJ):
                scr_ref[j, pl.ds(r0 + t, nc, stride=CH), :] = (
                    yt_ref[t, c0:c0 + nc, j * LANES:(j + 1) * LANES].astype(F32))
        ys = jnp.concatenate([scr_ref[j, rows, :] for j in range(NJ)], axis=-1)
        z = jnp.dot(_gelu_tanh(ys).astype(BF16), wglu_ref[...], preferred_element_type=F32) + bglu_ref[...]
        glu = (z[:, :S5_WIDTH] * _sigmoid(z[:, S5_WIDTH:])).astype(BF16)
        g_s5 = _sigmoid(jnp.dot(u, wgs_ref[...], preferred_element_type=F32) + bg_ref[:, :D])
        g_fft = _sigmoid(jnp.dot(u, wgf_ref[...], preferred_element_type=F32) + bg_ref[:, D:])
        y_s5 = jnp.dot(glu, wbs_ref[...], preferred_element_type=F32)
        y_fft = jnp.dot(zr_ref[rows, :], wbf_ref[...], preferred_element_type=F32) + bbf_ref[...]
        mixed = (g_s5 * y_s5 + g_fft * y_fft).astype(BF16)
        y = jnp.dot(mixed, wo_ref[...], preferred_element_type=F32) + bo_ref[...]
        h1 = _layer_norm(ALPHA * h + g1_ref[...] * y, l1g_ref[...], l1b_ref[...])
        h1_ref[rows, :] = h1
        u2 = h1 * m2_ref[...] + s2_ref[...]
        u2_ref[rows, :] = u2.astype(BF16)
        u_hi = u2.astype(BF16)
        u_lo = (u2 - u_hi.astype(F32)).astype(BF16)

        def nt(a, b):
            return lax.dot_general(a, b, (((1,), (1,)), ((), ())), preferred_element_type=F32)
        return nt(wr_ref[0], u_hi) + nt(wr_ref[0], u_lo) + nt(wr_ref[1], u_hi)

    logits = front(0, TM) + br_ref[:, 0:1]
    eidx = lax.broadcasted_iota(jnp.int32, (N_EXPERTS, TM), 0)
    vals, hots = [], []
    cur = logits
    for _k in range(TOP_K):
        m = jnp.max(cur, axis=0, keepdims=True)
        sel = jnp.min(jnp.where(cur == m, eidx, N_EXPERTS), axis=0, keepdims=True)
        hot = eidx == sel
        cur = jnp.where(hot, -jnp.inf, cur)
        vals.append(m)
        hots.append(hot)
    exps = [jnp.exp(v - vals[0]) for v in vals]
    den = exps[0] + exps[1] + exps[2] + exps[3]
    gate4 = jnp.concatenate([e / den for e in exps], axis=0)

    hot_sum = (hots[0] | hots[1] | hots[2] | hots[3]).astype(F32)
    before = jnp.dot(hot_sum.astype(BF16), tri_ref[...], preferred_element_type=F32)
    cnt = jnp.broadcast_to(jnp.sum(hot_sum, axis=1, keepdims=True), (N_EXPERTS, LANES))
    cnt8 = jnp.floor((cnt + (SEG_ALIGN - 1)) * (1.0 / SEG_ALIGN)) * SEG_ALIGN
    seg0 = jnp.dot(etri_ref[...], cnt8.astype(BF16), preferred_element_type=F32)
    tot = seg0[:, 0:1] + before
    pos4 = jnp.concatenate(
        [jnp.sum(jnp.where(hk, tot, 0.0), axis=0, keepdims=True) for hk in hots], axis=0)
    pos_ref[...] = pos4.astype(jnp.int32)
    cnt_ref[0] = cnt

    gate_ref[...] = gate4


def _mix(x, emb_r, emb_c, lg, lb, m1, s1, w_in, bg, y_t, zr, wglu, bglu, wbs, wbf, bbf, wo, bo,
         g1, l1g, l1b, m2, s2, wr_t, br, tri, etri):
    gate_cols = (S5_WIDTH + FFT_WIDTH) // D
    vec = pl.BlockSpec((1, D), lambda i: (0, 0))

    def full(a):
        return pl.BlockSpec(a.shape, lambda i: (0,) * a.ndim)
    return pl.pallas_call(
        _mix_kernel,
        grid=(N_TOK // TM,),
        in_specs=[pl.BlockSpec((TM, D), lambda i: (i, 0)),
                  pl.BlockSpec((TM // GRID_W, D // 2), lambda i: (i, 0)),
                  pl.BlockSpec((GRID_W, D // 2), lambda i: (0, 0)),
                  vec, vec, vec, vec,
                  pl.BlockSpec((D, D), lambda i: (0, gate_cols)),
                  pl.BlockSpec((D, D), lambda i: (0, gate_cols + 1)), full(bg),
                  pl.BlockSpec((CH, TM // CH, S5_WIDTH), lambda i: (0, i, 0)),
                  pl.BlockSpec((TM, FFT_WIDTH), lambda i: (i, 0)),
                  full(wglu), full(bglu), full(wbs), full(wbf), full(bbf), full(wo), full(bo),
                  vec, vec, vec, vec, vec, full(wr_t), full(br), full(tri), full(etri)],
        out_specs=(pl.BlockSpec((TM, D), lambda i: (i, 0)),
                   pl.BlockSpec((TM, D), lambda i: (i, 0)),
                   pl.BlockSpec((TOP_K, TM), lambda i: (0, i)),
                   pl.BlockSpec((TOP_K, TM), lambda i: (0, i)),
                   pl.BlockSpec((1, N_EXPERTS, LANES), lambda i: (i, 0, 0))),
        out_shape=(jax.ShapeDtypeStruct((N_TOK, D), F32),
                   jax.ShapeDtypeStruct((N_TOK, D), BF16),
                   jax.ShapeDtypeStruct((TOP_K, N_TOK), jnp.int32),
                   jax.ShapeDtypeStruct((TOP_K, N_TOK), F32),
                   jax.ShapeDtypeStruct((N_TILES, N_EXPERTS, LANES), F32)),
        scratch_shapes=[pltpu.VMEM((NJ, TM, LANES), F32)],
        compiler_params=_cparams(("parallel",)),
        name="mix",
    )(x, emb_r, emb_c, lg, lb, m1, s1, w_in, w_in, bg, y_t, zr, wglu, bglu, wbs, wbf, bbf, wo, bo,
      g1, l1g, l1b, m2, s2, wr_t, br, tri, etri)


def _on_parity(i, fn):
    @pl.when(i % 2 == 0)
    def _():
        fn(0)

    @pl.when(i % 2 == 1)
    def _():
        fn(1)


def _dispatch_kernel(dprev_ref, dest_ref, pos_ref, u_ref, zeroed_ref, buf_ref, sorted_ref, sems):
    del zeroed_ref
    i = pl.program_id(0)

    @pl.when(i == 0)
    def _():
        sorted_ref[...] = jnp.zeros_like(sorted_ref)

    def chunk_copy(slot, table_ref, j):
        dst = pl.multiple_of(table_ref[0, 0, j], SEG_ALIGN)
        return pltpu.make_async_copy(sorted_ref.at[slot, pl.ds(j * SEG_ALIGN, SEG_ALIGN)],
                                     buf_ref.at[pl.ds(dst, SEG_ALIGN)], sems.at[slot])

    def drain(slot):
        pltpu.make_async_copy(sorted_ref.at[slot], buf_ref.at[pl.ds(0, CAP)], sems.at[slot]).wait()

    def run(slot):
        pos = pos_ref[...]
        u = u_ref[...]
        n_rb = CAP // CAP_BLOCK
        per_rb = NCHK // (n_rb // 2)
        for rb in range(n_rb):
            for j in range(rb * per_rb, min((rb + 1) * per_rb, NCHK)):
                chunk_copy(1 - slot, dprev_ref, j).start()
            rows = lax.broadcasted_iota(jnp.int32, (CAP_BLOCK, TM), 0) + rb * CAP_BLOCK
            hit = rows == pos[0:1]
            for k in range(1, TOP_K):
                hit = hit | (rows == pos[k:k + 1])
            onehot = jnp.where(hit, 1.0, 0.0).astype(BF16)
            sorted_ref[slot, rb * CAP_BLOCK:(rb + 1) * CAP_BLOCK, :] = jnp.dot(
                onehot, u, preferred_element_type=F32).astype(BF16)
        drain(1 - slot)

        @pl.when(i == N_TILES - 1)
        def _():
            def issue(j, c):
                chunk_copy(slot, dest_ref, j).start()
                return c
            lax.fori_loop(0, NCHK, issue, 0)
            drain(slot)
    _on_parity(i, run)


def _dispatch(chunk_table, pos_t, u2, zeroed):
    return pl.pallas_call(
        _dispatch_kernel,
        grid=(N_TILES,),
        in_specs=[pl.BlockSpec((1, 1, NCHK), lambda i: (i, 0, 0), memory_space=pltpu.SMEM),
                  pl.BlockSpec((1, 1, NCHK), lambda i: (i + 1, 0, 0), memory_space=pltpu.SMEM),
                  pl.BlockSpec((TOP_K, TM), lambda i: (0, i)),
                  pl.BlockSpec((TM, D), lambda i: (i, 0)),
                  pl.BlockSpec(memory_space=pl.ANY)],
        out_specs=pl.BlockSpec(memory_space=pl.ANY),
        out_shape=jax.ShapeDtypeStruct((ROWS_ALL, D), BF16),
        scratch_shapes=[pltpu.VMEM((2, CAP, D), BF16),
                        pltpu.SemaphoreType.DMA((2,))],
        input_output_aliases={4: 0},
        compiler_params=_cparams(("arbitrary",)),
        name="dispatch",
    )(chunk_table, chunk_table, pos_t, u2, zeroed)


def _ffn_kernel(be_ref, nu_ref, run_ref, nxt_ref, valid_ref, x_ref, wu_hbm, bu_ref, wd_hbm, bd_ref,
                y_ref, wu_ref, wd_ref, wub_ref, wdb_ref, sems):
    i = pl.program_id(0)
    used = i < nu_ref[0]

    def weight_copies(e, slot):
        return (pltpu.make_async_copy(wu_hbm.at[e], wu_ref.at[slot], sems.at[slot]),
                pltpu.make_async_copy(wd_hbm.at[e], wd_ref.at[slot], sems.at[slot]))

    @pl.when(used)
    def _():
        run = run_ref[i]

        @pl.when(run >= 0)
        def _():
            def open_run(slot):
                @pl.when(run == 0)
                def _():
                    for cp in weight_copies(be_ref[i], slot):
                        cp.start()

                @pl.when(nxt_ref[i] >= 0)
                def _():
                    for cp in weight_copies(nxt_ref[i], 1 - slot):
                        cp.start()
                for cp in weight_copies(be_ref[i], slot):
                    cp.wait()
                wub_ref[...] = wu_ref[slot].astype(BF16)
                wdb_ref[...] = wd_ref[slot].astype(BF16)
            _on_parity(run, open_run)

        def expert_rows(r0, nr):
            rows = slice(r0, r0 + nr)
            e = be_ref[i]
            h = (jnp.dot(x_ref[rows, :], wub_ref[...], preferred_element_type=F32)
                 + bu_ref[pl.ds(e, 1), :])
            h_glu = jnp.minimum(h[:, :D], SWIGLU_LIMIT)
            h_lin = jnp.clip(h[:, D:], -SWIGLU_LIMIT, SWIGLU_LIMIT)
            act = (h_glu * _sigmoid(SWIGLU_ALPHA * h_glu) * (h_lin + 1.0)).astype(BF16)
            y_ref[rows, :] = (jnp.dot(act, wdb_ref[...], preferred_element_type=F32)
                              + bd_ref[pl.ds(e, 1), :]).astype(BF16)

        valid = valid_ref[i]

        @pl.when(valid == BM)
        def _():
            expert_rows(0, BM)

        @pl.when(valid < BM)
        def _():
            for h0 in range(0, BM, FFN_HALF):
                @pl.when(valid >= h0 + FFN_HALF)
                def _(h0=h0):
                    expert_rows(h0, FFN_HALF)

                @pl.when(valid < h0 + FFN_HALF)
                def _(h0=h0):
                    for r0 in range(h0, h0 + FFN_HALF, FFN_TAIL):
                        @pl.when(r0 < valid)
                        def _(r0=r0):
                            expert_rows(r0, FFN_TAIL)

                        @pl.when(r0 >= valid)
                        def _(r0=r0):
                            y_ref[r0:r0 + FFN_TAIL, :] = jnp.zeros((FFN_TAIL, D), BF16)


def _ffn(block_expert, n_used, run_id, next_expert, valid, buf, w_up, b_up, w_down, b_down):
    def blk(i, be, nu, *_):
        return jnp.minimum(i, nu[0] - 1)
    return pl.pallas_call(
        _ffn_kernel,
        grid_spec=pltpu.PrefetchScalarGridSpec(
            num_scalar_prefetch=5,
            grid=(N_BLOCKS_ALL,),
            in_specs=[pl.BlockSpec((BM, D), lambda i, *s: (blk(i, *s), 0)),
                      pl.BlockSpec(memory_space=pl.ANY),
                      pl.BlockSpec((N_EXPERTS, 2 * D), lambda i, *s: (0, 0)),
                      pl.BlockSpec(memory_space=pl.ANY),
                      pl.BlockSpec((N_EXPERTS, D), lambda i, *s: (0, 0))],
            out_specs=pl.BlockSpec((BM, D), lambda i, *s: (blk(i, *s), 0)),
            scratch_shapes=[pltpu.VMEM((2, D, 2 * D), F32),
                            pltpu.VMEM((2, D, D), F32),
                            pltpu.VMEM((D, 2 * D), BF16),
                            pltpu.VMEM((D, D), BF16),
                            pltpu.SemaphoreType.DMA((2,))]),
        out_shape=jax.ShapeDtypeStruct((ROWS_ALL, D), BF16),
        input_output_aliases={5: 0},
        compiler_params=_cparams(("arbitrary",)),
        name="ffn",
    )(block_expert, n_used, run_id, next_expert, valid, buf, w_up, b_up, w_down, b_down)


def _combine_kernel(dest_ref, dnext_ref, y_ref, h1_ref, pos_ref, gate_ref, g2_ref, lg_ref, lb_ref,
                    o_ref, sorted_ref, sems):
    i = pl.program_id(0)

    def chunk_copy(slot, table_ref, j):
        src = pl.multiple_of(table_ref[0, 0, j], SEG_ALIGN)
        return pltpu.make_async_copy(y_ref.at[pl.ds(src, SEG_ALIGN)],
                                     sorted_ref.at[slot, pl.ds(j * SEG_ALIGN, SEG_ALIGN)],
                                     sems.at[slot])

    def drain(slot):
        pltpu.make_async_copy(y_ref.at[pl.ds(0, CAP)], sorted_ref.at[slot], sems.at[slot]).wait()

    @pl.when(i == 0)
    def _():
        def issue(j, c):
            chunk_copy(0, dest_ref, j).start()
            return c
        lax.fori_loop(0, NCHK, issue, 0)

    def run(slot):
        drain(slot)

        pos = pos_ref[...]
        gate = gate_ref[...]
        m = jnp.zeros((TM, D), F32)
        n_cb = CAP // CAP_BLOCK
        per_cb = NCHK // (n_cb // 2)
        for cb in range(n_cb):
            for j in range(cb * per_cb, min((cb + 1) * per_cb, NCHK)):
                chunk_copy(1 - slot, dnext_ref, j).start()
            rws = lax.broadcasted_iota(jnp.int32, (CAP_BLOCK, TM), 0) + cb * CAP_BLOCK
            g = jnp.where(rws == pos[0:1], gate[0:1], 0.0)
            for k in range(1, TOP_K):
                g = g + jnp.where(rws == pos[k:k + 1], gate[k:k + 1], 0.0)
            rows = sorted_ref[slot, cb * CAP_BLOCK:(cb + 1) * CAP_BLOCK, :]
            m = m + lax.dot_general(g.astype(BF16), rows, (((0,), (0,)), ((), ())),
                                    preferred_element_type=F32)
        o_ref[...] = _layer_norm(ALPHA * h1_ref[...] + g2_ref[...] * m, lg_ref[...], lb_ref[...])

        @pl.when(i == N_TILES - 1)
        def _():
            drain(1 - slot)
    _on_parity(i, run)


def _combine(chunk_table, y_buf, h1, pos_t, gate_t, g2, lg, lb):
    vec = pl.BlockSpec((1, D), lambda i: (0, 0))
    return pl.pallas_call(
        _combine_kernel,
        grid_spec=pltpu.PrefetchScalarGridSpec(
            num_scalar_prefetch=0,
            grid=(N_TILES,),
            in_specs=[pl.BlockSpec((1, 1, NCHK), lambda i: (i + 1, 0, 0), memory_space=pltpu.SMEM),
                      pl.BlockSpec((1, 1, NCHK), lambda i: (i + 2, 0, 0), memory_space=pltpu.SMEM),
                      pl.BlockSpec(memory_space=pl.ANY),
                      pl.BlockSpec((TM, D), lambda i: (i, 0)),
                      pl.BlockSpec((TOP_K, TM), lambda i: (0, i)),
                      pl.BlockSpec((TOP_K, TM), lambda i: (0, i)),
                      vec, vec, vec],
            out_specs=pl.BlockSpec((TM, D), lambda i: (i, 0)),
            scratch_shapes=[pltpu.VMEM((2, CAP, D), BF16),
                            pltpu.SemaphoreType.DMA((2,))]),
        out_shape=jax.ShapeDtypeStruct((N_TOK, D), F32),
        compiler_params=_cparams(("arbitrary",)),
        name="combine",
    )(chunk_table, chunk_table, y_buf, h1, pos_t, gate_t, g2, lg, lb)


def _sincos_tables():
    q = D // 4
    omega = 1.0 / (10000.0 ** (np.arange(q) / q))

    def emb(n):
        ang = np.arange(n)[:, None] * omega[None, :]
        return jnp.asarray(np.concatenate([np.sin(ang), np.cos(ang)], axis=-1), F32)
    return emb(N_TOK // GRID_W), emb(GRID_W)


def kernel(x, c, ctx, c_ctx, ln_in_g, ln_in_b, w_ada, b_ada, w_in, b_in, s5_lambda_re, s5_lambda_im, s5_log_dt, s5_b_re, s5_b_im, s5_c_re, s5_c_im, s5_d, w_glu, b_glu, w_br_s5, w_br_fft, b_br_fft, w_out, b_out, ln1_g, ln1_b, w_router, b_router, w_up, b_up, w_down, b_down, ln2_g, ln2_b):
    assert x.shape == (1, N_TOK, D) and ctx.shape == (1, N_CTX, D) and w_ada.shape[0] == 1
    row = lambda v: v.reshape(1, -1).astype(F32)

    cc = jnp.concatenate([c.reshape(1, D), c_ctx.reshape(1, D), jnp.zeros((SUBLANES - 2, D), F32)], axis=0)
    ada = _ada(cc, w_ada[0], row(b_ada[0]))
    sh1, sc1, g1, sh2, sc2, g2 = (ada[0:1, k * D:(k + 1) * D] for k in range(6))
    sh1c, sc1c = ada[1:2, 0:D], ada[1:2, D:2 * D]

    emb_r, emb_c = _sincos_tables()
    st1, st2, fc = _dft_tables()
    lg, lb = row(ln_in_g), row(ln_in_b)

    w_in_bf = w_in[0].astype(BF16)
    b_s5 = row(b_in[0][:S5_WIDTH])
    b_fft8 = jnp.concatenate([row(b_in[0][S5_WIDTH:S5_WIDTH + FFT_WIDTH]),
                              jnp.zeros((SUBLANES - 1, FFT_WIDTH), F32)], axis=0)
    b_g = row(b_in[0][S5_WIDTH + FFT_WIDTH:])
    w_fc, b_fc = _fft_weights(w_in[0], b_fft8, fc)
    bcat = jnp.concatenate([b_s5, b_fc[0:1]], axis=1)

    x2 = x[0]
    p_t, xr, xi = _proj(x2, emb_r, emb_c, lg, lb, 1.0 + sc1, sh1, w_in_bf, w_fc, bcat)
    pc_t = _ctx_proj(ctx[0], lg, lb, 1.0 + sc1c, sh1c, w_in_bf[:, :S5_WIDTH], b_s5)

    b_c, a_q, a_p, trans, ctx_w = _s5_tables(
        s5_lambda_re[0], s5_lambda_im[0], s5_log_dt[0], s5_b_re[0], s5_b_im[0],
        s5_c_re[0], s5_c_im[0], s5_d[0])
    y_t, zeroed = _s5(p_t, pc_t, b_c, a_q, a_p, trans, ctx_w)

    yr, yi = _fft1(xr, xi, st1)
    zr = _fft2(yr, yi, st2)

    tri = jnp.asarray(np.arange(TM)[:, None] < np.arange(TM)[None, :], BF16)
    br = jnp.broadcast_to(b_router[0].reshape(N_EXPERTS, 1), (N_EXPERTS, LANES))
    etri = jnp.asarray(np.arange(N_EXPERTS)[:, None] > np.arange(N_EXPERTS)[None, :], BF16)
    wr_t = jnp.transpose(w_router[0])
    wr_hi = wr_t.astype(BF16)
    wr_split = jnp.stack([wr_hi, (wr_t - wr_hi.astype(F32)).astype(BF16)], axis=0)
    h1, u2, pos_t, gate_t, counts = _mix(
        x2, emb_r, emb_c, lg, lb, 1.0 + sc1, sh1, w_in_bf, b_g, y_t, zr,
        w_glu[0].astype(BF16), row(b_glu[0]), w_br_s5[0].astype(BF16), w_br_fft[0].astype(BF16),
        row(b_br_fft[0]), w_out[0].astype(BF16), row(b_out[0]), g1, row(ln1_g[0]), row(ln1_b[0]),
        1.0 + sc2, sh2, wr_split, br, tri, etri)

    cnt = counts[:, :, 0].astype(jnp.int32)
    seg = (cnt + SEG_ALIGN - 1) // SEG_ALIGN * SEG_ALIGN
    seg_end = jnp.cumsum(seg, axis=1)
    seg_start = seg_end - seg
    padded = (jnp.sum(seg, axis=0) + BM - 1) // BM * BM
    pad_ends = jnp.cumsum(padded)
    seg_dest = (pad_ends - padded)[None, :] + jnp.cumsum(seg, axis=0) - seg
    chunk_row = jnp.arange(NCHK, dtype=jnp.int32) * SEG_ALIGN
    chunk_exp = jnp.minimum(jnp.sum(chunk_row[None, :, None] >= seg_end[:, None, :], axis=-1),
                            N_EXPERTS - 1)
    own = chunk_exp[:, :, None] == jnp.arange(N_EXPERTS, dtype=jnp.int32)[None, None, :]
    chunk_dest = (jnp.sum(jnp.where(own, (seg_dest - seg_start)[:, None, :], 0), axis=-1)
                  + chunk_row[None, :]).astype(jnp.int32).reshape(N_TILES, 1, NCHK)
    nchk = (seg_end[:, -1] // SEG_ALIGN).astype(jnp.int32)
    block_start = jnp.arange(N_BLOCKS_ALL, dtype=jnp.int32) * BM
    block_expert = jnp.minimum(jnp.sum(block_start[:, None] >= pad_ends[None, :], axis=1),
                               N_EXPERTS - 1).astype(jnp.int32)
    n_used = (pad_ends[-1:] // BM).astype(jnp.int32)
    opens = (block_start < pad_ends[-1]) & (
        block_expert != jnp.concatenate([jnp.full((1,), -1, jnp.int32), block_expert[:-1]]))
    run_id = jnp.where(opens, jnp.cumsum(opens.astype(jnp.int32)) - 1, -1).astype(jnp.int32)
    experts = jnp.arange(N_EXPERTS, dtype=jnp.int32)
    later = (experts[None, :] > block_expert[:, None]) & (padded[None, :] > 0)
    next_expert = jnp.min(jnp.where(later, experts[None, :], N_EXPERTS), axis=1)
    next_expert = jnp.where(next_expert < N_EXPERTS, next_expert, -1).astype(jnp.int32)

    spare = (ROWS + chunk_row)[None, None, :]
    chunk_table = jnp.concatenate(
        [spare,
         jnp.where(chunk_row[None, None, :] < (nchk * SEG_ALIGN)[:, None, None], chunk_dest, spare),
         spare], axis=0).astype(jnp.int32)
    fill_ends = (pad_ends - padded + jnp.sum(seg, axis=0)).astype(jnp.int32)
    buf = _dispatch(chunk_table, pos_t, u2, zeroed)
    mine = block_expert[:, None] == experts[None, :]
    filled = jnp.sum(jnp.where(mine, fill_ends[None, :], 0), axis=1)
    valid = jnp.clip(filled - block_start, 0, BM).astype(jnp.int32)
    y_buf = _ffn(block_expert, n_used, run_id, next_expert, valid, buf, w_up[0],
                 b_up[0], w_down[0], b_down[0])
    out = _combine(chunk_table, y_buf, h1, pos_t, gate_t, g2, row(ln2_g[0]), row(ln2_b[0]))
    return out.reshape(1, N_TOK, D)
```

```python
import functools
import math

import jax
import jax.numpy as jnp
import numpy as np
from jax import lax
from jax.experimental import pallas as pl
from jax.experimental.pallas import tpu as pltpu

F32 = jnp.float32
BF16 = jnp.bfloat16
HI = lax.Precision.HIGHEST

D = 1024
N_TOK = 16384
N_CTX = 256
GRID_W = 64
S5_GROUP = 16
S5_GROUPS = 32
S5_STATE = 64
S5_WIDTH = 512
FFT_GROUPS = 4
FFT_DIM = 128
FFT_WIDTH = 512
N_EXPERTS = 32
TOP_K = 4
LN_EPS = 1e-5
ALPHA = 2.0 ** 0.25
SWIGLU_ALPHA = 1.702
SWIGLU_LIMIT = 7.0

LANES = 128
SUBLANES = 8
VMEM_LIMIT = 56 * 1024 * 1024

CH = 8
N_CHUNK = N_TOK // CH
N_CHUNK_CTX = N_CTX // CH
NSEG = SUBLANES
SEG = N_CHUNK // NSEG
SCAN_UNROLL = 8
GPT = LANES // S5_GROUP
NJ = S5_WIDTH // LANES
CL = CH * LANES
SW = 4 * GPT * S5_STATE

FN = 128
FB = 16

TM = 512
TM_PROJ = 1024
N_TILES = N_TOK // TM
BM = 1024
FFN_HALF = 512
FFN_TAIL = 256
N_SLOTS = N_TOK * TOP_K
SEG_ALIGN = 2 * SUBLANES
CAP_BLOCK = 256
CAP = -(-(TOP_K * TM + N_EXPERTS * (SEG_ALIGN - 1)) // CAP_BLOCK) * CAP_BLOCK
NCHK = CAP // SEG_ALIGN
N_BLOCKS = -(-(N_SLOTS + N_TILES * N_EXPERTS * (SEG_ALIGN - 1)) // BM) + N_EXPERTS
ROWS = N_BLOCKS * BM
N_BLOCKS_ALL = N_BLOCKS + -(-CAP // BM)
ROWS_ALL = N_BLOCKS_ALL * BM
ZERO_UNIT = 512
ZERO_PARTS = ROWS_ALL // (NJ * ZERO_UNIT)
assert ZERO_UNIT * ZERO_PARTS * NJ == ROWS_ALL


def _cparams(sem):
    return pltpu.CompilerParams(dimension_semantics=sem, vmem_limit_bytes=VMEM_LIMIT)


def _layer_norm(x, g, b):
    mu = jnp.mean(x, axis=-1, keepdims=True)
    xc = x - mu
    var = jnp.mean(xc * xc, axis=-1, keepdims=True)
    return xc * lax.rsqrt(var + LN_EPS) * g + b


def _sigmoid(x):
    return 0.5 * jnp.tanh(0.5 * x) + 0.5


def _ada_kernel(c_ref, w_ref, b_ref, o_ref):
    c = c_ref[...]
    s = c * _sigmoid(c)
    w = w_ref[...]
    s_hi, w_hi = s.astype(BF16), w.astype(BF16)
    s_lo = (s - s_hi.astype(F32)).astype(BF16)
    w_lo = (w - w_hi.astype(F32)).astype(BF16)

    def mm(a, b):
        return jnp.dot(a, b, preferred_element_type=F32)
    o_ref[...] = mm(s_hi, w_hi) + mm(s_lo, w_hi) + mm(s_hi, w_lo) + b_ref[...]


def _ada(cc, w_ada, b_ada):
    nb = 4
    wb = 6 * D // nb
    return pl.pallas_call(
        _ada_kernel,
        grid=(nb,),
        in_specs=[pl.BlockSpec((SUBLANES, D), lambda i: (0, 0)),
                  pl.BlockSpec((D, wb), lambda i: (0, i)),
                  pl.BlockSpec((1, wb), lambda i: (0, i))],
        out_specs=pl.BlockSpec((SUBLANES, wb), lambda i: (0, i)),
        out_shape=jax.ShapeDtypeStruct((SUBLANES, 6 * D), F32),
        compiler_params=_cparams(("parallel",)),
        name="ada",
    )(cc, w_ada, b_ada)


def _fftw_kernel(w_ref, b_ref, f_ref, wo_ref, bo_ref):
    f = f_ref[...]
    w = w_ref[...]
    w_hi, f_hi = w.astype(BF16), f.astype(BF16)
    w_lo = (w - w_hi.astype(F32)).astype(BF16)
    f_lo = (f - f_hi.astype(F32)).astype(BF16)

    def mm(a, b):
        return jnp.dot(a, b, preferred_element_type=F32)
    wo_ref[...] = (mm(w_hi, f_hi) + mm(w_lo, f_hi) + mm(w_hi, f_lo)).astype(BF16)
    bo_ref[...] = jnp.dot(b_ref[...], f, preferred_element_type=F32, precision=HI)


def _fft_weights(w_in, b_fft8, fc):
    def full(a):
        return pl.BlockSpec(a.shape, lambda i: (0,) * a.ndim)
    outs = (jax.ShapeDtypeStruct((D, 2 * FFT_WIDTH), BF16),
            jax.ShapeDtypeStruct((SUBLANES, 2 * FFT_WIDTH), F32))
    return pl.pallas_call(
        _fftw_kernel,
        grid=(1,),
        in_specs=[pl.BlockSpec((D, FFT_WIDTH), lambda i: (0, S5_WIDTH // FFT_WIDTH)),
                  full(b_fft8), full(fc)],
        out_specs=tuple(pl.BlockSpec(o.shape, lambda i: (0, 0)) for o in outs),
        out_shape=outs,
        compiler_params=_cparams(("arbitrary",)),
        name="fftw",
    )(w_in, b_fft8, fc)


def _pos_code(er_ref, ec_ref, tm):
    nr = tm // GRID_W
    er = er_ref[...]
    row = jnp.broadcast_to(er[:, None, :], (nr, GRID_W, D // 2)).reshape(tm, D // 2)
    col = jnp.concatenate([ec_ref[...]] * nr, axis=0)
    return jnp.concatenate([row, col], axis=-1)


def _to_chunk_major(val, scr_ref, out_ref, tm):
    for j in range(NJ):
        scr_ref[j] = val[:, j * LANES:(j + 1) * LANES]
    for t in range(CH):
        for j in range(NJ):
            piece = scr_ref[j, pl.ds(t, tm // CH, stride=CH), :]
            out_ref[t, :, j * LANES:(j + 1) * LANES] = piece.astype(out_ref.dtype)


def _proj_kernel(x_ref, er_ref, ec_ref, lg_ref, lb_ref, m_ref, s_ref, ws_ref, wf_ref, b_ref,
                 p_ref, xr_ref, xi_ref, scr_ref):
    x = x_ref[...] + _pos_code(er_ref, ec_ref, TM_PROJ)
    h = _layer_norm(x, lg_ref[...], lb_ref[...])
    u = (h * m_ref[...] + s_ref[...]).astype(BF16)
    p_s5 = jnp.dot(u, ws_ref[...], preferred_element_type=F32) + b_ref[:, :S5_WIDTH]
    _to_chunk_major(p_s5, scr_ref, p_ref, TM_PROJ)
    p_f = jnp.dot(u, wf_ref[...], preferred_element_type=F32) + b_ref[:, S5_WIDTH:]
    xr_ref[...] = p_f[:, :FFT_WIDTH].astype(BF16)
    xi_ref[...] = p_f[:, FFT_WIDTH:].astype(BF16)


def _proj(x, emb_r, emb_c, lg, lb, m1, s1, w_in, w_fc, bcat):
    nw = bcat.shape[1]
    vec = pl.BlockSpec((1, D), lambda i: (0, 0))
    return pl.pallas_call(
        _proj_kernel,
        grid=(N_TOK // TM_PROJ,),
        in_specs=[pl.BlockSpec((TM_PROJ, D), lambda i: (i, 0)),
                  pl.BlockSpec((TM_PROJ // GRID_W, D // 2), lambda i: (i, 0)),
                  pl.BlockSpec((GRID_W, D // 2), lambda i: (0, 0)),
                  vec, vec, vec, vec,
                  pl.BlockSpec((D, S5_WIDTH), lambda i: (0, 0)),
                  pl.BlockSpec((D, 2 * FFT_WIDTH), lambda i: (0, 0)),
                  pl.BlockSpec((1, nw), lambda i: (0, 0))],
        out_specs=(pl.BlockSpec((CH, TM_PROJ // CH, S5_WIDTH), lambda i: (0, i, 0)),
                   pl.BlockSpec((TM_PROJ, FFT_WIDTH), lambda i: (i, 0)),
                   pl.BlockSpec((TM_PROJ, FFT_WIDTH), lambda i: (i, 0))),
        out_shape=(jax.ShapeDtypeStruct((CH, N_CHUNK, S5_WIDTH), BF16),
                   jax.ShapeDtypeStruct((N_TOK, FFT_WIDTH), BF16),
                   jax.ShapeDtypeStruct((N_TOK, FFT_WIDTH), BF16)),
        scratch_shapes=[pltpu.VMEM((NJ, TM_PROJ, LANES), F32)],
        compiler_params=_cparams(("parallel",)),
        name="proj",
    )(x, emb_r, emb_c, lg, lb, m1, s1, w_in, w_fc, bcat)


def _ctx_proj_kernel(x_ref, lg_ref, lb_ref, m_ref, s_ref, w_ref, b_ref, p_ref, scr_ref):
    h = _layer_norm(x_ref[...], lg_ref[...], lb_ref[...])
    u = (h * m_ref[...] + s_ref[...]).astype(BF16)
    p = jnp.dot(u, w_ref[...], preferred_element_type=F32) + b_ref[...]
    _to_chunk_major(p, scr_ref, p_ref, N_CTX)


def _ctx_proj(ctx, lg, lb, m1, s1, w_s5, b_s5):
    return pl.pallas_call(
        _ctx_proj_kernel,
        out_shape=jax.ShapeDtypeStruct((CH, N_CHUNK_CTX, S5_WIDTH), BF16),
        scratch_shapes=[pltpu.VMEM((NJ, N_CTX, LANES), F32)],
        compiler_params=pltpu.CompilerParams(vmem_limit_bytes=VMEM_LIMIT),
        name="ctxproj",
    )(ctx, lg, lb, m1, s1, w_s5, b_s5)


def _s5_tables(lam_re, lam_im, log_dt, b_re, b_im, c_re, c_im, d_skip):
    dt = jnp.exp(log_dt)[..., None]
    zr = lam_re * dt
    zi = lam_im * dt

    def apow(m):
        m = jnp.asarray(m, F32)
        mag = jnp.exp(zr[..., None] * m)
        return mag * jnp.cos(zi[..., None] * m), mag * jnp.sin(zi[..., None] * m)

    a_re, a_im = apow(jnp.ones((1,), F32))
    a_re, a_im = a_re[..., 0], a_im[..., 0]
    den = lam_re * lam_re + lam_im * lam_im
    num_re = a_re - 1.0
    k_re = (num_re * lam_re + a_im * lam_im) / den
    k_im = (a_im * lam_re - num_re * lam_im) / den
    bb_re = k_re[..., None] * b_re - k_im[..., None] * b_im
    bb_im = k_re[..., None] * b_im + k_im[..., None] * b_re

    ks = jnp.arange(CH + 1, dtype=F32)
    pw_re, pw_im = apow(ks)
    kmag = jnp.exp(zr[:, :, None, :] * ks[None, None, :, None])
    pk_re = kmag * jnp.cos(zi[:, :, None, :] * ks[None, None, :, None])
    pk_im = kmag * jnp.sin(zi[:, :, None, :] * ks[None, None, :, None])
    bt_re, bt_im = jnp.swapaxes(b_re, 2, 3), jnp.swapaxes(b_im, 2, 3)
    bbt_re = k_re[:, :, None, :] * bt_re - k_im[:, :, None, :] * bt_im
    bbt_im = k_re[:, :, None, :] * bt_im + k_im[:, :, None, :] * bt_re

    ar, ai = pw_re[:, :, :, :CH, None], pw_im[:, :, :, :CH, None]
    cr = jnp.swapaxes(c_re, 2, 3)[:, :, :, None, :]
    ci = jnp.swapaxes(c_im, 2, 3)[:, :, :, None, :]
    ca = jnp.concatenate([cr * ar - ci * ai, -(cr * ai + ci * ar)], axis=2)
    ca = ca.reshape(2, S5_GROUPS, 2 * S5_STATE, CH * S5_GROUP)
    bbt = jnp.concatenate([bbt_re, bbt_im], axis=-1)
    taps = jnp.einsum('dghq,dgqn->dghn', bbt, ca, precision=HI)
    skip = (d_skip.reshape(S5_GROUPS, S5_GROUP, 1) * jnp.eye(S5_GROUP, dtype=F32)[None])
    taps = taps.at[0, :, :, :S5_GROUP].add(skip)
    b_c = jnp.transpose(taps.reshape(2, NJ, GPT, S5_GROUP, CH, S5_GROUP), (1, 4, 0, 2, 3, 5))
    b_c = b_c.reshape(NJ, 2 * CH, LANES, S5_GROUP)

    ef = (CH - 1) - jnp.arange(CH)
    eb = jnp.arange(CH)

    def q_part(d, e):
        pr = pk_re[d][:, e, None, :]
        pi = pk_im[d][:, e, None, :]
        br = bbt_re[d][:, None, :, :]
        bi = bbt_im[d][:, None, :, :]
        return pr * br - pi * bi, pr * bi + pi * br

    def q_rows(v):
        v = v.reshape(NJ, GPT, CH, S5_GROUP, S5_STATE)
        return jnp.transpose(v, (0, 2, 1, 3, 4)).reshape(NJ, CL, S5_STATE)
    a_q = jnp.stack([q_rows(v) for v in q_part(0, ef) + q_part(1, eb)], axis=0)

    of = jnp.arange(CH) + 1
    ob = CH - jnp.arange(CH)

    def p_part(d, e):
        pr = pw_re[d][..., e][:, :, :, None]
        pi = pw_im[d][..., e][:, :, :, None]
        return (ct_re[d] * pr - ct_im[d] * pi, -(ct_re[d] * pi + ct_im[d] * pr))
    ct_re = jnp.swapaxes(c_re, 2, 3)[:, :, :, None, :]
    ct_im = jnp.swapaxes(c_im, 2, 3)[:, :, :, None, :]
    a_p = jnp.stack([v.reshape(NJ, GPT * S5_STATE, CH * S5_GROUP)
                     for v in p_part(0, of) + p_part(1, ob)], axis=0)

    def lanes(v):
        return jnp.transpose(v.reshape(2, NJ, GPT * S5_STATE), (1, 0, 2))
    c_r, c_i = apow(jnp.full((1,), float(CH), F32))
    s_r, s_i = apow(jnp.full((1,), float(CH * SEG), F32))
    cr, ci, sr, si = (lanes(v[..., 0]) for v in (c_r, c_i, s_r, s_i))
    trans = jnp.stack([cr[:, 0], ci[:, 0], cr[:, 1], ci[:, 1],
                       sr[:, 0], si[:, 0], sr[:, 1], si[:, 1]], axis=1)

    cidx = jnp.arange(N_CHUNK_CTX, dtype=F32)
    wf_r, wf_i = apow(CH * (N_CHUNK_CTX - 1 - cidx))
    wb_r, wb_i = apow(CH * cidx)

    def ctx_lanes(v, d):
        return jnp.transpose(v[d].reshape(NJ, GPT * S5_STATE, N_CHUNK_CTX), (0, 2, 1))
    ctx_w = jnp.stack([ctx_lanes(wf_r, 0), ctx_lanes(wf_i, 0),
                       ctx_lanes(wb_r, 1), ctx_lanes(wb_i, 1)], axis=1)
    return b_c, a_q, a_p, trans, ctx_w


def _build_operators(bc_ref, aq_ref, ap_ref, c16_ref, c64_ref, cm_ref, wm_ref, wq_ref, wp_ref):
    def expand(a, c, row_shift, col_shift):
        w = jnp.dot(a, c, preferred_element_type=F32)
        rg = (lax.broadcasted_iota(jnp.int32, (w.shape[0], 1), 0) >> row_shift) & (GPT - 1)
        cg = (lax.broadcasted_iota(jnp.int32, (1, w.shape[1]), 1) >> col_shift) & (GPT - 1)
        return jnp.where(rg == cg, w, 0.0)

    blk = [expand(bc_ref[0, kd], c16_ref[...], 4, 4) for kd in range(2 * CH)]
    for t in range(CH):
        for u in range(CH):
            b = blk[2 * (u - t)] if u > t else blk[2 * (t - u) + 1] if u < t else blk[0] + blk[1]
            wm_ref[0, t * LANES:(t + 1) * LANES, u * LANES:(u + 1) * LANES] = b.astype(BF16)
    half = GPT * S5_STATE
    for s in range(4):
        wq_ref[0, :, s * half:(s + 1) * half] = expand(aq_ref[s, 0], c64_ref[...], 4, 6).astype(BF16)
        wp_ref[0, s * half:(s + 1) * half, :] = expand(ap_ref[s, 0], cm_ref[...], 6, 4).astype(BF16)


def _s5_kernel(p_ref, pc_ref, bc_ref, aq_ref, ap_ref, c16_ref, c64_ref, cm_ref, tr_ref, cw_ref,
               y_ref, rows_ref, v_ref, wm_ref, wq_ref, wp_ref, zero_ref, zsem):
    nq = NJ
    half = GPT * S5_STATE

    zero_ref[...] = jnp.zeros_like(zero_ref)

    def clears(fn):
        def unit(b, c):
            start = pl.multiple_of((pl.program_id(0) * ZERO_PARTS + b) * ZERO_UNIT, ZERO_UNIT)
            fn(pltpu.make_async_copy(zero_ref, rows_ref.at[pl.ds(start, ZERO_UNIT)], zsem))
            return c
        lax.fori_loop(0, ZERO_PARTS, unit, 0)
    clears(lambda cp: cp.start())
    _build_operators(bc_ref, aq_ref, ap_ref, c16_ref, c64_ref, cm_ref, wm_ref, wq_ref, wp_ref)

    def chunk_rows(ref, r0, nrows):
        return jnp.concatenate([ref[t, pl.ds(r0, nrows), :] for t in range(CH)], axis=-1)

    def fill(k, c):
        r0 = pl.multiple_of(k * SEG, SEG)
        v = jnp.dot(chunk_rows(p_ref, r0, SEG), wq_ref[0], preferred_element_type=F32)
        for s in range(4 * nq):
            v_ref[s, pl.ds(k, SEG, stride=NSEG), :] = v[:, s * LANES:(s + 1) * LANES]
        return c
    lax.fori_loop(0, NSEG, fill, 0)

    vc = jnp.dot(chunk_rows(pc_ref, 0, N_CHUNK_CTX), wq_ref[0], preferred_element_type=F32)
    vfr, vfi, vbr, vbi = (vc[:, i * half:(i + 1) * half] for i in range(4))
    wfr, wfi, wbr, wbi = (cw_ref[0, i] for i in range(4))
    s0_fr = jnp.sum(wfr * vfr - wfi * vfi, axis=0, keepdims=True)
    s0_fi = jnp.sum(wfr * vfi + wfi * vfr, axis=0, keepdims=True)
    s0_br = jnp.sum(wbr * vbr - wbi * vbi, axis=0, keepdims=True)
    s0_bi = jnp.sum(wbr * vbi + wbi * vbr, axis=0, keepdims=True)

    tr = tr_ref[0]
    afr, afi, abr, abi = (jnp.broadcast_to(tr[i:i + 1], (NSEG, half)) for i in range(4))
    gfr, gfi, gbr, gbi = (tr[i:i + 1] for i in range(4, 8))

    def load_part(part, i):
        return jnp.concatenate(
            [v_ref[part * nq + q, pl.ds(pl.multiple_of(i * NSEG, NSEG), NSEG), :] for q in range(nq)],
            axis=-1)

    def store_part(part, i, val):
        for q in range(nq):
            v_ref[part * nq + q, pl.ds(pl.multiple_of(i * NSEG, NSEG), NSEG), :] = (
                val[:, q * LANES:(q + 1) * LANES])

    def step(i, carry, write):
        fr, fi, br, bi = carry
        ib = SEG - 1 - i
        ufr, ufi = load_part(0, i), load_part(1, i)
        ubr, ubi = load_part(2, ib), load_part(3, ib)
        if write:
            store_part(0, i, fr)
            store_part(1, i, fi)
            store_part(2, ib, br)
            store_part(3, ib, bi)
        return (afr * fr - afi * fi + ufr, afr * fi + afi * fr + ufi,
                abr * br - abi * bi + ubr, abr * bi + abi * br + ubi)

    zero = jnp.zeros((NSEG, half), F32)
    ffr, ffi, fbr, fbi = lax.fori_loop(0, SEG, functools.partial(step, write=False),
                                       (zero, zero, zero, zero), unroll=SCAN_UNROLL)

    rows_fr, rows_fi = [s0_fr], [s0_fi]
    for k in range(1, NSEG):
        pr, pi = rows_fr[-1], rows_fi[-1]
        rows_fr.append(gfr * pr - gfi * pi + ffr[k - 1:k])
        rows_fi.append(gfr * pi + gfi * pr + ffi[k - 1:k])
    rows_br, rows_bi = [s0_br], [s0_bi]
    for k in range(NSEG - 2, -1, -1):
        pr, pi = rows_br[0], rows_bi[0]
        rows_br.insert(0, gbr * pr - gbi * pi + fbr[k + 1:k + 2])
        rows_bi.insert(0, gbr * pi + gbi * pr + fbi[k + 1:k + 2])
    init = tuple(jnp.concatenate(r, axis=0) for r in (rows_fr, rows_fi, rows_br, rows_bi))

    lax.fori_loop(0, SEG, functools.partial(step, write=True), init, unroll=SCAN_UNROLL)

    def emit(k, c):
        r0 = pl.multiple_of(k * SEG, SEG)
        b = chunk_rows(p_ref, r0, SEG)
        sin = jnp.concatenate([v_ref[s, pl.ds(k, SEG, stride=NSEG), :] for s in range(4 * nq)], axis=-1)
        y = (jnp.dot(b, wm_ref[0], preferred_element_type=F32)
             + jnp.dot(sin.astype(BF16), wp_ref[0], preferred_element_type=F32))
        for t in range(CH):
            y_ref[t, pl.ds(r0, SEG), :] = y[:, t * LANES:(t + 1) * LANES].astype(y_ref.dtype)
        return c
    lax.fori_loop(0, NSEG, emit, 0)
    clears(lambda cp: cp.wait())


def _s5(p_t, pc_t, b_c, a_q, a_p, trans, ctx_w):
    rep = np.ones((1, GPT))
    c16 = jnp.asarray(np.kron(rep, np.eye(S5_GROUP)), F32).astype(BF16)
    c64 = jnp.asarray(np.kron(rep, np.eye(S5_STATE)), F32).astype(BF16)
    c_m = jnp.asarray(np.kron(np.eye(CH), np.kron(rep, np.eye(S5_GROUP))), F32).astype(BF16)
    b_c, a_q, a_p = b_c.astype(BF16), a_q.astype(BF16), a_p.astype(BF16)
    half = GPT * S5_STATE
    return pl.pallas_call(
        _s5_kernel,
        grid=(NJ,),
        in_specs=[pl.BlockSpec((CH, N_CHUNK, LANES), lambda j: (0, 0, j)),
                  pl.BlockSpec((CH, N_CHUNK_CTX, LANES), lambda j: (0, 0, j)),
                  pl.BlockSpec((1, 2 * CH, LANES, S5_GROUP), lambda j: (j, 0, 0, 0)),
                  pl.BlockSpec((4, 1, CL, S5_STATE), lambda j: (0, j, 0, 0)),
                  pl.BlockSpec((4, 1, half, CH * S5_GROUP), lambda j: (0, j, 0, 0)),
                  pl.BlockSpec(c16.shape, lambda j: (0, 0)),
                  pl.BlockSpec(c64.shape, lambda j: (0, 0)),
                  pl.BlockSpec(c_m.shape, lambda j: (0, 0)),
                  pl.BlockSpec((1, SUBLANES, half), lambda j: (j, 0, 0)),
                  pl.BlockSpec((1, 4, N_CHUNK_CTX, half), lambda j: (j, 0, 0, 0))],
        out_specs=(pl.BlockSpec((CH, N_CHUNK, LANES), lambda j: (0, 0, j)),
                   pl.BlockSpec(memory_space=pl.ANY)),
        out_shape=(jax.ShapeDtypeStruct((CH, N_CHUNK, S5_WIDTH), BF16),
                   jax.ShapeDtypeStruct((ROWS_ALL, D), BF16)),
        scratch_shapes=[pltpu.VMEM((4 * NJ, N_CHUNK, LANES), F32),
                        pltpu.VMEM((1, CL, CL), BF16),
                        pltpu.VMEM((1, CL, SW), BF16),
                        pltpu.VMEM((1, SW, CL), BF16),
                        pltpu.VMEM((ZERO_UNIT, D), BF16),
                        pltpu.SemaphoreType.DMA(())],
        compiler_params=_cparams(("parallel",)),
        name="s5",
    )(p_t, pc_t, b_c, a_q, a_p, c16, c64, c_m, trans, ctx_w)


def _dft_tables():
    n = np.arange(FN)
    ang = 2.0 * np.pi * np.outer(n, n) / FN
    c, s = np.cos(ang), np.sin(ang)
    st1 = np.block([[c, s], [-s, c]])
    tw = 2.0 * np.pi * np.outer(n, n) / (FN * FN)
    wr, wi = np.cos(tw), -np.sin(tw)
    fr = c[None] * wr[:, None, :] + s[None] * wi[:, None, :]
    fi = c[None] * wi[:, None, :] - s[None] * wr[:, None, :]
    st2 = np.concatenate([fr, -fi], axis=-1)
    scale = 1.0 / math.sqrt(N_TOK * FFT_DIM)
    blk_c = np.kron(np.eye(FFT_GROUPS), c) * scale
    blk_s = np.kron(np.eye(FFT_GROUPS), s) * scale
    fc = np.concatenate([blk_c, -blk_s], axis=1)
    return (jnp.asarray(st1, F32).astype(BF16), jnp.asarray(st2, F32).astype(BF16), jnp.asarray(fc, F32))


FSL = FFT_WIDTH // LANES


FBH = FB // SUBLANES


def _block_to_slabs(blk, slab_ref, first, per_half):
    for bh in range(FBH):
        val = blk[:, bh * SUBLANES:(bh + 1) * SUBLANES, :].reshape(FN * SUBLANES, FFT_WIDTH)
        for s in range(FSL):
            slab_ref[bh * per_half + first + s] = val[:, s * LANES:(s + 1) * LANES]


def _slab_rows(b, first, per_half):
    return (b // SUBLANES) * per_half + first, pl.ds(b % SUBLANES, FN, stride=SUBLANES)


def _slabs_to_block(slab_ref, first, per_half):
    halves = []
    for bh in range(FBH):
        val = jnp.concatenate([slab_ref[bh * per_half + first + s] for s in range(FSL)], axis=-1)
        halves.append(val.reshape(FN, SUBLANES, FFT_WIDTH))
    return jnp.concatenate(halves, axis=1)


def _fft1_kernel(xr_ref, xi_ref, f_ref, yr_ref, yi_ref, in_ref, out_ref):
    _block_to_slabs(xr_ref[...].astype(F32), in_ref, 0, 2 * FSL)
    _block_to_slabs(xi_ref[...].astype(F32), in_ref, FSL, 2 * FSL)
    for b in range(FB):
        def part(first):
            base, rows = _slab_rows(b, first, 2 * FSL)
            return jnp.concatenate([in_ref[base + s, rows, :] for s in range(FSL)], axis=-1)
        xs = jnp.concatenate([part(0), part(FSL)], axis=0).astype(BF16)
        y = jnp.dot(f_ref[...], xs, preferred_element_type=F32)
        base, rows = _slab_rows(b, 0, 2 * FSL)
        for s in range(FSL):
            out_ref[base + s, rows, :] = y[:FN, s * LANES:(s + 1) * LANES]
            out_ref[base + FSL + s, rows, :] = y[FN:, s * LANES:(s + 1) * LANES]
    yr_ref[...] = _slabs_to_block(out_ref, 0, 2 * FSL).astype(BF16)
    yi_ref[...] = _slabs_to_block(out_ref, FSL, 2 * FSL).astype(BF16)


def _fft1(xr, xi, st1):
    spec = pl.BlockSpec((FN, FB, FFT_WIDTH), lambda i: (0, i, 0))
    slabs = pltpu.VMEM((FBH * 2 * FSL, FN * SUBLANES, LANES), F32)
    return pl.pallas_call(
        _fft1_kernel,
        grid=(FN // FB,),
        in_specs=[spec, spec, pl.BlockSpec((2 * FN, 2 * FN), lambda i: (0, 0))],
        out_specs=(spec, spec),
        out_shape=(jax.ShapeDtypeStruct((FN, FN, FFT_WIDTH), BF16),) * 2,
        scratch_shapes=[slabs, slabs],
        compiler_params=_cparams(("parallel",)),
        name="fft1",
    )(xr.reshape(FN, FN, FFT_WIDTH), xi.reshape(FN, FN, FFT_WIDTH), st1)


def _fft2_kernel(yr_ref, yi_ref, f_ref, z_ref, out_ref):
    for b in range(FB):
        ys = jnp.concatenate([yr_ref[b * FN:(b + 1) * FN, :], yi_ref[b * FN:(b + 1) * FN, :]], axis=0)
        z = jnp.dot(f_ref[b], ys, preferred_element_type=F32)
        base, rows = _slab_rows(b, 0, FSL)
        for s in range(FSL):
            out_ref[base + s, rows, :] = z[:, s * LANES:(s + 1) * LANES]
    z_ref[...] = _slabs_to_block(out_ref, 0, FSL).astype(BF16)


def _fft2(yr, yi, st2):
    rows = pl.BlockSpec((FB * FN, FFT_WIDTH), lambda i: (i, 0))
    z = pl.pallas_call(
        _fft2_kernel,
        grid=(FN // FB,),
        in_specs=[rows, rows, pl.BlockSpec((FB, FN, 2 * FN), lambda i: (i, 0, 0))],
        out_specs=pl.BlockSpec((FN, FB, FFT_WIDTH), lambda i: (0, i, 0)),
        out_shape=jax.ShapeDtypeStruct((FN, FN, FFT_WIDTH), BF16),
        scratch_shapes=[pltpu.VMEM((FBH * FSL, FN * SUBLANES, LANES), F32)],
        compiler_params=_cparams(("parallel",)),
        name="fft2",
    )(yr.reshape(N_TOK, FFT_WIDTH), yi.reshape(N_TOK, FFT_WIDTH), st2)
    return z.reshape(N_TOK, FFT_WIDTH)


def _gelu_tanh(x):
    return 0.5 * x * (1.0 + jnp.tanh(math.sqrt(2.0 / math.pi) * (x + 0.044715 * (x * x * x))))


def _mix_kernel(x_ref, er_ref, ec_ref, lg_ref, lb_ref, m1_ref, s1_ref, wgs_ref, wgf_ref, bg_ref,
                yt_ref, zr_ref, wglu_ref, bglu_ref, wbs_ref, wbf_ref, bbf_ref, wo_ref, bo_ref,
                g1_ref, l1g_ref, l1b_ref, m2_ref, s2_ref, wr_ref, br_ref, tri_ref, etri_ref,
                h1_ref, u2_ref, pos_ref, gate_ref, cnt_ref, scr_ref):
    pos = _pos_code(er_ref, ec_ref, TM)

    def front(r0, nr):
        rows = slice(r0, r0 + nr)
        h = _layer_norm(x_ref[rows, :] + pos[rows, :], lg_ref[...], lb_ref[...])
        u = (h * m1_ref[...] + s1_ref[...]).astype(BF16)

        c0, nc = r0 // CH, nr // CH
        for t in range(CH):
            for j in range(NJ):
                scr_ref[j, pl.ds(r0 + t, nc, stride=CH), :] = (
                    yt_ref[t, c0:c0 + nc, j * LANES:(j + 1) * LANES].astype(F32))
        ys = jnp.concatenate([scr_ref[j, rows, :] for j in range(NJ)], axis=-1)
        z = jnp.dot(_gelu_tanh(ys).astype(BF16), wglu_ref[...], preferred_element_type=F32) + bglu_ref[...]
        glu = (z[:, :S5_WIDTH] * _sigmoid(z[:, S5_WIDTH:])).astype(BF16)
        g_s5 = _sigmoid(jnp.dot(u, wgs_ref[...], preferred_element_type=F32) + bg_ref[:, :D])
        g_fft = _sigmoid(jnp.dot(u, wgf_ref[...], preferred_element_type=F32) + bg_ref[:, D:])
        y_s5 = jnp.dot(glu, wbs_ref[...], preferred_element_type=F32)
        y_fft = jnp.dot(zr_ref[rows, :], wbf_ref[...], preferred_element_type=F32) + bbf_ref[...]
        mixed = (g_s5 * y_s5 + g_fft * y_fft).astype(BF16)
        y = jnp.dot(mixed, wo_ref[...], preferred_element_type=F32) + bo_ref[...]
        h1 = _layer_norm(ALPHA * h + g1_ref[...] * y, l1g_ref[...], l1b_ref[...])
        h1_ref[rows, :] = h1
        u2 = h1 * m2_ref[...] + s2_ref[...]
        u2_ref[rows, :] = u2.astype(BF16)
        u_hi = u2.astype(BF16)
        u_lo = (u2 - u_hi.astype(F32)).astype(BF16)

        def nt(a, b):
            return lax.dot_general(a, b, (((1,), (1,)), ((), ())), preferred_element_type=F32)
        return nt(wr_ref[0], u_hi) + nt(wr_ref[0], u_lo) + nt(wr_ref[1], u_hi)

    logits = front(0, TM) + br_ref[:, 0:1]
    eidx = lax.broadcasted_iota(jnp.int32, (N_EXPERTS, TM), 0)
    vals, hots = [], []
    cur = logits
    for _k in range(TOP_K):
        m = jnp.max(cur, axis=0, keepdims=True)
        sel = jnp.min(jnp.where(cur == m, eidx, N_EXPERTS), axis=0, keepdims=True)
        hot = eidx == sel
        cur = jnp.where(hot, -jnp.inf, cur)
        vals.append(m)
        hots.append(hot)
    exps = [jnp.exp(v - vals[0]) for v in vals]
    den = exps[0] + exps[1] + exps[2] + exps[3]
    gate4 = jnp.concatenate([e / den for e in exps], axis=0)

    hot_sum = (hots[0] | hots[1] | hots[2] | hots[3]).astype(F32)
    before = jnp.dot(hot_sum.astype(BF16), tri_ref[...], preferred_element_type=F32)
    cnt = jnp.broadcast_to(jnp.sum(hot_sum, axis=1, keepdims=True), (N_EXPERTS, LANES))
    cnt8 = jnp.floor((cnt + (SEG_ALIGN - 1)) * (1.0 / SEG_ALIGN)) * SEG_ALIGN
    seg0 = jnp.dot(etri_ref[...], cnt8.astype(BF16), preferred_element_type=F32)
    tot = seg0[:, 0:1] + before
    pos4 = jnp.concatenate(
        [jnp.sum(jnp.where(hk, tot, 0.0), axis=0, keepdims=True) for hk in hots], axis=0)
    pos_ref[...] = pos4.astype(jnp.int32)
    cnt_ref[0] = cnt

    gate_ref[...] = gate4


def _mix(x, emb_r, emb_c, lg, lb, m1, s1, w_in, bg, y_t, zr, wglu, bglu, wbs, wbf, bbf, wo, bo,
         g1, l1g, l1b, m2, s2, wr_t, br, tri, etri):
    gate_cols = (S5_WIDTH + FFT_WIDTH) // D
    vec = pl.BlockSpec((1, D), lambda i: (0, 0))

    def full(a):
        return pl.BlockSpec(a.shape, lambda i: (0,) * a.ndim)
    return pl.pallas_call(
        _mix_kernel,
        grid=(N_TOK // TM,),
        in_specs=[pl.BlockSpec((TM, D), lambda i: (i, 0)),
                  pl.BlockSpec((TM // GRID_W, D // 2), lambda i: (i, 0)),
                  pl.BlockSpec((GRID_W, D // 2), lambda i: (0, 0)),
                  vec, vec, vec, vec,
                  pl.BlockSpec((D, D), lambda i: (0, gate_cols)),
                  pl.BlockSpec((D, D), lambda i: (0, gate_cols + 1)), full(bg),
                  pl.BlockSpec((CH, TM // CH, S5_WIDTH), lambda i: (0, i, 0)),
                  pl.BlockSpec((TM, FFT_WIDTH), lambda i: (i, 0)),
                  full(wglu), full(bglu), full(wbs), full(wbf), full(bbf), full(wo), full(bo),
                  vec, vec, vec, vec, vec, full(wr_t), full(br), full(tri), full(etri)],
        out_specs=(pl.BlockSpec((TM, D), lambda i: (i, 0)),
                   pl.BlockSpec((TM, D), lambda i: (i, 0)),
                   pl.BlockSpec((TOP_K, TM), lambda i: (0, i)),
                   pl.BlockSpec((TOP_K, TM), lambda i: (0, i)),
                   pl.BlockSpec((1, N_EXPERTS, LANES), lambda i: (i, 0, 0))),
        out_shape=(jax.ShapeDtypeStruct((N_TOK, D), F32),
                   jax.ShapeDtypeStruct((N_TOK, D), BF16),
                   jax.ShapeDtypeStruct((TOP_K, N_TOK), jnp.int32),
                   jax.ShapeDtypeStruct((TOP_K, N_TOK), F32),
                   jax.ShapeDtypeStruct((N_TILES, N_EXPERTS, LANES), F32)),
        scratch_shapes=[pltpu.VMEM((NJ, TM, LANES), F32)],
        compiler_params=_cparams(("parallel",)),
        name="mix",
    )(x, emb_r, emb_c, lg, lb, m1, s1, w_in, w_in, bg, y_t, zr, wglu, bglu, wbs, wbf, bbf, wo, bo,
      g1, l1g, l1b, m2, s2, wr_t, br, tri, etri)


def _on_parity(i, fn):
    @pl.when(i % 2 == 0)
    def _():
        fn(0)

    @pl.when(i % 2 == 1)
    def _():
        fn(1)


def _dispatch_kernel(dprev_ref, dest_ref, pos_ref, u_ref, zeroed_ref, buf_ref, sorted_ref, sems):
    del zeroed_ref
    i = pl.program_id(0)

    @pl.when(i == 0)
    def _():
        sorted_ref[...] = jnp.zeros_like(sorted_ref)

    def chunk_copy(slot, table_ref, j):
        dst = pl.multiple_of(table_ref[0, 0, j], SEG_ALIGN)
        return pltpu.make_async_copy(sorted_ref.at[slot, pl.ds(j * SEG_ALIGN, SEG_ALIGN)],
                                     buf_ref.at[pl.ds(dst, SEG_ALIGN)], sems.at[slot])

    def drain(slot):
        pltpu.make_async_copy(sorted_ref.at[slot], buf_ref.at[pl.ds(0, CAP)], sems.at[slot]).wait()

    def run(slot):
        pos = pos_ref[...]
        u = u_ref[...]
        n_rb = CAP // CAP_BLOCK
        per_rb = NCHK // (n_rb // 2)
        for rb in range(n_rb):
            for j in range(rb * per_rb, min((rb + 1) * per_rb, NCHK)):
                chunk_copy(1 - slot, dprev_ref, j).start()
            rows = lax.broadcasted_iota(jnp.int32, (CAP_BLOCK, TM), 0) + rb * CAP_BLOCK
            hit = rows == pos[0:1]
            for k in range(1, TOP_K):
                hit = hit | (rows == pos[k:k + 1])
            onehot = jnp.where(hit, 1.0, 0.0).astype(BF16)
            sorted_ref[slot, rb * CAP_BLOCK:(rb + 1) * CAP_BLOCK, :] = jnp.dot(
                onehot, u, preferred_element_type=F32).astype(BF16)
        drain(1 - slot)

        @pl.when(i == N_TILES - 1)
        def _():
            def issue(j, c):
                chunk_copy(slot, dest_ref, j).start()
                return c
            lax.fori_loop(0, NCHK, issue, 0)
            drain(slot)
    _on_parity(i, run)


def _dispatch(chunk_table, pos_t, u2, zeroed):
    return pl.pallas_call(
        _dispatch_kernel,
        grid=(N_TILES,),
        in_specs=[pl.BlockSpec((1, 1, NCHK), lambda i: (i, 0, 0), memory_space=pltpu.SMEM),
                  pl.BlockSpec((1, 1, NCHK), lambda i: (i + 1, 0, 0), memory_space=pltpu.SMEM),
                  pl.BlockSpec((TOP_K, TM), lambda i: (0, i)),
                  pl.BlockSpec((TM, D), lambda i: (i, 0)),
                  pl.BlockSpec(memory_space=pl.ANY)],
        out_specs=pl.BlockSpec(memory_space=pl.ANY),
        out_shape=jax.ShapeDtypeStruct((ROWS_ALL, D), BF16),
        scratch_shapes=[pltpu.VMEM((2, CAP, D), BF16),
                        pltpu.SemaphoreType.DMA((2,))],
        input_output_aliases={4: 0},
        compiler_params=_cparams(("arbitrary",)),
        name="dispatch",
    )(chunk_table, chunk_table, pos_t, u2, zeroed)


def _ffn_kernel(be_ref, nu_ref, run_ref, nxt_ref, valid_ref, x_ref, wu_hbm, bu_ref, wd_hbm, bd_ref,
                y_ref, wu_ref, wd_ref, wub_ref, wdb_ref, sems):
    i = pl.program_id(0)
    used = i < nu_ref[0]

    def weight_copies(e, slot):
        return (pltpu.make_async_copy(wu_hbm.at[e], wu_ref.at[slot], sems.at[slot]),
                pltpu.make_async_copy(wd_hbm.at[e], wd_ref.at[slot], sems.at[slot]))

    @pl.when(used)
    def _():
        run = run_ref[i]

        @pl.when(run >= 0)
        def _():
            def open_run(slot):
                @pl.when(run == 0)
                def _():
                    for cp in weight_copies(be_ref[i], slot):
                        cp.start()

                @pl.when(nxt_ref[i] >= 0)
                def _():
                    for cp in weight_copies(nxt_ref[i], 1 - slot):
                        cp.start()
                for cp in weight_copies(be_ref[i], slot):
                    cp.wait()
                wub_ref[...] = wu_ref[slot].astype(BF16)
                wdb_ref[...] = wd_ref[slot].astype(BF16)
            _on_parity(run, open_run)

        def expert_rows(r0, nr):
            rows = slice(r0, r0 + nr)
            e = be_ref[i]
            h = (jnp.dot(x_ref[rows, :], wub_ref[...], preferred_element_type=F32)
                 + bu_ref[pl.ds(e, 1), :])
            h_glu = jnp.minimum(h[:, :D], SWIGLU_LIMIT)
            h_lin = jnp.clip(h[:, D:], -SWIGLU_LIMIT, SWIGLU_LIMIT)
            act = (h_glu * _sigmoid(SWIGLU_ALPHA * h_glu) * (h_lin + 1.0)).astype(BF16)
            y_ref[rows, :] = (jnp.dot(act, wdb_ref[...], preferred_element_type=F32)
                              + bd_ref[pl.ds(e, 1), :]).astype(BF16)

        valid = valid_ref[i]

        @pl.when(valid == BM)
        def _():
            expert_rows(0, BM)

        @pl.when(valid < BM)
        def _():
            for h0 in range(0, BM, FFN_HALF):
                @pl.when(valid >= h0 + FFN_HALF)
                def _(h0=h0):
                    expert_rows(h0, FFN_HALF)

                @pl.when(valid < h0 + FFN_HALF)
                def _(h0=h0):
                    for r0 in range(h0, h0 + FFN_HALF, FFN_TAIL):
                        @pl.when(r0 < valid)
                        def _(r0=r0):
                            expert_rows(r0, FFN_TAIL)

                        @pl.when(r0 >= valid)
                        def _(r0=r0):
                            y_ref[r0:r0 + FFN_TAIL, :] = jnp.zeros((FFN_TAIL, D), BF16)


def _ffn(block_expert, n_used, run_id, next_expert, valid, buf, w_up, b_up, w_down, b_down):
    def blk(i, be, nu, *_):
        return jnp.minimum(i, nu[0] - 1)
    return pl.pallas_call(
        _ffn_kernel,
        grid_spec=pltpu.PrefetchScalarGridSpec(
            num_scalar_prefetch=5,
            grid=(N_BLOCKS_ALL,),
            in_specs=[pl.BlockSpec((BM, D), lambda i, *s: (blk(i, *s), 0)),
                      pl.BlockSpec(memory_space=pl.ANY),
                      pl.BlockSpec((N_EXPERTS, 2 * D), lambda i, *s: (0, 0)),
                      pl.BlockSpec(memory_space=pl.ANY),
                      pl.BlockSpec((N_EXPERTS, D), lambda i, *s: (0, 0))],
            out_specs=pl.BlockSpec((BM, D), lambda i, *s: (blk(i, *s), 0)),
            scratch_shapes=[pltpu.VMEM((2, D, 2 * D), F32),
                            pltpu.VMEM((2, D, D), F32),
                            pltpu.VMEM((D, 2 * D), BF16),
                            pltpu.VMEM((D, D), BF16),
                            pltpu.SemaphoreType.DMA((2,))]),
        out_shape=jax.ShapeDtypeStruct((ROWS_ALL, D), BF16),
        input_output_aliases={5: 0},
        compiler_params=_cparams(("arbitrary",)),
        name="ffn",
    )(block_expert, n_used, run_id, next_expert, valid, buf, w_up, b_up, w_down, b_down)


def _combine_kernel(dest_ref, dnext_ref, y_ref, h1_ref, pos_ref, gate_ref, g2_ref, lg_ref, lb_ref,
                    o_ref, sorted_ref, sems):
    i = pl.program_id(0)

    def chunk_copy(slot, table_ref, j):
        src = pl.multiple_of(table_ref[0, 0, j], SEG_ALIGN)
        return pltpu.make_async_copy(y_ref.at[pl.ds(src, SEG_ALIGN)],
                                     sorted_ref.at[slot, pl.ds(j * SEG_ALIGN, SEG_ALIGN)],
                                     sems.at[slot])

    def drain(slot):
        pltpu.make_async_copy(y_ref.at[pl.ds(0, CAP)], sorted_ref.at[slot], sems.at[slot]).wait()

    @pl.when(i == 0)
    def _():
        def issue(j, c):
            chunk_copy(0, dest_ref, j).start()
            return c
        lax.fori_loop(0, NCHK, issue, 0)

    def run(slot):
        drain(slot)

        pos = pos_ref[...]
        gate = gate_ref[...]
        m = jnp.zeros((TM, D), F32)
        n_cb = CAP // CAP_BLOCK
        per_cb = NCHK // (n_cb // 2)
        for cb in range(n_cb):
            for j in range(cb * per_cb, min((cb + 1) * per_cb, NCHK)):
                chunk_copy(1 - slot, dnext_ref, j).start()
            rws = lax.broadcasted_iota(jnp.int32, (CAP_BLOCK, TM), 0) + cb * CAP_BLOCK
            g = jnp.where(rws == pos[0:1], gate[0:1], 0.0)
            for k in range(1, TOP_K):
                g = g + jnp.where(rws == pos[k:k + 1], gate[k:k + 1], 0.0)
            rows = sorted_ref[slot, cb * CAP_BLOCK:(cb + 1) * CAP_BLOCK, :]
            m = m + lax.dot_general(g.astype(BF16), rows, (((0,), (0,)), ((), ())),
                                    preferred_element_type=F32)
        o_ref[...] = _layer_norm(ALPHA * h1_ref[...] + g2_ref[...] * m, lg_ref[...], lb_ref[...])

        @pl.when(i == N_TILES - 1)
        def _():
            drain(1 - slot)
    _on_parity(i, run)


def _combine(chunk_table, y_buf, h1, pos_t, gate_t, g2, lg, lb):
    vec = pl.BlockSpec((1, D), lambda i: (0, 0))
    return pl.pallas_call(
        _combine_kernel,
        grid_spec=pltpu.PrefetchScalarGridSpec(
            num_scalar_prefetch=0,
            grid=(N_TILES,),
            in_specs=[pl.BlockSpec((1, 1, NCHK), lambda i: (i + 1, 0, 0), memory_space=pltpu.SMEM),
                      pl.BlockSpec((1, 1, NCHK), lambda i: (i + 2, 0, 0), memory_space=pltpu.SMEM),
                      pl.BlockSpec(memory_space=pl.ANY),
                      pl.BlockSpec((TM, D), lambda i: (i, 0)),
                      pl.BlockSpec((TOP_K, TM), lambda i: (0, i)),
                      pl.BlockSpec((TOP_K, TM), lambda i: (0, i)),
                      vec, vec, vec],
            out_specs=pl.BlockSpec((TM, D), lambda i: (i, 0)),
            scratch_shapes=[pltpu.VMEM((2, CAP, D), BF16),
                            pltpu.SemaphoreType.DMA((2,))]),
        out_shape=jax.ShapeDtypeStruct((N_TOK, D), F32),
        compiler_params=_cparams(("arbitrary",)),
        name="combine",
    )(chunk_table, chunk_table, y_buf, h1, pos_t, gate_t, g2, lg, lb)


def _sincos_tables():
    q = D // 4
    omega = 1.0 / (10000.0 ** (np.arange(q) / q))

    def emb(n):
        ang = np.arange(n)[:, None] * omega[None, :]
        return jnp.asarray(np.concatenate([np.sin(ang), np.cos(ang)], axis=-1), F32)
    return emb(N_TOK // GRID_W), emb(GRID_W)


def kernel(x, c, ctx, c_ctx, ln_in_g, ln_in_b, w_ada, b_ada, w_in, b_in, s5_lambda_re, s5_lambda_im, s5_log_dt, s5_b_re, s5_b_im, s5_c_re, s5_c_im, s5_d, w_glu, b_glu, w_br_s5, w_br_fft, b_br_fft, w_out, b_out, ln1_g, ln1_b, w_router, b_router, w_up, b_up, w_down, b_down, ln2_g, ln2_b):
    assert x.shape == (1, N_TOK, D) and ctx.shape == (1, N_CTX, D) and w_ada.shape[0] == 1
    row = lambda v: v.reshape(1, -1).astype(F32)

    cc = jnp.concatenate([c.reshape(1, D), c_ctx.reshape(1, D), jnp.zeros((SUBLANES - 2, D), F32)], axis=0)
    ada = _ada(cc, w_ada[0], row(b_ada[0]))
    sh1, sc1, g1, sh2, sc2, g2 = (ada[0:1, k * D:(k + 1) * D] for k in range(6))
    sh1c, sc1c = ada[1:2, 0:D], ada[1:2, D:2 * D]

    emb_r, emb_c = _sincos_tables()
    st1, st2, fc = _dft_tables()
    lg, lb = row(ln_in_g), row(ln_in_b)

    w_in_bf = w_in[0].astype(BF16)
    b_s5 = row(b_in[0][:S5_WIDTH])
    b_fft8 = jnp.concatenate([row(b_in[0][S5_WIDTH:S5_WIDTH + FFT_WIDTH]),
                              jnp.zeros((SUBLANES - 1, FFT_WIDTH), F32)], axis=0)
    b_g = row(b_in[0][S5_WIDTH + FFT_WIDTH:])
    w_fc, b_fc = _fft_weights(w_in[0], b_fft8, fc)
    bcat = jnp.concatenate([b_s5, b_fc[0:1]], axis=1)

    x2 = x[0]
    p_t, xr, xi = _proj(x2, emb_r, emb_c, lg, lb, 1.0 + sc1, sh1, w_in_bf, w_fc, bcat)
    pc_t = _ctx_proj(ctx[0], lg, lb, 1.0 + sc1c, sh1c, w_in_bf[:, :S5_WIDTH], b_s5)

    b_c, a_q, a_p, trans, ctx_w = _s5_tables(
        s5_lambda_re[0], s5_lambda_im[0], s5_log_dt[0], s5_b_re[0], s5_b_im[0],
        s5_c_re[0], s5_c_im[0], s5_d[0])
    y_t, zeroed = _s5(p_t, pc_t, b_c, a_q, a_p, trans, ctx_w)

    yr, yi = _fft1(xr, xi, st1)
    zr = _fft2(yr, yi, st2)

    tri = jnp.asarray(np.arange(TM)[:, None] < np.arange(TM)[None, :], BF16)
    br = jnp.broadcast_to(b_router[0].reshape(N_EXPERTS, 1), (N_EXPERTS, LANES))
    etri = jnp.asarray(np.arange(N_EXPERTS)[:, None] > np.arange(N_EXPERTS)[None, :], BF16)
    wr_t = jnp.transpose(w_router[0])
    wr_hi = wr_t.astype(BF16)
    wr_split = jnp.stack([wr_hi, (wr_t - wr_hi.astype(F32)).astype(BF16)], axis=0)
    h1, u2, pos_t, gate_t, counts = _mix(
        x2, emb_r, emb_c, lg, lb, 1.0 + sc1, sh1, w_in_bf, b_g, y_t, zr,
        w_glu[0].astype(BF16), row(b_glu[0]), w_br_s5[0].astype(BF16), w_br_fft[0].astype(BF16),
        row(b_br_fft[0]), w_out[0].astype(BF16), row(b_out[0]), g1, row(ln1_g[0]), row(ln1_b[0]),
        1.0 + sc2, sh2, wr_split, br, tri, etri)

    cnt = counts[:, :, 0].astype(jnp.int32)
    seg = (cnt + SEG_ALIGN - 1) // SEG_ALIGN * SEG_ALIGN
    seg_end = jnp.cumsum(seg, axis=1)
    seg_start = seg_end - seg
    padded = (jnp.sum(seg, axis=0) + BM - 1) // BM * BM
    pad_ends = jnp.cumsum(padded)
    seg_dest = (pad_ends - padded)[None, :] + jnp.cumsum(seg, axis=0) - seg
    chunk_row = jnp.arange(NCHK, dtype=jnp.int32) * SEG_ALIGN
    chunk_exp = jnp.minimum(jnp.sum(chunk_row[None, :, None] >= seg_end[:, None, :], axis=-1),
                            N_EXPERTS - 1)
    own = chunk_exp[:, :, None] == jnp.arange(N_EXPERTS, dtype=jnp.int32)[None, None, :]
    chunk_dest = (jnp.sum(jnp.where(own, (seg_dest - seg_start)[:, None, :], 0), axis=-1)
                  + chunk_row[None, :]).astype(jnp.int32).reshape(N_TILES, 1, NCHK)
    nchk = (seg_end[:, -1] // SEG_ALIGN).astype(jnp.int32)
    block_start = jnp.arange(N_BLOCKS_ALL, dtype=jnp.int32) * BM
    block_expert = jnp.minimum(jnp.sum(block_start[:, None] >= pad_ends[None, :], axis=1),
                               N_EXPERTS - 1).astype(jnp.int32)
    n_used = (pad_ends[-1:] // BM).astype(jnp.int32)
    opens = (block_start < pad_ends[-1]) & (
        block_expert != jnp.concatenate([jnp.full((1,), -1, jnp.int32), block_expert[:-1]]))
    run_id = jnp.where(opens, jnp.cumsum(opens.astype(jnp.int32)) - 1, -1).astype(jnp.int32)
    experts = jnp.arange(N_EXPERTS, dtype=jnp.int32)
    later = (experts[None, :] > block_expert[:, None]) & (padded[None, :] > 0)
    next_expert = jnp.min(jnp.where(later, experts[None, :], N_EXPERTS), axis=1)
    next_expert = jnp.where(next_expert < N_EXPERTS, next_expert, -1).astype(jnp.int32)

    spare = (ROWS + chunk_row)[None, None, :]
    chunk_table = jnp.concatenate(
        [spare,
         jnp.where(chunk_row[None, None, :] < (nchk * SEG_ALIGN)[:, None, None], chunk_dest, spare),
         spare], axis=0).astype(jnp.int32)
    fill_ends = (pad_ends - padded + jnp.sum(seg, axis=0)).astype(jnp.int32)
    buf = _dispatch(chunk_table, pos_t, u2, zeroed)
    mine = block_expert[:, None] == experts[None, :]
    filled = jnp.sum(jnp.where(mine, fill_ends[None, :], 0), axis=1)
    valid = jnp.clip(filled - block_start, 0, BM).astype(jnp.int32)
    y_buf = _ffn(block_expert, n_used, run_id, next_expert, valid, buf, w_up[0],
                 b_up[0], w_down[0], b_down[0])
    out = _combine(chunk_table, y_buf, h1, pos_t, gate_t, g2, row(ln2_g[0]), row(ln2_b[0]))
    return out.reshape(1, N_TOK, D)
```

```python
import functools
import math

import jax
import jax.numpy as jnp
import numpy as np
from jax import lax
from jax.experimental import pallas as pl
from jax.experimental.pallas import tpu as pltpu

F32 = jnp.float32
BF16 = jnp.bfloat16
HI = lax.Precision.HIGHEST

D = 1024
N_TOK = 16384
N_CTX = 256
GRID_W = 64
S5_GROUP = 16
S5_GROUPS = 32
S5_STATE = 64
S5_WIDTH = 512
FFT_GROUPS = 4
FFT_DIM = 128
FFT_WIDTH = 512
N_EXPERTS = 32
TOP_K = 4
LN_EPS = 1e-5
ALPHA = 2.0 ** 0.25
SWIGLU_ALPHA = 1.702
SWIGLU_LIMIT = 7.0

LANES = 128
SUBLANES = 8
VMEM_LIMIT = 56 * 1024 * 1024

CH = 8
N_CHUNK = N_TOK // CH
N_CHUNK_CTX = N_CTX // CH
NSEG = SUBLANES
SEG = N_CHUNK // NSEG
SCAN_UNROLL = 8
GPT = LANES // S5_GROUP
NJ = S5_WIDTH // LANES
CL = CH * LANES
SW = 4 * GPT * S5_STATE

FN = 128
FB = 16

TM = 512
TM_PROJ = 1024
N_TILES = N_TOK // TM
BM = 1024
FFN_HALF = 512
FFN_TAIL = 256
N_SLOTS = N_TOK * TOP_K
SEG_ALIGN = 2 * SUBLANES
CAP_BLOCK = 256
SORT_BLOCK = 128
CAP = -(-(TOP_K * TM + N_EXPERTS * (SEG_ALIGN - 1)) // CAP_BLOCK) * CAP_BLOCK
NCHK = CAP // SEG_ALIGN
N_BLOCKS = -(-(N_SLOTS + N_TILES * N_EXPERTS * (SEG_ALIGN - 1)) // BM) + N_EXPERTS
ROWS = N_BLOCKS * BM
N_BLOCKS_ALL = N_BLOCKS + -(-CAP // BM)
ROWS_ALL = N_BLOCKS_ALL * BM
ZERO_UNIT = 512
ZERO_PARTS = ROWS_ALL // (NJ * ZERO_UNIT)
assert ZERO_UNIT * ZERO_PARTS * NJ == ROWS_ALL


def _cparams(sem):
    return pltpu.CompilerParams(dimension_semantics=sem, vmem_limit_bytes=VMEM_LIMIT)


def _layer_norm(x, g, b):
    mu = jnp.mean(x, axis=-1, keepdims=True)
    xc = x - mu
    var = jnp.mean(xc * xc, axis=-1, keepdims=True)
    return xc * lax.rsqrt(var + LN_EPS) * g + b


def _sigmoid(x):
    return 0.5 * jnp.tanh(0.5 * x) + 0.5


def _ada_kernel(c_ref, w_ref, b_ref, o_ref):
    c = c_ref[...]
    s = c * _sigmoid(c)
    w = w_ref[...]
    s_hi, w_hi = s.astype(BF16), w.astype(BF16)
    s_lo = (s - s_hi.astype(F32)).astype(BF16)
    w_lo = (w - w_hi.astype(F32)).astype(BF16)

    def mm(a, b):
        return jnp.dot(a, b, preferred_element_type=F32)
    o_ref[...] = mm(s_hi, w_hi) + mm(s_lo, w_hi) + mm(s_hi, w_lo) + b_ref[...]


def _ada(cc, w_ada, b_ada):
    nb = 4
    wb = 6 * D // nb
    return pl.pallas_call(
        _ada_kernel,
        grid=(nb,),
        in_specs=[pl.BlockSpec((SUBLANES, D), lambda i: (0, 0)),
                  pl.BlockSpec((D, wb), lambda i: (0, i)),
                  pl.BlockSpec((1, wb), lambda i: (0, i))],
        out_specs=pl.BlockSpec((SUBLANES, wb), lambda i: (0, i)),
        out_shape=jax.ShapeDtypeStruct((SUBLANES, 6 * D), F32),
        compiler_params=_cparams(("parallel",)),
        name="ada",
    )(cc, w_ada, b_ada)


def _fftw_kernel(w_ref, b_ref, f_ref, wo_ref, bo_ref):
    f = f_ref[...]
    w = w_ref[...]
    w_hi, f_hi = w.astype(BF16), f.astype(BF16)
    w_lo = (w - w_hi.astype(F32)).astype(BF16)
    f_lo = (f - f_hi.astype(F32)).astype(BF16)

    def mm(a, b):
        return jnp.dot(a, b, preferred_element_type=F32)
    wo_ref[...] = (mm(w_hi, f_hi) + mm(w_lo, f_hi) + mm(w_hi, f_lo)).astype(BF16)
    bo_ref[...] = jnp.dot(b_ref[...], f, preferred_element_type=F32, precision=HI)


def _fft_weights(w_in, b_fft8, fc):
    def full(a):
        return pl.BlockSpec(a.shape, lambda i: (0,) * a.ndim)
    outs = (jax.ShapeDtypeStruct((D, 2 * FFT_WIDTH), BF16),
            jax.ShapeDtypeStruct((SUBLANES, 2 * FFT_WIDTH), F32))
    return pl.pallas_call(
        _fftw_kernel,
        grid=(1,),
        in_specs=[pl.BlockSpec((D, FFT_WIDTH), lambda i: (0, S5_WIDTH // FFT_WIDTH)),
                  full(b_fft8), full(fc)],
        out_specs=tuple(pl.BlockSpec(o.shape, lambda i: (0, 0)) for o in outs),
        out_shape=outs,
        compiler_params=_cparams(("arbitrary",)),
        name="fftw",
    )(w_in, b_fft8, fc)


def _pos_code(er_ref, ec_ref, tm):
    nr = tm // GRID_W
    er = er_ref[...]
    row = jnp.broadcast_to(er[:, None, :], (nr, GRID_W, D // 2)).reshape(tm, D // 2)
    col = jnp.concatenate([ec_ref[...]] * nr, axis=0)
    return jnp.concatenate([row, col], axis=-1)


def _to_chunk_major(val, scr_ref, out_ref, tm):
    for j in range(NJ):
        scr_ref[j] = val[:, j * LANES:(j + 1) * LANES]
    for t in range(CH):
        for j in range(NJ):
            piece = scr_ref[j, pl.ds(t, tm // CH, stride=CH), :]
            out_ref[t, :, j * LANES:(j + 1) * LANES] = piece.astype(out_ref.dtype)


def _proj_kernel(x_ref, er_ref, ec_ref, lg_ref, lb_ref, m_ref, s_ref, ws_ref, wf_ref, b_ref,
                 p_ref, xr_ref, xi_ref, scr_ref):
    x = x_ref[...] + _pos_code(er_ref, ec_ref, TM_PROJ)
    h = _layer_norm(x, lg_ref[...], lb_ref[...])
    u = (h * m_ref[...] + s_ref[...]).astype(BF16)
    p_s5 = jnp.dot(u, ws_ref[...], preferred_element_type=F32) + b_ref[:, :S5_WIDTH]
    _to_chunk_major(p_s5, scr_ref, p_ref, TM_PROJ)
    p_f = jnp.dot(u, wf_ref[...], preferred_element_type=F32) + b_ref[:, S5_WIDTH:]
    xr_ref[...] = p_f[:, :FFT_WIDTH].astype(BF16)
    xi_ref[...] = p_f[:, FFT_WIDTH:].astype(BF16)


def _proj(x, emb_r, emb_c, lg, lb, m1, s1, w_in, w_fc, bcat):
    nw = bcat.shape[1]
    vec = pl.BlockSpec((1, D), lambda i: (0, 0))
    return pl.pallas_call(
        _proj_kernel,
        grid=(N_TOK // TM_PROJ,),
        in_specs=[pl.BlockSpec((TM_PROJ, D), lambda i: (i, 0)),
                  pl.BlockSpec((TM_PROJ // GRID_W, D // 2), lambda i: (i, 0)),
                  pl.BlockSpec((GRID_W, D // 2), lambda i: (0, 0)),
                  vec, vec, vec, vec,
                  pl.BlockSpec((D, S5_WIDTH), lambda i: (0, 0)),
                  pl.BlockSpec((D, 2 * FFT_WIDTH), lambda i: (0, 0)),
                  pl.BlockSpec((1, nw), lambda i: (0, 0))],
        out_specs=(pl.BlockSpec((CH, TM_PROJ // CH, S5_WIDTH), lambda i: (0, i, 0)),
                   pl.BlockSpec((TM_PROJ, FFT_WIDTH), lambda i: (i, 0)),
                   pl.BlockSpec((TM_PROJ, FFT_WIDTH), lambda i: (i, 0))),
        out_shape=(jax.ShapeDtypeStruct((CH, N_CHUNK, S5_WIDTH), BF16),
                   jax.ShapeDtypeStruct((N_TOK, FFT_WIDTH), BF16),
                   jax.ShapeDtypeStruct((N_TOK, FFT_WIDTH), BF16)),
        scratch_shapes=[pltpu.VMEM((NJ, TM_PROJ, LANES), F32)],
        compiler_params=_cparams(("parallel",)),
        name="proj",
    )(x, emb_r, emb_c, lg, lb, m1, s1, w_in, w_fc, bcat)


def _ctx_proj_kernel(x_ref, lg_ref, lb_ref, m_ref, s_ref, w_ref, b_ref, p_ref, scr_ref):
    h = _layer_norm(x_ref[...], lg_ref[...], lb_ref[...])
    u = (h * m_ref[...] + s_ref[...]).astype(BF16)
    p = jnp.dot(u, w_ref[...], preferred_element_type=F32) + b_ref[...]
    _to_chunk_major(p, scr_ref, p_ref, N_CTX)


def _ctx_proj(ctx, lg, lb, m1, s1, w_s5, b_s5):
    return pl.pallas_call(
        _ctx_proj_kernel,
        out_shape=jax.ShapeDtypeStruct((CH, N_CHUNK_CTX, S5_WIDTH), BF16),
        scratch_shapes=[pltpu.VMEM((NJ, N_CTX, LANES), F32)],
        compiler_params=pltpu.CompilerParams(vmem_limit_bytes=VMEM_LIMIT),
        name="ctxproj",
    )(ctx, lg, lb, m1, s1, w_s5, b_s5)


def _s5_tables(lam_re, lam_im, log_dt, b_re, b_im, c_re, c_im, d_skip):
    dt = jnp.exp(log_dt)[..., None]
    zr = lam_re * dt
    zi = lam_im * dt

    def apow(m):
        m = jnp.asarray(m, F32)
        mag = jnp.exp(zr[..., None] * m)
        return mag * jnp.cos(zi[..., None] * m), mag * jnp.sin(zi[..., None] * m)

    a_re, a_im = apow(jnp.ones((1,), F32))
    a_re, a_im = a_re[..., 0], a_im[..., 0]
    den = lam_re * lam_re + lam_im * lam_im
    num_re = a_re - 1.0
    k_re = (num_re * lam_re + a_im * lam_im) / den
    k_im = (a_im * lam_re - num_re * lam_im) / den
    bb_re = k_re[..., None] * b_re - k_im[..., None] * b_im
    bb_im = k_re[..., None] * b_im + k_im[..., None] * b_re

    ks = jnp.arange(CH + 1, dtype=F32)
    pw_re, pw_im = apow(ks)
    kmag = jnp.exp(zr[:, :, None, :] * ks[None, None, :, None])
    pk_re = kmag * jnp.cos(zi[:, :, None, :] * ks[None, None, :, None])
    pk_im = kmag * jnp.sin(zi[:, :, None, :] * ks[None, None, :, None])
    bt_re, bt_im = jnp.swapaxes(b_re, 2, 3), jnp.swapaxes(b_im, 2, 3)
    bbt_re = k_re[:, :, None, :] * bt_re - k_im[:, :, None, :] * bt_im
    bbt_im = k_re[:, :, None, :] * bt_im + k_im[:, :, None, :] * bt_re

    ar, ai = pw_re[:, :, :, :CH, None], pw_im[:, :, :, :CH, None]
    cr = jnp.swapaxes(c_re, 2, 3)[:, :, :, None, :]
    ci = jnp.swapaxes(c_im, 2, 3)[:, :, :, None, :]
    ca = jnp.concatenate([cr * ar - ci * ai, -(cr * ai + ci * ar)], axis=2)
    ca = ca.reshape(2, S5_GROUPS, 2 * S5_STATE, CH * S5_GROUP)
    bbt = jnp.concatenate([bbt_re, bbt_im], axis=-1)
    taps = jnp.einsum('dghq,dgqn->dghn', bbt, ca, precision=HI)
    skip = (d_skip.reshape(S5_GROUPS, S5_GROUP, 1) * jnp.eye(S5_GROUP, dtype=F32)[None])
    taps = taps.at[0, :, :, :S5_GROUP].add(skip)
    b_c = jnp.transpose(taps.reshape(2, NJ, GPT, S5_GROUP, CH, S5_GROUP), (1, 4, 0, 2, 3, 5))
    b_c = b_c.reshape(NJ, 2 * CH, LANES, S5_GROUP)

    ef = (CH - 1) - jnp.arange(CH)
    eb = jnp.arange(CH)

    def q_part(d, e):
        pr = pk_re[d][:, e, None, :]
        pi = pk_im[d][:, e, None, :]
        br = bbt_re[d][:, None, :, :]
        bi = bbt_im[d][:, None, :, :]
        return pr * br - pi * bi, pr * bi + pi * br

    def q_rows(v):
        v = v.reshape(NJ, GPT, CH, S5_GROUP, S5_STATE)
        return jnp.transpose(v, (0, 2, 1, 3, 4)).reshape(NJ, CL, S5_STATE)
    a_q = jnp.stack([q_rows(v) for v in q_part(0, ef) + q_part(1, eb)], axis=0)

    of = jnp.arange(CH) + 1
    ob = CH - jnp.arange(CH)

    def p_part(d, e):
        pr = pw_re[d][..., e][:, :, :, None]
        pi = pw_im[d][..., e][:, :, :, None]
        return (ct_re[d] * pr - ct_im[d] * pi, -(ct_re[d] * pi + ct_im[d] * pr))
    ct_re = jnp.swapaxes(c_re, 2, 3)[:, :, :, None, :]
    ct_im = jnp.swapaxes(c_im, 2, 3)[:, :, :, None, :]
    a_p = jnp.stack([v.reshape(NJ, GPT * S5_STATE, CH * S5_GROUP)
                     for v in p_part(0, of) + p_part(1, ob)], axis=0)

    def lanes(v):
        return jnp.transpose(v.reshape(2, NJ, GPT * S5_STATE), (1, 0, 2))
    c_r, c_i = apow(jnp.full((1,), float(CH), F32))
    s_r, s_i = apow(jnp.full((1,), float(CH * SEG), F32))
    cr, ci, sr, si = (lanes(v[..., 0]) for v in (c_r, c_i, s_r, s_i))
    trans = jnp.stack([cr[:, 0], ci[:, 0], cr[:, 1], ci[:, 1],
                       sr[:, 0], si[:, 0], sr[:, 1], si[:, 1]], axis=1)

    cidx = jnp.arange(N_CHUNK_CTX, dtype=F32)
    wf_r, wf_i = apow(CH * (N_CHUNK_CTX - 1 - cidx))
    wb_r, wb_i = apow(CH * cidx)

    def ctx_lanes(v, d):
        return jnp.transpose(v[d].reshape(NJ, GPT * S5_STATE, N_CHUNK_CTX), (0, 2, 1))
    ctx_w = jnp.stack([ctx_lanes(wf_r, 0), ctx_lanes(wf_i, 0),
                       ctx_lanes(wb_r, 1), ctx_lanes(wb_i, 1)], axis=1)
    return b_c, a_q, a_p, trans, ctx_w


def _build_operators(bc_ref, aq_ref, ap_ref, c16_ref, c64_ref, cm_ref, wm_ref, wq_ref, wp_ref):
    def expand(a, c, row_shift, col_shift):
        w = jnp.dot(a, c, preferred_element_type=F32)
        rg = (lax.broadcasted_iota(jnp.int32, (w.shape[0], 1), 0) >> row_shift) & (GPT - 1)
        cg = (lax.broadcasted_iota(jnp.int32, (1, w.shape[1]), 1) >> col_shift) & (GPT - 1)
        return jnp.where(rg == cg, w, 0.0)

    blk = [expand(bc_ref[0, kd], c16_ref[...], 4, 4) for kd in range(2 * CH)]
    for t in range(CH):
        for u in range(CH):
            b = blk[2 * (u - t)] if u > t else blk[2 * (t - u) + 1] if u < t else blk[0] + blk[1]
            wm_ref[0, t * LANES:(t + 1) * LANES, u * LANES:(u + 1) * LANES] = b.astype(BF16)
    half = GPT * S5_STATE
    for s in range(4):
        wq_ref[0, :, s * half:(s + 1) * half] = expand(aq_ref[s, 0], c64_ref[...], 4, 6).astype(BF16)
        wp_ref[0, s * half:(s + 1) * half, :] = expand(ap_ref[s, 0], cm_ref[...], 6, 4).astype(BF16)


def _s5_kernel(p_ref, pc_ref, bc_ref, aq_ref, ap_ref, c16_ref, c64_ref, cm_ref, tr_ref, cw_ref,
               y_ref, rows_ref, v_ref, wm_ref, wq_ref, wp_ref, zero_ref, zsem):
    nq = NJ
    half = GPT * S5_STATE

    zero_ref[...] = jnp.zeros_like(zero_ref)

    def clears(fn):
        def unit(b, c):
            start = pl.multiple_of((pl.program_id(0) * ZERO_PARTS + b) * ZERO_UNIT, ZERO_UNIT)
            fn(pltpu.make_async_copy(zero_ref, rows_ref.at[pl.ds(start, ZERO_UNIT)], zsem))
            return c
        lax.fori_loop(0, ZERO_PARTS, unit, 0)
    clears(lambda cp: cp.start())
    _build_operators(bc_ref, aq_ref, ap_ref, c16_ref, c64_ref, cm_ref, wm_ref, wq_ref, wp_ref)

    def chunk_rows(ref, r0, nrows):
        return jnp.concatenate([ref[t, pl.ds(r0, nrows), :] for t in range(CH)], axis=-1)

    def fill(k, c):
        r0 = pl.multiple_of(k * SEG, SEG)
        v = jnp.dot(chunk_rows(p_ref, r0, SEG), wq_ref[0], preferred_element_type=F32)
        for s in range(4 * nq):
            v_ref[s, pl.ds(k, SEG, stride=NSEG), :] = v[:, s * LANES:(s + 1) * LANES]
        return c
    lax.fori_loop(0, NSEG, fill, 0)

    vc = jnp.dot(chunk_rows(pc_ref, 0, N_CHUNK_CTX), wq_ref[0], preferred_element_type=F32)
    vfr, vfi, vbr, vbi = (vc[:, i * half:(i + 1) * half] for i in range(4))
    wfr, wfi, wbr, wbi = (cw_ref[0, i] for i in range(4))
    s0_fr = jnp.sum(wfr * vfr - wfi * vfi, axis=0, keepdims=True)
    s0_fi = jnp.sum(wfr * vfi + wfi * vfr, axis=0, keepdims=True)
    s0_br = jnp.sum(wbr * vbr - wbi * vbi, axis=0, keepdims=True)
    s0_bi = jnp.sum(wbr * vbi + wbi * vbr, axis=0, keepdims=True)

    tr = tr_ref[0]
    afr, afi, abr, abi = (jnp.broadcast_to(tr[i:i + 1], (NSEG, half)) for i in range(4))
    gfr, gfi, gbr, gbi = (tr[i:i + 1] for i in range(4, 8))

    def load_part(part, i):
        return jnp.concatenate(
            [v_ref[part * nq + q, pl.ds(pl.multiple_of(i * NSEG, NSEG), NSEG), :] for q in range(nq)],
            axis=-1)

    def store_part(part, i, val):
        for q in range(nq):
            v_ref[part * nq + q, pl.ds(pl.multiple_of(i * NSEG, NSEG), NSEG), :] = (
                val[:, q * LANES:(q + 1) * LANES])

    def step(i, carry, write):
        fr, fi, br, bi = carry
        ib = SEG - 1 - i
        ufr, ufi = load_part(0, i), load_part(1, i)
        ubr, ubi = load_part(2, ib), load_part(3, ib)
        if write:
            store_part(0, i, fr)
            store_part(1, i, fi)
            store_part(2, ib, br)
            store_part(3, ib, bi)
        return (afr * fr - afi * fi + ufr, afr * fi + afi * fr + ufi,
                abr * br - abi * bi + ubr, abr * bi + abi * br + ubi)

    zero = jnp.zeros((NSEG, half), F32)
    ffr, ffi, fbr, fbi = lax.fori_loop(0, SEG, functools.partial(step, write=False),
                                       (zero, zero, zero, zero), unroll=SCAN_UNROLL)

    rows_fr, rows_fi = [s0_fr], [s0_fi]
    for k in range(1, NSEG):
        pr, pi = rows_fr[-1], rows_fi[-1]
        rows_fr.append(gfr * pr - gfi * pi + ffr[k - 1:k])
        rows_fi.append(gfr * pi + gfi * pr + ffi[k - 1:k])
    rows_br, rows_bi = [s0_br], [s0_bi]
    for k in range(NSEG - 2, -1, -1):
        pr, pi = rows_br[0], rows_bi[0]
        rows_br.insert(0, gbr * pr - gbi * pi + fbr[k + 1:k + 2])
        rows_bi.insert(0, gbr * pi + gbi * pr + fbi[k + 1:k + 2])
    init = tuple(jnp.concatenate(r, axis=0) for r in (rows_fr, rows_fi, rows_br, rows_bi))

    lax.fori_loop(0, SEG, functools.partial(step, write=True), init, unroll=SCAN_UNROLL)

    def emit(k, c):
        r0 = pl.multiple_of(k * SEG, SEG)
        b = chunk_rows(p_ref, r0, SEG)
        sin = jnp.concatenate([v_ref[s, pl.ds(k, SEG, stride=NSEG), :] for s in range(4 * nq)], axis=-1)
        y = (jnp.dot(b, wm_ref[0], preferred_element_type=F32)
             + jnp.dot(sin.astype(BF16), wp_ref[0], preferred_element_type=F32))
        for t in range(CH):
            y_ref[t, pl.ds(r0, SEG), :] = y[:, t * LANES:(t + 1) * LANES].astype(y_ref.dtype)
        return c
    lax.fori_loop(0, NSEG, emit, 0)
    clears(lambda cp: cp.wait())


def _s5(p_t, pc_t, b_c, a_q, a_p, trans, ctx_w):
    rep = np.ones((1, GPT))
    c16 = jnp.asarray(np.kron(rep, np.eye(S5_GROUP)), F32).astype(BF16)
    c64 = jnp.asarray(np.kron(rep, np.eye(S5_STATE)), F32).astype(BF16)
    c_m = jnp.asarray(np.kron(np.eye(CH), np.kron(rep, np.eye(S5_GROUP))), F32).astype(BF16)
    b_c, a_q, a_p = b_c.astype(BF16), a_q.astype(BF16), a_p.astype(BF16)
    half = GPT * S5_STATE
    return pl.pallas_call(
        _s5_kernel,
        grid=(NJ,),
        in_specs=[pl.BlockSpec((CH, N_CHUNK, LANES), lambda j: (0, 0, j)),
                  pl.BlockSpec((CH, N_CHUNK_CTX, LANES), lambda j: (0, 0, j)),
                  pl.BlockSpec((1, 2 * CH, LANES, S5_GROUP), lambda j: (j, 0, 0, 0)),
                  pl.BlockSpec((4, 1, CL, S5_STATE), lambda j: (0, j, 0, 0)),
                  pl.BlockSpec((4, 1, half, CH * S5_GROUP), lambda j: (0, j, 0, 0)),
                  pl.BlockSpec(c16.shape, lambda j: (0, 0)),
                  pl.BlockSpec(c64.shape, lambda j: (0, 0)),
                  pl.BlockSpec(c_m.shape, lambda j: (0, 0)),
                  pl.BlockSpec((1, SUBLANES, half), lambda j: (j, 0, 0)),
                  pl.BlockSpec((1, 4, N_CHUNK_CTX, half), lambda j: (j, 0, 0, 0))],
        out_specs=(pl.BlockSpec((CH, N_CHUNK, LANES), lambda j: (0, 0, j)),
                   pl.BlockSpec(memory_space=pl.ANY)),
        out_shape=(jax.ShapeDtypeStruct((CH, N_CHUNK, S5_WIDTH), BF16),
                   jax.ShapeDtypeStruct((ROWS_ALL, D), BF16)),
        scratch_shapes=[pltpu.VMEM((4 * NJ, N_CHUNK, LANES), F32),
                        pltpu.VMEM((1, CL, CL), BF16),
                        pltpu.VMEM((1, CL, SW), BF16),
                        pltpu.VMEM((1, SW, CL), BF16),
                        pltpu.VMEM((ZERO_UNIT, D), BF16),
                        pltpu.SemaphoreType.DMA(())],
        compiler_params=_cparams(("parallel",)),
        name="s5",
    )(p_t, pc_t, b_c, a_q, a_p, c16, c64, c_m, trans, ctx_w)


def _dft_tables():
    n = np.arange(FN)
    ang = 2.0 * np.pi * np.outer(n, n) / FN
    c, s = np.cos(ang), np.sin(ang)
    st1 = np.block([[c, s], [-s, c]])
    tw = 2.0 * np.pi * np.outer(n, n) / (FN * FN)
    wr, wi = np.cos(tw), -np.sin(tw)
    fr = c[None] * wr[:, None, :] + s[None] * wi[:, None, :]
    fi = c[None] * wi[:, None, :] - s[None] * wr[:, None, :]
    st2 = np.concatenate([fr, -fi], axis=-1)
    scale = 1.0 / math.sqrt(N_TOK * FFT_DIM)
    blk_c = np.kron(np.eye(FFT_GROUPS), c) * scale
    blk_s = np.kron(np.eye(FFT_GROUPS), s) * scale
    fc = np.concatenate([blk_c, -blk_s], axis=1)
    return (jnp.asarray(st1, F32).astype(BF16), jnp.asarray(st2, F32).astype(BF16), jnp.asarray(fc, F32))


FSL = FFT_WIDTH // LANES


FBH = FB // SUBLANES


def _block_to_slabs(blk, slab_ref, first, per_half):
    for bh in range(FBH):
        val = blk[:, bh * SUBLANES:(bh + 1) * SUBLANES, :].reshape(FN * SUBLANES, FFT_WIDTH)
        for s in range(FSL):
            slab_ref[bh * per_half + first + s] = val[:, s * LANES:(s + 1) * LANES]


def _slab_rows(b, first, per_half):
    return (b // SUBLANES) * per_half + first, pl.ds(b % SUBLANES, FN, stride=SUBLANES)


def _slabs_to_block(slab_ref, first, per_half):
    halves = []
    for bh in range(FBH):
        val = jnp.concatenate([slab_ref[bh * per_half + first + s] for s in range(FSL)], axis=-1)
        halves.append(val.reshape(FN, SUBLANES, FFT_WIDTH))
    return jnp.concatenate(halves, axis=1)


def _fft1_kernel(xr_ref, xi_ref, f_ref, yr_ref, yi_ref, in_ref, out_ref):
    _block_to_slabs(xr_ref[...].astype(F32), in_ref, 0, 2 * FSL)
    _block_to_slabs(xi_ref[...].astype(F32), in_ref, FSL, 2 * FSL)
    for b in range(FB):
        def part(first):
            base, rows = _slab_rows(b, first, 2 * FSL)
            return jnp.concatenate([in_ref[base + s, rows, :] for s in range(FSL)], axis=-1)
        xs = jnp.concatenate([part(0), part(FSL)], axis=0).astype(BF16)
        y = jnp.dot(f_ref[...], xs, preferred_element_type=F32)
        base, rows = _slab_rows(b, 0, 2 * FSL)
        for s in range(FSL):
            out_ref[base + s, rows, :] = y[:FN, s * LANES:(s + 1) * LANES]
            out_ref[base + FSL + s, rows, :] = y[FN:, s * LANES:(s + 1) * LANES]
    yr_ref[...] = _slabs_to_block(out_ref, 0, 2 * FSL).astype(BF16)
    yi_ref[...] = _slabs_to_block(out_ref, FSL, 2 * FSL).astype(BF16)


def _fft1(xr, xi, st1):
    spec = pl.BlockSpec((FN, FB, FFT_WIDTH), lambda i: (0, i, 0))
    slabs = pltpu.VMEM((FBH * 2 * FSL, FN * SUBLANES, LANES), F32)
    return pl.pallas_call(
        _fft1_kernel,
        grid=(FN // FB,),
        in_specs=[spec, spec, pl.BlockSpec((2 * FN, 2 * FN), lambda i: (0, 0))],
        out_specs=(spec, spec),
        out_shape=(jax.ShapeDtypeStruct((FN, FN, FFT_WIDTH), BF16),) * 2,
        scratch_shapes=[slabs, slabs],
        compiler_params=_cparams(("parallel",)),
        name="fft1",
    )(xr.reshape(FN, FN, FFT_WIDTH), xi.reshape(FN, FN, FFT_WIDTH), st1)


def _fft2_kernel(yr_ref, yi_ref, f_ref, z_ref, out_ref):
    for b in range(FB):
        ys = jnp.concatenate([yr_ref[b * FN:(b + 1) * FN, :], yi_ref[b * FN:(b + 1) * FN, :]], axis=0)
        z = jnp.dot(f_ref[b], ys, preferred_element_type=F32)
        base, rows = _slab_rows(b, 0, FSL)
        for s in range(FSL):
            out_ref[base + s, rows, :] = z[:, s * LANES:(s + 1) * LANES]
    z_ref[...] = _slabs_to_block(out_ref, 0, FSL).astype(BF16)


def _fft2(yr, yi, st2):
    rows = pl.BlockSpec((FB * FN, FFT_WIDTH), lambda i: (i, 0))
    z = pl.pallas_call(
        _fft2_kernel,
        grid=(FN // FB,),
        in_specs=[rows, rows, pl.BlockSpec((FB, FN, 2 * FN), lambda i: (i, 0, 0))],
        out_specs=pl.BlockSpec((FN, FB, FFT_WIDTH), lambda i: (0, i, 0)),
        out_shape=jax.ShapeDtypeStruct((FN, FN, FFT_WIDTH), BF16),
        scratch_shapes=[pltpu.VMEM((FBH * FSL, FN * SUBLANES, LANES), F32)],
        compiler_params=_cparams(("parallel",)),
        name="fft2",
    )(yr.reshape(N_TOK, FFT_WIDTH), yi.reshape(N_TOK, FFT_WIDTH), st2)
    return z.reshape(N_TOK, FFT_WIDTH)


def _gelu_tanh(x):
    return 0.5 * x * (1.0 + jnp.tanh(math.sqrt(2.0 / math.pi) * (x + 0.044715 * (x * x * x))))


def _mix_kernel(x_ref, er_ref, ec_ref, lg_ref, lb_ref, m1_ref, s1_ref, wgs_ref, wgf_ref, bg_ref,
                yt_ref, zr_ref, wglu_ref, bglu_ref, wbs_ref, wbf_ref, bbf_ref, wo_ref, bo_ref,
                g1_ref, l1g_ref, l1b_ref, m2_ref, s2_ref, wr_ref, br_ref, tri_ref, etri_ref,
                h1_ref, u2_ref, pos_ref, gate_ref, cnt_ref, scr_ref):
    pos = _pos_code(er_ref, ec_ref, TM)

    def front(r0, nr):
        rows = slice(r0, r0 + nr)
        h = _layer_norm(x_ref[rows, :] + pos[rows, :], lg_ref[...], lb_ref[...])
        u = (h * m1_ref[...] + s1_ref[...]).astype(BF16)

        c0, nc = r0 // CH, nr // CH
        for t in range(CH):
            for j in range(NJ):
                scr_ref[j, pl.ds(r0 + t, nc, stride=CH), :] = (
                    yt_ref[t, c0:c0 + nc, j * LANES:(j + 1) * LANES].astype(F32))
        ys = jnp.concatenate([scr_ref[j, rows, :] for j in range(NJ)], axis=-1)
        z = jnp.dot(_gelu_tanh(ys).astype(BF16), wglu_ref[...], preferred_element_type=F32) + bglu_ref[...]
        glu = (z[:, :S5_WIDTH] * _sigmoid(z[:, S5_WIDTH:])).astype(BF16)
        g_s5 = _sigmoid(jnp.dot(u, wgs_ref[...], preferred_element_type=F32) + bg_ref[:, :D])
        g_fft = _sigmoid(jnp.dot(u, wgf_ref[...], preferred_element_type=F32) + bg_ref[:, D:])
        y_s5 = jnp.dot(glu, wbs_ref[...], preferred_element_type=F32)
        y_fft = jnp.dot(zr_ref[rows, :], wbf_ref[...], preferred_element_type=F32) + bbf_ref[...]
        mixed = (g_s5 * y_s5 + g_fft * y_fft).astype(BF16)
        y = jnp.dot(mixed, wo_ref[...], preferred_element_type=F32) + bo_ref[...]
        h1 = _layer_norm(ALPHA * h + g1_ref[...] * y, l1g_ref[...], l1b_ref[...])
        h1_ref[rows, :] = h1
        u2 = h1 * m2_ref[...] + s2_ref[...]
        u2_ref[rows, :] = u2.astype(BF16)
        u_hi = u2.astype(BF16)
        u_lo = (u2 - u_hi.astype(F32)).astype(BF16)

        def nt(a, b):
            return lax.dot_general(a, b, (((1,), (1,)), ((), ())), preferred_element_type=F32)
        return nt(wr_ref[0], u_hi) + nt(wr_ref[0], u_lo) + nt(wr_ref[1], u_hi)

    logits = front(0, TM) + br_ref[:, 0:1]
    eidx = lax.broadcasted_iota(jnp.int32, (N_EXPERTS, TM), 0)
    vals, hots = [], []
    cur = logits
    for _k in range(TOP_K):
        m = jnp.max(cur, axis=0, keepdims=True)
        sel = jnp.min(jnp.where(cur == m, eidx, N_EXPERTS), axis=0, keepdims=True)
        hot = eidx == sel
        cur = jnp.where(hot, -jnp.inf, cur)
        vals.append(m)
        hots.append(hot)
    exps = [jnp.exp(v - vals[0]) for v in vals]
    den = exps[0] + exps[1] + exps[2] + exps[3]
    gate4 = jnp.concatenate([e / den for e in exps], axis=0)

    hot_sum = (hots[0] | hots[1] | hots[2] | hots[3]).astype(F32)
    before = jnp.dot(hot_sum.astype(BF16), tri_ref[...], preferred_element_type=F32)
    cnt = jnp.broadcast_to(jnp.sum(hot_sum, axis=1, keepdims=True), (N_EXPERTS, LANES))
    cnt8 = jnp.floor((cnt + (SEG_ALIGN - 1)) * (1.0 / SEG_ALIGN)) * SEG_ALIGN
    seg0 = jnp.dot(etri_ref[...], cnt8.astype(BF16), preferred_element_type=F32)
    tot = seg0[:, 0:1] + before
    pos4 = jnp.concatenate(
        [jnp.sum(jnp.where(hk, tot, 0.0), axis=0, keepdims=True) for hk in hots], axis=0)
    pos_ref[...] = pos4.astype(jnp.int32)
    cnt_ref[0] = cnt

    gate_ref[...] = gate4


def _mix(x, emb_r, emb_c, lg, lb, m1, s1, w_in, bg, y_t, zr, wglu, bglu, wbs, wbf, bbf, wo, bo,
         g1, l1g, l1b, m2, s2, wr_t, br, tri, etri):
    gate_cols = (S5_WIDTH + FFT_WIDTH) // D
    vec = pl.BlockSpec((1, D), lambda i: (0, 0))

    def full(a):
        return pl.BlockSpec(a.shape, lambda i: (0,) * a.ndim)
    return pl.pallas_call(
        _mix_kernel,
        grid=(N_TOK // TM,),
        in_specs=[pl.BlockSpec((TM, D), lambda i: (i, 0)),
                  pl.BlockSpec((TM // GRID_W, D // 2), lambda i: (i, 0)),
                  pl.BlockSpec((GRID_W, D // 2), lambda i: (0, 0)),
                  vec, vec, vec, vec,
                  pl.BlockSpec((D, D), lambda i: (0, gate_cols)),
                  pl.BlockSpec((D, D), lambda i: (0, gate_cols + 1)), full(bg),
                  pl.BlockSpec((CH, TM // CH, S5_WIDTH), lambda i: (0, i, 0)),
                  pl.BlockSpec((TM, FFT_WIDTH), lambda i: (i, 0)),
                  full(wglu), full(bglu), full(wbs), full(wbf), full(bbf), full(wo), full(bo),
                  vec, vec, vec, vec, vec, full(wr_t), full(br), full(tri), full(etri)],
        out_specs=(pl.BlockSpec((TM, D), lambda i: (i, 0)),
                   pl.BlockSpec((TM, D), lambda i: (i, 0)),
                   pl.BlockSpec((TOP_K, TM), lambda i: (0, i)),
                   pl.BlockSpec((TOP_K, TM), lambda i: (0, i)),
                   pl.BlockSpec((1, N_EXPERTS, LANES), lambda i: (i, 0, 0))),
        out_shape=(jax.ShapeDtypeStruct((N_TOK, D), F32),
                   jax.ShapeDtypeStruct((N_TOK, D), BF16),
                   jax.ShapeDtypeStruct((TOP_K, N_TOK), jnp.int32),
                   jax.ShapeDtypeStruct((TOP_K, N_TOK), F32),
                   jax.ShapeDtypeStruct((N_TILES, N_EXPERTS, LANES), F32)),
        scratch_shapes=[pltpu.VMEM((NJ, TM, LANES), F32)],
        compiler_params=_cparams(("parallel",)),
        name="mix",
    )(x, emb_r, emb_c, lg, lb, m1, s1, w_in, w_in, bg, y_t, zr, wglu, bglu, wbs, wbf, bbf, wo, bo,
      g1, l1g, l1b, m2, s2, wr_t, br, tri, etri)


def _on_parity(i, fn):
    @pl.when(i % 2 == 0)
    def _():
        fn(0)

    @pl.when(i % 2 == 1)
    def _():
        fn(1)


def _dispatch_kernel(dprev_ref, dest_ref, pos_ref, u_ref, zeroed_ref, buf_ref, sorted_ref, sems):
    del zeroed_ref
    i = pl.program_id(0)

    @pl.when(i == 0)
    def _():
        sorted_ref[...] = jnp.zeros_like(sorted_ref)

    def chunk_copy(slot, table_ref, j):
        dst = pl.multiple_of(table_ref[0, 0, j], SEG_ALIGN)
        return pltpu.make_async_copy(sorted_ref.at[slot, pl.ds(j * SEG_ALIGN, SEG_ALIGN)],
                                     buf_ref.at[pl.ds(dst, SEG_ALIGN)], sems.at[slot])

    def drain(slot):
        pltpu.make_async_copy(sorted_ref.at[slot], buf_ref.at[pl.ds(0, CAP)], sems.at[slot]).wait()

    def run(slot):
        pos = pos_ref[...]
        u = u_ref[...]
        n_rb = CAP // SORT_BLOCK
        per_rb = NCHK // (n_rb // 2)
        for rb in range(n_rb):
            for j in range(rb * per_rb, min((rb + 1) * per_rb, NCHK)):
                chunk_copy(1 - slot, dprev_ref, j).start()
            rows = lax.broadcasted_iota(jnp.int32, (SORT_BLOCK, TM), 0) + rb * SORT_BLOCK
            hit = rows == pos[0:1]
            for k in range(1, TOP_K):
                hit = hit | (rows == pos[k:k + 1])
            onehot = jnp.where(hit, 1.0, 0.0).astype(BF16)
            sorted_ref[slot, rb * SORT_BLOCK:(rb + 1) * SORT_BLOCK, :] = jnp.dot(
                onehot, u, preferred_element_type=F32).astype(BF16)
        drain(1 - slot)

        @pl.when(i == N_TILES - 1)
        def _():
            def issue(j, c):
                chunk_copy(slot, dest_ref, j).start()
                return c
            lax.fori_loop(0, NCHK, issue, 0)
            drain(slot)
    _on_parity(i, run)


def _dispatch(chunk_table, pos_t, u2, zeroed):
    return pl.pallas_call(
        _dispatch_kernel,
        grid=(N_TILES,),
        in_specs=[pl.BlockSpec((1, 1, NCHK), lambda i: (i, 0, 0), memory_space=pltpu.SMEM),
                  pl.BlockSpec((1, 1, NCHK), lambda i: (i + 1, 0, 0), memory_space=pltpu.SMEM),
                  pl.BlockSpec((TOP_K, TM), lambda i: (0, i)),
                  pl.BlockSpec((TM, D), lambda i: (i, 0)),
                  pl.BlockSpec(memory_space=pl.ANY)],
        out_specs=pl.BlockSpec(memory_space=pl.ANY),
        out_shape=jax.ShapeDtypeStruct((ROWS_ALL, D), BF16),
        scratch_shapes=[pltpu.VMEM((2, CAP, D), BF16),
                        pltpu.SemaphoreType.DMA((2,))],
        input_output_aliases={4: 0},
        compiler_params=_cparams(("arbitrary",)),
        name="dispatch",
    )(chunk_table, chunk_table, pos_t, u2, zeroed)


def _ffn_kernel(be_ref, nu_ref, run_ref, nxt_ref, valid_ref, x_ref, wu_hbm, bu_ref, wd_hbm, bd_ref,
                y_ref, wu_ref, wd_ref, wub_ref, wdb_ref, sems):
    i = pl.program_id(0)
    used = i < nu_ref[0]

    def weight_copies(e, slot):
        return (pltpu.make_async_copy(wu_hbm.at[e], wu_ref.at[slot], sems.at[slot]),
                pltpu.make_async_copy(wd_hbm.at[e], wd_ref.at[slot], sems.at[slot]))

    @pl.when(used)
    def _():
        run = run_ref[i]

        @pl.when(run >= 0)
        def _():
            def open_run(slot):
                @pl.when(run == 0)
                def _():
                    for cp in weight_copies(be_ref[i], slot):
                        cp.start()

                @pl.when(nxt_ref[i] >= 0)
                def _():
                    for cp in weight_copies(nxt_ref[i], 1 - slot):
                        cp.start()
                for cp in weight_copies(be_ref[i], slot):
                    cp.wait()
                wub_ref[...] = wu_ref[slot].astype(BF16)
                wdb_ref[...] = wd_ref[slot].astype(BF16)
            _on_parity(run, open_run)

        def expert_rows(r0, nr):
            rows = slice(r0, r0 + nr)
            e = be_ref[i]
            h = (jnp.dot(x_ref[rows, :], wub_ref[...], preferred_element_type=F32)
                 + bu_ref[pl.ds(e, 1), :])
            h_glu = jnp.minimum(h[:, :D], SWIGLU_LIMIT)
            h_lin = jnp.clip(h[:, D:], -SWIGLU_LIMIT, SWIGLU_LIMIT)
            act = (h_glu * _sigmoid(SWIGLU_ALPHA * h_glu) * (h_lin + 1.0)).astype(BF16)
            y_ref[rows, :] = (jnp.dot(act, wdb_ref[...], preferred_element_type=F32)
                              + bd_ref[pl.ds(e, 1), :]).astype(BF16)

        valid = valid_ref[i]

        @pl.when(valid == BM)
        def _():
            expert_rows(0, BM)

        @pl.when(valid < BM)
        def _():
            for h0 in range(0, BM, FFN_HALF):
                @pl.when(valid >= h0 + FFN_HALF)
                def _(h0=h0):
                    expert_rows(h0, FFN_HALF)

                @pl.when(valid < h0 + FFN_HALF)
                def _(h0=h0):
                    for r0 in range(h0, h0 + FFN_HALF, FFN_TAIL):
                        @pl.when(r0 < valid)
                        def _(r0=r0):
                            expert_rows(r0, FFN_TAIL)

                        @pl.when(r0 >= valid)
                        def _(r0=r0):
                            y_ref[r0:r0 + FFN_TAIL, :] = jnp.zeros((FFN_TAIL, D), BF16)


def _ffn(block_expert, n_used, run_id, next_expert, valid, buf, w_up, b_up, w_down, b_down):
    def blk(i, be, nu, *_):
        return jnp.minimum(i, nu[0] - 1)
    return pl.pallas_call(
        _ffn_kernel,
        grid_spec=pltpu.PrefetchScalarGridSpec(
            num_scalar_prefetch=5,
            grid=(N_BLOCKS_ALL,),
            in_specs=[pl.BlockSpec((BM, D), lambda i, *s: (blk(i, *s), 0)),
                      pl.BlockSpec(memory_space=pl.ANY),
                      pl.BlockSpec((N_EXPERTS, 2 * D), lambda i, *s: (0, 0)),
                      pl.BlockSpec(memory_space=pl.ANY),
                      pl.BlockSpec((N_EXPERTS, D), lambda i, *s: (0, 0))],
            out_specs=pl.BlockSpec((BM, D), lambda i, *s: (blk(i, *s), 0)),
            scratch_shapes=[pltpu.VMEM((2, D, 2 * D), F32),
                            pltpu.VMEM((2, D, D), F32),
                            pltpu.VMEM((D, 2 * D), BF16),
                            pltpu.VMEM((D, D), BF16),
                            pltpu.SemaphoreType.DMA((2,))]),
        out_shape=jax.ShapeDtypeStruct((ROWS_ALL, D), BF16),
        input_output_aliases={5: 0},
        compiler_params=_cparams(("arbitrary",)),
        name="ffn",
    )(block_expert, n_used, run_id, next_expert, valid, buf, w_up, b_up, w_down, b_down)


def _combine_kernel(dest_ref, dnext_ref, y_ref, h1_ref, pos_ref, gate_ref, g2_ref, lg_ref, lb_ref,
                    o_ref, sorted_ref, sems):
    i = pl.program_id(0)

    def chunk_copy(slot, table_ref, j):
        src = pl.multiple_of(table_ref[0, 0, j], SEG_ALIGN)
        return pltpu.make_async_copy(y_ref.at[pl.ds(src, SEG_ALIGN)],
                                     sorted_ref.at[slot, pl.ds(j * SEG_ALIGN, SEG_ALIGN)],
                                     sems.at[slot])

    def drain(slot):
        pltpu.make_async_copy(y_ref.at[pl.ds(0, CAP)], sorted_ref.at[slot], sems.at[slot]).wait()

    @pl.when(i == 0)
    def _():
        def issue(j, c):
            chunk_copy(0, dest_ref, j).start()
            return c
        lax.fori_loop(0, NCHK, issue, 0)

    def run(slot):
        drain(slot)

        pos = pos_ref[...]
        gate = gate_ref[...]
        m = jnp.zeros((TM, D), F32)
        n_cb = CAP // CAP_BLOCK
        per_cb = NCHK // (n_cb // 2)
        for cb in range(n_cb):
            for j in range(cb * per_cb, min((cb + 1) * per_cb, NCHK)):
                chunk_copy(1 - slot, dnext_ref, j).start()
            rws = lax.broadcasted_iota(jnp.int32, (CAP_BLOCK, TM), 0) + cb * CAP_BLOCK
            g = jnp.where(rws == pos[0:1], gate[0:1], 0.0)
            for k in range(1, TOP_K):
                g = g + jnp.where(rws == pos[k:k + 1], gate[k:k + 1], 0.0)
            rows = sorted_ref[slot, cb * CAP_BLOCK:(cb + 1) * CAP_BLOCK, :]
            m = m + lax.dot_general(g.astype(BF16), rows, (((0,), (0,)), ((), ())),
                                    preferred_element_type=F32)
        o_ref[...] = _layer_norm(ALPHA * h1_ref[...] + g2_ref[...] * m, lg_ref[...], lb_ref[...])

        @pl.when(i == N_TILES - 1)
        def _():
            drain(1 - slot)
    _on_parity(i, run)


def _combine(chunk_table, y_buf, h1, pos_t, gate_t, g2, lg, lb):
    vec = pl.BlockSpec((1, D), lambda i: (0, 0))
    return pl.pallas_call(
        _combine_kernel,
        grid_spec=pltpu.PrefetchScalarGridSpec(
            num_scalar_prefetch=0,
            grid=(N_TILES,),
            in_specs=[pl.BlockSpec((1, 1, NCHK), lambda i: (i + 1, 0, 0), memory_space=pltpu.SMEM),
                      pl.BlockSpec((1, 1, NCHK), lambda i: (i + 2, 0, 0), memory_space=pltpu.SMEM),
                      pl.BlockSpec(memory_space=pl.ANY),
                      pl.BlockSpec((TM, D), lambda i: (i, 0)),
                      pl.BlockSpec((TOP_K, TM), lambda i: (0, i)),
                      pl.BlockSpec((TOP_K, TM), lambda i: (0, i)),
                      vec, vec, vec],
            out_specs=pl.BlockSpec((TM, D), lambda i: (i, 0)),
            scratch_shapes=[pltpu.VMEM((2, CAP, D), BF16),
                            pltpu.SemaphoreType.DMA((2,))]),
        out_shape=jax.ShapeDtypeStruct((N_TOK, D), F32),
        compiler_params=_cparams(("arbitrary",)),
        name="combine",
    )(chunk_table, chunk_table, y_buf, h1, pos_t, gate_t, g2, lg, lb)


def _sincos_tables():
    q = D // 4
    omega = 1.0 / (10000.0 ** (np.arange(q) / q))

    def emb(n):
        ang = np.arange(n)[:, None] * omega[None, :]
        return jnp.asarray(np.concatenate([np.sin(ang), np.cos(ang)], axis=-1), F32)
    return emb(N_TOK // GRID_W), emb(GRID_W)


def kernel(x, c, ctx, c_ctx, ln_in_g, ln_in_b, w_ada, b_ada, w_in, b_in, s5_lambda_re, s5_lambda_im, s5_log_dt, s5_b_re, s5_b_im, s5_c_re, s5_c_im, s5_d, w_glu, b_glu, w_br_s5, w_br_fft, b_br_fft, w_out, b_out, ln1_g, ln1_b, w_router, b_router, w_up, b_up, w_down, b_down, ln2_g, ln2_b):
    assert x.shape == (1, N_TOK, D) and ctx.shape == (1, N_CTX, D) and w_ada.shape[0] == 1
    row = lambda v: v.reshape(1, -1).astype(F32)

    cc = jnp.concatenate([c.reshape(1, D), c_ctx.reshape(1, D), jnp.zeros((SUBLANES - 2, D), F32)], axis=0)
    ada = _ada(cc, w_ada[0], row(b_ada[0]))
    sh1, sc1, g1, sh2, sc2, g2 = (ada[0:1, k * D:(k + 1) * D] for k in range(6))
    sh1c, sc1c = ada[1:2, 0:D], ada[1:2, D:2 * D]

    emb_r, emb_c = _sincos_tables()
    st1, st2, fc = _dft_tables()
    lg, lb = row(ln_in_g), row(ln_in_b)

    w_in_bf = w_in[0].astype(BF16)
    b_s5 = row(b_in[0][:S5_WIDTH])
    b_fft8 = jnp.concatenate([row(b_in[0][S5_WIDTH:S5_WIDTH + FFT_WIDTH]),
                              jnp.zeros((SUBLANES - 1, FFT_WIDTH), F32)], axis=0)
    b_g = row(b_in[0][S5_WIDTH + FFT_WIDTH:])
    w_fc, b_fc = _fft_weights(w_in[0], b_fft8, fc)
    bcat = jnp.concatenate([b_s5, b_fc[0:1]], axis=1)

    x2 = x[0]
    p_t, xr, xi = _proj(x2, emb_r, emb_c, lg, lb, 1.0 + sc1, sh1, w_in_bf, w_fc, bcat)
    pc_t = _ctx_proj(ctx[0], lg, lb, 1.0 + sc1c, sh1c, w_in_bf[:, :S5_WIDTH], b_s5)

    b_c, a_q, a_p, trans, ctx_w = _s5_tables(
        s5_lambda_re[0], s5_lambda_im[0], s5_log_dt[0], s5_b_re[0], s5_b_im[0],
        s5_c_re[0], s5_c_im[0], s5_d[0])
    y_t, zeroed = _s5(p_t, pc_t, b_c, a_q, a_p, trans, ctx_w)

    yr, yi = _fft1(xr, xi, st1)
    zr = _fft2(yr, yi, st2)

    tri = jnp.asarray(np.arange(TM)[:, None] < np.arange(TM)[None, :], BF16)
    br = jnp.broadcast_to(b_router[0].reshape(N_EXPERTS, 1), (N_EXPERTS, LANES))
    etri = jnp.asarray(np.arange(N_EXPERTS)[:, None] > np.arange(N_EXPERTS)[None, :], BF16)
    wr_t = jnp.transpose(w_router[0])
    wr_hi = wr_t.astype(BF16)
    wr_split = jnp.stack([wr_hi, (wr_t - wr_hi.astype(F32)).astype(BF16)], axis=0)
    h1, u2, pos_t, gate_t, counts = _mix(
        x2, emb_r, emb_c, lg, lb, 1.0 + sc1, sh1, w_in_bf, b_g, y_t, zr,
        w_glu[0].astype(BF16), row(b_glu[0]), w_br_s5[0].astype(BF16), w_br_fft[0].astype(BF16),
        row(b_br_fft[0]), w_out[0].astype(BF16), row(b_out[0]), g1, row(ln1_g[0]), row(ln1_b[0]),
        1.0 + sc2, sh2, wr_split, br, tri, etri)

    cnt = counts[:, :, 0].astype(jnp.int32)
    seg = (cnt + SEG_ALIGN - 1) // SEG_ALIGN * SEG_ALIGN
    seg_end = jnp.cumsum(seg, axis=1)
    seg_start = seg_end - seg
    padded = (jnp.sum(seg, axis=0) + BM - 1) // BM * BM
    pad_ends = jnp.cumsum(padded)
    seg_dest = (pad_ends - padded)[None, :] + jnp.cumsum(seg, axis=0) - seg
    chunk_row = jnp.arange(NCHK, dtype=jnp.int32) * SEG_ALIGN
    chunk_exp = jnp.minimum(jnp.sum(chunk_row[None, :, None] >= seg_end[:, None, :], axis=-1),
                            N_EXPERTS - 1)
    own = chunk_exp[:, :, None] == jnp.arange(N_EXPERTS, dtype=jnp.int32)[None, None, :]
    chunk_dest = (jnp.sum(jnp.where(own, (seg_dest - seg_start)[:, None, :], 0), axis=-1)
                  + chunk_row[None, :]).astype(jnp.int32).reshape(N_TILES, 1, NCHK)
    nchk = (seg_end[:, -1] // SEG_ALIGN).astype(jnp.int32)
    block_start = jnp.arange(N_BLOCKS_ALL, dtype=jnp.int32) * BM
    block_expert = jnp.minimum(jnp.sum(block_start[:, None] >= pad_ends[None, :], axis=1),
                               N_EXPERTS - 1).astype(jnp.int32)
    n_used = (pad_ends[-1:] // BM).astype(jnp.int32)
    opens = (block_start < pad_ends[-1]) & (
        block_expert != jnp.concatenate([jnp.full((1,), -1, jnp.int32), block_expert[:-1]]))
    run_id = jnp.where(opens, jnp.cumsum(opens.astype(jnp.int32)) - 1, -1).astype(jnp.int32)
    experts = jnp.arange(N_EXPERTS, dtype=jnp.int32)
    later = (experts[None, :] > block_expert[:, None]) & (padded[None, :] > 0)
    next_expert = jnp.min(jnp.where(later, experts[None, :], N_EXPERTS), axis=1)
    next_expert = jnp.where(next_expert < N_EXPERTS, next_expert, -1).astype(jnp.int32)

    spare = (ROWS + chunk_row)[None, None, :]
    chunk_table = jnp.concatenate(
        [spare,
         jnp.where(chunk_row[None, None, :] < (nchk * SEG_ALIGN)[:, None, None], chunk_dest, spare),
         spare], axis=0).astype(jnp.int32)
    fill_ends = (pad_ends - padded + jnp.sum(seg, axis=0)).astype(jnp.int32)
    buf = _dispatch(chunk_table, pos_t, u2, zeroed)
    mine = block_expert[:, None] == experts[None, :]
    filled = jnp.sum(jnp.where(mine, fill_ends[None, :], 0), axis=1)
    valid = jnp.clip(filled - block_start, 0, BM).astype(jnp.int32)
    y_buf = _ffn(block_expert, n_used, run_id, next_expert, valid, buf, w_up[0],
                 b_up[0], w_down[0], b_down[0])
    out = _combine(chunk_table, y_buf, h1, pos_t, gate_t, g2, row(ln2_g[0]), row(ln2_b[0]))
    return out.reshape(1, N_TOK, D)
```

```python
import functools
import math

import jax
import jax.numpy as jnp
import numpy as np
from jax import lax
from jax.experimental import pallas as pl
from jax.experimental.pallas import tpu as pltpu

F32 = jnp.float32
BF16 = jnp.bfloat16
HI = lax.Precision.HIGHEST

D = 1024
N_TOK = 16384
N_CTX = 256
GRID_W = 64
S5_GROUP = 16
S5_GROUPS = 32
S5_STATE = 64
S5_WIDTH = 512
FFT_GROUPS = 4
FFT_DIM = 128
FFT_WIDTH = 512
N_EXPERTS = 32
TOP_K = 4
LN_EPS = 1e-5
ALPHA = 2.0 ** 0.25
SWIGLU_ALPHA = 1.702
SWIGLU_LIMIT = 7.0

LANES = 128
SUBLANES = 8
VMEM_LIMIT = 56 * 1024 * 1024

CH = 8
N_CHUNK = N_TOK // CH
N_CHUNK_CTX = N_CTX // CH
NSEG = SUBLANES
SEG = N_CHUNK // NSEG
SCAN_UNROLL = 8
GPT = LANES // S5_GROUP
NJ = S5_WIDTH // LANES
CL = CH * LANES
SW = 4 * GPT * S5_STATE

FN = 128
FB = 16

TM = 512
TM_PROJ = 1024
N_TILES = N_TOK // TM
BM = 1024
FFN_HALF = 512
FFN_TAIL = 256
N_SLOTS = N_TOK * TOP_K
SEG_ALIGN = 2 * SUBLANES
CAP_BLOCK = 256
SORT_BLOCK = 128
CAP = -(-(TOP_K * TM + N_EXPERTS * (SEG_ALIGN - 1)) // CAP_BLOCK) * CAP_BLOCK
NCHK = CAP // SEG_ALIGN
N_BLOCKS = -(-(N_SLOTS + N_TILES * N_EXPERTS * (SEG_ALIGN - 1)) // BM) + N_EXPERTS
ROWS = N_BLOCKS * BM
N_BLOCKS_ALL = N_BLOCKS + -(-CAP // BM)
ROWS_ALL = N_BLOCKS_ALL * BM
ZERO_UNIT = 512
ZERO_PARTS = ROWS_ALL // (NJ * ZERO_UNIT)
assert ZERO_UNIT * ZERO_PARTS * NJ == ROWS_ALL


def _cparams(sem):
    return pltpu.CompilerParams(dimension_semantics=sem, vmem_limit_bytes=VMEM_LIMIT)


def _layer_norm(x, g, b):
    mu = jnp.mean(x, axis=-1, keepdims=True)
    xc = x - mu
    var = jnp.mean(xc * xc, axis=-1, keepdims=True)
    return xc * lax.rsqrt(var + LN_EPS) * g + b


def _sigmoid(x):
    return 0.5 * jnp.tanh(0.5 * x) + 0.5


def _ada_kernel(c_ref, w_ref, b_ref, o_ref):
    c = c_ref[...]
    s = c * _sigmoid(c)
    w = w_ref[...]
    s_hi, w_hi = s.astype(BF16), w.astype(BF16)
    s_lo = (s - s_hi.astype(F32)).astype(BF16)
    w_lo = (w - w_hi.astype(F32)).astype(BF16)

    def mm(a, b):
        return jnp.dot(a, b, preferred_element_type=F32)
    o_ref[...] = mm(s_hi, w_hi) + mm(s_lo, w_hi) + mm(s_hi, w_lo) + b_ref[...]


def _ada(cc, w_ada, b_ada):
    nb = 4
    wb = 6 * D // nb
    return pl.pallas_call(
        _ada_kernel,
        grid=(nb,),
        in_specs=[pl.BlockSpec((SUBLANES, D), lambda i: (0, 0)),
                  pl.BlockSpec((D, wb), lambda i: (0, i)),
                  pl.BlockSpec((1, wb), lambda i: (0, i))],
        out_specs=pl.BlockSpec((SUBLANES, wb), lambda i: (0, i)),
        out_shape=jax.ShapeDtypeStruct((SUBLANES, 6 * D), F32),
        compiler_params=_cparams(("parallel",)),
        name="ada",
    )(cc, w_ada, b_ada)


def _fftw_kernel(w_ref, b_ref, f_ref, wo_ref, bo_ref):
    f = f_ref[...]
    w = w_ref[...]
    w_hi, f_hi = w.astype(BF16), f.astype(BF16)
    w_lo = (w - w_hi.astype(F32)).astype(BF16)
    f_lo = (f - f_hi.astype(F32)).astype(BF16)

    def mm(a, b):
        return jnp.dot(a, b, preferred_element_type=F32)
    wo_ref[...] = (mm(w_hi, f_hi) + mm(w_lo, f_hi) + mm(w_hi, f_lo)).astype(BF16)
    bo_ref[...] = jnp.dot(b_ref[...], f, preferred_element_type=F32, precision=HI)


def _fft_weights(w_in, b_fft8, fc):
    def full(a):
        return pl.BlockSpec(a.shape, lambda i: (0,) * a.ndim)
    outs = (jax.ShapeDtypeStruct((D, 2 * FFT_WIDTH), BF16),
            jax.ShapeDtypeStruct((SUBLANES, 2 * FFT_WIDTH), F32))
    return pl.pallas_call(
        _fftw_kernel,
        grid=(1,),
        in_specs=[pl.BlockSpec((D, FFT_WIDTH), lambda i: (0, S5_WIDTH // FFT_WIDTH)),
                  full(b_fft8), full(fc)],
        out_specs=tuple(pl.BlockSpec(o.shape, lambda i: (0, 0)) for o in outs),
        out_shape=outs,
        compiler_params=_cparams(("arbitrary",)),
        name="fftw",
    )(w_in, b_fft8, fc)


def _pos_code(er_ref, ec_ref, tm):
    nr = tm // GRID_W
    er = er_ref[...]
    row = jnp.broadcast_to(er[:, None, :], (nr, GRID_W, D // 2)).reshape(tm, D // 2)
    col = jnp.concatenate([ec_ref[...]] * nr, axis=0)
    return jnp.concatenate([row, col], axis=-1)


def _to_chunk_major(val, scr_ref, out_ref, tm):
    for j in range(NJ):
        scr_ref[j] = val[:, j * LANES:(j + 1) * LANES]
    for t in range(CH):
        for j in range(NJ):
            piece = scr_ref[j, pl.ds(t, tm // CH, stride=CH), :]
            out_ref[t, :, j * LANES:(j + 1) * LANES] = piece.astype(out_ref.dtype)


def _proj_kernel(x_ref, er_ref, ec_ref, lg_ref, lb_ref, m_ref, s_ref, ws_ref, wf_ref, b_ref,
                 p_ref, xr_ref, xi_ref, scr_ref):
    x = x_ref[...] + _pos_code(er_ref, ec_ref, TM_PROJ)
    h = _layer_norm(x, lg_ref[...], lb_ref[...])
    u = (h * m_ref[...] + s_ref[...]).astype(BF16)
    p_s5 = jnp.dot(u, ws_ref[...], preferred_element_type=F32) + b_ref[:, :S5_WIDTH]
    _to_chunk_major(p_s5, scr_ref, p_ref, TM_PROJ)
    p_f = jnp.dot(u, wf_ref[...], preferred_element_type=F32) + b_ref[:, S5_WIDTH:]
    xr_ref[...] = p_f[:, :FFT_WIDTH].astype(BF16)
    xi_ref[...] = p_f[:, FFT_WIDTH:].astype(BF16)


def _proj(x, emb_r, emb_c, lg, lb, m1, s1, w_in, w_fc, bcat):
    nw = bcat.shape[1]
    vec = pl.BlockSpec((1, D), lambda i: (0, 0))
    return pl.pallas_call(
        _proj_kernel,
        grid=(N_TOK // TM_PROJ,),
        in_specs=[pl.BlockSpec((TM_PROJ, D), lambda i: (i, 0)),
                  pl.BlockSpec((TM_PROJ // GRID_W, D // 2), lambda i: (i, 0)),
                  pl.BlockSpec((GRID_W, D // 2), lambda i: (0, 0)),
                  vec, vec, vec, vec,
                  pl.BlockSpec((D, S5_WIDTH), lambda i: (0, 0)),
                  pl.BlockSpec((D, 2 * FFT_WIDTH), lambda i: (0, 0)),
                  pl.BlockSpec((1, nw), lambda i: (0, 0))],
        out_specs=(pl.BlockSpec((CH, TM_PROJ // CH, S5_WIDTH), lambda i: (0, i, 0)),
                   pl.BlockSpec((TM_PROJ, FFT_WIDTH), lambda i: (i, 0)),
                   pl.BlockSpec((TM_PROJ, FFT_WIDTH), lambda i: (i, 0))),
        out_shape=(jax.ShapeDtypeStruct((CH, N_CHUNK, S5_WIDTH), BF16),
                   jax.ShapeDtypeStruct((N_TOK, FFT_WIDTH), BF16),
                   jax.ShapeDtypeStruct((N_TOK, FFT_WIDTH), BF16)),
        scratch_shapes=[pltpu.VMEM((NJ, TM_PROJ, LANES), F32)],
        compiler_params=_cparams(("parallel",)),
        name="proj",
    )(x, emb_r, emb_c, lg, lb, m1, s1, w_in, w_fc, bcat)


def _ctx_proj_kernel(x_ref, lg_ref, lb_ref, m_ref, s_ref, w_ref, b_ref, p_ref, scr_ref):
    h = _layer_norm(x_ref[...], lg_ref[...], lb_ref[...])
    u = (h * m_ref[...] + s_ref[...]).astype(BF16)
    p = jnp.dot(u, w_ref[...], preferred_element_type=F32) + b_ref[...]
    _to_chunk_major(p, scr_ref, p_ref, N_CTX)


def _ctx_proj(ctx, lg, lb, m1, s1, w_s5, b_s5):
    return pl.pallas_call(
        _ctx_proj_kernel,
        out_shape=jax.ShapeDtypeStruct((CH, N_CHUNK_CTX, S5_WIDTH), BF16),
        scratch_shapes=[pltpu.VMEM((NJ, N_CTX, LANES), F32)],
        compiler_params=pltpu.CompilerParams(vmem_limit_bytes=VMEM_LIMIT),
        name="ctxproj",
    )(ctx, lg, lb, m1, s1, w_s5, b_s5)


def _s5_tables(lam_re, lam_im, log_dt, b_re, b_im, c_re, c_im, d_skip):
    dt = jnp.exp(log_dt)[..., None]
    zr = lam_re * dt
    zi = lam_im * dt

    def apow(m):
        m = jnp.asarray(m, F32)
        mag = jnp.exp(zr[..., None] * m)
        return mag * jnp.cos(zi[..., None] * m), mag * jnp.sin(zi[..., None] * m)

    a_re, a_im = apow(jnp.ones((1,), F32))
    a_re, a_im = a_re[..., 0], a_im[..., 0]
    den = lam_re * lam_re + lam_im * lam_im
    num_re = a_re - 1.0
    k_re = (num_re * lam_re + a_im * lam_im) / den
    k_im = (a_im * lam_re - num_re * lam_im) / den
    bb_re = k_re[..., None] * b_re - k_im[..., None] * b_im
    bb_im = k_re[..., None] * b_im + k_im[..., None] * b_re

    ks = jnp.arange(CH + 1, dtype=F32)
    pw_re, pw_im = apow(ks)
    kmag = jnp.exp(zr[:, :, None, :] * ks[None, None, :, None])
    pk_re = kmag * jnp.cos(zi[:, :, None, :] * ks[None, None, :, None])
    pk_im = kmag * jnp.sin(zi[:, :, None, :] * ks[None, None, :, None])
    bt_re, bt_im = jnp.swapaxes(b_re, 2, 3), jnp.swapaxes(b_im, 2, 3)
    bbt_re = k_re[:, :, None, :] * bt_re - k_im[:, :, None, :] * bt_im
    bbt_im = k_re[:, :, None, :] * bt_im + k_im[:, :, None, :] * bt_re

    ar, ai = pw_re[:, :, :, :CH, None], pw_im[:, :, :, :CH, None]
    cr = jnp.swapaxes(c_re, 2, 3)[:, :, :, None, :]
    ci = jnp.swapaxes(c_im, 2, 3)[:, :, :, None, :]
    ca = jnp.concatenate([cr * ar - ci * ai, -(cr * ai + ci * ar)], axis=2)
    ca = ca.reshape(2, S5_GROUPS, 2 * S5_STATE, CH * S5_GROUP)
    bbt = jnp.concatenate([bbt_re, bbt_im], axis=-1)
    taps = jnp.einsum('dghq,dgqn->dghn', bbt, ca, precision=HI)
    skip = (d_skip.reshape(S5_GROUPS, S5_GROUP, 1) * jnp.eye(S5_GROUP, dtype=F32)[None])
    taps = taps.at[0, :, :, :S5_GROUP].add(skip)
    b_c = jnp.transpose(taps.reshape(2, NJ, GPT, S5_GROUP, CH, S5_GROUP), (1, 4, 0, 2, 3, 5))
    b_c = b_c.reshape(NJ, 2 * CH, LANES, S5_GROUP)

    def q_part(d, descending):
        pr = pk_re[d][:, :CH, None, :]
        pi = pk_im[d][:, :CH, None, :]
        if descending:
            pr, pi = jnp.flip(pr, axis=1), jnp.flip(pi, axis=1)
        br = bbt_re[d][:, None, :, :]
        bi = bbt_im[d][:, None, :, :]
        return pr * br - pi * bi, pr * bi + pi * br

    def q_rows(v):
        v = v.reshape(NJ, GPT, CH, S5_GROUP, S5_STATE)
        return jnp.transpose(v, (0, 2, 1, 3, 4)).reshape(NJ, CL, S5_STATE)
    a_q = jnp.stack([q_rows(v) for v in q_part(0, True) + q_part(1, False)], axis=0)

    def p_part(d, descending):
        pr = pw_re[d][:, :, 1:CH + 1, None]
        pi = pw_im[d][:, :, 1:CH + 1, None]
        if descending:
            pr, pi = jnp.flip(pr, axis=2), jnp.flip(pi, axis=2)
        return (ct_re[d] * pr - ct_im[d] * pi, -(ct_re[d] * pi + ct_im[d] * pr))
    ct_re = jnp.swapaxes(c_re, 2, 3)[:, :, :, None, :]
    ct_im = jnp.swapaxes(c_im, 2, 3)[:, :, :, None, :]
    a_p = jnp.stack([v.reshape(NJ, GPT * S5_STATE, CH * S5_GROUP)
                     for v in p_part(0, False) + p_part(1, True)], axis=0)

    def lanes(v):
        return jnp.transpose(v.reshape(2, NJ, GPT * S5_STATE), (1, 0, 2))
    c_r, c_i = apow(jnp.full((1,), float(CH), F32))
    s_r, s_i = apow(jnp.full((1,), float(CH * SEG), F32))
    cr, ci, sr, si = (lanes(v[..., 0]) for v in (c_r, c_i, s_r, s_i))
    trans = jnp.stack([cr[:, 0], ci[:, 0], cr[:, 1], ci[:, 1],
                       sr[:, 0], si[:, 0], sr[:, 1], si[:, 1]], axis=1)

    cidx = jnp.arange(N_CHUNK_CTX, dtype=F32)
    wf_r, wf_i = apow(CH * (N_CHUNK_CTX - 1 - cidx))
    wb_r, wb_i = apow(CH * cidx)

    def ctx_lanes(v, d):
        return jnp.transpose(v[d].reshape(NJ, GPT * S5_STATE, N_CHUNK_CTX), (0, 2, 1))
    ctx_w = jnp.stack([ctx_lanes(wf_r, 0), ctx_lanes(wf_i, 0),
                       ctx_lanes(wb_r, 1), ctx_lanes(wb_i, 1)], axis=1)
    return b_c, a_q, a_p, trans, ctx_w


def _build_operators(bc_ref, aq_ref, ap_ref, c16_ref, c64_ref, cm_ref, wm_ref, wq_ref, wp_ref):
    def expand(a, c, row_shift, col_shift):
        w = jnp.dot(a, c, preferred_element_type=F32)
        rg = (lax.broadcasted_iota(jnp.int32, (w.shape[0], 1), 0) >> row_shift) & (GPT - 1)
        cg = (lax.broadcasted_iota(jnp.int32, (1, w.shape[1]), 1) >> col_shift) & (GPT - 1)
        return jnp.where(rg == cg, w, 0.0)

    blk = [expand(bc_ref[0, kd], c16_ref[...], 4, 4) for kd in range(2 * CH)]
    for t in range(CH):
        for u in range(CH):
            b = blk[2 * (u - t)] if u > t else blk[2 * (t - u) + 1] if u < t else blk[0] + blk[1]
            wm_ref[0, t * LANES:(t + 1) * LANES, u * LANES:(u + 1) * LANES] = b.astype(BF16)
    half = GPT * S5_STATE
    for s in range(4):
        wq_ref[0, :, s * half:(s + 1) * half] = expand(aq_ref[s, 0], c64_ref[...], 4, 6).astype(BF16)
        wp_ref[0, s * half:(s + 1) * half, :] = expand(ap_ref[s, 0], cm_ref[...], 6, 4).astype(BF16)


def _s5_kernel(p_ref, pc_ref, bc_ref, aq_ref, ap_ref, c16_ref, c64_ref, cm_ref, tr_ref, cw_ref,
               y_ref, rows_ref, v_ref, wm_ref, wq_ref, wp_ref, zero_ref, zsem):
    nq = NJ
    half = GPT * S5_STATE

    zero_ref[...] = jnp.zeros_like(zero_ref)

    def clears(fn):
        def unit(b, c):
            start = pl.multiple_of((pl.program_id(0) * ZERO_PARTS + b) * ZERO_UNIT, ZERO_UNIT)
            fn(pltpu.make_async_copy(zero_ref, rows_ref.at[pl.ds(start, ZERO_UNIT)], zsem))
            return c
        lax.fori_loop(0, ZERO_PARTS, unit, 0)
    clears(lambda cp: cp.start())
    _build_operators(bc_ref, aq_ref, ap_ref, c16_ref, c64_ref, cm_ref, wm_ref, wq_ref, wp_ref)

    def chunk_rows(ref, r0, nrows):
        return jnp.concatenate([ref[t, pl.ds(r0, nrows), :] for t in range(CH)], axis=-1)

    def fill(k, c):
        r0 = pl.multiple_of(k * SEG, SEG)
        v = jnp.dot(chunk_rows(p_ref, r0, SEG), wq_ref[0], preferred_element_type=F32)
        for s in range(4 * nq):
            v_ref[s, pl.ds(k, SEG, stride=NSEG), :] = v[:, s * LANES:(s + 1) * LANES]
        return c
    lax.fori_loop(0, NSEG, fill, 0)

    vc = jnp.dot(chunk_rows(pc_ref, 0, N_CHUNK_CTX), wq_ref[0], preferred_element_type=F32)
    vfr, vfi, vbr, vbi = (vc[:, i * half:(i + 1) * half] for i in range(4))
    wfr, wfi, wbr, wbi = (cw_ref[0, i] for i in range(4))
    s0_fr = jnp.sum(wfr * vfr - wfi * vfi, axis=0, keepdims=True)
    s0_fi = jnp.sum(wfr * vfi + wfi * vfr, axis=0, keepdims=True)
    s0_br = jnp.sum(wbr * vbr - wbi * vbi, axis=0, keepdims=True)
    s0_bi = jnp.sum(wbr * vbi + wbi * vbr, axis=0, keepdims=True)

    tr = tr_ref[0]
    afr, afi, abr, abi = (jnp.broadcast_to(tr[i:i + 1], (NSEG, half)) for i in range(4))
    gfr, gfi, gbr, gbi = (tr[i:i + 1] for i in range(4, 8))

    def load_part(part, i):
        return jnp.concatenate(
            [v_ref[part * nq + q, pl.ds(pl.multiple_of(i * NSEG, NSEG), NSEG), :] for q in range(nq)],
            axis=-1)

    def store_part(part, i, val):
        for q in range(nq):
            v_ref[part * nq + q, pl.ds(pl.multiple_of(i * NSEG, NSEG), NSEG), :] = (
                val[:, q * LANES:(q + 1) * LANES])

    def step(i, carry, write):
        fr, fi, br, bi = carry
        ib = SEG - 1 - i
        ufr, ufi = load_part(0, i), load_part(1, i)
        ubr, ubi = load_part(2, ib), load_part(3, ib)
        if write:
            store_part(0, i, fr)
            store_part(1, i, fi)
            store_part(2, ib, br)
            store_part(3, ib, bi)
        return (afr * fr - afi * fi + ufr, afr * fi + afi * fr + ufi,
                abr * br - abi * bi + ubr, abr * bi + abi * br + ubi)

    zero = jnp.zeros((NSEG, half), F32)
    ffr, ffi, fbr, fbi = lax.fori_loop(0, SEG, functools.partial(step, write=False),
                                       (zero, zero, zero, zero), unroll=SCAN_UNROLL)

    rows_fr, rows_fi = [s0_fr], [s0_fi]
    for k in range(1, NSEG):
        pr, pi = rows_fr[-1], rows_fi[-1]
        rows_fr.append(gfr * pr - gfi * pi + ffr[k - 1:k])
        rows_fi.append(gfr * pi + gfi * pr + ffi[k - 1:k])
    rows_br, rows_bi = [s0_br], [s0_bi]
    for k in range(NSEG - 2, -1, -1):
        pr, pi = rows_br[0], rows_bi[0]
        rows_br.insert(0, gbr * pr - gbi * pi + fbr[k + 1:k + 2])
        rows_bi.insert(0, gbr * pi + gbi * pr + fbi[k + 1:k + 2])
    init = tuple(jnp.concatenate(r, axis=0) for r in (rows_fr, rows_fi, rows_br, rows_bi))

    lax.fori_loop(0, SEG, functools.partial(step, write=True), init, unroll=SCAN_UNROLL)

    def emit(k, c):
        r0 = pl.multiple_of(k * SEG, SEG)
        b = chunk_rows(p_ref, r0, SEG)
        sin = jnp.concatenate([v_ref[s, pl.ds(k, SEG, stride=NSEG), :] for s in range(4 * nq)], axis=-1)
        y = (jnp.dot(b, wm_ref[0], preferred_element_type=F32)
             + jnp.dot(sin.astype(BF16), wp_ref[0], preferred_element_type=F32))
        for t in range(CH):
            y_ref[t, pl.ds(r0, SEG), :] = y[:, t * LANES:(t + 1) * LANES].astype(y_ref.dtype)
        return c
    lax.fori_loop(0, NSEG, emit, 0)
    clears(lambda cp: cp.wait())


def _s5(p_t, pc_t, b_c, a_q, a_p, trans, ctx_w):
    rep = np.ones((1, GPT))
    c16 = jnp.asarray(np.kron(rep, np.eye(S5_GROUP)), F32).astype(BF16)
    c64 = jnp.asarray(np.kron(rep, np.eye(S5_STATE)), F32).astype(BF16)
    c_m = jnp.asarray(np.kron(np.eye(CH), np.kron(rep, np.eye(S5_GROUP))), F32).astype(BF16)
    b_c, a_q, a_p = b_c.astype(BF16), a_q.astype(BF16), a_p.astype(BF16)
    half = GPT * S5_STATE
    return pl.pallas_call(
        _s5_kernel,
        grid=(NJ,),
        in_specs=[pl.BlockSpec((CH, N_CHUNK, LANES), lambda j: (0, 0, j)),
                  pl.BlockSpec((CH, N_CHUNK_CTX, LANES), lambda j: (0, 0, j)),
                  pl.BlockSpec((1, 2 * CH, LANES, S5_GROUP), lambda j: (j, 0, 0, 0)),
                  pl.BlockSpec((4, 1, CL, S5_STATE), lambda j: (0, j, 0, 0)),
                  pl.BlockSpec((4, 1, half, CH * S5_GROUP), lambda j: (0, j, 0, 0)),
                  pl.BlockSpec(c16.shape, lambda j: (0, 0)),
                  pl.BlockSpec(c64.shape, lambda j: (0, 0)),
                  pl.BlockSpec(c_m.shape, lambda j: (0, 0)),
                  pl.BlockSpec((1, SUBLANES, half), lambda j: (j, 0, 0)),
                  pl.BlockSpec((1, 4, N_CHUNK_CTX, half), lambda j: (j, 0, 0, 0))],
        out_specs=(pl.BlockSpec((CH, N_CHUNK, LANES), lambda j: (0, 0, j)),
                   pl.BlockSpec(memory_space=pl.ANY)),
        out_shape=(jax.ShapeDtypeStruct((CH, N_CHUNK, S5_WIDTH), BF16),
                   jax.ShapeDtypeStruct((ROWS_ALL, D), BF16)),
        scratch_shapes=[pltpu.VMEM((4 * NJ, N_CHUNK, LANES), F32),
                        pltpu.VMEM((1, CL, CL), BF16),
                        pltpu.VMEM((1, CL, SW), BF16),
                        pltpu.VMEM((1, SW, CL), BF16),
                        pltpu.VMEM((ZERO_UNIT, D), BF16),
                        pltpu.SemaphoreType.DMA(())],
        compiler_params=_cparams(("parallel",)),
        name="s5",
    )(p_t, pc_t, b_c, a_q, a_p, c16, c64, c_m, trans, ctx_w)


def _dft_tables():
    n = np.arange(FN)
    ang = 2.0 * np.pi * np.outer(n, n) / FN
    c, s = np.cos(ang), np.sin(ang)
    st1 = np.block([[c, s], [-s, c]])
    tw = 2.0 * np.pi * np.outer(n, n) / (FN * FN)
    wr, wi = np.cos(tw), -np.sin(tw)
    fr = c[None] * wr[:, None, :] + s[None] * wi[:, None, :]
    fi = c[None] * wi[:, None, :] - s[None] * wr[:, None, :]
    st2 = np.concatenate([fr, -fi], axis=-1)
    scale = 1.0 / math.sqrt(N_TOK * FFT_DIM)
    blk_c = np.kron(np.eye(FFT_GROUPS), c) * scale
    blk_s = np.kron(np.eye(FFT_GROUPS), s) * scale
    fc = np.concatenate([blk_c, -blk_s], axis=1)
    return (jnp.asarray(st1, F32).astype(BF16), jnp.asarray(st2, F32).astype(BF16), jnp.asarray(fc, F32))


FSL = FFT_WIDTH // LANES


FBH = FB // SUBLANES


def _block_to_slabs(blk, slab_ref, first, per_half):
    for bh in range(FBH):
        val = blk[:, bh * SUBLANES:(bh + 1) * SUBLANES, :].reshape(FN * SUBLANES, FFT_WIDTH)
        for s in range(FSL):
            slab_ref[bh * per_half + first + s] = val[:, s * LANES:(s + 1) * LANES]


def _slab_rows(b, first, per_half):
    return (b // SUBLANES) * per_half + first, pl.ds(b % SUBLANES, FN, stride=SUBLANES)


def _slabs_to_block(slab_ref, first, per_half):
    halves = []
    for bh in range(FBH):
        val = jnp.concatenate([slab_ref[bh * per_half + first + s] for s in range(FSL)], axis=-1)
        halves.append(val.reshape(FN, SUBLANES, FFT_WIDTH))
    return jnp.concatenate(halves, axis=1)


def _fft1_kernel(xr_ref, xi_ref, f_ref, yr_ref, yi_ref, in_ref, out_ref):
    _block_to_slabs(xr_ref[...].astype(F32), in_ref, 0, 2 * FSL)
    _block_to_slabs(xi_ref[...].astype(F32), in_ref, FSL, 2 * FSL)
    for b in range(FB):
        def part(first):
            base, rows = _slab_rows(b, first, 2 * FSL)
            return jnp.concatenate([in_ref[base + s, rows, :] for s in range(FSL)], axis=-1)
        xs = jnp.concatenate([part(0), part(FSL)], axis=0).astype(BF16)
        y = jnp.dot(f_ref[...], xs, preferred_element_type=F32)
        base, rows = _slab_rows(b, 0, 2 * FSL)
        for s in range(FSL):
            out_ref[base + s, rows, :] = y[:FN, s * LANES:(s + 1) * LANES]
            out_ref[base + FSL + s, rows, :] = y[FN:, s * LANES:(s + 1) * LANES]
    yr_ref[...] = _slabs_to_block(out_ref, 0, 2 * FSL).astype(BF16)
    yi_ref[...] = _slabs_to_block(out_ref, FSL, 2 * FSL).astype(BF16)


def _fft1(xr, xi, st1):
    spec = pl.BlockSpec((FN, FB, FFT_WIDTH), lambda i: (0, i, 0))
    slabs = pltpu.VMEM((FBH * 2 * FSL, FN * SUBLANES, LANES), F32)
    return pl.pallas_call(
        _fft1_kernel,
        grid=(FN // FB,),
        in_specs=[spec, spec, pl.BlockSpec((2 * FN, 2 * FN), lambda i: (0, 0))],
        out_specs=(spec, spec),
        out_shape=(jax.ShapeDtypeStruct((FN, FN, FFT_WIDTH), BF16),) * 2,
        scratch_shapes=[slabs, slabs],
        compiler_params=_cparams(("parallel",)),
        name="fft1",
    )(xr.reshape(FN, FN, FFT_WIDTH), xi.reshape(FN, FN, FFT_WIDTH), st1)


def _fft2_kernel(yr_ref, yi_ref, f_ref, z_ref, out_ref):
    for b in range(FB):
        ys = jnp.concatenate([yr_ref[b * FN:(b + 1) * FN, :], yi_ref[b * FN:(b + 1) * FN, :]], axis=0)
        z = jnp.dot(f_ref[b], ys, preferred_element_type=F32)
        base, rows = _slab_rows(b, 0, FSL)
        for s in range(FSL):
            out_ref[base + s, rows, :] = z[:, s * LANES:(s + 1) * LANES]
    z_ref[...] = _slabs_to_block(out_ref, 0, FSL).astype(BF16)


def _fft2(yr, yi, st2):
    rows = pl.BlockSpec((FB * FN, FFT_WIDTH), lambda i: (i, 0))
    z = pl.pallas_call(
        _fft2_kernel,
        grid=(FN // FB,),
        in_specs=[rows, rows, pl.BlockSpec((FB, FN, 2 * FN), lambda i: (i, 0, 0))],
        out_specs=pl.BlockSpec((FN, FB, FFT_WIDTH), lambda i: (0, i, 0)),
        out_shape=jax.ShapeDtypeStruct((FN, FN, FFT_WIDTH), BF16),
        scratch_shapes=[pltpu.VMEM((FBH * FSL, FN * SUBLANES, LANES), F32)],
        compiler_params=_cparams(("parallel",)),
        name="fft2",
    )(yr.reshape(N_TOK, FFT_WIDTH), yi.reshape(N_TOK, FFT_WIDTH), st2)
    return z.reshape(N_TOK, FFT_WIDTH)


def _gelu_tanh(x):
    return 0.5 * x * (1.0 + jnp.tanh(math.sqrt(2.0 / math.pi) * (x + 0.044715 * (x * x * x))))


def _mix_kernel(x_ref, er_ref, ec_ref, lg_ref, lb_ref, m1_ref, s1_ref, wgs_ref, wgf_ref, bg_ref,
                yt_ref, zr_ref, wglu_ref, bglu_ref, wbs_ref, wbf_ref, bbf_ref, wo_ref, bo_ref,
                g1_ref, l1g_ref, l1b_ref, m2_ref, s2_ref, wr_ref, br_ref, tri_ref, etri_ref,
                h1_ref, u2_ref, pos_ref, gate_ref, cnt_ref, scr_ref):
    pos = _pos_code(er_ref, ec_ref, TM)

    def front(r0, nr):
        rows = slice(r0, r0 + nr)
        h = _layer_norm(x_ref[rows, :] + pos[rows, :], lg_ref[...], lb_ref[...])
        u = (h * m1_ref[...] + s1_ref[...]).astype(BF16)

        c0, nc = r0 // CH, nr // CH
        for t in range(CH):
            for j in range(NJ):
                scr_ref[j, pl.ds(r0 + t, nc, stride=CH), :] = (
                    yt_ref[t, c0:c0 + nc, j * LANES:(j + 1) * LANES].astype(F32))
        ys = jnp.concatenate([scr_ref[j, rows, :] for j in range(NJ)], axis=-1)
        z = jnp.dot(_gelu_tanh(ys).astype(BF16), wglu_ref[...], preferred_element_type=F32) + bglu_ref[...]
        glu = (z[:, :S5_WIDTH] * _sigmoid(z[:, S5_WIDTH:])).astype(BF16)
        g_s5 = _sigmoid(jnp.dot(u, wgs_ref[...], preferred_element_type=F32) + bg_ref[:, :D])
        g_fft = _sigmoid(jnp.dot(u, wgf_ref[...], preferred_element_type=F32) + bg_ref[:, D:])
        y_s5 = jnp.dot(glu, wbs_ref[...], preferred_element_type=F32)
        y_fft = jnp.dot(zr_ref[rows, :], wbf_ref[...], preferred_element_type=F32) + bbf_ref[...]
        mixed = (g_s5 * y_s5 + g_fft * y_fft).astype(BF16)
        y = jnp.dot(mixed, wo_ref[...], preferred_element_type=F32) + bo_ref[...]
        h1 = _layer_norm(ALPHA * h + g1_ref[...] * y, l1g_ref[...], l1b_ref[...])
        h1_ref[rows, :] = h1
        u2 = h1 * m2_ref[...] + s2_ref[...]
        u2_ref[rows, :] = u2.astype(BF16)
        u_hi = u2.astype(BF16)
        u_lo = (u2 - u_hi.astype(F32)).astype(BF16)

        def nt(a, b):
            return lax.dot_general(a, b, (((1,), (1,)), ((), ())), preferred_element_type=F32)
        return nt(wr_ref[0], u_hi) + nt(wr_ref[0], u_lo) + nt(wr_ref[1], u_hi)

    logits = front(0, TM) + br_ref[:, 0:1]
    eidx = lax.broadcasted_iota(jnp.int32, (N_EXPERTS, TM), 0)
    vals, hots = [], []
    cur = logits
    for _k in range(TOP_K):
        m = jnp.max(cur, axis=0, keepdims=True)
        sel = jnp.min(jnp.where(cur == m, eidx, N_EXPERTS), axis=0, keepdims=True)
        hot = eidx == sel
        cur = jnp.where(hot, -jnp.inf, cur)
        vals.append(m)
        hots.append(hot)
    exps = [jnp.exp(v - vals[0]) for v in vals]
    den = exps[0] + exps[1] + exps[2] + exps[3]
    gate4 = jnp.concatenate([e / den for e in exps], axis=0)

    hot_sum = (hots[0] | hots[1] | hots[2] | hots[3]).astype(F32)
    before = jnp.dot(hot_sum.astype(BF16), tri_ref[...], preferred_element_type=F32)
    cnt = jnp.broadcast_to(jnp.sum(hot_sum, axis=1, keepdims=True), (N_EXPERTS, LANES))
    cnt8 = jnp.floor((cnt + (SEG_ALIGN - 1)) * (1.0 / SEG_ALIGN)) * SEG_ALIGN
    seg0 = jnp.dot(etri_ref[...], cnt8.astype(BF16), preferred_element_type=F32)
    tot = seg0[:, 0:1] + before
    pos4 = jnp.concatenate(
        [jnp.sum(jnp.where(hk, tot, 0.0), axis=0, keepdims=True) for hk in hots], axis=0)
    pos_ref[...] = pos4.astype(jnp.int32)
    cnt_ref[0] = cnt

    gate_ref[...] = gate4


def _mix(x, emb_r, emb_c, lg, lb, m1, s1, w_in, bg, y_t, zr, wglu, bglu, wbs, wbf, bbf, wo, bo,
         g1, l1g, l1b, m2, s2, wr_t, br, tri, etri):
    gate_cols = (S5_WIDTH + FFT_WIDTH) // D
    vec = pl.BlockSpec((1, D), lambda i: (0, 0))

    def full(a):
        return pl.BlockSpec(a.shape, lambda i: (0,) * a.ndim)
    return pl.pallas_call(
        _mix_kernel,
        grid=(N_TOK // TM,),
        in_specs=[pl.BlockSpec((TM, D), lambda i: (i, 0)),
                  pl.BlockSpec((TM // GRID_W, D // 2), lambda i: (i, 0)),
                  pl.BlockSpec((GRID_W, D // 2), lambda i: (0, 0)),
                  vec, vec, vec, vec,
                  pl.BlockSpec((D, D), lambda i: (0, gate_cols)),
                  pl.BlockSpec((D, D), lambda i: (0, gate_cols + 1)), full(bg),
                  pl.BlockSpec((CH, TM // CH, S5_WIDTH), lambda i: (0, i, 0)),
                  pl.BlockSpec((TM, FFT_WIDTH), lambda i: (i, 0)),
                  full(wglu), full(bglu), full(wbs), full(wbf), full(bbf), full(wo), full(bo),
                  vec, vec, vec, vec, vec, full(wr_t), full(br), full(tri), full(etri)],
        out_specs=(pl.BlockSpec((TM, D), lambda i: (i, 0)),
                   pl.BlockSpec((TM, D), lambda i: (i, 0)),
                   pl.BlockSpec((TOP_K, TM), lambda i: (0, i)),
                   pl.BlockSpec((TOP_K, TM), lambda i: (0, i)),
                   pl.BlockSpec((1, N_EXPERTS, LANES), lambda i: (i, 0, 0))),
        out_shape=(jax.ShapeDtypeStruct((N_TOK, D), F32),
                   jax.ShapeDtypeStruct((N_TOK, D), BF16),
                   jax.ShapeDtypeStruct((TOP_K, N_TOK), jnp.int32),
                   jax.ShapeDtypeStruct((TOP_K, N_TOK), F32),
                   jax.ShapeDtypeStruct((N_TILES, N_EXPERTS, LANES), F32)),
        scratch_shapes=[pltpu.VMEM((NJ, TM, LANES), F32)],
        compiler_params=_cparams(("parallel",)),
        name="mix",
    )(x, emb_r, emb_c, lg, lb, m1, s1, w_in, w_in, bg, y_t, zr, wglu, bglu, wbs, wbf, bbf, wo, bo,
      g1, l1g, l1b, m2, s2, wr_t, br, tri, etri)


def _on_parity(i, fn):
    @pl.when(i % 2 == 0)
    def _():
        fn(0)

    @pl.when(i % 2 == 1)
    def _():
        fn(1)


def _dispatch_kernel(dprev_ref, dest_ref, pos_ref, u_ref, zeroed_ref, buf_ref, sorted_ref, sems):
    del zeroed_ref
    i = pl.program_id(0)

    @pl.when(i == 0)
    def _():
        sorted_ref[...] = jnp.zeros_like(sorted_ref)

    def chunk_copy(slot, table_ref, j):
        dst = pl.multiple_of(table_ref[0, 0, j], SEG_ALIGN)
        return pltpu.make_async_copy(sorted_ref.at[slot, pl.ds(j * SEG_ALIGN, SEG_ALIGN)],
                                     buf_ref.at[pl.ds(dst, SEG_ALIGN)], sems.at[slot])

    def drain(slot):
        pltpu.make_async_copy(sorted_ref.at[slot], buf_ref.at[pl.ds(0, CAP)], sems.at[slot]).wait()

    def run(slot):
        pos = pos_ref[...]
        u = u_ref[...]
        n_rb = CAP // SORT_BLOCK
        per_rb = NCHK // (n_rb // 2)
        for rb in range(n_rb):
            for j in range(rb * per_rb, min((rb + 1) * per_rb, NCHK)):
                chunk_copy(1 - slot, dprev_ref, j).start()
            rows = lax.broadcasted_iota(jnp.int32, (SORT_BLOCK, TM), 0) + rb * SORT_BLOCK
            hit = rows == pos[0:1]
            for k in range(1, TOP_K):
                hit = hit | (rows == pos[k:k + 1])
            onehot = jnp.where(hit, 1.0, 0.0).astype(BF16)
            sorted_ref[slot, rb * SORT_BLOCK:(rb + 1) * SORT_BLOCK, :] = jnp.dot(
                onehot, u, preferred_element_type=F32).astype(BF16)
        drain(1 - slot)

        @pl.when(i == N_TILES - 1)
        def _():
            def issue(j, c):
                chunk_copy(slot, dest_ref, j).start()
                return c
            lax.fori_loop(0, NCHK, issue, 0)
            drain(slot)
    _on_parity(i, run)


def _dispatch(chunk_table, pos_t, u2, zeroed):
    return pl.pallas_call(
        _dispatch_kernel,
        grid=(N_TILES,),
        in_specs=[pl.BlockSpec((1, 1, NCHK), lambda i: (i, 0, 0), memory_space=pltpu.SMEM),
                  pl.BlockSpec((1, 1, NCHK), lambda i: (i + 1, 0, 0), memory_space=pltpu.SMEM),
                  pl.BlockSpec((TOP_K, TM), lambda i: (0, i)),
                  pl.BlockSpec((TM, D), lambda i: (i, 0)),
                  pl.BlockSpec(memory_space=pl.ANY)],
        out_specs=pl.BlockSpec(memory_space=pl.ANY),
        out_shape=jax.ShapeDtypeStruct((ROWS_ALL, D), BF16),
        scratch_shapes=[pltpu.VMEM((2, CAP, D), BF16),
                        pltpu.SemaphoreType.DMA((2,))],
        input_output_aliases={4: 0},
        compiler_params=_cparams(("arbitrary",)),
        name="dispatch",
    )(chunk_table, chunk_table, pos_t, u2, zeroed)


def _ffn_kernel(be_ref, nu_ref, run_ref, nxt_ref, valid_ref, x_ref, wu_hbm, bu_ref, wd_hbm, bd_ref,
                y_ref, wu_ref, wd_ref, wub_ref, wdb_ref, sems):
    i = pl.program_id(0)
    used = i < nu_ref[0]

    def weight_copies(e, slot):
        return (pltpu.make_async_copy(wu_hbm.at[e], wu_ref.at[slot], sems.at[slot]),
                pltpu.make_async_copy(wd_hbm.at[e], wd_ref.at[slot], sems.at[slot]))

    @pl.when(used)
    def _():
        run = run_ref[i]

        @pl.when(run >= 0)
        def _():
            def open_run(slot):
                @pl.when(run == 0)
                def _():
                    for cp in weight_copies(be_ref[i], slot):
                        cp.start()

                @pl.when(nxt_ref[i] >= 0)
                def _():
                    for cp in weight_copies(nxt_ref[i], 1 - slot):
                        cp.start()
                for cp in weight_copies(be_ref[i], slot):
                    cp.wait()
                wub_ref[...] = wu_ref[slot].astype(BF16)
                wdb_ref[...] = wd_ref[slot].astype(BF16)
            _on_parity(run, open_run)

        def expert_rows(r0, nr):
            rows = slice(r0, r0 + nr)
            e = be_ref[i]
            h = (jnp.dot(x_ref[rows, :], wub_ref[...], preferred_element_type=F32)
                 + bu_ref[pl.ds(e, 1), :])
            h_glu = jnp.minimum(h[:, :D], SWIGLU_LIMIT)
            h_lin = jnp.clip(h[:, D:], -SWIGLU_LIMIT, SWIGLU_LIMIT)
            act = (h_glu * _sigmoid(SWIGLU_ALPHA * h_glu) * (h_lin + 1.0)).astype(BF16)
            y_ref[rows, :] = (jnp.dot(act, wdb_ref[...], preferred_element_type=F32)
                              + bd_ref[pl.ds(e, 1), :]).astype(BF16)

        valid = valid_ref[i]

        @pl.when(valid == BM)
        def _():
            expert_rows(0, BM)

        @pl.when(valid < BM)
        def _():
            for h0 in range(0, BM, FFN_HALF):
                @pl.when(valid >= h0 + FFN_HALF)
                def _(h0=h0):
                    expert_rows(h0, FFN_HALF)

                @pl.when(valid < h0 + FFN_HALF)
                def _(h0=h0):
                    for r0 in range(h0, h0 + FFN_HALF, FFN_TAIL):
                        @pl.when(r0 < valid)
                        def _(r0=r0):
                            expert_rows(r0, FFN_TAIL)

                        @pl.when(r0 >= valid)
                        def _(r0=r0):
                            y_ref[r0:r0 + FFN_TAIL, :] = jnp.zeros((FFN_TAIL, D), BF16)


def _ffn(block_expert, n_used, run_id, next_expert, valid, buf, w_up, b_up, w_down, b_down):
    def blk(i, be, nu, *_):
        return jnp.minimum(i, nu[0] - 1)
    return pl.pallas_call(
        _ffn_kernel,
        grid_spec=pltpu.PrefetchScalarGridSpec(
            num_scalar_prefetch=5,
            grid=(N_BLOCKS_ALL,),
            in_specs=[pl.BlockSpec((BM, D), lambda i, *s: (blk(i, *s), 0)),
                      pl.BlockSpec(memory_space=pl.ANY),
                      pl.BlockSpec((N_EXPERTS, 2 * D), lambda i, *s: (0, 0)),
                      pl.BlockSpec(memory_space=pl.ANY),
                      pl.BlockSpec((N_EXPERTS, D), lambda i, *s: (0, 0))],
            out_specs=pl.BlockSpec((BM, D), lambda i, *s: (blk(i, *s), 0)),
            scratch_shapes=[pltpu.VMEM((2, D, 2 * D), F32),
                            pltpu.VMEM((2, D, D), F32),
                            pltpu.VMEM((D, 2 * D), BF16),
                            pltpu.VMEM((D, D), BF16),
                            pltpu.SemaphoreType.DMA((2,))]),
        out_shape=jax.ShapeDtypeStruct((ROWS_ALL, D), BF16),
        input_output_aliases={5: 0},
        compiler_params=_cparams(("arbitrary",)),
        name="ffn",
    )(block_expert, n_used, run_id, next_expert, valid, buf, w_up, b_up, w_down, b_down)


def _combine_kernel(dest_ref, dnext_ref, y_ref, h1_ref, pos_ref, gate_ref, g2_ref, lg_ref, lb_ref,
                    o_ref, sorted_ref, sems):
    i = pl.program_id(0)

    def chunk_copy(slot, table_ref, j):
        src = pl.multiple_of(table_ref[0, 0, j], SEG_ALIGN)
        return pltpu.make_async_copy(y_ref.at[pl.ds(src, SEG_ALIGN)],
                                     sorted_ref.at[slot, pl.ds(j * SEG_ALIGN, SEG_ALIGN)],
                                     sems.at[slot])

    def drain(slot):
        pltpu.make_async_copy(y_ref.at[pl.ds(0, CAP)], sorted_ref.at[slot], sems.at[slot]).wait()

    @pl.when(i == 0)
    def _():
        def issue(j, c):
            chunk_copy(0, dest_ref, j).start()
            return c
        lax.fori_loop(0, NCHK, issue, 0)

    def run(slot):
        drain(slot)

        pos = pos_ref[...]
        gate = gate_ref[...]
        m = jnp.zeros((TM, D), F32)
        n_cb = CAP // CAP_BLOCK
        per_cb = NCHK // (n_cb // 2)
        for cb in range(n_cb):
            for j in range(cb * per_cb, min((cb + 1) * per_cb, NCHK)):
                chunk_copy(1 - slot, dnext_ref, j).start()
            rws = lax.broadcasted_iota(jnp.int32, (CAP_BLOCK, TM), 0) + cb * CAP_BLOCK
            g = jnp.where(rws == pos[0:1], gate[0:1], 0.0)
            for k in range(1, TOP_K):
                g = g + jnp.where(rws == pos[k:k + 1], gate[k:k + 1], 0.0)
            rows = sorted_ref[slot, cb * CAP_BLOCK:(cb + 1) * CAP_BLOCK, :]
            m = m + lax.dot_general(g.astype(BF16), rows, (((0,), (0,)), ((), ())),
                                    preferred_element_type=F32)
        o_ref[...] = _layer_norm(ALPHA * h1_ref[...] + g2_ref[...] * m, lg_ref[...], lb_ref[...])

        @pl.when(i == N_TILES - 1)
        def _():
            drain(1 - slot)
    _on_parity(i, run)


def _combine(chunk_table, y_buf, h1, pos_t, gate_t, g2, lg, lb):
    vec = pl.BlockSpec((1, D), lambda i: (0, 0))
    return pl.pallas_call(
        _combine_kernel,
        grid_spec=pltpu.PrefetchScalarGridSpec(
            num_scalar_prefetch=0,
            grid=(N_TILES,),
            in_specs=[pl.BlockSpec((1, 1, NCHK), lambda i: (i + 1, 0, 0), memory_space=pltpu.SMEM),
                      pl.BlockSpec((1, 1, NCHK), lambda i: (i + 2, 0, 0), memory_space=pltpu.SMEM),
                      pl.BlockSpec(memory_space=pl.ANY),
                      pl.BlockSpec((TM, D), lambda i: (i, 0)),
                      pl.BlockSpec((TOP_K, TM), lambda i: (0, i)),
                      pl.BlockSpec((TOP_K, TM), lambda i: (0, i)),
                      vec, vec, vec],
            out_specs=pl.BlockSpec((TM, D), lambda i: (i, 0)),
            scratch_shapes=[pltpu.VMEM((2, CAP, D), BF16),
                            pltpu.SemaphoreType.DMA((2,))]),
        out_shape=jax.ShapeDtypeStruct((N_TOK, D), F32),
        compiler_params=_cparams(("arbitrary",)),
        name="combine",
    )(chunk_table, chunk_table, y_buf, h1, pos_t, gate_t, g2, lg, lb)


def _sincos_tables():
    q = D // 4
    omega = 1.0 / (10000.0 ** (np.arange(q) / q))

    def emb(n):
        ang = np.arange(n)[:, None] * omega[None, :]
        return jnp.asarray(np.concatenate([np.sin(ang), np.cos(ang)], axis=-1), F32)
    return emb(N_TOK // GRID_W), emb(GRID_W)


def kernel(x, c, ctx, c_ctx, ln_in_g, ln_in_b, w_ada, b_ada, w_in, b_in, s5_lambda_re, s5_lambda_im, s5_log_dt, s5_b_re, s5_b_im, s5_c_re, s5_c_im, s5_d, w_glu, b_glu, w_br_s5, w_br_fft, b_br_fft, w_out, b_out, ln1_g, ln1_b, w_router, b_router, w_up, b_up, w_down, b_down, ln2_g, ln2_b):
    assert x.shape == (1, N_TOK, D) and ctx.shape == (1, N_CTX, D) and w_ada.shape[0] == 1
    row = lambda v: v.reshape(1, -1).astype(F32)

    cc = jnp.concatenate([c.reshape(1, D), c_ctx.reshape(1, D), jnp.zeros((SUBLANES - 2, D), F32)], axis=0)
    ada = _ada(cc, w_ada[0], row(b_ada[0]))
    sh1, sc1, g1, sh2, sc2, g2 = (ada[0:1, k * D:(k + 1) * D] for k in range(6))
    sh1c, sc1c = ada[1:2, 0:D], ada[1:2, D:2 * D]

    emb_r, emb_c = _sincos_tables()
    st1, st2, fc = _dft_tables()
    lg, lb = row(ln_in_g), row(ln_in_b)

    w_in_bf = w_in[0].astype(BF16)
    b_s5 = row(b_in[0][:S5_WIDTH])
    b_fft8 = jnp.concatenate([row(b_in[0][S5_WIDTH:S5_WIDTH + FFT_WIDTH]),
                              jnp.zeros((SUBLANES - 1, FFT_WIDTH), F32)], axis=0)
    b_g = row(b_in[0][S5_WIDTH + FFT_WIDTH:])
    w_fc, b_fc = _fft_weights(w_in[0], b_fft8, fc)
    bcat = jnp.concatenate([b_s5, b_fc[0:1]], axis=1)

    x2 = x[0]
    p_t, xr, xi = _proj(x2, emb_r, emb_c, lg, lb, 1.0 + sc1, sh1, w_in_bf, w_fc, bcat)
    pc_t = _ctx_proj(ctx[0], lg, lb, 1.0 + sc1c, sh1c, w_in_bf[:, :S5_WIDTH], b_s5)

    b_c, a_q, a_p, trans, ctx_w = _s5_tables(
        s5_lambda_re[0], s5_lambda_im[0], s5_log_dt[0], s5_b_re[0], s5_b_im[0],
        s5_c_re[0], s5_c_im[0], s5_d[0])
    y_t, zeroed = _s5(p_t, pc_t, b_c, a_q, a_p, trans, ctx_w)

    yr, yi = _fft1(xr, xi, st1)
    zr = _fft2(yr, yi, st2)

    tri = jnp.asarray(np.arange(TM)[:, None] < np.arange(TM)[None, :], BF16)
    br = jnp.broadcast_to(b_router[0].reshape(N_EXPERTS, 1), (N_EXPERTS, LANES))
    etri = jnp.asarray(np.arange(N_EXPERTS)[:, None] > np.arange(N_EXPERTS)[None, :], BF16)
    wr_t = jnp.transpose(w_router[0])
    wr_hi = wr_t.astype(BF16)
    wr_split = jnp.stack([wr_hi, (wr_t - wr_hi.astype(F32)).astype(BF16)], axis=0)
    h1, u2, pos_t, gate_t, counts = _mix(
        x2, emb_r, emb_c, lg, lb, 1.0 + sc1, sh1, w_in_bf, b_g, y_t, zr,
        w_glu[0].astype(BF16), row(b_glu[0]), w_br_s5[0].astype(BF16), w_br_fft[0].astype(BF16),
        row(b_br_fft[0]), w_out[0].astype(BF16), row(b_out[0]), g1, row(ln1_g[0]), row(ln1_b[0]),
        1.0 + sc2, sh2, wr_split, br, tri, etri)

    cnt = counts[:, :, 0].astype(jnp.int32)
    seg = (cnt + SEG_ALIGN - 1) // SEG_ALIGN * SEG_ALIGN
    seg_end = jnp.cumsum(seg, axis=1)
    seg_start = seg_end - seg
    padded = (jnp.sum(seg, axis=0) + BM - 1) // BM * BM
    pad_ends = jnp.cumsum(padded)
    seg_dest = (pad_ends - padded)[None, :] + jnp.cumsum(seg, axis=0) - seg
    chunk_row = jnp.arange(NCHK, dtype=jnp.int32) * SEG_ALIGN
    chunk_exp = jnp.minimum(jnp.sum(chunk_row[None, :, None] >= seg_end[:, None, :], axis=-1),
                            N_EXPERTS - 1)
    own = chunk_exp[:, :, None] == jnp.arange(N_EXPERTS, dtype=jnp.int32)[None, None, :]
    chunk_dest = (jnp.sum(jnp.where(own, (seg_dest - seg_start)[:, None, :], 0), axis=-1)
                  + chunk_row[None, :]).astype(jnp.int32).reshape(N_TILES, 1, NCHK)
    nchk = (seg_end[:, -1] // SEG_ALIGN).astype(jnp.int32)
    block_start = jnp.arange(N_BLOCKS_ALL, dtype=jnp.int32) * BM
    block_expert = jnp.minimum(jnp.sum(block_start[:, None] >= pad_ends[None, :], axis=1),
                               N_EXPERTS - 1).astype(jnp.int32)
    n_used = (pad_ends[-1:] // BM).astype(jnp.int32)
    opens = (block_start < pad_ends[-1]) & (
        block_expert != jnp.concatenate([jnp.full((1,), -1, jnp.int32), block_expert[:-1]]))
    run_id = jnp.where(opens, jnp.cumsum(opens.astype(jnp.int32)) - 1, -1).astype(jnp.int32)
    experts = jnp.arange(N_EXPERTS, dtype=jnp.int32)
    later = (experts[None, :] > block_expert[:, None]) & (padded[None, :] > 0)
    next_expert = jnp.min(jnp.where(later, experts[None, :], N_EXPERTS), axis=1)
    next_expert = jnp.where(next_expert < N_EXPERTS, next_expert, -1).astype(jnp.int32)

    spare = (ROWS + chunk_row)[None, None, :]
    chunk_table = jnp.concatenate(
        [spare,
         jnp.where(chunk_row[None, None, :] < (nchk * SEG_ALIGN)[:, None, None], chunk_dest, spare),
         spare], axis=0).astype(jnp.int32)
    fill_ends = (pad_ends - padded + jnp.sum(seg, axis=0)).astype(jnp.int32)
    buf = _dispatch(chunk_table, pos_t, u2, zeroed)
    mine = block_expert[:, None] == experts[None, :]
    filled = jnp.sum(jnp.where(mine, fill_ends[None, :], 0), axis=1)
    valid = jnp.clip(filled - block_start, 0, BM).astype(jnp.int32)
    y_buf = _ffn(block_expert, n_used, run_id, next_expert, valid, buf, w_up[0],
                 b_up[0], w_down[0], b_down[0])
    out = _combine(chunk_table, y_buf, h1, pos_t, gate_t, g2, row(ln2_g[0]), row(ln2_b[0]))
    return out.reshape(1, N_TOK, D)
```

```python
import functools
import math

import jax
import jax.numpy as jnp
import numpy as np
from jax import lax
from jax.experimental import pallas as pl
from jax.experimental.pallas import tpu as pltpu

F32 = jnp.float32
BF16 = jnp.bfloat16
HI = lax.Precision.HIGHEST

D = 1024
N_TOK = 16384
N_CTX = 256
GRID_W = 64
S5_GROUP = 16
S5_GROUPS = 32
S5_STATE = 64
S5_WIDTH = 512
FFT_GROUPS = 4
FFT_DIM = 128
FFT_WIDTH = 512
N_EXPERTS = 32
TOP_K = 4
LN_EPS = 1e-5
ALPHA = 2.0 ** 0.25
SWIGLU_ALPHA = 1.702
SWIGLU_LIMIT = 7.0

LANES = 128
SUBLANES = 8
VMEM_LIMIT = 56 * 1024 * 1024

CH = 8
N_CHUNK = N_TOK // CH
N_CHUNK_CTX = N_CTX // CH
NSEG = SUBLANES
SEG = N_CHUNK // NSEG
SCAN_UNROLL = 8
GPT = LANES // S5_GROUP
NJ = S5_WIDTH // LANES
CL = CH * LANES
SW = 4 * GPT * S5_STATE

FN = 128
FB = 16

TM = 512
TM_PROJ = 1024
N_TILES = N_TOK // TM
BM = 1024
FFN_HALF = 512
FFN_TAIL = 256
N_SLOTS = N_TOK * TOP_K
SEG_ALIGN = 2 * SUBLANES
CAP_BLOCK = 256
SORT_BLOCK = 128
CAP = -(-(TOP_K * TM + N_EXPERTS * (SEG_ALIGN - 1)) // CAP_BLOCK) * CAP_BLOCK
NCHK = CAP // SEG_ALIGN
N_BLOCKS = -(-(N_SLOTS + N_TILES * N_EXPERTS * (SEG_ALIGN - 1)) // BM) + N_EXPERTS
ROWS = N_BLOCKS * BM
N_BLOCKS_ALL = N_BLOCKS + -(-CAP // BM)
ROWS_ALL = N_BLOCKS_ALL * BM
ZERO_UNIT = 512
ZERO_PARTS = ROWS_ALL // (NJ * ZERO_UNIT)
assert ZERO_UNIT * ZERO_PARTS * NJ == ROWS_ALL


def _cparams(sem):
    return pltpu.CompilerParams(dimension_semantics=sem, vmem_limit_bytes=VMEM_LIMIT)


def _layer_norm(x, g, b):
    mu = jnp.mean(x, axis=-1, keepdims=True)
    xc = x - mu
    var = jnp.mean(xc * xc, axis=-1, keepdims=True)
    return xc * lax.rsqrt(var + LN_EPS) * g + b


def _sigmoid(x):
    return 0.5 * jnp.tanh(0.5 * x) + 0.5


def _ada_kernel(c_ref, w_ref, b_ref, o_ref):
    c = c_ref[...]
    s = c * _sigmoid(c)
    w = w_ref[...]
    s_hi, w_hi = s.astype(BF16), w.astype(BF16)
    s_lo = (s - s_hi.astype(F32)).astype(BF16)
    w_lo = (w - w_hi.astype(F32)).astype(BF16)

    def mm(a, b):
        return jnp.dot(a, b, preferred_element_type=F32)
    o_ref[...] = mm(s_hi, w_hi) + mm(s_lo, w_hi) + mm(s_hi, w_lo) + b_ref[...]


def _ada(cc, w_ada, b_ada):
    nb = 4
    wb = 6 * D // nb
    return pl.pallas_call(
        _ada_kernel,
        grid=(nb,),
        in_specs=[pl.BlockSpec((SUBLANES, D), lambda i: (0, 0)),
                  pl.BlockSpec((D, wb), lambda i: (0, i)),
                  pl.BlockSpec((1, wb), lambda i: (0, i))],
        out_specs=pl.BlockSpec((SUBLANES, wb), lambda i: (0, i)),
        out_shape=jax.ShapeDtypeStruct((SUBLANES, 6 * D), F32),
        compiler_params=_cparams(("parallel",)),
        name="ada",
    )(cc, w_ada, b_ada)


def _fftw_kernel(w_ref, b_ref, f_ref, wo_ref, bo_ref):
    f = f_ref[...]
    w = w_ref[...]
    w_hi, f_hi = w.astype(BF16), f.astype(BF16)
    w_lo = (w - w_hi.astype(F32)).astype(BF16)
    f_lo = (f - f_hi.astype(F32)).astype(BF16)

    def mm(a, b):
        return jnp.dot(a, b, preferred_element_type=F32)
    wo_ref[...] = (mm(w_hi, f_hi) + mm(w_lo, f_hi) + mm(w_hi, f_lo)).astype(BF16)
    bo_ref[...] = jnp.dot(b_ref[...], f, preferred_element_type=F32, precision=HI)


def _fft_weights(w_in, b_fft8, fc):
    def full(a):
        return pl.BlockSpec(a.shape, lambda i: (0,) * a.ndim)
    outs = (jax.ShapeDtypeStruct((D, 2 * FFT_WIDTH), BF16),
            jax.ShapeDtypeStruct((SUBLANES, 2 * FFT_WIDTH), F32))
    return pl.pallas_call(
        _fftw_kernel,
        grid=(1,),
        in_specs=[pl.BlockSpec((D, FFT_WIDTH), lambda i: (0, S5_WIDTH // FFT_WIDTH)),
                  full(b_fft8), full(fc)],
        out_specs=tuple(pl.BlockSpec(o.shape, lambda i: (0, 0)) for o in outs),
        out_shape=outs,
        compiler_params=_cparams(("arbitrary",)),
        name="fftw",
    )(w_in, b_fft8, fc)


def _pos_code(er_ref, ec_ref, tm):
    nr = tm // GRID_W
    er = er_ref[...]
    row = jnp.broadcast_to(er[:, None, :], (nr, GRID_W, D // 2)).reshape(tm, D // 2)
    col = jnp.concatenate([ec_ref[...]] * nr, axis=0)
    return jnp.concatenate([row, col], axis=-1)


def _to_chunk_major(val, scr_ref, out_ref, tm):
    for j in range(NJ):
        scr_ref[j] = val[:, j * LANES:(j + 1) * LANES]
    for t in range(CH):
        for j in range(NJ):
            piece = scr_ref[j, pl.ds(t, tm // CH, stride=CH), :]
            out_ref[t, :, j * LANES:(j + 1) * LANES] = piece.astype(out_ref.dtype)


def _proj_kernel(x_ref, er_ref, ec_ref, lg_ref, lb_ref, m_ref, s_ref, ws_ref, wf_ref, b_ref,
                 p_ref, xr_ref, xi_ref, scr_ref):
    x = x_ref[...] + _pos_code(er_ref, ec_ref, TM_PROJ)
    h = _layer_norm(x, lg_ref[...], lb_ref[...])
    u = (h * m_ref[...] + s_ref[...]).astype(BF16)
    p_s5 = jnp.dot(u, ws_ref[...], preferred_element_type=F32) + b_ref[:, :S5_WIDTH]
    _to_chunk_major(p_s5, scr_ref, p_ref, TM_PROJ)
    p_f = jnp.dot(u, wf_ref[...], preferred_element_type=F32) + b_ref[:, S5_WIDTH:]
    xr_ref[...] = p_f[:, :FFT_WIDTH].astype(BF16)
    xi_ref[...] = p_f[:, FFT_WIDTH:].astype(BF16)


def _proj(x, emb_r, emb_c, lg, lb, m1, s1, w_in, w_fc, bcat):
    nw = bcat.shape[1]
    vec = pl.BlockSpec((1, D), lambda i: (0, 0))
    return pl.pallas_call(
        _proj_kernel,
        grid=(N_TOK // TM_PROJ,),
        in_specs=[pl.BlockSpec((TM_PROJ, D), lambda i: (i, 0)),
                  pl.BlockSpec((TM_PROJ // GRID_W, D // 2), lambda i: (i, 0)),
                  pl.BlockSpec((GRID_W, D // 2), lambda i: (0, 0)),
                  vec, vec, vec, vec,
                  pl.BlockSpec((D, S5_WIDTH), lambda i: (0, 0)),
                  pl.BlockSpec((D, 2 * FFT_WIDTH), lambda i: (0, 0)),
                  pl.BlockSpec((1, nw), lambda i: (0, 0))],
        out_specs=(pl.BlockSpec((CH, TM_PROJ // CH, S5_WIDTH), lambda i: (0, i, 0)),
                   pl.BlockSpec((TM_PROJ, FFT_WIDTH), lambda i: (i, 0)),
                   pl.BlockSpec((TM_PROJ, FFT_WIDTH), lambda i: (i, 0))),
        out_shape=(jax.ShapeDtypeStruct((CH, N_CHUNK, S5_WIDTH), BF16),
                   jax.ShapeDtypeStruct((N_TOK, FFT_WIDTH), BF16),
                   jax.ShapeDtypeStruct((N_TOK, FFT_WIDTH), BF16)),
        scratch_shapes=[pltpu.VMEM((NJ, TM_PROJ, LANES), F32)],
        compiler_params=_cparams(("parallel",)),
        name="proj",
    )(x, emb_r, emb_c, lg, lb, m1, s1, w_in, w_fc, bcat)


def _ctx_proj_kernel(x_ref, lg_ref, lb_ref, m_ref, s_ref, w_ref, b_ref, p_ref, scr_ref):
    h = _layer_norm(x_ref[...], lg_ref[...], lb_ref[...])
    u = (h * m_ref[...] + s_ref[...]).astype(BF16)
    p = jnp.dot(u, w_ref[...], preferred_element_type=F32) + b_ref[...]
    _to_chunk_major(p, scr_ref, p_ref, N_CTX)


def _ctx_proj(ctx, lg, lb, m1, s1, w_s5, b_s5):
    return pl.pallas_call(
        _ctx_proj_kernel,
        out_shape=jax.ShapeDtypeStruct((CH, N_CHUNK_CTX, S5_WIDTH), BF16),
        scratch_shapes=[pltpu.VMEM((NJ, N_CTX, LANES), F32)],
        compiler_params=pltpu.CompilerParams(vmem_limit_bytes=VMEM_LIMIT),
        name="ctxproj",
    )(ctx, lg, lb, m1, s1, w_s5, b_s5)


def _s5_tables(lam_re, lam_im, log_dt, b_re, b_im, c_re, c_im, d_skip):
    dt = jnp.exp(log_dt)[..., None]
    zr = lam_re * dt
    zi = lam_im * dt

    def apow(m):
        m = jnp.asarray(m, F32)
        mag = jnp.exp(zr[..., None] * m)
        return mag * jnp.cos(zi[..., None] * m), mag * jnp.sin(zi[..., None] * m)

    a_re, a_im = apow(jnp.ones((1,), F32))
    a_re, a_im = a_re[..., 0], a_im[..., 0]
    den = lam_re * lam_re + lam_im * lam_im
    num_re = a_re - 1.0
    k_re = (num_re * lam_re + a_im * lam_im) / den
    k_im = (a_im * lam_re - num_re * lam_im) / den
    bb_re = k_re[..., None] * b_re - k_im[..., None] * b_im
    bb_im = k_re[..., None] * b_im + k_im[..., None] * b_re

    ks = jnp.arange(CH + 1, dtype=F32)
    pw_re, pw_im = apow(ks)
    kmag = jnp.exp(zr[:, :, None, :] * ks[None, None, :, None])
    pk_re = kmag * jnp.cos(zi[:, :, None, :] * ks[None, None, :, None])
    pk_im = kmag * jnp.sin(zi[:, :, None, :] * ks[None, None, :, None])
    bt_re, bt_im = jnp.swapaxes(b_re, 2, 3), jnp.swapaxes(b_im, 2, 3)
    bbt_re = k_re[:, :, None, :] * bt_re - k_im[:, :, None, :] * bt_im
    bbt_im = k_re[:, :, None, :] * bt_im + k_im[:, :, None, :] * bt_re

    ar, ai = pw_re[:, :, :, :CH, None], pw_im[:, :, :, :CH, None]
    cr = jnp.swapaxes(c_re, 2, 3)[:, :, :, None, :]
    ci = jnp.swapaxes(c_im, 2, 3)[:, :, :, None, :]
    ca = jnp.concatenate([cr * ar - ci * ai, -(cr * ai + ci * ar)], axis=2)
    ca = ca.reshape(2, S5_GROUPS, 2 * S5_STATE, CH * S5_GROUP)
    bbt = jnp.concatenate([bbt_re, bbt_im], axis=-1)
    taps = jnp.einsum('dghq,dgqn->dghn', bbt, ca, precision=HI)
    skip = (d_skip.reshape(S5_GROUPS, S5_GROUP, 1) * jnp.eye(S5_GROUP, dtype=F32)[None])
    taps = taps.at[0, :, :, :S5_GROUP].add(skip)
    b_c = jnp.transpose(taps.reshape(2, NJ, GPT, S5_GROUP, CH, S5_GROUP), (1, 4, 0, 2, 3, 5))
    b_c = b_c.reshape(NJ, 2 * CH, LANES, S5_GROUP)

    def q_part(d, descending):
        pr = pk_re[d][:, :CH, None, :]
        pi = pk_im[d][:, :CH, None, :]
        if descending:
            pr, pi = jnp.flip(pr, axis=1), jnp.flip(pi, axis=1)
        br = bbt_re[d][:, None, :, :]
        bi = bbt_im[d][:, None, :, :]
        return pr * br - pi * bi, pr * bi + pi * br

    def q_rows(v):
        v = v.reshape(NJ, GPT, CH, S5_GROUP, S5_STATE)
        return jnp.transpose(v, (0, 2, 1, 3, 4)).reshape(NJ, CL, S5_STATE)
    a_q = jnp.stack([q_rows(v) for v in q_part(0, True) + q_part(1, False)], axis=0)

    def p_part(d, descending):
        pr = pw_re[d][:, :, 1:CH + 1, None]
        pi = pw_im[d][:, :, 1:CH + 1, None]
        if descending:
            pr, pi = jnp.flip(pr, axis=2), jnp.flip(pi, axis=2)
        return (ct_re[d] * pr - ct_im[d] * pi, -(ct_re[d] * pi + ct_im[d] * pr))
    ct_re = jnp.swapaxes(c_re, 2, 3)[:, :, :, None, :]
    ct_im = jnp.swapaxes(c_im, 2, 3)[:, :, :, None, :]
    a_p = jnp.stack([v.reshape(NJ, GPT * S5_STATE, CH * S5_GROUP)
                     for v in p_part(0, False) + p_part(1, True)], axis=0)

    def lanes(v):
        return jnp.transpose(v.reshape(2, NJ, GPT * S5_STATE), (1, 0, 2))
    c_r, c_i = apow(jnp.full((1,), float(CH), F32))
    s_r, s_i = apow(jnp.full((1,), float(CH * SEG), F32))
    cr, ci, sr, si = (lanes(v[..., 0]) for v in (c_r, c_i, s_r, s_i))
    trans = jnp.stack([cr[:, 0], ci[:, 0], cr[:, 1], ci[:, 1],
                       sr[:, 0], si[:, 0], sr[:, 1], si[:, 1]], axis=1)

    cidx = jnp.arange(N_CHUNK_CTX, dtype=F32)
    def ctx_pow(d, m):
        m = m[None, :, None]
        zr_l = zr[d].reshape(NJ, 1, GPT * S5_STATE)
        zi_l = zi[d].reshape(NJ, 1, GPT * S5_STATE)
        mag = jnp.exp(zr_l * m)
        return mag * jnp.cos(zi_l * m), mag * jnp.sin(zi_l * m)
    ctx_w = jnp.stack(ctx_pow(0, CH * (N_CHUNK_CTX - 1 - cidx)) + ctx_pow(1, CH * cidx),
                      axis=1)
    return b_c, a_q, a_p, trans, ctx_w


def _build_operators(bc_ref, aq_ref, ap_ref, c16_ref, c64_ref, cm_ref, wm_ref, wq_ref, wp_ref):
    def expand(a, c, row_shift, col_shift):
        w = jnp.dot(a, c, preferred_element_type=F32)
        rg = (lax.broadcasted_iota(jnp.int32, (w.shape[0], 1), 0) >> row_shift) & (GPT - 1)
        cg = (lax.broadcasted_iota(jnp.int32, (1, w.shape[1]), 1) >> col_shift) & (GPT - 1)
        return jnp.where(rg == cg, w, 0.0)

    blk = [expand(bc_ref[0, kd], c16_ref[...], 4, 4) for kd in range(2 * CH)]
    for t in range(CH):
        for u in range(CH):
            b = blk[2 * (u - t)] if u > t else blk[2 * (t - u) + 1] if u < t else blk[0] + blk[1]
            wm_ref[0, t * LANES:(t + 1) * LANES, u * LANES:(u + 1) * LANES] = b.astype(BF16)
    half = GPT * S5_STATE
    for s in range(4):
        wq_ref[0, :, s * half:(s + 1) * half] = expand(aq_ref[s, 0], c64_ref[...], 4, 6).astype(BF16)
        wp_ref[0, s * half:(s + 1) * half, :] = expand(ap_ref[s, 0], cm_ref[...], 6, 4).astype(BF16)


def _s5_kernel(p_ref, pc_ref, bc_ref, aq_ref, ap_ref, c16_ref, c64_ref, cm_ref, tr_ref, cw_ref,
               y_ref, rows_ref, v_ref, wm_ref, wq_ref, wp_ref, zero_ref, zsem):
    nq = NJ
    half = GPT * S5_STATE

    zero_ref[...] = jnp.zeros_like(zero_ref)

    def clears(fn):
        def unit(b, c):
            start = pl.multiple_of((pl.program_id(0) * ZERO_PARTS + b) * ZERO_UNIT, ZERO_UNIT)
            fn(pltpu.make_async_copy(zero_ref, rows_ref.at[pl.ds(start, ZERO_UNIT)], zsem))
            return c
        lax.fori_loop(0, ZERO_PARTS, unit, 0)
    clears(lambda cp: cp.start())
    _build_operators(bc_ref, aq_ref, ap_ref, c16_ref, c64_ref, cm_ref, wm_ref, wq_ref, wp_ref)

    def chunk_rows(ref, r0, nrows):
        return jnp.concatenate([ref[t, pl.ds(r0, nrows), :] for t in range(CH)], axis=-1)

    def fill(k, c):
        r0 = pl.multiple_of(k * SEG, SEG)
        v = jnp.dot(chunk_rows(p_ref, r0, SEG), wq_ref[0], preferred_element_type=F32)
        for s in range(4 * nq):
            v_ref[s, pl.ds(k, SEG, stride=NSEG), :] = v[:, s * LANES:(s + 1) * LANES]
        return c
    lax.fori_loop(0, NSEG, fill, 0)

    vc = jnp.dot(chunk_rows(pc_ref, 0, N_CHUNK_CTX), wq_ref[0], preferred_element_type=F32)
    vfr, vfi, vbr, vbi = (vc[:, i * half:(i + 1) * half] for i in range(4))
    wfr, wfi, wbr, wbi = (cw_ref[0, i] for i in range(4))
    s0_fr = jnp.sum(wfr * vfr - wfi * vfi, axis=0, keepdims=True)
    s0_fi = jnp.sum(wfr * vfi + wfi * vfr, axis=0, keepdims=True)
    s0_br = jnp.sum(wbr * vbr - wbi * vbi, axis=0, keepdims=True)
    s0_bi = jnp.sum(wbr * vbi + wbi * vbr, axis=0, keepdims=True)

    tr = tr_ref[0]
    afr, afi, abr, abi = (jnp.broadcast_to(tr[i:i + 1], (NSEG, half)) for i in range(4))
    gfr, gfi, gbr, gbi = (tr[i:i + 1] for i in range(4, 8))

    def load_part(part, i):
        return jnp.concatenate(
            [v_ref[part * nq + q, pl.ds(pl.multiple_of(i * NSEG, NSEG), NSEG), :] for q in range(nq)],
            axis=-1)

    def store_part(part, i, val):
        for q in range(nq):
            v_ref[part * nq + q, pl.ds(pl.multiple_of(i * NSEG, NSEG), NSEG), :] = (
                val[:, q * LANES:(q + 1) * LANES])

    def step(i, carry, write):
        fr, fi, br, bi = carry
        ib = SEG - 1 - i
        ufr, ufi = load_part(0, i), load_part(1, i)
        ubr, ubi = load_part(2, ib), load_part(3, ib)
        if write:
            store_part(0, i, fr)
            store_part(1, i, fi)
            store_part(2, ib, br)
            store_part(3, ib, bi)
        return (afr * fr - afi * fi + ufr, afr * fi + afi * fr + ufi,
                abr * br - abi * bi + ubr, abr * bi + abi * br + ubi)

    zero = jnp.zeros((NSEG, half), F32)
    ffr, ffi, fbr, fbi = lax.fori_loop(0, SEG, functools.partial(step, write=False),
                                       (zero, zero, zero, zero), unroll=SCAN_UNROLL)

    rows_fr, rows_fi = [s0_fr], [s0_fi]
    for k in range(1, NSEG):
        pr, pi = rows_fr[-1], rows_fi[-1]
        rows_fr.append(gfr * pr - gfi * pi + ffr[k - 1:k])
        rows_fi.append(gfr * pi + gfi * pr + ffi[k - 1:k])
    rows_br, rows_bi = [s0_br], [s0_bi]
    for k in range(NSEG - 2, -1, -1):
        pr, pi = rows_br[0], rows_bi[0]
        rows_br.insert(0, gbr * pr - gbi * pi + fbr[k + 1:k + 2])
        rows_bi.insert(0, gbr * pi + gbi * pr + fbi[k + 1:k + 2])
    init = tuple(jnp.concatenate(r, axis=0) for r in (rows_fr, rows_fi, rows_br, rows_bi))

    lax.fori_loop(0, SEG, functools.partial(step, write=True), init, unroll=SCAN_UNROLL)

    def emit(k, c):
        r0 = pl.multiple_of(k * SEG, SEG)
        b = chunk_rows(p_ref, r0, SEG)
        sin = jnp.concatenate([v_ref[s, pl.ds(k, SEG, stride=NSEG), :] for s in range(4 * nq)], axis=-1)
        y = (jnp.dot(b, wm_ref[0], preferred_element_type=F32)
             + jnp.dot(sin.astype(BF16), wp_ref[0], preferred_element_type=F32))
        for t in range(CH):
            y_ref[t, pl.ds(r0, SEG), :] = y[:, t * LANES:(t + 1) * LANES].astype(y_ref.dtype)
        return c
    lax.fori_loop(0, NSEG, emit, 0)
    clears(lambda cp: cp.wait())


def _s5(p_t, pc_t, b_c, a_q, a_p, trans, ctx_w):
    rep = np.ones((1, GPT))
    c16 = jnp.asarray(np.kron(rep, np.eye(S5_GROUP)), F32).astype(BF16)
    c64 = jnp.asarray(np.kron(rep, np.eye(S5_STATE)), F32).astype(BF16)
    c_m = jnp.asarray(np.kron(np.eye(CH), np.kron(rep, np.eye(S5_GROUP))), F32).astype(BF16)
    b_c, a_q, a_p = b_c.astype(BF16), a_q.astype(BF16), a_p.astype(BF16)
    half = GPT * S5_STATE
    return pl.pallas_call(
        _s5_kernel,
        grid=(NJ,),
        in_specs=[pl.BlockSpec((CH, N_CHUNK, LANES), lambda j: (0, 0, j)),
                  pl.BlockSpec((CH, N_CHUNK_CTX, LANES), lambda j: (0, 0, j)),
                  pl.BlockSpec((1, 2 * CH, LANES, S5_GROUP), lambda j: (j, 0, 0, 0)),
                  pl.BlockSpec((4, 1, CL, S5_STATE), lambda j: (0, j, 0, 0)),
                  pl.BlockSpec((4, 1, half, CH * S5_GROUP), lambda j: (0, j, 0, 0)),
                  pl.BlockSpec(c16.shape, lambda j: (0, 0)),
                  pl.BlockSpec(c64.shape, lambda j: (0, 0)),
                  pl.BlockSpec(c_m.shape, lambda j: (0, 0)),
                  pl.BlockSpec((1, SUBLANES, half), lambda j: (j, 0, 0)),
                  pl.BlockSpec((1, 4, N_CHUNK_CTX, half), lambda j: (j, 0, 0, 0))],
        out_specs=(pl.BlockSpec((CH, N_CHUNK, LANES), lambda j: (0, 0, j)),
                   pl.BlockSpec(memory_space=pl.ANY)),
        out_shape=(jax.ShapeDtypeStruct((CH, N_CHUNK, S5_WIDTH), BF16),
                   jax.ShapeDtypeStruct((ROWS_ALL, D), BF16)),
        scratch_shapes=[pltpu.VMEM((4 * NJ, N_CHUNK, LANES), F32),
                        pltpu.VMEM((1, CL, CL), BF16),
                        pltpu.VMEM((1, CL, SW), BF16),
                        pltpu.VMEM((1, SW, CL), BF16),
                        pltpu.VMEM((ZERO_UNIT, D), BF16),
                        pltpu.SemaphoreType.DMA(())],
        compiler_params=_cparams(("parallel",)),
        name="s5",
    )(p_t, pc_t, b_c, a_q, a_p, c16, c64, c_m, trans, ctx_w)


def _dft_tables():
    n = np.arange(FN)
    ang = 2.0 * np.pi * np.outer(n, n) / FN
    c, s = np.cos(ang), np.sin(ang)
    st1 = np.block([[c, s], [-s, c]])
    tw = 2.0 * np.pi * np.outer(n, n) / (FN * FN)
    wr, wi = np.cos(tw), -np.sin(tw)
    fr = c[None] * wr[:, None, :] + s[None] * wi[:, None, :]
    fi = c[None] * wi[:, None, :] - s[None] * wr[:, None, :]
    st2 = np.concatenate([fr, -fi], axis=-1)
    scale = 1.0 / math.sqrt(N_TOK * FFT_DIM)
    blk_c = np.kron(np.eye(FFT_GROUPS), c) * scale
    blk_s = np.kron(np.eye(FFT_GROUPS), s) * scale
    fc = np.concatenate([blk_c, -blk_s], axis=1)
    return (jnp.asarray(st1, F32).astype(BF16), jnp.asarray(st2, F32).astype(BF16), jnp.asarray(fc, F32))


FSL = FFT_WIDTH // LANES


FBH = FB // SUBLANES


def _block_to_slabs(blk, slab_ref, first, per_half):
    for bh in range(FBH):
        val = blk[:, bh * SUBLANES:(bh + 1) * SUBLANES, :].reshape(FN * SUBLANES, FFT_WIDTH)
        for s in range(FSL):
            slab_ref[bh * per_half + first + s] = val[:, s * LANES:(s + 1) * LANES]


def _slab_rows(b, first, per_half):
    return (b // SUBLANES) * per_half + first, pl.ds(b % SUBLANES, FN, stride=SUBLANES)


def _slabs_to_block(slab_ref, first, per_half):
    halves = []
    for bh in range(FBH):
        val = jnp.concatenate([slab_ref[bh * per_half + first + s] for s in range(FSL)], axis=-1)
        halves.append(val.reshape(FN, SUBLANES, FFT_WIDTH))
    return jnp.concatenate(halves, axis=1)


def _fft1_kernel(xr_ref, xi_ref, f_ref, yr_ref, yi_ref, in_ref, out_ref):
    _block_to_slabs(xr_ref[...].astype(F32), in_ref, 0, 2 * FSL)
    _block_to_slabs(xi_ref[...].astype(F32), in_ref, FSL, 2 * FSL)
    for b in range(FB):
        def part(first):
            base, rows = _slab_rows(b, first, 2 * FSL)
            return jnp.concatenate([in_ref[base + s, rows, :] for s in range(FSL)], axis=-1)
        xs = jnp.concatenate([part(0), part(FSL)], axis=0).astype(BF16)
        y = jnp.dot(f_ref[...], xs, preferred_element_type=F32)
        base, rows = _slab_rows(b, 0, 2 * FSL)
        for s in range(FSL):
            out_ref[base + s, rows, :] = y[:FN, s * LANES:(s + 1) * LANES]
            out_ref[base + FSL + s, rows, :] = y[FN:, s * LANES:(s + 1) * LANES]
    yr_ref[...] = _slabs_to_block(out_ref, 0, 2 * FSL).astype(BF16)
    yi_ref[...] = _slabs_to_block(out_ref, FSL, 2 * FSL).astype(BF16)


def _fft1(xr, xi, st1):
    spec = pl.BlockSpec((FN, FB, FFT_WIDTH), lambda i: (0, i, 0))
    slabs = pltpu.VMEM((FBH * 2 * FSL, FN * SUBLANES, LANES), F32)
    return pl.pallas_call(
        _fft1_kernel,
        grid=(FN // FB,),
        in_specs=[spec, spec, pl.BlockSpec((2 * FN, 2 * FN), lambda i: (0, 0))],
        out_specs=(spec, spec),
        out_shape=(jax.ShapeDtypeStruct((FN, FN, FFT_WIDTH), BF16),) * 2,
        scratch_shapes=[slabs, slabs],
        compiler_params=_cparams(("parallel",)),
        name="fft1",
    )(xr.reshape(FN, FN, FFT_WIDTH), xi.reshape(FN, FN, FFT_WIDTH), st1)


def _fft2_kernel(yr_ref, yi_ref, f_ref, z_ref, out_ref):
    for b in range(FB):
        ys = jnp.concatenate([yr_ref[b * FN:(b + 1) * FN, :], yi_ref[b * FN:(b + 1) * FN, :]], axis=0)
        z = jnp.dot(f_ref[b], ys, preferred_element_type=F32)
        base, rows = _slab_rows(b, 0, FSL)
        for s in range(FSL):
            out_ref[base + s, rows, :] = z[:, s * LANES:(s + 1) * LANES]
    z_ref[...] = _slabs_to_block(out_ref, 0, FSL).astype(BF16)


def _fft2(yr, yi, st2):
    rows = pl.BlockSpec((FB * FN, FFT_WIDTH), lambda i: (i, 0))
    z = pl.pallas_call(
        _fft2_kernel,
        grid=(FN // FB,),
        in_specs=[rows, rows, pl.BlockSpec((FB, FN, 2 * FN), lambda i: (i, 0, 0))],
        out_specs=pl.BlockSpec((FN, FB, FFT_WIDTH), lambda i: (0, i, 0)),
        out_shape=jax.ShapeDtypeStruct((FN, FN, FFT_WIDTH), BF16),
        scratch_shapes=[pltpu.VMEM((FBH * FSL, FN * SUBLANES, LANES), F32)],
        compiler_params=_cparams(("parallel",)),
        name="fft2",
    )(yr.reshape(N_TOK, FFT_WIDTH), yi.reshape(N_TOK, FFT_WIDTH), st2)
    return z.reshape(N_TOK, FFT_WIDTH)


def _gelu_tanh(x):
    return 0.5 * x * (1.0 + jnp.tanh(math.sqrt(2.0 / math.pi) * (x + 0.044715 * (x * x * x))))


def _mix_kernel(x_ref, er_ref, ec_ref, lg_ref, lb_ref, m1_ref, s1_ref, wgs_ref, wgf_ref, bg_ref,
                yt_ref, zr_ref, wglu_ref, bglu_ref, wbs_ref, wbf_ref, bbf_ref, wo_ref, bo_ref,
                g1_ref, l1g_ref, l1b_ref, m2_ref, s2_ref, wr_ref, br_ref, tri_ref, etri_ref,
                h1_ref, u2_ref, pos_ref, gate_ref, cnt_ref, scr_ref):
    pos = _pos_code(er_ref, ec_ref, TM)

    def front(r0, nr):
        rows = slice(r0, r0 + nr)
        h = _layer_norm(x_ref[rows, :] + pos[rows, :], lg_ref[...], lb_ref[...])
        u_half = (h * (0.5 * m1_ref[...]) + 0.5 * s1_ref[...]).astype(BF16)

        c0, nc = r0 // CH, nr // CH
        for t in range(CH):
            for j in range(NJ):
                scr_ref[j, pl.ds(r0 + t, nc, stride=CH), :] = (
                    yt_ref[t, c0:c0 + nc, j * LANES:(j + 1) * LANES].astype(F32))
        ys = jnp.concatenate([scr_ref[j, rows, :] for j in range(NJ)], axis=-1)
        z = jnp.dot(_gelu_tanh(ys).astype(BF16), wglu_ref[...], preferred_element_type=F32) + bglu_ref[...]
        glu = (z[:, :S5_WIDTH] * _sigmoid(z[:, S5_WIDTH:])).astype(BF16)
        t_s5 = jnp.tanh(jnp.dot(u_half, wgs_ref[...], preferred_element_type=F32) + 0.5 * bg_ref[:, :D])
        t_fft = jnp.tanh(jnp.dot(u_half, wgf_ref[...], preferred_element_type=F32) + 0.5 * bg_ref[:, D:])
        y_s5 = jnp.dot(glu, wbs_ref[...], preferred_element_type=F32)
        y_fft = jnp.dot(zr_ref[rows, :], wbf_ref[...], preferred_element_type=F32) + bbf_ref[...]
        mixed2 = ((t_s5 + 1.0) * y_s5 + (t_fft + 1.0) * y_fft).astype(BF16)
        y = 0.5 * jnp.dot(mixed2, wo_ref[...], preferred_element_type=F32) + bo_ref[...]
        h1 = _layer_norm(ALPHA * h + g1_ref[...] * y, l1g_ref[...], l1b_ref[...])
        h1_ref[rows, :] = h1
        u2 = h1 * m2_ref[...] + s2_ref[...]
        u2_ref[rows, :] = u2.astype(BF16)
        u_hi = u2.astype(BF16)
        u_lo = (u2 - u_hi.astype(F32)).astype(BF16)

        def nt(a, b):
            return lax.dot_general(a, b, (((1,), (1,)), ((), ())), preferred_element_type=F32)
        return nt(wr_ref[0], u_hi) + nt(wr_ref[0], u_lo) + nt(wr_ref[1], u_hi)

    logits = front(0, TM) + br_ref[:, 0:1]
    eidx = lax.broadcasted_iota(jnp.int32, (N_EXPERTS, TM), 0)
    vals, hots = [], []
    cur = logits
    for _k in range(TOP_K):
        m = jnp.max(cur, axis=0, keepdims=True)
        sel = jnp.min(jnp.where(cur == m, eidx, N_EXPERTS), axis=0, keepdims=True)
        hot = eidx == sel
        cur = jnp.where(hot, -jnp.inf, cur)
        vals.append(m)
        hots.append(hot)
    exps = [jnp.exp(v - vals[0]) for v in vals]
    den = exps[0] + exps[1] + exps[2] + exps[3]
    gate4 = jnp.concatenate([e / den for e in exps], axis=0)

    hot_sum = (hots[0] | hots[1] | hots[2] | hots[3]).astype(F32)
    before = jnp.dot(hot_sum.astype(BF16), tri_ref[...], preferred_element_type=F32)
    cnt = jnp.broadcast_to(jnp.sum(hot_sum, axis=1, keepdims=True), (N_EXPERTS, LANES))
    cnt8 = jnp.floor((cnt + (SEG_ALIGN - 1)) * (1.0 / SEG_ALIGN)) * SEG_ALIGN
    seg0 = jnp.dot(etri_ref[...], cnt8.astype(BF16), preferred_element_type=F32)
    tot = seg0[:, 0:1] + before
    pos4 = jnp.concatenate(
        [jnp.sum(jnp.where(hk, tot, 0.0), axis=0, keepdims=True) for hk in hots], axis=0)
    pos_ref[...] = pos4.astype(jnp.int32)
    cnt_ref[0] = cnt

    gate_ref[...] = gate4


def _mix(x, emb_r, emb_c, lg, lb, m1, s1, w_in, bg, y_t, zr, wglu, bglu, wbs, wbf, bbf, wo, bo,
         g1, l1g, l1b, m2, s2, wr_t, br, tri, etri):
    gate_cols = (S5_WIDTH + FFT_WIDTH) // D
    vec = pl.BlockSpec((1, D), lambda i: (0, 0))

    def full(a):
        return pl.BlockSpec(a.shape, lambda i: (0,) * a.ndim)
    return pl.pallas_call(
        _mix_kernel,
        grid=(N_TOK // TM,),
        in_specs=[pl.BlockSpec((TM, D), lambda i: (i, 0)),
                  pl.BlockSpec((TM // GRID_W, D // 2), lambda i: (i, 0)),
                  pl.BlockSpec((GRID_W, D // 2), lambda i: (0, 0)),
                  vec, vec, vec, vec,
                  pl.BlockSpec((D, D), lambda i: (0, gate_cols)),
                  pl.BlockSpec((D, D), lambda i: (0, gate_cols + 1)), full(bg),
                  pl.BlockSpec((CH, TM // CH, S5_WIDTH), lambda i: (0, i, 0)),
                  pl.BlockSpec((TM, FFT_WIDTH), lambda i: (i, 0)),
                  full(wglu), full(bglu), full(wbs), full(wbf), full(bbf), full(wo), full(bo),
                  vec, vec, vec, vec, vec, full(wr_t), full(br), full(tri), full(etri)],
        out_specs=(pl.BlockSpec((TM, D), lambda i: (i, 0)),
                   pl.BlockSpec((TM, D), lambda i: (i, 0)),
                   pl.BlockSpec((TOP_K, TM), lambda i: (0, i)),
                   pl.BlockSpec((TOP_K, TM), lambda i: (0, i)),
                   pl.BlockSpec((1, N_EXPERTS, LANES), lambda i: (i, 0, 0))),
        out_shape=(jax.ShapeDtypeStruct((N_TOK, D), F32),
                   jax.ShapeDtypeStruct((N_TOK, D), BF16),
                   jax.ShapeDtypeStruct((TOP_K, N_TOK), jnp.int32),
                   jax.ShapeDtypeStruct((TOP_K, N_TOK), F32),
                   jax.ShapeDtypeStruct((N_TILES, N_EXPERTS, LANES), F32)),
        scratch_shapes=[pltpu.VMEM((NJ, TM, LANES), F32)],
        compiler_params=_cparams(("parallel",)),
        name="mix",
    )(x, emb_r, emb_c, lg, lb, m1, s1, w_in, w_in, bg, y_t, zr, wglu, bglu, wbs, wbf, bbf, wo, bo,
      g1, l1g, l1b, m2, s2, wr_t, br, tri, etri)


def _on_parity(i, fn):
    @pl.when(i % 2 == 0)
    def _():
        fn(0)

    @pl.when(i % 2 == 1)
    def _():
        fn(1)


def _dispatch_kernel(dprev_ref, dest_ref, pos_ref, u_ref, zeroed_ref, buf_ref, sorted_ref, sems):
    del zeroed_ref
    i = pl.program_id(0)

    @pl.when(i == 0)
    def _():
        sorted_ref[...] = jnp.zeros_like(sorted_ref)

    def chunk_copy(slot, table_ref, j):
        dst = pl.multiple_of(table_ref[0, 0, j], SEG_ALIGN)
        return pltpu.make_async_copy(sorted_ref.at[slot, pl.ds(j * SEG_ALIGN, SEG_ALIGN)],
                                     buf_ref.at[pl.ds(dst, SEG_ALIGN)], sems.at[slot])

    def drain(slot):
        pltpu.make_async_copy(sorted_ref.at[slot], buf_ref.at[pl.ds(0, CAP)], sems.at[slot]).wait()

    def run(slot):
        pos = pos_ref[...]
        u = u_ref[...]
        n_rb = CAP // SORT_BLOCK
        per_rb = NCHK // (n_rb // 2)
        for rb in range(n_rb):
            for j in range(rb * per_rb, min((rb + 1) * per_rb, NCHK)):
                chunk_copy(1 - slot, dprev_ref, j).start()
            rows = lax.broadcasted_iota(jnp.int32, (SORT_BLOCK, TM), 0) + rb * SORT_BLOCK
            hit = rows == pos[0:1]
            for k in range(1, TOP_K):
                hit = hit | (rows == pos[k:k + 1])
            onehot = jnp.where(hit, 1.0, 0.0).astype(BF16)
            sorted_ref[slot, rb * SORT_BLOCK:(rb + 1) * SORT_BLOCK, :] = jnp.dot(
                onehot, u, preferred_element_type=F32).astype(BF16)
        drain(1 - slot)

        @pl.when(i == N_TILES - 1)
        def _():
            def issue(j, c):
                chunk_copy(slot, dest_ref, j).start()
                return c
            lax.fori_loop(0, NCHK, issue, 0)
            drain(slot)
    _on_parity(i, run)


def _dispatch(chunk_table, pos_t, u2, zeroed):
    return pl.pallas_call(
        _dispatch_kernel,
        grid=(N_TILES,),
        in_specs=[pl.BlockSpec((1, 1, NCHK), lambda i: (i, 0, 0), memory_space=pltpu.SMEM),
                  pl.BlockSpec((1, 1, NCHK), lambda i: (i + 1, 0, 0), memory_space=pltpu.SMEM),
                  pl.BlockSpec((TOP_K, TM), lambda i: (0, i)),
                  pl.BlockSpec((TM, D), lambda i: (i, 0)),
                  pl.BlockSpec(memory_space=pl.ANY)],
        out_specs=pl.BlockSpec(memory_space=pl.ANY),
        out_shape=jax.ShapeDtypeStruct((ROWS_ALL, D), BF16),
        scratch_shapes=[pltpu.VMEM((2, CAP, D), BF16),
                        pltpu.SemaphoreType.DMA((2,))],
        input_output_aliases={4: 0},
        compiler_params=_cparams(("arbitrary",)),
        name="dispatch",
    )(chunk_table, chunk_table, pos_t, u2, zeroed)


def _ffn_kernel(be_ref, nu_ref, run_ref, nxt_ref, valid_ref, x_ref, wu_hbm, bu_ref, wd_hbm, bd_ref,
                y_ref, wu_ref, wd_ref, wub_ref, wdb_ref, sems):
    i = pl.program_id(0)
    used = i < nu_ref[0]

    def weight_copies(e, slot):
        return (pltpu.make_async_copy(wu_hbm.at[e], wu_ref.at[slot], sems.at[slot]),
                pltpu.make_async_copy(wd_hbm.at[e], wd_ref.at[slot], sems.at[slot]))

    @pl.when(used)
    def _():
        run = run_ref[i]

        @pl.when(run >= 0)
        def _():
            def open_run(slot):
                @pl.when(run == 0)
                def _():
                    for cp in weight_copies(be_ref[i], slot):
                        cp.start()

                @pl.when(nxt_ref[i] >= 0)
                def _():
                    for cp in weight_copies(nxt_ref[i], 1 - slot):
                        cp.start()
                for cp in weight_copies(be_ref[i], slot):
                    cp.wait()
                wub_ref[...] = wu_ref[slot].astype(BF16)
                wdb_ref[...] = wd_ref[slot].astype(BF16)
            _on_parity(run, open_run)

        def expert_rows(r0, nr):
            rows = slice(r0, r0 + nr)
            e = be_ref[i]
            h = (jnp.dot(x_ref[rows, :], wub_ref[...], preferred_element_type=F32)
                 + bu_ref[pl.ds(e, 1), :])
            h_glu = jnp.minimum(h[:, :D], SWIGLU_LIMIT)
            h_lin = jnp.clip(h[:, D:], -SWIGLU_LIMIT, SWIGLU_LIMIT)
            act = (h_glu * _sigmoid(SWIGLU_ALPHA * h_glu) * (h_lin + 1.0)).astype(BF16)
            y_ref[rows, :] = (jnp.dot(act, wdb_ref[...], preferred_element_type=F32)
                              + bd_ref[pl.ds(e, 1), :]).astype(BF16)

        valid = valid_ref[i]

        @pl.when(valid == BM)
        def _():
            expert_rows(0, BM)

        @pl.when(valid < BM)
        def _():
            for h0 in range(0, BM, FFN_HALF):
                @pl.when(valid >= h0 + FFN_HALF)
                def _(h0=h0):
                    expert_rows(h0, FFN_HALF)

                @pl.when(valid < h0 + FFN_HALF)
                def _(h0=h0):
                    for r0 in range(h0, h0 + FFN_HALF, FFN_TAIL):
                        @pl.when(r0 < valid)
                        def _(r0=r0):
                            expert_rows(r0, FFN_TAIL)

                        @pl.when(r0 >= valid)
                        def _(r0=r0):
                            y_ref[r0:r0 + FFN_TAIL, :] = jnp.zeros((FFN_TAIL, D), BF16)


def _ffn(block_expert, n_used, run_id, next_expert, valid, buf, w_up, b_up, w_down, b_down):
    def blk(i, be, nu, *_):
        return jnp.minimum(i, nu[0] - 1)
    return pl.pallas_call(
        _ffn_kernel,
        grid_spec=pltpu.PrefetchScalarGridSpec(
            num_scalar_prefetch=5,
            grid=(N_BLOCKS_ALL,),
            in_specs=[pl.BlockSpec((BM, D), lambda i, *s: (blk(i, *s), 0)),
                      pl.BlockSpec(memory_space=pl.ANY),
                      pl.BlockSpec((N_EXPERTS, 2 * D), lambda i, *s: (0, 0)),
                      pl.BlockSpec(memory_space=pl.ANY),
                      pl.BlockSpec((N_EXPERTS, D), lambda i, *s: (0, 0))],
            out_specs=pl.BlockSpec((BM, D), lambda i, *s: (blk(i, *s), 0)),
            scratch_shapes=[pltpu.VMEM((2, D, 2 * D), F32),
                            pltpu.VMEM((2, D, D), F32),
                            pltpu.VMEM((D, 2 * D), BF16),
                            pltpu.VMEM((D, D), BF16),
                            pltpu.SemaphoreType.DMA((2,))]),
        out_shape=jax.ShapeDtypeStruct((ROWS_ALL, D), BF16),
        input_output_aliases={5: 0},
        compiler_params=_cparams(("arbitrary",)),
        name="ffn",
    )(block_expert, n_used, run_id, next_expert, valid, buf, w_up, b_up, w_down, b_down)


def _combine_kernel(dest_ref, dnext_ref, y_ref, h1_ref, pos_ref, gate_ref, g2_ref, lg_ref, lb_ref,
                    o_ref, sorted_ref, sems):
    i = pl.program_id(0)

    def chunk_copy(slot, table_ref, j):
        src = pl.multiple_of(table_ref[0, 0, j], SEG_ALIGN)
        return pltpu.make_async_copy(y_ref.at[pl.ds(src, SEG_ALIGN)],
                                     sorted_ref.at[slot, pl.ds(j * SEG_ALIGN, SEG_ALIGN)],
                                     sems.at[slot])

    def drain(slot):
        pltpu.make_async_copy(y_ref.at[pl.ds(0, CAP)], sorted_ref.at[slot], sems.at[slot]).wait()

    @pl.when(i == 0)
    def _():
        def issue(j, c):
            chunk_copy(0, dest_ref, j).start()
            return c
        lax.fori_loop(0, NCHK, issue, 0)

    def run(slot):
        drain(slot)

        pos = pos_ref[...]
        gate = gate_ref[...]
        m = jnp.zeros((TM, D), F32)
        n_cb = CAP // CAP_BLOCK
        per_cb = NCHK // (n_cb // 2)
        for cb in range(n_cb):
            for j in range(cb * per_cb, min((cb + 1) * per_cb, NCHK)):
                chunk_copy(1 - slot, dnext_ref, j).start()
            rws = lax.broadcasted_iota(jnp.int32, (CAP_BLOCK, TM), 0) + cb * CAP_BLOCK
            g = jnp.where(rws == pos[0:1], gate[0:1], 0.0)
            for k in range(1, TOP_K):
                g = g + jnp.where(rws == pos[k:k + 1], gate[k:k + 1], 0.0)
            rows = sorted_ref[slot, cb * CAP_BLOCK:(cb + 1) * CAP_BLOCK, :]
            m = m + lax.dot_general(g.astype(BF16), rows, (((0,), (0,)), ((), ())),
                                    preferred_element_type=F32)
        o_ref[...] = _layer_norm(ALPHA * h1_ref[...] + g2_ref[...] * m, lg_ref[...], lb_ref[...])

        @pl.when(i == N_TILES - 1)
        def _():
            drain(1 - slot)
    _on_parity(i, run)


def _combine(chunk_table, y_buf, h1, pos_t, gate_t, g2, lg, lb):
    vec = pl.BlockSpec((1, D), lambda i: (0, 0))
    return pl.pallas_call(
        _combine_kernel,
        grid_spec=pltpu.PrefetchScalarGridSpec(
            num_scalar_prefetch=0,
            grid=(N_TILES,),
            in_specs=[pl.BlockSpec((1, 1, NCHK), lambda i: (i + 1, 0, 0), memory_space=pltpu.SMEM),
                      pl.BlockSpec((1, 1, NCHK), lambda i: (i + 2, 0, 0), memory_space=pltpu.SMEM),
                      pl.BlockSpec(memory_space=pl.ANY),
                      pl.BlockSpec((TM, D), lambda i: (i, 0)),
                      pl.BlockSpec((TOP_K, TM), lambda i: (0, i)),
                      pl.BlockSpec((TOP_K, TM), lambda i: (0, i)),
                      vec, vec, vec],
            out_specs=pl.BlockSpec((TM, D), lambda i: (i, 0)),
            scratch_shapes=[pltpu.VMEM((2, CAP, D), BF16),
                            pltpu.SemaphoreType.DMA((2,))]),
        out_shape=jax.ShapeDtypeStruct((N_TOK, D), F32),
        compiler_params=_cparams(("arbitrary",)),
        name="combine",
    )(chunk_table, chunk_table, y_buf, h1, pos_t, gate_t, g2, lg, lb)


def _sincos_tables():
    q = D // 4
    omega = 1.0 / (10000.0 ** (np.arange(q) / q))

    def emb(n):
        ang = np.arange(n)[:, None] * omega[None, :]
        return jnp.asarray(np.concatenate([np.sin(ang), np.cos(ang)], axis=-1), F32)
    return emb(N_TOK // GRID_W), emb(GRID_W)


def kernel(x, c, ctx, c_ctx, ln_in_g, ln_in_b, w_ada, b_ada, w_in, b_in, s5_lambda_re, s5_lambda_im, s5_log_dt, s5_b_re, s5_b_im, s5_c_re, s5_c_im, s5_d, w_glu, b_glu, w_br_s5, w_br_fft, b_br_fft, w_out, b_out, ln1_g, ln1_b, w_router, b_router, w_up, b_up, w_down, b_down, ln2_g, ln2_b):
    assert x.shape == (1, N_TOK, D) and ctx.shape == (1, N_CTX, D) and w_ada.shape[0] == 1
    row = lambda v: v.reshape(1, -1).astype(F32)

    cc = jnp.concatenate([c.reshape(1, D), c_ctx.reshape(1, D), jnp.zeros((SUBLANES - 2, D), F32)], axis=0)
    ada = _ada(cc, w_ada[0], row(b_ada[0]))
    sh1, sc1, g1, sh2, sc2, g2 = (ada[0:1, k * D:(k + 1) * D] for k in range(6))
    sh1c, sc1c = ada[1:2, 0:D], ada[1:2, D:2 * D]

    emb_r, emb_c = _sincos_tables()
    st1, st2, fc = _dft_tables()
    lg, lb = row(ln_in_g), row(ln_in_b)

    w_in_bf = w_in[0].astype(BF16)
    b_s5 = row(b_in[0][:S5_WIDTH])
    b_fft8 = jnp.concatenate([row(b_in[0][S5_WIDTH:S5_WIDTH + FFT_WIDTH]),
                              jnp.zeros((SUBLANES - 1, FFT_WIDTH), F32)], axis=0)
    b_g = row(b_in[0][S5_WIDTH + FFT_WIDTH:])
    w_fc, b_fc = _fft_weights(w_in[0], b_fft8, fc)
    bcat = jnp.concatenate([b_s5, b_fc[0:1]], axis=1)

    x2 = x[0]
    p_t, xr, xi = _proj(x2, emb_r, emb_c, lg, lb, 1.0 + sc1, sh1, w_in_bf, w_fc, bcat)
    pc_t = _ctx_proj(ctx[0], lg, lb, 1.0 + sc1c, sh1c, w_in_bf[:, :S5_WIDTH], b_s5)

    b_c, a_q, a_p, trans, ctx_w = _s5_tables(
        s5_lambda_re[0], s5_lambda_im[0], s5_log_dt[0], s5_b_re[0], s5_b_im[0],
        s5_c_re[0], s5_c_im[0], s5_d[0])
    y_t, zeroed = _s5(p_t, pc_t, b_c, a_q, a_p, trans, ctx_w)

    yr, yi = _fft1(xr, xi, st1)
    zr = _fft2(yr, yi, st2)

    tri = jnp.asarray(np.arange(TM)[:, None] < np.arange(TM)[None, :], BF16)
    br = jnp.broadcast_to(b_router[0].reshape(N_EXPERTS, 1), (N_EXPERTS, LANES))
    etri = jnp.asarray(np.arange(N_EXPERTS)[:, None] > np.arange(N_EXPERTS)[None, :], BF16)
    wr_t = jnp.transpose(w_router[0])
    wr_hi = wr_t.astype(BF16)
    wr_split = jnp.stack([wr_hi, (wr_t - wr_hi.astype(F32)).astype(BF16)], axis=0)
    h1, u2, pos_t, gate_t, counts = _mix(
        x2, emb_r, emb_c, lg, lb, 1.0 + sc1, sh1, w_in_bf, b_g, y_t, zr,
        w_glu[0].astype(BF16), row(b_glu[0]), w_br_s5[0].astype(BF16), w_br_fft[0].astype(BF16),
        row(b_br_fft[0]), w_out[0].astype(BF16), row(b_out[0]), g1, row(ln1_g[0]), row(ln1_b[0]),
        1.0 + sc2, sh2, wr_split, br, tri, etri)

    cnt = counts[:, :, 0].astype(jnp.int32)
    seg = (cnt + SEG_ALIGN - 1) // SEG_ALIGN * SEG_ALIGN
    seg_end = jnp.cumsum(seg, axis=1)
    seg_start = seg_end - seg
    padded = (jnp.sum(seg, axis=0) + BM - 1) // BM * BM
    pad_ends = jnp.cumsum(padded)
    seg_dest = (pad_ends - padded)[None, :] + jnp.cumsum(seg, axis=0) - seg
    chunk_row = jnp.arange(NCHK, dtype=jnp.int32) * SEG_ALIGN
    chunk_exp = jnp.minimum(jnp.sum(chunk_row[None, :, None] >= seg_end[:, None, :], axis=-1),
                            N_EXPERTS - 1)
    own = chunk_exp[:, :, None] == jnp.arange(N_EXPERTS, dtype=jnp.int32)[None, None, :]
    chunk_dest = (jnp.sum(jnp.where(own, (seg_dest - seg_start)[:, None, :], 0), axis=-1)
                  + chunk_row[None, :]).astype(jnp.int32).reshape(N_TILES, 1, NCHK)
    nchk = (seg_end[:, -1] // SEG_ALIGN).astype(jnp.int32)
    block_start = jnp.arange(N_BLOCKS_ALL, dtype=jnp.int32) * BM
    block_expert = jnp.minimum(jnp.sum(block_start[:, None] >= pad_ends[None, :], axis=1),
                               N_EXPERTS - 1).astype(jnp.int32)
    n_used = (pad_ends[-1:] // BM).astype(jnp.int32)
    opens = (block_start < pad_ends[-1]) & (
        block_expert != jnp.concatenate([jnp.full((1,), -1, jnp.int32), block_expert[:-1]]))
    run_id = jnp.where(opens, jnp.cumsum(opens.astype(jnp.int32)) - 1, -1).astype(jnp.int32)
    experts = jnp.arange(N_EXPERTS, dtype=jnp.int32)
    later = (experts[None, :] > block_expert[:, None]) & (padded[None, :] > 0)
    next_expert = jnp.min(jnp.where(later, experts[None, :], N_EXPERTS), axis=1)
    next_expert = jnp.where(next_expert < N_EXPERTS, next_expert, -1).astype(jnp.int32)

    spare = (ROWS + chunk_row)[None, None, :]
    chunk_table = jnp.concatenate(
        [spare,
         jnp.where(chunk_row[None, None, :] < (nchk * SEG_ALIGN)[:, None, None], chunk_dest, spare),
         spare], axis=0).astype(jnp.int32)
    fill_ends = (pad_ends - padded + jnp.sum(seg, axis=0)).astype(jnp.int32)
    buf = _dispatch(chunk_table, pos_t, u2, zeroed)
    mine = block_expert[:, None] == experts[None, :]
    filled = jnp.sum(jnp.where(mine, fill_ends[None, :], 0), axis=1)
    valid = jnp.clip(filled - block_start, 0, BM).astype(jnp.int32)
    y_buf = _ffn(block_expert, n_used, run_id, next_expert, valid, buf, w_up[0],
                 b_up[0], w_down[0], b_down[0])
    out = _combine(chunk_table, y_buf, h1, pos_t, gate_t, g2, row(ln2_g[0]), row(ln2_b[0]))
    return out.reshape(1, N_TOK, D)
```

```python
import functools
import math

import jax
import jax.numpy as jnp
import numpy as np
from jax import lax
from jax.experimental import pallas as pl
from jax.experimental.pallas import tpu as pltpu

F32 = jnp.float32
BF16 = jnp.bfloat16
HI = lax.Precision.HIGHEST

D = 1024
N_TOK = 16384
N_CTX = 256
GRID_W = 64
S5_GROUP = 16
S5_GROUPS = 32
S5_STATE = 64
S5_WIDTH = 512
FFT_GROUPS = 4
FFT_DIM = 128
FFT_WIDTH = 512
N_EXPERTS = 32
TOP_K = 4
LN_EPS = 1e-5
ALPHA = 2.0 ** 0.25
SWIGLU_ALPHA = 1.702
SWIGLU_LIMIT = 7.0

LANES = 128
SUBLANES = 8
VMEM_LIMIT = 56 * 1024 * 1024

CH = 8
N_CHUNK = N_TOK // CH
N_CHUNK_CTX = N_CTX // CH
NSEG = SUBLANES
SEG = N_CHUNK // NSEG
SCAN_UNROLL = 8
GPT = LANES // S5_GROUP
NJ = S5_WIDTH // LANES
CL = CH * LANES
SW = 4 * GPT * S5_STATE

FN = 128
FB = 16

TM = 512
TM_PROJ = 1024
N_TILES = N_TOK // TM
BM = 1024
FFN_HALF = 512
FFN_TAIL = 256
N_SLOTS = N_TOK * TOP_K
SEG_ALIGN = 2 * SUBLANES
CAP_BLOCK = 256
SORT_BLOCK = 128
CAP = -(-(TOP_K * TM + N_EXPERTS * (SEG_ALIGN - 1)) // CAP_BLOCK) * CAP_BLOCK
NCHK = CAP // SEG_ALIGN
N_BLOCKS = -(-(N_SLOTS + N_TILES * N_EXPERTS * (SEG_ALIGN - 1)) // BM) + N_EXPERTS
ROWS = N_BLOCKS * BM
N_BLOCKS_ALL = N_BLOCKS + -(-CAP // BM)
ROWS_ALL = N_BLOCKS_ALL * BM
ZERO_UNIT = 512
ZERO_PARTS = ROWS_ALL // (NJ * ZERO_UNIT)
assert ZERO_UNIT * ZERO_PARTS * NJ == ROWS_ALL


def _cparams(sem):
    return pltpu.CompilerParams(dimension_semantics=sem, vmem_limit_bytes=VMEM_LIMIT)


def _layer_norm(x, g, b):
    mu = jnp.mean(x, axis=-1, keepdims=True)
    xc = x - mu
    var = jnp.mean(xc * xc, axis=-1, keepdims=True)
    return xc * lax.rsqrt(var + LN_EPS) * g + b


def _sigmoid(x):
    return 0.5 * jnp.tanh(0.5 * x) + 0.5


def _ada_kernel(c_ref, w_ref, b_ref, o_ref):
    c = c_ref[...]
    s = c * _sigmoid(c)
    w = w_ref[...]
    s_hi, w_hi = s.astype(BF16), w.astype(BF16)
    s_lo = (s - s_hi.astype(F32)).astype(BF16)
    w_lo = (w - w_hi.astype(F32)).astype(BF16)

    def mm(a, b):
        return jnp.dot(a, b, preferred_element_type=F32)
    o_ref[...] = mm(s_hi, w_hi) + mm(s_lo, w_hi) + mm(s_hi, w_lo) + b_ref[...]


def _ada(cc, w_ada, b_ada):
    nb = 4
    wb = 6 * D // nb
    return pl.pallas_call(
        _ada_kernel,
        grid=(nb,),
        in_specs=[pl.BlockSpec((SUBLANES, D), lambda i: (0, 0)),
                  pl.BlockSpec((D, wb), lambda i: (0, i)),
                  pl.BlockSpec((1, wb), lambda i: (0, i))],
        out_specs=pl.BlockSpec((SUBLANES, wb), lambda i: (0, i)),
        out_shape=jax.ShapeDtypeStruct((SUBLANES, 6 * D), F32),
        compiler_params=_cparams(("parallel",)),
        name="ada",
    )(cc, w_ada, b_ada)


def _fftw_kernel(w_ref, b_ref, f_ref, wo_ref, bo_ref):
    f = f_ref[...]
    w = w_ref[...]
    w_hi, f_hi = w.astype(BF16), f.astype(BF16)
    w_lo = (w - w_hi.astype(F32)).astype(BF16)
    f_lo = (f - f_hi.astype(F32)).astype(BF16)

    def mm(a, b):
        return jnp.dot(a, b, preferred_element_type=F32)
    wo_ref[...] = (mm(w_hi, f_hi) + mm(w_lo, f_hi) + mm(w_hi, f_lo)).astype(BF16)
    bo_ref[...] = jnp.dot(b_ref[...], f, preferred_element_type=F32, precision=HI)


def _fft_weights(w_in, b_fft8, fc):
    def full(a):
        return pl.BlockSpec(a.shape, lambda i: (0,) * a.ndim)
    outs = (jax.ShapeDtypeStruct((D, 2 * FFT_WIDTH), BF16),
            jax.ShapeDtypeStruct((SUBLANES, 2 * FFT_WIDTH), F32))
    return pl.pallas_call(
        _fftw_kernel,
        grid=(1,),
        in_specs=[pl.BlockSpec((D, FFT_WIDTH), lambda i: (0, S5_WIDTH // FFT_WIDTH)),
                  full(b_fft8), full(fc)],
        out_specs=tuple(pl.BlockSpec(o.shape, lambda i: (0, 0)) for o in outs),
        out_shape=outs,
        compiler_params=_cparams(("arbitrary",)),
        name="fftw",
    )(w_in, b_fft8, fc)


def _pos_code(er_ref, ec_ref, tm):
    nr = tm // GRID_W
    er = er_ref[...]
    row = jnp.broadcast_to(er[:, None, :], (nr, GRID_W, D // 2)).reshape(tm, D // 2)
    col = jnp.concatenate([ec_ref[...]] * nr, axis=0)
    return jnp.concatenate([row, col], axis=-1)


def _to_chunk_major(val, scr_ref, out_ref, tm):
    for j in range(NJ):
        scr_ref[j] = val[:, j * LANES:(j + 1) * LANES]
    for t in range(CH):
        for j in range(NJ):
            piece = scr_ref[j, pl.ds(t, tm // CH, stride=CH), :]
            out_ref[t, :, j * LANES:(j + 1) * LANES] = piece.astype(out_ref.dtype)


def _proj_kernel(x_ref, er_ref, ec_ref, lg_ref, lb_ref, m_ref, s_ref, ws_ref, wf_ref, b_ref,
                 p_ref, xr_ref, xi_ref, scr_ref):
    x = x_ref[...] + _pos_code(er_ref, ec_ref, TM_PROJ)
    h = _layer_norm(x, lg_ref[...], lb_ref[...])
    u = (h * m_ref[...] + s_ref[...]).astype(BF16)
    p_s5 = jnp.dot(u, ws_ref[...], preferred_element_type=F32) + b_ref[:, :S5_WIDTH]
    _to_chunk_major(p_s5, scr_ref, p_ref, TM_PROJ)
    p_f = jnp.dot(u, wf_ref[...], preferred_element_type=F32) + b_ref[:, S5_WIDTH:]
    xr_ref[...] = p_f[:, :FFT_WIDTH].astype(BF16)
    xi_ref[...] = p_f[:, FFT_WIDTH:].astype(BF16)


def _proj(x, emb_r, emb_c, lg, lb, m1, s1, w_in, w_fc, bcat):
    nw = bcat.shape[1]
    vec = pl.BlockSpec((1, D), lambda i: (0, 0))
    return pl.pallas_call(
        _proj_kernel,
        grid=(N_TOK // TM_PROJ,),
        in_specs=[pl.BlockSpec((TM_PROJ, D), lambda i: (i, 0)),
                  pl.BlockSpec((TM_PROJ // GRID_W, D // 2), lambda i: (i, 0)),
                  pl.BlockSpec((GRID_W, D // 2), lambda i: (0, 0)),
                  vec, vec, vec, vec,
                  pl.BlockSpec((D, S5_WIDTH), lambda i: (0, 0)),
                  pl.BlockSpec((D, 2 * FFT_WIDTH), lambda i: (0, 0)),
                  pl.BlockSpec((1, nw), lambda i: (0, 0))],
        out_specs=(pl.BlockSpec((CH, TM_PROJ // CH, S5_WIDTH), lambda i: (0, i, 0)),
                   pl.BlockSpec((TM_PROJ, FFT_WIDTH), lambda i: (i, 0)),
                   pl.BlockSpec((TM_PROJ, FFT_WIDTH), lambda i: (i, 0))),
        out_shape=(jax.ShapeDtypeStruct((CH, N_CHUNK, S5_WIDTH), BF16),
                   jax.ShapeDtypeStruct((N_TOK, FFT_WIDTH), BF16),
                   jax.ShapeDtypeStruct((N_TOK, FFT_WIDTH), BF16)),
        scratch_shapes=[pltpu.VMEM((NJ, TM_PROJ, LANES), F32)],
        compiler_params=_cparams(("parallel",)),
        name="proj",
    )(x, emb_r, emb_c, lg, lb, m1, s1, w_in, w_fc, bcat)


def _ctx_proj_kernel(x_ref, lg_ref, lb_ref, m_ref, s_ref, w_ref, b_ref, p_ref, scr_ref):
    h = _layer_norm(x_ref[...], lg_ref[...], lb_ref[...])
    u = (h * m_ref[...] + s_ref[...]).astype(BF16)
    p = jnp.dot(u, w_ref[...], preferred_element_type=F32) + b_ref[...]
    _to_chunk_major(p, scr_ref, p_ref, N_CTX)


def _ctx_proj(ctx, lg, lb, m1, s1, w_s5, b_s5):
    return pl.pallas_call(
        _ctx_proj_kernel,
        out_shape=jax.ShapeDtypeStruct((CH, N_CHUNK_CTX, S5_WIDTH), BF16),
        scratch_shapes=[pltpu.VMEM((NJ, N_CTX, LANES), F32)],
        compiler_params=pltpu.CompilerParams(vmem_limit_bytes=VMEM_LIMIT),
        name="ctxproj",
    )(ctx, lg, lb, m1, s1, w_s5, b_s5)


def _s5_tables(lam_re, lam_im, log_dt, b_re, b_im, c_re, c_im, d_skip):
    dt = jnp.exp(log_dt)[..., None]
    zr = lam_re * dt
    zi = lam_im * dt

    def apow(m):
        m = jnp.asarray(m, F32)
        mag = jnp.exp(zr[..., None] * m)
        return mag * jnp.cos(zi[..., None] * m), mag * jnp.sin(zi[..., None] * m)

    a_re, a_im = apow(jnp.ones((1,), F32))
    a_re, a_im = a_re[..., 0], a_im[..., 0]
    den = lam_re * lam_re + lam_im * lam_im
    num_re = a_re - 1.0
    k_re = (num_re * lam_re + a_im * lam_im) / den
    k_im = (a_im * lam_re - num_re * lam_im) / den
    bb_re = k_re[..., None] * b_re - k_im[..., None] * b_im
    bb_im = k_re[..., None] * b_im + k_im[..., None] * b_re

    ks = jnp.arange(CH + 1, dtype=F32)
    pw_re, pw_im = apow(ks)
    kmag = jnp.exp(zr[:, :, None, :] * ks[None, None, :, None])
    pk_re = kmag * jnp.cos(zi[:, :, None, :] * ks[None, None, :, None])
    pk_im = kmag * jnp.sin(zi[:, :, None, :] * ks[None, None, :, None])
    bt_re, bt_im = jnp.swapaxes(b_re, 2, 3), jnp.swapaxes(b_im, 2, 3)
    bbt_re = k_re[:, :, None, :] * bt_re - k_im[:, :, None, :] * bt_im
    bbt_im = k_re[:, :, None, :] * bt_im + k_im[:, :, None, :] * bt_re

    ar, ai = pw_re[:, :, :, :CH, None], pw_im[:, :, :, :CH, None]
    cr = jnp.swapaxes(c_re, 2, 3)[:, :, :, None, :]
    ci = jnp.swapaxes(c_im, 2, 3)[:, :, :, None, :]
    ca = jnp.concatenate([cr * ar - ci * ai, -(cr * ai + ci * ar)], axis=2)
    ca = ca.reshape(2, S5_GROUPS, 2 * S5_STATE, CH * S5_GROUP)
    bbt = jnp.concatenate([bbt_re, bbt_im], axis=-1)
    taps = jnp.einsum('dghq,dgqn->dghn', bbt, ca, precision=HI)
    skip = (d_skip.reshape(S5_GROUPS, S5_GROUP, 1) * jnp.eye(S5_GROUP, dtype=F32)[None])
    taps = taps.at[0, :, :, :S5_GROUP].add(skip)
    b_c = jnp.transpose(taps.reshape(2, NJ, GPT, S5_GROUP, CH, S5_GROUP), (1, 4, 0, 2, 3, 5))
    b_c = b_c.reshape(NJ, 2 * CH, LANES, S5_GROUP)

    def q_part(d, descending):
        pr = pk_re[d][:, :CH, None, :]
        pi = pk_im[d][:, :CH, None, :]
        if descending:
            pr, pi = jnp.flip(pr, axis=1), jnp.flip(pi, axis=1)
        br = bbt_re[d][:, None, :, :]
        bi = bbt_im[d][:, None, :, :]
        return pr * br - pi * bi, pr * bi + pi * br

    def q_rows(v):
        v = v.reshape(NJ, GPT, CH, S5_GROUP, S5_STATE)
        return jnp.transpose(v, (0, 2, 1, 3, 4)).reshape(NJ, CL, S5_STATE)
    a_q = jnp.stack([q_rows(v) for v in q_part(0, True) + q_part(1, False)], axis=0)

    def p_part(d, descending):
        pr = pw_re[d][:, :, 1:CH + 1, None]
        pi = pw_im[d][:, :, 1:CH + 1, None]
        if descending:
            pr, pi = jnp.flip(pr, axis=2), jnp.flip(pi, axis=2)
        return (ct_re[d] * pr - ct_im[d] * pi, -(ct_re[d] * pi + ct_im[d] * pr))
    ct_re = jnp.swapaxes(c_re, 2, 3)[:, :, :, None, :]
    ct_im = jnp.swapaxes(c_im, 2, 3)[:, :, :, None, :]
    a_p = jnp.stack([v.reshape(NJ, GPT * S5_STATE, CH * S5_GROUP)
                     for v in p_part(0, False) + p_part(1, True)], axis=0)

    def lanes(v):
        return jnp.transpose(v.reshape(2, NJ, GPT * S5_STATE), (1, 0, 2))
    c_r, c_i = apow(jnp.full((1,), float(CH), F32))
    s_r, s_i = apow(jnp.full((1,), float(CH * SEG), F32))
    cr, ci, sr, si = (lanes(v[..., 0]) for v in (c_r, c_i, s_r, s_i))
    trans = jnp.stack([cr[:, 0], ci[:, 0], cr[:, 1], ci[:, 1],
                       sr[:, 0], si[:, 0], sr[:, 1], si[:, 1]], axis=1)

    cidx = jnp.arange(N_CHUNK_CTX, dtype=F32)
    wf_r, wf_i = apow(CH * (N_CHUNK_CTX - 1 - cidx))
    wb_r, wb_i = apow(CH * cidx)

    def ctx_lanes(v, d):
        return jnp.transpose(v[d].reshape(NJ, GPT * S5_STATE, N_CHUNK_CTX), (0, 2, 1))
    ctx_w = jnp.stack([ctx_lanes(wf_r, 0), ctx_lanes(wf_i, 0),
                       ctx_lanes(wb_r, 1), ctx_lanes(wb_i, 1)], axis=1)
    return b_c, a_q, a_p, trans, ctx_w


def _build_operators(bc_ref, aq_ref, ap_ref, c16_ref, c64_ref, cm_ref, wm_ref, wq_ref, wp_ref):
    def expand(a, c, row_shift, col_shift):
        w = jnp.dot(a, c, preferred_element_type=F32)
        rg = (lax.broadcasted_iota(jnp.int32, (w.shape[0], 1), 0) >> row_shift) & (GPT - 1)
        cg = (lax.broadcasted_iota(jnp.int32, (1, w.shape[1]), 1) >> col_shift) & (GPT - 1)
        return jnp.where(rg == cg, w, 0.0)

    blk = [expand(bc_ref[0, kd], c16_ref[...], 4, 4) for kd in range(2 * CH)]
    for t in range(CH):
        for u in range(CH):
            b = blk[2 * (u - t)] if u > t else blk[2 * (t - u) + 1] if u < t else blk[0] + blk[1]
            wm_ref[0, t * LANES:(t + 1) * LANES, u * LANES:(u + 1) * LANES] = b.astype(BF16)
    half = GPT * S5_STATE
    for s in range(4):
        wq_ref[0, :, s * half:(s + 1) * half] = expand(aq_ref[s, 0], c64_ref[...], 4, 6).astype(BF16)
        wp_ref[0, s * half:(s + 1) * half, :] = expand(ap_ref[s, 0], cm_ref[...], 6, 4).astype(BF16)


def _s5_kernel(p_ref, pc_ref, bc_ref, aq_ref, ap_ref, c16_ref, c64_ref, cm_ref, tr_ref, cw_ref,
               y_ref, rows_ref, v_ref, wm_ref, wq_ref, wp_ref, zero_ref, zsem):
    nq = NJ
    half = GPT * S5_STATE

    zero_ref[...] = jnp.zeros_like(zero_ref)

    def clears(fn):
        def unit(b, c):
            start = pl.multiple_of((pl.program_id(0) * ZERO_PARTS + b) * ZERO_UNIT, ZERO_UNIT)
            fn(pltpu.make_async_copy(zero_ref, rows_ref.at[pl.ds(start, ZERO_UNIT)], zsem))
            return c
        lax.fori_loop(0, ZERO_PARTS, unit, 0)
    clears(lambda cp: cp.start())
    _build_operators(bc_ref, aq_ref, ap_ref, c16_ref, c64_ref, cm_ref, wm_ref, wq_ref, wp_ref)

    def chunk_rows(ref, r0, nrows):
        return jnp.concatenate([ref[t, pl.ds(r0, nrows), :] for t in range(CH)], axis=-1)

    def fill(k, c):
        r0 = pl.multiple_of(k * SEG, SEG)
        v = jnp.dot(chunk_rows(p_ref, r0, SEG), wq_ref[0], preferred_element_type=F32)
        for s in range(4 * nq):
            v_ref[s, pl.ds(k, SEG, stride=NSEG), :] = v[:, s * LANES:(s + 1) * LANES]
        return c
    lax.fori_loop(0, NSEG, fill, 0)

    vc = jnp.dot(chunk_rows(pc_ref, 0, N_CHUNK_CTX), wq_ref[0], preferred_element_type=F32)
    vfr, vfi, vbr, vbi = (vc[:, i * half:(i + 1) * half] for i in range(4))
    wfr, wfi, wbr, wbi = (cw_ref[0, i] for i in range(4))
    s0_fr = jnp.sum(wfr * vfr - wfi * vfi, axis=0, keepdims=True)
    s0_fi = jnp.sum(wfr * vfi + wfi * vfr, axis=0, keepdims=True)
    s0_br = jnp.sum(wbr * vbr - wbi * vbi, axis=0, keepdims=True)
    s0_bi = jnp.sum(wbr * vbi + wbi * vbr, axis=0, keepdims=True)

    tr = tr_ref[0]
    afr, afi, abr, abi = (jnp.broadcast_to(tr[i:i + 1], (NSEG, half)) for i in range(4))
    gfr, gfi, gbr, gbi = (tr[i:i + 1] for i in range(4, 8))

    def load_part(part, i):
        return jnp.concatenate(
            [v_ref[part * nq + q, pl.ds(pl.multiple_of(i * NSEG, NSEG), NSEG), :] for q in range(nq)],
            axis=-1)

    def store_part(part, i, val):
        for q in range(nq):
            v_ref[part * nq + q, pl.ds(pl.multiple_of(i * NSEG, NSEG), NSEG), :] = (
                val[:, q * LANES:(q + 1) * LANES])

    def step(i, carry, write):
        fr, fi, br, bi = carry
        ib = SEG - 1 - i
        ufr, ufi = load_part(0, i), load_part(1, i)
        ubr, ubi = load_part(2, ib), load_part(3, ib)
        if write:
            store_part(0, i, fr)
            store_part(1, i, fi)
            store_part(2, ib, br)
            store_part(3, ib, bi)
        return (afr * fr - afi * fi + ufr, afr * fi + afi * fr + ufi,
                abr * br - abi * bi + ubr, abr * bi + abi * br + ubi)

    zero = jnp.zeros((NSEG, half), F32)
    ffr, ffi, fbr, fbi = lax.fori_loop(0, SEG, functools.partial(step, write=False),
                                       (zero, zero, zero, zero), unroll=SCAN_UNROLL)

    rows_fr, rows_fi = [s0_fr], [s0_fi]
    for k in range(1, NSEG):
        pr, pi = rows_fr[-1], rows_fi[-1]
        rows_fr.append(gfr * pr - gfi * pi + ffr[k - 1:k])
        rows_fi.append(gfr * pi + gfi * pr + ffi[k - 1:k])
    rows_br, rows_bi = [s0_br], [s0_bi]
    for k in range(NSEG - 2, -1, -1):
        pr, pi = rows_br[0], rows_bi[0]
        rows_br.insert(0, gbr * pr - gbi * pi + fbr[k + 1:k + 2])
        rows_bi.insert(0, gbr * pi + gbi * pr + fbi[k + 1:k + 2])
    init = tuple(jnp.concatenate(r, axis=0) for r in (rows_fr, rows_fi, rows_br, rows_bi))

    lax.fori_loop(0, SEG, functools.partial(step, write=True), init, unroll=SCAN_UNROLL)

    def emit(k, c):
        r0 = pl.multiple_of(k * SEG, SEG)
        b = chunk_rows(p_ref, r0, SEG)
        sin = jnp.concatenate([v_ref[s, pl.ds(k, SEG, stride=NSEG), :] for s in range(4 * nq)], axis=-1)
        y = (jnp.dot(b, wm_ref[0], preferred_element_type=F32)
             + jnp.dot(sin.astype(BF16), wp_ref[0], preferred_element_type=F32))
        for t in range(CH):
            y_ref[t, pl.ds(r0, SEG), :] = y[:, t * LANES:(t + 1) * LANES].astype(y_ref.dtype)
        return c
    lax.fori_loop(0, NSEG, emit, 0)
    clears(lambda cp: cp.wait())


def _s5(p_t, pc_t, b_c, a_q, a_p, trans, ctx_w):
    rep = np.ones((1, GPT))
    c16 = jnp.asarray(np.kron(rep, np.eye(S5_GROUP)), F32).astype(BF16)
    c64 = jnp.asarray(np.kron(rep, np.eye(S5_STATE)), F32).astype(BF16)
    c_m = jnp.asarray(np.kron(np.eye(CH), np.kron(rep, np.eye(S5_GROUP))), F32).astype(BF16)
    b_c, a_q, a_p = b_c.astype(BF16), a_q.astype(BF16), a_p.astype(BF16)
    half = GPT * S5_STATE
    return pl.pallas_call(
        _s5_kernel,
        grid=(NJ,),
        in_specs=[pl.BlockSpec((CH, N_CHUNK, LANES), lambda j: (0, 0, j)),
                  pl.BlockSpec((CH, N_CHUNK_CTX, LANES), lambda j: (0, 0, j)),
                  pl.BlockSpec((1, 2 * CH, LANES, S5_GROUP), lambda j: (j, 0, 0, 0)),
                  pl.BlockSpec((4, 1, CL, S5_STATE), lambda j: (0, j, 0, 0)),
                  pl.BlockSpec((4, 1, half, CH * S5_GROUP), lambda j: (0, j, 0, 0)),
                  pl.BlockSpec(c16.shape, lambda j: (0, 0)),
                  pl.BlockSpec(c64.shape, lambda j: (0, 0)),
                  pl.BlockSpec(c_m.shape, lambda j: (0, 0)),
                  pl.BlockSpec((1, SUBLANES, half), lambda j: (j, 0, 0)),
                  pl.BlockSpec((1, 4, N_CHUNK_CTX, half), lambda j: (j, 0, 0, 0))],
        out_specs=(pl.BlockSpec((CH, N_CHUNK, LANES), lambda j: (0, 0, j)),
                   pl.BlockSpec(memory_space=pl.ANY)),
        out_shape=(jax.ShapeDtypeStruct((CH, N_CHUNK, S5_WIDTH), BF16),
                   jax.ShapeDtypeStruct((ROWS_ALL, D), BF16)),
        scratch_shapes=[pltpu.VMEM((4 * NJ, N_CHUNK, LANES), F32),
                        pltpu.VMEM((1, CL, CL), BF16),
                        pltpu.VMEM((1, CL, SW), BF16),
                        pltpu.VMEM((1, SW, CL), BF16),
                        pltpu.VMEM((ZERO_UNIT, D), BF16),
                        pltpu.SemaphoreType.DMA(())],
        compiler_params=_cparams(("parallel",)),
        name="s5",
    )(p_t, pc_t, b_c, a_q, a_p, c16, c64, c_m, trans, ctx_w)


def _dft_tables():
    n = np.arange(FN)
    ang = 2.0 * np.pi * np.outer(n, n) / FN
    c, s = np.cos(ang), np.sin(ang)
    st1 = np.block([[c, s], [-s, c]])
    tw = 2.0 * np.pi * np.outer(n, n) / (FN * FN)
    wr, wi = np.cos(tw), -np.sin(tw)
    fr = c[None] * wr[:, None, :] + s[None] * wi[:, None, :]
    fi = c[None] * wi[:, None, :] - s[None] * wr[:, None, :]
    st2 = np.concatenate([fr, -fi], axis=-1)
    scale = 1.0 / math.sqrt(N_TOK * FFT_DIM)
    blk_c = np.kron(np.eye(FFT_GROUPS), c) * scale
    blk_s = np.kron(np.eye(FFT_GROUPS), s) * scale
    fc = np.concatenate([blk_c, -blk_s], axis=1)
    return (jnp.asarray(st1, F32).astype(BF16), jnp.asarray(st2, F32).astype(BF16), jnp.asarray(fc, F32))


FSL = FFT_WIDTH // LANES


FBH = FB // SUBLANES


def _block_to_slabs(blk, slab_ref, first, per_half):
    for bh in range(FBH):
        val = blk[:, bh * SUBLANES:(bh + 1) * SUBLANES, :].reshape(FN * SUBLANES, FFT_WIDTH)
        for s in range(FSL):
            slab_ref[bh * per_half + first + s] = val[:, s * LANES:(s + 1) * LANES]


def _slab_rows(b, first, per_half):
    return (b // SUBLANES) * per_half + first, pl.ds(b % SUBLANES, FN, stride=SUBLANES)


def _slabs_to_block(slab_ref, first, per_half):
    halves = []
    for bh in range(FBH):
        val = jnp.concatenate([slab_ref[bh * per_half + first + s] for s in range(FSL)], axis=-1)
        halves.append(val.reshape(FN, SUBLANES, FFT_WIDTH))
    return jnp.concatenate(halves, axis=1)


def _fft1_kernel(xr_ref, xi_ref, f_ref, yr_ref, yi_ref, in_ref, out_ref):
    _block_to_slabs(xr_ref[...].astype(F32), in_ref, 0, 2 * FSL)
    _block_to_slabs(xi_ref[...].astype(F32), in_ref, FSL, 2 * FSL)
    for b in range(FB):
        def part(first):
            base, rows = _slab_rows(b, first, 2 * FSL)
            return jnp.concatenate([in_ref[base + s, rows, :] for s in range(FSL)], axis=-1)
        xs = jnp.concatenate([part(0), part(FSL)], axis=0).astype(BF16)
        y = jnp.dot(f_ref[...], xs, preferred_element_type=F32)
        base, rows = _slab_rows(b, 0, 2 * FSL)
        for s in range(FSL):
            out_ref[base + s, rows, :] = y[:FN, s * LANES:(s + 1) * LANES]
            out_ref[base + FSL + s, rows, :] = y[FN:, s * LANES:(s + 1) * LANES]
    yr_ref[...] = _slabs_to_block(out_ref, 0, 2 * FSL).astype(BF16)
    yi_ref[...] = _slabs_to_block(out_ref, FSL, 2 * FSL).astype(BF16)


def _fft1(xr, xi, st1):
    spec = pl.BlockSpec((FN, FB, FFT_WIDTH), lambda i: (0, i, 0))
    slabs = pltpu.VMEM((FBH * 2 * FSL, FN * SUBLANES, LANES), F32)
    return pl.pallas_call(
        _fft1_kernel,
        grid=(FN // FB,),
        in_specs=[spec, spec, pl.BlockSpec((2 * FN, 2 * FN), lambda i: (0, 0))],
        out_specs=(spec, spec),
        out_shape=(jax.ShapeDtypeStruct((FN, FN, FFT_WIDTH), BF16),) * 2,
        scratch_shapes=[slabs, slabs],
        compiler_params=_cparams(("parallel",)),
        name="fft1",
    )(xr.reshape(FN, FN, FFT_WIDTH), xi.reshape(FN, FN, FFT_WIDTH), st1)


def _fft2_kernel(yr_ref, yi_ref, f_ref, z_ref, out_ref):
    for b in range(FB):
        ys = jnp.concatenate([yr_ref[b * FN:(b + 1) * FN, :], yi_ref[b * FN:(b + 1) * FN, :]], axis=0)
        z = jnp.dot(f_ref[b], ys, preferred_element_type=F32)
        base, rows = _slab_rows(b, 0, FSL)
        for s in range(FSL):
            out_ref[base + s, rows, :] = z[:, s * LANES:(s + 1) * LANES]
    z_ref[...] = _slabs_to_block(out_ref, 0, FSL).astype(BF16)


def _fft2(yr, yi, st2):
    rows = pl.BlockSpec((FB * FN, FFT_WIDTH), lambda i: (i, 0))
    z = pl.pallas_call(
        _fft2_kernel,
        grid=(FN // FB,),
        in_specs=[rows, rows, pl.BlockSpec((FB, FN, 2 * FN), lambda i: (i, 0, 0))],
        out_specs=pl.BlockSpec((FN, FB, FFT_WIDTH), lambda i: (0, i, 0)),
        out_shape=jax.ShapeDtypeStruct((FN, FN, FFT_WIDTH), BF16),
        scratch_shapes=[pltpu.VMEM((FBH * FSL, FN * SUBLANES, LANES), F32)],
        compiler_params=_cparams(("parallel",)),
        name="fft2",
    )(yr.reshape(N_TOK, FFT_WIDTH), yi.reshape(N_TOK, FFT_WIDTH), st2)
    return z.reshape(N_TOK, FFT_WIDTH)


def _gelu_tanh(x):
    return 0.5 * x * (1.0 + jnp.tanh(math.sqrt(2.0 / math.pi) * (x + 0.044715 * (x * x * x))))


def _mix_kernel(x_ref, er_ref, ec_ref, lg_ref, lb_ref, m1_ref, s1_ref, wgs_ref, wgf_ref, bg_ref,
                yt_ref, zr_ref, wglu_ref, bglu_ref, wbs_ref, wbf_ref, bbf_ref, wo_ref, bo_ref,
                g1_ref, l1g_ref, l1b_ref, m2_ref, s2_ref, wr_ref, br_ref, tri_ref, etri_ref,
                h1_ref, u2_ref, pos_ref, gate_ref, cnt_ref, scr_ref):
    pos = _pos_code(er_ref, ec_ref, TM)

    def front(r0, nr):
        rows = slice(r0, r0 + nr)
        h = _layer_norm(x_ref[rows, :] + pos[rows, :], lg_ref[...], lb_ref[...])
        u_half = (h * (0.5 * m1_ref[...]) + 0.5 * s1_ref[...]).astype(BF16)

        c0, nc = r0 // CH, nr // CH
        for t in range(CH):
            for j in range(NJ):
                scr_ref[j, pl.ds(r0 + t, nc, stride=CH), :] = (
                    yt_ref[t, c0:c0 + nc, j * LANES:(j + 1) * LANES].astype(F32))
        ys = jnp.concatenate([scr_ref[j, rows, :] for j in range(NJ)], axis=-1)
        z = jnp.dot(_gelu_tanh(ys).astype(BF16), wglu_ref[...], preferred_element_type=F32) + bglu_ref[...]
        glu = (z[:, :S5_WIDTH] * _sigmoid(z[:, S5_WIDTH:])).astype(BF16)
        t_s5 = jnp.tanh(jnp.dot(u_half, wgs_ref[...], preferred_element_type=F32) + 0.5 * bg_ref[:, :D])
        t_fft = jnp.tanh(jnp.dot(u_half, wgf_ref[...], preferred_element_type=F32) + 0.5 * bg_ref[:, D:])
        y_s5 = jnp.dot(glu, wbs_ref[...], preferred_element_type=F32)
        y_fft = jnp.dot(zr_ref[rows, :], wbf_ref[...], preferred_element_type=F32) + bbf_ref[...]
        mixed2 = ((t_s5 + 1.0) * y_s5 + (t_fft + 1.0) * y_fft).astype(BF16)
        y = 0.5 * jnp.dot(mixed2, wo_ref[...], preferred_element_type=F32) + bo_ref[...]
        h1 = _layer_norm(ALPHA * h + g1_ref[...] * y, l1g_ref[...], l1b_ref[...])
        h1_ref[rows, :] = h1
        u2 = h1 * m2_ref[...] + s2_ref[...]
        u2_ref[rows, :] = u2.astype(BF16)
        u_hi = u2.astype(BF16)
        u_lo = (u2 - u_hi.astype(F32)).astype(BF16)

        def nt(a, b):
            return lax.dot_general(a, b, (((1,), (1,)), ((), ())), preferred_element_type=F32)
        return nt(wr_ref[0], u_hi) + nt(wr_ref[0], u_lo) + nt(wr_ref[1], u_hi)

    logits = front(0, TM) + br_ref[:, 0:1]
    eidx = lax.broadcasted_iota(jnp.int32, (N_EXPERTS, TM), 0)
    vals, hots = [], []
    cur = logits
    for _k in range(TOP_K):
        m = jnp.max(cur, axis=0, keepdims=True)
        sel = jnp.min(jnp.where(cur == m, eidx, N_EXPERTS), axis=0, keepdims=True)
        hot = eidx == sel
        cur = jnp.where(hot, -jnp.inf, cur)
        vals.append(m)
        hots.append(hot)
    exps = [jnp.exp(v - vals[0]) for v in vals]
    den = exps[0] + exps[1] + exps[2] + exps[3]
    gate4 = jnp.concatenate([e / den for e in exps], axis=0)

    hot_sum = (hots[0] | hots[1] | hots[2] | hots[3]).astype(F32)
    before = jnp.dot(hot_sum.astype(BF16), tri_ref[...], preferred_element_type=F32)
    cnt = jnp.broadcast_to(jnp.sum(hot_sum, axis=1, keepdims=True), (N_EXPERTS, LANES))
    cnt8 = jnp.floor((cnt + (SEG_ALIGN - 1)) * (1.0 / SEG_ALIGN)) * SEG_ALIGN
    seg0 = jnp.dot(etri_ref[...], cnt8.astype(BF16), preferred_element_type=F32)
    tot = seg0[:, 0:1] + before
    pos4 = jnp.concatenate(
        [jnp.sum(jnp.where(hk, tot, 0.0), axis=0, keepdims=True) for hk in hots], axis=0)
    pos_ref[...] = pos4.astype(jnp.int32)
    cnt_ref[0] = cnt

    gate_ref[...] = gate4


def _mix(x, emb_r, emb_c, lg, lb, m1, s1, w_in, bg, y_t, zr, wglu, bglu, wbs, wbf, bbf, wo, bo,
         g1, l1g, l1b, m2, s2, wr_t, br, tri, etri):
    gate_cols = (S5_WIDTH + FFT_WIDTH) // D
    vec = pl.BlockSpec((1, D), lambda i: (0, 0))

    def full(a):
        return pl.BlockSpec(a.shape, lambda i: (0,) * a.ndim)
    return pl.pallas_call(
        _mix_kernel,
        grid=(N_TOK // TM,),
        in_specs=[pl.BlockSpec((TM, D), lambda i: (i, 0)),
                  pl.BlockSpec((TM // GRID_W, D // 2), lambda i: (i, 0)),
                  pl.BlockSpec((GRID_W, D // 2), lambda i: (0, 0)),
                  vec, vec, vec, vec,
                  pl.BlockSpec((D, D), lambda i: (0, gate_cols)),
                  pl.BlockSpec((D, D), lambda i: (0, gate_cols + 1)), full(bg),
                  pl.BlockSpec((CH, TM // CH, S5_WIDTH), lambda i: (0, i, 0)),
                  pl.BlockSpec((TM, FFT_WIDTH), lambda i: (i, 0)),
                  full(wglu), full(bglu), full(wbs), full(wbf), full(bbf), full(wo), full(bo),
                  vec, vec, vec, vec, vec, full(wr_t), full(br), full(tri), full(etri)],
        out_specs=(pl.BlockSpec((TM, D), lambda i: (i, 0)),
                   pl.BlockSpec((TM, D), lambda i: (i, 0)),
                   pl.BlockSpec((TOP_K, TM), lambda i: (0, i)),
                   pl.BlockSpec((TOP_K, TM), lambda i: (0, i)),
                   pl.BlockSpec((1, N_EXPERTS, LANES), lambda i: (i, 0, 0))),
        out_shape=(jax.ShapeDtypeStruct((N_TOK, D), F32),
                   jax.ShapeDtypeStruct((N_TOK, D), BF16),
                   jax.ShapeDtypeStruct((TOP_K, N_TOK), jnp.int32),
                   jax.ShapeDtypeStruct((TOP_K, N_TOK), F32),
                   jax.ShapeDtypeStruct((N_TILES, N_EXPERTS, LANES), F32)),
        scratch_shapes=[pltpu.VMEM((NJ, TM, LANES), F32)],
        compiler_params=_cparams(("parallel",)),
        name="mix",
    )(x, emb_r, emb_c, lg, lb, m1, s1, w_in, w_in, bg, y_t, zr, wglu, bglu, wbs, wbf, bbf, wo, bo,
      g1, l1g, l1b, m2, s2, wr_t, br, tri, etri)


def _on_parity(i, fn):
    @pl.when(i % 2 == 0)
    def _():
        fn(0)

    @pl.when(i % 2 == 1)
    def _():
        fn(1)


def _dispatch_kernel(dprev_ref, dest_ref, pos_ref, u_ref, zeroed_ref, buf_ref, sorted_ref, sems):
    del zeroed_ref
    i = pl.program_id(0)

    @pl.when(i == 0)
    def _():
        sorted_ref[...] = jnp.zeros_like(sorted_ref)

    def chunk_copy(slot, table_ref, j):
        dst = pl.multiple_of(table_ref[0, 0, j], SEG_ALIGN)
        return pltpu.make_async_copy(sorted_ref.at[slot, pl.ds(j * SEG_ALIGN, SEG_ALIGN)],
                                     buf_ref.at[pl.ds(dst, SEG_ALIGN)], sems.at[slot])

    def drain(slot):
        pltpu.make_async_copy(sorted_ref.at[slot], buf_ref.at[pl.ds(0, CAP)], sems.at[slot]).wait()

    def run(slot):
        pos = pos_ref[...]
        u = u_ref[...]
        n_rb = CAP // SORT_BLOCK
        per_rb = NCHK // (n_rb // 2)
        for rb in range(n_rb):
            for j in range(rb * per_rb, min((rb + 1) * per_rb, NCHK)):
                chunk_copy(1 - slot, dprev_ref, j).start()
            rows = lax.broadcasted_iota(jnp.int32, (SORT_BLOCK, TM), 0) + rb * SORT_BLOCK
            hit = rows == pos[0:1]
            for k in range(1, TOP_K):
                hit = hit | (rows == pos[k:k + 1])
            onehot = jnp.where(hit, 1.0, 0.0).astype(BF16)
            sorted_ref[slot, rb * SORT_BLOCK:(rb + 1) * SORT_BLOCK, :] = jnp.dot(
                onehot, u, preferred_element_type=F32).astype(BF16)
        drain(1 - slot)

        @pl.when(i == N_TILES - 1)
        def _():
            def issue(j, c):
                chunk_copy(slot, dest_ref, j).start()
                return c
            lax.fori_loop(0, NCHK, issue, 0)
            drain(slot)
    _on_parity(i, run)


def _dispatch(chunk_table, pos_t, u2, zeroed):
    return pl.pallas_call(
        _dispatch_kernel,
        grid=(N_TILES,),
        in_specs=[pl.BlockSpec((1, 1, NCHK), lambda i: (i, 0, 0), memory_space=pltpu.SMEM),
                  pl.BlockSpec((1, 1, NCHK), lambda i: (i + 1, 0, 0), memory_space=pltpu.SMEM),
                  pl.BlockSpec((TOP_K, TM), lambda i: (0, i)),
                  pl.BlockSpec((TM, D), lambda i: (i, 0)),
                  pl.BlockSpec(memory_space=pl.ANY)],
        out_specs=pl.BlockSpec(memory_space=pl.ANY),
        out_shape=jax.ShapeDtypeStruct((ROWS_ALL, D), BF16),
        scratch_shapes=[pltpu.VMEM((2, CAP, D), BF16),
                        pltpu.SemaphoreType.DMA((2,))],
        input_output_aliases={4: 0},
        compiler_params=_cparams(("arbitrary",)),
        name="dispatch",
    )(chunk_table, chunk_table, pos_t, u2, zeroed)


def _ffn_kernel(be_ref, nu_ref, run_ref, nxt_ref, valid_ref, x_ref, wu_hbm, bu_ref, wd_hbm, bd_ref,
                y_ref, wu_ref, wd_ref, wub_ref, wdb_ref, sems):
    i = pl.program_id(0)
    used = i < nu_ref[0]

    def weight_copies(e, slot):
        return (pltpu.make_async_copy(wu_hbm.at[e], wu_ref.at[slot], sems.at[slot]),
                pltpu.make_async_copy(wd_hbm.at[e], wd_ref.at[slot], sems.at[slot]))

    @pl.when(used)
    def _():
        run = run_ref[i]

        @pl.when(run >= 0)
        def _():
            def open_run(slot):
                @pl.when(run == 0)
                def _():
                    for cp in weight_copies(be_ref[i], slot):
                        cp.start()

                @pl.when(nxt_ref[i] >= 0)
                def _():
                    for cp in weight_copies(nxt_ref[i], 1 - slot):
                        cp.start()
                for cp in weight_copies(be_ref[i], slot):
                    cp.wait()
                wub_ref[...] = wu_ref[slot].astype(BF16)
                wdb_ref[...] = wd_ref[slot].astype(BF16)
            _on_parity(run, open_run)

        def expert_rows(r0, nr):
            rows = slice(r0, r0 + nr)
            e = be_ref[i]
            h = (jnp.dot(x_ref[rows, :], wub_ref[...], preferred_element_type=F32)
                 + bu_ref[pl.ds(e, 1), :])
            h_glu = jnp.minimum(h[:, :D], SWIGLU_LIMIT)
            h_lin = jnp.clip(h[:, D:], -SWIGLU_LIMIT, SWIGLU_LIMIT)
            act = (h_glu * _sigmoid(SWIGLU_ALPHA * h_glu) * (h_lin + 1.0)).astype(BF16)
            y_ref[rows, :] = (jnp.dot(act, wdb_ref[...], preferred_element_type=F32)
                              + bd_ref[pl.ds(e, 1), :]).astype(BF16)

        valid = valid_ref[i]

        @pl.when(valid == BM)
        def _():
            expert_rows(0, BM)

        @pl.when(valid < BM)
        def _():
            for h0 in range(0, BM, FFN_HALF):
                @pl.when(valid >= h0 + FFN_HALF)
                def _(h0=h0):
                    expert_rows(h0, FFN_HALF)

                @pl.when(valid < h0 + FFN_HALF)
                def _(h0=h0):
                    for r0 in range(h0, h0 + FFN_HALF, FFN_TAIL):
                        @pl.when(r0 < valid)
                        def _(r0=r0):
                            expert_rows(r0, FFN_TAIL)

                        @pl.when(r0 >= valid)
                        def _(r0=r0):
                            y_ref[r0:r0 + FFN_TAIL, :] = jnp.zeros((FFN_TAIL, D), BF16)


def _ffn(block_expert, n_used, run_id, next_expert, valid, buf, w_up, b_up, w_down, b_down):
    def blk(i, be, nu, *_):
        return jnp.minimum(i, nu[0] - 1)
    return pl.pallas_call(
        _ffn_kernel,
        grid_spec=pltpu.PrefetchScalarGridSpec(
            num_scalar_prefetch=5,
            grid=(N_BLOCKS_ALL,),
            in_specs=[pl.BlockSpec((BM, D), lambda i, *s: (blk(i, *s), 0)),
                      pl.BlockSpec(memory_space=pl.ANY),
                      pl.BlockSpec((N_EXPERTS, 2 * D), lambda i, *s: (0, 0)),
                      pl.BlockSpec(memory_space=pl.ANY),
                      pl.BlockSpec((N_EXPERTS, D), lambda i, *s: (0, 0))],
            out_specs=pl.BlockSpec((BM, D), lambda i, *s: (blk(i, *s), 0)),
            scratch_shapes=[pltpu.VMEM((2, D, 2 * D), F32),
                            pltpu.VMEM((2, D, D), F32),
                            pltpu.VMEM((D, 2 * D), BF16),
                            pltpu.VMEM((D, D), BF16),
                            pltpu.SemaphoreType.DMA((2,))]),
        out_shape=jax.ShapeDtypeStruct((ROWS_ALL, D), BF16),
        input_output_aliases={5: 0},
        compiler_params=_cparams(("arbitrary",)),
        name="ffn",
    )(block_expert, n_used, run_id, next_expert, valid, buf, w_up, b_up, w_down, b_down)


def _combine_kernel(dest_ref, dnext_ref, y_ref, h1_ref, pos_ref, gate_ref, g2_ref, lg_ref, lb_ref,
                    o_ref, sorted_ref, sems):
    i = pl.program_id(0)

    def chunk_copy(slot, table_ref, j):
        src = pl.multiple_of(table_ref[0, 0, j], SEG_ALIGN)
        return pltpu.make_async_copy(y_ref.at[pl.ds(src, SEG_ALIGN)],
                                     sorted_ref.at[slot, pl.ds(j * SEG_ALIGN, SEG_ALIGN)],
                                     sems.at[slot])

    def drain(slot):
        pltpu.make_async_copy(y_ref.at[pl.ds(0, CAP)], sorted_ref.at[slot], sems.at[slot]).wait()

    @pl.when(i == 0)
    def _():
        def issue(j, c):
            chunk_copy(0, dest_ref, j).start()
            return c
        lax.fori_loop(0, NCHK, issue, 0)

    def run(slot):
        drain(slot)

        pos = pos_ref[...]
        gate = gate_ref[...]
        m = jnp.zeros((TM, D), F32)
        n_cb = CAP // CAP_BLOCK
        per_cb = NCHK // (n_cb // 2)
        for cb in range(n_cb):
            for j in range(cb * per_cb, min((cb + 1) * per_cb, NCHK)):
                chunk_copy(1 - slot, dnext_ref, j).start()
            rws = lax.broadcasted_iota(jnp.int32, (CAP_BLOCK, TM), 0) + cb * CAP_BLOCK
            g = jnp.where(rws == pos[0:1], gate[0:1], 0.0)
            for k in range(1, TOP_K):
                g = g + jnp.where(rws == pos[k:k + 1], gate[k:k + 1], 0.0)
            rows = sorted_ref[slot, cb * CAP_BLOCK:(cb + 1) * CAP_BLOCK, :]
            m = m + lax.dot_general(g.astype(BF16), rows, (((0,), (0,)), ((), ())),
                                    preferred_element_type=F32)
        o_ref[...] = _layer_norm(ALPHA * h1_ref[...] + g2_ref[...] * m, lg_ref[...], lb_ref[...])

        @pl.when(i == N_TILES - 1)
        def _():
            drain(1 - slot)
    _on_parity(i, run)


def _combine(chunk_table, y_buf, h1, pos_t, gate_t, g2, lg, lb):
    vec = pl.BlockSpec((1, D), lambda i: (0, 0))
    return pl.pallas_call(
        _combine_kernel,
        grid_spec=pltpu.PrefetchScalarGridSpec(
            num_scalar_prefetch=0,
            grid=(N_TILES,),
            in_specs=[pl.BlockSpec((1, 1, NCHK), lambda i: (i + 1, 0, 0), memory_space=pltpu.SMEM),
                      pl.BlockSpec((1, 1, NCHK), lambda i: (i + 2, 0, 0), memory_space=pltpu.SMEM),
                      pl.BlockSpec(memory_space=pl.ANY),
                      pl.BlockSpec((TM, D), lambda i: (i, 0)),
                      pl.BlockSpec((TOP_K, TM), lambda i: (0, i)),
                      pl.BlockSpec((TOP_K, TM), lambda i: (0, i)),
                      vec, vec, vec],
            out_specs=pl.BlockSpec((TM, D), lambda i: (i, 0)),
            scratch_shapes=[pltpu.VMEM((2, CAP, D), BF16),
                            pltpu.SemaphoreType.DMA((2,))]),
        out_shape=jax.ShapeDtypeStruct((N_TOK, D), F32),
        compiler_params=_cparams(("arbitrary",)),
        name="combine",
    )(chunk_table, chunk_table, y_buf, h1, pos_t, gate_t, g2, lg, lb)


def _sincos_tables():
    q = D // 4
    omega = 1.0 / (10000.0 ** (np.arange(q) / q))

    def emb(n):
        ang = np.arange(n)[:, None] * omega[None, :]
        return jnp.asarray(np.concatenate([np.sin(ang), np.cos(ang)], axis=-1), F32)
    return emb(N_TOK // GRID_W), emb(GRID_W)


def kernel(x, c, ctx, c_ctx, ln_in_g, ln_in_b, w_ada, b_ada, w_in, b_in, s5_lambda_re, s5_lambda_im, s5_log_dt, s5_b_re, s5_b_im, s5_c_re, s5_c_im, s5_d, w_glu, b_glu, w_br_s5, w_br_fft, b_br_fft, w_out, b_out, ln1_g, ln1_b, w_router, b_router, w_up, b_up, w_down, b_down, ln2_g, ln2_b):
    assert x.shape == (1, N_TOK, D) and ctx.shape == (1, N_CTX, D) and w_ada.shape[0] == 1
    row = lambda v: v.reshape(1, -1).astype(F32)

    cc = jnp.concatenate([c.reshape(1, D), c_ctx.reshape(1, D), jnp.zeros((SUBLANES - 2, D), F32)], axis=0)
    ada = _ada(cc, w_ada[0], row(b_ada[0]))
    sh1, sc1, g1, sh2, sc2, g2 = (ada[0:1, k * D:(k + 1) * D] for k in range(6))
    sh1c, sc1c = ada[1:2, 0:D], ada[1:2, D:2 * D]

    emb_r, emb_c = _sincos_tables()
    st1, st2, fc = _dft_tables()
    lg, lb = row(ln_in_g), row(ln_in_b)

    w_in_bf = w_in[0].astype(BF16)
    b_s5 = row(b_in[0][:S5_WIDTH])
    b_fft8 = jnp.concatenate([row(b_in[0][S5_WIDTH:S5_WIDTH + FFT_WIDTH]),
                              jnp.zeros((SUBLANES - 1, FFT_WIDTH), F32)], axis=0)
    b_g = row(b_in[0][S5_WIDTH + FFT_WIDTH:])
    w_fc, b_fc = _fft_weights(w_in[0], b_fft8, fc)
    bcat = jnp.concatenate([b_s5, b_fc[0:1]], axis=1)

    x2 = x[0]
    p_t, xr, xi = _proj(x2, emb_r, emb_c, lg, lb, 1.0 + sc1, sh1, w_in_bf, w_fc, bcat)
    pc_t = _ctx_proj(ctx[0], lg, lb, 1.0 + sc1c, sh1c, w_in_bf[:, :S5_WIDTH], b_s5)

    b_c, a_q, a_p, trans, ctx_w = _s5_tables(
        s5_lambda_re[0], s5_lambda_im[0], s5_log_dt[0], s5_b_re[0], s5_b_im[0],
        s5_c_re[0], s5_c_im[0], s5_d[0])
    y_t, zeroed = _s5(p_t, pc_t, b_c, a_q, a_p, trans, ctx_w)

    yr, yi = _fft1(xr, xi, st1)
    zr = _fft2(yr, yi, st2)

    tri = jnp.asarray(np.arange(TM)[:, None] < np.arange(TM)[None, :], BF16)
    br = jnp.broadcast_to(b_router[0].reshape(N_EXPERTS, 1), (N_EXPERTS, LANES))
    etri = jnp.asarray(np.arange(N_EXPERTS)[:, None] > np.arange(N_EXPERTS)[None, :], BF16)
    wr_t = jnp.transpose(w_router[0])
    wr_hi = wr_t.astype(BF16)
    wr_split = jnp.stack([wr_hi, (wr_t - wr_hi.astype(F32)).astype(BF16)], axis=0)
    h1, u2, pos_t, gate_t, counts = _mix(
        x2, emb_r, emb_c, lg, lb, 1.0 + sc1, sh1, w_in_bf, b_g, y_t, zr,
        w_glu[0].astype(BF16), row(b_glu[0]), w_br_s5[0].astype(BF16), w_br_fft[0].astype(BF16),
        row(b_br_fft[0]), w_out[0].astype(BF16), row(b_out[0]), g1, row(ln1_g[0]), row(ln1_b[0]),
        1.0 + sc2, sh2, wr_split, br, tri, etri)

    cnt = counts[:, :, 0].astype(jnp.int32)
    seg = (cnt + SEG_ALIGN - 1) // SEG_ALIGN * SEG_ALIGN
    seg_end = jnp.cumsum(seg, axis=1)
    seg_start = seg_end - seg
    padded = (jnp.sum(seg, axis=0) + BM - 1) // BM * BM
    pad_ends = jnp.cumsum(padded)
    seg_dest = (pad_ends - padded)[None, :] + jnp.cumsum(seg, axis=0) - seg
    chunk_row = jnp.arange(NCHK, dtype=jnp.int32) * SEG_ALIGN
    chunk_exp = jnp.minimum(jnp.sum(chunk_row[None, :, None] >= seg_end[:, None, :], axis=-1),
                            N_EXPERTS - 1)
    own = chunk_exp[:, :, None] == jnp.arange(N_EXPERTS, dtype=jnp.int32)[None, None, :]
    chunk_dest = (jnp.sum(jnp.where(own, (seg_dest - seg_start)[:, None, :], 0), axis=-1)
                  + chunk_row[None, :]).astype(jnp.int32).reshape(N_TILES, 1, NCHK)
    nchk = (seg_end[:, -1] // SEG_ALIGN).astype(jnp.int32)
    block_start = jnp.arange(N_BLOCKS_ALL, dtype=jnp.int32) * BM
    block_expert = jnp.minimum(jnp.sum(block_start[:, None] >= pad_ends[None, :], axis=1),
                               N_EXPERTS - 1).astype(jnp.int32)
    n_used = (pad_ends[-1:] // BM).astype(jnp.int32)
    opens = (block_start < pad_ends[-1]) & (
        block_expert != jnp.concatenate([jnp.full((1,), -1, jnp.int32), block_expert[:-1]]))
    run_id = jnp.where(opens, jnp.cumsum(opens.astype(jnp.int32)) - 1, -1).astype(jnp.int32)
    experts = jnp.arange(N_EXPERTS, dtype=jnp.int32)
    later = (experts[None, :] > block_expert[:, None]) & (padded[None, :] > 0)
    next_expert = jnp.min(jnp.where(later, experts[None, :], N_EXPERTS), axis=1)
    next_expert = jnp.where(next_expert < N_EXPERTS, next_expert, -1).astype(jnp.int32)

    spare = (ROWS + chunk_row)[None, None, :]
    chunk_table = jnp.concatenate(
        [spare,
         jnp.where(chunk_row[None, None, :] < (nchk * SEG_ALIGN)[:, None, None], chunk_dest, spare),
         spare], axis=0).astype(jnp.int32)
    fill_ends = (pad_ends - padded + jnp.sum(seg, axis=0)).astype(jnp.int32)
    buf = _dispatch(chunk_table, pos_t, u2, zeroed)
    mine = block_expert[:, None] == experts[None, :]
    filled = jnp.sum(jnp.where(mine, fill_ends[None, :], 0), axis=1)
    valid = jnp.clip(filled - block_start, 0, BM).astype(jnp.int32)
    y_buf = _ffn(block_expert, n_used, run_id, next_expert, valid, buf, w_up[0],
                 b_up[0], w_down[0], b_down[0])
    out = _combine(chunk_table, y_buf, h1, pos_t, gate_t, g2, row(ln2_g[0]), row(ln2_b[0]))
    return out.reshape(1, N_TOK, D)
```

```python
import functools
import math

import jax
import jax.numpy as jnp
import numpy as np
from jax import lax
from jax.experimental import pallas as pl
from jax.experimental.pallas import tpu as pltpu

F32 = jnp.float32
BF16 = jnp.bfloat16
HI = lax.Precision.HIGHEST

D = 1024
N_TOK = 16384
N_CTX = 256
GRID_W = 64
S5_GROUP = 16
S5_GROUPS = 32
S5_STATE = 64
S5_WIDTH = 512
FFT_GROUPS = 4
FFT_DIM = 128
FFT_WIDTH = 512
N_EXPERTS = 32
TOP_K = 4
LN_EPS = 1e-5
ALPHA = 2.0 ** 0.25
SWIGLU_ALPHA = 1.702
SWIGLU_LIMIT = 7.0

LANES = 128
SUBLANES = 8
VMEM_LIMIT = 56 * 1024 * 1024

CH = 8
N_CHUNK = N_TOK // CH
N_CHUNK_CTX = N_CTX // CH
NSEG = SUBLANES
SEG = N_CHUNK // NSEG
SCAN_UNROLL = 8
GPT = LANES // S5_GROUP
NJ = S5_WIDTH // LANES
CL = CH * LANES
SW = 4 * GPT * S5_STATE

FN = 128
FB = 16

TM = 512
TM_PROJ = 1024
N_TILES = N_TOK // TM
BM = 1024
FFN_HALF = 512
FFN_TAIL = 256
N_SLOTS = N_TOK * TOP_K
SEG_ALIGN = 2 * SUBLANES
CAP_BLOCK = 256
SORT_BLOCK = 128
CAP = -(-(TOP_K * TM + N_EXPERTS * (SEG_ALIGN - 1)) // CAP_BLOCK) * CAP_BLOCK
NCHK = CAP // SEG_ALIGN
N_BLOCKS = -(-(N_SLOTS + N_TILES * N_EXPERTS * (SEG_ALIGN - 1)) // BM) + N_EXPERTS
ROWS = N_BLOCKS * BM
N_BLOCKS_ALL = N_BLOCKS + -(-CAP // BM)
ROWS_ALL = N_BLOCKS_ALL * BM
ZERO_UNIT = 512
ZERO_PARTS = ROWS_ALL // (NJ * ZERO_UNIT)
assert ZERO_UNIT * ZERO_PARTS * NJ == ROWS_ALL


def _cparams(sem):
    return pltpu.CompilerParams(dimension_semantics=sem, vmem_limit_bytes=VMEM_LIMIT)


def _layer_norm(x, g, b):
    mu = jnp.mean(x, axis=-1, keepdims=True)
    xc = x - mu
    var = jnp.mean(xc * xc, axis=-1, keepdims=True)
    return xc * lax.rsqrt(var + LN_EPS) * g + b


def _sigmoid(x):
    return 0.5 * jnp.tanh(0.5 * x) + 0.5


def _ada_kernel(c_ref, w_ref, b_ref, o_ref):
    c = c_ref[...]
    s = c * _sigmoid(c)
    w = w_ref[...]
    s_hi, w_hi = s.astype(BF16), w.astype(BF16)
    s_lo = (s - s_hi.astype(F32)).astype(BF16)
    w_lo = (w - w_hi.astype(F32)).astype(BF16)

    def mm(a, b):
        return jnp.dot(a, b, preferred_element_type=F32)
    o_ref[...] = mm(s_hi, w_hi) + mm(s_lo, w_hi) + mm(s_hi, w_lo) + b_ref[...]


def _ada(cc, w_ada, b_ada):
    nb = 4
    wb = 6 * D // nb
    return pl.pallas_call(
        _ada_kernel,
        grid=(nb,),
        in_specs=[pl.BlockSpec((SUBLANES, D), lambda i: (0, 0)),
                  pl.BlockSpec((D, wb), lambda i: (0, i)),
                  pl.BlockSpec((1, wb), lambda i: (0, i))],
        out_specs=pl.BlockSpec((SUBLANES, wb), lambda i: (0, i)),
        out_shape=jax.ShapeDtypeStruct((SUBLANES, 6 * D), F32),
        compiler_params=_cparams(("parallel",)),
        name="ada",
    )(cc, w_ada, b_ada)


def _fftw_kernel(w_ref, b_ref, f_ref, wo_ref, bo_ref):
    f = f_ref[...]
    w = w_ref[...]
    w_hi, f_hi = w.astype(BF16), f.astype(BF16)
    w_lo = (w - w_hi.astype(F32)).astype(BF16)
    f_lo = (f - f_hi.astype(F32)).astype(BF16)

    def mm(a, b):
        return jnp.dot(a, b, preferred_element_type=F32)
    wo_ref[...] = (mm(w_hi, f_hi) + mm(w_lo, f_hi) + mm(w_hi, f_lo)).astype(BF16)
    bo_ref[...] = jnp.dot(b_ref[...], f, preferred_element_type=F32, precision=HI)


def _fft_weights(w_in, b_fft8, fc):
    def full(a):
        return pl.BlockSpec(a.shape, lambda i: (0,) * a.ndim)
    outs = (jax.ShapeDtypeStruct((D, 2 * FFT_WIDTH), BF16),
            jax.ShapeDtypeStruct((SUBLANES, 2 * FFT_WIDTH), F32))
    return pl.pallas_call(
        _fftw_kernel,
        grid=(1,),
        in_specs=[pl.BlockSpec((D, FFT_WIDTH), lambda i: (0, S5_WIDTH // FFT_WIDTH)),
                  full(b_fft8), full(fc)],
        out_specs=tuple(pl.BlockSpec(o.shape, lambda i: (0, 0)) for o in outs),
        out_shape=outs,
        compiler_params=_cparams(("arbitrary",)),
        name="fftw",
    )(w_in, b_fft8, fc)


def _pos_code(er_ref, ec_ref, tm):
    nr = tm // GRID_W
    er = er_ref[...]
    row = jnp.broadcast_to(er[:, None, :], (nr, GRID_W, D // 2)).reshape(tm, D // 2)
    col = jnp.concatenate([ec_ref[...]] * nr, axis=0)
    return jnp.concatenate([row, col], axis=-1)


def _to_chunk_major(val, scr_ref, out_ref, tm):
    for j in range(NJ):
        scr_ref[j] = val[:, j * LANES:(j + 1) * LANES]
    for t in range(CH):
        for j in range(NJ):
            piece = scr_ref[j, pl.ds(t, tm // CH, stride=CH), :]
            out_ref[t, :, j * LANES:(j + 1) * LANES] = piece.astype(out_ref.dtype)


def _proj_kernel(x_ref, er_ref, ec_ref, lg_ref, lb_ref, m_ref, s_ref, ws_ref, wf_ref, b_ref,
                 p_ref, xr_ref, xi_ref, h_ref, scr_ref):
    x = x_ref[...] + _pos_code(er_ref, ec_ref, TM_PROJ)
    h = _layer_norm(x, lg_ref[...], lb_ref[...])
    h_ref[...] = h
    u = (h * m_ref[...] + s_ref[...]).astype(BF16)
    p_s5 = jnp.dot(u, ws_ref[...], preferred_element_type=F32) + b_ref[:, :S5_WIDTH]
    _to_chunk_major(p_s5, scr_ref, p_ref, TM_PROJ)
    p_f = jnp.dot(u, wf_ref[...], preferred_element_type=F32) + b_ref[:, S5_WIDTH:]
    xr_ref[...] = p_f[:, :FFT_WIDTH].astype(BF16)
    xi_ref[...] = p_f[:, FFT_WIDTH:].astype(BF16)


def _proj(x, emb_r, emb_c, lg, lb, m1, s1, w_in, w_fc, bcat):
    nw = bcat.shape[1]
    vec = pl.BlockSpec((1, D), lambda i: (0, 0))
    return pl.pallas_call(
        _proj_kernel,
        grid=(N_TOK // TM_PROJ,),
        in_specs=[pl.BlockSpec((TM_PROJ, D), lambda i: (i, 0)),
                  pl.BlockSpec((TM_PROJ // GRID_W, D // 2), lambda i: (i, 0)),
                  pl.BlockSpec((GRID_W, D // 2), lambda i: (0, 0)),
                  vec, vec, vec, vec,
                  pl.BlockSpec((D, S5_WIDTH), lambda i: (0, 0)),
                  pl.BlockSpec((D, 2 * FFT_WIDTH), lambda i: (0, 0)),
                  pl.BlockSpec((1, nw), lambda i: (0, 0))],
        out_specs=(pl.BlockSpec((CH, TM_PROJ // CH, S5_WIDTH), lambda i: (0, i, 0)),
                   pl.BlockSpec((TM_PROJ, FFT_WIDTH), lambda i: (i, 0)),
                   pl.BlockSpec((TM_PROJ, FFT_WIDTH), lambda i: (i, 0)),
                   pl.BlockSpec((TM_PROJ, D), lambda i: (i, 0))),
        out_shape=(jax.ShapeDtypeStruct((CH, N_CHUNK, S5_WIDTH), BF16),
                   jax.ShapeDtypeStruct((N_TOK, FFT_WIDTH), BF16),
                   jax.ShapeDtypeStruct((N_TOK, FFT_WIDTH), BF16),
                   jax.ShapeDtypeStruct((N_TOK, D), F32)),
        scratch_shapes=[pltpu.VMEM((NJ, TM_PROJ, LANES), F32)],
        compiler_params=_cparams(("parallel",)),
        name="proj",
    )(x, emb_r, emb_c, lg, lb, m1, s1, w_in, w_fc, bcat)


def _ctx_proj_kernel(x_ref, lg_ref, lb_ref, m_ref, s_ref, w_ref, b_ref, p_ref, scr_ref):
    h = _layer_norm(x_ref[...], lg_ref[...], lb_ref[...])
    u = (h * m_ref[...] + s_ref[...]).astype(BF16)
    p = jnp.dot(u, w_ref[...], preferred_element_type=F32) + b_ref[...]
    _to_chunk_major(p, scr_ref, p_ref, N_CTX)


def _ctx_proj(ctx, lg, lb, m1, s1, w_s5, b_s5):
    return pl.pallas_call(
        _ctx_proj_kernel,
        out_shape=jax.ShapeDtypeStruct((CH, N_CHUNK_CTX, S5_WIDTH), BF16),
        scratch_shapes=[pltpu.VMEM((NJ, N_CTX, LANES), F32)],
        compiler_params=pltpu.CompilerParams(vmem_limit_bytes=VMEM_LIMIT),
        name="ctxproj",
    )(ctx, lg, lb, m1, s1, w_s5, b_s5)


def _s5_tables(lam_re, lam_im, log_dt, b_re, b_im, c_re, c_im, d_skip):
    dt = jnp.exp(log_dt)[..., None]
    zr = lam_re * dt
    zi = lam_im * dt

    def apow(m):
        m = jnp.asarray(m, F32)
        mag = jnp.exp(zr[..., None] * m)
        return mag * jnp.cos(zi[..., None] * m), mag * jnp.sin(zi[..., None] * m)

    a_re, a_im = apow(jnp.ones((1,), F32))
    a_re, a_im = a_re[..., 0], a_im[..., 0]
    den = lam_re * lam_re + lam_im * lam_im
    num_re = a_re - 1.0
    k_re = (num_re * lam_re + a_im * lam_im) / den
    k_im = (a_im * lam_re - num_re * lam_im) / den
    bb_re = k_re[..., None] * b_re - k_im[..., None] * b_im
    bb_im = k_re[..., None] * b_im + k_im[..., None] * b_re

    ks = jnp.arange(CH + 1, dtype=F32)
    pw_re, pw_im = apow(ks)
    kmag = jnp.exp(zr[:, :, None, :] * ks[None, None, :, None])
    pk_re = kmag * jnp.cos(zi[:, :, None, :] * ks[None, None, :, None])
    pk_im = kmag * jnp.sin(zi[:, :, None, :] * ks[None, None, :, None])
    bt_re, bt_im = jnp.swapaxes(b_re, 2, 3), jnp.swapaxes(b_im, 2, 3)
    bbt_re = k_re[:, :, None, :] * bt_re - k_im[:, :, None, :] * bt_im
    bbt_im = k_re[:, :, None, :] * bt_im + k_im[:, :, None, :] * bt_re

    ar, ai = pw_re[:, :, :, :CH, None], pw_im[:, :, :, :CH, None]
    cr = jnp.swapaxes(c_re, 2, 3)[:, :, :, None, :]
    ci = jnp.swapaxes(c_im, 2, 3)[:, :, :, None, :]
    ca = jnp.concatenate([cr * ar - ci * ai, -(cr * ai + ci * ar)], axis=2)
    ca = ca.reshape(2, S5_GROUPS, 2 * S5_STATE, CH * S5_GROUP)
    bbt = jnp.concatenate([bbt_re, bbt_im], axis=-1)
    taps = jnp.einsum('dghq,dgqn->dghn', bbt, ca, precision=HI)
    skip = (d_skip.reshape(S5_GROUPS, S5_GROUP, 1) * jnp.eye(S5_GROUP, dtype=F32)[None])
    taps = taps.at[0, :, :, :S5_GROUP].add(skip)
    b_c = jnp.transpose(taps.reshape(2, NJ, GPT, S5_GROUP, CH, S5_GROUP), (1, 4, 0, 2, 3, 5))
    b_c = b_c.reshape(NJ, 2 * CH, LANES, S5_GROUP)

    def q_part(d, descending):
        pr = pk_re[d][:, :CH, None, :]
        pi = pk_im[d][:, :CH, None, :]
        if descending:
            pr, pi = jnp.flip(pr, axis=1), jnp.flip(pi, axis=1)
        br = bbt_re[d][:, None, :, :]
        bi = bbt_im[d][:, None, :, :]
        return pr * br - pi * bi, pr * bi + pi * br

    def q_rows(v):
        v = v.reshape(NJ, GPT, CH, S5_GROUP, S5_STATE)
        return jnp.transpose(v, (0, 2, 1, 3, 4)).reshape(NJ, CL, S5_STATE)
    a_q = jnp.stack([q_rows(v) for v in q_part(0, True) + q_part(1, False)], axis=0)

    def p_part(d, descending):
        pr = pw_re[d][:, :, 1:CH + 1, None]
        pi = pw_im[d][:, :, 1:CH + 1, None]
        if descending:
            pr, pi = jnp.flip(pr, axis=2), jnp.flip(pi, axis=2)
        return (ct_re[d] * pr - ct_im[d] * pi, -(ct_re[d] * pi + ct_im[d] * pr))
    ct_re = jnp.swapaxes(c_re, 2, 3)[:, :, :, None, :]
    ct_im = jnp.swapaxes(c_im, 2, 3)[:, :, :, None, :]
    a_p = jnp.stack([v.reshape(NJ, GPT * S5_STATE, CH * S5_GROUP)
                     for v in p_part(0, False) + p_part(1, True)], axis=0)

    def lanes(v):
        return jnp.transpose(v.reshape(2, NJ, GPT * S5_STATE), (1, 0, 2))
    c_r, c_i = apow(jnp.full((1,), float(CH), F32))
    s_r, s_i = apow(jnp.full((1,), float(CH * SEG), F32))
    cr, ci, sr, si = (lanes(v[..., 0]) for v in (c_r, c_i, s_r, s_i))
    trans = jnp.stack([cr[:, 0], ci[:, 0], cr[:, 1], ci[:, 1],
                       sr[:, 0], si[:, 0], sr[:, 1], si[:, 1]], axis=1)

    cidx = jnp.arange(N_CHUNK_CTX, dtype=F32)
    wf_r, wf_i = apow(CH * (N_CHUNK_CTX - 1 - cidx))
    wb_r, wb_i = apow(CH * cidx)

    def ctx_lanes(v, d):
        return jnp.transpose(v[d].reshape(NJ, GPT * S5_STATE, N_CHUNK_CTX), (0, 2, 1))
    ctx_w = jnp.stack([ctx_lanes(wf_r, 0), ctx_lanes(wf_i, 0),
                       ctx_lanes(wb_r, 1), ctx_lanes(wb_i, 1)], axis=1)
    return b_c, a_q, a_p, trans, ctx_w


def _build_operators(bc_ref, aq_ref, ap_ref, c16_ref, c64_ref, cm_ref, wm_ref, wq_ref, wp_ref):
    def expand(a, c, row_shift, col_shift):
        w = jnp.dot(a, c, preferred_element_type=F32)
        rg = (lax.broadcasted_iota(jnp.int32, (w.shape[0], 1), 0) >> row_shift) & (GPT - 1)
        cg = (lax.broadcasted_iota(jnp.int32, (1, w.shape[1]), 1) >> col_shift) & (GPT - 1)
        return jnp.where(rg == cg, w, 0.0)

    blk = [expand(bc_ref[0, kd], c16_ref[...], 4, 4) for kd in range(2 * CH)]
    for t in range(CH):
        for u in range(CH):
            b = blk[2 * (u - t)] if u > t else blk[2 * (t - u) + 1] if u < t else blk[0] + blk[1]
            wm_ref[0, t * LANES:(t + 1) * LANES, u * LANES:(u + 1) * LANES] = b.astype(BF16)
    half = GPT * S5_STATE
    for s in range(4):
        wq_ref[0, :, s * half:(s + 1) * half] = expand(aq_ref[s, 0], c64_ref[...], 4, 6).astype(BF16)
        wp_ref[0, s * half:(s + 1) * half, :] = expand(ap_ref[s, 0], cm_ref[...], 6, 4).astype(BF16)


def _s5_kernel(p_ref, pc_ref, bc_ref, aq_ref, ap_ref, c16_ref, c64_ref, cm_ref, tr_ref, cw_ref,
               y_ref, rows_ref, v_ref, wm_ref, wq_ref, wp_ref, zero_ref, zsem):
    nq = NJ
    half = GPT * S5_STATE

    zero_ref[...] = jnp.zeros_like(zero_ref)

    def clears(fn):
        def unit(b, c):
            start = pl.multiple_of((pl.program_id(0) * ZERO_PARTS + b) * ZERO_UNIT, ZERO_UNIT)
            fn(pltpu.make_async_copy(zero_ref, rows_ref.at[pl.ds(start, ZERO_UNIT)], zsem))
            return c
        lax.fori_loop(0, ZERO_PARTS, unit, 0)
    clears(lambda cp: cp.start())
    _build_operators(bc_ref, aq_ref, ap_ref, c16_ref, c64_ref, cm_ref, wm_ref, wq_ref, wp_ref)

    def chunk_rows(ref, r0, nrows):
        return jnp.concatenate([ref[t, pl.ds(r0, nrows), :] for t in range(CH)], axis=-1)

    def fill(k, c):
        r0 = pl.multiple_of(k * SEG, SEG)
        v = jnp.dot(chunk_rows(p_ref, r0, SEG), wq_ref[0], preferred_element_type=F32)
        for s in range(4 * nq):
            v_ref[s, pl.ds(k, SEG, stride=NSEG), :] = v[:, s * LANES:(s + 1) * LANES]
        return c
    lax.fori_loop(0, NSEG, fill, 0)

    vc = jnp.dot(chunk_rows(pc_ref, 0, N_CHUNK_CTX), wq_ref[0], preferred_element_type=F32)
    vfr, vfi, vbr, vbi = (vc[:, i * half:(i + 1) * half] for i in range(4))
    wfr, wfi, wbr, wbi = (cw_ref[0, i] for i in range(4))
    s0_fr = jnp.sum(wfr * vfr - wfi * vfi, axis=0, keepdims=True)
    s0_fi = jnp.sum(wfr * vfi + wfi * vfr, axis=0, keepdims=True)
    s0_br = jnp.sum(wbr * vbr - wbi * vbi, axis=0, keepdims=True)
    s0_bi = jnp.sum(wbr * vbi + wbi * vbr, axis=0, keepdims=True)

    tr = tr_ref[0]
    afr, afi, abr, abi = (jnp.broadcast_to(tr[i:i + 1], (NSEG, half)) for i in range(4))
    gfr, gfi, gbr, gbi = (tr[i:i + 1] for i in range(4, 8))

    def load_part(part, i):
        return jnp.concatenate(
            [v_ref[part * nq + q, pl.ds(pl.multiple_of(i * NSEG, NSEG), NSEG), :] for q in range(nq)],
            axis=-1)

    def store_part(part, i, val):
        for q in range(nq):
            v_ref[part * nq + q, pl.ds(pl.multiple_of(i * NSEG, NSEG), NSEG), :] = (
                val[:, q * LANES:(q + 1) * LANES])

    def step(i, carry, write):
        fr, fi, br, bi = carry
        ib = SEG - 1 - i
        ufr, ufi = load_part(0, i), load_part(1, i)
        ubr, ubi = load_part(2, ib), load_part(3, ib)
        if write:
            store_part(0, i, fr)
            store_part(1, i, fi)
            store_part(2, ib, br)
            store_part(3, ib, bi)
        return (afr * fr - afi * fi + ufr, afr * fi + afi * fr + ufi,
                abr * br - abi * bi + ubr, abr * bi + abi * br + ubi)

    zero = jnp.zeros((NSEG, half), F32)
    ffr, ffi, fbr, fbi = lax.fori_loop(0, SEG, functools.partial(step, write=False),
                                       (zero, zero, zero, zero), unroll=SCAN_UNROLL)

    rows_fr, rows_fi = [s0_fr], [s0_fi]
    for k in range(1, NSEG):
        pr, pi = rows_fr[-1], rows_fi[-1]
        rows_fr.append(gfr * pr - gfi * pi + ffr[k - 1:k])
        rows_fi.append(gfr * pi + gfi * pr + ffi[k - 1:k])
    rows_br, rows_bi = [s0_br], [s0_bi]
    for k in range(NSEG - 2, -1, -1):
        pr, pi = rows_br[0], rows_bi[0]
        rows_br.insert(0, gbr * pr - gbi * pi + fbr[k + 1:k + 2])
        rows_bi.insert(0, gbr * pi + gbi * pr + fbi[k + 1:k + 2])
    init = tuple(jnp.concatenate(r, axis=0) for r in (rows_fr, rows_fi, rows_br, rows_bi))

    lax.fori_loop(0, SEG, functools.partial(step, write=True), init, unroll=SCAN_UNROLL)

    def emit(k, c):
        r0 = pl.multiple_of(k * SEG, SEG)
        b = chunk_rows(p_ref, r0, SEG)
        sin = jnp.concatenate([v_ref[s, pl.ds(k, SEG, stride=NSEG), :] for s in range(4 * nq)], axis=-1)
        y = (jnp.dot(b, wm_ref[0], preferred_element_type=F32)
             + jnp.dot(sin.astype(BF16), wp_ref[0], preferred_element_type=F32))
        for t in range(CH):
            y_ref[t, pl.ds(r0, SEG), :] = y[:, t * LANES:(t + 1) * LANES].astype(y_ref.dtype)
        return c
    lax.fori_loop(0, NSEG, emit, 0)
    clears(lambda cp: cp.wait())


def _s5(p_t, pc_t, b_c, a_q, a_p, trans, ctx_w):
    rep = np.ones((1, GPT))
    c16 = jnp.asarray(np.kron(rep, np.eye(S5_GROUP)), F32).astype(BF16)
    c64 = jnp.asarray(np.kron(rep, np.eye(S5_STATE)), F32).astype(BF16)
    c_m = jnp.asarray(np.kron(np.eye(CH), np.kron(rep, np.eye(S5_GROUP))), F32).astype(BF16)
    b_c, a_q, a_p = b_c.astype(BF16), a_q.astype(BF16), a_p.astype(BF16)
    half = GPT * S5_STATE
    return pl.pallas_call(
        _s5_kernel,
        grid=(NJ,),
        in_specs=[pl.BlockSpec((CH, N_CHUNK, LANES), lambda j: (0, 0, j)),
                  pl.BlockSpec((CH, N_CHUNK_CTX, LANES), lambda j: (0, 0, j)),
                  pl.BlockSpec((1, 2 * CH, LANES, S5_GROUP), lambda j: (j, 0, 0, 0)),
                  pl.BlockSpec((4, 1, CL, S5_STATE), lambda j: (0, j, 0, 0)),
                  pl.BlockSpec((4, 1, half, CH * S5_GROUP), lambda j: (0, j, 0, 0)),
                  pl.BlockSpec(c16.shape, lambda j: (0, 0)),
                  pl.BlockSpec(c64.shape, lambda j: (0, 0)),
                  pl.BlockSpec(c_m.shape, lambda j: (0, 0)),
                  pl.BlockSpec((1, SUBLANES, half), lambda j: (j, 0, 0)),
                  pl.BlockSpec((1, 4, N_CHUNK_CTX, half), lambda j: (j, 0, 0, 0))],
        out_specs=(pl.BlockSpec((CH, N_CHUNK, LANES), lambda j: (0, 0, j)),
                   pl.BlockSpec(memory_space=pl.ANY)),
        out_shape=(jax.ShapeDtypeStruct((CH, N_CHUNK, S5_WIDTH), BF16),
                   jax.ShapeDtypeStruct((ROWS_ALL, D), BF16)),
        scratch_shapes=[pltpu.VMEM((4 * NJ, N_CHUNK, LANES), F32),
                        pltpu.VMEM((1, CL, CL), BF16),
                        pltpu.VMEM((1, CL, SW), BF16),
                        pltpu.VMEM((1, SW, CL), BF16),
                        pltpu.VMEM((ZERO_UNIT, D), BF16),
                        pltpu.SemaphoreType.DMA(())],
        compiler_params=_cparams(("parallel",)),
        name="s5",
    )(p_t, pc_t, b_c, a_q, a_p, c16, c64, c_m, trans, ctx_w)


def _dft_tables():
    n = np.arange(FN)
    ang = 2.0 * np.pi * np.outer(n, n) / FN
    c, s = np.cos(ang), np.sin(ang)
    st1 = np.block([[c, s], [-s, c]])
    tw = 2.0 * np.pi * np.outer(n, n) / (FN * FN)
    wr, wi = np.cos(tw), -np.sin(tw)
    fr = c[None] * wr[:, None, :] + s[None] * wi[:, None, :]
    fi = c[None] * wi[:, None, :] - s[None] * wr[:, None, :]
    st2 = np.concatenate([fr, -fi], axis=-1)
    scale = 1.0 / math.sqrt(N_TOK * FFT_DIM)
    blk_c = np.kron(np.eye(FFT_GROUPS), c) * scale
    blk_s = np.kron(np.eye(FFT_GROUPS), s) * scale
    fc = np.concatenate([blk_c, -blk_s], axis=1)
    return (jnp.asarray(st1, F32).astype(BF16), jnp.asarray(st2, F32).astype(BF16), jnp.asarray(fc, F32))


FSL = FFT_WIDTH // LANES


FBH = FB // SUBLANES


def _block_to_slabs(blk, slab_ref, first, per_half):
    for bh in range(FBH):
        val = blk[:, bh * SUBLANES:(bh + 1) * SUBLANES, :].reshape(FN * SUBLANES, FFT_WIDTH)
        for s in range(FSL):
            slab_ref[bh * per_half + first + s] = val[:, s * LANES:(s + 1) * LANES]


def _slab_rows(b, first, per_half):
    return (b // SUBLANES) * per_half + first, pl.ds(b % SUBLANES, FN, stride=SUBLANES)


def _slabs_to_block(slab_ref, first, per_half):
    halves = []
    for bh in range(FBH):
        val = jnp.concatenate([slab_ref[bh * per_half + first + s] for s in range(FSL)], axis=-1)
        halves.append(val.reshape(FN, SUBLANES, FFT_WIDTH))
    return jnp.concatenate(halves, axis=1)


def _fft1_kernel(xr_ref, xi_ref, f_ref, yr_ref, yi_ref, in_ref, out_ref):
    _block_to_slabs(xr_ref[...].astype(F32), in_ref, 0, 2 * FSL)
    _block_to_slabs(xi_ref[...].astype(F32), in_ref, FSL, 2 * FSL)
    for b in range(FB):
        def part(first):
            base, rows = _slab_rows(b, first, 2 * FSL)
            return jnp.concatenate([in_ref[base + s, rows, :] for s in range(FSL)], axis=-1)
        xs = jnp.concatenate([part(0), part(FSL)], axis=0).astype(BF16)
        y = jnp.dot(f_ref[...], xs, preferred_element_type=F32)
        base, rows = _slab_rows(b, 0, 2 * FSL)
        for s in range(FSL):
            out_ref[base + s, rows, :] = y[:FN, s * LANES:(s + 1) * LANES]
            out_ref[base + FSL + s, rows, :] = y[FN:, s * LANES:(s + 1) * LANES]
    yr_ref[...] = _slabs_to_block(out_ref, 0, 2 * FSL).astype(BF16)
    yi_ref[...] = _slabs_to_block(out_ref, FSL, 2 * FSL).astype(BF16)


def _fft1(xr, xi, st1):
    spec = pl.BlockSpec((FN, FB, FFT_WIDTH), lambda i: (0, i, 0))
    slabs = pltpu.VMEM((FBH * 2 * FSL, FN * SUBLANES, LANES), F32)
    return pl.pallas_call(
        _fft1_kernel,
        grid=(FN // FB,),
        in_specs=[spec, spec, pl.BlockSpec((2 * FN, 2 * FN), lambda i: (0, 0))],
        out_specs=(spec, spec),
        out_shape=(jax.ShapeDtypeStruct((FN, FN, FFT_WIDTH), BF16),) * 2,
        scratch_shapes=[slabs, slabs],
        compiler_params=_cparams(("parallel",)),
        name="fft1",
    )(xr.reshape(FN, FN, FFT_WIDTH), xi.reshape(FN, FN, FFT_WIDTH), st1)


def _fft2_kernel(yr_ref, yi_ref, f_ref, z_ref, out_ref):
    for b in range(FB):
        ys = jnp.concatenate([yr_ref[b * FN:(b + 1) * FN, :], yi_ref[b * FN:(b + 1) * FN, :]], axis=0)
        z = jnp.dot(f_ref[b], ys, preferred_element_type=F32)
        base, rows = _slab_rows(b, 0, FSL)
        for s in range(FSL):
            out_ref[base + s, rows, :] = z[:, s * LANES:(s + 1) * LANES]
    z_ref[...] = _slabs_to_block(out_ref, 0, FSL).astype(BF16)


def _fft2(yr, yi, st2):
    rows = pl.BlockSpec((FB * FN, FFT_WIDTH), lambda i: (i, 0))
    z = pl.pallas_call(
        _fft2_kernel,
        grid=(FN // FB,),
        in_specs=[rows, rows, pl.BlockSpec((FB, FN, 2 * FN), lambda i: (i, 0, 0))],
        out_specs=pl.BlockSpec((FN, FB, FFT_WIDTH), lambda i: (0, i, 0)),
        out_shape=jax.ShapeDtypeStruct((FN, FN, FFT_WIDTH), BF16),
        scratch_shapes=[pltpu.VMEM((FBH * FSL, FN * SUBLANES, LANES), F32)],
        compiler_params=_cparams(("parallel",)),
        name="fft2",
    )(yr.reshape(N_TOK, FFT_WIDTH), yi.reshape(N_TOK, FFT_WIDTH), st2)
    return z.reshape(N_TOK, FFT_WIDTH)


def _gelu_tanh(x):
    return 0.5 * x * (1.0 + jnp.tanh(math.sqrt(2.0 / math.pi) * (x + 0.044715 * (x * x * x))))


def _mix_kernel(h_ref, m1_ref, s1_ref, wgs_ref, wgf_ref, bg_ref,
                yt_ref, zr_ref, wglu_ref, bglu_ref, wbs_ref, wbf_ref, bbf_ref, wo_ref, bo_ref,
                g1_ref, l1g_ref, l1b_ref, m2_ref, s2_ref, wr_ref, br_ref, tri_ref, etri_ref,
                h1_ref, u2_ref, pos_ref, gate_ref, cnt_ref, scr_ref):
    def front(r0, nr):
        rows = slice(r0, r0 + nr)
        h = h_ref[rows, :]
        u_half = (h * (0.5 * m1_ref[...]) + 0.5 * s1_ref[...]).astype(BF16)

        c0, nc = r0 // CH, nr // CH
        for t in range(CH):
            for j in range(NJ):
                scr_ref[j, pl.ds(r0 + t, nc, stride=CH), :] = (
                    yt_ref[t, c0:c0 + nc, j * LANES:(j + 1) * LANES].astype(F32))
        ys = jnp.concatenate([scr_ref[j, rows, :] for j in range(NJ)], axis=-1)
        z = jnp.dot(_gelu_tanh(ys).astype(BF16), wglu_ref[...], preferred_element_type=F32) + bglu_ref[...]
        glu = (z[:, :S5_WIDTH] * _sigmoid(z[:, S5_WIDTH:])).astype(BF16)
        t_s5 = jnp.tanh(jnp.dot(u_half, wgs_ref[...], preferred_element_type=F32) + 0.5 * bg_ref[:, :D])
        t_fft = jnp.tanh(jnp.dot(u_half, wgf_ref[...], preferred_element_type=F32) + 0.5 * bg_ref[:, D:])
        y_s5 = jnp.dot(glu, wbs_ref[...], preferred_element_type=F32)
        y_fft = jnp.dot(zr_ref[rows, :], wbf_ref[...], preferred_element_type=F32) + bbf_ref[...]
        mixed2 = ((t_s5 + 1.0) * y_s5 + (t_fft + 1.0) * y_fft).astype(BF16)
        y = 0.5 * jnp.dot(mixed2, wo_ref[...], preferred_element_type=F32) + bo_ref[...]
        h1 = _layer_norm(ALPHA * h + g1_ref[...] * y, l1g_ref[...], l1b_ref[...])
        h1_ref[rows, :] = h1
        u2 = h1 * m2_ref[...] + s2_ref[...]
        u2_ref[rows, :] = u2.astype(BF16)
        u_hi = u2.astype(BF16)
        u_lo = (u2 - u_hi.astype(F32)).astype(BF16)

        def nt(a, b):
            return lax.dot_general(a, b, (((1,), (1,)), ((), ())), preferred_element_type=F32)
        return nt(wr_ref[0], u_hi) + nt(wr_ref[0], u_lo) + nt(wr_ref[1], u_hi)

    logits = front(0, TM) + br_ref[:, 0:1]
    eidx = lax.broadcasted_iota(jnp.int32, (N_EXPERTS, TM), 0)
    vals, hots = [], []
    cur = logits
    for _k in range(TOP_K):
        m = jnp.max(cur, axis=0, keepdims=True)
        sel = jnp.min(jnp.where(cur == m, eidx, N_EXPERTS), axis=0, keepdims=True)
        hot = eidx == sel
        cur = jnp.where(hot, -jnp.inf, cur)
        vals.append(m)
        hots.append(hot)
    exps = [jnp.exp(v - vals[0]) for v in vals]
    den = exps[0] + exps[1] + exps[2] + exps[3]
    gate4 = jnp.concatenate([e / den for e in exps], axis=0)

    hot_sum = (hots[0] | hots[1] | hots[2] | hots[3]).astype(F32)
    before = jnp.dot(hot_sum.astype(BF16), tri_ref[...], preferred_element_type=F32)
    cnt = jnp.broadcast_to(jnp.sum(hot_sum, axis=1, keepdims=True), (N_EXPERTS, LANES))
    cnt8 = jnp.floor((cnt + (SEG_ALIGN - 1)) * (1.0 / SEG_ALIGN)) * SEG_ALIGN
    seg0 = jnp.dot(etri_ref[...], cnt8.astype(BF16), preferred_element_type=F32)
    tot = seg0[:, 0:1] + before
    pos4 = jnp.concatenate(
        [jnp.sum(jnp.where(hk, tot, 0.0), axis=0, keepdims=True) for hk in hots], axis=0)
    pos_ref[...] = pos4.astype(jnp.int32)
    cnt_ref[0] = cnt

    gate_ref[...] = gate4


def _mix(h, m1, s1, w_in, bg, y_t, zr, wglu, bglu, wbs, wbf, bbf, wo, bo,
         g1, l1g, l1b, m2, s2, wr_t, br, tri, etri):
    gate_cols = (S5_WIDTH + FFT_WIDTH) // D
    vec = pl.BlockSpec((1, D), lambda i: (0, 0))

    def full(a):
        return pl.BlockSpec(a.shape, lambda i: (0,) * a.ndim)
    return pl.pallas_call(
        _mix_kernel,
        grid=(N_TOK // TM,),
        in_specs=[pl.BlockSpec((TM, D), lambda i: (i, 0)),
                  vec, vec,
                  pl.BlockSpec((D, D), lambda i: (0, gate_cols)),
                  pl.BlockSpec((D, D), lambda i: (0, gate_cols + 1)), full(bg),
                  pl.BlockSpec((CH, TM // CH, S5_WIDTH), lambda i: (0, i, 0)),
                  pl.BlockSpec((TM, FFT_WIDTH), lambda i: (i, 0)),
                  full(wglu), full(bglu), full(wbs), full(wbf), full(bbf), full(wo), full(bo),
                  vec, vec, vec, vec, vec, full(wr_t), full(br), full(tri), full(etri)],
        out_specs=(pl.BlockSpec((TM, D), lambda i: (i, 0)),
                   pl.BlockSpec((TM, D), lambda i: (i, 0)),
                   pl.BlockSpec((TOP_K, TM), lambda i: (0, i)),
                   pl.BlockSpec((TOP_K, TM), lambda i: (0, i)),
                   pl.BlockSpec((1, N_EXPERTS, LANES), lambda i: (i, 0, 0))),
        out_shape=(jax.ShapeDtypeStruct((N_TOK, D), F32),
                   jax.ShapeDtypeStruct((N_TOK, D), BF16),
                   jax.ShapeDtypeStruct((TOP_K, N_TOK), jnp.int32),
                   jax.ShapeDtypeStruct((TOP_K, N_TOK), F32),
                   jax.ShapeDtypeStruct((N_TILES, N_EXPERTS, LANES), F32)),
        scratch_shapes=[pltpu.VMEM((NJ, TM, LANES), F32)],
        compiler_params=_cparams(("parallel",)),
        name="mix",
    )(h, m1, s1, w_in, w_in, bg, y_t, zr, wglu, bglu, wbs, wbf, bbf, wo, bo,
      g1, l1g, l1b, m2, s2, wr_t, br, tri, etri)


def _on_parity(i, fn):
    @pl.when(i % 2 == 0)
    def _():
        fn(0)

    @pl.when(i % 2 == 1)
    def _():
        fn(1)


def _dispatch_kernel(dprev_ref, dest_ref, pos_ref, u_ref, zeroed_ref, buf_ref, sorted_ref, sems):
    del zeroed_ref
    i = pl.program_id(0)

    @pl.when(i == 0)
    def _():
        sorted_ref[...] = jnp.zeros_like(sorted_ref)

    def chunk_copy(slot, table_ref, j):
        dst = pl.multiple_of(table_ref[0, 0, j], SEG_ALIGN)
        return pltpu.make_async_copy(sorted_ref.at[slot, pl.ds(j * SEG_ALIGN, SEG_ALIGN)],
                                     buf_ref.at[pl.ds(dst, SEG_ALIGN)], sems.at[slot])

    def drain(slot):
        pltpu.make_async_copy(sorted_ref.at[slot], buf_ref.at[pl.ds(0, CAP)], sems.at[slot]).wait()

    def run(slot):
        pos = pos_ref[...]
        u = u_ref[...]
        n_rb = CAP // SORT_BLOCK
        per_rb = NCHK // (n_rb // 2)
        for rb in range(n_rb):
            for j in range(rb * per_rb, min((rb + 1) * per_rb, NCHK)):
                chunk_copy(1 - slot, dprev_ref, j).start()
            rows = lax.broadcasted_iota(jnp.int32, (SORT_BLOCK, TM), 0) + rb * SORT_BLOCK
            hit = rows == pos[0:1]
            for k in range(1, TOP_K):
                hit = hit | (rows == pos[k:k + 1])
            onehot = jnp.where(hit, 1.0, 0.0).astype(BF16)
            sorted_ref[slot, rb * SORT_BLOCK:(rb + 1) * SORT_BLOCK, :] = jnp.dot(
                onehot, u, preferred_element_type=F32).astype(BF16)
        drain(1 - slot)

        @pl.when(i == N_TILES - 1)
        def _():
            def issue(j, c):
                chunk_copy(slot, dest_ref, j).start()
                return c
            lax.fori_loop(0, NCHK, issue, 0)
            drain(slot)
    _on_parity(i, run)


def _dispatch(chunk_table, pos_t, u2, zeroed):
    return pl.pallas_call(
        _dispatch_kernel,
        grid=(N_TILES,),
        in_specs=[pl.BlockSpec((1, 1, NCHK), lambda i: (i, 0, 0), memory_space=pltpu.SMEM),
                  pl.BlockSpec((1, 1, NCHK), lambda i: (i + 1, 0, 0), memory_space=pltpu.SMEM),
                  pl.BlockSpec((TOP_K, TM), lambda i: (0, i)),
                  pl.BlockSpec((TM, D), lambda i: (i, 0)),
                  pl.BlockSpec(memory_space=pl.ANY)],
        out_specs=pl.BlockSpec(memory_space=pl.ANY),
        out_shape=jax.ShapeDtypeStruct((ROWS_ALL, D), BF16),
        scratch_shapes=[pltpu.VMEM((2, CAP, D), BF16),
                        pltpu.SemaphoreType.DMA((2,))],
        input_output_aliases={4: 0},
        compiler_params=_cparams(("arbitrary",)),
        name="dispatch",
    )(chunk_table, chunk_table, pos_t, u2, zeroed)


def _ffn_kernel(be_ref, nu_ref, run_ref, nxt_ref, valid_ref, x_ref, wu_hbm, bu_ref, wd_hbm, bd_ref,
                y_ref, wu_ref, wd_ref, wub_ref, wdb_ref, sems):
    i = pl.program_id(0)
    used = i < nu_ref[0]

    def weight_copies(e, slot):
        return (pltpu.make_async_copy(wu_hbm.at[e], wu_ref.at[slot], sems.at[slot]),
                pltpu.make_async_copy(wd_hbm.at[e], wd_ref.at[slot], sems.at[slot]))

    @pl.when(used)
    def _():
        run = run_ref[i]

        @pl.when(run >= 0)
        def _():
            def open_run(slot):
                @pl.when(run == 0)
                def _():
                    for cp in weight_copies(be_ref[i], slot):
                        cp.start()

                @pl.when(nxt_ref[i] >= 0)
                def _():
                    for cp in weight_copies(nxt_ref[i], 1 - slot):
                        cp.start()
                for cp in weight_copies(be_ref[i], slot):
                    cp.wait()
                wub_ref[...] = wu_ref[slot].astype(BF16)
                wdb_ref[...] = wd_ref[slot].astype(BF16)
            _on_parity(run, open_run)

        def expert_rows(r0, nr):
            rows = slice(r0, r0 + nr)
            e = be_ref[i]
            h = (jnp.dot(x_ref[rows, :], wub_ref[...], preferred_element_type=F32)
                 + bu_ref[pl.ds(e, 1), :])
            h_glu = jnp.minimum(h[:, :D], SWIGLU_LIMIT)
            h_lin = jnp.clip(h[:, D:], -SWIGLU_LIMIT, SWIGLU_LIMIT)
            act = (h_glu * _sigmoid(SWIGLU_ALPHA * h_glu) * (h_lin + 1.0)).astype(BF16)
            y_ref[rows, :] = (jnp.dot(act, wdb_ref[...], preferred_element_type=F32)
                              + bd_ref[pl.ds(e, 1), :]).astype(BF16)

        valid = valid_ref[i]

        @pl.when(valid == BM)
        def _():
            expert_rows(0, BM)

        @pl.when(valid < BM)
        def _():
            for h0 in range(0, BM, FFN_HALF):
                @pl.when(valid >= h0 + FFN_HALF)
                def _(h0=h0):
                    expert_rows(h0, FFN_HALF)

                @pl.when(valid < h0 + FFN_HALF)
                def _(h0=h0):
                    for r0 in range(h0, h0 + FFN_HALF, FFN_TAIL):
                        @pl.when(r0 < valid)
                        def _(r0=r0):
                            expert_rows(r0, FFN_TAIL)

                        @pl.when(r0 >= valid)
                        def _(r0=r0):
                            y_ref[r0:r0 + FFN_TAIL, :] = jnp.zeros((FFN_TAIL, D), BF16)


def _ffn(block_expert, n_used, run_id, next_expert, valid, buf, w_up, b_up, w_down, b_down):
    def blk(i, be, nu, *_):
        return jnp.minimum(i, nu[0] - 1)
    return pl.pallas_call(
        _ffn_kernel,
        grid_spec=pltpu.PrefetchScalarGridSpec(
            num_scalar_prefetch=5,
            grid=(N_BLOCKS_ALL,),
            in_specs=[pl.BlockSpec((BM, D), lambda i, *s: (blk(i, *s), 0)),
                      pl.BlockSpec(memory_space=pl.ANY),
                      pl.BlockSpec((N_EXPERTS, 2 * D), lambda i, *s: (0, 0)),
                      pl.BlockSpec(memory_space=pl.ANY),
                      pl.BlockSpec((N_EXPERTS, D), lambda i, *s: (0, 0))],
            out_specs=pl.BlockSpec((BM, D), lambda i, *s: (blk(i, *s), 0)),
            scratch_shapes=[pltpu.VMEM((2, D, 2 * D), F32),
                            pltpu.VMEM((2, D, D), F32),
                            pltpu.VMEM((D, 2 * D), BF16),
                            pltpu.VMEM((D, D), BF16),
                            pltpu.SemaphoreType.DMA((2,))]),
        out_shape=jax.ShapeDtypeStruct((ROWS_ALL, D), BF16),
        input_output_aliases={5: 0},
        compiler_params=_cparams(("arbitrary",)),
        name="ffn",
    )(block_expert, n_used, run_id, next_expert, valid, buf, w_up, b_up, w_down, b_down)


def _combine_kernel(dest_ref, dnext_ref, y_ref, h1_ref, pos_ref, gate_ref, g2_ref, lg_ref, lb_ref,
                    o_ref, sorted_ref, sems):
    i = pl.program_id(0)

    def chunk_copy(slot, table_ref, j):
        src = pl.multiple_of(table_ref[0, 0, j], SEG_ALIGN)
        return pltpu.make_async_copy(y_ref.at[pl.ds(src, SEG_ALIGN)],
                                     sorted_ref.at[slot, pl.ds(j * SEG_ALIGN, SEG_ALIGN)],
                                     sems.at[slot])

    def drain(slot):
        pltpu.make_async_copy(y_ref.at[pl.ds(0, CAP)], sorted_ref.at[slot], sems.at[slot]).wait()

    @pl.when(i == 0)
    def _():
        def issue(j, c):
            chunk_copy(0, dest_ref, j).start()
            return c
        lax.fori_loop(0, NCHK, issue, 0)

    def run(slot):
        drain(slot)

        pos = pos_ref[...]
        gate = gate_ref[...]
        m = jnp.zeros((TM, D), F32)
        n_cb = CAP // CAP_BLOCK
        per_cb = NCHK // (n_cb // 2)
        for cb in range(n_cb):
            for j in range(cb * per_cb, min((cb + 1) * per_cb, NCHK)):
                chunk_copy(1 - slot, dnext_ref, j).start()
            rws = lax.broadcasted_iota(jnp.int32, (CAP_BLOCK, TM), 0) + cb * CAP_BLOCK
            g = jnp.where(rws == pos[0:1], gate[0:1], 0.0)
            for k in range(1, TOP_K):
                g = g + jnp.where(rws == pos[k:k + 1], gate[k:k + 1], 0.0)
            rows = sorted_ref[slot, cb * CAP_BLOCK:(cb + 1) * CAP_BLOCK, :]
            m = m + lax.dot_general(g.astype(BF16), rows, (((0,), (0,)), ((), ())),
                                    preferred_element_type=F32)
        o_ref[...] = _layer_norm(ALPHA * h1_ref[...] + g2_ref[...] * m, lg_ref[...], lb_ref[...])

        @pl.when(i == N_TILES - 1)
        def _():
            drain(1 - slot)
    _on_parity(i, run)


def _combine(chunk_table, y_buf, h1, pos_t, gate_t, g2, lg, lb):
    vec = pl.BlockSpec((1, D), lambda i: (0, 0))
    return pl.pallas_call(
        _combine_kernel,
        grid_spec=pltpu.PrefetchScalarGridSpec(
            num_scalar_prefetch=0,
            grid=(N_TILES,),
            in_specs=[pl.BlockSpec((1, 1, NCHK), lambda i: (i + 1, 0, 0), memory_space=pltpu.SMEM),
                      pl.BlockSpec((1, 1, NCHK), lambda i: (i + 2, 0, 0), memory_space=pltpu.SMEM),
                      pl.BlockSpec(memory_space=pl.ANY),
                      pl.BlockSpec((TM, D), lambda i: (i, 0)),
                      pl.BlockSpec((TOP_K, TM), lambda i: (0, i)),
                      pl.BlockSpec((TOP_K, TM), lambda i: (0, i)),
                      vec, vec, vec],
            out_specs=pl.BlockSpec((TM, D), lambda i: (i, 0)),
            scratch_shapes=[pltpu.VMEM((2, CAP, D), BF16),
                            pltpu.SemaphoreType.DMA((2,))]),
        out_shape=jax.ShapeDtypeStruct((N_TOK, D), F32),
        compiler_params=_cparams(("arbitrary",)),
        name="combine",
    )(chunk_table, chunk_table, y_buf, h1, pos_t, gate_t, g2, lg, lb)


def _sincos_tables():
    q = D // 4
    omega = 1.0 / (10000.0 ** (np.arange(q) / q))

    def emb(n):
        ang = np.arange(n)[:, None] * omega[None, :]
        return jnp.asarray(np.concatenate([np.sin(ang), np.cos(ang)], axis=-1), F32)
    return emb(N_TOK // GRID_W), emb(GRID_W)


def kernel(x, c, ctx, c_ctx, ln_in_g, ln_in_b, w_ada, b_ada, w_in, b_in, s5_lambda_re, s5_lambda_im, s5_log_dt, s5_b_re, s5_b_im, s5_c_re, s5_c_im, s5_d, w_glu, b_glu, w_br_s5, w_br_fft, b_br_fft, w_out, b_out, ln1_g, ln1_b, w_router, b_router, w_up, b_up, w_down, b_down, ln2_g, ln2_b):
    assert x.shape == (1, N_TOK, D) and ctx.shape == (1, N_CTX, D) and w_ada.shape[0] == 1
    row = lambda v: v.reshape(1, -1).astype(F32)

    cc = jnp.concatenate([c.reshape(1, D), c_ctx.reshape(1, D), jnp.zeros((SUBLANES - 2, D), F32)], axis=0)
    ada = _ada(cc, w_ada[0], row(b_ada[0]))
    sh1, sc1, g1, sh2, sc2, g2 = (ada[0:1, k * D:(k + 1) * D] for k in range(6))
    sh1c, sc1c = ada[1:2, 0:D], ada[1:2, D:2 * D]

    emb_r, emb_c = _sincos_tables()
    st1, st2, fc = _dft_tables()
    lg, lb = row(ln_in_g), row(ln_in_b)

    w_in_bf = w_in[0].astype(BF16)
    b_s5 = row(b_in[0][:S5_WIDTH])
    b_fft8 = jnp.concatenate([row(b_in[0][S5_WIDTH:S5_WIDTH + FFT_WIDTH]),
                              jnp.zeros((SUBLANES - 1, FFT_WIDTH), F32)], axis=0)
    b_g = row(b_in[0][S5_WIDTH + FFT_WIDTH:])
    w_fc, b_fc = _fft_weights(w_in[0], b_fft8, fc)
    bcat = jnp.concatenate([b_s5, b_fc[0:1]], axis=1)

    x2 = x[0]
    p_t, xr, xi, h0 = _proj(x2, emb_r, emb_c, lg, lb, 1.0 + sc1, sh1, w_in_bf, w_fc, bcat)
    pc_t = _ctx_proj(ctx[0], lg, lb, 1.0 + sc1c, sh1c, w_in_bf[:, :S5_WIDTH], b_s5)

    b_c, a_q, a_p, trans, ctx_w = _s5_tables(
        s5_lambda_re[0], s5_lambda_im[0], s5_log_dt[0], s5_b_re[0], s5_b_im[0],
        s5_c_re[0], s5_c_im[0], s5_d[0])
    y_t, zeroed = _s5(p_t, pc_t, b_c, a_q, a_p, trans, ctx_w)

    yr, yi = _fft1(xr, xi, st1)
    zr = _fft2(yr, yi, st2)

    tri = jnp.asarray(np.arange(TM)[:, None] < np.arange(TM)[None, :], BF16)
    br = jnp.broadcast_to(b_router[0].reshape(N_EXPERTS, 1), (N_EXPERTS, LANES))
    etri = jnp.asarray(np.arange(N_EXPERTS)[:, None] > np.arange(N_EXPERTS)[None, :], BF16)
    wr_t = jnp.transpose(w_router[0])
    wr_hi = wr_t.astype(BF16)
    wr_split = jnp.stack([wr_hi, (wr_t - wr_hi.astype(F32)).astype(BF16)], axis=0)
    h1, u2, pos_t, gate_t, counts = _mix(
        h0, 1.0 + sc1, sh1, w_in_bf, b_g, y_t, zr,
        w_glu[0].astype(BF16), row(b_glu[0]), w_br_s5[0].astype(BF16), w_br_fft[0].astype(BF16),
        row(b_br_fft[0]), w_out[0].astype(BF16), row(b_out[0]), g1, row(ln1_g[0]), row(ln1_b[0]),
        1.0 + sc2, sh2, wr_split, br, tri, etri)

    cnt = counts[:, :, 0].astype(jnp.int32)
    seg = (cnt + SEG_ALIGN - 1) // SEG_ALIGN * SEG_ALIGN
    seg_end = jnp.cumsum(seg, axis=1)
    seg_start = seg_end - seg
    padded = (jnp.sum(seg, axis=0) + BM - 1) // BM * BM
    pad_ends = jnp.cumsum(padded)
    seg_dest = (pad_ends - padded)[None, :] + jnp.cumsum(seg, axis=0) - seg
    chunk_row = jnp.arange(NCHK, dtype=jnp.int32) * SEG_ALIGN
    chunk_exp = jnp.minimum(jnp.sum(chunk_row[None, :, None] >= seg_end[:, None, :], axis=-1),
                            N_EXPERTS - 1)
    own = chunk_exp[:, :, None] == jnp.arange(N_EXPERTS, dtype=jnp.int32)[None, None, :]
    chunk_dest = (jnp.sum(jnp.where(own, (seg_dest - seg_start)[:, None, :], 0), axis=-1)
                  + chunk_row[None, :]).astype(jnp.int32).reshape(N_TILES, 1, NCHK)
    nchk = (seg_end[:, -1] // SEG_ALIGN).astype(jnp.int32)
    block_start = jnp.arange(N_BLOCKS_ALL, dtype=jnp.int32) * BM
    block_expert = jnp.minimum(jnp.sum(block_start[:, None] >= pad_ends[None, :], axis=1),
                               N_EXPERTS - 1).astype(jnp.int32)
    n_used = (pad_ends[-1:] // BM).astype(jnp.int32)
    opens = (block_start < pad_ends[-1]) & (
        block_expert != jnp.concatenate([jnp.full((1,), -1, jnp.int32), block_expert[:-1]]))
    run_id = jnp.where(opens, jnp.cumsum(opens.astype(jnp.int32)) - 1, -1).astype(jnp.int32)
    experts = jnp.arange(N_EXPERTS, dtype=jnp.int32)
    later = (experts[None, :] > block_expert[:, None]) & (padded[None, :] > 0)
    next_expert = jnp.min(jnp.where(later, experts[None, :], N_EXPERTS), axis=1)
    next_expert = jnp.where(next_expert < N_EXPERTS, next_expert, -1).astype(jnp.int32)

    spare = (ROWS + chunk_row)[None, None, :]
    chunk_table = jnp.concatenate(
        [spare,
         jnp.where(chunk_row[None, None, :] < (nchk * SEG_ALIGN)[:, None, None], chunk_dest, spare),
         spare], axis=0).astype(jnp.int32)
    fill_ends = (pad_ends - padded + jnp.sum(seg, axis=0)).astype(jnp.int32)
    buf = _dispatch(chunk_table, pos_t, u2, zeroed)
    mine = block_expert[:, None] == experts[None, :]
    filled = jnp.sum(jnp.where(mine, fill_ends[None, :], 0), axis=1)
    valid = jnp.clip(filled - block_start, 0, BM).astype(jnp.int32)
    y_buf = _ffn(block_expert, n_used, run_id, next_expert, valid, buf, w_up[0],
                 b_up[0], w_down[0], b_down[0])
    out = _combine(chunk_table, y_buf, h1, pos_t, gate_t, g2, row(ln2_g[0]), row(ln2_b[0]))
    return out.reshape(1, N_TOK, D)
```

```python
import functools
import math

import jax
import jax.numpy as jnp
import numpy as np
from jax import lax
from jax.experimental import pallas as pl
from jax.experimental.pallas import tpu as pltpu

F32 = jnp.float32
BF16 = jnp.bfloat16
HI = lax.Precision.HIGHEST

D = 1024
N_TOK = 16384
N_CTX = 256
GRID_W = 64
S5_GROUP = 16
S5_GROUPS = 32
S5_STATE = 64
S5_WIDTH = 512
FFT_GROUPS = 4
FFT_DIM = 128
FFT_WIDTH = 512
N_EXPERTS = 32
TOP_K = 4
LN_EPS = 1e-5
ALPHA = 2.0 ** 0.25
SWIGLU_ALPHA = 1.702
SWIGLU_LIMIT = 7.0

LANES = 128
SUBLANES = 8
VMEM_LIMIT = 56 * 1024 * 1024

CH = 8
N_CHUNK = N_TOK // CH
N_CHUNK_CTX = N_CTX // CH
NSEG = SUBLANES
SEG = N_CHUNK // NSEG
SCAN_UNROLL = 8
GPT = LANES // S5_GROUP
NJ = S5_WIDTH // LANES
CL = CH * LANES
SW = 4 * GPT * S5_STATE

FN = 128
FB = 16

TM = 512
TM_PROJ = 1024
N_TILES = N_TOK // TM
BM = 1024
FFN_HALF = 512
FFN_TAIL = 256
N_SLOTS = N_TOK * TOP_K
SEG_ALIGN = 2 * SUBLANES
CAP_BLOCK = 256
SORT_BLOCK = 128
CAP = -(-(TOP_K * TM + N_EXPERTS * (SEG_ALIGN - 1)) // CAP_BLOCK) * CAP_BLOCK
NCHK = CAP // SEG_ALIGN
N_BLOCKS = -(-(N_SLOTS + N_TILES * N_EXPERTS * (SEG_ALIGN - 1)) // BM) + N_EXPERTS
ROWS = N_BLOCKS * BM
N_BLOCKS_ALL = N_BLOCKS + -(-CAP // BM)
ROWS_ALL = N_BLOCKS_ALL * BM
ZERO_UNIT = 512
ZERO_PARTS = ROWS_ALL // (NJ * ZERO_UNIT)
assert ZERO_UNIT * ZERO_PARTS * NJ == ROWS_ALL


def _cparams(sem):
    return pltpu.CompilerParams(dimension_semantics=sem, vmem_limit_bytes=VMEM_LIMIT)


def _layer_norm(x, g, b):
    mu = jnp.mean(x, axis=-1, keepdims=True)
    xc = x - mu
    var = jnp.mean(xc * xc, axis=-1, keepdims=True)
    return xc * lax.rsqrt(var + LN_EPS) * g + b


def _sigmoid(x):
    return 0.5 * jnp.tanh(0.5 * x) + 0.5


def _ada_kernel(c_ref, w_ref, b_ref, o_ref):
    c = c_ref[...]
    s = c * _sigmoid(c)
    w = w_ref[...]
    s_hi, w_hi = s.astype(BF16), w.astype(BF16)
    s_lo = (s - s_hi.astype(F32)).astype(BF16)
    w_lo = (w - w_hi.astype(F32)).astype(BF16)

    def mm(a, b):
        return jnp.dot(a, b, preferred_element_type=F32)
    o_ref[...] = mm(s_hi, w_hi) + mm(s_lo, w_hi) + mm(s_hi, w_lo) + b_ref[...]


def _ada(cc, w_ada, b_ada):
    nb = 4
    wb = 6 * D // nb
    return pl.pallas_call(
        _ada_kernel,
        grid=(nb,),
        in_specs=[pl.BlockSpec((SUBLANES, D), lambda i: (0, 0)),
                  pl.BlockSpec((D, wb), lambda i: (0, i)),
                  pl.BlockSpec((1, wb), lambda i: (0, i))],
        out_specs=pl.BlockSpec((SUBLANES, wb), lambda i: (0, i)),
        out_shape=jax.ShapeDtypeStruct((SUBLANES, 6 * D), F32),
        compiler_params=_cparams(("parallel",)),
        name="ada",
    )(cc, w_ada, b_ada)


def _fftw_kernel(w_ref, b_ref, f_ref, wo_ref, bo_ref):
    f = f_ref[...]
    w = w_ref[...]
    w_hi, f_hi = w.astype(BF16), f.astype(BF16)
    w_lo = (w - w_hi.astype(F32)).astype(BF16)
    f_lo = (f - f_hi.astype(F32)).astype(BF16)

    def mm(a, b):
        return jnp.dot(a, b, preferred_element_type=F32)
    wo_ref[...] = (mm(w_hi, f_hi) + mm(w_lo, f_hi) + mm(w_hi, f_lo)).astype(BF16)
    bo_ref[...] = jnp.dot(b_ref[...], f, preferred_element_type=F32, precision=HI)


def _fft_weights(w_in, b_fft8, fc):
    def full(a):
        return pl.BlockSpec(a.shape, lambda i: (0,) * a.ndim)
    outs = (jax.ShapeDtypeStruct((D, 2 * FFT_WIDTH), BF16),
            jax.ShapeDtypeStruct((SUBLANES, 2 * FFT_WIDTH), F32))
    return pl.pallas_call(
        _fftw_kernel,
        grid=(1,),
        in_specs=[pl.BlockSpec((D, FFT_WIDTH), lambda i: (0, S5_WIDTH // FFT_WIDTH)),
                  full(b_fft8), full(fc)],
        out_specs=tuple(pl.BlockSpec(o.shape, lambda i: (0, 0)) for o in outs),
        out_shape=outs,
        compiler_params=_cparams(("arbitrary",)),
        name="fftw",
    )(w_in, b_fft8, fc)


def _pos_code(er_ref, ec_ref, tm):
    nr = tm // GRID_W
    er = er_ref[...]
    row = jnp.broadcast_to(er[:, None, :], (nr, GRID_W, D // 2)).reshape(tm, D // 2)
    col = jnp.concatenate([ec_ref[...]] * nr, axis=0)
    return jnp.concatenate([row, col], axis=-1)


def _to_chunk_major(val, scr_ref, out_ref, tm):
    for j in range(NJ):
        scr_ref[j] = val[:, j * LANES:(j + 1) * LANES]
    for t in range(CH):
        for j in range(NJ):
            piece = scr_ref[j, pl.ds(t, tm // CH, stride=CH), :]
            out_ref[t, :, j * LANES:(j + 1) * LANES] = piece.astype(out_ref.dtype)


def _proj_kernel(x_ref, er_ref, ec_ref, lg_ref, lb_ref, m_ref, s_ref, ws_ref, wf_ref, b_ref,
                 p_ref, xr_ref, xi_ref, h_ref, uh_ref, scr_ref):
    x = x_ref[...] + _pos_code(er_ref, ec_ref, TM_PROJ)
    h = _layer_norm(x, lg_ref[...], lb_ref[...])
    h_ref[...] = h
    u = (h * m_ref[...] + s_ref[...]).astype(BF16)
    uh_ref[...] = u * 0.5
    p_s5 = jnp.dot(u, ws_ref[...], preferred_element_type=F32) + b_ref[:, :S5_WIDTH]
    _to_chunk_major(p_s5, scr_ref, p_ref, TM_PROJ)
    p_f = jnp.dot(u, wf_ref[...], preferred_element_type=F32) + b_ref[:, S5_WIDTH:]
    xr_ref[...] = p_f[:, :FFT_WIDTH].astype(BF16)
    xi_ref[...] = p_f[:, FFT_WIDTH:].astype(BF16)


def _proj(x, emb_r, emb_c, lg, lb, m1, s1, w_in, w_fc, bcat):
    nw = bcat.shape[1]
    vec = pl.BlockSpec((1, D), lambda i: (0, 0))
    return pl.pallas_call(
        _proj_kernel,
        grid=(N_TOK // TM_PROJ,),
        in_specs=[pl.BlockSpec((TM_PROJ, D), lambda i: (i, 0)),
                  pl.BlockSpec((TM_PROJ // GRID_W, D // 2), lambda i: (i, 0)),
                  pl.BlockSpec((GRID_W, D // 2), lambda i: (0, 0)),
                  vec, vec, vec, vec,
                  pl.BlockSpec((D, S5_WIDTH), lambda i: (0, 0)),
                  pl.BlockSpec((D, 2 * FFT_WIDTH), lambda i: (0, 0)),
                  pl.BlockSpec((1, nw), lambda i: (0, 0))],
        out_specs=(pl.BlockSpec((CH, TM_PROJ // CH, S5_WIDTH), lambda i: (0, i, 0)),
                   pl.BlockSpec((TM_PROJ, FFT_WIDTH), lambda i: (i, 0)),
                   pl.BlockSpec((TM_PROJ, FFT_WIDTH), lambda i: (i, 0)),
                   pl.BlockSpec((TM_PROJ, D), lambda i: (i, 0)),
                   pl.BlockSpec((TM_PROJ, D), lambda i: (i, 0))),
        out_shape=(jax.ShapeDtypeStruct((CH, N_CHUNK, S5_WIDTH), BF16),
                   jax.ShapeDtypeStruct((N_TOK, FFT_WIDTH), BF16),
                   jax.ShapeDtypeStruct((N_TOK, FFT_WIDTH), BF16),
                   jax.ShapeDtypeStruct((N_TOK, D), F32),
                   jax.ShapeDtypeStruct((N_TOK, D), BF16)),
        scratch_shapes=[pltpu.VMEM((NJ, TM_PROJ, LANES), F32)],
        compiler_params=_cparams(("parallel",)),
        name="proj",
    )(x, emb_r, emb_c, lg, lb, m1, s1, w_in, w_fc, bcat)


def _ctx_proj_kernel(x_ref, lg_ref, lb_ref, m_ref, s_ref, w_ref, b_ref, p_ref, scr_ref):
    h = _layer_norm(x_ref[...], lg_ref[...], lb_ref[...])
    u = (h * m_ref[...] + s_ref[...]).astype(BF16)
    p = jnp.dot(u, w_ref[...], preferred_element_type=F32) + b_ref[...]
    _to_chunk_major(p, scr_ref, p_ref, N_CTX)


def _ctx_proj(ctx, lg, lb, m1, s1, w_s5, b_s5):
    return pl.pallas_call(
        _ctx_proj_kernel,
        out_shape=jax.ShapeDtypeStruct((CH, N_CHUNK_CTX, S5_WIDTH), BF16),
        scratch_shapes=[pltpu.VMEM((NJ, N_CTX, LANES), F32)],
        compiler_params=pltpu.CompilerParams(vmem_limit_bytes=VMEM_LIMIT),
        name="ctxproj",
    )(ctx, lg, lb, m1, s1, w_s5, b_s5)


def _s5_tables(lam_re, lam_im, log_dt, b_re, b_im, c_re, c_im, d_skip):
    dt = jnp.exp(log_dt)[..., None]
    zr = lam_re * dt
    zi = lam_im * dt

    def apow(m):
        m = jnp.asarray(m, F32)
        mag = jnp.exp(zr[..., None] * m)
        return mag * jnp.cos(zi[..., None] * m), mag * jnp.sin(zi[..., None] * m)

    a_re, a_im = apow(jnp.ones((1,), F32))
    a_re, a_im = a_re[..., 0], a_im[..., 0]
    den = lam_re * lam_re + lam_im * lam_im
    num_re = a_re - 1.0
    k_re = (num_re * lam_re + a_im * lam_im) / den
    k_im = (a_im * lam_re - num_re * lam_im) / den
    bb_re = k_re[..., None] * b_re - k_im[..., None] * b_im
    bb_im = k_re[..., None] * b_im + k_im[..., None] * b_re

    ks = jnp.arange(CH + 1, dtype=F32)
    pw_re, pw_im = apow(ks)
    kmag = jnp.exp(zr[:, :, None, :] * ks[None, None, :, None])
    pk_re = kmag * jnp.cos(zi[:, :, None, :] * ks[None, None, :, None])
    pk_im = kmag * jnp.sin(zi[:, :, None, :] * ks[None, None, :, None])
    bt_re, bt_im = jnp.swapaxes(b_re, 2, 3), jnp.swapaxes(b_im, 2, 3)
    bbt_re = k_re[:, :, None, :] * bt_re - k_im[:, :, None, :] * bt_im
    bbt_im = k_re[:, :, None, :] * bt_im + k_im[:, :, None, :] * bt_re

    ar, ai = pw_re[:, :, :, :CH, None], pw_im[:, :, :, :CH, None]
    cr = jnp.swapaxes(c_re, 2, 3)[:, :, :, None, :]
    ci = jnp.swapaxes(c_im, 2, 3)[:, :, :, None, :]
    ca = jnp.concatenate([cr * ar - ci * ai, -(cr * ai + ci * ar)], axis=2)
    ca = ca.reshape(2, S5_GROUPS, 2 * S5_STATE, CH * S5_GROUP)
    bbt = jnp.concatenate([bbt_re, bbt_im], axis=-1)
    taps = jnp.einsum('dghq,dgqn->dghn', bbt, ca, precision=HI)
    skip = (d_skip.reshape(S5_GROUPS, S5_GROUP, 1) * jnp.eye(S5_GROUP, dtype=F32)[None])
    taps = taps.at[0, :, :, :S5_GROUP].add(skip)
    b_c = jnp.transpose(taps.reshape(2, NJ, GPT, S5_GROUP, CH, S5_GROUP), (1, 4, 0, 2, 3, 5))
    b_c = b_c.reshape(NJ, 2 * CH, LANES, S5_GROUP)

    def q_part(d, descending):
        pr = pk_re[d][:, :CH, None, :]
        pi = pk_im[d][:, :CH, None, :]
        if descending:
            pr, pi = jnp.flip(pr, axis=1), jnp.flip(pi, axis=1)
        br = bbt_re[d][:, None, :, :]
        bi = bbt_im[d][:, None, :, :]
        return pr * br - pi * bi, pr * bi + pi * br

    def q_rows(v):
        v = v.reshape(NJ, GPT, CH, S5_GROUP, S5_STATE)
        return jnp.transpose(v, (0, 2, 1, 3, 4)).reshape(NJ, CL, S5_STATE)
    a_q = jnp.stack([q_rows(v) for v in q_part(0, True) + q_part(1, False)], axis=0)

    def p_part(d, descending):
        pr = pw_re[d][:, :, 1:CH + 1, None]
        pi = pw_im[d][:, :, 1:CH + 1, None]
        if descending:
            pr, pi = jnp.flip(pr, axis=2), jnp.flip(pi, axis=2)
        return (ct_re[d] * pr - ct_im[d] * pi, -(ct_re[d] * pi + ct_im[d] * pr))
    ct_re = jnp.swapaxes(c_re, 2, 3)[:, :, :, None, :]
    ct_im = jnp.swapaxes(c_im, 2, 3)[:, :, :, None, :]
    a_p = jnp.stack([v.reshape(NJ, GPT * S5_STATE, CH * S5_GROUP)
                     for v in p_part(0, False) + p_part(1, True)], axis=0)

    def lanes(v):
        return jnp.transpose(v.reshape(2, NJ, GPT * S5_STATE), (1, 0, 2))
    c_r, c_i = apow(jnp.full((1,), float(CH), F32))
    s_r, s_i = apow(jnp.full((1,), float(CH * SEG), F32))
    cr, ci, sr, si = (lanes(v[..., 0]) for v in (c_r, c_i, s_r, s_i))
    trans = jnp.stack([cr[:, 0], ci[:, 0], cr[:, 1], ci[:, 1],
                       sr[:, 0], si[:, 0], sr[:, 1], si[:, 1]], axis=1)

    cidx = jnp.arange(N_CHUNK_CTX, dtype=F32)
    wf_r, wf_i = apow(CH * (N_CHUNK_CTX - 1 - cidx))
    wb_r, wb_i = apow(CH * cidx)

    def ctx_lanes(v, d):
        return jnp.transpose(v[d].reshape(NJ, GPT * S5_STATE, N_CHUNK_CTX), (0, 2, 1))
    ctx_w = jnp.stack([ctx_lanes(wf_r, 0), ctx_lanes(wf_i, 0),
                       ctx_lanes(wb_r, 1), ctx_lanes(wb_i, 1)], axis=1)
    return b_c, a_q, a_p, trans, ctx_w


def _build_operators(bc_ref, aq_ref, ap_ref, c16_ref, c64_ref, cm_ref, wm_ref, wq_ref, wp_ref):
    def expand(a, c, row_shift, col_shift):
        w = jnp.dot(a, c, preferred_element_type=F32)
        rg = (lax.broadcasted_iota(jnp.int32, (w.shape[0], 1), 0) >> row_shift) & (GPT - 1)
        cg = (lax.broadcasted_iota(jnp.int32, (1, w.shape[1]), 1) >> col_shift) & (GPT - 1)
        return jnp.where(rg == cg, w, 0.0)

    blk = [expand(bc_ref[0, kd], c16_ref[...], 4, 4) for kd in range(2 * CH)]
    for t in range(CH):
        for u in range(CH):
            b = blk[2 * (u - t)] if u > t else blk[2 * (t - u) + 1] if u < t else blk[0] + blk[1]
            wm_ref[0, t * LANES:(t + 1) * LANES, u * LANES:(u + 1) * LANES] = b.astype(BF16)
    half = GPT * S5_STATE
    for s in range(4):
        wq_ref[0, :, s * half:(s + 1) * half] = expand(aq_ref[s, 0], c64_ref[...], 4, 6).astype(BF16)
        wp_ref[0, s * half:(s + 1) * half, :] = expand(ap_ref[s, 0], cm_ref[...], 6, 4).astype(BF16)


def _s5_kernel(p_ref, pc_ref, bc_ref, aq_ref, ap_ref, c16_ref, c64_ref, cm_ref, tr_ref, cw_ref,
               y_ref, rows_ref, v_ref, wm_ref, wq_ref, wp_ref, zero_ref, zsem):
    nq = NJ
    half = GPT * S5_STATE

    zero_ref[...] = jnp.zeros_like(zero_ref)

    def clears(fn):
        def unit(b, c):
            start = pl.multiple_of((pl.program_id(0) * ZERO_PARTS + b) * ZERO_UNIT, ZERO_UNIT)
            fn(pltpu.make_async_copy(zero_ref, rows_ref.at[pl.ds(start, ZERO_UNIT)], zsem))
            return c
        lax.fori_loop(0, ZERO_PARTS, unit, 0)
    clears(lambda cp: cp.start())
    _build_operators(bc_ref, aq_ref, ap_ref, c16_ref, c64_ref, cm_ref, wm_ref, wq_ref, wp_ref)

    def chunk_rows(ref, r0, nrows):
        return jnp.concatenate([ref[t, pl.ds(r0, nrows), :] for t in range(CH)], axis=-1)

    def fill(k, c):
        r0 = pl.multiple_of(k * SEG, SEG)
        v = jnp.dot(chunk_rows(p_ref, r0, SEG), wq_ref[0], preferred_element_type=F32)
        for s in range(4 * nq):
            v_ref[s, pl.ds(k, SEG, stride=NSEG), :] = v[:, s * LANES:(s + 1) * LANES]
        return c
    lax.fori_loop(0, NSEG, fill, 0)

    vc = jnp.dot(chunk_rows(pc_ref, 0, N_CHUNK_CTX), wq_ref[0], preferred_element_type=F32)
    vfr, vfi, vbr, vbi = (vc[:, i * half:(i + 1) * half] for i in range(4))
    wfr, wfi, wbr, wbi = (cw_ref[0, i] for i in range(4))
    s0_fr = jnp.sum(wfr * vfr - wfi * vfi, axis=0, keepdims=True)
    s0_fi = jnp.sum(wfr * vfi + wfi * vfr, axis=0, keepdims=True)
    s0_br = jnp.sum(wbr * vbr - wbi * vbi, axis=0, keepdims=True)
    s0_bi = jnp.sum(wbr * vbi + wbi * vbr, axis=0, keepdims=True)

    tr = tr_ref[0]
    afr, afi, abr, abi = (jnp.broadcast_to(tr[i:i + 1], (NSEG, half)) for i in range(4))
    gfr, gfi, gbr, gbi = (tr[i:i + 1] for i in range(4, 8))

    def load_part(part, i):
        return jnp.concatenate(
            [v_ref[part * nq + q, pl.ds(pl.multiple_of(i * NSEG, NSEG), NSEG), :] for q in range(nq)],
            axis=-1)

    def store_part(part, i, val):
        for q in range(nq):
            v_ref[part * nq + q, pl.ds(pl.multiple_of(i * NSEG, NSEG), NSEG), :] = (
                val[:, q * LANES:(q + 1) * LANES])

    def step(i, carry, write):
        fr, fi, br, bi = carry
        ib = SEG - 1 - i
        ufr, ufi = load_part(0, i), load_part(1, i)
        ubr, ubi = load_part(2, ib), load_part(3, ib)
        if write:
            store_part(0, i, fr)
            store_part(1, i, fi)
            store_part(2, ib, br)
            store_part(3, ib, bi)
        return (afr * fr - afi * fi + ufr, afr * fi + afi * fr + ufi,
                abr * br - abi * bi + ubr, abr * bi + abi * br + ubi)

    zero = jnp.zeros((NSEG, half), F32)
    ffr, ffi, fbr, fbi = lax.fori_loop(0, SEG, functools.partial(step, write=False),
                                       (zero, zero, zero, zero), unroll=SCAN_UNROLL)

    rows_fr, rows_fi = [s0_fr], [s0_fi]
    for k in range(1, NSEG):
        pr, pi = rows_fr[-1], rows_fi[-1]
        rows_fr.append(gfr * pr - gfi * pi + ffr[k - 1:k])
        rows_fi.append(gfr * pi + gfi * pr + ffi[k - 1:k])
    rows_br, rows_bi = [s0_br], [s0_bi]
    for k in range(NSEG - 2, -1, -1):
        pr, pi = rows_br[0], rows_bi[0]
        rows_br.insert(0, gbr * pr - gbi * pi + fbr[k + 1:k + 2])
        rows_bi.insert(0, gbr * pi + gbi * pr + fbi[k + 1:k + 2])
    init = tuple(jnp.concatenate(r, axis=0) for r in (rows_fr, rows_fi, rows_br, rows_bi))

    lax.fori_loop(0, SEG, functools.partial(step, write=True), init, unroll=SCAN_UNROLL)

    def emit(k, c):
        r0 = pl.multiple_of(k * SEG, SEG)
        b = chunk_rows(p_ref, r0, SEG)
        sin = jnp.concatenate([v_ref[s, pl.ds(k, SEG, stride=NSEG), :] for s in range(4 * nq)], axis=-1)
        y = (jnp.dot(b, wm_ref[0], preferred_element_type=F32)
             + jnp.dot(sin.astype(BF16), wp_ref[0], preferred_element_type=F32))
        for t in range(CH):
            y_ref[t, pl.ds(r0, SEG), :] = y[:, t * LANES:(t + 1) * LANES].astype(y_ref.dtype)
        return c
    lax.fori_loop(0, NSEG, emit, 0)
    clears(lambda cp: cp.wait())


def _s5(p_t, pc_t, b_c, a_q, a_p, trans, ctx_w):
    rep = np.ones((1, GPT))
    c16 = jnp.asarray(np.kron(rep, np.eye(S5_GROUP)), F32).astype(BF16)
    c64 = jnp.asarray(np.kron(rep, np.eye(S5_STATE)), F32).astype(BF16)
    c_m = jnp.asarray(np.kron(np.eye(CH), np.kron(rep, np.eye(S5_GROUP))), F32).astype(BF16)
    b_c, a_q, a_p = b_c.astype(BF16), a_q.astype(BF16), a_p.astype(BF16)
    half = GPT * S5_STATE
    return pl.pallas_call(
        _s5_kernel,
        grid=(NJ,),
        in_specs=[pl.BlockSpec((CH, N_CHUNK, LANES), lambda j: (0, 0, j)),
                  pl.BlockSpec((CH, N_CHUNK_CTX, LANES), lambda j: (0, 0, j)),
                  pl.BlockSpec((1, 2 * CH, LANES, S5_GROUP), lambda j: (j, 0, 0, 0)),
                  pl.BlockSpec((4, 1, CL, S5_STATE), lambda j: (0, j, 0, 0)),
                  pl.BlockSpec((4, 1, half, CH * S5_GROUP), lambda j: (0, j, 0, 0)),
                  pl.BlockSpec(c16.shape, lambda j: (0, 0)),
                  pl.BlockSpec(c64.shape, lambda j: (0, 0)),
                  pl.BlockSpec(c_m.shape, lambda j: (0, 0)),
                  pl.BlockSpec((1, SUBLANES, half), lambda j: (j, 0, 0)),
                  pl.BlockSpec((1, 4, N_CHUNK_CTX, half), lambda j: (j, 0, 0, 0))],
        out_specs=(pl.BlockSpec((CH, N_CHUNK, LANES), lambda j: (0, 0, j)),
                   pl.BlockSpec(memory_space=pl.ANY)),
        out_shape=(jax.ShapeDtypeStruct((CH, N_CHUNK, S5_WIDTH), BF16),
                   jax.ShapeDtypeStruct((ROWS_ALL, D), BF16)),
        scratch_shapes=[pltpu.VMEM((4 * NJ, N_CHUNK, LANES), F32),
                        pltpu.VMEM((1, CL, CL), BF16),
                        pltpu.VMEM((1, CL, SW), BF16),
                        pltpu.VMEM((1, SW, CL), BF16),
                        pltpu.VMEM((ZERO_UNIT, D), BF16),
                        pltpu.SemaphoreType.DMA(())],
        compiler_params=_cparams(("parallel",)),
        name="s5",
    )(p_t, pc_t, b_c, a_q, a_p, c16, c64, c_m, trans, ctx_w)


def _dft_tables():
    n = np.arange(FN)
    ang = 2.0 * np.pi * np.outer(n, n) / FN
    c, s = np.cos(ang), np.sin(ang)
    st1 = np.block([[c, s], [-s, c]])
    tw = 2.0 * np.pi * np.outer(n, n) / (FN * FN)
    wr, wi = np.cos(tw), -np.sin(tw)
    fr = c[None] * wr[:, None, :] + s[None] * wi[:, None, :]
    fi = c[None] * wi[:, None, :] - s[None] * wr[:, None, :]
    st2 = np.concatenate([fr, -fi], axis=-1)
    scale = 1.0 / math.sqrt(N_TOK * FFT_DIM)
    blk_c = np.kron(np.eye(FFT_GROUPS), c) * scale
    blk_s = np.kron(np.eye(FFT_GROUPS), s) * scale
    fc = np.concatenate([blk_c, -blk_s], axis=1)
    return (jnp.asarray(st1, F32).astype(BF16), jnp.asarray(st2, F32).astype(BF16), jnp.asarray(fc, F32))


FSL = FFT_WIDTH // LANES


FBH = FB // SUBLANES


def _block_to_slabs(blk, slab_ref, first, per_half):
    for bh in range(FBH):
        val = blk[:, bh * SUBLANES:(bh + 1) * SUBLANES, :].reshape(FN * SUBLANES, FFT_WIDTH)
        for s in range(FSL):
            slab_ref[bh * per_half + first + s] = val[:, s * LANES:(s + 1) * LANES]


def _slab_rows(b, first, per_half):
    return (b // SUBLANES) * per_half + first, pl.ds(b % SUBLANES, FN, stride=SUBLANES)


def _slabs_to_block(slab_ref, first, per_half):
    halves = []
    for bh in range(FBH):
        val = jnp.concatenate([slab_ref[bh * per_half + first + s] for s in range(FSL)], axis=-1)
        halves.append(val.reshape(FN, SUBLANES, FFT_WIDTH))
    return jnp.concatenate(halves, axis=1)


def _fft1_kernel(xr_ref, xi_ref, f_ref, yr_ref, yi_ref, in_ref, out_ref):
    _block_to_slabs(xr_ref[...].astype(F32), in_ref, 0, 2 * FSL)
    _block_to_slabs(xi_ref[...].astype(F32), in_ref, FSL, 2 * FSL)
    for b in range(FB):
        def part(first):
            base, rows = _slab_rows(b, first, 2 * FSL)
            return jnp.concatenate([in_ref[base + s, rows, :] for s in range(FSL)], axis=-1)
        xs = jnp.concatenate([part(0), part(FSL)], axis=0).astype(BF16)
        y = jnp.dot(f_ref[...], xs, preferred_element_type=F32)
        base, rows = _slab_rows(b, 0, 2 * FSL)
        for s in range(FSL):
            out_ref[base + s, rows, :] = y[:FN, s * LANES:(s + 1) * LANES]
            out_ref[base + FSL + s, rows, :] = y[FN:, s * LANES:(s + 1) * LANES]
    yr_ref[...] = _slabs_to_block(out_ref, 0, 2 * FSL).astype(BF16)
    yi_ref[...] = _slabs_to_block(out_ref, FSL, 2 * FSL).astype(BF16)


def _fft1(xr, xi, st1):
    spec = pl.BlockSpec((FN, FB, FFT_WIDTH), lambda i: (0, i, 0))
    slabs = pltpu.VMEM((FBH * 2 * FSL, FN * SUBLANES, LANES), F32)
    return pl.pallas_call(
        _fft1_kernel,
        grid=(FN // FB,),
        in_specs=[spec, spec, pl.BlockSpec((2 * FN, 2 * FN), lambda i: (0, 0))],
        out_specs=(spec, spec),
        out_shape=(jax.ShapeDtypeStruct((FN, FN, FFT_WIDTH), BF16),) * 2,
        scratch_shapes=[slabs, slabs],
        compiler_params=_cparams(("parallel",)),
        name="fft1",
    )(xr.reshape(FN, FN, FFT_WIDTH), xi.reshape(FN, FN, FFT_WIDTH), st1)


def _fft2_kernel(yr_ref, yi_ref, f_ref, z_ref, out_ref):
    for b in range(FB):
        ys = jnp.concatenate([yr_ref[b * FN:(b + 1) * FN, :], yi_ref[b * FN:(b + 1) * FN, :]], axis=0)
        z = jnp.dot(f_ref[b], ys, preferred_element_type=F32)
        base, rows = _slab_rows(b, 0, FSL)
        for s in range(FSL):
            out_ref[base + s, rows, :] = z[:, s * LANES:(s + 1) * LANES]
    z_ref[...] = _slabs_to_block(out_ref, 0, FSL).astype(BF16)


def _fft2(yr, yi, st2):
    rows = pl.BlockSpec((FB * FN, FFT_WIDTH), lambda i: (i, 0))
    z = pl.pallas_call(
        _fft2_kernel,
        grid=(FN // FB,),
        in_specs=[rows, rows, pl.BlockSpec((FB, FN, 2 * FN), lambda i: (i, 0, 0))],
        out_specs=pl.BlockSpec((FN, FB, FFT_WIDTH), lambda i: (0, i, 0)),
        out_shape=jax.ShapeDtypeStruct((FN, FN, FFT_WIDTH), BF16),
        scratch_shapes=[pltpu.VMEM((FBH * FSL, FN * SUBLANES, LANES), F32)],
        compiler_params=_cparams(("parallel",)),
        name="fft2",
    )(yr.reshape(N_TOK, FFT_WIDTH), yi.reshape(N_TOK, FFT_WIDTH), st2)
    return z.reshape(N_TOK, FFT_WIDTH)


def _gelu_tanh(x):
    return 0.5 * x * (1.0 + jnp.tanh(math.sqrt(2.0 / math.pi) * (x + 0.044715 * (x * x * x))))


def _mix_kernel(h_ref, uh_ref, wgs_ref, wgf_ref, bg_ref,
                yt_ref, zr_ref, wglu_ref, bglu_ref, wbs_ref, wbf_ref, bbf_ref, wo_ref, bo_ref,
                g1_ref, l1g_ref, l1b_ref, m2_ref, s2_ref, wr_ref, br_ref, tri_ref, etri_ref,
                h1_ref, u2_ref, pos_ref, gate_ref, cnt_ref, scr_ref):
    def front(r0, nr):
        rows = slice(r0, r0 + nr)
        h = h_ref[rows, :]
        u_half = uh_ref[rows, :]

        c0, nc = r0 // CH, nr // CH
        for t in range(CH):
            for j in range(NJ):
                scr_ref[j, pl.ds(r0 + t, nc, stride=CH), :] = (
                    yt_ref[t, c0:c0 + nc, j * LANES:(j + 1) * LANES].astype(F32))
        ys = jnp.concatenate([scr_ref[j, rows, :] for j in range(NJ)], axis=-1)
        z = jnp.dot(_gelu_tanh(ys).astype(BF16), wglu_ref[...], preferred_element_type=F32) + bglu_ref[...]
        glu = (z[:, :S5_WIDTH] * _sigmoid(z[:, S5_WIDTH:])).astype(BF16)
        t_s5 = jnp.tanh(jnp.dot(u_half, wgs_ref[...], preferred_element_type=F32) + 0.5 * bg_ref[:, :D])
        t_fft = jnp.tanh(jnp.dot(u_half, wgf_ref[...], preferred_element_type=F32) + 0.5 * bg_ref[:, D:])
        y_s5 = jnp.dot(glu, wbs_ref[...], preferred_element_type=F32)
        y_fft = jnp.dot(zr_ref[rows, :], wbf_ref[...], preferred_element_type=F32) + bbf_ref[...]
        mixed2 = ((t_s5 + 1.0) * y_s5 + (t_fft + 1.0) * y_fft).astype(BF16)
        y = 0.5 * jnp.dot(mixed2, wo_ref[...], preferred_element_type=F32) + bo_ref[...]
        h1 = _layer_norm(ALPHA * h + g1_ref[...] * y, l1g_ref[...], l1b_ref[...])
        h1_ref[rows, :] = h1
        u2 = h1 * m2_ref[...] + s2_ref[...]
        u2_ref[rows, :] = u2.astype(BF16)
        u_hi = u2.astype(BF16)
        u_lo = (u2 - u_hi.astype(F32)).astype(BF16)

        def nt(a, b):
            return lax.dot_general(a, b, (((1,), (1,)), ((), ())), preferred_element_type=F32)
        return nt(wr_ref[0], u_hi) + nt(wr_ref[0], u_lo) + nt(wr_ref[1], u_hi)

    logits = front(0, TM) + br_ref[:, 0:1]
    eidx = lax.broadcasted_iota(jnp.int32, (N_EXPERTS, TM), 0)
    vals, hots = [], []
    cur = logits
    for _k in range(TOP_K):
        m = jnp.max(cur, axis=0, keepdims=True)
        sel = jnp.min(jnp.where(cur == m, eidx, N_EXPERTS), axis=0, keepdims=True)
        hot = eidx == sel
        cur = jnp.where(hot, -jnp.inf, cur)
        vals.append(m)
        hots.append(hot)
    exps = [jnp.exp(v - vals[0]) for v in vals]
    den = exps[0] + exps[1] + exps[2] + exps[3]
    gate4 = jnp.concatenate([e / den for e in exps], axis=0)

    hot_sum = (hots[0] | hots[1] | hots[2] | hots[3]).astype(F32)
    before = jnp.dot(hot_sum.astype(BF16), tri_ref[...], preferred_element_type=F32)
    cnt = jnp.broadcast_to(jnp.sum(hot_sum, axis=1, keepdims=True), (N_EXPERTS, LANES))
    cnt8 = jnp.floor((cnt + (SEG_ALIGN - 1)) * (1.0 / SEG_ALIGN)) * SEG_ALIGN
    seg0 = jnp.dot(etri_ref[...], cnt8.astype(BF16), preferred_element_type=F32)
    tot = seg0[:, 0:1] + before
    pos4 = jnp.concatenate(
        [jnp.sum(jnp.where(hk, tot, 0.0), axis=0, keepdims=True) for hk in hots], axis=0)
    pos_ref[...] = pos4.astype(jnp.int32)
    cnt_ref[0] = cnt

    gate_ref[...] = gate4


def _mix(h, uh, w_in, bg, y_t, zr, wglu, bglu, wbs, wbf, bbf, wo, bo,
         g1, l1g, l1b, m2, s2, wr_t, br, tri, etri):
    gate_cols = (S5_WIDTH + FFT_WIDTH) // D
    vec = pl.BlockSpec((1, D), lambda i: (0, 0))

    def full(a):
        return pl.BlockSpec(a.shape, lambda i: (0,) * a.ndim)
    return pl.pallas_call(
        _mix_kernel,
        grid=(N_TOK // TM,),
        in_specs=[pl.BlockSpec((TM, D), lambda i: (i, 0)),
                  pl.BlockSpec((TM, D), lambda i: (i, 0)),
                  pl.BlockSpec((D, D), lambda i: (0, gate_cols)),
                  pl.BlockSpec((D, D), lambda i: (0, gate_cols + 1)), full(bg),
                  pl.BlockSpec((CH, TM // CH, S5_WIDTH), lambda i: (0, i, 0)),
                  pl.BlockSpec((TM, FFT_WIDTH), lambda i: (i, 0)),
                  full(wglu), full(bglu), full(wbs), full(wbf), full(bbf), full(wo), full(bo),
                  vec, vec, vec, vec, vec, full(wr_t), full(br), full(tri), full(etri)],
        out_specs=(pl.BlockSpec((TM, D), lambda i: (i, 0)),
                   pl.BlockSpec((TM, D), lambda i: (i, 0)),
                   pl.BlockSpec((TOP_K, TM), lambda i: (0, i)),
                   pl.BlockSpec((TOP_K, TM), lambda i: (0, i)),
                   pl.BlockSpec((1, N_EXPERTS, LANES), lambda i: (i, 0, 0))),
        out_shape=(jax.ShapeDtypeStruct((N_TOK, D), F32),
                   jax.ShapeDtypeStruct((N_TOK, D), BF16),
                   jax.ShapeDtypeStruct((TOP_K, N_TOK), jnp.int32),
                   jax.ShapeDtypeStruct((TOP_K, N_TOK), F32),
                   jax.ShapeDtypeStruct((N_TILES, N_EXPERTS, LANES), F32)),
        scratch_shapes=[pltpu.VMEM((NJ, TM, LANES), F32)],
        compiler_params=_cparams(("parallel",)),
        name="mix",
    )(h, uh, w_in, w_in, bg, y_t, zr, wglu, bglu, wbs, wbf, bbf, wo, bo,
      g1, l1g, l1b, m2, s2, wr_t, br, tri, etri)


def _on_parity(i, fn):
    @pl.when(i % 2 == 0)
    def _():
        fn(0)

    @pl.when(i % 2 == 1)
    def _():
        fn(1)


def _dispatch_kernel(dprev_ref, dest_ref, pos_ref, u_ref, zeroed_ref, buf_ref, sorted_ref, sems):
    del zeroed_ref
    i = pl.program_id(0)

    @pl.when(i == 0)
    def _():
        sorted_ref[...] = jnp.zeros_like(sorted_ref)

    def chunk_copy(slot, table_ref, j):
        dst = pl.multiple_of(table_ref[0, 0, j], SEG_ALIGN)
        return pltpu.make_async_copy(sorted_ref.at[slot, pl.ds(j * SEG_ALIGN, SEG_ALIGN)],
                                     buf_ref.at[pl.ds(dst, SEG_ALIGN)], sems.at[slot])

    def drain(slot):
        pltpu.make_async_copy(sorted_ref.at[slot], buf_ref.at[pl.ds(0, CAP)], sems.at[slot]).wait()

    def run(slot):
        pos = pos_ref[...]
        u = u_ref[...]
        n_rb = CAP // SORT_BLOCK
        per_rb = NCHK // (n_rb // 2)
        for rb in range(n_rb):
            for j in range(rb * per_rb, min((rb + 1) * per_rb, NCHK)):
                chunk_copy(1 - slot, dprev_ref, j).start()
            rows = lax.broadcasted_iota(jnp.int32, (SORT_BLOCK, TM), 0) + rb * SORT_BLOCK
            hit = rows == pos[0:1]
            for k in range(1, TOP_K):
                hit = hit | (rows == pos[k:k + 1])
            onehot = jnp.where(hit, 1.0, 0.0).astype(BF16)
            sorted_ref[slot, rb * SORT_BLOCK:(rb + 1) * SORT_BLOCK, :] = jnp.dot(
                onehot, u, preferred_element_type=F32).astype(BF16)
        drain(1 - slot)

        @pl.when(i == N_TILES - 1)
        def _():
            def issue(j, c):
                chunk_copy(slot, dest_ref, j).start()
                return c
            lax.fori_loop(0, NCHK, issue, 0)
            drain(slot)
    _on_parity(i, run)


def _dispatch(chunk_table, pos_t, u2, zeroed):
    return pl.pallas_call(
        _dispatch_kernel,
        grid=(N_TILES,),
        in_specs=[pl.BlockSpec((1, 1, NCHK), lambda i: (i, 0, 0), memory_space=pltpu.SMEM),
                  pl.BlockSpec((1, 1, NCHK), lambda i: (i + 1, 0, 0), memory_space=pltpu.SMEM),
                  pl.BlockSpec((TOP_K, TM), lambda i: (0, i)),
                  pl.BlockSpec((TM, D), lambda i: (i, 0)),
                  pl.BlockSpec(memory_space=pl.ANY)],
        out_specs=pl.BlockSpec(memory_space=pl.ANY),
        out_shape=jax.ShapeDtypeStruct((ROWS_ALL, D), BF16),
        scratch_shapes=[pltpu.VMEM((2, CAP, D), BF16),
                        pltpu.SemaphoreType.DMA((2,))],
        input_output_aliases={4: 0},
        compiler_params=_cparams(("arbitrary",)),
        name="dispatch",
    )(chunk_table, chunk_table, pos_t, u2, zeroed)


def _ffn_kernel(be_ref, nu_ref, run_ref, nxt_ref, valid_ref, x_ref, wu_hbm, bu_ref, wd_hbm, bd_ref,
                y_ref, wu_ref, wd_ref, wub_ref, wdb_ref, sems):
    i = pl.program_id(0)
    used = i < nu_ref[0]

    def weight_copies(e, slot):
        return (pltpu.make_async_copy(wu_hbm.at[e], wu_ref.at[slot], sems.at[slot]),
                pltpu.make_async_copy(wd_hbm.at[e], wd_ref.at[slot], sems.at[slot]))

    @pl.when(used)
    def _():
        run = run_ref[i]

        @pl.when(run >= 0)
        def _():
            def open_run(slot):
                @pl.when(run == 0)
                def _():
                    for cp in weight_copies(be_ref[i], slot):
                        cp.start()

                @pl.when(nxt_ref[i] >= 0)
                def _():
                    for cp in weight_copies(nxt_ref[i], 1 - slot):
                        cp.start()
                for cp in weight_copies(be_ref[i], slot):
                    cp.wait()
                wub_ref[...] = wu_ref[slot].astype(BF16)
                wdb_ref[...] = wd_ref[slot].astype(BF16)
            _on_parity(run, open_run)

        def expert_rows(r0, nr):
            rows = slice(r0, r0 + nr)
            e = be_ref[i]
            h = (jnp.dot(x_ref[rows, :], wub_ref[...], preferred_element_type=F32)
                 + bu_ref[pl.ds(e, 1), :])
            h_glu = jnp.minimum(h[:, :D], SWIGLU_LIMIT)
            h_lin = jnp.clip(h[:, D:], -SWIGLU_LIMIT, SWIGLU_LIMIT)
            act = (h_glu * _sigmoid(SWIGLU_ALPHA * h_glu) * (h_lin + 1.0)).astype(BF16)
            y_ref[rows, :] = (jnp.dot(act, wdb_ref[...], preferred_element_type=F32)
                              + bd_ref[pl.ds(e, 1), :]).astype(BF16)

        valid = valid_ref[i]

        @pl.when(valid == BM)
        def _():
            expert_rows(0, BM)

        @pl.when(valid < BM)
        def _():
            for h0 in range(0, BM, FFN_HALF):
                @pl.when(valid >= h0 + FFN_HALF)
                def _(h0=h0):
                    expert_rows(h0, FFN_HALF)

                @pl.when(valid < h0 + FFN_HALF)
                def _(h0=h0):
                    for r0 in range(h0, h0 + FFN_HALF, FFN_TAIL):
                        @pl.when(r0 < valid)
                        def _(r0=r0):
                            expert_rows(r0, FFN_TAIL)

                        @pl.when(r0 >= valid)
                        def _(r0=r0):
                            y_ref[r0:r0 + FFN_TAIL, :] = jnp.zeros((FFN_TAIL, D), BF16)


def _ffn(block_expert, n_used, run_id, next_expert, valid, buf, w_up, b_up, w_down, b_down):
    def blk(i, be, nu, *_):
        return jnp.minimum(i, nu[0] - 1)
    return pl.pallas_call(
        _ffn_kernel,
        grid_spec=pltpu.PrefetchScalarGridSpec(
            num_scalar_prefetch=5,
            grid=(N_BLOCKS_ALL,),
            in_specs=[pl.BlockSpec((BM, D), lambda i, *s: (blk(i, *s), 0)),
                      pl.BlockSpec(memory_space=pl.ANY),
                      pl.BlockSpec((N_EXPERTS, 2 * D), lambda i, *s: (0, 0)),
                      pl.BlockSpec(memory_space=pl.ANY),
                      pl.BlockSpec((N_EXPERTS, D), lambda i, *s: (0, 0))],
            out_specs=pl.BlockSpec((BM, D), lambda i, *s: (blk(i, *s), 0)),
            scratch_shapes=[pltpu.VMEM((2, D, 2 * D), F32),
                            pltpu.VMEM((2, D, D), F32),
                            pltpu.VMEM((D, 2 * D), BF16),
                            pltpu.VMEM((D, D), BF16),
                            pltpu.SemaphoreType.DMA((2,))]),
        out_shape=jax.ShapeDtypeStruct((ROWS_ALL, D), BF16),
        input_output_aliases={5: 0},
        compiler_params=_cparams(("arbitrary",)),
        name="ffn",
    )(block_expert, n_used, run_id, next_expert, valid, buf, w_up, b_up, w_down, b_down)


def _combine_kernel(dest_ref, dnext_ref, y_ref, h1_ref, pos_ref, gate_ref, g2_ref, lg_ref, lb_ref,
                    o_ref, sorted_ref, sems):
    i = pl.program_id(0)

    def chunk_copy(slot, table_ref, j):
        src = pl.multiple_of(table_ref[0, 0, j], SEG_ALIGN)
        return pltpu.make_async_copy(y_ref.at[pl.ds(src, SEG_ALIGN)],
                                     sorted_ref.at[slot, pl.ds(j * SEG_ALIGN, SEG_ALIGN)],
                                     sems.at[slot])

    def drain(slot):
        pltpu.make_async_copy(y_ref.at[pl.ds(0, CAP)], sorted_ref.at[slot], sems.at[slot]).wait()

    @pl.when(i == 0)
    def _():
        def issue(j, c):
            chunk_copy(0, dest_ref, j).start()
            return c
        lax.fori_loop(0, NCHK, issue, 0)

    def run(slot):
        drain(slot)

        pos = pos_ref[...]
        gate = gate_ref[...]
        m = jnp.zeros((TM, D), F32)
        n_cb = CAP // CAP_BLOCK
        per_cb = NCHK // (n_cb // 2)
        for cb in range(n_cb):
            for j in range(cb * per_cb, min((cb + 1) * per_cb, NCHK)):
                chunk_copy(1 - slot, dnext_ref, j).start()
            rws = lax.broadcasted_iota(jnp.int32, (CAP_BLOCK, TM), 0) + cb * CAP_BLOCK
            g = jnp.where(rws == pos[0:1], gate[0:1], 0.0)
            for k in range(1, TOP_K):
                g = g + jnp.where(rws == pos[k:k + 1], gate[k:k + 1], 0.0)
            rows = sorted_ref[slot, cb * CAP_BLOCK:(cb + 1) * CAP_BLOCK, :]
            m = m + lax.dot_general(g.astype(BF16), rows, (((0,), (0,)), ((), ())),
                                    preferred_element_type=F32)
        o_ref[...] = _layer_norm(ALPHA * h1_ref[...] + g2_ref[...] * m, lg_ref[...], lb_ref[...])

        @pl.when(i == N_TILES - 1)
        def _():
            drain(1 - slot)
    _on_parity(i, run)


def _combine(chunk_table, y_buf, h1, pos_t, gate_t, g2, lg, lb):
    vec = pl.BlockSpec((1, D), lambda i: (0, 0))
    return pl.pallas_call(
        _combine_kernel,
        grid_spec=pltpu.PrefetchScalarGridSpec(
            num_scalar_prefetch=0,
            grid=(N_TILES,),
            in_specs=[pl.BlockSpec((1, 1, NCHK), lambda i: (i + 1, 0, 0), memory_space=pltpu.SMEM),
                      pl.BlockSpec((1, 1, NCHK), lambda i: (i + 2, 0, 0), memory_space=pltpu.SMEM),
                      pl.BlockSpec(memory_space=pl.ANY),
                      pl.BlockSpec((TM, D), lambda i: (i, 0)),
                      pl.BlockSpec((TOP_K, TM), lambda i: (0, i)),
                      pl.BlockSpec((TOP_K, TM), lambda i: (0, i)),
                      vec, vec, vec],
            out_specs=pl.BlockSpec((TM, D), lambda i: (i, 0)),
            scratch_shapes=[pltpu.VMEM((2, CAP, D), BF16),
                            pltpu.SemaphoreType.DMA((2,))]),
        out_shape=jax.ShapeDtypeStruct((N_TOK, D), F32),
        compiler_params=_cparams(("arbitrary",)),
        name="combine",
    )(chunk_table, chunk_table, y_buf, h1, pos_t, gate_t, g2, lg, lb)


def _sincos_tables():
    q = D // 4
    omega = 1.0 / (10000.0 ** (np.arange(q) / q))

    def emb(n):
        ang = np.arange(n)[:, None] * omega[None, :]
        return jnp.asarray(np.concatenate([np.sin(ang), np.cos(ang)], axis=-1), F32)
    return emb(N_TOK // GRID_W), emb(GRID_W)


def kernel(x, c, ctx, c_ctx, ln_in_g, ln_in_b, w_ada, b_ada, w_in, b_in, s5_lambda_re, s5_lambda_im, s5_log_dt, s5_b_re, s5_b_im, s5_c_re, s5_c_im, s5_d, w_glu, b_glu, w_br_s5, w_br_fft, b_br_fft, w_out, b_out, ln1_g, ln1_b, w_router, b_router, w_up, b_up, w_down, b_down, ln2_g, ln2_b):
    assert x.shape == (1, N_TOK, D) and ctx.shape == (1, N_CTX, D) and w_ada.shape[0] == 1
    row = lambda v: v.reshape(1, -1).astype(F32)

    cc = jnp.concatenate([c.reshape(1, D), c_ctx.reshape(1, D), jnp.zeros((SUBLANES - 2, D), F32)], axis=0)
    ada = _ada(cc, w_ada[0], row(b_ada[0]))
    sh1, sc1, g1, sh2, sc2, g2 = (ada[0:1, k * D:(k + 1) * D] for k in range(6))
    sh1c, sc1c = ada[1:2, 0:D], ada[1:2, D:2 * D]

    emb_r, emb_c = _sincos_tables()
    st1, st2, fc = _dft_tables()
    lg, lb = row(ln_in_g), row(ln_in_b)

    w_in_bf = w_in[0].astype(BF16)
    b_s5 = row(b_in[0][:S5_WIDTH])
    b_fft8 = jnp.concatenate([row(b_in[0][S5_WIDTH:S5_WIDTH + FFT_WIDTH]),
                              jnp.zeros((SUBLANES - 1, FFT_WIDTH), F32)], axis=0)
    b_g = row(b_in[0][S5_WIDTH + FFT_WIDTH:])
    w_fc, b_fc = _fft_weights(w_in[0], b_fft8, fc)
    bcat = jnp.concatenate([b_s5, b_fc[0:1]], axis=1)

    x2 = x[0]
    p_t, xr, xi, h0, uh0 = _proj(x2, emb_r, emb_c, lg, lb, 1.0 + sc1, sh1, w_in_bf, w_fc, bcat)
    pc_t = _ctx_proj(ctx[0], lg, lb, 1.0 + sc1c, sh1c, w_in_bf[:, :S5_WIDTH], b_s5)

    b_c, a_q, a_p, trans, ctx_w = _s5_tables(
        s5_lambda_re[0], s5_lambda_im[0], s5_log_dt[0], s5_b_re[0], s5_b_im[0],
        s5_c_re[0], s5_c_im[0], s5_d[0])
    y_t, zeroed = _s5(p_t, pc_t, b_c, a_q, a_p, trans, ctx_w)

    yr, yi = _fft1(xr, xi, st1)
    zr = _fft2(yr, yi, st2)

    tri = jnp.asarray(np.arange(TM)[:, None] < np.arange(TM)[None, :], BF16)
    br = jnp.broadcast_to(b_router[0].reshape(N_EXPERTS, 1), (N_EXPERTS, LANES))
    etri = jnp.asarray(np.arange(N_EXPERTS)[:, None] > np.arange(N_EXPERTS)[None, :], BF16)
    wr_t = jnp.transpose(w_router[0])
    wr_hi = wr_t.astype(BF16)
    wr_split = jnp.stack([wr_hi, (wr_t - wr_hi.astype(F32)).astype(BF16)], axis=0)
    h1, u2, pos_t, gate_t, counts = _mix(
        h0, uh0, w_in_bf, b_g, y_t, zr,
        w_glu[0].astype(BF16), row(b_glu[0]), w_br_s5[0].astype(BF16), w_br_fft[0].astype(BF16),
        row(b_br_fft[0]), w_out[0].astype(BF16), row(b_out[0]), g1, row(ln1_g[0]), row(ln1_b[0]),
        1.0 + sc2, sh2, wr_split, br, tri, etri)

    cnt = counts[:, :, 0].astype(jnp.int32)
    seg = (cnt + SEG_ALIGN - 1) // SEG_ALIGN * SEG_ALIGN
    seg_end = jnp.cumsum(seg, axis=1)
    seg_start = seg_end - seg
    padded = (jnp.sum(seg, axis=0) + BM - 1) // BM * BM
    pad_ends = jnp.cumsum(padded)
    seg_dest = (pad_ends - padded)[None, :] + jnp.cumsum(seg, axis=0) - seg
    chunk_row = jnp.arange(NCHK, dtype=jnp.int32) * SEG_ALIGN
    chunk_exp = jnp.minimum(jnp.sum(chunk_row[None, :, None] >= seg_end[:, None, :], axis=-1),
                            N_EXPERTS - 1)
    own = chunk_exp[:, :, None] == jnp.arange(N_EXPERTS, dtype=jnp.int32)[None, None, :]
    chunk_dest = (jnp.sum(jnp.where(own, (seg_dest - seg_start)[:, None, :], 0), axis=-1)
                  + chunk_row[None, :]).astype(jnp.int32).reshape(N_TILES, 1, NCHK)
    nchk = (seg_end[:, -1] // SEG_ALIGN).astype(jnp.int32)
    block_start = jnp.arange(N_BLOCKS_ALL, dtype=jnp.int32) * BM
    block_expert = jnp.minimum(jnp.sum(block_start[:, None] >= pad_ends[None, :], axis=1),
                               N_EXPERTS - 1).astype(jnp.int32)
    n_used = (pad_ends[-1:] // BM).astype(jnp.int32)
    opens = (block_start < pad_ends[-1]) & (
        block_expert != jnp.concatenate([jnp.full((1,), -1, jnp.int32), block_expert[:-1]]))
    run_id = jnp.where(opens, jnp.cumsum(opens.astype(jnp.int32)) - 1, -1).astype(jnp.int32)
    experts = jnp.arange(N_EXPERTS, dtype=jnp.int32)
    later = (experts[None, :] > block_expert[:, None]) & (padded[None, :] > 0)
    next_expert = jnp.min(jnp.where(later, experts[None, :], N_EXPERTS), axis=1)
    next_expert = jnp.where(next_expert < N_EXPERTS, next_expert, -1).astype(jnp.int32)

    spare = (ROWS + chunk_row)[None, None, :]
    chunk_table = jnp.concatenate(
        [spare,
         jnp.where(chunk_row[None, None, :] < (nchk * SEG_ALIGN)[:, None, None], chunk_dest, spare),
         spare], axis=0).astype(jnp.int32)
    fill_ends = (pad_ends - padded + jnp.sum(seg, axis=0)).astype(jnp.int32)
    buf = _dispatch(chunk_table, pos_t, u2, zeroed)
    mine = block_expert[:, None] == experts[None, :]
    filled = jnp.sum(jnp.where(mine, fill_ends[None, :], 0), axis=1)
    valid = jnp.clip(filled - block_start, 0, BM).astype(jnp.int32)
    y_buf = _ffn(block_expert, n_used, run_id, next_expert, valid, buf, w_up[0],
                 b_up[0], w_down[0], b_down[0])
    out = _combine(chunk_table, y_buf, h1, pos_t, gate_t, g2, row(ln2_g[0]), row(ln2_b[0]))
    return out.reshape(1, N_TOK, D)
```
